```python
import math
import jax, jax.numpy as jnp
from jax import lax
import numpy as np


D_MODEL = 1024
BATCH = 8
SEQ = 4096
DEPTH = 4

CHUNK = 64
Q_BLOCK = 128
DA_HEADS = D_MODEL // 256
DA_HEAD_DIM = 64
DA_WIDTH = DA_HEADS * 2 * DA_HEAD_DIM
ML_HEADS = D_MODEL // 256
ML_HEAD_DIM = 128
ML_WIDTH = ML_HEADS * ML_HEAD_DIM
ML_CONV = 4
FX_HEADS = D_MODEL // 64
FX_HEAD_DIM = 64
FX_WIDTH = FX_HEADS * FX_HEAD_DIM
FFN_HIDDEN = ((8 * D_MODEL + 3 * 256 - 1) // (3 * 256)) * 256
AB_IN = 3 * DA_WIDTH + 4 * ML_WIDTH + 2 * ML_HEADS
AB_OUT = DA_WIDTH + ML_WIDTH
FX_IN = 3 * FX_WIDTH + FX_HEADS
N_EVEN = (DEPTH + 1) // 2
N_ODD = DEPTH // 2
EPS = 1e-6

kernel_name = "hybrid_diffattn_mlstm_fox_block"


def rms_norm(x, g):
    xf = x.astype(jnp.float32)
    y = xf * lax.rsqrt(jnp.mean(xf * xf, axis=-1, keepdims=True) + EPS)
    return (y * g.astype(jnp.float32)).astype(x.dtype)


def modulate(h, shift, scale):
    return h * (1.0 + scale[:, None, :]) + shift[:, None, :]


def causal_conv(x, w, b):
    K = w.shape[0]
    S = x.shape[1]
    xp = jnp.pad(x, ((0, 0), (K - 1, 0), (0, 0)))
    return sum(xp[:, j:j + S] * w[j] for j in range(K)) + b


def alibi_slopes(n):
    return 2.0 ** (-8.0 * jnp.arange(1, n + 1, dtype=jnp.float32) / n)


def diff_attention(q, k, v, q_g, k_g, lam_p, subln_g, layer_idx):
    B, S, _ = q.shape
    q = rms_norm(q.reshape(B, S, DA_HEADS, 2, DA_HEAD_DIM).transpose(0, 2, 3, 1, 4), q_g)
    k = rms_norm(k.reshape(B, S, DA_HEADS, 2, DA_HEAD_DIM).transpose(0, 2, 3, 1, 4), k_g)
    v = v.reshape(B, S, DA_HEADS, 2 * DA_HEAD_DIM).transpose(0, 2, 1, 3)
    lambda_init = 0.8 - 0.6 * math.exp(-0.3 * layer_idx)
    lp = lam_p.astype(jnp.float32)
    lam = jnp.exp(jnp.sum(lp[0] * lp[1])) - jnp.exp(jnp.sum(lp[2] * lp[3])) + lambda_init
    slopes = alibi_slopes(DA_HEADS)[None, :, None, None, None]
    k_pos = jnp.arange(S)
    scale = DA_HEAD_DIM ** -0.5

    def block(i):
        start = i * Q_BLOCK
        qb = lax.dynamic_slice_in_dim(q, start, Q_BLOCK, axis=3)
        q_pos = start + jnp.arange(Q_BLOCK)
        logits = jnp.einsum('bhmqd,bhmkd->bhmqk', qb, k, preferred_element_type=jnp.float32) * scale
        dist = jnp.abs(q_pos[:, None] - k_pos[None, :]).astype(jnp.float32)
        allowed = (k_pos[None, :] // CHUNK) <= (q_pos[:, None] // CHUNK)
        logits = jnp.where(allowed, logits - slopes * dist, -jnp.inf)
        p = jax.nn.softmax(logits, axis=-1)
        p_diff = p[:, :, 0] - lam * p[:, :, 1]
        return jnp.einsum('bhqk,bhkd->bhqd', p_diff.astype(v.dtype), v)

    out = lax.map(block, jnp.arange(S // Q_BLOCK))
    out = out.transpose(1, 2, 0, 3, 4).reshape(B, DA_HEADS, S, 2 * DA_HEAD_DIM)
    out = rms_norm(out, subln_g) * (1.0 - lambda_init)
    return out.transpose(0, 2, 1, 3).reshape(B, S, DA_WIDTH)


def mlstm(q, k, v, i_pre, f_pre):
    B, S, _ = q.shape
    H, d, L = ML_HEADS, ML_HEAD_DIM, CHUNK
    NC = S // L
    f32 = jnp.float32

    def heads(t):
        return t.astype(f32).reshape(B, NC, L, H, d).transpose(0, 3, 1, 2, 4)

    def gates(t):
        return t.astype(f32).reshape(B, NC, L, H).transpose(0, 3, 1, 2)

    q, k, v = heads(q), heads(k) * (d ** -0.5), heads(v)
    ig = gates(i_pre)
    a = jnp.cumsum(jax.nn.log_sigmoid(gates(f_pre)), axis=-1)
    a_last = a[..., -1]
    causal = jnp.tril(jnp.ones((L, L), dtype=bool))
    log_d = jnp.where(causal, a[..., :, None] - a[..., None, :] + ig[..., None, :], -jnp.inf)
    g = a_last[..., None] - a + ig
    m_loc = jnp.max(g, axis=-1)
    w = jnp.exp(g - m_loc[..., None])
    dC = jnp.einsum('bhcl,bhcld,bhcle->bhcde', w, v, k)
    dn = jnp.einsum('bhcl,bhcle->bhce', w, k)

    def step(carry, inp):
        C, n, m = carry
        dC_c, dn_c, m_loc_c, a_last_c = inp
        m_new = jnp.maximum(a_last_c + m, m_loc_c)
        decay = jnp.exp(a_last_c + m - m_new)
        s_loc = jnp.exp(m_loc_c - m_new)
        C_new = decay[..., None, None] * C + s_loc[..., None, None] * dC_c
        n_new = decay[..., None] * n + s_loc[..., None] * dn_c
        return (C_new, n_new, m_new), (C, n, m)

    init = (jnp.zeros((B, H, d, d), f32), jnp.zeros((B, H, d), f32), jnp.zeros((B, H), f32))
    xs = (jnp.moveaxis(dC, 2, 0), jnp.moveaxis(dn, 2, 0), jnp.moveaxis(m_loc, 2, 0), jnp.moveaxis(a_last, 2, 0))
    _, (C_prev, n_prev, m_prev) = lax.scan(step, init, xs)
    C_prev = jnp.moveaxis(C_prev, 0, 2)
    n_prev = jnp.moveaxis(n_prev, 0, 2)
    m_prev = jnp.moveaxis(m_prev, 0, 2)

    m_t = jnp.maximum(a + m_prev[..., None], jnp.max(log_d, axis=-1))
    inter_w = jnp.exp(a + m_prev[..., None] - m_t)
    qk = jnp.einsum('bhcld,bhcsd->bhcls', q, k) * jnp.exp(log_d - m_t[..., None])
    num = inter_w[..., None] * jnp.einsum('bhcde,bhcle->bhcld', C_prev, q) + jnp.einsum('bhcls,bhcsd->bhcld', qk, v)
    den = inter_w * jnp.einsum('bhce,bhcle->bhcl', n_prev, q) + jnp.sum(qk, axis=-1)
    h = num / jnp.maximum(jnp.abs(den), jnp.exp(-m_t))[..., None]
    return h.transpose(0, 2, 3, 1, 4).reshape(B, S, H * d)


def forgetting_attention(q, k, v, f_pre, q_g, k_g):
    B, S, _ = q.shape

    def heads(t):
        return t.reshape(B, S, FX_HEADS, FX_HEAD_DIM).transpose(0, 2, 1, 3)

    q = rms_norm(heads(q), q_g)
    k = rms_norm(heads(k), k_g)
    v = heads(v)
    cum_logf = jnp.cumsum(jax.nn.log_sigmoid(f_pre.astype(jnp.float32)), axis=1).transpose(0, 2, 1)
    k_pos = jnp.arange(S)
    scale = FX_HEAD_DIM ** -0.5

    def block(i):
        start = i * Q_BLOCK
        qb = lax.dynamic_slice_in_dim(q, start, Q_BLOCK, axis=2)
        fq = lax.dynamic_slice_in_dim(cum_logf, start, Q_BLOCK, axis=2)
        q_pos = start + jnp.arange(Q_BLOCK)
        logits = jnp.einsum('bhqd,bhkd->bhqk', qb, k, preferred_element_type=jnp.float32) * scale
        logits = logits + fq[..., :, None] - cum_logf[..., None, :]
        logits = jnp.where(k_pos[None, :] <= q_pos[:, None], logits, -jnp.inf)
        p = jax.nn.softmax(logits, axis=-1)
        return jnp.einsum('bhqk,bhkd->bhqd', p.astype(v.dtype), v)

    out = lax.map(block, jnp.arange(S // Q_BLOCK))
    return out.transpose(1, 0, 3, 2, 4).reshape(B, S, FX_WIDTH)


def setup_inputs(seed: int = 0) -> dict:
    key = jax.random.key(seed)
    ks = jax.random.split(key, 24)
    f32 = jnp.float32

    def nrm(k, shape, s):
        return jax.random.normal(k, shape, f32) * s

    return {
        'x': nrm(ks[0], (BATCH, SEQ, D_MODEL), 1.0),
        'c': nrm(ks[1], (BATCH, D_MODEL), 1.0),
        'ada_w': nrm(ks[2], (DEPTH, D_MODEL, 6 * D_MODEL), 0.5 * D_MODEL ** -0.5),
        'ada_b': nrm(ks[3], (DEPTH, 6 * D_MODEL), 0.02),
        'norm_mix_g': 1.0 + nrm(ks[4], (DEPTH, D_MODEL), 0.02),
        'norm_ffn_g': 1.0 + nrm(ks[5], (DEPTH, D_MODEL), 0.02),
        'ab_w_in': nrm(ks[6], (N_EVEN, D_MODEL, AB_IN), D_MODEL ** -0.5),
        'ml_b_i': nrm(ks[7], (N_EVEN, ML_HEADS), 0.1),
        'ml_b_f': jnp.linspace(3.0, 6.0, ML_HEADS, dtype=f32)[None, :] + nrm(ks[8], (N_EVEN, ML_HEADS), 0.1),
        'ml_conv_w': nrm(ks[9], (N_EVEN, ML_CONV, 2 * ML_WIDTH), ML_CONV ** -0.5),
        'ml_conv_b': nrm(ks[10], (N_EVEN, 2 * ML_WIDTH), 0.02),
        'da_q_g': 1.0 + nrm(ks[11], (N_EVEN, DA_HEAD_DIM), 0.02),
        'da_k_g': 1.0 + nrm(ks[12], (N_EVEN, DA_HEAD_DIM), 0.02),
        'da_lambda': nrm(ks[13], (N_EVEN, 4, DA_HEAD_DIM), 0.1),
        'da_subln_g': 1.0 + nrm(ks[14], (N_EVEN, 2 * DA_HEAD_DIM), 0.02),
        'ab_w_out': nrm(ks[15], (N_EVEN, AB_OUT, D_MODEL), AB_OUT ** -0.5),
        'fx_w_in': nrm(ks[16], (N_ODD, D_MODEL, FX_IN), D_MODEL ** -0.5),
        'fx_b_f': jnp.linspace(1.0, 6.0, FX_HEADS, dtype=f32)[None, :] + nrm(ks[17], (N_ODD, FX_HEADS), 0.1),
        'fx_q_g': 1.0 + nrm(ks[18], (N_ODD, FX_HEAD_DIM), 0.02),
        'fx_k_g': 1.0 + nrm(ks[19], (N_ODD, FX_HEAD_DIM), 0.02),
        'fx_w_out': nrm(ks[20], (N_ODD, FX_WIDTH, D_MODEL), FX_WIDTH ** -0.5),
        'ffn_w1': nrm(ks[21], (DEPTH, D_MODEL, FFN_HIDDEN), D_MODEL ** -0.5),
        'ffn_w3': nrm(ks[22], (DEPTH, D_MODEL, FFN_HIDDEN), D_MODEL ** -0.5),
        'ffn_w2': nrm(ks[23], (DEPTH, FFN_HIDDEN, D_MODEL), FFN_HIDDEN ** -0.5),
    }


def reference(x, c, ada_w, ada_b, norm_mix_g, norm_ffn_g, ab_w_in, ml_b_i, ml_b_f, ml_conv_w, ml_conv_b,
              da_q_g, da_k_g, da_lambda, da_subln_g, ab_w_out, fx_w_in, fx_b_f, fx_q_g, fx_k_g, fx_w_out,
              ffn_w1, ffn_w3, ffn_w2):
    c_act = jax.nn.silu(c)
    ab_sizes = [DA_WIDTH] * 3 + [ML_WIDTH] * 4 + [ML_HEADS] * 2
    ab_split = np.cumsum(ab_sizes)[:-1].tolist()
    for l in range(DEPTH):
        mod = c_act @ ada_w[l] + ada_b[l]
        sh_m, sc_m, g_m, sh_f, sc_f, g_f = jnp.split(mod, 6, axis=-1)
        h = modulate(rms_norm(x, norm_mix_g[l]), sh_m, sc_m)
        j = l // 2
        if l % 2 == 0:
            z = h @ ab_w_in[j]
            da_q, da_k, da_v, ml_q, ml_k, ml_v, ml_o, ml_i, ml_f = jnp.split(z, ab_split, axis=-1)
            ml_qk = jax.nn.silu(causal_conv(jnp.concatenate([ml_q, ml_k], axis=-1), ml_conv_w[j], ml_conv_b[j]))
            ml_q, ml_k = jnp.split(ml_qk, 2, axis=-1)
            y_da = diff_attention(da_q, da_k, da_v, da_q_g[j], da_k_g[j], da_lambda[j], da_subln_g[j], l)
            y_ml = jax.nn.sigmoid(ml_o) * mlstm(ml_q, ml_k, ml_v, ml_i + ml_b_i[j], ml_f + ml_b_f[j])
            y = jnp.concatenate([y_da, y_ml.astype(h.dtype)], axis=-1) @ ab_w_out[j]
        else:
            z = h @ fx_w_in[j]
            fq, fk, fv, ff = jnp.split(z, [FX_WIDTH, 2 * FX_WIDTH, 3 * FX_WIDTH], axis=-1)
            y = forgetting_attention(fq, fk, fv, ff + fx_b_f[j], fx_q_g[j], fx_k_g[j]) @ fx_w_out[j]
        x = x + g_m[:, None, :] * y
        h = modulate(rms_norm(x, norm_ffn_g[l]), sh_f, sc_f)
        ffn = (jax.nn.silu(h @ ffn_w1[l]) * (h @ ffn_w3[l])) @ ffn_w2[l]
        x = x + g_f[:, None, :] * ffn
    return x
```

```python
import functools
import math

import jax
import jax.numpy as jnp
from jax import lax
from jax.experimental import pallas as pl
from jax.experimental.pallas import tpu as pltpu

F32 = jnp.float32
BF16 = jnp.bfloat16

EPS = 1e-6
LANES = 128
HEAD_DIM = 64
CHUNK = 64
NEG = -1e30

VMEM_LIMIT = 56 * 1024 * 1024

ROW_TILE = 512
ATTN_TILE = 256
ML_CHUNK = 256
GATE_CHUNK = 256
FFN_CHUNK = 256


def _params(*sem):
    return pltpu.CompilerParams(dimension_semantics=sem, vmem_limit_bytes=VMEM_LIMIT)


def _log_sigmoid(x):
    return jnp.minimum(x, 0.0) - jnp.log1p(jnp.exp(-jnp.abs(x)))


def _sigmoid(x):
    return 1.0 / (1.0 + jnp.exp(-x))


def _split3(x):
    x1 = x.astype(BF16)
    r1 = x - x1.astype(F32)
    x2 = r1.astype(BF16)
    x3 = (r1 - x2.astype(F32)).astype(BF16)
    return x1, x2, x3


def _dot(a, b):
    return jnp.dot(a, b, preferred_element_type=F32)


def _dot_nt(a, b):
    return lax.dot_general(a, b, (((1,), (1,)), ((), ())), preferred_element_type=F32)


def _dot_tn(a, b):
    return lax.dot_general(a, b, (((0,), (0,)), ((), ())), preferred_element_type=F32)


def _exact_lhs_dot(m_bf16, x):
    x1, x2, x3 = _split3(x)
    return _dot(m_bf16, x1) + _dot(m_bf16, x2) + _dot(m_bf16, x3)


def _exact_rhs_dot(x, m_bf16):
    x1, x2, x3 = _split3(x)
    return _dot(x1, m_bf16) + _dot(x2, m_bf16) + _dot(x3, m_bf16)


def _norm_modulate(x, g, sh, sc):
    ms = jnp.mean(x * x, axis=-1, keepdims=True)
    y = x * lax.rsqrt(ms + EPS) * g
    return y * (1.0 + sc) + sh


def _ada_kernel(c_ref, w_ref, b_ref, o_ref):
    c = c_ref[...]
    ca = c * _sigmoid(c)
    c1, c2, c3 = _split3(ca)
    w1, w2, w3 = _split3(w_ref[...])
    acc = _dot(c1, w1) + _dot(c1, w2) + _dot(c2, w1)
    acc = acc + _dot(c1, w3) + _dot(c2, w2) + _dot(c3, w1)
    o_ref[...] = acc + b_ref[...]


def _ada_call(c, ada_w, ada_b):
    depth, d, n = ada_w.shape
    b = c.shape[0]
    tn = 1536
    assert n % tn == 0
    return pl.pallas_call(
        _ada_kernel,
        grid=(depth, n // tn),
        in_specs=[
            pl.BlockSpec((b, d), lambda l, j: (0, 0)),
            pl.BlockSpec((None, d, tn), lambda l, j: (l, 0, j)),
            pl.BlockSpec((None, 1, tn), lambda l, j: (l, 0, j)),
        ],
        out_specs=pl.BlockSpec((None, b, tn), lambda l, j: (l, 0, j)),
        out_shape=jax.ShapeDtypeStruct((depth, b, n), F32),
        compiler_params=_params("arbitrary", "arbitrary"),
        name="ada_mod",
    )(c, ada_w, ada_b.reshape(depth, 1, n))


def _inproj_kernel(x_ref, sh_ref, sc_ref, g_ref, w_ref, wg_ref, gain_ref, z_ref, gate_ref,
                   *, n_norm, chunk):
    h = _norm_modulate(x_ref[...], g_ref[...], sh_ref[...], sc_ref[...]).astype(BF16)
    n = w_ref.shape[1]
    lane = lax.broadcasted_iota(jnp.int32, (1, LANES), 1)
    lo = lane < HEAD_DIM
    for c0 in range(0, n, chunk):
        zc = _dot(h, w_ref[:, c0:c0 + chunk])
        if c0 < n_norm:
            parts = []
            for s0 in range(0, chunk, LANES):
                zs = zc[:, s0:s0 + LANES]
                sq = zs * zs
                s_lo = jnp.sum(jnp.where(lo, sq, 0.0), axis=-1, keepdims=True)
                s_hi = jnp.sum(jnp.where(lo, 0.0, sq), axis=-1, keepdims=True)
                r_lo = lax.rsqrt(s_lo * (1.0 / HEAD_DIM) + EPS)
                r_hi = lax.rsqrt(s_hi * (1.0 / HEAD_DIM) + EPS)
                parts.append(zs * jnp.where(lo, r_lo, r_hi))
            zc = jnp.concatenate(parts, axis=-1) * gain_ref[:, c0:c0 + chunk]
        z_ref[:, c0:c0 + chunk] = zc.astype(BF16)
    gate_ref[...] = _dot(h, wg_ref[...])


def _inproj_call(x, shift, scale, g, w, wg, gain, n_norm, name):
    b, s, d = x.shape
    n = w.shape[1]
    tm = min(ROW_TILE, s)
    chunk = 512
    assert s % tm == 0 and n % chunk == 0 and n_norm % chunk == 0
    kern = functools.partial(_inproj_kernel, n_norm=n_norm, chunk=chunk)
    return pl.pallas_call(
        kern,
        grid=(b, s // tm),
        in_specs=[
            pl.BlockSpec((None, tm, d), lambda i, j: (i, j, 0)),
            pl.BlockSpec((None, 1, d), lambda i, j: (i, 0, 0)),
            pl.BlockSpec((None, 1, d), lambda i, j: (i, 0, 0)),
            pl.BlockSpec((1, d), lambda i, j: (0, 0)),
            pl.BlockSpec((d, n), lambda i, j: (0, 0)),
            pl.BlockSpec((d, LANES), lambda i, j: (0, 0)),
            pl.BlockSpec((1, n), lambda i, j: (0, 0)),
        ],
        out_specs=[
            pl.BlockSpec((None, tm, n), lambda i, j: (i, j, 0)),
            pl.BlockSpec((None, tm, LANES), lambda i, j: (i, j, 0)),
        ],
        out_shape=[
            jax.ShapeDtypeStruct((b, s, n), BF16),
            jax.ShapeDtypeStruct((b, s, LANES), F32),
        ],
        compiler_params=_params("arbitrary", "arbitrary"),
        name=name,
    )(x, shift, scale, g, w, wg, gain)


def _softmax_step(s, v, m_ref, l_ref, acc_ref):
    m_old = m_ref[...]
    m_new = jnp.maximum(m_old, jnp.max(s, axis=-1, keepdims=True))
    alpha = jnp.exp(m_old - m_new)
    p = jnp.exp(s - m_new)
    l_ref[...] = alpha * l_ref[...] + jnp.sum(p, axis=-1, keepdims=True)
    acc_ref[...] = alpha * acc_ref[...] + _dot(p.astype(BF16), v)
    m_ref[...] = m_new


def _init_stats(m_ref, l_ref, acc_ref):
    m_ref[...] = jnp.full(m_ref.shape, NEG, F32)
    l_ref[...] = jnp.zeros(l_ref.shape, F32)
    acc_ref[...] = jnp.zeros(acc_ref.shape, F32)


def _split_subheads(q):
    lane = lax.broadcasted_iota(jnp.int32, (1, LANES), 1)
    lo = lane < HEAD_DIM
    zero = jnp.zeros_like(q)
    return jnp.where(lo, q, zero), jnp.where(lo, zero, q), lo


def _da_kernel(slope_ref, q_ref, k_ref, v_ref, lam_ref, sg_ref, o_ref,
               ma_ref, la_ref, acca_ref, mb_ref, lb_ref, accb_ref, *, t, lambda_init):
    hd = pl.program_id(1)
    i = pl.program_id(2)
    slope = slope_ref[hd]
    qa, qb, _ = _split_subheads(q_ref[...])
    _init_stats(ma_ref, la_ref, acca_ref)
    _init_stats(mb_ref, lb_ref, accb_ref)
    col = lax.broadcasted_iota(jnp.int32, (1, t), 1).astype(F32)

    def tile(j, bias):
        k0 = pl.multiple_of(j * t, t)
        kj = k_ref[pl.ds(k0, t), :]
        vj = v_ref[pl.ds(k0, t), :]
        _softmax_step(_dot_nt(qa, kj) + bias, vj, ma_ref, la_ref, acca_ref)
        _softmax_step(_dot_nt(qb, kj) + bias, vj, mb_ref, lb_ref, accb_ref)

    def off_diag(j, carry):
        tile(j, slope * (col + ((j - i) * t).astype(F32)))
        return carry

    lax.fori_loop(0, i, off_diag, 0)

    r = lax.broadcasted_iota(jnp.int32, (t, t), 0)
    c = lax.broadcasted_iota(jnp.int32, (t, t), 1)
    allowed = (c // CHUNK) <= (r // CHUNK)
    dist = jnp.abs(r - c)
    bias = jnp.where(allowed, slope * (r - dist).astype(F32), NEG)
    tile(i, bias)

    lp = lam_ref[...]
    lam = (jnp.exp(jnp.sum(lp[0:1] * lp[1:2], axis=-1, keepdims=True))
           - jnp.exp(jnp.sum(lp[2:3] * lp[3:4], axis=-1, keepdims=True)) + lambda_init)
    out = acca_ref[...] / la_ref[...] - lam * (accb_ref[...] / lb_ref[...])
    ms = jnp.mean(out * out, axis=-1, keepdims=True)
    out = out * lax.rsqrt(ms + EPS) * sg_ref[...] * (1.0 - lambda_init)
    o_ref[...] = out.astype(BF16)


def _da_call(z, slopes, lam_p, subln_g, heads, lambda_init):
    b, s, _ = z.shape
    t = min(ATTN_TILE, s)
    assert s % t == 0 and t % CHUNK == 0
    kern = functools.partial(_da_kernel, t=t, lambda_init=lambda_init)
    stat = lambda: pltpu.VMEM((t, 1), F32)
    acc = lambda: pltpu.VMEM((t, LANES), F32)
    return pl.pallas_call(
        kern,
        grid_spec=pltpu.PrefetchScalarGridSpec(
            num_scalar_prefetch=1,
            grid=(b, heads, s // t),
            in_specs=[
                pl.BlockSpec((None, t, LANES), lambda bi, h, i, sl: (bi, i, h)),
                pl.BlockSpec((None, s, LANES), lambda bi, h, i, sl: (bi, 0, heads + h)),
                pl.BlockSpec((None, s, LANES), lambda bi, h, i, sl: (bi, 0, 2 * heads + h)),
                pl.BlockSpec((4, HEAD_DIM), lambda bi, h, i, sl: (0, 0)),
                pl.BlockSpec((1, LANES), lambda bi, h, i, sl: (0, 0)),
            ],
            out_specs=pl.BlockSpec((None, t, LANES), lambda bi, h, i, sl: (bi, i, h)),
            scratch_shapes=[stat(), stat(), acc(), stat(), stat(), acc()],
        ),
        out_shape=jax.ShapeDtypeStruct((b, s, heads * LANES), BF16),
        compiler_params=_params("arbitrary", "arbitrary", "arbitrary"),
        name="diff_attention",
    )(slopes, z, z, z, lam_p, subln_g)


def _fox_gate_kernel(g_ref, b_ref, o_ref, *, chunk, heads):
    s = g_ref.shape[0]
    r = lax.broadcasted_iota(jnp.int32, (chunk, chunk), 0)
    c = lax.broadcasted_iota(jnp.int32, (chunk, chunk), 1)
    tri = (c <= r).astype(BF16)
    carry = jnp.zeros((1, LANES), F32)
    for c0 in range(0, s, chunk):
        ls = _log_sigmoid(g_ref[c0:c0 + chunk, :] + b_ref[...])
        cs = _exact_lhs_dot(tri, ls) + carry
        carry = cs[chunk - 1:chunk, :]
        o_ref[:, c0:c0 + chunk] = cs.T[:heads, :]


def _fox_gate_call(gates, bias_row, heads):
    b, s, _ = gates.shape
    chunk = min(GATE_CHUNK, s)
    assert s % chunk == 0 and heads % 8 == 0
    kern = functools.partial(_fox_gate_kernel, chunk=chunk, heads=heads)
    return pl.pallas_call(
        kern,
        grid=(b,),
        in_specs=[
            pl.BlockSpec((None, s, LANES), lambda i: (i, 0, 0)),
            pl.BlockSpec((1, LANES), lambda i: (0, 0)),
        ],
        out_specs=pl.BlockSpec((None, heads, s), lambda i: (i, 0, 0)),
        out_shape=jax.ShapeDtypeStruct((b, heads, s), F32),
        compiler_params=_params("arbitrary"),
        name="fox_gate_cumsum",
    )(gates, bias_row)


def _fox_kernel(q_ref, k_ref, v_ref, f_ref, o_ref,
                ma_ref, la_ref, acca_ref, mb_ref, lb_ref, accb_ref, *, t):
    i = pl.program_id(2)
    qa, qb, lo = _split_subheads(q_ref[...])
    _init_stats(ma_ref, la_ref, acca_ref)
    _init_stats(mb_ref, lb_ref, accb_ref)
    q0 = pl.multiple_of(i * t, t)
    f_start = f_ref[:, pl.ds(q0, t)][:, 0:1]

    def tile(j, mask):
        k0 = pl.multiple_of(j * t, t)
        kj = k_ref[pl.ds(k0, t), :]
        vj = v_ref[pl.ds(k0, t), :]
        bias = f_start - f_ref[:, pl.ds(k0, t)]
        sa = _dot_nt(qa, kj) + bias[0:1, :]
        sb = _dot_nt(qb, kj) + bias[1:2, :]
        if mask is not None:
            sa = jnp.where(mask, sa, NEG)
            sb = jnp.where(mask, sb, NEG)
        _softmax_step(sa, vj, ma_ref, la_ref, acca_ref)
        _softmax_step(sb, vj, mb_ref, lb_ref, accb_ref)

    def off_diag(j, carry):
        tile(j, None)
        return carry

    lax.fori_loop(0, i, off_diag, 0)
    r = lax.broadcasted_iota(jnp.int32, (t, t), 0)
    c = lax.broadcasted_iota(jnp.int32, (t, t), 1)
    tile(i, c <= r)

    out = jnp.where(lo, acca_ref[...] / la_ref[...], accb_ref[...] / lb_ref[...])
    o_ref[...] = out.astype(BF16)


def _fox_call(z, fcum, heads):
    b, s, _ = z.shape
    pairs = heads // 2
    t = min(ATTN_TILE, s)
    assert s % t == 0
    kern = functools.partial(_fox_kernel, t=t)
    stat = lambda: pltpu.VMEM((t, 1), F32)
    acc = lambda: pltpu.VMEM((t, LANES), F32)
    return pl.pallas_call(
        kern,
        grid=(b, pairs, s // t),
        in_specs=[
            pl.BlockSpec((None, t, LANES), lambda bi, p, i: (bi, i, p)),
            pl.BlockSpec((None, s, LANES), lambda bi, p, i: (bi, 0, pairs + p)),
            pl.BlockSpec((None, s, LANES), lambda bi, p, i: (bi, 0, 2 * pairs + p)),
            pl.BlockSpec((None, None, 2, s), lambda bi, p, i: (bi, p, 0, 0)),
        ],
        out_specs=pl.BlockSpec((None, t, LANES), lambda bi, p, i: (bi, i, p)),
        out_shape=jax.ShapeDtypeStruct((b, s, pairs * LANES), BF16),
        scratch_shapes=[stat(), stat(), acc(), stat(), stat(), acc()],
        compiler_params=_params("arbitrary", "arbitrary", "arbitrary"),
        name="forgetting_attention",
    )(z, z, z, fcum.reshape(b, pairs, 2, s))


def _mlstm_kernel(q_ref, k_ref, v_ref, o_ref, g_ref, gb_ref, cw_ref, cb_ref, y_ref,
                  halo_ref, ct_ref, n_ref, m_ref, *, heads, taps):
    ci = pl.program_id(1)
    lc, width = q_ref.shape
    d = width // heads
    pad = halo_ref.shape[0]

    @pl.when(ci == 0)
    def _():
        halo_ref[...] = jnp.zeros(halo_ref.shape, F32)
        ct_ref[...] = jnp.zeros(ct_ref.shape, F32)
        n_ref[...] = jnp.zeros(n_ref.shape, F32)
        m_ref[...] = jnp.zeros(m_ref.shape, F32)

    xqk = jnp.concatenate([q_ref[...].astype(F32), k_ref[...].astype(F32)], axis=-1)
    full = jnp.concatenate([halo_ref[...], xqk], axis=0)
    conv = cb_ref[...]
    for j in range(taps):
        off = pad - (taps - 1) + j
        conv = conv + full[off:off + lc, :] * cw_ref[j:j + 1, :]
    halo_ref[...] = xqk[lc - pad:, :]
    qk = conv * _sigmoid(conv)

    gcol = g_ref[...] + gb_ref[...]
    grow = gcol.T[:2 * heads, :]
    r = lax.broadcasted_iota(jnp.int32, (lc, lc), 0)
    c = lax.broadcasted_iota(jnp.int32, (lc, lc), 1)
    causal = c <= r
    a_cols = _exact_lhs_dot(causal.astype(BF16), _log_sigmoid(gcol))
    a_rows = _exact_rhs_dot(_log_sigmoid(grow), (r <= c).astype(BF16))

    for hd in range(heads):
        q = qk[:, hd * d:(hd + 1) * d]
        k = qk[:, width + hd * d:width + (hd + 1) * d] * (d ** -0.5)
        v = v_ref[:, hd * d:(hd + 1) * d]
        qb = q.astype(BF16)
        kb = k.astype(BF16)
        ig_c = gcol[:, hd:hd + 1]
        ig_r = grow[hd:hd + 1, :]
        a_c = a_cols[:, heads + hd:heads + hd + 1]
        a_r = a_rows[heads + hd:heads + hd + 1, :]
        a_last = a_c[lc - 1:lc, :]
        ct = ct_ref[hd]
        n_prev = n_ref[hd:hd + 1, :]
        m_prev = m_ref[hd:hd + 1, 0:1]

        log_d = jnp.where(causal, a_c - a_r + ig_r, NEG)
        m_t = jnp.maximum(a_c + m_prev, jnp.max(log_d, axis=-1, keepdims=True))
        inter_w = jnp.exp(a_c + m_prev - m_t)
        sqk = _dot_nt(qb, kb) * jnp.exp(log_d - m_t)
        num = inter_w * _dot(qb, ct.astype(BF16)) + _dot(sqk.astype(BF16), v)
        den = (inter_w * jnp.sum(q * n_prev, axis=-1, keepdims=True)
               + jnp.sum(sqk, axis=-1, keepdims=True))
        hval = num / jnp.maximum(jnp.abs(den), jnp.exp(-m_t))
        og = o_ref[:, hd * d:(hd + 1) * d].astype(F32)
        y_ref[:, hd * d:(hd + 1) * d] = (_sigmoid(og) * hval).astype(BF16)

        g_c = a_last - a_c + ig_c
        m_loc = jnp.max(g_c, axis=0, keepdims=True)
        w_c = jnp.exp(g_c - m_loc)
        m_new = jnp.maximum(a_last + m_prev, m_loc)
        decay = jnp.exp(a_last + m_prev - m_new)
        s_loc = jnp.exp(m_loc - m_new)
        wv = (w_c * v.astype(F32)).astype(BF16)
        ct_ref[hd] = decay * ct + s_loc * _dot_tn(kb, wv)
        n_ref[hd:hd + 1, :] = decay * n_prev + s_loc * jnp.sum(w_c * k, axis=0, keepdims=True)
        m_ref[hd:hd + 1, :] = jnp.broadcast_to(m_new, (1, LANES))


def _mlstm_call(z, gates, gate_bias, conv_w, conv_b, heads, col0):
    b, s, _ = z.shape
    width = conv_w.shape[1] // 2
    lc = min(ML_CHUNK, s)
    assert s % lc == 0 and col0 % width == 0 and width // heads == LANES
    cb = col0 // width
    taps = conv_w.shape[0]
    kern = functools.partial(_mlstm_kernel, heads=heads, taps=taps)
    zspec = lambda off: pl.BlockSpec((None, lc, width), lambda bi, ci: (bi, ci, cb + off))
    return pl.pallas_call(
        kern,
        grid=(b, s // lc),
        in_specs=[
            zspec(0), zspec(1), zspec(2), zspec(3),
            pl.BlockSpec((None, lc, LANES), lambda bi, ci: (bi, ci, 0)),
            pl.BlockSpec((1, LANES), lambda bi, ci: (0, 0)),
            pl.BlockSpec((taps, 2 * width), lambda bi, ci: (0, 0)),
            pl.BlockSpec((1, 2 * width), lambda bi, ci: (0, 0)),
        ],
        out_specs=pl.BlockSpec((None, lc, width), lambda bi, ci: (bi, ci, 0)),
        out_shape=jax.ShapeDtypeStruct((b, s, width), BF16),
        scratch_shapes=[
            pltpu.VMEM((8, 2 * width), F32),
            pltpu.VMEM((heads, LANES, LANES), F32),
            pltpu.VMEM((8, LANES), F32),
            pltpu.VMEM((8, LANES), F32),
        ],
        compiler_params=_params("arbitrary", "arbitrary"),
        name="mlstm",
    )(z, z, z, z, gates, gate_bias, conv_w, conv_b)


def _outproj_kernel(*refs, n_in):
    y_refs = refs[:n_in]
    w_refs = refs[n_in:2 * n_in]
    x_ref, gate_ref, o_ref = refs[2 * n_in:]
    acc = _dot(y_refs[0][...], w_refs[0][...])
    for y_ref, w_ref in zip(y_refs[1:], w_refs[1:]):
        acc = acc + _dot(y_ref[...], w_ref[...])
    o_ref[...] = x_ref[...] + gate_ref[...] * acc


def _outproj_call(ys, ws, x, gate, name):
    b, s, d = x.shape
    tm = min(ROW_TILE, s)
    assert s % tm == 0
    n_in = len(ys)
    kern = functools.partial(_outproj_kernel, n_in=n_in)
    y_specs = [pl.BlockSpec((None, tm, y.shape[2]), lambda i, j: (i, j, 0)) for y in ys]
    w_specs = [pl.BlockSpec(w.shape, lambda i, j: (0, 0)) for w in ws]
    return pl.pallas_call(
        kern,
        grid=(b, s // tm),
        in_specs=y_specs + w_specs + [
            pl.BlockSpec((None, tm, d), lambda i, j: (i, j, 0)),
            pl.BlockSpec((None, 1, d), lambda i, j: (i, 0, 0)),
        ],
        out_specs=pl.BlockSpec((None, tm, d), lambda i, j: (i, j, 0)),
        out_shape=jax.ShapeDtypeStruct((b, s, d), F32),
        compiler_params=_params("arbitrary", "arbitrary"),
        name=name,
    )(*ys, *ws, x, gate)


def _ffn_kernel(x_ref, sh_ref, sc_ref, g_ref, gate_ref, w1_ref, w3_ref, w2_ref, o_ref, *, chunk):
    x = x_ref[...]
    h = _norm_modulate(x, g_ref[...], sh_ref[...], sc_ref[...]).astype(BF16)
    hidden = w1_ref.shape[1]
    acc = None
    for c0 in range(0, hidden, chunk):
        a = _dot(h, w1_ref[:, c0:c0 + chunk])
        bb = _dot(h, w3_ref[:, c0:c0 + chunk])
        u = (a * _sigmoid(a) * bb).astype(BF16)
        part = _dot(u, w2_ref[c0:c0 + chunk, :])
        acc = part if acc is None else acc + part
    o_ref[...] = x + gate_ref[...] * acc


def _ffn_call(x, shift, scale, g, gate, w1, w3, w2):
    b, s, d = x.shape
    hidden = w1.shape[1]
    tm = min(ROW_TILE, s)
    assert s % tm == 0 and hidden % FFN_CHUNK == 0
    kern = functools.partial(_ffn_kernel, chunk=FFN_CHUNK)
    vec = pl.BlockSpec((None, 1, d), lambda i, j: (i, 0, 0))
    return pl.pallas_call(
        kern,
        grid=(b, s // tm),
        in_specs=[
            pl.BlockSpec((None, tm, d), lambda i, j: (i, j, 0)),
            vec, vec,
            pl.BlockSpec((1, d), lambda i, j: (0, 0)),
            vec,
            pl.BlockSpec((d, hidden), lambda i, j: (0, 0)),
            pl.BlockSpec((d, hidden), lambda i, j: (0, 0)),
            pl.BlockSpec((hidden, d), lambda i, j: (0, 0)),
        ],
        out_specs=pl.BlockSpec((None, tm, d), lambda i, j: (i, j, 0)),
        out_shape=jax.ShapeDtypeStruct((b, s, d), F32),
        compiler_params=_params("arbitrary", "arbitrary"),
        name="ffn_swiglu",
    )(x, shift, scale, g, gate, w1, w3, w2)


def _pad_cols(w, n):
    return jnp.pad(w, ((0, 0), (0, n - w.shape[1])))


def _gain_row(q_g, k_g, width, total):
    reps = width // HEAD_DIM
    row = jnp.concatenate([jnp.tile(q_g.astype(F32) * (HEAD_DIM ** -0.5), reps),
                           jnp.tile(k_g.astype(F32), reps),
                           jnp.ones((total - 2 * width,), F32)])
    return row.reshape(1, total)


def kernel(x, c, ada_w, ada_b, norm_mix_g, norm_ffn_g, ab_w_in, ml_b_i, ml_b_f, ml_conv_w, ml_conv_b,
           da_q_g, da_k_g, da_lambda, da_subln_g, ab_w_out, fx_w_in, fx_b_f, fx_q_g, fx_k_g, fx_w_out,
           ffn_w1, ffn_w3, ffn_w2):
    depth = ada_w.shape[0]
    b, s, d = x.shape
    ml_width = ml_conv_w.shape[2] // 2
    ml_heads = ml_b_i.shape[1]
    da_width = (ab_w_in.shape[2] - 4 * ml_width - 2 * ml_heads) // 3
    da_heads = da_width // (2 * HEAD_DIM)
    fx_heads = fx_b_f.shape[1]
    fx_width = fx_heads * HEAD_DIM
    ab_main = 3 * da_width + 4 * ml_width
    fx_main = 3 * fx_width

    mods = _ada_call(c, ada_w, ada_b).reshape(depth, b, 6, 1, d)
    slopes = 2.0 ** (-8.0 * jnp.arange(1, da_heads + 1, dtype=F32) / da_heads)

    for l in range(depth):
        sh_m, sc_m, g_m, sh_f, sc_f, g_f = (mods[l, :, t] for t in range(6))
        j = l // 2
        g_mix = norm_mix_g[l].reshape(1, d)
        if l % 2 == 0:
            w_in = ab_w_in[j]
            z, gates = _inproj_call(
                x, sh_m, sc_m, g_mix,
                w_in[:, :ab_main].astype(BF16),
                _pad_cols(w_in[:, ab_main:], LANES).astype(BF16),
                _gain_row(da_q_g[j], da_k_g[j], da_width, ab_main),
                2 * da_width, "inproj_even")
            lambda_init = 0.8 - 0.6 * math.exp(-0.3 * l)
            y_da = _da_call(z, slopes, da_lambda[j], da_subln_g[j].reshape(1, -1),
                            da_heads, lambda_init)
            gate_bias = _pad_cols(jnp.concatenate([ml_b_i[j], ml_b_f[j]]).reshape(1, -1), LANES)
            y_ml = _mlstm_call(z, gates, gate_bias, ml_conv_w[j], ml_conv_b[j].reshape(1, -1),
                               ml_heads, 3 * da_width)
            w_out = ab_w_out[j].astype(BF16)
            x = _outproj_call([y_da, y_ml], [w_out[:da_width], w_out[da_width:]], x, g_m,
                              "outproj_even")
        else:
            w_in = fx_w_in[j]
            z, gates = _inproj_call(
                x, sh_m, sc_m, g_mix,
                w_in[:, :fx_main].astype(BF16),
                _pad_cols(w_in[:, fx_main:], LANES).astype(BF16),
                _gain_row(fx_q_g[j], fx_k_g[j], fx_width, fx_main),
                2 * fx_width, "inproj_odd")
            fcum = _fox_gate_call(gates, _pad_cols(fx_b_f[j].reshape(1, -1), LANES), fx_heads)
            y = _fox_call(z, fcum, fx_heads)
            x = _outproj_call([y], [fx_w_out[j].astype(BF16)], x, g_m, "outproj_odd")
        x = _ffn_call(x, sh_f, sc_f, norm_ffn_g[l].reshape(1, d), g_f,
                      ffn_w1[l].astype(BF16), ffn_w3[l].astype(BF16), ffn_w2[l].astype(BF16))
    return x
```

```python
import functools
import math

import jax
import jax.numpy as jnp
from jax import lax
from jax.experimental import pallas as pl
from jax.experimental.pallas import tpu as pltpu

F32 = jnp.float32
BF16 = jnp.bfloat16

EPS = 1e-6
LANES = 128
HEAD_DIM = 64
CHUNK = 64
NEG = -1e30
LOG2E = math.log2(math.e)
BIAS_PIECES = 3

VMEM_LIMIT = 56 * 1024 * 1024

ROW_TILE = 512
ATTN_TILE = 256
ATTN_KEYS = 2 * ATTN_TILE
PACK_ROWS = 512
ML_CHUNK = 256
GATE_CHUNK = 256
FFN_CHUNK = 256


def _params(*sem):
    return pltpu.CompilerParams(dimension_semantics=sem, vmem_limit_bytes=VMEM_LIMIT)


def _log_sigmoid(x):
    return jnp.minimum(x, 0.0) - jnp.log1p(jnp.exp(-jnp.abs(x)))


def _sigmoid(x):
    return 1.0 / (1.0 + jnp.exp(-x))


def _split3(x):
    x1 = x.astype(BF16)
    r1 = x - x1.astype(F32)
    x2 = r1.astype(BF16)
    x3 = (r1 - x2.astype(F32)).astype(BF16)
    return x1, x2, x3


def _dot(a, b):
    return jnp.dot(a, b, preferred_element_type=F32)


def _dot_nt(a, b):
    return lax.dot_general(a, b, (((1,), (1,)), ((), ())), preferred_element_type=F32)


def _dot_tn(a, b):
    return lax.dot_general(a, b, (((0,), (0,)), ((), ())), preferred_element_type=F32)


def _exact_lhs_dot(m_bf16, x):
    x1, x2, x3 = _split3(x)
    return _dot(m_bf16, x1) + _dot(m_bf16, x2) + _dot(m_bf16, x3)


def _exact_rhs_dot(x, m_bf16):
    x1, x2, x3 = _split3(x)
    return _dot(x1, m_bf16) + _dot(x2, m_bf16) + _dot(x3, m_bf16)


def _norm_modulate(x, g, sh, sc):
    ms = jnp.mean(x * x, axis=-1, keepdims=True)
    y = x * lax.rsqrt(ms + EPS) * g
    return y * (1.0 + sc) + sh


def _ada_kernel(c_ref, w_ref, b_ref, o_ref):
    c = c_ref[...]
    ca = c * _sigmoid(c)
    c1, c2, c3 = _split3(ca)
    w1, w2, w3 = _split3(w_ref[...])
    acc = _dot(c1, w1) + _dot(c1, w2) + _dot(c2, w1)
    acc = acc + _dot(c1, w3) + _dot(c2, w2) + _dot(c3, w1)
    o_ref[...] = acc + b_ref[...]


def _ada_call(c, ada_w, ada_b):
    depth, d, n = ada_w.shape
    b = c.shape[0]
    tn = 1536
    assert n % tn == 0
    return pl.pallas_call(
        _ada_kernel,
        grid=(depth, n // tn),
        in_specs=[
            pl.BlockSpec((b, d), lambda l, j: (0, 0)),
            pl.BlockSpec((None, d, tn), lambda l, j: (l, 0, j)),
            pl.BlockSpec((None, 1, tn), lambda l, j: (l, 0, j)),
        ],
        out_specs=pl.BlockSpec((None, b, tn), lambda l, j: (l, 0, j)),
        out_shape=jax.ShapeDtypeStruct((depth, b, n), F32),
        compiler_params=_params("arbitrary", "arbitrary"),
        name="ada_mod",
    )(c, ada_w, ada_b.reshape(depth, 1, n))


def _inproj_kernel(x_ref, sh_ref, sc_ref, g_ref, w_ref, wg_ref, gain_ref, z_ref, gate_ref,
                   *, n_norm, chunk):
    h = _norm_modulate(x_ref[...], g_ref[...], sh_ref[...], sc_ref[...]).astype(BF16)
    n = w_ref.shape[1]
    lane = lax.broadcasted_iota(jnp.int32, (1, LANES), 1)
    lo = lane < HEAD_DIM
    for c0 in range(0, n, chunk):
        zc = _dot(h, w_ref[:, c0:c0 + chunk])
        if c0 < n_norm:
            parts = []
            for s0 in range(0, chunk, LANES):
                zs = zc[:, s0:s0 + LANES]
                sq = zs * zs
                s_lo = jnp.sum(jnp.where(lo, sq, 0.0), axis=-1, keepdims=True)
                s_hi = jnp.sum(jnp.where(lo, 0.0, sq), axis=-1, keepdims=True)
                r_lo = lax.rsqrt(s_lo * (1.0 / HEAD_DIM) + EPS)
                r_hi = lax.rsqrt(s_hi * (1.0 / HEAD_DIM) + EPS)
                parts.append(zs * jnp.where(lo, r_lo, r_hi))
            zc = jnp.concatenate(parts, axis=-1) * gain_ref[:, c0:c0 + chunk]
        z_ref[:, c0:c0 + chunk] = zc.astype(BF16)
    gate_ref[...] = _dot(h, wg_ref[...])


def _inproj_call(x, shift, scale, g, w, wg, gain, n_norm, name):
    b, s, d = x.shape
    n = w.shape[1]
    tm = min(ROW_TILE, s)
    chunk = 512
    assert s % tm == 0 and n % chunk == 0 and n_norm % chunk == 0
    kern = functools.partial(_inproj_kernel, n_norm=n_norm, chunk=chunk)
    return pl.pallas_call(
        kern,
        grid=(b, s // tm),
        in_specs=[
            pl.BlockSpec((None, tm, d), lambda i, j: (i, j, 0)),
            pl.BlockSpec((None, 1, d), lambda i, j: (i, 0, 0)),
            pl.BlockSpec((None, 1, d), lambda i, j: (i, 0, 0)),
            pl.BlockSpec((1, d), lambda i, j: (0, 0)),
            pl.BlockSpec((d, n), lambda i, j: (0, 0)),
            pl.BlockSpec((d, LANES), lambda i, j: (0, 0)),
            pl.BlockSpec((1, n), lambda i, j: (0, 0)),
        ],
        out_specs=[
            pl.BlockSpec((None, tm, n), lambda i, j: (i, j, 0)),
            pl.BlockSpec((None, tm, LANES), lambda i, j: (i, j, 0)),
        ],
        out_shape=[
            jax.ShapeDtypeStruct((b, s, n), BF16),
            jax.ShapeDtypeStruct((b, s, LANES), F32),
        ],
        compiler_params=_params("arbitrary", "arbitrary"),
        name=name,
    )(x, shift, scale, g, w, wg, gain)


def _lane_ids():
    return lax.broadcasted_iota(jnp.int32, (1, LANES), 1)


def _place_pieces(pieces, first_lane):
    lane = _lane_ids()
    out = jnp.zeros(pieces[0].shape, F32)
    for j, p in enumerate(pieces):
        out = jnp.where(lane == first_lane + j, p.astype(F32), out)
    return out


def _pack_keys(k_ref, v_ref, extras_fn, ka_ref, kb_ref, vt_ref):
    s = k_ref.shape[0]
    rows = min(PACK_ROWS, s)
    lo = _lane_ids() < HEAD_DIM

    def body(c, carry):
        r0 = pl.multiple_of(c * rows, rows)
        k2 = k_ref[pl.ds(r0, rows), :].astype(F32)
        ea, eb = extras_fn(r0, rows)
        ka_ref[pl.ds(r0, rows), :] = jnp.where(lo, k2, ea).astype(BF16)
        kb_ref[pl.ds(r0, rows), :] = jnp.where(lo, eb, k2).astype(BF16)
        vt_ref[:, pl.ds(r0, rows)] = v_ref[pl.ds(r0, rows), :].astype(F32).T.astype(BF16)
        return carry

    lax.fori_loop(0, s // rows, body, 0)


def _pack_queries(q2):
    lane = _lane_ids()
    q2 = q2.astype(F32)
    ones_a = jnp.where((lane >= HEAD_DIM) & (lane < HEAD_DIM + BIAS_PIECES), 1.0, 0.0)
    ones_b = jnp.where(lane < BIAS_PIECES, 1.0, 0.0)
    lo = lane < HEAD_DIM
    return jnp.where(lo, q2, ones_a).astype(BF16), jnp.where(lo, ones_b, q2).astype(BF16)


def _init_stats(m_ref, l_ref, acc_ref):
    m_ref[...] = jnp.full(m_ref.shape, NEG, F32)
    l_ref[...] = jnp.zeros(l_ref.shape, F32)
    acc_ref[...] = jnp.zeros(acc_ref.shape, F32)


def _softmax_step(s, vt, m_ref, l_ref, acc_ref):
    m_old = m_ref[...]
    m_new = jnp.maximum(m_old, jnp.max(s, axis=0, keepdims=True))
    alpha = jnp.exp2(m_old - m_new)
    p = jnp.exp2(s - m_new)
    l_ref[...] = alpha * l_ref[...] + jnp.sum(p, axis=0, keepdims=True)
    acc_ref[...] = alpha * acc_ref[...] + _dot(vt, p.astype(BF16))
    m_ref[...] = m_new


def _attend(i, t, qa, qb, ka_ref, kb_ref, vta, vtb, diag_add, stats_a, stats_b):
    tk = ATTN_KEYS

    def tile(k0, size, add):
        sa = _dot_nt(ka_ref[pl.ds(k0, size), :], qa)
        sb = _dot_nt(kb_ref[pl.ds(k0, size), :], qb)
        if add is not None:
            sa = sa + add
            sb = sb + add
        _softmax_step(sa, vta(k0, size), *stats_a)
        _softmax_step(sb, vtb(k0, size), *stats_b)

    def below(j, carry):
        tile(pl.multiple_of(j * tk, tk), tk, None)
        return carry

    n_wide = (i * t) // tk
    lax.fori_loop(0, n_wide, below, 0)
    for r in range(tk // t - 1):
        @pl.when(i * t - n_wide * tk > r * t)
        def _():
            tile(pl.multiple_of(n_wide * tk + r * t, t), t, None)
    tile(pl.multiple_of(i * t, t), t, diag_add)


def _da_kernel(slope_ref, q_ref, k_ref, v_ref, lam_ref, sg_ref, o_ref,
               ka_ref, kb_ref, vt_ref, ma_ref, la_ref, acca_ref, mb_ref, lb_ref, accb_ref,
               *, t, lambda_init):
    i = pl.program_id(2)
    slope2 = slope_ref[pl.program_id(1)] * LOG2E

    @pl.when(i == 0)
    def _():
        def extras(r0, rows):
            pos = (lax.broadcasted_iota(jnp.int32, (rows, LANES), 0) + r0).astype(F32) * slope2
            pieces = _split3(pos)
            return _place_pieces(pieces, HEAD_DIM), _place_pieces(pieces, 0)
        _pack_keys(k_ref, v_ref, extras, ka_ref, kb_ref, vt_ref)

    qa, qb = _pack_queries(q_ref[...])
    stats_a = (ma_ref, la_ref, acca_ref)
    stats_b = (mb_ref, lb_ref, accb_ref)
    _init_stats(*stats_a)
    _init_stats(*stats_b)

    key = lax.broadcasted_iota(jnp.int32, (t, t), 0)
    qry = lax.broadcasted_iota(jnp.int32, (t, t), 1)
    allowed = (key // CHUNK) <= (qry // CHUNK)
    above = jnp.maximum(key - qry, 0).astype(F32)
    diag_add = jnp.where(allowed, (-2.0 * slope2) * above, NEG)

    vt = lambda k0, size: vt_ref[:, pl.ds(k0, size)]
    _attend(i, t, qa, qb, ka_ref, kb_ref, vt, vt, diag_add, stats_a, stats_b)

    lp = lam_ref[...]
    lam = (jnp.exp(jnp.sum(lp[0:1] * lp[1:2], axis=-1, keepdims=True))
           - jnp.exp(jnp.sum(lp[2:3] * lp[3:4], axis=-1, keepdims=True)) + lambda_init)
    out = acca_ref[...] / la_ref[...] - lam * (accb_ref[...] / lb_ref[...])
    ms = jnp.mean(out * out, axis=0, keepdims=True)
    out = out * lax.rsqrt(ms + EPS) * (sg_ref[...] * (1.0 - lambda_init))
    o_ref[...] = out.T.astype(BF16)


def _da_call(z, slopes, lam_p, subln_g, heads, lambda_init):
    b, s, _ = z.shape
    t = min(ATTN_TILE, s)
    assert s % ATTN_KEYS == 0 and t % CHUNK == 0
    kern = functools.partial(_da_kernel, t=t, lambda_init=lambda_init)
    stat = lambda: pltpu.VMEM((1, t), F32)
    acc = lambda: pltpu.VMEM((LANES, t), F32)
    return pl.pallas_call(
        kern,
        grid_spec=pltpu.PrefetchScalarGridSpec(
            num_scalar_prefetch=1,
            grid=(b, heads, s // t),
            in_specs=[
                pl.BlockSpec((None, t, LANES), lambda bi, h, i, sl: (bi, i, h)),
                pl.BlockSpec((None, s, LANES), lambda bi, h, i, sl: (bi, 0, heads + h)),
                pl.BlockSpec((None, s, LANES), lambda bi, h, i, sl: (bi, 0, 2 * heads + h)),
                pl.BlockSpec((4, HEAD_DIM), lambda bi, h, i, sl: (0, 0)),
                pl.BlockSpec((LANES, 1), lambda bi, h, i, sl: (0, 0)),
            ],
            out_specs=pl.BlockSpec((None, t, LANES), lambda bi, h, i, sl: (bi, i, h)),
            scratch_shapes=[
                pltpu.VMEM((s, LANES), BF16), pltpu.VMEM((s, LANES), BF16),
                pltpu.VMEM((LANES, s), BF16),
                stat(), stat(), acc(), stat(), stat(), acc(),
            ],
        ),
        out_shape=jax.ShapeDtypeStruct((b, s, heads * LANES), BF16),
        compiler_params=_params("arbitrary", "arbitrary", "arbitrary"),
        name="diff_attention",
    )(slopes, z, z, z, lam_p, subln_g.reshape(LANES, 1))


def _fox_gate_kernel(g_ref, b_ref, o_ref, *, chunk):
    s = g_ref.shape[0]
    r = lax.broadcasted_iota(jnp.int32, (chunk, chunk), 0)
    c = lax.broadcasted_iota(jnp.int32, (chunk, chunk), 1)
    tri = (c <= r).astype(BF16)
    carry = jnp.zeros((1, LANES), F32)
    for c0 in range(0, s, chunk):
        ls = _log_sigmoid(g_ref[c0:c0 + chunk, :] + b_ref[...])
        cs = _exact_lhs_dot(tri, ls) + carry
        carry = cs[chunk - 1:chunk, :]
        o_ref[c0:c0 + chunk, :] = cs * (-LOG2E)


def _fox_gate_call(gates, bias_row):
    b, s, _ = gates.shape
    chunk = min(GATE_CHUNK, s)
    assert s % chunk == 0
    kern = functools.partial(_fox_gate_kernel, chunk=chunk)
    return pl.pallas_call(
        kern,
        grid=(b,),
        in_specs=[
            pl.BlockSpec((None, s, LANES), lambda i: (i, 0, 0)),
            pl.BlockSpec((1, LANES), lambda i: (0, 0)),
        ],
        out_specs=pl.BlockSpec((None, s, LANES), lambda i: (i, 0, 0)),
        out_shape=jax.ShapeDtypeStruct((b, s, LANES), F32),
        compiler_params=_params("arbitrary"),
        name="fox_gate_cumsum",
    )(gates, bias_row)


def _fox_kernel(q_ref, k_ref, v_ref, f_ref, o_ref,
                ka_ref, kb_ref, vt_ref, ma_ref, la_ref, acca_ref, mb_ref, lb_ref, accb_ref, *, t):
    pair = pl.program_id(1)
    i = pl.program_id(2)

    @pl.when(i == 0)
    def _():
        n = BIAS_PIECES * LANES
        src = lax.broadcasted_iota(jnp.int32, (n, LANES), 0)
        dst = lax.broadcasted_iota(jnp.int32, (n, LANES), 1)
        piece = src // LANES
        head = src - piece * LANES
        place_a = ((head == 2 * pair) & (dst == HEAD_DIM + piece)).astype(BF16)
        place_b = ((head == 2 * pair + 1) & (dst == piece) & (dst < BIAS_PIECES)).astype(BF16)

        def extras(r0, rows):
            pieces = jnp.concatenate(_split3(f_ref[pl.ds(r0, rows), :]), axis=-1)
            return _dot(pieces, place_a), _dot(pieces, place_b)
        _pack_keys(k_ref, v_ref, extras, ka_ref, kb_ref, vt_ref)

    qa, qb = _pack_queries(q_ref[...])
    stats_a = (ma_ref, la_ref, acca_ref)
    stats_b = (mb_ref, lb_ref, accb_ref)
    _init_stats(*stats_a)
    _init_stats(*stats_b)

    key = lax.broadcasted_iota(jnp.int32, (t, t), 0)
    qry = lax.broadcasted_iota(jnp.int32, (t, t), 1)
    diag_add = jnp.where(key <= qry, 0.0, NEG)

    vta = lambda k0, size: vt_ref[0:HEAD_DIM, pl.ds(k0, size)]
    vtb = lambda k0, size: vt_ref[HEAD_DIM:2 * HEAD_DIM, pl.ds(k0, size)]
    _attend(i, t, qa, qb, ka_ref, kb_ref, vta, vtb, diag_add, stats_a, stats_b)

    out = jnp.concatenate([acca_ref[...] / la_ref[...], accb_ref[...] / lb_ref[...]], axis=0)
    o_ref[...] = out.T.astype(BF16)


def _fox_call(z, fneg, heads):
    b, s, _ = z.shape
    pairs = heads // 2
    t = min(ATTN_TILE, s)
    assert s % ATTN_KEYS == 0
    kern = functools.partial(_fox_kernel, t=t)
    stat = lambda: pltpu.VMEM((1, t), F32)
    acc = lambda: pltpu.VMEM((HEAD_DIM, t), F32)
    return pl.pallas_call(
        kern,
        grid=(b, pairs, s // t),
        in_specs=[
            pl.BlockSpec((None, t, LANES), lambda bi, p, i: (bi, i, p)),
            pl.BlockSpec((None, s, LANES), lambda bi, p, i: (bi, 0, pairs + p)),
            pl.BlockSpec((None, s, LANES), lambda bi, p, i: (bi, 0, 2 * pairs + p)),
            pl.BlockSpec((None, s, LANES), lambda bi, p, i: (bi, 0, 0)),
        ],
        out_specs=pl.BlockSpec((None, t, LANES), lambda bi, p, i: (bi, i, p)),
        out_shape=jax.ShapeDtypeStruct((b, s, pairs * LANES), BF16),
        scratch_shapes=[
            pltpu.VMEM((s, LANES), BF16), pltpu.VMEM((s, LANES), BF16),
            pltpu.VMEM((LANES, s), BF16),
            stat(), stat(), acc(), stat(), stat(), acc(),
        ],
        compiler_params=_params("arbitrary", "arbitrary", "arbitrary"),
        name="forgetting_attention",
    )(z, z, z, fneg)


def _mlstm_kernel(q_ref, k_ref, v_ref, o_ref, g_ref, gb_ref, cw_ref, cb_ref, y_ref,
                  halo_ref, ct_ref, n_ref, m_ref, *, heads, taps):
    ci = pl.program_id(1)
    lc, width = q_ref.shape
    d = width // heads
    pad = halo_ref.shape[0]

    @pl.when(ci == 0)
    def _():
        halo_ref[...] = jnp.zeros(halo_ref.shape, F32)
        ct_ref[...] = jnp.zeros(ct_ref.shape, F32)
        n_ref[...] = jnp.zeros(n_ref.shape, F32)
        m_ref[...] = jnp.zeros(m_ref.shape, F32)

    xqk = jnp.concatenate([q_ref[...].astype(F32), k_ref[...].astype(F32)], axis=-1)
    full = jnp.concatenate([halo_ref[...], xqk], axis=0)
    conv = cb_ref[...]
    for j in range(taps):
        off = pad - (taps - 1) + j
        conv = conv + full[off:off + lc, :] * cw_ref[j:j + 1, :]
    halo_ref[...] = xqk[lc - pad:, :]
    qk = conv * _sigmoid(conv)

    gcol = g_ref[...] + gb_ref[...]
    grow = gcol.T[:2 * heads, :]
    r = lax.broadcasted_iota(jnp.int32, (lc, lc), 0)
    c = lax.broadcasted_iota(jnp.int32, (lc, lc), 1)
    causal = c <= r
    a_cols = _exact_lhs_dot(causal.astype(BF16), _log_sigmoid(gcol))
    a_rows = _exact_rhs_dot(_log_sigmoid(grow), (r <= c).astype(BF16))

    for hd in range(heads):
        q = qk[:, hd * d:(hd + 1) * d]
        k = qk[:, width + hd * d:width + (hd + 1) * d] * (d ** -0.5)
        v = v_ref[:, hd * d:(hd + 1) * d]
        qb = q.astype(BF16)
        kb = k.astype(BF16)
        ig_c = gcol[:, hd:hd + 1]
        ig_r = grow[hd:hd + 1, :]
        a_c = a_cols[:, heads + hd:heads + hd + 1]
        a_r = a_rows[heads + hd:heads + hd + 1, :]
        a_last = a_c[lc - 1:lc, :]
        ct = ct_ref[hd]
        n_prev = n_ref[hd:hd + 1, :]
        m_prev = m_ref[hd:hd + 1, 0:1]

        log_d = jnp.where(causal, a_c - a_r + ig_r, NEG)
        m_t = jnp.maximum(a_c + m_prev, jnp.max(log_d, axis=-1, keepdims=True))
        inter_w = jnp.exp(a_c + m_prev - m_t)
        sqk = _dot_nt(qb, kb) * jnp.exp(log_d - m_t)
        num = inter_w * _dot(qb, ct.astype(BF16)) + _dot(sqk.astype(BF16), v)
        den = (inter_w * jnp.sum(q * n_prev, axis=-1, keepdims=True)
               + jnp.sum(sqk, axis=-1, keepdims=True))
        hval = num / jnp.maximum(jnp.abs(den), jnp.exp(-m_t))
        og = o_ref[:, hd * d:(hd + 1) * d].astype(F32)
        y_ref[:, hd * d:(hd + 1) * d] = (_sigmoid(og) * hval).astype(BF16)

        g_c = a_last - a_c + ig_c
        m_loc = jnp.max(g_c, axis=0, keepdims=True)
        w_c = jnp.exp(g_c - m_loc)
        m_new = jnp.maximum(a_last + m_prev, m_loc)
        decay = jnp.exp(a_last + m_prev - m_new)
        s_loc = jnp.exp(m_loc - m_new)
        wv = (w_c * v.astype(F32)).astype(BF16)
        ct_ref[hd] = decay * ct + s_loc * _dot_tn(kb, wv)
        n_ref[hd:hd + 1, :] = decay * n_prev + s_loc * jnp.sum(w_c * k, axis=0, keepdims=True)
        m_ref[hd:hd + 1, :] = jnp.broadcast_to(m_new, (1, LANES))


def _mlstm_call(z, gates, gate_bias, conv_w, conv_b, heads, col0):
    b, s, _ = z.shape
    width = conv_w.shape[1] // 2
    lc = min(ML_CHUNK, s)
    assert s % lc == 0 and col0 % width == 0 and width // heads == LANES
    cb = col0 // width
    taps = conv_w.shape[0]
    kern = functools.partial(_mlstm_kernel, heads=heads, taps=taps)
    zspec = lambda off: pl.BlockSpec((None, lc, width), lambda bi, ci: (bi, ci, cb + off))
    return pl.pallas_call(
        kern,
        grid=(b, s // lc),
        in_specs=[
            zspec(0), zspec(1), zspec(2), zspec(3),
            pl.BlockSpec((None, lc, LANES), lambda bi, ci: (bi, ci, 0)),
            pl.BlockSpec((1, LANES), lambda bi, ci: (0, 0)),
            pl.BlockSpec((taps, 2 * width), lambda bi, ci: (0, 0)),
            pl.BlockSpec((1, 2 * width), lambda bi, ci: (0, 0)),
        ],
        out_specs=pl.BlockSpec((None, lc, width), lambda bi, ci: (bi, ci, 0)),
        out_shape=jax.ShapeDtypeStruct((b, s, width), BF16),
        scratch_shapes=[
            pltpu.VMEM((8, 2 * width), F32),
            pltpu.VMEM((heads, LANES, LANES), F32),
            pltpu.VMEM((8, LANES), F32),
            pltpu.VMEM((8, LANES), F32),
        ],
        compiler_params=_params("arbitrary", "arbitrary"),
        name="mlstm",
    )(z, z, z, z, gates, gate_bias, conv_w, conv_b)


def _outproj_kernel(*refs, n_in):
    y_refs = refs[:n_in]
    w_refs = refs[n_in:2 * n_in]
    x_ref, gate_ref, o_ref = refs[2 * n_in:]
    acc = _dot(y_refs[0][...], w_refs[0][...])
    for y_ref, w_ref in zip(y_refs[1:], w_refs[1:]):
        acc = acc + _dot(y_ref[...], w_ref[...])
    o_ref[...] = x_ref[...] + gate_ref[...] * acc


def _outproj_call(ys, ws, x, gate, name):
    b, s, d = x.shape
    tm = min(ROW_TILE, s)
    assert s % tm == 0
    n_in = len(ys)
    kern = functools.partial(_outproj_kernel, n_in=n_in)
    y_specs = [pl.BlockSpec((None, tm, y.shape[2]), lambda i, j: (i, j, 0)) for y in ys]
    w_specs = [pl.BlockSpec(w.shape, lambda i, j: (0, 0)) for w in ws]
    return pl.pallas_call(
        kern,
        grid=(b, s // tm),
        in_specs=y_specs + w_specs + [
            pl.BlockSpec((None, tm, d), lambda i, j: (i, j, 0)),
            pl.BlockSpec((None, 1, d), lambda i, j: (i, 0, 0)),
        ],
        out_specs=pl.BlockSpec((None, tm, d), lambda i, j: (i, j, 0)),
        out_shape=jax.ShapeDtypeStruct((b, s, d), F32),
        compiler_params=_params("arbitrary", "arbitrary"),
        name=name,
    )(*ys, *ws, x, gate)


def _ffn_kernel(x_ref, sh_ref, sc_ref, g_ref, gate_ref, w1_ref, w3_ref, w2_ref, o_ref, *, chunk):
    x = x_ref[...]
    h = _norm_modulate(x, g_ref[...], sh_ref[...], sc_ref[...]).astype(BF16)
    hidden = w1_ref.shape[1]
    acc = None
    for c0 in range(0, hidden, chunk):
        a = _dot(h, w1_ref[:, c0:c0 + chunk])
        bb = _dot(h, w3_ref[:, c0:c0 + chunk])
        u = (a * _sigmoid(a) * bb).astype(BF16)
        part = _dot(u, w2_ref[c0:c0 + chunk, :])
        acc = part if acc is None else acc + part
    o_ref[...] = x + gate_ref[...] * acc


def _ffn_call(x, shift, scale, g, gate, w1, w3, w2):
    b, s, d = x.shape
    hidden = w1.shape[1]
    tm = min(ROW_TILE, s)
    assert s % tm == 0 and hidden % FFN_CHUNK == 0
    kern = functools.partial(_ffn_kernel, chunk=FFN_CHUNK)
    vec = pl.BlockSpec((None, 1, d), lambda i, j: (i, 0, 0))
    return pl.pallas_call(
        kern,
        grid=(b, s // tm),
        in_specs=[
            pl.BlockSpec((None, tm, d), lambda i, j: (i, j, 0)),
            vec, vec,
            pl.BlockSpec((1, d), lambda i, j: (0, 0)),
            vec,
            pl.BlockSpec((d, hidden), lambda i, j: (0, 0)),
            pl.BlockSpec((d, hidden), lambda i, j: (0, 0)),
            pl.BlockSpec((hidden, d), lambda i, j: (0, 0)),
        ],
        out_specs=pl.BlockSpec((None, tm, d), lambda i, j: (i, j, 0)),
        out_shape=jax.ShapeDtypeStruct((b, s, d), F32),
        compiler_params=_params("arbitrary", "arbitrary"),
        name="ffn_swiglu",
    )(x, shift, scale, g, gate, w1, w3, w2)


def _pad_cols(w, n):
    return jnp.pad(w, ((0, 0), (0, n - w.shape[1])))


def _gain_row(q_g, k_g, width, total):
    reps = width // HEAD_DIM
    row = jnp.concatenate([jnp.tile(q_g.astype(F32) * (HEAD_DIM ** -0.5 * LOG2E), reps),
                           jnp.tile(k_g.astype(F32), reps),
                           jnp.ones((total - 2 * width,), F32)])
    return row.reshape(1, total)


def kernel(x, c, ada_w, ada_b, norm_mix_g, norm_ffn_g, ab_w_in, ml_b_i, ml_b_f, ml_conv_w, ml_conv_b,
           da_q_g, da_k_g, da_lambda, da_subln_g, ab_w_out, fx_w_in, fx_b_f, fx_q_g, fx_k_g, fx_w_out,
           ffn_w1, ffn_w3, ffn_w2):
    depth = ada_w.shape[0]
    b, s, d = x.shape
    ml_width = ml_conv_w.shape[2] // 2
    ml_heads = ml_b_i.shape[1]
    da_width = (ab_w_in.shape[2] - 4 * ml_width - 2 * ml_heads) // 3
    da_heads = da_width // (2 * HEAD_DIM)
    fx_heads = fx_b_f.shape[1]
    fx_width = fx_heads * HEAD_DIM
    ab_main = 3 * da_width + 4 * ml_width
    fx_main = 3 * fx_width

    mods = _ada_call(c, ada_w, ada_b).reshape(depth, b, 6, 1, d)
    slopes = 2.0 ** (-8.0 * jnp.arange(1, da_heads + 1, dtype=F32) / da_heads)

    for l in range(depth):
        sh_m, sc_m, g_m, sh_f, sc_f, g_f = (mods[l, :, t] for t in range(6))
        j = l // 2
        g_mix = norm_mix_g[l].reshape(1, d)
        if l % 2 == 0:
            w_in = ab_w_in[j]
            z, gates = _inproj_call(
                x, sh_m, sc_m, g_mix,
                w_in[:, :ab_main].astype(BF16),
                _pad_cols(w_in[:, ab_main:], LANES).astype(BF16),
                _gain_row(da_q_g[j], da_k_g[j], da_width, ab_main),
                2 * da_width, "inproj_even")
            lambda_init = 0.8 - 0.6 * math.exp(-0.3 * l)
            y_da = _da_call(z, slopes, da_lambda[j], da_subln_g[j], da_heads, lambda_init)
            gate_bias = _pad_cols(jnp.concatenate([ml_b_i[j], ml_b_f[j]]).reshape(1, -1), LANES)
            y_ml = _mlstm_call(z, gates, gate_bias, ml_conv_w[j], ml_conv_b[j].reshape(1, -1),
                               ml_heads, 3 * da_width)
            w_out = ab_w_out[j].astype(BF16)
            x = _outproj_call([y_da, y_ml], [w_out[:da_width], w_out[da_width:]], x, g_m,
                              "outproj_even")
        else:
            w_in = fx_w_in[j]
            z, gates = _inproj_call(
                x, sh_m, sc_m, g_mix,
                w_in[:, :fx_main].astype(BF16),
                _pad_cols(w_in[:, fx_main:], LANES).astype(BF16),
                _gain_row(fx_q_g[j], fx_k_g[j], fx_width, fx_main),
                2 * fx_width, "inproj_odd")
            fneg = _fox_gate_call(gates, _pad_cols(fx_b_f[j].reshape(1, -1), LANES))
            y = _fox_call(z, fneg, fx_heads)
            x = _outproj_call([y], [fx_w_out[j].astype(BF16)], x, g_m, "outproj_odd")
        x = _ffn_call(x, sh_f, sc_f, norm_ffn_g[l].reshape(1, d), g_f,
                      ffn_w1[l].astype(BF16), ffn_w3[l].astype(BF16), ffn_w2[l].astype(BF16))
    return x
```

```python
import functools
import math

import jax
import jax.numpy as jnp
from jax import lax
from jax.experimental import pallas as pl
from jax.experimental.pallas import tpu as pltpu

F32 = jnp.float32
BF16 = jnp.bfloat16

EPS = 1e-6
LANES = 128
HEAD_DIM = 64
CHUNK = 64
NEG = -1e30
LOG2E = math.log2(math.e)
BIAS_PIECES = 3
ONES_ROWS = 16

VMEM_LIMIT = 56 * 1024 * 1024

ROW_TILE = 512
ATTN_TILE = 256
ATTN_KEYS = 2 * ATTN_TILE
ATTN_BLOCKS = 4
PACK_ROWS = 512
ML_CHUNK = 256
GATE_CHUNK = 256
FFN_CHUNK = 256


def _params(*sem):
    return pltpu.CompilerParams(dimension_semantics=sem, vmem_limit_bytes=VMEM_LIMIT)


def _log_sigmoid(x):
    return jnp.minimum(x, 0.0) - jnp.log1p(jnp.exp(-jnp.abs(x)))


def _sigmoid(x):
    return 1.0 / (1.0 + jnp.exp(-x))


def _split3(x):
    x1 = x.astype(BF16)
    r1 = x - x1.astype(F32)
    x2 = r1.astype(BF16)
    x3 = (r1 - x2.astype(F32)).astype(BF16)
    return x1, x2, x3


def _dot(a, b):
    return jnp.dot(a, b, preferred_element_type=F32)


def _dot_nt(a, b):
    return lax.dot_general(a, b, (((1,), (1,)), ((), ())), preferred_element_type=F32)


def _dot_tn(a, b):
    return lax.dot_general(a, b, (((0,), (0,)), ((), ())), preferred_element_type=F32)


def _exact_lhs_dot(m_bf16, x):
    x1, x2, x3 = _split3(x)
    return _dot(m_bf16, x1) + _dot(m_bf16, x2) + _dot(m_bf16, x3)


def _exact_rhs_dot(x, m_bf16):
    x1, x2, x3 = _split3(x)
    return _dot(x1, m_bf16) + _dot(x2, m_bf16) + _dot(x3, m_bf16)


def _norm_modulate(x, g, sh, sc):
    ms = jnp.mean(x * x, axis=-1, keepdims=True)
    y = x * lax.rsqrt(ms + EPS) * g
    return y * (1.0 + sc) + sh


def _ada_kernel(c_ref, w_ref, b_ref, o_ref):
    c = c_ref[...]
    ca = c * _sigmoid(c)
    c1, c2, c3 = _split3(ca)
    w1, w2, w3 = _split3(w_ref[...])
    acc = _dot(c1, w1) + _dot(c1, w2) + _dot(c2, w1)
    acc = acc + _dot(c1, w3) + _dot(c2, w2) + _dot(c3, w1)
    o_ref[...] = acc + b_ref[...]


def _ada_call(c, ada_w, ada_b):
    depth, d, n = ada_w.shape
    b = c.shape[0]
    tn = 1536
    assert n % tn == 0
    return pl.pallas_call(
        _ada_kernel,
        grid=(depth, n // tn),
        in_specs=[
            pl.BlockSpec((b, d), lambda l, j: (0, 0)),
            pl.BlockSpec((None, d, tn), lambda l, j: (l, 0, j)),
            pl.BlockSpec((None, 1, tn), lambda l, j: (l, 0, j)),
        ],
        out_specs=pl.BlockSpec((None, b, tn), lambda l, j: (l, 0, j)),
        out_shape=jax.ShapeDtypeStruct((depth, b, n), F32),
        compiler_params=_params("arbitrary", "arbitrary"),
        name="ada_mod",
    )(c, ada_w, ada_b.reshape(depth, 1, n))


def _inproj_kernel(x_ref, sh_ref, sc_ref, g_ref, w_ref, wg_ref, gain_ref, z_ref, gate_ref,
                   *, n_norm, chunk):
    h = _norm_modulate(x_ref[...], g_ref[...], sh_ref[...], sc_ref[...]).astype(BF16)
    n = w_ref.shape[1]
    lane = lax.broadcasted_iota(jnp.int32, (1, LANES), 1)
    lo = lane < HEAD_DIM
    for c0 in range(0, n, chunk):
        zc = _dot(h, w_ref[:, c0:c0 + chunk])
        if c0 < n_norm:
            parts = []
            for s0 in range(0, chunk, LANES):
                zs = zc[:, s0:s0 + LANES]
                sq = zs * zs
                s_lo = jnp.sum(jnp.where(lo, sq, 0.0), axis=-1, keepdims=True)
                s_hi = jnp.sum(jnp.where(lo, 0.0, sq), axis=-1, keepdims=True)
                r_lo = lax.rsqrt(s_lo * (1.0 / HEAD_DIM) + EPS)
                r_hi = lax.rsqrt(s_hi * (1.0 / HEAD_DIM) + EPS)
                parts.append(zs * jnp.where(lo, r_lo, r_hi))
            zc = jnp.concatenate(parts, axis=-1) * gain_ref[:, c0:c0 + chunk]
        z_ref[:, c0:c0 + chunk] = zc.astype(BF16)
    gate_ref[...] = _dot(h, wg_ref[...])


def _inproj_call(x, shift, scale, g, w, wg, gain, n_norm, name):
    b, s, d = x.shape
    n = w.shape[1]
    tm = min(ROW_TILE, s)
    chunk = 512
    assert s % tm == 0 and n % chunk == 0 and n_norm % chunk == 0
    kern = functools.partial(_inproj_kernel, n_norm=n_norm, chunk=chunk)
    return pl.pallas_call(
        kern,
        grid=(b, s // tm),
        in_specs=[
            pl.BlockSpec((None, tm, d), lambda i, j: (i, j, 0)),
            pl.BlockSpec((None, 1, d), lambda i, j: (i, 0, 0)),
            pl.BlockSpec((None, 1, d), lambda i, j: (i, 0, 0)),
            pl.BlockSpec((1, d), lambda i, j: (0, 0)),
            pl.BlockSpec((d, n), lambda i, j: (0, 0)),
            pl.BlockSpec((d, LANES), lambda i, j: (0, 0)),
            pl.BlockSpec((1, n), lambda i, j: (0, 0)),
        ],
        out_specs=[
            pl.BlockSpec((None, tm, n), lambda i, j: (i, j, 0)),
            pl.BlockSpec((None, tm, LANES), lambda i, j: (i, j, 0)),
        ],
        out_shape=[
            jax.ShapeDtypeStruct((b, s, n), BF16),
            jax.ShapeDtypeStruct((b, s, LANES), F32),
        ],
        compiler_params=_params("arbitrary", "arbitrary"),
        name=name,
    )(x, shift, scale, g, w, wg, gain)


def _lane_ids():
    return lax.broadcasted_iota(jnp.int32, (1, LANES), 1)


def _place_pieces(pieces, first_lane):
    lane = _lane_ids()
    out = jnp.zeros(pieces[0].shape, F32)
    for j, p in enumerate(pieces):
        out = jnp.where(lane == first_lane + j, p.astype(F32), out)
    return out


def _pack_keys(k_ref, v_ref, extras_fn, ka_ref, kb_ref, vt_refs):
    s = k_ref.shape[0]
    rows = min(PACK_ROWS, s)
    lo = _lane_ids() < HEAD_DIM

    def body(c, carry):
        r0 = pl.multiple_of(c * rows, rows)
        k2 = k_ref[pl.ds(r0, rows), :].astype(F32)
        ea, eb = extras_fn(r0, rows)
        ka_ref[pl.ds(r0, rows), :] = jnp.where(lo, k2, ea).astype(BF16)
        kb_ref[pl.ds(r0, rows), :] = jnp.where(lo, eb, k2).astype(BF16)
        vt = v_ref[pl.ds(r0, rows), :].astype(F32).T.astype(BF16)
        for ref, c0, width in vt_refs:
            ref[0:width, pl.ds(r0, rows)] = vt[c0:c0 + width, :]
            ref[width:width + ONES_ROWS, pl.ds(r0, rows)] = jnp.ones((ONES_ROWS, rows), BF16)
        return carry

    lax.fori_loop(0, s // rows, body, 0)


def _pack_queries(q2):
    lane = _lane_ids()
    q2 = q2.astype(F32)
    ones_a = jnp.where((lane >= HEAD_DIM) & (lane < HEAD_DIM + BIAS_PIECES), 1.0, 0.0)
    ones_b = jnp.where(lane < BIAS_PIECES, 1.0, 0.0)
    lo = lane < HEAD_DIM
    return jnp.where(lo, q2, ones_a).astype(BF16), jnp.where(lo, ones_b, q2).astype(BF16)


def _init_stats(m_ref, acc_ref):
    m_ref[...] = jnp.full(m_ref.shape, NEG, F32)
    acc_ref[...] = jnp.zeros(acc_ref.shape, F32)


def _softmax_step(s, vt, m_ref, acc_ref):
    m_old = m_ref[...]
    m_new = jnp.maximum(m_old, jnp.max(s, axis=0, keepdims=True))
    alpha = jnp.exp2(m_old - m_new)
    p = jnp.exp2(s - m_new)
    acc_ref[...] = alpha * acc_ref[...] + _dot(vt, p.astype(BF16))
    m_ref[...] = m_new


def _attend(i, t, streams):
    tk = ATTN_KEYS

    def tile(k0, size, diag):
        logits = []
        for q, k_ref, _, add, _ in streams:
            s = _dot_nt(k_ref[pl.ds(k0, size), :], q)
            logits.append(s + add if diag else s)
        for s, (_, _, vt, _, stats) in zip(logits, streams):
            _softmax_step(s, vt(k0, size), *stats)

    def below(j, carry):
        tile(pl.multiple_of(j * tk, tk), tk, False)
        return carry

    n_wide = (i * t) // tk
    lax.fori_loop(0, n_wide, below, 0)
    for r in range(tk // t - 1):
        @pl.when(i * t - n_wide * tk > r * t)
        def _():
            tile(pl.multiple_of(n_wide * tk + r * t, t), t, False)
    tile(pl.multiple_of(i * t, t), t, True)


def _lane_block(ref, h):
    return ref.at[:, pl.ds(h * LANES, LANES)]


def _da_kernel(slope_ref, q_ref, k_ref, v_ref, lam_ref, sg_ref, o_ref,
               ka_ref, kb_ref, vt_ref, m_ref, acc_ref, *, t, hb, lambda_init):
    dv = LANES
    g = pl.program_id(1)
    i = pl.program_id(2)
    slopes2 = [slope_ref[g * hb + h] * LOG2E for h in range(hb)]

    @pl.when(i == 0)
    def _():
        for h in range(hb):
            def extras(r0, rows, slope2=slopes2[h]):
                pos = (lax.broadcasted_iota(jnp.int32, (rows, LANES), 0) + r0).astype(F32) * slope2
                pieces = _split3(pos)
                return _place_pieces(pieces, HEAD_DIM), _place_pieces(pieces, 0)
            _pack_keys(_lane_block(k_ref, h), _lane_block(v_ref, h), extras,
                       ka_ref.at[h], kb_ref.at[h], [(vt_ref.at[h], 0, dv)])

    key = lax.broadcasted_iota(jnp.int32, (t, t), 0)
    qry = lax.broadcasted_iota(jnp.int32, (t, t), 1)
    allowed = (key // CHUNK) <= (qry // CHUNK)
    above = jnp.maximum(key - qry, 0).astype(F32)

    streams = []
    for h in range(hb):
        qa, qb = _pack_queries(q_ref[:, h * LANES:(h + 1) * LANES])
        diag_add = jnp.where(allowed, (-2.0 * slopes2[h]) * above, NEG)
        vt = lambda k0, size, h=h: vt_ref[h, :, pl.ds(k0, size)]
        for m, (q, k_sc) in enumerate(((qa, ka_ref), (qb, kb_ref))):
            stats = (m_ref.at[2 * h + m], acc_ref.at[2 * h + m])
            _init_stats(*stats)
            streams.append((q, k_sc.at[h], vt, diag_add, stats))
    _attend(i, t, streams)

    lp = lam_ref[...]
    lam = (jnp.exp(jnp.sum(lp[0:1] * lp[1:2], axis=-1, keepdims=True))
           - jnp.exp(jnp.sum(lp[2:3] * lp[3:4], axis=-1, keepdims=True)) + lambda_init)
    for h in range(hb):
        pa = acc_ref[2 * h]
        pb = acc_ref[2 * h + 1]
        out = pa[0:dv] / pa[dv:dv + 1] - lam * (pb[0:dv] / pb[dv:dv + 1])
        ms = jnp.mean(out * out, axis=0, keepdims=True)
        out = out * lax.rsqrt(ms + EPS) * (sg_ref[...] * (1.0 - lambda_init))
        o_ref[:, h * LANES:(h + 1) * LANES] = out.T.astype(BF16)


def _da_call(z, slopes, lam_p, subln_g, heads, lambda_init):
    b, s, _ = z.shape
    t = min(ATTN_TILE, s)
    assert s % ATTN_KEYS == 0 and t % CHUNK == 0
    hb = min(ATTN_BLOCKS, heads)
    groups = heads // hb
    assert heads % hb == 0
    kern = functools.partial(_da_kernel, t=t, hb=hb, lambda_init=lambda_init)
    w = hb * LANES
    return pl.pallas_call(
        kern,
        grid_spec=pltpu.PrefetchScalarGridSpec(
            num_scalar_prefetch=1,
            grid=(b, groups, s // t),
            in_specs=[
                pl.BlockSpec((None, t, w), lambda bi, g, i, sl: (bi, i, g)),
                pl.BlockSpec((None, s, w), lambda bi, g, i, sl: (bi, 0, groups + g)),
                pl.BlockSpec((None, s, w), lambda bi, g, i, sl: (bi, 0, 2 * groups + g)),
                pl.BlockSpec((4, HEAD_DIM), lambda bi, g, i, sl: (0, 0)),
                pl.BlockSpec((LANES, 1), lambda bi, g, i, sl: (0, 0)),
            ],
            out_specs=pl.BlockSpec((None, t, w), lambda bi, g, i, sl: (bi, i, g)),
            scratch_shapes=[
                pltpu.VMEM((hb, s, LANES), BF16), pltpu.VMEM((hb, s, LANES), BF16),
                pltpu.VMEM((hb, LANES + ONES_ROWS, s), BF16),
                pltpu.VMEM((2 * hb, 1, t), F32),
                pltpu.VMEM((2 * hb, LANES + ONES_ROWS, t), F32),
            ],
        ),
        out_shape=jax.ShapeDtypeStruct((b, s, heads * LANES), BF16),
        compiler_params=_params("arbitrary", "arbitrary", "arbitrary"),
        name="diff_attention",
    )(slopes, z, z, z, lam_p, subln_g.reshape(LANES, 1))


def _fox_gate_kernel(g_ref, b_ref, o_ref, *, chunk):
    s = g_ref.shape[0]
    r = lax.broadcasted_iota(jnp.int32, (chunk, chunk), 0)
    c = lax.broadcasted_iota(jnp.int32, (chunk, chunk), 1)
    tri = (c <= r).astype(BF16)
    carry = jnp.zeros((1, LANES), F32)
    for c0 in range(0, s, chunk):
        ls = _log_sigmoid(g_ref[c0:c0 + chunk, :] + b_ref[...])
        cs = _exact_lhs_dot(tri, ls) + carry
        carry = cs[chunk - 1:chunk, :]
        o_ref[c0:c0 + chunk, :] = cs * (-LOG2E)


def _fox_gate_call(gates, bias_row):
    b, s, _ = gates.shape
    chunk = min(GATE_CHUNK, s)
    assert s % chunk == 0
    kern = functools.partial(_fox_gate_kernel, chunk=chunk)
    return pl.pallas_call(
        kern,
        grid=(b,),
        in_specs=[
            pl.BlockSpec((None, s, LANES), lambda i: (i, 0, 0)),
            pl.BlockSpec((1, LANES), lambda i: (0, 0)),
        ],
        out_specs=pl.BlockSpec((None, s, LANES), lambda i: (i, 0, 0)),
        out_shape=jax.ShapeDtypeStruct((b, s, LANES), F32),
        compiler_params=_params("arbitrary"),
        name="fox_gate_cumsum",
    )(gates, bias_row)


def _fox_kernel(q_ref, k_ref, v_ref, f_ref, o_ref,
                ka_ref, kb_ref, vt_ref, m_ref, acc_ref, *, t, hb):
    dv = HEAD_DIM
    g = pl.program_id(1)
    i = pl.program_id(2)

    @pl.when(i == 0)
    def _():
        n = BIAS_PIECES * LANES
        src = lax.broadcasted_iota(jnp.int32, (n, LANES), 0)
        dst = lax.broadcasted_iota(jnp.int32, (n, LANES), 1)
        piece = src // LANES
        head = src - piece * LANES
        for h in range(hb):
            pair = g * hb + h
            place_a = ((head == 2 * pair) & (dst == HEAD_DIM + piece)).astype(BF16)
            place_b = ((head == 2 * pair + 1) & (dst == piece)).astype(BF16)

            def extras(r0, rows, place_a=place_a, place_b=place_b):
                pieces = jnp.concatenate(_split3(f_ref[pl.ds(r0, rows), :]), axis=-1)
                return _dot(pieces, place_a), _dot(pieces, place_b)
            _pack_keys(_lane_block(k_ref, h), _lane_block(v_ref, h), extras,
                       ka_ref.at[h], kb_ref.at[h],
                       [(vt_ref.at[2 * h + m], m * dv, dv) for m in range(2)])

    key = lax.broadcasted_iota(jnp.int32, (t, t), 0)
    qry = lax.broadcasted_iota(jnp.int32, (t, t), 1)
    diag_add = jnp.where(key <= qry, 0.0, NEG)

    streams = []
    for h in range(hb):
        qa, qb = _pack_queries(q_ref[:, h * LANES:(h + 1) * LANES])
        for m, (q, k_sc) in enumerate(((qa, ka_ref), (qb, kb_ref))):
            vt = lambda k0, size, n=2 * h + m: vt_ref[n, :, pl.ds(k0, size)]
            stats = (m_ref.at[2 * h + m], acc_ref.at[2 * h + m])
            _init_stats(*stats)
            streams.append((q, k_sc.at[h], vt, diag_add, stats))
    _attend(i, t, streams)

    for h in range(hb):
        pa = acc_ref[2 * h]
        pb = acc_ref[2 * h + 1]
        out = jnp.concatenate([pa[0:dv] / pa[dv:dv + 1], pb[0:dv] / pb[dv:dv + 1]], axis=0)
        o_ref[:, h * LANES:(h + 1) * LANES] = out.T.astype(BF16)


def _fox_call(z, fneg, heads):
    b, s, _ = z.shape
    pairs = heads // 2
    t = min(ATTN_TILE, s)
    assert s % ATTN_KEYS == 0
    hb = min(ATTN_BLOCKS, pairs)
    groups = pairs // hb
    assert pairs % hb == 0
    kern = functools.partial(_fox_kernel, t=t, hb=hb)
    w = hb * LANES
    return pl.pallas_call(
        kern,
        grid=(b, groups, s // t),
        in_specs=[
            pl.BlockSpec((None, t, w), lambda bi, g, i: (bi, i, g)),
            pl.BlockSpec((None, s, w), lambda bi, g, i: (bi, 0, groups + g)),
            pl.BlockSpec((None, s, w), lambda bi, g, i: (bi, 0, 2 * groups + g)),
            pl.BlockSpec((None, s, LANES), lambda bi, g, i: (bi, 0, 0)),
        ],
        out_specs=pl.BlockSpec((None, t, w), lambda bi, g, i: (bi, i, g)),
        out_shape=jax.ShapeDtypeStruct((b, s, pairs * LANES), BF16),
        scratch_shapes=[
            pltpu.VMEM((hb, s, LANES), BF16), pltpu.VMEM((hb, s, LANES), BF16),
            pltpu.VMEM((2 * hb, HEAD_DIM + ONES_ROWS, s), BF16),
            pltpu.VMEM((2 * hb, 1, t), F32),
            pltpu.VMEM((2 * hb, HEAD_DIM + ONES_ROWS, t), F32),
        ],
        compiler_params=_params("arbitrary", "arbitrary", "arbitrary"),
        name="forgetting_attention",
    )(z, z, z, fneg)


def _mlstm_kernel(q_ref, k_ref, v_ref, o_ref, g_ref, gb_ref, cw_ref, cb_ref, y_ref,
                  halo_ref, ct_ref, n_ref, m_ref, *, heads, taps):
    ci = pl.program_id(1)
    lc, width = q_ref.shape
    d = width // heads
    pad = halo_ref.shape[0]

    @pl.when(ci == 0)
    def _():
        halo_ref[...] = jnp.zeros(halo_ref.shape, F32)
        ct_ref[...] = jnp.zeros(ct_ref.shape, F32)
        n_ref[...] = jnp.zeros(n_ref.shape, F32)
        m_ref[...] = jnp.zeros(m_ref.shape, F32)

    xqk = jnp.concatenate([q_ref[...].astype(F32), k_ref[...].astype(F32)], axis=-1)
    full = jnp.concatenate([halo_ref[...], xqk], axis=0)
    conv = cb_ref[...]
    for j in range(taps):
        off = pad - (taps - 1) + j
        conv = conv + full[off:off + lc, :] * cw_ref[j:j + 1, :]
    halo_ref[...] = xqk[lc - pad:, :]
    qk = conv * _sigmoid(conv)

    gcol = g_ref[...] + gb_ref[...]
    grow = gcol.T[:2 * heads, :]
    r = lax.broadcasted_iota(jnp.int32, (lc, lc), 0)
    c = lax.broadcasted_iota(jnp.int32, (lc, lc), 1)
    causal = c <= r
    a_cols = _exact_lhs_dot(causal.astype(BF16), _log_sigmoid(gcol))
    a_rows = _exact_rhs_dot(_log_sigmoid(grow), (r <= c).astype(BF16))

    for hd in range(heads):
        q = qk[:, hd * d:(hd + 1) * d]
        k = qk[:, width + hd * d:width + (hd + 1) * d] * (d ** -0.5)
        v = v_ref[:, hd * d:(hd + 1) * d]
        qb = q.astype(BF16)
        kb = k.astype(BF16)
        ig_c = gcol[:, hd:hd + 1]
        ig_r = grow[hd:hd + 1, :]
        a_c = a_cols[:, heads + hd:heads + hd + 1]
        a_r = a_rows[heads + hd:heads + hd + 1, :]
        a_last = a_c[lc - 1:lc, :]
        ct = ct_ref[hd]
        n_prev = n_ref[hd:hd + 1, :]
        m_prev = m_ref[hd:hd + 1, 0:1]

        log_d = jnp.where(causal, a_c - a_r + ig_r, NEG)
        m_t = jnp.maximum(a_c + m_prev, jnp.max(log_d, axis=-1, keepdims=True))
        inter_w = jnp.exp(a_c + m_prev - m_t)
        sqk = _dot_nt(qb, kb) * jnp.exp(log_d - m_t)
        num = inter_w * _dot(qb, ct.astype(BF16)) + _dot(sqk.astype(BF16), v)
        den = (inter_w * jnp.sum(q * n_prev, axis=-1, keepdims=True)
               + jnp.sum(sqk, axis=-1, keepdims=True))
        hval = num / jnp.maximum(jnp.abs(den), jnp.exp(-m_t))
        og = o_ref[:, hd * d:(hd + 1) * d].astype(F32)
        y_ref[:, hd * d:(hd + 1) * d] = (_sigmoid(og) * hval).astype(BF16)

        g_c = a_last - a_c + ig_c
        m_loc = jnp.max(g_c, axis=0, keepdims=True)
        w_c = jnp.exp(g_c - m_loc)
        m_new = jnp.maximum(a_last + m_prev, m_loc)
        decay = jnp.exp(a_last + m_prev - m_new)
        s_loc = jnp.exp(m_loc - m_new)
        wv = (w_c * v.astype(F32)).astype(BF16)
        ct_ref[hd] = decay * ct + s_loc * _dot_tn(kb, wv)
        n_ref[hd:hd + 1, :] = decay * n_prev + s_loc * jnp.sum(w_c * k, axis=0, keepdims=True)
        m_ref[hd:hd + 1, :] = jnp.broadcast_to(m_new, (1, LANES))


def _mlstm_call(z, gates, gate_bias, conv_w, conv_b, heads, col0):
    b, s, _ = z.shape
    width = conv_w.shape[1] // 2
    lc = min(ML_CHUNK, s)
    assert s % lc == 0 and col0 % width == 0 and width // heads == LANES
    cb = col0 // width
    taps = conv_w.shape[0]
    kern = functools.partial(_mlstm_kernel, heads=heads, taps=taps)
    zspec = lambda off: pl.BlockSpec((None, lc, width), lambda bi, ci: (bi, ci, cb + off))
    return pl.pallas_call(
        kern,
        grid=(b, s // lc),
        in_specs=[
            zspec(0), zspec(1), zspec(2), zspec(3),
            pl.BlockSpec((None, lc, LANES), lambda bi, ci: (bi, ci, 0)),
            pl.BlockSpec((1, LANES), lambda bi, ci: (0, 0)),
            pl.BlockSpec((taps, 2 * width), lambda bi, ci: (0, 0)),
            pl.BlockSpec((1, 2 * width), lambda bi, ci: (0, 0)),
        ],
        out_specs=pl.BlockSpec((None, lc, width), lambda bi, ci: (bi, ci, 0)),
        out_shape=jax.ShapeDtypeStruct((b, s, width), BF16),
        scratch_shapes=[
            pltpu.VMEM((8, 2 * width), F32),
            pltpu.VMEM((heads, LANES, LANES), F32),
            pltpu.VMEM((8, LANES), F32),
            pltpu.VMEM((8, LANES), F32),
        ],
        compiler_params=_params("arbitrary", "arbitrary"),
        name="mlstm",
    )(z, z, z, z, gates, gate_bias, conv_w, conv_b)


def _outproj_kernel(*refs, n_in):
    y_refs = refs[:n_in]
    w_refs = refs[n_in:2 * n_in]
    x_ref, gate_ref, o_ref = refs[2 * n_in:]
    acc = _dot(y_refs[0][...], w_refs[0][...])
    for y_ref, w_ref in zip(y_refs[1:], w_refs[1:]):
        acc = acc + _dot(y_ref[...], w_ref[...])
    o_ref[...] = x_ref[...] + gate_ref[...] * acc


def _outproj_call(ys, ws, x, gate, name):
    b, s, d = x.shape
    tm = min(ROW_TILE, s)
    assert s % tm == 0
    n_in = len(ys)
    kern = functools.partial(_outproj_kernel, n_in=n_in)
    y_specs = [pl.BlockSpec((None, tm, y.shape[2]), lambda i, j: (i, j, 0)) for y in ys]
    w_specs = [pl.BlockSpec(w.shape, lambda i, j: (0, 0)) for w in ws]
    return pl.pallas_call(
        kern,
        grid=(b, s // tm),
        in_specs=y_specs + w_specs + [
            pl.BlockSpec((None, tm, d), lambda i, j: (i, j, 0)),
            pl.BlockSpec((None, 1, d), lambda i, j: (i, 0, 0)),
        ],
        out_specs=pl.BlockSpec((None, tm, d), lambda i, j: (i, j, 0)),
        out_shape=jax.ShapeDtypeStruct((b, s, d), F32),
        compiler_params=_params("arbitrary", "arbitrary"),
        name=name,
    )(*ys, *ws, x, gate)


def _ffn_kernel(x_ref, sh_ref, sc_ref, g_ref, gate_ref, w1_ref, w3_ref, w2_ref, o_ref, *, chunk):
    x = x_ref[...]
    h = _norm_modulate(x, g_ref[...], sh_ref[...], sc_ref[...]).astype(BF16)
    hidden = w1_ref.shape[1]
    acc = None
    for c0 in range(0, hidden, chunk):
        a = _dot(h, w1_ref[:, c0:c0 + chunk])
        bb = _dot(h, w3_ref[:, c0:c0 + chunk])
        u = (a * _sigmoid(a) * bb).astype(BF16)
        part = _dot(u, w2_ref[c0:c0 + chunk, :])
        acc = part if acc is None else acc + part
    o_ref[...] = x + gate_ref[...] * acc


def _ffn_call(x, shift, scale, g, gate, w1, w3, w2):
    b, s, d = x.shape
    hidden = w1.shape[1]
    tm = min(ROW_TILE, s)
    assert s % tm == 0 and hidden % FFN_CHUNK == 0
    kern = functools.partial(_ffn_kernel, chunk=FFN_CHUNK)
    vec = pl.BlockSpec((None, 1, d), lambda i, j: (i, 0, 0))
    return pl.pallas_call(
        kern,
        grid=(b, s // tm),
        in_specs=[
            pl.BlockSpec((None, tm, d), lambda i, j: (i, j, 0)),
            vec, vec,
            pl.BlockSpec((1, d), lambda i, j: (0, 0)),
            vec,
            pl.BlockSpec((d, hidden), lambda i, j: (0, 0)),
            pl.BlockSpec((d, hidden), lambda i, j: (0, 0)),
            pl.BlockSpec((hidden, d), lambda i, j: (0, 0)),
        ],
        out_specs=pl.BlockSpec((None, tm, d), lambda i, j: (i, j, 0)),
        out_shape=jax.ShapeDtypeStruct((b, s, d), F32),
        compiler_params=_params("arbitrary", "arbitrary"),
        name="ffn_swiglu",
    )(x, shift, scale, g, gate, w1, w3, w2)


def _pad_cols(w, n):
    return jnp.pad(w, ((0, 0), (0, n - w.shape[1])))


def _gain_row(q_g, k_g, width, total):
    reps = width // HEAD_DIM
    row = jnp.concatenate([jnp.tile(q_g.astype(F32) * (HEAD_DIM ** -0.5 * LOG2E), reps),
                           jnp.tile(k_g.astype(F32), reps),
                           jnp.ones((total - 2 * width,), F32)])
    return row.reshape(1, total)


def kernel(x, c, ada_w, ada_b, norm_mix_g, norm_ffn_g, ab_w_in, ml_b_i, ml_b_f, ml_conv_w, ml_conv_b,
           da_q_g, da_k_g, da_lambda, da_subln_g, ab_w_out, fx_w_in, fx_b_f, fx_q_g, fx_k_g, fx_w_out,
           ffn_w1, ffn_w3, ffn_w2):
    depth = ada_w.shape[0]
    b, s, d = x.shape
    ml_width = ml_conv_w.shape[2] // 2
    ml_heads = ml_b_i.shape[1]
    da_width = (ab_w_in.shape[2] - 4 * ml_width - 2 * ml_heads) // 3
    da_heads = da_width // (2 * HEAD_DIM)
    fx_heads = fx_b_f.shape[1]
    fx_width = fx_heads * HEAD_DIM
    ab_main = 3 * da_width + 4 * ml_width
    fx_main = 3 * fx_width

    mods = _ada_call(c, ada_w, ada_b).reshape(depth, b, 6, 1, d)
    slopes = 2.0 ** (-8.0 * jnp.arange(1, da_heads + 1, dtype=F32) / da_heads)

    for l in range(depth):
        sh_m, sc_m, g_m, sh_f, sc_f, g_f = (mods[l, :, t] for t in range(6))
        j = l // 2
        g_mix = norm_mix_g[l].reshape(1, d)
        if l % 2 == 0:
            w_in = ab_w_in[j]
            z, gates = _inproj_call(
                x, sh_m, sc_m, g_mix,
                w_in[:, :ab_main].astype(BF16),
                _pad_cols(w_in[:, ab_main:], LANES).astype(BF16),
                _gain_row(da_q_g[j], da_k_g[j], da_width, ab_main),
                2 * da_width, "inproj_even")
            lambda_init = 0.8 - 0.6 * math.exp(-0.3 * l)
            y_da = _da_call(z, slopes, da_lambda[j], da_subln_g[j], da_heads, lambda_init)
            gate_bias = _pad_cols(jnp.concatenate([ml_b_i[j], ml_b_f[j]]).reshape(1, -1), LANES)
            y_ml = _mlstm_call(z, gates, gate_bias, ml_conv_w[j], ml_conv_b[j].reshape(1, -1),
                               ml_heads, 3 * da_width)
            w_out = ab_w_out[j].astype(BF16)
            x = _outproj_call([y_da, y_ml], [w_out[:da_width], w_out[da_width:]], x, g_m,
                              "outproj_even")
        else:
            w_in = fx_w_in[j]
            z, gates = _inproj_call(
                x, sh_m, sc_m, g_mix,
                w_in[:, :fx_main].astype(BF16),
                _pad_cols(w_in[:, fx_main:], LANES).astype(BF16),
                _gain_row(fx_q_g[j], fx_k_g[j], fx_width, fx_main),
                2 * fx_width, "inproj_odd")
            fneg = _fox_gate_call(gates, _pad_cols(fx_b_f[j].reshape(1, -1), LANES))
            y = _fox_call(z, fneg, fx_heads)
            x = _outproj_call([y], [fx_w_out[j].astype(BF16)], x, g_m, "outproj_odd")
        x = _ffn_call(x, sh_f, sc_f, norm_ffn_g[l].reshape(1, d), g_f,
                      ffn_w1[l].astype(BF16), ffn_w3[l].astype(BF16), ffn_w2[l].astype(BF16))
    return x
```

```python
import functools
import math

import jax
import jax.numpy as jnp
from jax import lax
from jax.experimental import pallas as pl
from jax.experimental.pallas import tpu as pltpu

F32 = jnp.float32
BF16 = jnp.bfloat16

EPS = 1e-6
LANES = 128
HEAD_DIM = 64
CHUNK = 64
NEG = -1e30
LOG2E = math.log2(math.e)
BIAS_PIECES = 3
ONES_ROWS = 16

VMEM_LIMIT = 56 * 1024 * 1024

ROW_TILE = 512
ATTN_TILE = 256
ATTN_KEYS = 2 * ATTN_TILE
ATTN_BLOCKS = 4
PACK_ROWS = 512
ML_CHUNK = 256
GATE_CHUNK = 256
FFN_CHUNK = 256


def _params(*sem):
    return pltpu.CompilerParams(dimension_semantics=sem, vmem_limit_bytes=VMEM_LIMIT)


def _log_sigmoid(x):
    return jnp.minimum(x, 0.0) - jnp.log1p(jnp.exp(-jnp.abs(x)))


def _sigmoid(x):
    return 1.0 / (1.0 + jnp.exp(-x))


def _split3(x):
    x1 = x.astype(BF16)
    r1 = x - x1.astype(F32)
    x2 = r1.astype(BF16)
    x3 = (r1 - x2.astype(F32)).astype(BF16)
    return x1, x2, x3


def _dot(a, b):
    return jnp.dot(a, b, preferred_element_type=F32)


def _dot_nt(a, b):
    return lax.dot_general(a, b, (((1,), (1,)), ((), ())), preferred_element_type=F32)


def _dot_tn(a, b):
    return lax.dot_general(a, b, (((0,), (0,)), ((), ())), preferred_element_type=F32)


def _exact_lhs_dot(m_bf16, x):
    x1, x2, x3 = _split3(x)
    return _dot(m_bf16, x1) + _dot(m_bf16, x2) + _dot(m_bf16, x3)


def _exact_rhs_dot(x, m_bf16):
    x1, x2, x3 = _split3(x)
    return _dot(x1, m_bf16) + _dot(x2, m_bf16) + _dot(x3, m_bf16)


def _norm_modulate(x, g, sh, sc):
    ms = jnp.mean(x * x, axis=-1, keepdims=True)
    y = x * lax.rsqrt(ms + EPS) * g
    return y * (1.0 + sc) + sh


def _ada_kernel(c_ref, w_ref, b_ref, o_ref):
    c = c_ref[...]
    ca = c * _sigmoid(c)
    c1, c2, c3 = _split3(ca)
    w1, w2, w3 = _split3(w_ref[...])
    acc = _dot(c1, w1) + _dot(c1, w2) + _dot(c2, w1)
    acc = acc + _dot(c1, w3) + _dot(c2, w2) + _dot(c3, w1)
    o_ref[...] = acc + b_ref[...]


def _ada_call(c, ada_w, ada_b):
    depth, d, n = ada_w.shape
    b = c.shape[0]
    tn = 1536
    assert n % tn == 0
    return pl.pallas_call(
        _ada_kernel,
        grid=(depth, n // tn),
        in_specs=[
            pl.BlockSpec((b, d), lambda l, j: (0, 0)),
            pl.BlockSpec((None, d, tn), lambda l, j: (l, 0, j)),
            pl.BlockSpec((None, 1, tn), lambda l, j: (l, 0, j)),
        ],
        out_specs=pl.BlockSpec((None, b, tn), lambda l, j: (l, 0, j)),
        out_shape=jax.ShapeDtypeStruct((depth, b, n), F32),
        compiler_params=_params("arbitrary", "arbitrary"),
        name="ada_mod",
    )(c, ada_w, ada_b.reshape(depth, 1, n))


def _inproj_kernel(x_ref, sh_ref, sc_ref, g_ref, w_ref, wg_ref, gain_ref, z_ref, gate_ref,
                   *, n_norm, chunk):
    h = _norm_modulate(x_ref[...], g_ref[...], sh_ref[...], sc_ref[...]).astype(BF16)
    n = w_ref.shape[1]
    lane = lax.broadcasted_iota(jnp.int32, (1, LANES), 1)
    lo = lane < HEAD_DIM
    for c0 in range(0, n, chunk):
        zc = _dot(h, w_ref[:, c0:c0 + chunk])
        if c0 < n_norm:
            parts = []
            for s0 in range(0, chunk, LANES):
                zs = zc[:, s0:s0 + LANES]
                sq = zs * zs
                s_lo = jnp.sum(jnp.where(lo, sq, 0.0), axis=-1, keepdims=True)
                s_hi = jnp.sum(jnp.where(lo, 0.0, sq), axis=-1, keepdims=True)
                r_lo = lax.rsqrt(s_lo * (1.0 / HEAD_DIM) + EPS)
                r_hi = lax.rsqrt(s_hi * (1.0 / HEAD_DIM) + EPS)
                parts.append(zs * jnp.where(lo, r_lo, r_hi))
            zc = jnp.concatenate(parts, axis=-1) * gain_ref[:, c0:c0 + chunk]
        z_ref[:, c0:c0 + chunk] = zc.astype(BF16)
    gate_ref[...] = _dot(h, wg_ref[...])


def _inproj_call(x, shift, scale, g, w, wg, gain, n_norm, name):
    b, s, d = x.shape
    n = w.shape[1]
    tm = min(ROW_TILE, s)
    chunk = 512
    assert s % tm == 0 and n % chunk == 0 and n_norm % chunk == 0
    kern = functools.partial(_inproj_kernel, n_norm=n_norm, chunk=chunk)
    return pl.pallas_call(
        kern,
        grid=(b, s // tm),
        in_specs=[
            pl.BlockSpec((None, tm, d), lambda i, j: (i, j, 0)),
            pl.BlockSpec((None, 1, d), lambda i, j: (i, 0, 0)),
            pl.BlockSpec((None, 1, d), lambda i, j: (i, 0, 0)),
            pl.BlockSpec((1, d), lambda i, j: (0, 0)),
            pl.BlockSpec((d, n), lambda i, j: (0, 0)),
            pl.BlockSpec((d, LANES), lambda i, j: (0, 0)),
            pl.BlockSpec((1, n), lambda i, j: (0, 0)),
        ],
        out_specs=[
            pl.BlockSpec((None, tm, n), lambda i, j: (i, j, 0)),
            pl.BlockSpec((None, tm, LANES), lambda i, j: (i, j, 0)),
        ],
        out_shape=[
            jax.ShapeDtypeStruct((b, s, n), BF16),
            jax.ShapeDtypeStruct((b, s, LANES), F32),
        ],
        compiler_params=_params("arbitrary", "arbitrary"),
        name=name,
    )(x, shift, scale, g, w, wg, gain)


def _lane_ids():
    return lax.broadcasted_iota(jnp.int32, (1, LANES), 1)


def _place_pieces(pieces, first_lane):
    lane = _lane_ids()
    out = jnp.zeros(pieces[0].shape, F32)
    for j, p in enumerate(pieces):
        out = jnp.where(lane == first_lane + j, p.astype(F32), out)
    return out


def _pack_keys(k_ref, v_ref, extras_fn, ka_ref, kb_ref, vt_refs):
    s = k_ref.shape[0]
    rows = min(PACK_ROWS, s)
    lo = _lane_ids() < HEAD_DIM

    def body(c, carry):
        r0 = pl.multiple_of(c * rows, rows)
        k2 = k_ref[pl.ds(r0, rows), :].astype(F32)
        ea, eb = extras_fn(r0, rows)
        ka_ref[pl.ds(r0, rows), :] = jnp.where(lo, k2, ea).astype(BF16)
        kb_ref[pl.ds(r0, rows), :] = jnp.where(lo, eb, k2).astype(BF16)
        vt = v_ref[pl.ds(r0, rows), :].astype(F32).T.astype(BF16)
        for ref, c0, width in vt_refs:
            ref[0:width, pl.ds(r0, rows)] = vt[c0:c0 + width, :]
            ref[width:width + ONES_ROWS, pl.ds(r0, rows)] = jnp.ones((ONES_ROWS, rows), BF16)
        return carry

    lax.fori_loop(0, s // rows, body, 0)


def _pack_queries(q2):
    lane = _lane_ids()
    q2 = q2.astype(F32)
    ones_a = jnp.where((lane >= HEAD_DIM) & (lane < HEAD_DIM + BIAS_PIECES), 1.0, 0.0)
    ones_b = jnp.where(lane < BIAS_PIECES, 1.0, 0.0)
    lo = lane < HEAD_DIM
    return jnp.where(lo, q2, ones_a).astype(BF16), jnp.where(lo, ones_b, q2).astype(BF16)


def _init_stats(m_ref, acc_ref):
    m_ref[...] = jnp.full(m_ref.shape, NEG, F32)
    acc_ref[...] = jnp.zeros(acc_ref.shape, F32)


def _softmax_step(s, tile_max, vt, m_ref, acc_ref):
    m_old = m_ref[...]
    m_new = jnp.maximum(m_old, tile_max)
    alpha = jnp.exp2(m_old - m_new)
    p = jnp.exp2(s - m_new)
    acc_ref[...] = alpha * acc_ref[...] + _dot(vt, p.astype(BF16))
    m_ref[...] = m_new


def _attend(i, t, streams, s_buf, mx_buf):
    tk = ATTN_KEYS
    n = (i * t + t + tk - 1) // tk

    def qk_stage(step, slot):
        k0 = pl.multiple_of(step * tk, tk)
        for idx, (q, k_ref, _, _, _) in enumerate(streams):
            s = _dot_nt(k_ref[pl.ds(k0, tk), :], q)
            s_buf[slot][idx] = s
            mx_buf[slot][idx] = jnp.max(s, axis=0, keepdims=True)

    def softmax_stage(step, slot, last):
        k0 = pl.multiple_of(step * tk, tk)
        for idx, (_, _, vt, add, stats) in enumerate(streams):
            s = s_buf[slot][idx]
            if last:
                s = s + add
                tile_max = jnp.max(s, axis=0, keepdims=True)
            else:
                tile_max = mx_buf[slot][idx]
            _softmax_step(s, tile_max, vt(k0, tk), *stats)

    qk_stage(0, 0)
    pairs = (n - 1) // 2

    def body(p, carry):
        qk_stage(2 * p + 1, 1)
        softmax_stage(2 * p, 0, False)
        qk_stage(2 * p + 2, 0)
        softmax_stage(2 * p + 1, 1, False)
        return carry

    lax.fori_loop(0, pairs, body, 0)

    @pl.when(n - 1 > 2 * pairs)
    def _():
        qk_stage(n - 1, 1)
        softmax_stage(n - 2, 0, False)
        softmax_stage(n - 1, 1, True)

    @pl.when(n - 1 == 2 * pairs)
    def _():
        softmax_stage(n - 1, 0, True)


def _last_step_offset(i, t):
    tk = ATTN_KEYS
    n = (i * t + t + tk - 1) // tk
    return i * t - (n - 1) * tk


def _lane_block(ref, h):
    return ref.at[:, pl.ds(h * LANES, LANES)]


def _da_kernel(slope_ref, q_ref, k_ref, v_ref, lam_ref, sg_ref, o_ref,
               ka_ref, kb_ref, vt_ref, m_ref, acc_ref, s0_ref, s1_ref, mx0_ref, mx1_ref,
               *, t, hb, lambda_init):
    dv = LANES
    g = pl.program_id(1)
    i = pl.program_id(2)
    slopes2 = [slope_ref[g * hb + h] * LOG2E for h in range(hb)]

    @pl.when(i == 0)
    def _():
        for h in range(hb):
            def extras(r0, rows, slope2=slopes2[h]):
                pos = (lax.broadcasted_iota(jnp.int32, (rows, LANES), 0) + r0).astype(F32) * slope2
                pieces = _split3(pos)
                return _place_pieces(pieces, HEAD_DIM), _place_pieces(pieces, 0)
            _pack_keys(_lane_block(k_ref, h), _lane_block(v_ref, h), extras,
                       ka_ref.at[h], kb_ref.at[h], [(vt_ref.at[h], 0, dv)])

    key = lax.broadcasted_iota(jnp.int32, (ATTN_KEYS, t), 0)
    qry = lax.broadcasted_iota(jnp.int32, (ATTN_KEYS, t), 1) + _last_step_offset(i, t)
    allowed = (key // CHUNK) <= (qry // CHUNK)
    above = jnp.maximum(key - qry, 0).astype(F32)

    streams = []
    for h in range(hb):
        qa, qb = _pack_queries(q_ref[:, h * LANES:(h + 1) * LANES])
        diag_add = jnp.where(allowed, (-2.0 * slopes2[h]) * above, NEG)
        vt = lambda k0, size, h=h: vt_ref[h, :, pl.ds(k0, size)]
        for m, (q, k_sc) in enumerate(((qa, ka_ref), (qb, kb_ref))):
            stats = (m_ref.at[2 * h + m], acc_ref.at[2 * h + m])
            _init_stats(*stats)
            streams.append((q, k_sc.at[h], vt, diag_add, stats))
    _attend(i, t, streams, (s0_ref, s1_ref), (mx0_ref, mx1_ref))

    lp = lam_ref[...]
    lam = (jnp.exp(jnp.sum(lp[0:1] * lp[1:2], axis=-1, keepdims=True))
           - jnp.exp(jnp.sum(lp[2:3] * lp[3:4], axis=-1, keepdims=True)) + lambda_init)
    for h in range(hb):
        pa = acc_ref[2 * h]
        pb = acc_ref[2 * h + 1]
        out = pa[0:dv] / pa[dv:dv + 1] - lam * (pb[0:dv] / pb[dv:dv + 1])
        ms = jnp.mean(out * out, axis=0, keepdims=True)
        out = out * lax.rsqrt(ms + EPS) * (sg_ref[...] * (1.0 - lambda_init))
        o_ref[:, h * LANES:(h + 1) * LANES] = out.T.astype(BF16)


def _da_call(z, slopes, lam_p, subln_g, heads, lambda_init):
    b, s, _ = z.shape
    t = min(ATTN_TILE, s)
    assert s % ATTN_KEYS == 0 and t % CHUNK == 0
    hb = min(ATTN_BLOCKS, heads)
    groups = heads // hb
    assert heads % hb == 0
    kern = functools.partial(_da_kernel, t=t, hb=hb, lambda_init=lambda_init)
    w = hb * LANES
    return pl.pallas_call(
        kern,
        grid_spec=pltpu.PrefetchScalarGridSpec(
            num_scalar_prefetch=1,
            grid=(b, groups, s // t),
            in_specs=[
                pl.BlockSpec((None, t, w), lambda bi, g, i, sl: (bi, i, g)),
                pl.BlockSpec((None, s, w), lambda bi, g, i, sl: (bi, 0, groups + g)),
                pl.BlockSpec((None, s, w), lambda bi, g, i, sl: (bi, 0, 2 * groups + g)),
                pl.BlockSpec((4, HEAD_DIM), lambda bi, g, i, sl: (0, 0)),
                pl.BlockSpec((LANES, 1), lambda bi, g, i, sl: (0, 0)),
            ],
            out_specs=pl.BlockSpec((None, t, w), lambda bi, g, i, sl: (bi, i, g)),
            scratch_shapes=[
                pltpu.VMEM((hb, s, LANES), BF16), pltpu.VMEM((hb, s, LANES), BF16),
                pltpu.VMEM((hb, LANES + ONES_ROWS, s), BF16),
                pltpu.VMEM((2 * hb, 1, t), F32),
                pltpu.VMEM((2 * hb, LANES + ONES_ROWS, t), F32),
                pltpu.VMEM((2 * hb, ATTN_KEYS, t), F32), pltpu.VMEM((2 * hb, ATTN_KEYS, t), F32),
                pltpu.VMEM((2 * hb, 1, t), F32), pltpu.VMEM((2 * hb, 1, t), F32),
            ],
        ),
        out_shape=jax.ShapeDtypeStruct((b, s, heads * LANES), BF16),
        compiler_params=_params("arbitrary", "arbitrary", "arbitrary"),
        name="diff_attention",
    )(slopes, z, z, z, lam_p, subln_g.reshape(LANES, 1))


def _fox_gate_kernel(g_ref, b_ref, o_ref, *, chunk):
    s = g_ref.shape[0]
    r = lax.broadcasted_iota(jnp.int32, (chunk, chunk), 0)
    c = lax.broadcasted_iota(jnp.int32, (chunk, chunk), 1)
    tri = (c <= r).astype(BF16)
    carry = jnp.zeros((1, LANES), F32)
    for c0 in range(0, s, chunk):
        ls = _log_sigmoid(g_ref[c0:c0 + chunk, :] + b_ref[...])
        cs = _exact_lhs_dot(tri, ls) + carry
        carry = cs[chunk - 1:chunk, :]
        o_ref[c0:c0 + chunk, :] = cs * (-LOG2E)


def _fox_gate_call(gates, bias_row):
    b, s, _ = gates.shape
    chunk = min(GATE_CHUNK, s)
    assert s % chunk == 0
    kern = functools.partial(_fox_gate_kernel, chunk=chunk)
    return pl.pallas_call(
        kern,
        grid=(b,),
        in_specs=[
            pl.BlockSpec((None, s, LANES), lambda i: (i, 0, 0)),
            pl.BlockSpec((1, LANES), lambda i: (0, 0)),
        ],
        out_specs=pl.BlockSpec((None, s, LANES), lambda i: (i, 0, 0)),
        out_shape=jax.ShapeDtypeStruct((b, s, LANES), F32),
        compiler_params=_params("arbitrary"),
        name="fox_gate_cumsum",
    )(gates, bias_row)


def _fox_kernel(q_ref, k_ref, v_ref, f_ref, o_ref,
                ka_ref, kb_ref, vt_ref, m_ref, acc_ref, s0_ref, s1_ref, mx0_ref, mx1_ref, *, t, hb):
    dv = HEAD_DIM
    g = pl.program_id(1)
    i = pl.program_id(2)

    @pl.when(i == 0)
    def _():
        n = BIAS_PIECES * LANES
        src = lax.broadcasted_iota(jnp.int32, (n, LANES), 0)
        dst = lax.broadcasted_iota(jnp.int32, (n, LANES), 1)
        piece = src // LANES
        head = src - piece * LANES
        for h in range(hb):
            pair = g * hb + h
            place_a = ((head == 2 * pair) & (dst == HEAD_DIM + piece)).astype(BF16)
            place_b = ((head == 2 * pair + 1) & (dst == piece)).astype(BF16)

            def extras(r0, rows, place_a=place_a, place_b=place_b):
                pieces = jnp.concatenate(_split3(f_ref[pl.ds(r0, rows), :]), axis=-1)
                return _dot(pieces, place_a), _dot(pieces, place_b)
            _pack_keys(_lane_block(k_ref, h), _lane_block(v_ref, h), extras,
                       ka_ref.at[h], kb_ref.at[h],
                       [(vt_ref.at[2 * h + m], m * dv, dv) for m in range(2)])

    key = lax.broadcasted_iota(jnp.int32, (ATTN_KEYS, t), 0)
    qry = lax.broadcasted_iota(jnp.int32, (ATTN_KEYS, t), 1) + _last_step_offset(i, t)
    diag_add = jnp.where(key <= qry, 0.0, NEG)

    streams = []
    for h in range(hb):
        qa, qb = _pack_queries(q_ref[:, h * LANES:(h + 1) * LANES])
        for m, (q, k_sc) in enumerate(((qa, ka_ref), (qb, kb_ref))):
            vt = lambda k0, size, n=2 * h + m: vt_ref[n, :, pl.ds(k0, size)]
            stats = (m_ref.at[2 * h + m], acc_ref.at[2 * h + m])
            _init_stats(*stats)
            streams.append((q, k_sc.at[h], vt, diag_add, stats))
    _attend(i, t, streams, (s0_ref, s1_ref), (mx0_ref, mx1_ref))

    for h in range(hb):
        pa = acc_ref[2 * h]
        pb = acc_ref[2 * h + 1]
        out = jnp.concatenate([pa[0:dv] / pa[dv:dv + 1], pb[0:dv] / pb[dv:dv + 1]], axis=0)
        o_ref[:, h * LANES:(h + 1) * LANES] = out.T.astype(BF16)


def _fox_call(z, fneg, heads):
    b, s, _ = z.shape
    pairs = heads // 2
    t = min(ATTN_TILE, s)
    assert s % ATTN_KEYS == 0
    hb = min(ATTN_BLOCKS, pairs)
    groups = pairs // hb
    assert pairs % hb == 0
    kern = functools.partial(_fox_kernel, t=t, hb=hb)
    w = hb * LANES
    return pl.pallas_call(
        kern,
        grid=(b, groups, s // t),
        in_specs=[
            pl.BlockSpec((None, t, w), lambda bi, g, i: (bi, i, g)),
            pl.BlockSpec((None, s, w), lambda bi, g, i: (bi, 0, groups + g)),
            pl.BlockSpec((None, s, w), lambda bi, g, i: (bi, 0, 2 * groups + g)),
            pl.BlockSpec((None, s, LANES), lambda bi, g, i: (bi, 0, 0)),
        ],
        out_specs=pl.BlockSpec((None, t, w), lambda bi, g, i: (bi, i, g)),
        out_shape=jax.ShapeDtypeStruct((b, s, pairs * LANES), BF16),
        scratch_shapes=[
            pltpu.VMEM((hb, s, LANES), BF16), pltpu.VMEM((hb, s, LANES), BF16),
            pltpu.VMEM((2 * hb, HEAD_DIM + ONES_ROWS, s), BF16),
            pltpu.VMEM((2 * hb, 1, t), F32),
            pltpu.VMEM((2 * hb, HEAD_DIM + ONES_ROWS, t), F32),
            pltpu.VMEM((2 * hb, ATTN_KEYS, t), F32), pltpu.VMEM((2 * hb, ATTN_KEYS, t), F32),
            pltpu.VMEM((2 * hb, 1, t), F32), pltpu.VMEM((2 * hb, 1, t), F32),
        ],
        compiler_params=_params("arbitrary", "arbitrary", "arbitrary"),
        name="forgetting_attention",
    )(z, z, z, fneg)


def _mlstm_kernel(q_ref, k_ref, v_ref, o_ref, g_ref, gb_ref, cw_ref, cb_ref, y_ref,
                  halo_ref, ct_ref, n_ref, m_ref, *, heads, taps):
    ci = pl.program_id(1)
    lc, width = q_ref.shape
    d = width // heads
    pad = halo_ref.shape[0]

    @pl.when(ci == 0)
    def _():
        halo_ref[...] = jnp.zeros(halo_ref.shape, F32)
        ct_ref[...] = jnp.zeros(ct_ref.shape, F32)
        n_ref[...] = jnp.zeros(n_ref.shape, F32)
        m_ref[...] = jnp.zeros(m_ref.shape, F32)

    xqk = jnp.concatenate([q_ref[...].astype(F32), k_ref[...].astype(F32)], axis=-1)
    full = jnp.concatenate([halo_ref[...], xqk], axis=0)
    conv = cb_ref[...]
    for j in range(taps):
        off = pad - (taps - 1) + j
        conv = conv + full[off:off + lc, :] * cw_ref[j:j + 1, :]
    halo_ref[...] = xqk[lc - pad:, :]
    qk = conv * _sigmoid(conv)

    gcol = g_ref[...] + gb_ref[...]
    grow = gcol.T[:2 * heads, :]
    r = lax.broadcasted_iota(jnp.int32, (lc, lc), 0)
    c = lax.broadcasted_iota(jnp.int32, (lc, lc), 1)
    causal = c <= r
    a_cols = _exact_lhs_dot(causal.astype(BF16), _log_sigmoid(gcol))
    a_rows = _exact_rhs_dot(_log_sigmoid(grow), (r <= c).astype(BF16))

    for hd in range(heads):
        q = qk[:, hd * d:(hd + 1) * d]
        k = qk[:, width + hd * d:width + (hd + 1) * d] * (d ** -0.5)
        v = v_ref[:, hd * d:(hd + 1) * d]
        qb = q.astype(BF16)
        kb = k.astype(BF16)
        ig_c = gcol[:, hd:hd + 1]
        ig_r = grow[hd:hd + 1, :]
        a_c = a_cols[:, heads + hd:heads + hd + 1]
        a_r = a_rows[heads + hd:heads + hd + 1, :]
        a_last = a_c[lc - 1:lc, :]
        ct = ct_ref[hd]
        n_prev = n_ref[hd:hd + 1, :]
        m_prev = m_ref[hd:hd + 1, 0:1]

        log_d = jnp.where(causal, a_c - a_r + ig_r, NEG)
        m_t = jnp.maximum(a_c + m_prev, jnp.max(log_d, axis=-1, keepdims=True))
        inter_w = jnp.exp(a_c + m_prev - m_t)
        sqk = _dot_nt(qb, kb) * jnp.exp(log_d - m_t)
        num = inter_w * _dot(qb, ct.astype(BF16)) + _dot(sqk.astype(BF16), v)
        den = (inter_w * jnp.sum(q * n_prev, axis=-1, keepdims=True)
               + jnp.sum(sqk, axis=-1, keepdims=True))
        hval = num / jnp.maximum(jnp.abs(den), jnp.exp(-m_t))
        og = o_ref[:, hd * d:(hd + 1) * d].astype(F32)
        y_ref[:, hd * d:(hd + 1) * d] = (_sigmoid(og) * hval).astype(BF16)

        g_c = a_last - a_c + ig_c
        m_loc = jnp.max(g_c, axis=0, keepdims=True)
        w_c = jnp.exp(g_c - m_loc)
        m_new = jnp.maximum(a_last + m_prev, m_loc)
        decay = jnp.exp(a_last + m_prev - m_new)
        s_loc = jnp.exp(m_loc - m_new)
        wv = (w_c * v.astype(F32)).astype(BF16)
        ct_ref[hd] = decay * ct + s_loc * _dot_tn(kb, wv)
        n_ref[hd:hd + 1, :] = decay * n_prev + s_loc * jnp.sum(w_c * k, axis=0, keepdims=True)
        m_ref[hd:hd + 1, :] = jnp.broadcast_to(m_new, (1, LANES))


def _mlstm_call(z, gates, gate_bias, conv_w, conv_b, heads, col0):
    b, s, _ = z.shape
    width = conv_w.shape[1] // 2
    lc = min(ML_CHUNK, s)
    assert s % lc == 0 and col0 % width == 0 and width // heads == LANES
    cb = col0 // width
    taps = conv_w.shape[0]
    kern = functools.partial(_mlstm_kernel, heads=heads, taps=taps)
    zspec = lambda off: pl.BlockSpec((None, lc, width), lambda bi, ci: (bi, ci, cb + off))
    return pl.pallas_call(
        kern,
        grid=(b, s // lc),
        in_specs=[
            zspec(0), zspec(1), zspec(2), zspec(3),
            pl.BlockSpec((None, lc, LANES), lambda bi, ci: (bi, ci, 0)),
            pl.BlockSpec((1, LANES), lambda bi, ci: (0, 0)),
            pl.BlockSpec((taps, 2 * width), lambda bi, ci: (0, 0)),
            pl.BlockSpec((1, 2 * width), lambda bi, ci: (0, 0)),
        ],
        out_specs=pl.BlockSpec((None, lc, width), lambda bi, ci: (bi, ci, 0)),
        out_shape=jax.ShapeDtypeStruct((b, s, width), BF16),
        scratch_shapes=[
            pltpu.VMEM((8, 2 * width), F32),
            pltpu.VMEM((heads, LANES, LANES), F32),
            pltpu.VMEM((8, LANES), F32),
            pltpu.VMEM((8, LANES), F32),
        ],
        compiler_params=_params("arbitrary", "arbitrary"),
        name="mlstm",
    )(z, z, z, z, gates, gate_bias, conv_w, conv_b)


def _outproj_kernel(*refs, n_in):
    y_refs = refs[:n_in]
    w_refs = refs[n_in:2 * n_in]
    x_ref, gate_ref, o_ref = refs[2 * n_in:]
    acc = _dot(y_refs[0][...], w_refs[0][...])
    for y_ref, w_ref in zip(y_refs[1:], w_refs[1:]):
        acc = acc + _dot(y_ref[...], w_ref[...])
    o_ref[...] = x_ref[...] + gate_ref[...] * acc


def _outproj_call(ys, ws, x, gate, name):
    b, s, d = x.shape
    tm = min(ROW_TILE, s)
    assert s % tm == 0
    n_in = len(ys)
    kern = functools.partial(_outproj_kernel, n_in=n_in)
    y_specs = [pl.BlockSpec((None, tm, y.shape[2]), lambda i, j: (i, j, 0)) for y in ys]
    w_specs = [pl.BlockSpec(w.shape, lambda i, j: (0, 0)) for w in ws]
    return pl.pallas_call(
        kern,
        grid=(b, s // tm),
        in_specs=y_specs + w_specs + [
            pl.BlockSpec((None, tm, d), lambda i, j: (i, j, 0)),
            pl.BlockSpec((None, 1, d), lambda i, j: (i, 0, 0)),
        ],
        out_specs=pl.BlockSpec((None, tm, d), lambda i, j: (i, j, 0)),
        out_shape=jax.ShapeDtypeStruct((b, s, d), F32),
        compiler_params=_params("arbitrary", "arbitrary"),
        name=name,
    )(*ys, *ws, x, gate)


def _ffn_kernel(x_ref, sh_ref, sc_ref, g_ref, gate_ref, w1_ref, w3_ref, w2_ref, o_ref, *, chunk):
    x = x_ref[...]
    h = _norm_modulate(x, g_ref[...], sh_ref[...], sc_ref[...]).astype(BF16)
    hidden = w1_ref.shape[1]
    acc = None
    for c0 in range(0, hidden, chunk):
        a = _dot(h, w1_ref[:, c0:c0 + chunk])
        bb = _dot(h, w3_ref[:, c0:c0 + chunk])
        u = (a * _sigmoid(a) * bb).astype(BF16)
        part = _dot(u, w2_ref[c0:c0 + chunk, :])
        acc = part if acc is None else acc + part
    o_ref[...] = x + gate_ref[...] * acc


def _ffn_call(x, shift, scale, g, gate, w1, w3, w2):
    b, s, d = x.shape
    hidden = w1.shape[1]
    tm = min(ROW_TILE, s)
    assert s % tm == 0 and hidden % FFN_CHUNK == 0
    kern = functools.partial(_ffn_kernel, chunk=FFN_CHUNK)
    vec = pl.BlockSpec((None, 1, d), lambda i, j: (i, 0, 0))
    return pl.pallas_call(
        kern,
        grid=(b, s // tm),
        in_specs=[
            pl.BlockSpec((None, tm, d), lambda i, j: (i, j, 0)),
            vec, vec,
            pl.BlockSpec((1, d), lambda i, j: (0, 0)),
            vec,
            pl.BlockSpec((d, hidden), lambda i, j: (0, 0)),
            pl.BlockSpec((d, hidden), lambda i, j: (0, 0)),
            pl.BlockSpec((hidden, d), lambda i, j: (0, 0)),
        ],
        out_specs=pl.BlockSpec((None, tm, d), lambda i, j: (i, j, 0)),
        out_shape=jax.ShapeDtypeStruct((b, s, d), F32),
        compiler_params=_params("arbitrary", "arbitrary"),
        name="ffn_swiglu",
    )(x, shift, scale, g, gate, w1, w3, w2)


def _pad_cols(w, n):
    return jnp.pad(w, ((0, 0), (0, n - w.shape[1])))


def _gain_row(q_g, k_g, width, total):
    reps = width // HEAD_DIM
    row = jnp.concatenate([jnp.tile(q_g.astype(F32) * (HEAD_DIM ** -0.5 * LOG2E), reps),
                           jnp.tile(k_g.astype(F32), reps),
                           jnp.ones((total - 2 * width,), F32)])
    return row.reshape(1, total)


def kernel(x, c, ada_w, ada_b, norm_mix_g, norm_ffn_g, ab_w_in, ml_b_i, ml_b_f, ml_conv_w, ml_conv_b,
           da_q_g, da_k_g, da_lambda, da_subln_g, ab_w_out, fx_w_in, fx_b_f, fx_q_g, fx_k_g, fx_w_out,
           ffn_w1, ffn_w3, ffn_w2):
    depth = ada_w.shape[0]
    b, s, d = x.shape
    ml_width = ml_conv_w.shape[2] // 2
    ml_heads = ml_b_i.shape[1]
    da_width = (ab_w_in.shape[2] - 4 * ml_width - 2 * ml_heads) // 3
    da_heads = da_width // (2 * HEAD_DIM)
    fx_heads = fx_b_f.shape[1]
    fx_width = fx_heads * HEAD_DIM
    ab_main = 3 * da_width + 4 * ml_width
    fx_main = 3 * fx_width

    mods = _ada_call(c, ada_w, ada_b).reshape(depth, b, 6, 1, d)
    slopes = 2.0 ** (-8.0 * jnp.arange(1, da_heads + 1, dtype=F32) / da_heads)

    for l in range(depth):
        sh_m, sc_m, g_m, sh_f, sc_f, g_f = (mods[l, :, t] for t in range(6))
        j = l // 2
        g_mix = norm_mix_g[l].reshape(1, d)
        if l % 2 == 0:
            w_in = ab_w_in[j]
            z, gates = _inproj_call(
                x, sh_m, sc_m, g_mix,
                w_in[:, :ab_main].astype(BF16),
                _pad_cols(w_in[:, ab_main:], LANES).astype(BF16),
                _gain_row(da_q_g[j], da_k_g[j], da_width, ab_main),
                2 * da_width, "inproj_even")
            lambda_init = 0.8 - 0.6 * math.exp(-0.3 * l)
            y_da = _da_call(z, slopes, da_lambda[j], da_subln_g[j], da_heads, lambda_init)
            gate_bias = _pad_cols(jnp.concatenate([ml_b_i[j], ml_b_f[j]]).reshape(1, -1), LANES)
            y_ml = _mlstm_call(z, gates, gate_bias, ml_conv_w[j], ml_conv_b[j].reshape(1, -1),
                               ml_heads, 3 * da_width)
            w_out = ab_w_out[j].astype(BF16)
            x = _outproj_call([y_da, y_ml], [w_out[:da_width], w_out[da_width:]], x, g_m,
                              "outproj_even")
        else:
            w_in = fx_w_in[j]
            z, gates = _inproj_call(
                x, sh_m, sc_m, g_mix,
                w_in[:, :fx_main].astype(BF16),
                _pad_cols(w_in[:, fx_main:], LANES).astype(BF16),
                _gain_row(fx_q_g[j], fx_k_g[j], fx_width, fx_main),
                2 * fx_width, "inproj_odd")
            fneg = _fox_gate_call(gates, _pad_cols(fx_b_f[j].reshape(1, -1), LANES))
            y = _fox_call(z, fneg, fx_heads)
            x = _outproj_call([y], [fx_w_out[j].astype(BF16)], x, g_m, "outproj_odd")
        x = _ffn_call(x, sh_f, sc_f, norm_ffn_g[l].reshape(1, d), g_f,
                      ffn_w1[l].astype(BF16), ffn_w3[l].astype(BF16), ffn_w2[l].astype(BF16))
    return x
```

```python
import functools
import math

import jax
import jax.numpy as jnp
from jax import lax
from jax.experimental import pallas as pl
from jax.experimental.pallas import tpu as pltpu

F32 = jnp.float32
BF16 = jnp.bfloat16

EPS = 1e-6
LANES = 128
HEAD_DIM = 64
CHUNK = 64
NEG = -1e30
LOG2E = math.log2(math.e)
BIAS_PIECES = 3
ONES_ROWS = 16

VMEM_LIMIT = 56 * 1024 * 1024

ROW_TILE = 512
ATTN_TILE = 256
ATTN_KEYS = 2 * ATTN_TILE
ATTN_BLOCKS = 4
PACK_ROWS = 512
ML_CHUNK = 256
GATE_CHUNK = 256
FFN_CHUNK = 256


def _params(*sem):
    return pltpu.CompilerParams(dimension_semantics=sem, vmem_limit_bytes=VMEM_LIMIT)


def _log_sigmoid(x):
    return jnp.minimum(x, 0.0) - jnp.log1p(jnp.exp(-jnp.abs(x)))


def _sigmoid(x):
    return 0.5 * jnp.tanh(0.5 * x) + 0.5


def _prefix_max_lanes(x):
    axis = x.ndim - 1
    lane = lax.broadcasted_iota(jnp.int32, x.shape, axis)
    shift = 1
    while shift < x.shape[axis]:
        x = jnp.maximum(x, jnp.where(lane >= shift, pltpu.roll(x, shift, axis=axis), NEG))
        shift *= 2
    return x


def _split3(x):
    x1 = x.astype(BF16)
    r1 = x - x1.astype(F32)
    x2 = r1.astype(BF16)
    x3 = (r1 - x2.astype(F32)).astype(BF16)
    return x1, x2, x3


def _dot(a, b):
    return jnp.dot(a, b, preferred_element_type=F32)


def _dot_nt(a, b):
    return lax.dot_general(a, b, (((1,), (1,)), ((), ())), preferred_element_type=F32)


def _dot_tn(a, b):
    return lax.dot_general(a, b, (((0,), (0,)), ((), ())), preferred_element_type=F32)


def _exact_lhs_dot(m_bf16, x):
    x1, x2, x3 = _split3(x)
    return _dot(m_bf16, x1) + _dot(m_bf16, x2) + _dot(m_bf16, x3)


def _exact_rhs_dot(x, m_bf16):
    x1, x2, x3 = _split3(x)
    return _dot(x1, m_bf16) + _dot(x2, m_bf16) + _dot(x3, m_bf16)


def _norm_modulate(x, g, sh, sc):
    ms = jnp.mean(x * x, axis=-1, keepdims=True)
    y = x * lax.rsqrt(ms + EPS) * g
    return y * (1.0 + sc) + sh


def _ada_kernel(c_ref, w_ref, b_ref, o_ref):
    c = c_ref[...]
    ca = c * _sigmoid(c)
    c1, c2, c3 = _split3(ca)
    w1, w2, w3 = _split3(w_ref[...])
    acc = _dot(c1, w1) + _dot(c1, w2) + _dot(c2, w1)
    acc = acc + _dot(c1, w3) + _dot(c2, w2) + _dot(c3, w1)
    o_ref[...] = acc + b_ref[...]


def _ada_call(c, ada_w, ada_b):
    depth, d, n = ada_w.shape
    b = c.shape[0]
    tn = 1536
    assert n % tn == 0
    return pl.pallas_call(
        _ada_kernel,
        grid=(depth, n // tn),
        in_specs=[
            pl.BlockSpec((b, d), lambda l, j: (0, 0)),
            pl.BlockSpec((None, d, tn), lambda l, j: (l, 0, j)),
            pl.BlockSpec((None, 1, tn), lambda l, j: (l, 0, j)),
        ],
        out_specs=pl.BlockSpec((None, b, tn), lambda l, j: (l, 0, j)),
        out_shape=jax.ShapeDtypeStruct((depth, b, n), F32),
        compiler_params=_params("arbitrary", "arbitrary"),
        name="ada_mod",
    )(c, ada_w, ada_b.reshape(depth, 1, n))


def _inproj_kernel(x_ref, sh_ref, sc_ref, g_ref, w_ref, wg_ref, gain_ref, z_ref, gate_ref,
                   *, n_norm, chunk):
    h = _norm_modulate(x_ref[...], g_ref[...], sh_ref[...], sc_ref[...]).astype(BF16)
    n = w_ref.shape[1]
    lane = lax.broadcasted_iota(jnp.int32, (1, LANES), 1)
    lo = lane < HEAD_DIM
    for c0 in range(0, n, chunk):
        zc = _dot(h, w_ref[:, c0:c0 + chunk])
        if c0 < n_norm:
            parts = []
            for s0 in range(0, chunk, LANES):
                zs = zc[:, s0:s0 + LANES]
                sq = zs * zs
                s_lo = jnp.sum(jnp.where(lo, sq, 0.0), axis=-1, keepdims=True)
                s_hi = jnp.sum(jnp.where(lo, 0.0, sq), axis=-1, keepdims=True)
                r_lo = lax.rsqrt(s_lo * (1.0 / HEAD_DIM) + EPS)
                r_hi = lax.rsqrt(s_hi * (1.0 / HEAD_DIM) + EPS)
                parts.append(zs * jnp.where(lo, r_lo, r_hi))
            zc = jnp.concatenate(parts, axis=-1) * gain_ref[:, c0:c0 + chunk]
        z_ref[:, c0:c0 + chunk] = zc.astype(BF16)
    gate_ref[...] = _dot(h, wg_ref[...])


def _inproj_call(x, shift, scale, g, w, wg, gain, n_norm, name):
    b, s, d = x.shape
    n = w.shape[1]
    tm = min(ROW_TILE, s)
    chunk = 512
    assert s % tm == 0 and n % chunk == 0 and n_norm % chunk == 0
    kern = functools.partial(_inproj_kernel, n_norm=n_norm, chunk=chunk)
    return pl.pallas_call(
        kern,
        grid=(b, s // tm),
        in_specs=[
            pl.BlockSpec((None, tm, d), lambda i, j: (i, j, 0)),
            pl.BlockSpec((None, 1, d), lambda i, j: (i, 0, 0)),
            pl.BlockSpec((None, 1, d), lambda i, j: (i, 0, 0)),
            pl.BlockSpec((1, d), lambda i, j: (0, 0)),
            pl.BlockSpec((d, n), lambda i, j: (0, 0)),
            pl.BlockSpec((d, LANES), lambda i, j: (0, 0)),
            pl.BlockSpec((1, n), lambda i, j: (0, 0)),
        ],
        out_specs=[
            pl.BlockSpec((None, tm, n), lambda i, j: (i, j, 0)),
            pl.BlockSpec((None, tm, LANES), lambda i, j: (i, j, 0)),
        ],
        out_shape=[
            jax.ShapeDtypeStruct((b, s, n), BF16),
            jax.ShapeDtypeStruct((b, s, LANES), F32),
        ],
        compiler_params=_params("arbitrary", "arbitrary"),
        name=name,
    )(x, shift, scale, g, w, wg, gain)


def _lane_ids():
    return lax.broadcasted_iota(jnp.int32, (1, LANES), 1)


def _place_pieces(pieces, first_lane):
    lane = _lane_ids()
    out = jnp.zeros(pieces[0].shape, F32)
    for j, p in enumerate(pieces):
        out = jnp.where(lane == first_lane + j, p.astype(F32), out)
    return out


def _pack_keys(k_ref, v_ref, extras_fn, ka_ref, kb_ref, vt_refs):
    s = k_ref.shape[0]
    rows = min(PACK_ROWS, s)
    lo = _lane_ids() < HEAD_DIM

    def body(c, carry):
        r0 = pl.multiple_of(c * rows, rows)
        k2 = k_ref[pl.ds(r0, rows), :].astype(F32)
        ea, eb = extras_fn(r0, rows)
        ka_ref[pl.ds(r0, rows), :] = jnp.where(lo, k2, ea).astype(BF16)
        kb_ref[pl.ds(r0, rows), :] = jnp.where(lo, eb, k2).astype(BF16)
        vt = v_ref[pl.ds(r0, rows), :].astype(F32).T.astype(BF16)
        for ref, c0, width in vt_refs:
            ref[0:width, pl.ds(r0, rows)] = vt[c0:c0 + width, :]
            ref[width:width + ONES_ROWS, pl.ds(r0, rows)] = jnp.ones((ONES_ROWS, rows), BF16)
        return carry

    lax.fori_loop(0, s // rows, body, 0)


def _pack_queries(q2):
    lane = _lane_ids()
    q2 = q2.astype(F32)
    ones_a = jnp.where((lane >= HEAD_DIM) & (lane < HEAD_DIM + BIAS_PIECES), 1.0, 0.0)
    ones_b = jnp.where(lane < BIAS_PIECES, 1.0, 0.0)
    lo = lane < HEAD_DIM
    return jnp.where(lo, q2, ones_a).astype(BF16), jnp.where(lo, ones_b, q2).astype(BF16)


def _init_stats(m_ref, acc_ref):
    m_ref[...] = jnp.full(m_ref.shape, NEG, F32)
    acc_ref[...] = jnp.zeros(acc_ref.shape, F32)


def _softmax_step(s, tile_max, vt, m_ref, acc_ref):
    m_old = m_ref[...]
    m_new = jnp.maximum(m_old, tile_max)
    alpha = jnp.exp2(m_old - m_new)
    p = jnp.exp2(s - m_new)
    acc_ref[...] = alpha * acc_ref[...] + _dot(vt, p.astype(BF16))
    m_ref[...] = m_new


def _attend(i, t, streams, s_buf, mx_buf):
    tk = ATTN_KEYS
    n = (i * t + t + tk - 1) // tk

    def qk_stage(step, slot):
        k0 = pl.multiple_of(step * tk, tk)
        for idx, (q, k_ref, _, _, _) in enumerate(streams):
            s = _dot_nt(k_ref[pl.ds(k0, tk), :], q)
            s_buf[slot][idx] = s
            mx_buf[slot][idx] = jnp.max(s, axis=0, keepdims=True)

    def softmax_stage(step, slot, last):
        k0 = pl.multiple_of(step * tk, tk)
        for idx, (_, _, vt, add, stats) in enumerate(streams):
            s = s_buf[slot][idx]
            if last:
                s = s + add
                tile_max = jnp.max(s, axis=0, keepdims=True)
            else:
                tile_max = mx_buf[slot][idx]
            _softmax_step(s, tile_max, vt(k0, tk), *stats)

    qk_stage(0, 0)
    pairs = (n - 1) // 2

    def body(p, carry):
        qk_stage(2 * p + 1, 1)
        softmax_stage(2 * p, 0, False)
        qk_stage(2 * p + 2, 0)
        softmax_stage(2 * p + 1, 1, False)
        return carry

    lax.fori_loop(0, pairs, body, 0)

    @pl.when(n - 1 > 2 * pairs)
    def _():
        qk_stage(n - 1, 1)
        softmax_stage(n - 2, 0, False)
        softmax_stage(n - 1, 1, True)

    @pl.when(n - 1 == 2 * pairs)
    def _():
        softmax_stage(n - 1, 0, True)


def _last_step_offset(i, t):
    tk = ATTN_KEYS
    n = (i * t + t + tk - 1) // tk
    return i * t - (n - 1) * tk


def _lane_block(ref, h):
    return ref.at[:, pl.ds(h * LANES, LANES)]


def _da_kernel(slope_ref, q_ref, k_ref, v_ref, lam_ref, sg_ref, o_ref,
               ka_ref, kb_ref, vt_ref, m_ref, acc_ref, s0_ref, s1_ref, mx0_ref, mx1_ref,
               *, t, hb, lambda_init):
    dv = LANES
    g = pl.program_id(1)
    i = pl.program_id(2)
    slopes2 = [slope_ref[g * hb + h] * LOG2E for h in range(hb)]

    @pl.when(i == 0)
    def _():
        for h in range(hb):
            def extras(r0, rows, slope2=slopes2[h]):
                pos = (lax.broadcasted_iota(jnp.int32, (rows, LANES), 0) + r0).astype(F32) * slope2
                pieces = _split3(pos)
                return _place_pieces(pieces, HEAD_DIM), _place_pieces(pieces, 0)
            _pack_keys(_lane_block(k_ref, h), _lane_block(v_ref, h), extras,
                       ka_ref.at[h], kb_ref.at[h], [(vt_ref.at[h], 0, dv)])

    key = lax.broadcasted_iota(jnp.int32, (ATTN_KEYS, t), 0)
    qry = lax.broadcasted_iota(jnp.int32, (ATTN_KEYS, t), 1) + _last_step_offset(i, t)
    allowed = (key // CHUNK) <= (qry // CHUNK)
    above = jnp.maximum(key - qry, 0).astype(F32)

    streams = []
    for h in range(hb):
        qa, qb = _pack_queries(q_ref[:, h * LANES:(h + 1) * LANES])
        diag_add = jnp.where(allowed, (-2.0 * slopes2[h]) * above, NEG)
        vt = lambda k0, size, h=h: vt_ref[h, :, pl.ds(k0, size)]
        for m, (q, k_sc) in enumerate(((qa, ka_ref), (qb, kb_ref))):
            stats = (m_ref.at[2 * h + m], acc_ref.at[2 * h + m])
            _init_stats(*stats)
            streams.append((q, k_sc.at[h], vt, diag_add, stats))
    _attend(i, t, streams, (s0_ref, s1_ref), (mx0_ref, mx1_ref))

    lp = lam_ref[...]
    lam = (jnp.exp(jnp.sum(lp[0:1] * lp[1:2], axis=-1, keepdims=True))
           - jnp.exp(jnp.sum(lp[2:3] * lp[3:4], axis=-1, keepdims=True)) + lambda_init)
    for h in range(hb):
        pa = acc_ref[2 * h]
        pb = acc_ref[2 * h + 1]
        out = pa[0:dv] / pa[dv:dv + 1] - lam * (pb[0:dv] / pb[dv:dv + 1])
        ms = jnp.mean(out * out, axis=0, keepdims=True)
        out = out * lax.rsqrt(ms + EPS) * (sg_ref[...] * (1.0 - lambda_init))
        o_ref[:, h * LANES:(h + 1) * LANES] = out.T.astype(BF16)


def _da_call(z, slopes, lam_p, subln_g, heads, lambda_init):
    b, s, _ = z.shape
    t = min(ATTN_TILE, s)
    assert s % ATTN_KEYS == 0 and t % CHUNK == 0
    hb = min(ATTN_BLOCKS, heads)
    groups = heads // hb
    assert heads % hb == 0
    kern = functools.partial(_da_kernel, t=t, hb=hb, lambda_init=lambda_init)
    w = hb * LANES
    return pl.pallas_call(
        kern,
        grid_spec=pltpu.PrefetchScalarGridSpec(
            num_scalar_prefetch=1,
            grid=(b, groups, s // t),
            in_specs=[
                pl.BlockSpec((None, t, w), lambda bi, g, i, sl: (bi, i, g)),
                pl.BlockSpec((None, s, w), lambda bi, g, i, sl: (bi, 0, groups + g)),
                pl.BlockSpec((None, s, w), lambda bi, g, i, sl: (bi, 0, 2 * groups + g)),
                pl.BlockSpec((4, HEAD_DIM), lambda bi, g, i, sl: (0, 0)),
                pl.BlockSpec((LANES, 1), lambda bi, g, i, sl: (0, 0)),
            ],
            out_specs=pl.BlockSpec((None, t, w), lambda bi, g, i, sl: (bi, i, g)),
            scratch_shapes=[
                pltpu.VMEM((hb, s, LANES), BF16), pltpu.VMEM((hb, s, LANES), BF16),
                pltpu.VMEM((hb, LANES + ONES_ROWS, s), BF16),
                pltpu.VMEM((2 * hb, 1, t), F32),
                pltpu.VMEM((2 * hb, LANES + ONES_ROWS, t), F32),
                pltpu.VMEM((2 * hb, ATTN_KEYS, t), F32), pltpu.VMEM((2 * hb, ATTN_KEYS, t), F32),
                pltpu.VMEM((2 * hb, 1, t), F32), pltpu.VMEM((2 * hb, 1, t), F32),
            ],
        ),
        out_shape=jax.ShapeDtypeStruct((b, s, heads * LANES), BF16),
        compiler_params=_params("arbitrary", "arbitrary", "arbitrary"),
        name="diff_attention",
    )(slopes, z, z, z, lam_p, subln_g.reshape(LANES, 1))


def _fox_gate_kernel(g_ref, b_ref, o_ref, *, chunk):
    s = g_ref.shape[0]
    r = lax.broadcasted_iota(jnp.int32, (chunk, chunk), 0)
    c = lax.broadcasted_iota(jnp.int32, (chunk, chunk), 1)
    tri = (c <= r).astype(BF16)
    carry = jnp.zeros((1, LANES), F32)
    for c0 in range(0, s, chunk):
        ls = _log_sigmoid(g_ref[c0:c0 + chunk, :] + b_ref[...])
        cs = _exact_lhs_dot(tri, ls) + carry
        carry = cs[chunk - 1:chunk, :]
        o_ref[c0:c0 + chunk, :] = cs * (-LOG2E)


def _fox_gate_call(gates, bias_row):
    b, s, _ = gates.shape
    chunk = min(GATE_CHUNK, s)
    assert s % chunk == 0
    kern = functools.partial(_fox_gate_kernel, chunk=chunk)
    return pl.pallas_call(
        kern,
        grid=(b,),
        in_specs=[
            pl.BlockSpec((None, s, LANES), lambda i: (i, 0, 0)),
            pl.BlockSpec((1, LANES), lambda i: (0, 0)),
        ],
        out_specs=pl.BlockSpec((None, s, LANES), lambda i: (i, 0, 0)),
        out_shape=jax.ShapeDtypeStruct((b, s, LANES), F32),
        compiler_params=_params("arbitrary"),
        name="fox_gate_cumsum",
    )(gates, bias_row)


def _fox_kernel(q_ref, k_ref, v_ref, f_ref, o_ref,
                ka_ref, kb_ref, vt_ref, m_ref, acc_ref, s0_ref, s1_ref, mx0_ref, mx1_ref, *, t, hb):
    dv = HEAD_DIM
    g = pl.program_id(1)
    i = pl.program_id(2)

    @pl.when(i == 0)
    def _():
        n = BIAS_PIECES * LANES
        src = lax.broadcasted_iota(jnp.int32, (n, LANES), 0)
        dst = lax.broadcasted_iota(jnp.int32, (n, LANES), 1)
        piece = src // LANES
        head = src - piece * LANES
        for h in range(hb):
            pair = g * hb + h
            place_a = ((head == 2 * pair) & (dst == HEAD_DIM + piece)).astype(BF16)
            place_b = ((head == 2 * pair + 1) & (dst == piece)).astype(BF16)

            def extras(r0, rows, place_a=place_a, place_b=place_b):
                pieces = jnp.concatenate(_split3(f_ref[pl.ds(r0, rows), :]), axis=-1)
                return _dot(pieces, place_a), _dot(pieces, place_b)
            _pack_keys(_lane_block(k_ref, h), _lane_block(v_ref, h), extras,
                       ka_ref.at[h], kb_ref.at[h],
                       [(vt_ref.at[2 * h + m], m * dv, dv) for m in range(2)])

    key = lax.broadcasted_iota(jnp.int32, (ATTN_KEYS, t), 0)
    qry = lax.broadcasted_iota(jnp.int32, (ATTN_KEYS, t), 1) + _last_step_offset(i, t)
    diag_add = jnp.where(key <= qry, 0.0, NEG)

    streams = []
    for h in range(hb):
        qa, qb = _pack_queries(q_ref[:, h * LANES:(h + 1) * LANES])
        for m, (q, k_sc) in enumerate(((qa, ka_ref), (qb, kb_ref))):
            vt = lambda k0, size, n=2 * h + m: vt_ref[n, :, pl.ds(k0, size)]
            stats = (m_ref.at[2 * h + m], acc_ref.at[2 * h + m])
            _init_stats(*stats)
            streams.append((q, k_sc.at[h], vt, diag_add, stats))
    _attend(i, t, streams, (s0_ref, s1_ref), (mx0_ref, mx1_ref))

    for h in range(hb):
        pa = acc_ref[2 * h]
        pb = acc_ref[2 * h + 1]
        out = jnp.concatenate([pa[0:dv] / pa[dv:dv + 1], pb[0:dv] / pb[dv:dv + 1]], axis=0)
        o_ref[:, h * LANES:(h + 1) * LANES] = out.T.astype(BF16)


def _fox_call(z, fneg, heads):
    b, s, _ = z.shape
    pairs = heads // 2
    t = min(ATTN_TILE, s)
    assert s % ATTN_KEYS == 0
    hb = min(ATTN_BLOCKS, pairs)
    groups = pairs // hb
    assert pairs % hb == 0
    kern = functools.partial(_fox_kernel, t=t, hb=hb)
    w = hb * LANES
    return pl.pallas_call(
        kern,
        grid=(b, groups, s // t),
        in_specs=[
            pl.BlockSpec((None, t, w), lambda bi, g, i: (bi, i, g)),
            pl.BlockSpec((None, s, w), lambda bi, g, i: (bi, 0, groups + g)),
            pl.BlockSpec((None, s, w), lambda bi, g, i: (bi, 0, 2 * groups + g)),
            pl.BlockSpec((None, s, LANES), lambda bi, g, i: (bi, 0, 0)),
        ],
        out_specs=pl.BlockSpec((None, t, w), lambda bi, g, i: (bi, i, g)),
        out_shape=jax.ShapeDtypeStruct((b, s, pairs * LANES), BF16),
        scratch_shapes=[
            pltpu.VMEM((hb, s, LANES), BF16), pltpu.VMEM((hb, s, LANES), BF16),
            pltpu.VMEM((2 * hb, HEAD_DIM + ONES_ROWS, s), BF16),
            pltpu.VMEM((2 * hb, 1, t), F32),
            pltpu.VMEM((2 * hb, HEAD_DIM + ONES_ROWS, t), F32),
            pltpu.VMEM((2 * hb, ATTN_KEYS, t), F32), pltpu.VMEM((2 * hb, ATTN_KEYS, t), F32),
            pltpu.VMEM((2 * hb, 1, t), F32), pltpu.VMEM((2 * hb, 1, t), F32),
        ],
        compiler_params=_params("arbitrary", "arbitrary", "arbitrary"),
        name="forgetting_attention",
    )(z, z, z, fneg)


def _mlstm_kernel(q_ref, k_ref, v_ref, o_ref, g_ref, gb_ref, cw_ref, cb_ref, y_ref,
                  halo_ref, ct_ref, m_ref, *, heads, taps):
    ci = pl.program_id(1)
    lc, width = q_ref.shape
    d = width // heads
    pad = halo_ref.shape[0]

    @pl.when(ci == 0)
    def _():
        halo_ref[...] = jnp.zeros(halo_ref.shape, F32)
        ct_ref[...] = jnp.zeros(ct_ref.shape, F32)
        m_ref[...] = jnp.zeros(m_ref.shape, F32)

    r = lax.broadcasted_iota(jnp.int32, (lc, lc), 0)
    c = lax.broadcasted_iota(jnp.int32, (lc, lc), 1)
    causal = c <= r

    xb = jnp.concatenate([q_ref[...], k_ref[...]], axis=-1)
    xf = xb.astype(F32)
    conv = cb_ref[...] + xf * cw_ref[taps - 1:taps, :]
    for sft in range(1, taps):
        shifted = _dot((r - c == sft).astype(BF16), xb)
        conv = conv + shifted * cw_ref[taps - 1 - sft:taps - sft, :]
    edge = jnp.concatenate([halo_ref[...], xf[0:pad, :]], axis=0)
    first = cb_ref[...]
    for j in range(taps):
        off = pad - (taps - 1) + j
        first = first + edge[off:off + pad, :] * cw_ref[j:j + 1, :]
    conv = jnp.concatenate([first, conv[pad:, :]], axis=0)
    halo_ref[...] = xf[lc - pad:, :]
    qk = conv * _sigmoid(conv)

    gcol = g_ref[...] + gb_ref[...]
    grow = gcol.T[:2 * heads, :]
    a_cols = _exact_lhs_dot(causal.astype(BF16), _log_sigmoid(gcol))
    a_rows = _exact_rhs_dot(_log_sigmoid(grow), (r <= c).astype(BF16))

    b_half = grow[0:heads, :] - a_rows[heads:2 * heads, :]
    b_rows = jnp.concatenate([b_half, b_half], axis=0)
    x_cols = jnp.maximum(_prefix_max_lanes(b_rows), m_ref[:, 0:1]).T
    ones = jnp.ones((lc, d), BF16)

    for hd in range(heads):
        q = qk[:, hd * d:(hd + 1) * d]
        k = qk[:, width + hd * d:width + (hd + 1) * d] * (d ** -0.5)
        v = v_ref[:, hd * d:(hd + 1) * d]
        qb = q.astype(BF16)
        kb = k.astype(BF16)
        ig_c = gcol[:, hd:hd + 1]
        a_c = a_cols[:, heads + hd:heads + hd + 1]
        a_last = a_c[lc - 1:lc, :]
        x_c = x_cols[:, hd:hd + 1]
        ct = ct_ref[hd]
        m_prev = m_ref[hd:hd + 1, 0:1]

        decay_mat = jnp.exp2(jnp.where(causal, b_rows[hd:hd + 1, :] * LOG2E - x_c * LOG2E, NEG))
        sqk = (_dot_nt(qb, kb) * decay_mat).astype(BF16)
        intra = _dot(sqk, jnp.concatenate([v, ones], axis=-1))
        inter = jnp.exp(m_prev - x_c) * _dot(qb, ct.astype(BF16))
        num = inter[:, 0:d] + intra[:, 0:d]
        den = inter[:, d:d + 1] + intra[:, d:d + 1]
        hval = num / jnp.maximum(jnp.abs(den), jnp.exp(-(a_c + x_c)))
        og = o_ref[:, hd * d:(hd + 1) * d].astype(F32)
        y_ref[:, hd * d:(hd + 1) * d] = (_sigmoid(og) * hval).astype(BF16)

        g_c = a_last - a_c + ig_c
        m_loc = jnp.max(g_c, axis=0, keepdims=True)
        w_c = jnp.exp(g_c - m_loc)
        m_new = jnp.maximum(a_last + m_prev, m_loc)
        decay = jnp.exp(a_last + m_prev - m_new)
        s_loc = jnp.exp(m_loc - m_new)
        wv = jnp.concatenate([w_c * v.astype(F32), jnp.broadcast_to(w_c, (lc, d))], axis=-1)
        ct_ref[hd] = decay * ct + s_loc * _dot_tn(kb, wv.astype(BF16))
        m_ref[hd:hd + 1, :] = jnp.broadcast_to(m_new, (1, LANES))
        m_ref[heads + hd:heads + hd + 1, :] = jnp.broadcast_to(m_new, (1, LANES))


def _mlstm_call(z, gates, gate_bias, conv_w, conv_b, heads, col0):
    b, s, _ = z.shape
    width = conv_w.shape[1] // 2
    lc = min(ML_CHUNK, s)
    assert s % lc == 0 and col0 % width == 0 and width // heads == LANES
    cb = col0 // width
    taps = conv_w.shape[0]
    kern = functools.partial(_mlstm_kernel, heads=heads, taps=taps)
    zspec = lambda off: pl.BlockSpec((None, lc, width), lambda bi, ci: (bi, ci, cb + off))
    return pl.pallas_call(
        kern,
        grid=(b, s // lc),
        in_specs=[
            zspec(0), zspec(1), zspec(2), zspec(3),
            pl.BlockSpec((None, lc, LANES), lambda bi, ci: (bi, ci, 0)),
            pl.BlockSpec((1, LANES), lambda bi, ci: (0, 0)),
            pl.BlockSpec((taps, 2 * width), lambda bi, ci: (0, 0)),
            pl.BlockSpec((1, 2 * width), lambda bi, ci: (0, 0)),
        ],
        out_specs=pl.BlockSpec((None, lc, width), lambda bi, ci: (bi, ci, 0)),
        out_shape=jax.ShapeDtypeStruct((b, s, width), BF16),
        scratch_shapes=[
            pltpu.VMEM((8, 2 * width), F32),
            pltpu.VMEM((heads, LANES, 2 * LANES), F32),
            pltpu.VMEM((2 * heads, LANES), F32),
        ],
        compiler_params=_params("arbitrary", "arbitrary"),
        name="mlstm",
    )(z, z, z, z, gates, gate_bias, conv_w, conv_b)


def _mix_ffn_kernel(*refs, n_in, chunk):
    y_refs = refs[:n_in]
    wo_refs = refs[n_in:2 * n_in]
    (x_ref, gm_ref, sh_ref, sc_ref, g_ref, gate_ref, w1_ref, w3_ref, w2_ref, o_ref) = refs[2 * n_in:]
    mix = _dot(y_refs[0][...], wo_refs[0][...])
    for y_ref, w_ref in zip(y_refs[1:], wo_refs[1:]):
        mix = mix + _dot(y_ref[...], w_ref[...])
    x = x_ref[...] + gm_ref[...] * mix
    h = _norm_modulate(x, g_ref[...], sh_ref[...], sc_ref[...]).astype(BF16)
    hidden = w1_ref.shape[1]
    acc = None
    for c0 in range(0, hidden, chunk):
        a = _dot(h, w1_ref[:, c0:c0 + chunk])
        bb = _dot(h, w3_ref[:, c0:c0 + chunk])
        u = (a * _sigmoid(a) * bb).astype(BF16)
        part = _dot(u, w2_ref[c0:c0 + chunk, :])
        acc = part if acc is None else acc + part
    o_ref[...] = x + gate_ref[...] * acc


def _mix_ffn_call(ys, wos, x, gate_mix, shift, scale, g, gate, w1, w3, w2, name):
    b, s, d = x.shape
    hidden = w1.shape[1]
    tm = min(ROW_TILE, s)
    assert s % tm == 0 and hidden % FFN_CHUNK == 0
    kern = functools.partial(_mix_ffn_kernel, n_in=len(ys), chunk=FFN_CHUNK)
    vec = pl.BlockSpec((None, 1, d), lambda i, j: (i, 0, 0))
    rows = lambda width: pl.BlockSpec((None, tm, width), lambda i, j: (i, j, 0))
    whole = lambda w: pl.BlockSpec(w.shape, lambda i, j: (0, 0))
    return pl.pallas_call(
        kern,
        grid=(b, s // tm),
        in_specs=[rows(y.shape[2]) for y in ys] + [whole(w) for w in wos] + [
            rows(d), vec, vec, vec,
            pl.BlockSpec((1, d), lambda i, j: (0, 0)),
            vec, whole(w1), whole(w3), whole(w2),
        ],
        out_specs=rows(d),
        out_shape=jax.ShapeDtypeStruct((b, s, d), F32),
        compiler_params=_params("arbitrary", "arbitrary"),
        name=name,
    )(*ys, *wos, x, gate_mix, shift, scale, g, gate, w1, w3, w2)


def _pad_cols(w, n):
    return jnp.pad(w, ((0, 0), (0, n - w.shape[1])))


def _gain_row(q_g, k_g, width, total):
    reps = width // HEAD_DIM
    row = jnp.concatenate([jnp.tile(q_g.astype(F32) * (HEAD_DIM ** -0.5 * LOG2E), reps),
                           jnp.tile(k_g.astype(F32), reps),
                           jnp.ones((total - 2 * width,), F32)])
    return row.reshape(1, total)


def kernel(x, c, ada_w, ada_b, norm_mix_g, norm_ffn_g, ab_w_in, ml_b_i, ml_b_f, ml_conv_w, ml_conv_b,
           da_q_g, da_k_g, da_lambda, da_subln_g, ab_w_out, fx_w_in, fx_b_f, fx_q_g, fx_k_g, fx_w_out,
           ffn_w1, ffn_w3, ffn_w2):
    depth = ada_w.shape[0]
    b, s, d = x.shape
    ml_width = ml_conv_w.shape[2] // 2
    ml_heads = ml_b_i.shape[1]
    da_width = (ab_w_in.shape[2] - 4 * ml_width - 2 * ml_heads) // 3
    da_heads = da_width // (2 * HEAD_DIM)
    fx_heads = fx_b_f.shape[1]
    fx_width = fx_heads * HEAD_DIM
    ab_main = 3 * da_width + 4 * ml_width
    fx_main = 3 * fx_width

    mods = _ada_call(c, ada_w, ada_b).reshape(depth, b, 6, 1, d)
    slopes = 2.0 ** (-8.0 * jnp.arange(1, da_heads + 1, dtype=F32) / da_heads)

    for l in range(depth):
        sh_m, sc_m, g_m, sh_f, sc_f, g_f = (mods[l, :, t] for t in range(6))
        j = l // 2
        g_mix = norm_mix_g[l].reshape(1, d)
        if l % 2 == 0:
            w_in = ab_w_in[j]
            z, gates = _inproj_call(
                x, sh_m, sc_m, g_mix,
                w_in[:, :ab_main].astype(BF16),
                _pad_cols(w_in[:, ab_main:], LANES).astype(BF16),
                _gain_row(da_q_g[j], da_k_g[j], da_width, ab_main),
                2 * da_width, "inproj_even")
            lambda_init = 0.8 - 0.6 * math.exp(-0.3 * l)
            y_da = _da_call(z, slopes, da_lambda[j], da_subln_g[j], da_heads, lambda_init)
            gate_bias = _pad_cols(jnp.concatenate([ml_b_i[j], ml_b_f[j]]).reshape(1, -1), LANES)
            y_ml = _mlstm_call(z, gates, gate_bias, ml_conv_w[j], ml_conv_b[j].reshape(1, -1),
                               ml_heads, 3 * da_width)
            w_out = ab_w_out[j].astype(BF16)
            ys, wos, name = [y_da, y_ml], [w_out[:da_width], w_out[da_width:]], "mix_ffn_even"
        else:
            w_in = fx_w_in[j]
            z, gates = _inproj_call(
                x, sh_m, sc_m, g_mix,
                w_in[:, :fx_main].astype(BF16),
                _pad_cols(w_in[:, fx_main:], LANES).astype(BF16),
                _gain_row(fx_q_g[j], fx_k_g[j], fx_width, fx_main),
                2 * fx_width, "inproj_odd")
            fneg = _fox_gate_call(gates, _pad_cols(fx_b_f[j].reshape(1, -1), LANES))
            ys, wos, name = [_fox_call(z, fneg, fx_heads)], [fx_w_out[j].astype(BF16)], "mix_ffn_odd"
        x = _mix_ffn_call(ys, wos, x, g_m, sh_f, sc_f, norm_ffn_g[l].reshape(1, d), g_f,
                          ffn_w1[l].astype(BF16), ffn_w3[l].astype(BF16), ffn_w2[l].astype(BF16),
                          name)
    return x
```

```python
import functools
import math

import jax
import jax.numpy as jnp
from jax import lax
from jax.experimental import pallas as pl
from jax.experimental.pallas import tpu as pltpu

F32 = jnp.float32
BF16 = jnp.bfloat16

EPS = 1e-6
LANES = 128
HEAD_DIM = 64
CHUNK = 64
CHUNK_SHIFT = CHUNK.bit_length() - 1
assert CHUNK == 1 << CHUNK_SHIFT
NEG = -1e30
LOG2E = math.log2(math.e)
BIAS_PIECES = 3
ONES_ROWS = 16

VMEM_LIMIT = 56 * 1024 * 1024

ROW_TILE = 512
ATTN_TILE = 256
ATTN_KEYS = 2 * ATTN_TILE
ATTN_BLOCKS = 4
PACK_ROWS = 512
ML_CHUNK = 256
GATE_CHUNK = 256
FFN_CHUNK = 256


def _params(*sem):
    return pltpu.CompilerParams(dimension_semantics=sem, vmem_limit_bytes=VMEM_LIMIT)


def _log_sigmoid(x):
    return jnp.minimum(x, 0.0) - jnp.log1p(jnp.exp(-jnp.abs(x)))


def _sigmoid(x):
    return 0.5 * jnp.tanh(0.5 * x) + 0.5


def _prefix_max_lanes(x):
    axis = x.ndim - 1
    lane = lax.broadcasted_iota(jnp.int32, x.shape, axis)
    shift = 1
    while shift < x.shape[axis]:
        x = jnp.maximum(x, jnp.where(lane >= shift, pltpu.roll(x, shift, axis=axis), NEG))
        shift *= 2
    return x


def _split3(x):
    x1 = x.astype(BF16)
    r1 = x - x1.astype(F32)
    x2 = r1.astype(BF16)
    x3 = (r1 - x2.astype(F32)).astype(BF16)
    return x1, x2, x3


def _dot(a, b):
    return jnp.dot(a, b, preferred_element_type=F32)


def _dot_nt(a, b):
    return lax.dot_general(a, b, (((1,), (1,)), ((), ())), preferred_element_type=F32)


def _dot_tn(a, b):
    return lax.dot_general(a, b, (((0,), (0,)), ((), ())), preferred_element_type=F32)


def _exact_lhs_dot(m_bf16, x):
    x1, x2, x3 = _split3(x)
    return _dot(m_bf16, x1) + _dot(m_bf16, x2) + _dot(m_bf16, x3)


def _exact_rhs_dot(x, m_bf16):
    x1, x2, x3 = _split3(x)
    return _dot(x1, m_bf16) + _dot(x2, m_bf16) + _dot(x3, m_bf16)


def _norm_modulate(x, g, sh, sc):
    ms = jnp.mean(x * x, axis=-1, keepdims=True)
    y = x * lax.rsqrt(ms + EPS) * g
    return y * (1.0 + sc) + sh


def _ada_kernel(c_ref, w_ref, b_ref, o_ref):
    c = c_ref[...]
    ca = c * _sigmoid(c)
    c1, c2, c3 = _split3(ca)
    w1, w2, w3 = _split3(w_ref[...])
    acc = _dot(c1, w1) + _dot(c1, w2) + _dot(c2, w1)
    acc = acc + _dot(c1, w3) + _dot(c2, w2) + _dot(c3, w1)
    o_ref[...] = acc + b_ref[...]


def _ada_call(c, ada_w, ada_b):
    depth, d, n = ada_w.shape
    b = c.shape[0]
    tn = 1536
    assert n % tn == 0
    return pl.pallas_call(
        _ada_kernel,
        grid=(depth, n // tn),
        in_specs=[
            pl.BlockSpec((b, d), lambda l, j: (0, 0)),
            pl.BlockSpec((None, d, tn), lambda l, j: (l, 0, j)),
            pl.BlockSpec((None, 1, tn), lambda l, j: (l, 0, j)),
        ],
        out_specs=pl.BlockSpec((None, b, tn), lambda l, j: (l, 0, j)),
        out_shape=jax.ShapeDtypeStruct((depth, b, n), F32),
        compiler_params=_params("arbitrary", "arbitrary"),
        name="ada_mod",
    )(c, ada_w, ada_b.reshape(depth, 1, n))


def _inproj_kernel(x_ref, sh_ref, sc_ref, g_ref, w_ref, wg_ref, gain_ref, z_ref, gate_ref,
                   *, n_norm, chunk):
    h = _norm_modulate(x_ref[...], g_ref[...], sh_ref[...], sc_ref[...]).astype(BF16)
    n = w_ref.shape[1]
    lane = lax.broadcasted_iota(jnp.int32, (1, LANES), 1)
    lo = lane < HEAD_DIM
    for c0 in range(0, n, chunk):
        zc = _dot(h, w_ref[:, c0:c0 + chunk])
        if c0 < n_norm:
            parts = []
            for s0 in range(0, chunk, LANES):
                zs = zc[:, s0:s0 + LANES]
                sq = zs * zs
                s_lo = jnp.sum(jnp.where(lo, sq, 0.0), axis=-1, keepdims=True)
                s_hi = jnp.sum(jnp.where(lo, 0.0, sq), axis=-1, keepdims=True)
                r_lo = lax.rsqrt(s_lo * (1.0 / HEAD_DIM) + EPS)
                r_hi = lax.rsqrt(s_hi * (1.0 / HEAD_DIM) + EPS)
                parts.append(zs * jnp.where(lo, r_lo, r_hi))
            zc = jnp.concatenate(parts, axis=-1) * gain_ref[:, c0:c0 + chunk]
        z_ref[:, c0:c0 + chunk] = zc.astype(BF16)
    gate_ref[...] = _dot(h, wg_ref[...])


def _inproj_call(x, shift, scale, g, w_all, layer, n, wg, gain, n_norm, name):
    b, s, d = x.shape
    tm = min(ROW_TILE, s)
    chunk = 512
    assert s % tm == 0 and n % chunk == 0 and n_norm % chunk == 0
    kern = functools.partial(_inproj_kernel, n_norm=n_norm, chunk=chunk)
    return pl.pallas_call(
        kern,
        grid=(b, s // tm),
        in_specs=[
            pl.BlockSpec((None, tm, d), lambda i, j: (i, j, 0)),
            pl.BlockSpec((None, 1, d), lambda i, j: (i, 0, 0)),
            pl.BlockSpec((None, 1, d), lambda i, j: (i, 0, 0)),
            pl.BlockSpec((1, d), lambda i, j: (0, 0)),
            pl.BlockSpec((None, d, n), lambda i, j: (layer, 0, 0)),
            pl.BlockSpec((d, LANES), lambda i, j: (0, 0)),
            pl.BlockSpec((1, n), lambda i, j: (0, 0)),
        ],
        out_specs=[
            pl.BlockSpec((None, tm, n), lambda i, j: (i, j, 0)),
            pl.BlockSpec((None, tm, LANES), lambda i, j: (i, j, 0)),
        ],
        out_shape=[
            jax.ShapeDtypeStruct((b, s, n), BF16),
            jax.ShapeDtypeStruct((b, s, LANES), F32),
        ],
        compiler_params=_params("arbitrary", "arbitrary"),
        name=name,
    )(x, shift, scale, g, w_all, wg, gain)


def _lane_ids():
    return lax.broadcasted_iota(jnp.int32, (1, LANES), 1)


def _place_pieces(pieces, first_lane):
    lane = _lane_ids()
    out = jnp.zeros(pieces[0].shape, F32)
    for j, p in enumerate(pieces):
        out = jnp.where(lane == first_lane + j, p.astype(F32), out)
    return out


def _pack_keys(k_ref, v_ref, extras_fn, ka_ref, kb_ref, vt_refs):
    s = k_ref.shape[0]
    rows = min(PACK_ROWS, s)
    lo = _lane_ids() < HEAD_DIM

    def body(c, carry):
        r0 = pl.multiple_of(c * rows, rows)
        k2 = k_ref[pl.ds(r0, rows), :].astype(F32)
        ea, eb = extras_fn(r0, rows)
        ka_ref[pl.ds(r0, rows), :] = jnp.where(lo, k2, ea).astype(BF16)
        kb_ref[pl.ds(r0, rows), :] = jnp.where(lo, eb, k2).astype(BF16)
        vt = v_ref[pl.ds(r0, rows), :].astype(F32).T.astype(BF16)
        for ref, c0, width in vt_refs:
            ref[0:width, pl.ds(r0, rows)] = vt[c0:c0 + width, :]
            ref[width:width + ONES_ROWS, pl.ds(r0, rows)] = jnp.ones((ONES_ROWS, rows), BF16)
        return carry

    lax.fori_loop(0, s // rows, body, 0)


def _pack_queries(q2, factors):
    lane = _lane_ids()
    q2 = q2.astype(F32)
    lo = lane < HEAD_DIM
    fac_a = _place_pieces(factors, HEAD_DIM)
    fac_b = _place_pieces(factors, 0)
    return jnp.where(lo, q2, fac_a).astype(BF16), jnp.where(lo, fac_b, q2).astype(BF16)


def _init_stats(m_ref, acc_ref):
    m_ref[...] = jnp.full(m_ref.shape, NEG, F32)
    acc_ref[...] = jnp.zeros(acc_ref.shape, F32)


def _softmax_step(s, tile_max, vt, m_ref, acc_ref):
    m_old = m_ref[...]
    m_new = jnp.maximum(m_old, tile_max)
    alpha = jnp.exp2(m_old - m_new)
    p = jnp.exp2(s - m_new)
    acc_ref[...] = alpha * acc_ref[...] + _dot(vt, p.astype(BF16))
    m_ref[...] = m_new


def _attend(i, t, streams, s_buf, mx_buf):
    tk = ATTN_KEYS
    n = (i * t + t + tk - 1) // tk

    def qk_stage(step, slot):
        k0 = pl.multiple_of(step * tk, tk)
        for idx, (q, k_ref, _, _, _) in enumerate(streams):
            s = _dot_nt(k_ref[pl.ds(k0, tk), :], q)
            s_buf[slot][idx] = s
            mx_buf[slot][idx] = jnp.max(s, axis=0, keepdims=True)

    def softmax_stage(step, slot, last):
        k0 = pl.multiple_of(step * tk, tk)
        for idx, (_, _, vt, add, stats) in enumerate(streams):
            s = s_buf[slot][idx]
            if last:
                s = s + add
                tile_max = jnp.max(s, axis=0, keepdims=True)
            else:
                tile_max = mx_buf[slot][idx]
            _softmax_step(s, tile_max, vt(k0, tk), *stats)

    qk_stage(0, 0)
    pairs = (n - 1) // 2

    def body(p, carry):
        qk_stage(2 * p + 1, 1)
        softmax_stage(2 * p, 0, False)
        qk_stage(2 * p + 2, 0)
        softmax_stage(2 * p + 1, 1, False)
        return carry

    lax.fori_loop(0, pairs, body, 0)

    @pl.when(n - 1 > 2 * pairs)
    def _():
        qk_stage(n - 1, 1)
        softmax_stage(n - 2, 0, False)
        softmax_stage(n - 1, 1, True)

    @pl.when(n - 1 == 2 * pairs)
    def _():
        softmax_stage(n - 1, 0, True)


def _last_step_offset(i, t):
    tk = ATTN_KEYS
    n = (i * t + t + tk - 1) // tk
    return i * t - (n - 1) * tk


def _lane_block(ref, h):
    return ref.at[:, pl.ds(h * LANES, LANES)]


def _da_kernel(slope_ref, q_ref, k_ref, v_ref, lam_ref, sg_ref, o_ref,
               ka_ref, kb_ref, vt_ref, m_ref, acc_ref, s0_ref, s1_ref, mx0_ref, mx1_ref,
               *, t, hb, lambda_init):
    dv = LANES
    g = pl.program_id(1)
    i = pl.program_id(2)
    slopes2 = [slope_ref[g * hb + h] * LOG2E for h in range(hb)]

    @pl.when(i == 0)
    def _():
        def extras(r0, rows):
            pos = lax.broadcasted_iota(jnp.int32, (rows, LANES), 0) + r0
            hi = jnp.right_shift(pos, CHUNK_SHIFT).astype(F32)
            lo = jnp.bitwise_and(pos, CHUNK - 1).astype(F32)
            lane = _lane_ids()

            def place(first):
                in_hi = (lane >= first) & (lane < first + BIAS_PIECES)
                in_lo = (lane >= first + BIAS_PIECES) & (lane < first + 2 * BIAS_PIECES)
                return jnp.where(in_hi, hi, jnp.where(in_lo, lo, 0.0))
            return place(HEAD_DIM), place(0)
        for h in range(hb):
            _pack_keys(_lane_block(k_ref, h), _lane_block(v_ref, h), extras,
                       ka_ref.at[h], kb_ref.at[h], [(vt_ref.at[h], 0, dv)])

    key = lax.broadcasted_iota(jnp.int32, (ATTN_KEYS, t), 0)
    qry = lax.broadcasted_iota(jnp.int32, (ATTN_KEYS, t), 1) + _last_step_offset(i, t)
    allowed = jnp.right_shift(key, CHUNK_SHIFT) <= jnp.right_shift(qry, CHUNK_SHIFT)
    above = jnp.maximum(key - qry, 0).astype(F32)

    streams = []
    for h in range(hb):
        slope_pieces = [p.astype(F32) for p in _split3(jnp.full((1, LANES), slopes2[h], F32))]
        factors = [p * float(CHUNK) for p in slope_pieces] + slope_pieces
        qa, qb = _pack_queries(q_ref[:, h * LANES:(h + 1) * LANES], factors)
        diag_add = jnp.where(allowed, (-2.0 * slopes2[h]) * above, NEG)
        vt = lambda k0, size, h=h: vt_ref[h, :, pl.ds(k0, size)]
        for m, (q, k_sc) in enumerate(((qa, ka_ref), (qb, kb_ref))):
            stats = (m_ref.at[2 * h + m], acc_ref.at[2 * h + m])
            _init_stats(*stats)
            streams.append((q, k_sc.at[h], vt, diag_add, stats))
    _attend(i, t, streams, (s0_ref, s1_ref), (mx0_ref, mx1_ref))

    lp = lam_ref[...]
    lam = (jnp.exp(jnp.sum(lp[0:1] * lp[1:2], axis=-1, keepdims=True))
           - jnp.exp(jnp.sum(lp[2:3] * lp[3:4], axis=-1, keepdims=True)) + lambda_init)
    for h in range(hb):
        pa = acc_ref[2 * h]
        pb = acc_ref[2 * h + 1]
        out = pa[0:dv] / pa[dv:dv + 1] - lam * (pb[0:dv] / pb[dv:dv + 1])
        ms = jnp.mean(out * out, axis=0, keepdims=True)
        out = out * lax.rsqrt(ms + EPS) * (sg_ref[...] * (1.0 - lambda_init))
        o_ref[:, h * LANES:(h + 1) * LANES] = out.T.astype(BF16)


def _da_call(z, slopes, lam_p, subln_g, heads, lambda_init):
    b, s, _ = z.shape
    t = min(ATTN_TILE, s)
    assert s % ATTN_KEYS == 0 and t % CHUNK == 0
    assert s <= CHUNK * 256, "key position // CHUNK must stay exact in bf16"
    hb = min(ATTN_BLOCKS, heads)
    groups = heads // hb
    assert heads % hb == 0
    kern = functools.partial(_da_kernel, t=t, hb=hb, lambda_init=lambda_init)
    w = hb * LANES
    return pl.pallas_call(
        kern,
        grid_spec=pltpu.PrefetchScalarGridSpec(
            num_scalar_prefetch=1,
            grid=(b, groups, s // t),
            in_specs=[
                pl.BlockSpec((None, t, w), lambda bi, g, i, sl: (bi, i, g)),
                pl.BlockSpec((None, s, w), lambda bi, g, i, sl: (bi, 0, groups + g)),
                pl.BlockSpec((None, s, w), lambda bi, g, i, sl: (bi, 0, 2 * groups + g)),
                pl.BlockSpec((4, HEAD_DIM), lambda bi, g, i, sl: (0, 0)),
                pl.BlockSpec((LANES, 1), lambda bi, g, i, sl: (0, 0)),
            ],
            out_specs=pl.BlockSpec((None, t, w), lambda bi, g, i, sl: (bi, i, g)),
            scratch_shapes=[
                pltpu.VMEM((hb, s, LANES), BF16), pltpu.VMEM((hb, s, LANES), BF16),
                pltpu.VMEM((hb, LANES + ONES_ROWS, s), BF16),
                pltpu.VMEM((2 * hb, 1, t), F32),
                pltpu.VMEM((2 * hb, LANES + ONES_ROWS, t), F32),
                pltpu.VMEM((2 * hb, ATTN_KEYS, t), F32), pltpu.VMEM((2 * hb, ATTN_KEYS, t), F32),
                pltpu.VMEM((2 * hb, 1, t), F32), pltpu.VMEM((2 * hb, 1, t), F32),
            ],
        ),
        out_shape=jax.ShapeDtypeStruct((b, s, heads * LANES), BF16),
        compiler_params=_params("arbitrary", "arbitrary", "arbitrary"),
        name="diff_attention",
    )(slopes, z, z, z, lam_p, subln_g.reshape(LANES, 1))


def _fox_gate_kernel(g_ref, b_ref, o_ref, *, chunk):
    s = g_ref.shape[0]
    r = lax.broadcasted_iota(jnp.int32, (chunk, chunk), 0)
    c = lax.broadcasted_iota(jnp.int32, (chunk, chunk), 1)
    tri = (c <= r).astype(BF16)
    carry = jnp.zeros((1, LANES), F32)
    for c0 in range(0, s, chunk):
        ls = _log_sigmoid(g_ref[c0:c0 + chunk, :] + b_ref[...])
        cs = _exact_lhs_dot(tri, ls) + carry
        carry = cs[chunk - 1:chunk, :]
        o_ref[c0:c0 + chunk, :] = cs * (-LOG2E)


def _fox_gate_call(gates, bias_row):
    b, s, _ = gates.shape
    chunk = min(GATE_CHUNK, s)
    assert s % chunk == 0
    kern = functools.partial(_fox_gate_kernel, chunk=chunk)
    return pl.pallas_call(
        kern,
        grid=(b,),
        in_specs=[
            pl.BlockSpec((None, s, LANES), lambda i: (i, 0, 0)),
            pl.BlockSpec((1, LANES), lambda i: (0, 0)),
        ],
        out_specs=pl.BlockSpec((None, s, LANES), lambda i: (i, 0, 0)),
        out_shape=jax.ShapeDtypeStruct((b, s, LANES), F32),
        compiler_params=_params("arbitrary"),
        name="fox_gate_cumsum",
    )(gates, bias_row)


def _fox_kernel(q_ref, k_ref, v_ref, f_ref, o_ref,
                ka_ref, kb_ref, vt_ref, m_ref, acc_ref, s0_ref, s1_ref, mx0_ref, mx1_ref, *, t, hb):
    dv = HEAD_DIM
    g = pl.program_id(1)
    i = pl.program_id(2)

    @pl.when(i == 0)
    def _():
        n = BIAS_PIECES * LANES
        src = lax.broadcasted_iota(jnp.int32, (n, LANES), 0)
        dst = lax.broadcasted_iota(jnp.int32, (n, LANES), 1)
        piece = jnp.right_shift(src, LANES.bit_length() - 1)
        head = src - piece * LANES
        for h in range(hb):
            pair = g * hb + h
            place_a = ((head == 2 * pair) & (dst == HEAD_DIM + piece)).astype(BF16)
            place_b = ((head == 2 * pair + 1) & (dst == piece)).astype(BF16)
            place = jnp.concatenate([place_a, place_b], axis=-1)

            def extras(r0, rows, place=place):
                pieces = jnp.concatenate(_split3(f_ref[pl.ds(r0, rows), :]), axis=-1)
                both = _dot(pieces, place)
                return both[:, 0:LANES], both[:, LANES:2 * LANES]
            _pack_keys(_lane_block(k_ref, h), _lane_block(v_ref, h), extras,
                       ka_ref.at[h], kb_ref.at[h],
                       [(vt_ref.at[2 * h + m], m * dv, dv) for m in range(2)])

    key = lax.broadcasted_iota(jnp.int32, (ATTN_KEYS, t), 0)
    qry = lax.broadcasted_iota(jnp.int32, (ATTN_KEYS, t), 1) + _last_step_offset(i, t)
    diag_add = jnp.where(key <= qry, 0.0, NEG)

    streams = []
    ones = [jnp.ones((1, LANES), F32)] * BIAS_PIECES
    for h in range(hb):
        qa, qb = _pack_queries(q_ref[:, h * LANES:(h + 1) * LANES], ones)
        for m, (q, k_sc) in enumerate(((qa, ka_ref), (qb, kb_ref))):
            vt = lambda k0, size, n=2 * h + m: vt_ref[n, :, pl.ds(k0, size)]
            stats = (m_ref.at[2 * h + m], acc_ref.at[2 * h + m])
            _init_stats(*stats)
            streams.append((q, k_sc.at[h], vt, diag_add, stats))
    _attend(i, t, streams, (s0_ref, s1_ref), (mx0_ref, mx1_ref))

    for h in range(hb):
        pa = acc_ref[2 * h]
        pb = acc_ref[2 * h + 1]
        out = jnp.concatenate([pa[0:dv] / pa[dv:dv + 1], pb[0:dv] / pb[dv:dv + 1]], axis=0)
        o_ref[:, h * LANES:(h + 1) * LANES] = out.T.astype(BF16)


def _fox_call(z, fneg, heads):
    b, s, _ = z.shape
    pairs = heads // 2
    t = min(ATTN_TILE, s)
    assert s % ATTN_KEYS == 0
    hb = min(ATTN_BLOCKS, pairs)
    groups = pairs // hb
    assert pairs % hb == 0
    kern = functools.partial(_fox_kernel, t=t, hb=hb)
    w = hb * LANES
    return pl.pallas_call(
        kern,
        grid=(b, groups, s // t),
        in_specs=[
            pl.BlockSpec((None, t, w), lambda bi, g, i: (bi, i, g)),
            pl.BlockSpec((None, s, w), lambda bi, g, i: (bi, 0, groups + g)),
            pl.BlockSpec((None, s, w), lambda bi, g, i: (bi, 0, 2 * groups + g)),
            pl.BlockSpec((None, s, LANES), lambda bi, g, i: (bi, 0, 0)),
        ],
        out_specs=pl.BlockSpec((None, t, w), lambda bi, g, i: (bi, i, g)),
        out_shape=jax.ShapeDtypeStruct((b, s, pairs * LANES), BF16),
        scratch_shapes=[
            pltpu.VMEM((hb, s, LANES), BF16), pltpu.VMEM((hb, s, LANES), BF16),
            pltpu.VMEM((2 * hb, HEAD_DIM + ONES_ROWS, s), BF16),
            pltpu.VMEM((2 * hb, 1, t), F32),
            pltpu.VMEM((2 * hb, HEAD_DIM + ONES_ROWS, t), F32),
            pltpu.VMEM((2 * hb, ATTN_KEYS, t), F32), pltpu.VMEM((2 * hb, ATTN_KEYS, t), F32),
            pltpu.VMEM((2 * hb, 1, t), F32), pltpu.VMEM((2 * hb, 1, t), F32),
        ],
        compiler_params=_params("arbitrary", "arbitrary", "arbitrary"),
        name="forgetting_attention",
    )(z, z, z, fneg)


def _mlstm_kernel(q_ref, k_ref, v_ref, o_ref, g_ref, gb_ref, cw_ref, cb_ref, y_ref,
                  halo_ref, ct_ref, m_ref, *, heads, taps):
    ci = pl.program_id(1)
    lc, width = q_ref.shape
    d = width // heads
    pad = halo_ref.shape[0]

    @pl.when(ci == 0)
    def _():
        halo_ref[...] = jnp.zeros(halo_ref.shape, F32)
        ct_ref[...] = jnp.zeros(ct_ref.shape, F32)
        m_ref[...] = jnp.zeros(m_ref.shape, F32)

    r = lax.broadcasted_iota(jnp.int32, (lc, lc), 0)
    c = lax.broadcasted_iota(jnp.int32, (lc, lc), 1)
    causal = c <= r

    xb = jnp.concatenate([q_ref[...], k_ref[...]], axis=-1)
    xf = xb.astype(F32)
    conv = cb_ref[...] + xf * cw_ref[taps - 1:taps, :]
    for sft in range(1, taps):
        shifted = _dot((r - c == sft).astype(BF16), xb)
        conv = conv + shifted * cw_ref[taps - 1 - sft:taps - sft, :]
    edge = jnp.concatenate([halo_ref[...], xf[0:pad, :]], axis=0)
    first = cb_ref[...]
    for j in range(taps):
        off = pad - (taps - 1) + j
        first = first + edge[off:off + pad, :] * cw_ref[j:j + 1, :]
    conv = jnp.concatenate([first, conv[pad:, :]], axis=0)
    halo_ref[...] = xf[lc - pad:, :]
    qk = conv * _sigmoid(conv)

    gcol = g_ref[...] + gb_ref[...]
    grow = gcol.T[:2 * heads, :]
    a_cols = _exact_lhs_dot(causal.astype(BF16), _log_sigmoid(gcol))
    a_rows = _exact_rhs_dot(_log_sigmoid(grow), (r <= c).astype(BF16))

    b_half = grow[0:heads, :] - a_rows[heads:2 * heads, :]
    b_rows = jnp.concatenate([b_half, b_half], axis=0)
    x_cols = jnp.maximum(_prefix_max_lanes(b_rows), m_ref[:, 0:1]).T
    ones = jnp.ones((lc, d), BF16)

    for hd in range(heads):
        q = qk[:, hd * d:(hd + 1) * d]
        k = qk[:, width + hd * d:width + (hd + 1) * d] * (d ** -0.5)
        v = v_ref[:, hd * d:(hd + 1) * d]
        qb = q.astype(BF16)
        kb = k.astype(BF16)
        ig_c = gcol[:, hd:hd + 1]
        a_c = a_cols[:, heads + hd:heads + hd + 1]
        a_last = a_c[lc - 1:lc, :]
        x_c = x_cols[:, hd:hd + 1]
        ct = ct_ref[hd]
        m_prev = m_ref[hd:hd + 1, 0:1]

        decay_mat = jnp.exp2(jnp.where(causal, b_rows[hd:hd + 1, :] * LOG2E - x_c * LOG2E, NEG))
        sqk = (_dot_nt(qb, kb) * decay_mat).astype(BF16)
        intra = _dot(sqk, jnp.concatenate([v, ones], axis=-1))
        inter = jnp.exp(m_prev - x_c) * _dot(qb, ct.astype(BF16))
        num = inter[:, 0:d] + intra[:, 0:d]
        den = inter[:, d:d + 1] + intra[:, d:d + 1]
        hval = num / jnp.maximum(jnp.abs(den), jnp.exp(-(a_c + x_c)))
        og = o_ref[:, hd * d:(hd + 1) * d].astype(F32)
        y_ref[:, hd * d:(hd + 1) * d] = (_sigmoid(og) * hval).astype(BF16)

        g_c = a_last - a_c + ig_c
        m_loc = jnp.max(g_c, axis=0, keepdims=True)
        w_c = jnp.exp(g_c - m_loc)
        m_new = jnp.maximum(a_last + m_prev, m_loc)
        decay = jnp.exp(a_last + m_prev - m_new)
        s_loc = jnp.exp(m_loc - m_new)
        wv = jnp.concatenate([w_c * v.astype(F32), jnp.broadcast_to(w_c, (lc, d))], axis=-1)
        ct_ref[hd] = decay * ct + s_loc * _dot_tn(kb, wv.astype(BF16))
        m_ref[hd:hd + 1, :] = jnp.broadcast_to(m_new, (1, LANES))
        m_ref[heads + hd:heads + hd + 1, :] = jnp.broadcast_to(m_new, (1, LANES))


def _mlstm_call(z, gates, gate_bias, conv_w, conv_b, heads, col0):
    b, s, _ = z.shape
    width = conv_w.shape[1] // 2
    lc = min(ML_CHUNK, s)
    assert s % lc == 0 and col0 % width == 0 and width // heads == LANES
    cb = col0 // width
    taps = conv_w.shape[0]
    kern = functools.partial(_mlstm_kernel, heads=heads, taps=taps)
    zspec = lambda off: pl.BlockSpec((None, lc, width), lambda bi, ci: (bi, ci, cb + off))
    return pl.pallas_call(
        kern,
        grid=(b, s // lc),
        in_specs=[
            zspec(0), zspec(1), zspec(2), zspec(3),
            pl.BlockSpec((None, lc, LANES), lambda bi, ci: (bi, ci, 0)),
            pl.BlockSpec((1, LANES), lambda bi, ci: (0, 0)),
            pl.BlockSpec((taps, 2 * width), lambda bi, ci: (0, 0)),
            pl.BlockSpec((1, 2 * width), lambda bi, ci: (0, 0)),
        ],
        out_specs=pl.BlockSpec((None, lc, width), lambda bi, ci: (bi, ci, 0)),
        out_shape=jax.ShapeDtypeStruct((b, s, width), BF16),
        scratch_shapes=[
            pltpu.VMEM((8, 2 * width), F32),
            pltpu.VMEM((heads, LANES, 2 * LANES), F32),
            pltpu.VMEM((2 * heads, LANES), F32),
        ],
        compiler_params=_params("arbitrary", "arbitrary"),
        name="mlstm",
    )(z, z, z, z, gates, gate_bias, conv_w, conv_b)


def _mix_ffn_kernel(*refs, n_in, chunk):
    y_refs = refs[:n_in]
    wo_refs = refs[n_in:2 * n_in]
    (x_ref, gm_ref, sh_ref, sc_ref, g_ref, gate_ref, w1_ref, w3_ref, w2_ref, o_ref) = refs[2 * n_in:]
    mix = _dot(y_refs[0][...], wo_refs[0][...])
    for y_ref, w_ref in zip(y_refs[1:], wo_refs[1:]):
        mix = mix + _dot(y_ref[...], w_ref[...])
    x = x_ref[...] + gm_ref[...] * mix
    h = _norm_modulate(x, g_ref[...], sh_ref[...], sc_ref[...]).astype(BF16)
    hidden = w1_ref.shape[1]
    acc = None
    for c0 in range(0, hidden, chunk):
        a = _dot(h, w1_ref[:, c0:c0 + chunk])
        bb = _dot(h, w3_ref[:, c0:c0 + chunk])
        u = (a * _sigmoid(a) * bb).astype(BF16)
        part = _dot(u, w2_ref[c0:c0 + chunk, :])
        acc = part if acc is None else acc + part
    o_ref[...] = x + gate_ref[...] * acc


def _mix_ffn_call(ys, wo_all, wo_layer, x, gate_mix, shift, scale, g, gate, w1, w3, w2, layer, name):
    b, s, d = x.shape
    hidden = w1.shape[2]
    tm = min(ROW_TILE, s)
    assert s % tm == 0 and hidden % FFN_CHUNK == 0
    assert all(y.shape[2] == ys[0].shape[2] for y in ys)
    kern = functools.partial(_mix_ffn_kernel, n_in=len(ys), chunk=FFN_CHUNK)
    vec = pl.BlockSpec((None, 1, d), lambda i, j: (i, 0, 0))
    rows = lambda width: pl.BlockSpec((None, tm, width), lambda i, j: (i, j, 0))
    stacked = lambda w: pl.BlockSpec((None,) + w.shape[1:], lambda i, j: (layer, 0, 0))
    wo_specs = [pl.BlockSpec((None, y.shape[2], d), lambda i, j, n=n: (wo_layer, n, 0))
                for n, y in enumerate(ys)]
    return pl.pallas_call(
        kern,
        grid=(b, s // tm),
        in_specs=[rows(y.shape[2]) for y in ys] + wo_specs + [
            rows(d), vec, vec, vec,
            pl.BlockSpec((1, d), lambda i, j: (0, 0)),
            vec, stacked(w1), stacked(w3), stacked(w2),
        ],
        out_specs=rows(d),
        out_shape=jax.ShapeDtypeStruct((b, s, d), F32),
        compiler_params=_params("arbitrary", "arbitrary"),
        name=name,
    )(*ys, *([wo_all] * len(ys)), x, gate_mix, shift, scale, g, gate, w1, w3, w2)


def _pad_cols(w, n):
    return jnp.pad(w, ((0, 0), (0, n - w.shape[1])))


def _gain_row(q_g, k_g, width, total):
    reps = width // HEAD_DIM
    row = jnp.concatenate([jnp.tile(q_g.astype(F32) * (HEAD_DIM ** -0.5 * LOG2E), reps),
                           jnp.tile(k_g.astype(F32), reps),
                           jnp.ones((total - 2 * width,), F32)])
    return row.reshape(1, total)


def kernel(x, c, ada_w, ada_b, norm_mix_g, norm_ffn_g, ab_w_in, ml_b_i, ml_b_f, ml_conv_w, ml_conv_b,
           da_q_g, da_k_g, da_lambda, da_subln_g, ab_w_out, fx_w_in, fx_b_f, fx_q_g, fx_k_g, fx_w_out,
           ffn_w1, ffn_w3, ffn_w2):
    depth = ada_w.shape[0]
    b, s, d = x.shape
    ml_width = ml_conv_w.shape[2] // 2
    ml_heads = ml_b_i.shape[1]
    da_width = (ab_w_in.shape[2] - 4 * ml_width - 2 * ml_heads) // 3
    da_heads = da_width // (2 * HEAD_DIM)
    fx_heads = fx_b_f.shape[1]
    fx_width = fx_heads * HEAD_DIM
    ab_main = 3 * da_width + 4 * ml_width
    fx_main = 3 * fx_width

    mods = _ada_call(c, ada_w, ada_b).reshape(depth, b, 6, 1, d)
    slopes = 2.0 ** (-8.0 * jnp.arange(1, da_heads + 1, dtype=F32) / da_heads)

    ab_w_in_b, fx_w_in_b = ab_w_in.astype(BF16), fx_w_in.astype(BF16)
    ab_w_out_b, fx_w_out_b = ab_w_out.astype(BF16), fx_w_out.astype(BF16)
    ffn_w1_b, ffn_w3_b, ffn_w2_b = ffn_w1.astype(BF16), ffn_w3.astype(BF16), ffn_w2.astype(BF16)

    for l in range(depth):
        sh_m, sc_m, g_m, sh_f, sc_f, g_f = (mods[l, :, t] for t in range(6))
        j = l // 2
        g_mix = norm_mix_g[l].reshape(1, d)
        if l % 2 == 0:
            z, gates = _inproj_call(
                x, sh_m, sc_m, g_mix, ab_w_in_b, j, ab_main,
                _pad_cols(ab_w_in[j][:, ab_main:], LANES).astype(BF16),
                _gain_row(da_q_g[j], da_k_g[j], da_width, ab_main),
                2 * da_width, "inproj_even")
            lambda_init = 0.8 - 0.6 * math.exp(-0.3 * l)
            y_da = _da_call(z, slopes, da_lambda[j], da_subln_g[j], da_heads, lambda_init)
            gate_bias = _pad_cols(jnp.concatenate([ml_b_i[j], ml_b_f[j]]).reshape(1, -1), LANES)
            y_ml = _mlstm_call(z, gates, gate_bias, ml_conv_w[j], ml_conv_b[j].reshape(1, -1),
                               ml_heads, 3 * da_width)
            ys, wo_all, name = [y_da, y_ml], ab_w_out_b, "mix_ffn_even"
        else:
            z, gates = _inproj_call(
                x, sh_m, sc_m, g_mix, fx_w_in_b, j, fx_main,
                _pad_cols(fx_w_in[j][:, fx_main:], LANES).astype(BF16),
                _gain_row(fx_q_g[j], fx_k_g[j], fx_width, fx_main),
                2 * fx_width, "inproj_odd")
            fneg = _fox_gate_call(gates, _pad_cols(fx_b_f[j].reshape(1, -1), LANES))
            ys, wo_all, name = [_fox_call(z, fneg, fx_heads)], fx_w_out_b, "mix_ffn_odd"
        x = _mix_ffn_call(ys, wo_all, j, x, g_m, sh_f, sc_f, norm_ffn_g[l].reshape(1, d), g_f,
                          ffn_w1_b, ffn_w3_b, ffn_w2_b, l, name)
    return x
```

```python
import functools
import math

import jax
import jax.numpy as jnp
from jax import lax
from jax.experimental import pallas as pl
from jax.experimental.pallas import tpu as pltpu

F32 = jnp.float32
BF16 = jnp.bfloat16

EPS = 1e-6
LANES = 128
HEAD_DIM = 64
CHUNK = 64
CHUNK_SHIFT = CHUNK.bit_length() - 1
assert CHUNK == 1 << CHUNK_SHIFT
NEG = -1e30
LOG2E = math.log2(math.e)
BIAS_PIECES = 3
ONES_ROWS = 16

VMEM_LIMIT = 56 * 1024 * 1024

ROW_TILE = 512
ATTN_VMEM_LIMIT = VMEM_LIMIT
ATTN_TILE = 256
ATTN_KEYS = 512
ATTN_BLOCKS = 4
PACK_ROWS = 512
ML_CHUNK = 256
ML_CHUNKS_PER_STEP = 2
GATE_CHUNK = 256
FFN_CHUNK = 256


def _params(*sem, vmem_limit=VMEM_LIMIT):
    return pltpu.CompilerParams(dimension_semantics=sem, vmem_limit_bytes=vmem_limit)


def _log_sigmoid(x):
    return jnp.minimum(x, 0.0) - jnp.log1p(jnp.exp(-jnp.abs(x)))


def _sigmoid(x):
    return 0.5 * jnp.tanh(0.5 * x) + 0.5


def _prefix_max_lanes(x):
    axis = x.ndim - 1
    lane = lax.broadcasted_iota(jnp.int32, x.shape, axis)
    shift = 1
    while shift < x.shape[axis]:
        x = jnp.maximum(x, jnp.where(lane >= shift, pltpu.roll(x, shift, axis=axis), NEG))
        shift *= 2
    return x


def _split3(x):
    x1 = x.astype(BF16)
    r1 = x - x1.astype(F32)
    x2 = r1.astype(BF16)
    x3 = (r1 - x2.astype(F32)).astype(BF16)
    return x1, x2, x3


def _dot(a, b):
    return jnp.dot(a, b, preferred_element_type=F32)


def _dot_nt(a, b):
    return lax.dot_general(a, b, (((1,), (1,)), ((), ())), preferred_element_type=F32)


def _dot_tn(a, b):
    return lax.dot_general(a, b, (((0,), (0,)), ((), ())), preferred_element_type=F32)


def _exact_lhs_dot(m_bf16, x):
    x1, x2, x3 = _split3(x)
    return _dot(m_bf16, x1) + _dot(m_bf16, x2) + _dot(m_bf16, x3)


def _exact_rhs_dot(x, m_bf16):
    x1, x2, x3 = _split3(x)
    return _dot(x1, m_bf16) + _dot(x2, m_bf16) + _dot(x3, m_bf16)


def _norm_modulate(x, g, sh, sc):
    ms = jnp.mean(x * x, axis=-1, keepdims=True)
    y = x * lax.rsqrt(ms + EPS) * g
    return y * (1.0 + sc) + sh


def _ada_kernel(c_ref, w_ref, b_ref, o_ref):
    c = c_ref[...]
    ca = c * _sigmoid(c)
    c1, c2, c3 = _split3(ca)
    w1, w2, w3 = _split3(w_ref[...])
    acc = _dot(c1, w1) + _dot(c1, w2) + _dot(c2, w1)
    acc = acc + _dot(c1, w3) + _dot(c2, w2) + _dot(c3, w1)
    o_ref[...] = acc + b_ref[...]


def _ada_call(c, ada_w, ada_b):
    depth, d, n = ada_w.shape
    b = c.shape[0]
    tn = 1536
    assert n % tn == 0
    return pl.pallas_call(
        _ada_kernel,
        grid=(depth, n // tn),
        in_specs=[
            pl.BlockSpec((b, d), lambda l, j: (0, 0)),
            pl.BlockSpec((None, d, tn), lambda l, j: (l, 0, j)),
            pl.BlockSpec((None, 1, tn), lambda l, j: (l, 0, j)),
        ],
        out_specs=pl.BlockSpec((None, b, tn), lambda l, j: (l, 0, j)),
        out_shape=jax.ShapeDtypeStruct((depth, b, n), F32),
        compiler_params=_params("arbitrary", "arbitrary"),
        name="ada_mod",
    )(c, ada_w, ada_b.reshape(depth, 1, n))


def _inproj_kernel(x_ref, sh_ref, sc_ref, g_ref, w_ref, wg_ref, gain_ref, z_ref, gate_ref,
                   *, n_norm, chunk):
    h = _norm_modulate(x_ref[...], g_ref[...], sh_ref[...], sc_ref[...]).astype(BF16)
    n = w_ref.shape[1]
    lane = lax.broadcasted_iota(jnp.int32, (1, LANES), 1)
    lo = lane < HEAD_DIM
    for c0 in range(0, n, chunk):
        zc = _dot(h, w_ref[:, c0:c0 + chunk])
        if c0 < n_norm:
            parts = []
            for s0 in range(0, chunk, LANES):
                zs = zc[:, s0:s0 + LANES]
                sq = zs * zs
                s_lo = jnp.sum(jnp.where(lo, sq, 0.0), axis=-1, keepdims=True)
                s_hi = jnp.sum(jnp.where(lo, 0.0, sq), axis=-1, keepdims=True)
                r_lo = lax.rsqrt(s_lo * (1.0 / HEAD_DIM) + EPS)
                r_hi = lax.rsqrt(s_hi * (1.0 / HEAD_DIM) + EPS)
                parts.append(zs * jnp.where(lo, r_lo, r_hi))
            zc = jnp.concatenate(parts, axis=-1) * gain_ref[:, c0:c0 + chunk]
        z_ref[:, c0:c0 + chunk] = zc.astype(BF16)
    gate_ref[...] = _dot(h, wg_ref[...])


def _inproj_call(x, shift, scale, g, w_all, layer, n, wg, gain, n_norm, name):
    b, s, d = x.shape
    tm = min(ROW_TILE, s)
    chunk = 512
    assert s % tm == 0 and n % chunk == 0 and n_norm % chunk == 0
    kern = functools.partial(_inproj_kernel, n_norm=n_norm, chunk=chunk)
    return pl.pallas_call(
        kern,
        grid=(b, s // tm),
        in_specs=[
            pl.BlockSpec((None, tm, d), lambda i, j: (i, j, 0)),
            pl.BlockSpec((None, 1, d), lambda i, j: (i, 0, 0)),
            pl.BlockSpec((None, 1, d), lambda i, j: (i, 0, 0)),
            pl.BlockSpec((1, d), lambda i, j: (0, 0)),
            pl.BlockSpec((None, d, n), lambda i, j: (layer, 0, 0)),
            pl.BlockSpec((d, LANES), lambda i, j: (0, 0)),
            pl.BlockSpec((1, n), lambda i, j: (0, 0)),
        ],
        out_specs=[
            pl.BlockSpec((None, tm, n), lambda i, j: (i, j, 0)),
            pl.BlockSpec((None, tm, LANES), lambda i, j: (i, j, 0)),
        ],
        out_shape=[
            jax.ShapeDtypeStruct((b, s, n), BF16),
            jax.ShapeDtypeStruct((b, s, LANES), F32),
        ],
        compiler_params=_params("arbitrary", "arbitrary"),
        name=name,
    )(x, shift, scale, g, w_all, wg, gain)


def _lane_ids():
    return lax.broadcasted_iota(jnp.int32, (1, LANES), 1)


def _place_pieces(pieces, first_lane):
    lane = _lane_ids()
    out = jnp.zeros(pieces[0].shape, F32)
    for j, p in enumerate(pieces):
        out = jnp.where(lane == first_lane + j, p.astype(F32), out)
    return out


def _pack_keys(k_ref, v_ref, extras_fn, ka_ref, kb_ref, vt_refs):
    s = k_ref.shape[0]
    rows = min(PACK_ROWS, s)
    keep_lo = jnp.where(_lane_ids() < HEAD_DIM, 1.0, 0.0).astype(BF16)
    keep_hi = jnp.where(_lane_ids() < HEAD_DIM, 0.0, 1.0).astype(BF16)

    def body(c, carry):
        r0 = pl.multiple_of(c * rows, rows)
        k2 = k_ref[pl.ds(r0, rows), :]
        ea, eb = extras_fn(r0, rows)
        ka_ref[pl.ds(r0, rows), :] = k2 * keep_lo + ea.astype(BF16)
        kb_ref[pl.ds(r0, rows), :] = k2 * keep_hi + eb.astype(BF16)
        vt = v_ref[pl.ds(r0, rows), :].T
        for ref, c0, width in vt_refs:
            ref[0:width, pl.ds(r0, rows)] = vt[c0:c0 + width, :]
            ref[width:width + ONES_ROWS, pl.ds(r0, rows)] = jnp.ones((ONES_ROWS, rows), BF16)
        return carry

    steps = s // rows
    lax.fori_loop(0, steps, body, 0, unroll=2 if steps % 2 == 0 else 1)


def _pack_queries(q2, factors):
    lane = _lane_ids()
    q2 = q2.astype(F32)
    lo = lane < HEAD_DIM
    fac_a = _place_pieces(factors, HEAD_DIM)
    fac_b = _place_pieces(factors, 0)
    return jnp.where(lo, q2, fac_a).astype(BF16), jnp.where(lo, fac_b, q2).astype(BF16)


def _init_stats(m_ref, acc_ref):
    m_ref[...] = jnp.full(m_ref.shape, NEG, F32)
    acc_ref[...] = jnp.zeros(acc_ref.shape, F32)


def _softmax_step(s, tile_max, vt, m_ref, acc_ref):
    m_old = m_ref[...]
    m_new = jnp.maximum(m_old, tile_max)
    alpha = jnp.exp2(m_old - m_new)
    p = jnp.exp2(s - m_new)
    acc_ref[...] = alpha * acc_ref[...] + _dot(vt, p.astype(BF16))
    m_ref[...] = m_new


def _attend(i, t, streams, s_buf, mx_buf, make_addend):
    tk = ATTN_KEYS
    n = (i * t + t + tk - 1) // tk

    def qk_stage(step, slot):
        k0 = pl.multiple_of(step * tk, tk)
        for idx, (q, k_ref, _, _, _) in enumerate(streams):
            s = _dot_nt(k_ref[pl.ds(k0, tk), :], q)
            s_buf[slot][idx] = s
            mx_buf[slot][idx] = jnp.max(s, axis=0, keepdims=True)

    def softmax_stage(step, slot, last):
        k0 = pl.multiple_of(step * tk, tk)
        addends = {}
        for idx, (_, _, vt, add_key, stats) in enumerate(streams):
            s = s_buf[slot][idx]
            if last:
                if add_key not in addends:
                    addends[add_key] = make_addend(add_key)
                s = s + addends[add_key]
                tile_max = jnp.max(s, axis=0, keepdims=True)
            else:
                tile_max = mx_buf[slot][idx]
            _softmax_step(s, tile_max, vt(k0, tk), *stats)

    qk_stage(0, 0)
    pairs = (n - 1) // 2

    def body(p, carry):
        qk_stage(2 * p + 1, 1)
        softmax_stage(2 * p, 0, False)
        qk_stage(2 * p + 2, 0)
        softmax_stage(2 * p + 1, 1, False)
        return carry

    lax.fori_loop(0, pairs, body, 0)

    @pl.when(n - 1 > 2 * pairs)
    def _():
        qk_stage(n - 1, 1)
        softmax_stage(n - 2, 0, False)
        softmax_stage(n - 1, 1, True)

    @pl.when(n - 1 == 2 * pairs)
    def _():
        softmax_stage(n - 1, 0, True)


def _last_step_offset(i, t):
    tk = ATTN_KEYS
    n = (i * t + t + tk - 1) // tk
    return i * t - (n - 1) * tk


def _lane_block(ref, h):
    return ref.at[:, pl.ds(h * LANES, LANES)]


def _da_kernel(slope_ref, q_ref, k_ref, v_ref, lam_ref, sg_ref, o_ref,
               ka_ref, kb_ref, vt_ref, m_ref, acc_ref, s0_ref, s1_ref, mx0_ref, mx1_ref,
               *, t, hb, lambda_init):
    dv = LANES
    g = pl.program_id(1)
    i = pl.program_id(2)
    slopes2 = [slope_ref[g * hb + h] * LOG2E for h in range(hb)]

    @pl.when(i == 0)
    def _():
        def extras(r0, rows):
            pos = lax.broadcasted_iota(jnp.int32, (rows, LANES), 0) + r0
            hi = jnp.right_shift(pos, CHUNK_SHIFT).astype(F32)
            lo = jnp.bitwise_and(pos, CHUNK - 1).astype(F32)
            lane = _lane_ids()

            def place(first):
                in_hi = (lane >= first) & (lane < first + BIAS_PIECES)
                in_lo = (lane >= first + BIAS_PIECES) & (lane < first + 2 * BIAS_PIECES)
                return jnp.where(in_hi, hi, jnp.where(in_lo, lo, 0.0))
            return place(HEAD_DIM), place(0)
        for h in range(hb):
            _pack_keys(_lane_block(k_ref, h), _lane_block(v_ref, h), extras,
                       ka_ref.at[h], kb_ref.at[h], [(vt_ref.at[h], 0, dv)])

    def last_step_addend(h):
        key = lax.broadcasted_iota(jnp.int32, (ATTN_KEYS, t), 0)
        qry = lax.broadcasted_iota(jnp.int32, (ATTN_KEYS, t), 1) + _last_step_offset(i, t)
        allowed = jnp.right_shift(key, CHUNK_SHIFT) <= jnp.right_shift(qry, CHUNK_SHIFT)
        above = jnp.maximum(key - qry, 0).astype(F32)
        return jnp.where(allowed, (-2.0 * slopes2[h]) * above, NEG)

    streams = []
    for h in range(hb):
        slope_pieces = [p.astype(F32) for p in _split3(jnp.full((1, LANES), slopes2[h], F32))]
        factors = [p * float(CHUNK) for p in slope_pieces] + slope_pieces
        qa, qb = _pack_queries(q_ref[:, h * LANES:(h + 1) * LANES], factors)
        vt = lambda k0, size, h=h: vt_ref[h, :, pl.ds(k0, size)]
        for m, (q, k_sc) in enumerate(((qa, ka_ref), (qb, kb_ref))):
            stats = (m_ref.at[2 * h + m], acc_ref.at[2 * h + m])
            _init_stats(*stats)
            streams.append((q, k_sc.at[h], vt, h, stats))
    _attend(i, t, streams, (s0_ref, s1_ref), (mx0_ref, mx1_ref), last_step_addend)

    lp = lam_ref[...]
    lam = (jnp.exp(jnp.sum(lp[0:1] * lp[1:2], axis=-1, keepdims=True))
           - jnp.exp(jnp.sum(lp[2:3] * lp[3:4], axis=-1, keepdims=True)) + lambda_init)
    for h in range(hb):
        pa = acc_ref[2 * h]
        pb = acc_ref[2 * h + 1]
        out = pa[0:dv] / pa[dv:dv + 1] - lam * (pb[0:dv] / pb[dv:dv + 1])
        ms = jnp.mean(out * out, axis=0, keepdims=True)
        out = out * lax.rsqrt(ms + EPS) * (sg_ref[...] * (1.0 - lambda_init))
        o_ref[:, h * LANES:(h + 1) * LANES] = out.T.astype(BF16)


def _da_call(z, slopes, lam_p, subln_g, heads, lambda_init):
    b, s, _ = z.shape
    t = min(ATTN_TILE, s)
    assert s % ATTN_KEYS == 0 and t % CHUNK == 0
    assert s <= CHUNK * 256, "key position // CHUNK must stay exact in bf16"
    hb = min(ATTN_BLOCKS, heads)
    groups = heads // hb
    assert heads % hb == 0
    kern = functools.partial(_da_kernel, t=t, hb=hb, lambda_init=lambda_init)
    w = hb * LANES
    return pl.pallas_call(
        kern,
        grid_spec=pltpu.PrefetchScalarGridSpec(
            num_scalar_prefetch=1,
            grid=(b, groups, s // t),
            in_specs=[
                pl.BlockSpec((None, t, w), lambda bi, g, i, sl: (bi, i, g)),
                pl.BlockSpec((None, s, w), lambda bi, g, i, sl: (bi, 0, groups + g)),
                pl.BlockSpec((None, s, w), lambda bi, g, i, sl: (bi, 0, 2 * groups + g)),
                pl.BlockSpec((4, HEAD_DIM), lambda bi, g, i, sl: (0, 0)),
                pl.BlockSpec((LANES, 1), lambda bi, g, i, sl: (0, 0)),
            ],
            out_specs=pl.BlockSpec((None, t, w), lambda bi, g, i, sl: (bi, i, g)),
            scratch_shapes=[
                pltpu.VMEM((hb, s, LANES), BF16), pltpu.VMEM((hb, s, LANES), BF16),
                pltpu.VMEM((hb, LANES + ONES_ROWS, s), BF16),
                pltpu.VMEM((2 * hb, 1, t), F32),
                pltpu.VMEM((2 * hb, LANES + ONES_ROWS, t), F32),
                pltpu.VMEM((2 * hb, ATTN_KEYS, t), F32), pltpu.VMEM((2 * hb, ATTN_KEYS, t), F32),
                pltpu.VMEM((2 * hb, 1, t), F32), pltpu.VMEM((2 * hb, 1, t), F32),
            ],
        ),
        out_shape=jax.ShapeDtypeStruct((b, s, heads * LANES), BF16),
        compiler_params=_params("arbitrary", "arbitrary", "arbitrary", vmem_limit=ATTN_VMEM_LIMIT),
        name="diff_attention",
    )(slopes, z, z, z, lam_p, subln_g.reshape(LANES, 1))


def _fox_gate_kernel(g_ref, b_ref, o_ref, *, chunk):
    s = g_ref.shape[0]
    r = lax.broadcasted_iota(jnp.int32, (chunk, chunk), 0)
    c = lax.broadcasted_iota(jnp.int32, (chunk, chunk), 1)
    tri = (c <= r).astype(BF16)
    carry = jnp.zeros((1, LANES), F32)
    for c0 in range(0, s, chunk):
        ls = _log_sigmoid(g_ref[c0:c0 + chunk, :] + b_ref[...])
        cs = _exact_lhs_dot(tri, ls) + carry
        carry = cs[chunk - 1:chunk, :]
        o_ref[c0:c0 + chunk, :] = cs * (-LOG2E)


def _fox_gate_call(gates, bias_row):
    b, s, _ = gates.shape
    chunk = min(GATE_CHUNK, s)
    assert s % chunk == 0
    kern = functools.partial(_fox_gate_kernel, chunk=chunk)
    return pl.pallas_call(
        kern,
        grid=(b,),
        in_specs=[
            pl.BlockSpec((None, s, LANES), lambda i: (i, 0, 0)),
            pl.BlockSpec((1, LANES), lambda i: (0, 0)),
        ],
        out_specs=pl.BlockSpec((None, s, LANES), lambda i: (i, 0, 0)),
        out_shape=jax.ShapeDtypeStruct((b, s, LANES), F32),
        compiler_params=_params("arbitrary"),
        name="fox_gate_cumsum",
    )(gates, bias_row)


def _fox_kernel(q_ref, k_ref, v_ref, f_ref, o_ref,
                ka_ref, kb_ref, vt_ref, m_ref, acc_ref, s0_ref, s1_ref, mx0_ref, mx1_ref, *, t, hb):
    dv = HEAD_DIM
    g = pl.program_id(1)
    i = pl.program_id(2)

    @pl.when(i == 0)
    def _():
        n = BIAS_PIECES * LANES
        src = lax.broadcasted_iota(jnp.int32, (n, LANES), 0)
        dst = lax.broadcasted_iota(jnp.int32, (n, LANES), 1)
        piece = jnp.right_shift(src, LANES.bit_length() - 1)
        head = src - piece * LANES
        for h in range(hb):
            pair = g * hb + h
            place_a = ((head == 2 * pair) & (dst == HEAD_DIM + piece)).astype(BF16)
            place_b = ((head == 2 * pair + 1) & (dst == piece)).astype(BF16)
            place = jnp.concatenate([place_a, place_b], axis=-1)

            def extras(r0, rows, place=place):
                pieces = jnp.concatenate(_split3(f_ref[pl.ds(r0, rows), :]), axis=-1)
                both = _dot(pieces, place)
                return both[:, 0:LANES], both[:, LANES:2 * LANES]
            _pack_keys(_lane_block(k_ref, h), _lane_block(v_ref, h), extras,
                       ka_ref.at[h], kb_ref.at[h],
                       [(vt_ref.at[2 * h + m], m * dv, dv) for m in range(2)])

    def last_step_addend(_):
        key = lax.broadcasted_iota(jnp.int32, (ATTN_KEYS, t), 0)
        qry = lax.broadcasted_iota(jnp.int32, (ATTN_KEYS, t), 1) + _last_step_offset(i, t)
        return jnp.where(key <= qry, 0.0, NEG)

    streams = []
    ones = [jnp.ones((1, LANES), F32)] * BIAS_PIECES
    for h in range(hb):
        qa, qb = _pack_queries(q_ref[:, h * LANES:(h + 1) * LANES], ones)
        for m, (q, k_sc) in enumerate(((qa, ka_ref), (qb, kb_ref))):
            vt = lambda k0, size, n=2 * h + m: vt_ref[n, :, pl.ds(k0, size)]
            stats = (m_ref.at[2 * h + m], acc_ref.at[2 * h + m])
            _init_stats(*stats)
            streams.append((q, k_sc.at[h], vt, 0, stats))
    _attend(i, t, streams, (s0_ref, s1_ref), (mx0_ref, mx1_ref), last_step_addend)

    for h in range(hb):
        pa = acc_ref[2 * h]
        pb = acc_ref[2 * h + 1]
        out = jnp.concatenate([pa[0:dv] / pa[dv:dv + 1], pb[0:dv] / pb[dv:dv + 1]], axis=0)
        o_ref[:, h * LANES:(h + 1) * LANES] = out.T.astype(BF16)


def _fox_call(z, fneg, heads):
    b, s, _ = z.shape
    pairs = heads // 2
    t = min(ATTN_TILE, s)
    assert s % ATTN_KEYS == 0
    hb = min(ATTN_BLOCKS, pairs)
    groups = pairs // hb
    assert pairs % hb == 0
    kern = functools.partial(_fox_kernel, t=t, hb=hb)
    w = hb * LANES
    return pl.pallas_call(
        kern,
        grid=(b, groups, s // t),
        in_specs=[
            pl.BlockSpec((None, t, w), lambda bi, g, i: (bi, i, g)),
            pl.BlockSpec((None, s, w), lambda bi, g, i: (bi, 0, groups + g)),
            pl.BlockSpec((None, s, w), lambda bi, g, i: (bi, 0, 2 * groups + g)),
            pl.BlockSpec((None, s, LANES), lambda bi, g, i: (bi, 0, 0)),
        ],
        out_specs=pl.BlockSpec((None, t, w), lambda bi, g, i: (bi, i, g)),
        out_shape=jax.ShapeDtypeStruct((b, s, pairs * LANES), BF16),
        scratch_shapes=[
            pltpu.VMEM((hb, s, LANES), BF16), pltpu.VMEM((hb, s, LANES), BF16),
            pltpu.VMEM((2 * hb, HEAD_DIM + ONES_ROWS, s), BF16),
            pltpu.VMEM((2 * hb, 1, t), F32),
            pltpu.VMEM((2 * hb, HEAD_DIM + ONES_ROWS, t), F32),
            pltpu.VMEM((2 * hb, ATTN_KEYS, t), F32), pltpu.VMEM((2 * hb, ATTN_KEYS, t), F32),
            pltpu.VMEM((2 * hb, 1, t), F32), pltpu.VMEM((2 * hb, 1, t), F32),
        ],
        compiler_params=_params("arbitrary", "arbitrary", "arbitrary", vmem_limit=ATTN_VMEM_LIMIT),
        name="forgetting_attention",
    )(z, z, z, fneg)


def _mlstm_kernel(q_ref, k_ref, v_ref, o_ref, g_ref, gb_ref, cw_ref, cb_ref, y_ref,
                  halo_ref, ct_ref, m_ref, *, heads, taps, lc):
    @pl.when(pl.program_id(1) == 0)
    def _():
        halo_ref[...] = jnp.zeros(halo_ref.shape, F32)
        ct_ref[...] = jnp.zeros(ct_ref.shape, F32)
        m_ref[...] = jnp.zeros(m_ref.shape, F32)

    for sub in range(q_ref.shape[0] // lc):
        rows = lambda ref: ref.at[pl.ds(sub * lc, lc), :]
        _mlstm_chunk(rows(q_ref), rows(k_ref), rows(v_ref), rows(o_ref), rows(g_ref), gb_ref,
                     cw_ref, cb_ref, rows(y_ref), halo_ref, ct_ref, m_ref, heads, taps)


def _mlstm_chunk(q_ref, k_ref, v_ref, o_ref, g_ref, gb_ref, cw_ref, cb_ref, y_ref,
                 halo_ref, ct_ref, m_ref, heads, taps):
    lc, width = q_ref.shape
    d = width // heads
    pad = halo_ref.shape[0]

    r = lax.broadcasted_iota(jnp.int32, (lc, lc), 0)
    c = lax.broadcasted_iota(jnp.int32, (lc, lc), 1)
    causal = c <= r

    xb = jnp.concatenate([q_ref[...], k_ref[...]], axis=-1)
    xf = xb.astype(F32)
    conv = cb_ref[...] + xf * cw_ref[taps - 1:taps, :]
    for sft in range(1, taps):
        shifted = _dot((r - c == sft).astype(BF16), xb)
        conv = conv + shifted * cw_ref[taps - 1 - sft:taps - sft, :]
    edge = jnp.concatenate([halo_ref[...], xf[0:pad, :]], axis=0)
    first = cb_ref[...]
    for j in range(taps):
        off = pad - (taps - 1) + j
        first = first + edge[off:off + pad, :] * cw_ref[j:j + 1, :]
    conv = jnp.concatenate([first, conv[pad:, :]], axis=0)
    halo_ref[...] = xf[lc - pad:, :]
    qk = conv * _sigmoid(conv)

    gcol = g_ref[...] + gb_ref[...]
    grow = gcol.T[:2 * heads, :]
    a_cols = _exact_lhs_dot(causal.astype(BF16), _log_sigmoid(gcol))
    a_rows = _exact_rhs_dot(_log_sigmoid(grow), (r <= c).astype(BF16))

    b_half = grow[0:heads, :] - a_rows[heads:2 * heads, :]
    b_rows = jnp.concatenate([b_half, b_half], axis=0)
    x_cols = jnp.maximum(_prefix_max_lanes(b_rows), m_ref[:, 0:1]).T
    ones = jnp.ones((lc, d), BF16)

    for hd in range(heads):
        q = qk[:, hd * d:(hd + 1) * d]
        k = qk[:, width + hd * d:width + (hd + 1) * d] * (d ** -0.5)
        v = v_ref[:, hd * d:(hd + 1) * d]
        qb = q.astype(BF16)
        kb = k.astype(BF16)
        ig_c = gcol[:, hd:hd + 1]
        a_c = a_cols[:, heads + hd:heads + hd + 1]
        a_last = a_c[lc - 1:lc, :]
        x_c = x_cols[:, hd:hd + 1]
        ct = ct_ref[hd]
        m_prev = m_ref[hd:hd + 1, 0:1]

        decay_mat = jnp.exp2(jnp.where(causal, b_rows[hd:hd + 1, :] * LOG2E - x_c * LOG2E, NEG))
        sqk = (_dot_nt(qb, kb) * decay_mat).astype(BF16)
        intra = _dot(sqk, jnp.concatenate([v, ones], axis=-1))
        inter = jnp.exp(m_prev - x_c) * _dot(qb, ct.astype(BF16))
        num = inter[:, 0:d] + intra[:, 0:d]
        den = inter[:, d:d + 1] + intra[:, d:d + 1]
        hval = num / jnp.maximum(jnp.abs(den), jnp.exp(-(a_c + x_c)))
        og = o_ref[:, hd * d:(hd + 1) * d].astype(F32)
        y_ref[:, hd * d:(hd + 1) * d] = (_sigmoid(og) * hval).astype(BF16)

        g_c = a_last - a_c + ig_c
        m_loc = jnp.max(g_c, axis=0, keepdims=True)
        w_c = jnp.exp(g_c - m_loc)
        m_new = jnp.maximum(a_last + m_prev, m_loc)
        decay = jnp.exp(a_last + m_prev - m_new)
        s_loc = jnp.exp(m_loc - m_new)
        wv = jnp.concatenate([w_c * v.astype(F32), jnp.broadcast_to(w_c, (lc, d))], axis=-1)
        ct_ref[hd] = decay * ct + s_loc * _dot_tn(kb, wv.astype(BF16))
        m_ref[hd:hd + 1, :] = jnp.broadcast_to(m_new, (1, LANES))
        m_ref[heads + hd:heads + hd + 1, :] = jnp.broadcast_to(m_new, (1, LANES))


def _mlstm_call(z, gates, gate_bias, conv_w, conv_b, heads, col0):
    b, s, _ = z.shape
    width = conv_w.shape[1] // 2
    lc = min(ML_CHUNK, s)
    rows = min(ML_CHUNKS_PER_STEP * lc, s)
    assert s % rows == 0 and rows % lc == 0 and col0 % width == 0 and width // heads == LANES
    assert 2 * heads == 8, "gate rows are handled as one 8-sublane tile"
    cb = col0 // width
    taps = conv_w.shape[0]
    kern = functools.partial(_mlstm_kernel, heads=heads, taps=taps, lc=lc)
    zspec = lambda off: pl.BlockSpec((None, rows, width), lambda bi, ci: (bi, ci, cb + off))
    return pl.pallas_call(
        kern,
        grid=(b, s // rows),
        in_specs=[
            zspec(0), zspec(1), zspec(2), zspec(3),
            pl.BlockSpec((None, rows, LANES), lambda bi, ci: (bi, ci, 0)),
            pl.BlockSpec((1, LANES), lambda bi, ci: (0, 0)),
            pl.BlockSpec((taps, 2 * width), lambda bi, ci: (0, 0)),
            pl.BlockSpec((1, 2 * width), lambda bi, ci: (0, 0)),
        ],
        out_specs=pl.BlockSpec((None, rows, width), lambda bi, ci: (bi, ci, 0)),
        out_shape=jax.ShapeDtypeStruct((b, s, width), BF16),
        scratch_shapes=[
            pltpu.VMEM((8, 2 * width), F32),
            pltpu.VMEM((heads, LANES, 2 * LANES), F32),
            pltpu.VMEM((2 * heads, LANES), F32),
        ],
        compiler_params=_params("arbitrary", "arbitrary"),
        name="mlstm",
    )(z, z, z, z, gates, gate_bias, conv_w, conv_b)


def _mix_ffn_kernel(*refs, n_in, chunk):
    y_refs = refs[:n_in]
    wo_refs = refs[n_in:2 * n_in]
    (x_ref, gm_ref, sh_ref, sc_ref, g_ref, gate_ref, w1_ref, w3_ref, w2_ref, o_ref) = refs[2 * n_in:]
    mix = _dot(y_refs[0][...], wo_refs[0][...])
    for y_ref, w_ref in zip(y_refs[1:], wo_refs[1:]):
        mix = mix + _dot(y_ref[...], w_ref[...])
    x = x_ref[...] + gm_ref[...] * mix
    h = _norm_modulate(x, g_ref[...], sh_ref[...], sc_ref[...]).astype(BF16)
    hidden = w1_ref.shape[1]
    acc = None
    for c0 in range(0, hidden, chunk):
        a = _dot(h, w1_ref[:, c0:c0 + chunk])
        bb = _dot(h, w3_ref[:, c0:c0 + chunk])
        u = (a * _sigmoid(a) * bb).astype(BF16)
        part = _dot(u, w2_ref[c0:c0 + chunk, :])
        acc = part if acc is None else acc + part
    o_ref[...] = x + gate_ref[...] * acc


def _mix_ffn_call(ys, wo_all, wo_layer, x, gate_mix, shift, scale, g, gate, w1, w3, w2, layer, name):
    b, s, d = x.shape
    hidden = w1.shape[2]
    tm = min(ROW_TILE, s)
    assert s % tm == 0 and hidden % FFN_CHUNK == 0
    assert all(y.shape[2] == ys[0].shape[2] for y in ys)
    kern = functools.partial(_mix_ffn_kernel, n_in=len(ys), chunk=FFN_CHUNK)
    vec = pl.BlockSpec((None, 1, d), lambda i, j: (i, 0, 0))
    rows = lambda width: pl.BlockSpec((None, tm, width), lambda i, j: (i, j, 0))
    stacked = lambda w: pl.BlockSpec((None,) + w.shape[1:], lambda i, j: (layer, 0, 0))
    wo_specs = [pl.BlockSpec((None, y.shape[2], d), lambda i, j, n=n: (wo_layer, n, 0))
                for n, y in enumerate(ys)]
    return pl.pallas_call(
        kern,
        grid=(b, s // tm),
        in_specs=[rows(y.shape[2]) for y in ys] + wo_specs + [
            rows(d), vec, vec, vec,
            pl.BlockSpec((1, d), lambda i, j: (0, 0)),
            vec, stacked(w1), stacked(w3), stacked(w2),
        ],
        out_specs=rows(d),
        out_shape=jax.ShapeDtypeStruct((b, s, d), F32),
        compiler_params=_params("arbitrary", "arbitrary"),
        name=name,
    )(*ys, *([wo_all] * len(ys)), x, gate_mix, shift, scale, g, gate, w1, w3, w2)


def _pad_cols(w, n):
    return jnp.pad(w, ((0, 0), (0, n - w.shape[1])))


def _gain_row(q_g, k_g, width, total):
    reps = width // HEAD_DIM
    row = jnp.concatenate([jnp.tile(q_g.astype(F32) * (HEAD_DIM ** -0.5 * LOG2E), reps),
                           jnp.tile(k_g.astype(F32), reps),
                           jnp.ones((total - 2 * width,), F32)])
    return row.reshape(1, total)


def kernel(x, c, ada_w, ada_b, norm_mix_g, norm_ffn_g, ab_w_in, ml_b_i, ml_b_f, ml_conv_w, ml_conv_b,
           da_q_g, da_k_g, da_lambda, da_subln_g, ab_w_out, fx_w_in, fx_b_f, fx_q_g, fx_k_g, fx_w_out,
           ffn_w1, ffn_w3, ffn_w2):
    depth = ada_w.shape[0]
    b, s, d = x.shape
    ml_width = ml_conv_w.shape[2] // 2
    ml_heads = ml_b_i.shape[1]
    da_width = (ab_w_in.shape[2] - 4 * ml_width - 2 * ml_heads) // 3
    da_heads = da_width // (2 * HEAD_DIM)
    fx_heads = fx_b_f.shape[1]
    fx_width = fx_heads * HEAD_DIM
    ab_main = 3 * da_width + 4 * ml_width
    fx_main = 3 * fx_width

    mods = _ada_call(c, ada_w, ada_b).reshape(depth, b, 6, 1, d)
    slopes = 2.0 ** (-8.0 * jnp.arange(1, da_heads + 1, dtype=F32) / da_heads)

    ab_w_in_b, fx_w_in_b = ab_w_in.astype(BF16), fx_w_in.astype(BF16)
    ab_w_out_b, fx_w_out_b = ab_w_out.astype(BF16), fx_w_out.astype(BF16)
    ffn_w1_b, ffn_w3_b, ffn_w2_b = ffn_w1.astype(BF16), ffn_w3.astype(BF16), ffn_w2.astype(BF16)

    for l in range(depth):
        sh_m, sc_m, g_m, sh_f, sc_f, g_f = (mods[l, :, t] for t in range(6))
        j = l // 2
        g_mix = norm_mix_g[l].reshape(1, d)
        if l % 2 == 0:
            z, gates = _inproj_call(
                x, sh_m, sc_m, g_mix, ab_w_in_b, j, ab_main,
                _pad_cols(ab_w_in[j][:, ab_main:], LANES).astype(BF16),
                _gain_row(da_q_g[j], da_k_g[j], da_width, ab_main),
                2 * da_width, "inproj_even")
            lambda_init = 0.8 - 0.6 * math.exp(-0.3 * l)
            y_da = _da_call(z, slopes, da_lambda[j], da_subln_g[j], da_heads, lambda_init)
            gate_bias = _pad_cols(jnp.concatenate([ml_b_i[j], ml_b_f[j]]).reshape(1, -1), LANES)
            y_ml = _mlstm_call(z, gates, gate_bias, ml_conv_w[j], ml_conv_b[j].reshape(1, -1),
                               ml_heads, 3 * da_width)
            ys, wo_all, name = [y_da, y_ml], ab_w_out_b, "mix_ffn_even"
        else:
            z, gates = _inproj_call(
                x, sh_m, sc_m, g_mix, fx_w_in_b, j, fx_main,
                _pad_cols(fx_w_in[j][:, fx_main:], LANES).astype(BF16),
                _gain_row(fx_q_g[j], fx_k_g[j], fx_width, fx_main),
                2 * fx_width, "inproj_odd")
            fneg = _fox_gate_call(gates, _pad_cols(fx_b_f[j].reshape(1, -1), LANES))
            ys, wo_all, name = [_fox_call(z, fneg, fx_heads)], fx_w_out_b, "mix_ffn_odd"
        x = _mix_ffn_call(ys, wo_all, j, x, g_m, sh_f, sc_f, norm_ffn_g[l].reshape(1, d), g_f,
                          ffn_w1_b, ffn_w3_b, ffn_w2_b, l, name)
    return x
```

```python
import functools
import math

import jax
import jax.numpy as jnp
from jax import lax
from jax.experimental import pallas as pl
from jax.experimental.pallas import tpu as pltpu

F32 = jnp.float32
BF16 = jnp.bfloat16

EPS = 1e-6
LANES = 128
HEAD_DIM = 64
CHUNK = 64
CHUNK_SHIFT = CHUNK.bit_length() - 1
assert CHUNK == 1 << CHUNK_SHIFT
NEG = -1e30
LOG2E = math.log2(math.e)
BIAS_PIECES = 3
ONES_ROWS = 16

VMEM_LIMIT = 56 * 1024 * 1024

ROW_TILE = 512
ATTN_TILE = 256
ATTN_KEYS = 512
ATTN_BLOCKS = 4
PACK_ROWS = 512
ML_CHUNK = 256
ML_CHUNKS_PER_STEP = 2
GATE_CHUNK = 256
FFN_CHUNK = 256


def _params(*sem):
    return pltpu.CompilerParams(dimension_semantics=sem, vmem_limit_bytes=VMEM_LIMIT)


def _log_sigmoid(x):
    return jnp.minimum(x, 0.0) - jnp.log1p(jnp.exp(-jnp.abs(x)))


def _sigmoid(x):
    return 0.5 * jnp.tanh(0.5 * x) + 0.5


def _prefix_max_lanes(x):
    axis = x.ndim - 1
    lane = lax.broadcasted_iota(jnp.int32, x.shape, axis)
    shift = 1
    while shift < x.shape[axis]:
        x = jnp.maximum(x, jnp.where(lane >= shift, pltpu.roll(x, shift, axis=axis), NEG))
        shift *= 2
    return x


def _split3(x):
    x1 = x.astype(BF16)
    r1 = x - x1.astype(F32)
    x2 = r1.astype(BF16)
    x3 = (r1 - x2.astype(F32)).astype(BF16)
    return x1, x2, x3


def _dot(a, b):
    return jnp.dot(a, b, preferred_element_type=F32)


def _dot_nt(a, b):
    return lax.dot_general(a, b, (((1,), (1,)), ((), ())), preferred_element_type=F32)


def _dot_tn(a, b):
    return lax.dot_general(a, b, (((0,), (0,)), ((), ())), preferred_element_type=F32)


def _exact_lhs_dot(m_bf16, x):
    x1, x2, x3 = _split3(x)
    return _dot(m_bf16, x1) + _dot(m_bf16, x2) + _dot(m_bf16, x3)


def _exact_rhs_dot(x, m_bf16):
    x1, x2, x3 = _split3(x)
    return _dot(x1, m_bf16) + _dot(x2, m_bf16) + _dot(x3, m_bf16)


def _norm_modulate(x, g, sh, sc):
    ms = jnp.mean(x * x, axis=-1, keepdims=True)
    y = x * lax.rsqrt(ms + EPS) * g
    return y * (1.0 + sc) + sh


def _ada_kernel(c_ref, w_ref, b_ref, o_ref):
    c = c_ref[...]
    ca = c * _sigmoid(c)
    c1, c2, c3 = _split3(ca)
    w1, w2, w3 = _split3(w_ref[...])
    acc = _dot(c1, w1) + _dot(c1, w2) + _dot(c2, w1)
    acc = acc + _dot(c1, w3) + _dot(c2, w2) + _dot(c3, w1)
    o_ref[...] = acc + b_ref[...]


def _ada_call(c, ada_w, ada_b):
    depth, d, n = ada_w.shape
    b = c.shape[0]
    tn = 1536
    assert n % tn == 0
    return pl.pallas_call(
        _ada_kernel,
        grid=(depth, n // tn),
        in_specs=[
            pl.BlockSpec((b, d), lambda l, j: (0, 0)),
            pl.BlockSpec((None, d, tn), lambda l, j: (l, 0, j)),
            pl.BlockSpec((None, 1, tn), lambda l, j: (l, 0, j)),
        ],
        out_specs=pl.BlockSpec((None, b, tn), lambda l, j: (l, 0, j)),
        out_shape=jax.ShapeDtypeStruct((depth, b, n), F32),
        compiler_params=_params("arbitrary", "arbitrary"),
        name="ada_mod",
    )(c, ada_w, ada_b.reshape(depth, 1, n))


def _inproj_kernel(x_ref, sh_ref, sc_ref, g_ref, w_ref, wg_ref, gain_ref, z_ref, gate_ref,
                   *, n_norm, chunk):
    h = _norm_modulate(x_ref[...], g_ref[...], sh_ref[...], sc_ref[...]).astype(BF16)
    n = w_ref.shape[1]
    lane = lax.broadcasted_iota(jnp.int32, (1, LANES), 1)
    lo = lane < HEAD_DIM
    for c0 in range(0, n, chunk):
        zc = _dot(h, w_ref[:, c0:c0 + chunk])
        if c0 < n_norm:
            parts = []
            for s0 in range(0, chunk, LANES):
                zs = zc[:, s0:s0 + LANES]
                sq = zs * zs
                s_lo = jnp.sum(jnp.where(lo, sq, 0.0), axis=-1, keepdims=True)
                s_hi = jnp.sum(jnp.where(lo, 0.0, sq), axis=-1, keepdims=True)
                r_lo = lax.rsqrt(s_lo * (1.0 / HEAD_DIM) + EPS)
                r_hi = lax.rsqrt(s_hi * (1.0 / HEAD_DIM) + EPS)
                parts.append(zs * jnp.where(lo, r_lo, r_hi))
            zc = jnp.concatenate(parts, axis=-1) * gain_ref[:, c0:c0 + chunk]
        z_ref[:, c0:c0 + chunk] = zc.astype(BF16)
    gate_ref[...] = _dot(h, wg_ref[...])


def _inproj_call(x, shift, scale, g, w_all, layer, n, wg, gain, n_norm, name):
    b, s, d = x.shape
    tm = min(ROW_TILE, s)
    chunk = 512
    assert s % tm == 0 and n % chunk == 0 and n_norm % chunk == 0
    kern = functools.partial(_inproj_kernel, n_norm=n_norm, chunk=chunk)
    return pl.pallas_call(
        kern,
        grid=(b, s // tm),
        in_specs=[
            pl.BlockSpec((None, tm, d), lambda i, j: (i, j, 0)),
            pl.BlockSpec((None, 1, d), lambda i, j: (i, 0, 0)),
            pl.BlockSpec((None, 1, d), lambda i, j: (i, 0, 0)),
            pl.BlockSpec((1, d), lambda i, j: (0, 0)),
            pl.BlockSpec((None, d, n), lambda i, j: (layer, 0, 0)),
            pl.BlockSpec((d, LANES), lambda i, j: (0, 0)),
            pl.BlockSpec((1, n), lambda i, j: (0, 0)),
        ],
        out_specs=[
            pl.BlockSpec((None, tm, n), lambda i, j: (i, j, 0)),
            pl.BlockSpec((None, tm, LANES), lambda i, j: (i, j, 0)),
        ],
        out_shape=[
            jax.ShapeDtypeStruct((b, s, n), BF16),
            jax.ShapeDtypeStruct((b, s, LANES), F32),
        ],
        compiler_params=_params("arbitrary", "arbitrary"),
        name=name,
    )(x, shift, scale, g, w_all, wg, gain)


def _lane_ids():
    return lax.broadcasted_iota(jnp.int32, (1, LANES), 1)


def _place_pieces(pieces, first_lane):
    lane = _lane_ids()
    out = jnp.zeros(pieces[0].shape, F32)
    for j, p in enumerate(pieces):
        out = jnp.where(lane == first_lane + j, p.astype(F32), out)
    return out


def _pack_keys(k_ref, v_ref, extras_fn, ka_ref, kb_ref, vt_refs):
    s = k_ref.shape[0]
    rows = min(PACK_ROWS, s)
    keep_lo = jnp.where(_lane_ids() < HEAD_DIM, 1.0, 0.0).astype(BF16)
    keep_hi = jnp.where(_lane_ids() < HEAD_DIM, 0.0, 1.0).astype(BF16)

    def body(c, carry):
        r0 = pl.multiple_of(c * rows, rows)
        k2 = k_ref[pl.ds(r0, rows), :]
        ea, eb = extras_fn(r0, rows)
        ka_ref[pl.ds(r0, rows), :] = k2 * keep_lo + ea.astype(BF16)
        kb_ref[pl.ds(r0, rows), :] = k2 * keep_hi + eb.astype(BF16)
        vt = v_ref[pl.ds(r0, rows), :].T
        for ref, c0, width in vt_refs:
            ref[0:width, pl.ds(r0, rows)] = vt[c0:c0 + width, :]
            ref[width:width + ONES_ROWS, pl.ds(r0, rows)] = jnp.ones((ONES_ROWS, rows), BF16)
        return carry

    steps = s // rows
    lax.fori_loop(0, steps, body, 0, unroll=2 if steps % 2 == 0 else 1)


def _pack_queries(q2, factors):
    lane = _lane_ids()
    q2 = q2.astype(F32)
    lo = lane < HEAD_DIM
    fac_a = _place_pieces(factors, HEAD_DIM)
    fac_b = _place_pieces(factors, 0)
    return jnp.where(lo, q2, fac_a).astype(BF16), jnp.where(lo, fac_b, q2).astype(BF16)


def _init_stats(m_ref, acc_ref):
    m_ref[...] = jnp.full(m_ref.shape, NEG, F32)
    acc_ref[...] = jnp.zeros(acc_ref.shape, F32)


def _softmax_step(s, tile_max, vt, m_ref, acc_ref):
    m_old = m_ref[...]
    m_new = jnp.maximum(m_old, tile_max)
    alpha = jnp.exp2(m_old - m_new)
    p = jnp.exp2(s - m_new)
    acc_ref[...] = alpha * acc_ref[...] + _dot(vt, p.astype(BF16))
    m_ref[...] = m_new


def _attend(i, t, streams, s_buf, mx_buf, make_addend):
    tk = ATTN_KEYS
    n = (i * t + t + tk - 1) // tk

    def qk_one(step, slot, idx):
        k0 = pl.multiple_of(step * tk, tk)
        q, k_ref = streams[idx][0], streams[idx][1]
        s = _dot_nt(k_ref[pl.ds(k0, tk), :], q)
        s_buf[slot][idx] = s
        mx_buf[slot][idx] = jnp.max(s, axis=0, keepdims=True)

    def softmax_one(step, slot, idx, last, addends):
        k0 = pl.multiple_of(step * tk, tk)
        _, _, vt, add_key, stats = streams[idx]
        s = s_buf[slot][idx]
        if last:
            if add_key not in addends:
                addends[add_key] = make_addend(add_key)
            s = s + addends[add_key]
            tile_max = jnp.max(s, axis=0, keepdims=True)
        else:
            tile_max = mx_buf[slot][idx]
        _softmax_step(s, tile_max, vt(k0, tk), *stats)

    def qk_stage(step, slot):
        for idx in range(len(streams)):
            qk_one(step, slot, idx)

    def softmax_stage(step, slot, last):
        addends = {}
        for idx in range(len(streams)):
            softmax_one(step, slot, idx, last, addends)

    def overlapped(qk_step, qk_slot, sm_step, sm_slot):
        for idx in range(len(streams)):
            qk_one(qk_step, qk_slot, idx)
            softmax_one(sm_step, sm_slot, idx, False, None)

    qk_stage(0, 0)
    pairs = (n - 1) // 2

    def body(p, carry):
        overlapped(2 * p + 1, 1, 2 * p, 0)
        overlapped(2 * p + 2, 0, 2 * p + 1, 1)
        return carry

    lax.fori_loop(0, pairs, body, 0)

    @pl.when(n - 1 > 2 * pairs)
    def _():
        overlapped(n - 1, 1, n - 2, 0)
        softmax_stage(n - 1, 1, True)

    @pl.when(n - 1 == 2 * pairs)
    def _():
        softmax_stage(n - 1, 0, True)


def _last_step_offset(i, t):
    tk = ATTN_KEYS
    n = (i * t + t + tk - 1) // tk
    return i * t - (n - 1) * tk


def _lane_block(ref, h):
    return ref.at[:, pl.ds(h * LANES, LANES)]


def _da_kernel(slope_ref, q_ref, k_ref, v_ref, lam_ref, sg_ref, o_ref,
               ka_ref, kb_ref, vt_ref, m_ref, acc_ref, s0_ref, s1_ref, mx0_ref, mx1_ref,
               *, t, hb, lambda_init):
    dv = LANES
    g = pl.program_id(1)
    i = pl.program_id(2)
    slopes2 = [slope_ref[g * hb + h] * LOG2E for h in range(hb)]

    @pl.when(i == 0)
    def _():
        def extras(r0, rows):
            pos = lax.broadcasted_iota(jnp.int32, (rows, LANES), 0) + r0
            hi = jnp.right_shift(pos, CHUNK_SHIFT).astype(F32)
            lo = jnp.bitwise_and(pos, CHUNK - 1).astype(F32)
            lane = _lane_ids()

            def place(first):
                in_hi = (lane >= first) & (lane < first + BIAS_PIECES)
                in_lo = (lane >= first + BIAS_PIECES) & (lane < first + 2 * BIAS_PIECES)
                return jnp.where(in_hi, hi, jnp.where(in_lo, lo, 0.0))
            return place(HEAD_DIM), place(0)
        for h in range(hb):
            _pack_keys(_lane_block(k_ref, h), _lane_block(v_ref, h), extras,
                       ka_ref.at[h], kb_ref.at[h], [(vt_ref.at[h], 0, dv)])

    def last_step_addend(h):
        key = lax.broadcasted_iota(jnp.int32, (ATTN_KEYS, t), 0)
        qry = lax.broadcasted_iota(jnp.int32, (ATTN_KEYS, t), 1) + _last_step_offset(i, t)
        allowed = jnp.right_shift(key, CHUNK_SHIFT) <= jnp.right_shift(qry, CHUNK_SHIFT)
        above = jnp.maximum(key - qry, 0).astype(F32)
        return jnp.where(allowed, (-2.0 * slopes2[h]) * above, NEG)

    streams = []
    for h in range(hb):
        slope_pieces = [p.astype(F32) for p in _split3(jnp.full((1, LANES), slopes2[h], F32))]
        factors = [p * float(CHUNK) for p in slope_pieces] + slope_pieces
        qa, qb = _pack_queries(q_ref[:, h * LANES:(h + 1) * LANES], factors)
        vt = lambda k0, size, h=h: vt_ref[h, :, pl.ds(k0, size)]
        for m, (q, k_sc) in enumerate(((qa, ka_ref), (qb, kb_ref))):
            stats = (m_ref.at[2 * h + m], acc_ref.at[2 * h + m])
            _init_stats(*stats)
            streams.append((q, k_sc.at[h], vt, h, stats))
    _attend(i, t, streams, (s0_ref, s1_ref), (mx0_ref, mx1_ref), last_step_addend)

    lp = lam_ref[...]
    lam = (jnp.exp(jnp.sum(lp[0:1] * lp[1:2], axis=-1, keepdims=True))
           - jnp.exp(jnp.sum(lp[2:3] * lp[3:4], axis=-1, keepdims=True)) + lambda_init)
    for h in range(hb):
        pa = acc_ref[2 * h]
        pb = acc_ref[2 * h + 1]
        out = pa[0:dv] / pa[dv:dv + 1] - lam * (pb[0:dv] / pb[dv:dv + 1])
        ms = jnp.mean(out * out, axis=0, keepdims=True)
        out = out * lax.rsqrt(ms + EPS) * (sg_ref[...] * (1.0 - lambda_init))
        o_ref[:, h * LANES:(h + 1) * LANES] = out.T.astype(BF16)


def _da_call(z, slopes, lam_p, subln_g, heads, lambda_init):
    b, s, _ = z.shape
    t = min(ATTN_TILE, s)
    assert s % ATTN_KEYS == 0 and t % CHUNK == 0
    assert s <= CHUNK * 256, "key position // CHUNK must stay exact in bf16"
    hb = min(ATTN_BLOCKS, heads)
    groups = heads // hb
    assert heads % hb == 0
    kern = functools.partial(_da_kernel, t=t, hb=hb, lambda_init=lambda_init)
    w = hb * LANES
    return pl.pallas_call(
        kern,
        grid_spec=pltpu.PrefetchScalarGridSpec(
            num_scalar_prefetch=1,
            grid=(b, groups, s // t),
            in_specs=[
                pl.BlockSpec((None, t, w), lambda bi, g, i, sl: (bi, i, g)),
                pl.BlockSpec((None, s, w), lambda bi, g, i, sl: (bi, 0, groups + g)),
                pl.BlockSpec((None, s, w), lambda bi, g, i, sl: (bi, 0, 2 * groups + g)),
                pl.BlockSpec((4, HEAD_DIM), lambda bi, g, i, sl: (0, 0)),
                pl.BlockSpec((LANES, 1), lambda bi, g, i, sl: (0, 0)),
            ],
            out_specs=pl.BlockSpec((None, t, w), lambda bi, g, i, sl: (bi, i, g)),
            scratch_shapes=[
                pltpu.VMEM((hb, s, LANES), BF16), pltpu.VMEM((hb, s, LANES), BF16),
                pltpu.VMEM((hb, LANES + ONES_ROWS, s), BF16),
                pltpu.VMEM((2 * hb, 1, t), F32),
                pltpu.VMEM((2 * hb, LANES + ONES_ROWS, t), F32),
                pltpu.VMEM((2 * hb, ATTN_KEYS, t), F32), pltpu.VMEM((2 * hb, ATTN_KEYS, t), F32),
                pltpu.VMEM((2 * hb, 1, t), F32), pltpu.VMEM((2 * hb, 1, t), F32),
            ],
        ),
        out_shape=jax.ShapeDtypeStruct((b, s, heads * LANES), BF16),
        compiler_params=_params("arbitrary", "arbitrary", "arbitrary"),
        name="diff_attention",
    )(slopes, z, z, z, lam_p, subln_g.reshape(LANES, 1))


def _fox_gate_kernel(g_ref, b_ref, o_ref, *, chunk):
    s = g_ref.shape[0]
    r = lax.broadcasted_iota(jnp.int32, (chunk, chunk), 0)
    c = lax.broadcasted_iota(jnp.int32, (chunk, chunk), 1)
    tri = (c <= r).astype(BF16)
    carry = jnp.zeros((1, LANES), F32)
    for c0 in range(0, s, chunk):
        ls = _log_sigmoid(g_ref[c0:c0 + chunk, :] + b_ref[...])
        cs = _exact_lhs_dot(tri, ls) + carry
        carry = cs[chunk - 1:chunk, :]
        o_ref[c0:c0 + chunk, :] = cs * (-LOG2E)


def _fox_gate_call(gates, bias_row):
    b, s, _ = gates.shape
    chunk = min(GATE_CHUNK, s)
    assert s % chunk == 0
    kern = functools.partial(_fox_gate_kernel, chunk=chunk)
    return pl.pallas_call(
        kern,
        grid=(b,),
        in_specs=[
            pl.BlockSpec((None, s, LANES), lambda i: (i, 0, 0)),
            pl.BlockSpec((1, LANES), lambda i: (0, 0)),
        ],
        out_specs=pl.BlockSpec((None, s, LANES), lambda i: (i, 0, 0)),
        out_shape=jax.ShapeDtypeStruct((b, s, LANES), F32),
        compiler_params=_params("arbitrary"),
        name="fox_gate_cumsum",
    )(gates, bias_row)


def _fox_kernel(q_ref, k_ref, v_ref, f_ref, o_ref,
                ka_ref, kb_ref, vt_ref, m_ref, acc_ref, s0_ref, s1_ref, mx0_ref, mx1_ref, *, t, hb):
    dv = HEAD_DIM
    g = pl.program_id(1)
    i = pl.program_id(2)

    @pl.when(i == 0)
    def _():
        n = BIAS_PIECES * LANES
        src = lax.broadcasted_iota(jnp.int32, (n, LANES), 0)
        dst = lax.broadcasted_iota(jnp.int32, (n, LANES), 1)
        piece = jnp.right_shift(src, LANES.bit_length() - 1)
        head = src - piece * LANES
        for h in range(hb):
            pair = g * hb + h
            place_a = ((head == 2 * pair) & (dst == HEAD_DIM + piece)).astype(BF16)
            place_b = ((head == 2 * pair + 1) & (dst == piece)).astype(BF16)
            place = jnp.concatenate([place_a, place_b], axis=-1)

            def extras(r0, rows, place=place):
                pieces = jnp.concatenate(_split3(f_ref[pl.ds(r0, rows), :]), axis=-1)
                both = _dot(pieces, place)
                return both[:, 0:LANES], both[:, LANES:2 * LANES]
            _pack_keys(_lane_block(k_ref, h), _lane_block(v_ref, h), extras,
                       ka_ref.at[h], kb_ref.at[h],
                       [(vt_ref.at[2 * h + m], m * dv, dv) for m in range(2)])

    def last_step_addend(_):
        key = lax.broadcasted_iota(jnp.int32, (ATTN_KEYS, t), 0)
        qry = lax.broadcasted_iota(jnp.int32, (ATTN_KEYS, t), 1) + _last_step_offset(i, t)
        return jnp.where(key <= qry, 0.0, NEG)

    streams = []
    ones = [jnp.ones((1, LANES), F32)] * BIAS_PIECES
    for h in range(hb):
        qa, qb = _pack_queries(q_ref[:, h * LANES:(h + 1) * LANES], ones)
        for m, (q, k_sc) in enumerate(((qa, ka_ref), (qb, kb_ref))):
            vt = lambda k0, size, n=2 * h + m: vt_ref[n, :, pl.ds(k0, size)]
            stats = (m_ref.at[2 * h + m], acc_ref.at[2 * h + m])
            _init_stats(*stats)
            streams.append((q, k_sc.at[h], vt, 0, stats))
    _attend(i, t, streams, (s0_ref, s1_ref), (mx0_ref, mx1_ref), last_step_addend)

    for h in range(hb):
        pa = acc_ref[2 * h]
        pb = acc_ref[2 * h + 1]
        out = jnp.concatenate([pa[0:dv] / pa[dv:dv + 1], pb[0:dv] / pb[dv:dv + 1]], axis=0)
        o_ref[:, h * LANES:(h + 1) * LANES] = out.T.astype(BF16)


def _fox_call(z, fneg, heads):
    b, s, _ = z.shape
    pairs = heads // 2
    t = min(ATTN_TILE, s)
    assert s % ATTN_KEYS == 0
    hb = min(ATTN_BLOCKS, pairs)
    groups = pairs // hb
    assert pairs % hb == 0
    kern = functools.partial(_fox_kernel, t=t, hb=hb)
    w = hb * LANES
    return pl.pallas_call(
        kern,
        grid=(b, groups, s // t),
        in_specs=[
            pl.BlockSpec((None, t, w), lambda bi, g, i: (bi, i, g)),
            pl.BlockSpec((None, s, w), lambda bi, g, i: (bi, 0, groups + g)),
            pl.BlockSpec((None, s, w), lambda bi, g, i: (bi, 0, 2 * groups + g)),
            pl.BlockSpec((None, s, LANES), lambda bi, g, i: (bi, 0, 0)),
        ],
        out_specs=pl.BlockSpec((None, t, w), lambda bi, g, i: (bi, i, g)),
        out_shape=jax.ShapeDtypeStruct((b, s, pairs * LANES), BF16),
        scratch_shapes=[
            pltpu.VMEM((hb, s, LANES), BF16), pltpu.VMEM((hb, s, LANES), BF16),
            pltpu.VMEM((2 * hb, HEAD_DIM + ONES_ROWS, s), BF16),
            pltpu.VMEM((2 * hb, 1, t), F32),
            pltpu.VMEM((2 * hb, HEAD_DIM + ONES_ROWS, t), F32),
            pltpu.VMEM((2 * hb, ATTN_KEYS, t), F32), pltpu.VMEM((2 * hb, ATTN_KEYS, t), F32),
            pltpu.VMEM((2 * hb, 1, t), F32), pltpu.VMEM((2 * hb, 1, t), F32),
        ],
        compiler_params=_params("arbitrary", "arbitrary", "arbitrary"),
        name="forgetting_attention",
    )(z, z, z, fneg)


def _mlstm_kernel(q_ref, k_ref, v_ref, o_ref, g_ref, gb_ref, cw_ref, cb_ref, y_ref,
                  halo_ref, ct_ref, m_ref, *, heads, taps, lc):
    @pl.when(pl.program_id(1) == 0)
    def _():
        halo_ref[...] = jnp.zeros(halo_ref.shape, F32)
        ct_ref[...] = jnp.zeros(ct_ref.shape, F32)
        m_ref[...] = jnp.zeros(m_ref.shape, F32)

    for sub in range(q_ref.shape[0] // lc):
        rows = lambda ref: ref.at[pl.ds(sub * lc, lc), :]
        _mlstm_chunk(rows(q_ref), rows(k_ref), rows(v_ref), rows(o_ref), rows(g_ref), gb_ref,
                     cw_ref, cb_ref, rows(y_ref), halo_ref, ct_ref, m_ref, heads, taps)


def _mlstm_chunk(q_ref, k_ref, v_ref, o_ref, g_ref, gb_ref, cw_ref, cb_ref, y_ref,
                 halo_ref, ct_ref, m_ref, heads, taps):
    lc, width = q_ref.shape
    d = width // heads
    pad = halo_ref.shape[0]

    r = lax.broadcasted_iota(jnp.int32, (lc, lc), 0)
    c = lax.broadcasted_iota(jnp.int32, (lc, lc), 1)
    causal = c <= r

    xb = jnp.concatenate([q_ref[...], k_ref[...]], axis=-1)
    xf = xb.astype(F32)
    conv = cb_ref[...] + xf * cw_ref[taps - 1:taps, :]
    for sft in range(1, taps):
        shifted = _dot((r - c == sft).astype(BF16), xb)
        conv = conv + shifted * cw_ref[taps - 1 - sft:taps - sft, :]
    edge = jnp.concatenate([halo_ref[...], xf[0:pad, :]], axis=0)
    first = cb_ref[...]
    for j in range(taps):
        off = pad - (taps - 1) + j
        first = first + edge[off:off + pad, :] * cw_ref[j:j + 1, :]
    conv = jnp.concatenate([first, conv[pad:, :]], axis=0)
    halo_ref[...] = xf[lc - pad:, :]
    qk = conv * _sigmoid(conv)

    gcol = g_ref[...] + gb_ref[...]
    grow = gcol.T[:2 * heads, :]
    a_cols = _exact_lhs_dot(causal.astype(BF16), _log_sigmoid(gcol))
    a_rows = _exact_rhs_dot(_log_sigmoid(grow), (r <= c).astype(BF16))

    b_half = grow[0:heads, :] - a_rows[heads:2 * heads, :]
    b_rows = jnp.concatenate([b_half, b_half], axis=0)
    x_cols = jnp.maximum(_prefix_max_lanes(b_rows), m_ref[:, 0:1]).T
    ones = jnp.ones((lc, d), BF16)

    for hd in range(heads):
        q = qk[:, hd * d:(hd + 1) * d]
        k = qk[:, width + hd * d:width + (hd + 1) * d] * (d ** -0.5)
        v = v_ref[:, hd * d:(hd + 1) * d]
        qb = q.astype(BF16)
        kb = k.astype(BF16)
        ig_c = gcol[:, hd:hd + 1]
        a_c = a_cols[:, heads + hd:heads + hd + 1]
        a_last = a_c[lc - 1:lc, :]
        x_c = x_cols[:, hd:hd + 1]
        ct = ct_ref[hd]
        m_prev = m_ref[hd:hd + 1, 0:1]

        decay_mat = jnp.exp2(jnp.where(causal, b_rows[hd:hd + 1, :] * LOG2E - x_c * LOG2E, NEG))
        sqk = (_dot_nt(qb, kb) * decay_mat).astype(BF16)
        intra = _dot(sqk, jnp.concatenate([v, ones], axis=-1))
        inter = jnp.exp(m_prev - x_c) * _dot(qb, ct.astype(BF16))
        num = inter[:, 0:d] + intra[:, 0:d]
        den = inter[:, d:d + 1] + intra[:, d:d + 1]
        hval = num / jnp.maximum(jnp.abs(den), jnp.exp(-(a_c + x_c)))
        og = o_ref[:, hd * d:(hd + 1) * d].astype(F32)
        y_ref[:, hd * d:(hd + 1) * d] = (_sigmoid(og) * hval).astype(BF16)

        g_c = a_last - a_c + ig_c
        m_loc = jnp.max(g_c, axis=0, keepdims=True)
        w_c = jnp.exp(g_c - m_loc)
        m_new = jnp.maximum(a_last + m_prev, m_loc)
        decay = jnp.exp(a_last + m_prev - m_new)
        s_loc = jnp.exp(m_loc - m_new)
        wv = jnp.concatenate([w_c * v.astype(F32), jnp.broadcast_to(w_c, (lc, d))], axis=-1)
        ct_ref[hd] = decay * ct + s_loc * _dot_tn(kb, wv.astype(BF16))
        m_ref[hd:hd + 1, :] = jnp.broadcast_to(m_new, (1, LANES))
        m_ref[heads + hd:heads + hd + 1, :] = jnp.broadcast_to(m_new, (1, LANES))


def _mlstm_call(z, gates, gate_bias, conv_w, conv_b, heads, col0):
    b, s, _ = z.shape
    width = conv_w.shape[1] // 2
    lc = min(ML_CHUNK, s)
    rows = min(ML_CHUNKS_PER_STEP * lc, s)
    assert s % rows == 0 and rows % lc == 0 and col0 % width == 0 and width // heads == LANES
    assert 2 * heads == 8, "gate rows are handled as one 8-sublane tile"
    cb = col0 // width
    taps = conv_w.shape[0]
    kern = functools.partial(_mlstm_kernel, heads=heads, taps=taps, lc=lc)
    zspec = lambda off: pl.BlockSpec((None, rows, width), lambda bi, ci: (bi, ci, cb + off))
    return pl.pallas_call(
        kern,
        grid=(b, s // rows),
        in_specs=[
            zspec(0), zspec(1), zspec(2), zspec(3),
            pl.BlockSpec((None, rows, LANES), lambda bi, ci: (bi, ci, 0)),
            pl.BlockSpec((1, LANES), lambda bi, ci: (0, 0)),
            pl.BlockSpec((taps, 2 * width), lambda bi, ci: (0, 0)),
            pl.BlockSpec((1, 2 * width), lambda bi, ci: (0, 0)),
        ],
        out_specs=pl.BlockSpec((None, rows, width), lambda bi, ci: (bi, ci, 0)),
        out_shape=jax.ShapeDtypeStruct((b, s, width), BF16),
        scratch_shapes=[
            pltpu.VMEM((8, 2 * width), F32),
            pltpu.VMEM((heads, LANES, 2 * LANES), F32),
            pltpu.VMEM((2 * heads, LANES), F32),
        ],
        compiler_params=_params("arbitrary", "arbitrary"),
        name="mlstm",
    )(z, z, z, z, gates, gate_bias, conv_w, conv_b)


def _mix_ffn_kernel(*refs, n_in, chunk):
    y_refs = refs[:n_in]
    wo_refs = refs[n_in:2 * n_in]
    (x_ref, gm_ref, sh_ref, sc_ref, g_ref, gate_ref, w1_ref, w3_ref, w2_ref, o_ref) = refs[2 * n_in:]
    mix = _dot(y_refs[0][...], wo_refs[0][...])
    for y_ref, w_ref in zip(y_refs[1:], wo_refs[1:]):
        mix = mix + _dot(y_ref[...], w_ref[...])
    x = x_ref[...] + gm_ref[...] * mix
    h = _norm_modulate(x, g_ref[...], sh_ref[...], sc_ref[...]).astype(BF16)
    hidden = w1_ref.shape[1]
    acc = None
    for c0 in range(0, hidden, chunk):
        a = _dot(h, w1_ref[:, c0:c0 + chunk])
        bb = _dot(h, w3_ref[:, c0:c0 + chunk])
        u = (a * _sigmoid(a) * bb).astype(BF16)
        part = _dot(u, w2_ref[c0:c0 + chunk, :])
        acc = part if acc is None else acc + part
    o_ref[...] = x + gate_ref[...] * acc


def _mix_ffn_call(ys, wo_all, wo_layer, x, gate_mix, shift, scale, g, gate, w1, w3, w2, layer, name):
    b, s, d = x.shape
    hidden = w1.shape[2]
    tm = min(ROW_TILE, s)
    assert s % tm == 0 and hidden % FFN_CHUNK == 0
    assert all(y.shape[2] == ys[0].shape[2] for y in ys)
    kern = functools.partial(_mix_ffn_kernel, n_in=len(ys), chunk=FFN_CHUNK)
    vec = pl.BlockSpec((None, 1, d), lambda i, j: (i, 0, 0))
    rows = lambda width: pl.BlockSpec((None, tm, width), lambda i, j: (i, j, 0))
    stacked = lambda w: pl.BlockSpec((None,) + w.shape[1:], lambda i, j: (layer, 0, 0))
    wo_specs = [pl.BlockSpec((None, y.shape[2], d), lambda i, j, n=n: (wo_layer, n, 0))
                for n, y in enumerate(ys)]
    return pl.pallas_call(
        kern,
        grid=(b, s // tm),
        in_specs=[rows(y.shape[2]) for y in ys] + wo_specs + [
            rows(d), vec, vec, vec,
            pl.BlockSpec((1, d), lambda i, j: (0, 0)),
            vec, stacked(w1), stacked(w3), stacked(w2),
        ],
        out_specs=rows(d),
        out_shape=jax.ShapeDtypeStruct((b, s, d), F32),
        compiler_params=_params("arbitrary", "arbitrary"),
        name=name,
    )(*ys, *([wo_all] * len(ys)), x, gate_mix, shift, scale, g, gate, w1, w3, w2)


def _pad_cols(w, n):
    return jnp.pad(w, ((0, 0), (0, n - w.shape[1])))


def _gain_row(q_g, k_g, width, total):
    reps = width // HEAD_DIM
    row = jnp.concatenate([jnp.tile(q_g.astype(F32) * (HEAD_DIM ** -0.5 * LOG2E), reps),
                           jnp.tile(k_g.astype(F32), reps),
                           jnp.ones((total - 2 * width,), F32)])
    return row.reshape(1, total)


def kernel(x, c, ada_w, ada_b, norm_mix_g, norm_ffn_g, ab_w_in, ml_b_i, ml_b_f, ml_conv_w, ml_conv_b,
           da_q_g, da_k_g, da_lambda, da_subln_g, ab_w_out, fx_w_in, fx_b_f, fx_q_g, fx_k_g, fx_w_out,
           ffn_w1, ffn_w3, ffn_w2):
    depth = ada_w.shape[0]
    b, s, d = x.shape
    ml_width = ml_conv_w.shape[2] // 2
    ml_heads = ml_b_i.shape[1]
    da_width = (ab_w_in.shape[2] - 4 * ml_width - 2 * ml_heads) // 3
    da_heads = da_width // (2 * HEAD_DIM)
    fx_heads = fx_b_f.shape[1]
    fx_width = fx_heads * HEAD_DIM
    ab_main = 3 * da_width + 4 * ml_width
    fx_main = 3 * fx_width

    mods = _ada_call(c, ada_w, ada_b).reshape(depth, b, 6, 1, d)
    slopes = 2.0 ** (-8.0 * jnp.arange(1, da_heads + 1, dtype=F32) / da_heads)

    ab_w_in_b, fx_w_in_b = ab_w_in.astype(BF16), fx_w_in.astype(BF16)
    ab_w_out_b, fx_w_out_b = ab_w_out.astype(BF16), fx_w_out.astype(BF16)
    ffn_w1_b, ffn_w3_b, ffn_w2_b = ffn_w1.astype(BF16), ffn_w3.astype(BF16), ffn_w2.astype(BF16)

    for l in range(depth):
        sh_m, sc_m, g_m, sh_f, sc_f, g_f = (mods[l, :, t] for t in range(6))
        j = l // 2
        g_mix = norm_mix_g[l].reshape(1, d)
        if l % 2 == 0:
            z, gates = _inproj_call(
                x, sh_m, sc_m, g_mix, ab_w_in_b, j, ab_main,
                _pad_cols(ab_w_in[j][:, ab_main:], LANES).astype(BF16),
                _gain_row(da_q_g[j], da_k_g[j], da_width, ab_main),
                2 * da_width, "inproj_even")
            lambda_init = 0.8 - 0.6 * math.exp(-0.3 * l)
            y_da = _da_call(z, slopes, da_lambda[j], da_subln_g[j], da_heads, lambda_init)
            gate_bias = _pad_cols(jnp.concatenate([ml_b_i[j], ml_b_f[j]]).reshape(1, -1), LANES)
            y_ml = _mlstm_call(z, gates, gate_bias, ml_conv_w[j], ml_conv_b[j].reshape(1, -1),
                               ml_heads, 3 * da_width)
            ys, wo_all, name = [y_da, y_ml], ab_w_out_b, "mix_ffn_even"
        else:
            z, gates = _inproj_call(
                x, sh_m, sc_m, g_mix, fx_w_in_b, j, fx_main,
                _pad_cols(fx_w_in[j][:, fx_main:], LANES).astype(BF16),
                _gain_row(fx_q_g[j], fx_k_g[j], fx_width, fx_main),
                2 * fx_width, "inproj_odd")
            fneg = _fox_gate_call(gates, _pad_cols(fx_b_f[j].reshape(1, -1), LANES))
            ys, wo_all, name = [_fox_call(z, fneg, fx_heads)], fx_w_out_b, "mix_ffn_odd"
        x = _mix_ffn_call(ys, wo_all, j, x, g_m, sh_f, sc_f, norm_ffn_g[l].reshape(1, d), g_f,
                          ffn_w1_b, ffn_w3_b, ffn_w2_b, l, name)
    return x
```

```python
import functools
import math

import jax
import jax.numpy as jnp
from jax import lax
from jax.experimental import pallas as pl
from jax.experimental.pallas import tpu as pltpu

F32 = jnp.float32
BF16 = jnp.bfloat16

EPS = 1e-6
LANES = 128
HEAD_DIM = 64
CHUNK = 64
CHUNK_SHIFT = CHUNK.bit_length() - 1
assert CHUNK == 1 << CHUNK_SHIFT
NEG = -1e30
LOG2E = math.log2(math.e)
BIAS_PIECES = 3
ONES_ROWS = 16

VMEM_LIMIT = 56 * 1024 * 1024

ROW_TILE = 512
ATTN_TILE = 256
ATTN_KEYS = 512
ATTN_BLOCKS = 4
PACK_ROWS = 512
ML_CHUNK = 256
ML_CHUNKS_PER_STEP = 2
GATE_CHUNK = 256
FFN_CHUNK = 256


def _params(*sem):
    return pltpu.CompilerParams(dimension_semantics=sem, vmem_limit_bytes=VMEM_LIMIT)


def _log_sigmoid(x):
    return jnp.minimum(x, 0.0) - jnp.log1p(jnp.exp(-jnp.abs(x)))


def _sigmoid(x):
    return 0.5 * jnp.tanh(0.5 * x) + 0.5


def _prefix_max_lanes(x):
    axis = x.ndim - 1
    lane = lax.broadcasted_iota(jnp.int32, x.shape, axis)
    shift = 1
    while shift < x.shape[axis]:
        x = jnp.maximum(x, jnp.where(lane >= shift, pltpu.roll(x, shift, axis=axis), NEG))
        shift *= 2
    return x


def _split3(x):
    x1 = x.astype(BF16)
    r1 = x - x1.astype(F32)
    x2 = r1.astype(BF16)
    x3 = (r1 - x2.astype(F32)).astype(BF16)
    return x1, x2, x3


def _dot(a, b):
    return jnp.dot(a, b, preferred_element_type=F32)


def _dot_nt(a, b):
    return lax.dot_general(a, b, (((1,), (1,)), ((), ())), preferred_element_type=F32)


def _dot_tn(a, b):
    return lax.dot_general(a, b, (((0,), (0,)), ((), ())), preferred_element_type=F32)


def _exact_lhs_dot(m_bf16, x):
    x1, x2, x3 = _split3(x)
    return _dot(m_bf16, x1) + _dot(m_bf16, x2) + _dot(m_bf16, x3)


def _exact_rhs_dot(x, m_bf16):
    x1, x2, x3 = _split3(x)
    return _dot(x1, m_bf16) + _dot(x2, m_bf16) + _dot(x3, m_bf16)


def _norm_modulate(x, g, sh, sc):
    ms = jnp.mean(x * x, axis=-1, keepdims=True)
    y = x * lax.rsqrt(ms + EPS) * g
    return y * (1.0 + sc) + sh


def _ada_kernel(c_ref, w_ref, b_ref, o_ref):
    c = c_ref[...]
    ca = c * _sigmoid(c)
    c1, c2, c3 = _split3(ca)
    w = w_ref[...]
    w1 = w.astype(BF16)
    w2 = (w - w1.astype(F32)).astype(BF16)
    acc = _dot(c1, w1) + _dot(c1, w2) + _dot(c2, w1) + _dot(c2, w2) + _dot(c3, w1)
    o_ref[...] = acc + b_ref[...]


def _ada_call(c, ada_w, ada_b):
    depth, d, n = ada_w.shape
    b = c.shape[0]
    tn = 1536
    assert n % tn == 0
    return pl.pallas_call(
        _ada_kernel,
        grid=(depth, n // tn),
        in_specs=[
            pl.BlockSpec((b, d), lambda l, j: (0, 0)),
            pl.BlockSpec((None, d, tn), lambda l, j: (l, 0, j)),
            pl.BlockSpec((None, 1, tn), lambda l, j: (l, 0, j)),
        ],
        out_specs=pl.BlockSpec((None, b, tn), lambda l, j: (l, 0, j)),
        out_shape=jax.ShapeDtypeStruct((depth, b, n), F32),
        compiler_params=_params("arbitrary", "arbitrary"),
        name="ada_mod",
    )(c, ada_w, ada_b.reshape(depth, 1, n))


def _inproj_kernel(x_ref, sh_ref, sc_ref, g_ref, w_ref, wg_ref, gain_ref, z_ref, gate_ref,
                   *, n_norm, chunk):
    h = _norm_modulate(x_ref[...], g_ref[...], sh_ref[...], sc_ref[...]).astype(BF16)
    n = w_ref.shape[1]
    lane = lax.broadcasted_iota(jnp.int32, (1, LANES), 1)
    lo = lane < HEAD_DIM
    for c0 in range(0, n, chunk):
        zc = _dot(h, w_ref[:, c0:c0 + chunk])
        if c0 < n_norm:
            parts = []
            for s0 in range(0, chunk, LANES):
                zs = zc[:, s0:s0 + LANES]
                sq = zs * zs
                s_lo = jnp.sum(jnp.where(lo, sq, 0.0), axis=-1, keepdims=True)
                s_hi = jnp.sum(jnp.where(lo, 0.0, sq), axis=-1, keepdims=True)
                r_lo = lax.rsqrt(s_lo * (1.0 / HEAD_DIM) + EPS)
                r_hi = lax.rsqrt(s_hi * (1.0 / HEAD_DIM) + EPS)
                parts.append(zs * jnp.where(lo, r_lo, r_hi))
            zc = jnp.concatenate(parts, axis=-1) * gain_ref[:, c0:c0 + chunk]
        z_ref[:, c0:c0 + chunk] = zc.astype(BF16)
    gate_ref[...] = _dot(h, wg_ref[...])


def _inproj_call(x, shift, scale, g, w_all, layer, n, wg, gain, n_norm, name):
    b, s, d = x.shape
    tm = min(ROW_TILE, s)
    chunk = 512
    assert s % tm == 0 and n % chunk == 0 and n_norm % chunk == 0
    kern = functools.partial(_inproj_kernel, n_norm=n_norm, chunk=chunk)
    return pl.pallas_call(
        kern,
        grid=(b, s // tm),
        in_specs=[
            pl.BlockSpec((None, tm, d), lambda i, j: (i, j, 0)),
            pl.BlockSpec((None, 1, d), lambda i, j: (i, 0, 0)),
            pl.BlockSpec((None, 1, d), lambda i, j: (i, 0, 0)),
            pl.BlockSpec((1, d), lambda i, j: (0, 0)),
            pl.BlockSpec((None, d, n), lambda i, j: (layer, 0, 0)),
            pl.BlockSpec((d, LANES), lambda i, j: (0, 0)),
            pl.BlockSpec((1, n), lambda i, j: (0, 0)),
        ],
        out_specs=[
            pl.BlockSpec((None, tm, n), lambda i, j: (i, j, 0)),
            pl.BlockSpec((None, tm, LANES), lambda i, j: (i, j, 0)),
        ],
        out_shape=[
            jax.ShapeDtypeStruct((b, s, n), BF16),
            jax.ShapeDtypeStruct((b, s, LANES), F32),
        ],
        compiler_params=_params("arbitrary", "arbitrary"),
        name=name,
    )(x, shift, scale, g, w_all, wg, gain)


def _lane_ids():
    return lax.broadcasted_iota(jnp.int32, (1, LANES), 1)


def _place_pieces(pieces, first_lane):
    lane = _lane_ids()
    out = jnp.zeros(pieces[0].shape, F32)
    for j, p in enumerate(pieces):
        out = jnp.where(lane == first_lane + j, p.astype(F32), out)
    return out


def _pack_keys(k_ref, v_ref, extras_fn, ka_ref, kb_ref, vt_refs):
    s = k_ref.shape[0]
    rows = min(PACK_ROWS, s)
    keep_lo = jnp.where(_lane_ids() < HEAD_DIM, 1.0, 0.0).astype(BF16)
    keep_hi = jnp.where(_lane_ids() < HEAD_DIM, 0.0, 1.0).astype(BF16)

    def body(c, carry):
        r0 = pl.multiple_of(c * rows, rows)
        k2 = k_ref[pl.ds(r0, rows), :]
        ea, eb = extras_fn(r0, rows)
        ka_ref[pl.ds(r0, rows), :] = k2 * keep_lo + ea.astype(BF16)
        kb_ref[pl.ds(r0, rows), :] = k2 * keep_hi + eb.astype(BF16)
        vt = v_ref[pl.ds(r0, rows), :].T
        for ref, c0, width in vt_refs:
            ref[0:width, pl.ds(r0, rows)] = vt[c0:c0 + width, :]
            ref[width:width + ONES_ROWS, pl.ds(r0, rows)] = jnp.ones((ONES_ROWS, rows), BF16)
        return carry

    steps = s // rows
    lax.fori_loop(0, steps, body, 0, unroll=2 if steps % 2 == 0 else 1)


def _pack_queries(q2, factors):
    lane = _lane_ids()
    q2 = q2.astype(F32)
    lo = lane < HEAD_DIM
    fac_a = _place_pieces(factors, HEAD_DIM)
    fac_b = _place_pieces(factors, 0)
    return jnp.where(lo, q2, fac_a).astype(BF16), jnp.where(lo, fac_b, q2).astype(BF16)


def _init_stats(m_ref, acc_ref):
    m_ref[...] = jnp.full(m_ref.shape, NEG, F32)
    acc_ref[...] = jnp.zeros(acc_ref.shape, F32)


def _softmax_step(s, tile_max, vt, m_ref, acc_ref):
    m_old = m_ref[...]
    m_new = jnp.maximum(m_old, tile_max)
    alpha = jnp.exp2(m_old - m_new)
    p = jnp.exp2(s - m_new)
    acc_ref[...] = alpha * acc_ref[...] + _dot(vt, p.astype(BF16))
    m_ref[...] = m_new


def _attend(i, t, streams, s_buf, mx_buf, make_addend, finish_pair):
    tk = ATTN_KEYS
    n = (i * t + t + tk - 1) // tk

    def qk_one(step, slot, idx):
        k0 = pl.multiple_of(step * tk, tk)
        q, k_ref = streams[idx][0], streams[idx][1]
        s = _dot_nt(k_ref[pl.ds(k0, tk), :], q)
        s_buf[slot][idx] = s
        mx_buf[slot][idx] = jnp.max(s, axis=0, keepdims=True)

    def softmax_one(step, slot, idx, last, addends):
        k0 = pl.multiple_of(step * tk, tk)
        _, _, vt, add_key, stats = streams[idx]
        s = s_buf[slot][idx]
        if last:
            if add_key not in addends:
                addends[add_key] = make_addend(add_key)
            s = s + addends[add_key]
            tile_max = jnp.max(s, axis=0, keepdims=True)
        else:
            tile_max = mx_buf[slot][idx]
        _softmax_step(s, tile_max, vt(k0, tk), *stats)

    def qk_stage(step, slot):
        for idx in range(len(streams)):
            qk_one(step, slot, idx)

    def softmax_stage(step, slot, last):
        addends = {}
        for idx in range(len(streams)):
            softmax_one(step, slot, idx, last, addends)
            if last and idx % 2 == 1:
                finish_pair(idx // 2)

    def overlapped(qk_step, qk_slot, sm_step, sm_slot):
        for idx in range(len(streams)):
            qk_one(qk_step, qk_slot, idx)
            softmax_one(sm_step, sm_slot, idx, False, None)

    for idx in range(len(streams)):
        qk_one(0, 0, idx)
        _init_stats(*streams[idx][4])
    pairs = (n - 1) // 2

    def body(p, carry):
        overlapped(2 * p + 1, 1, 2 * p, 0)
        overlapped(2 * p + 2, 0, 2 * p + 1, 1)
        return carry

    lax.fori_loop(0, pairs, body, 0)

    @pl.when(n - 1 > 2 * pairs)
    def _():
        overlapped(n - 1, 1, n - 2, 0)
        softmax_stage(n - 1, 1, True)

    @pl.when(n - 1 == 2 * pairs)
    def _():
        softmax_stage(n - 1, 0, True)


def _last_step_offset(i, t):
    tk = ATTN_KEYS
    n = (i * t + t + tk - 1) // tk
    return i * t - (n - 1) * tk


def _lane_block(ref, h):
    return ref.at[:, pl.ds(h * LANES, LANES)]


def _da_kernel(slope_ref, q_ref, k_ref, v_ref, lam_ref, sg_ref, o_ref,
               ka_ref, kb_ref, vt_ref, m_ref, acc_ref, s0_ref, s1_ref, mx0_ref, mx1_ref,
               *, t, hb, lambda_init):
    dv = LANES
    g = pl.program_id(1)
    i = pl.program_id(2)
    slopes2 = [slope_ref[g * hb + h] * LOG2E for h in range(hb)]

    @pl.when(i == 0)
    def _():
        def extras(r0, rows):
            pos = lax.broadcasted_iota(jnp.int32, (rows, LANES), 0) + r0
            hi = jnp.right_shift(pos, CHUNK_SHIFT).astype(F32)
            lo = jnp.bitwise_and(pos, CHUNK - 1).astype(F32)
            lane = _lane_ids()

            def place(first):
                in_hi = (lane >= first) & (lane < first + BIAS_PIECES)
                in_lo = (lane >= first + BIAS_PIECES) & (lane < first + 2 * BIAS_PIECES)
                return jnp.where(in_hi, hi, jnp.where(in_lo, lo, 0.0))
            return place(HEAD_DIM), place(0)
        for h in range(hb):
            _pack_keys(_lane_block(k_ref, h), _lane_block(v_ref, h), extras,
                       ka_ref.at[h], kb_ref.at[h], [(vt_ref.at[h], 0, dv)])

    def last_step_addend(h):
        key = lax.broadcasted_iota(jnp.int32, (ATTN_KEYS, t), 0)
        qry = lax.broadcasted_iota(jnp.int32, (ATTN_KEYS, t), 1) + _last_step_offset(i, t)
        allowed = jnp.right_shift(key, CHUNK_SHIFT) <= jnp.right_shift(qry, CHUNK_SHIFT)
        above = jnp.maximum(key - qry, 0).astype(F32)
        return jnp.where(allowed, (-2.0 * slopes2[h]) * above, NEG)

    streams = []
    for h in range(hb):
        slope_pieces = [p.astype(F32) for p in _split3(jnp.full((1, LANES), slopes2[h], F32))]
        factors = [p * float(CHUNK) for p in slope_pieces] + slope_pieces
        qa, qb = _pack_queries(q_ref[:, h * LANES:(h + 1) * LANES], factors)
        vt = lambda k0, size, h=h: vt_ref[h, :, pl.ds(k0, size)]
        for m, (q, k_sc) in enumerate(((qa, ka_ref), (qb, kb_ref))):
            stats = (m_ref.at[2 * h + m], acc_ref.at[2 * h + m])
            streams.append((q, k_sc.at[h], vt, h, stats))
    def finish_head(h):
        lp = lam_ref[...]
        lam = (jnp.exp(jnp.sum(lp[0:1] * lp[1:2], axis=-1, keepdims=True))
               - jnp.exp(jnp.sum(lp[2:3] * lp[3:4], axis=-1, keepdims=True)) + lambda_init)
        pa = acc_ref[2 * h]
        pb = acc_ref[2 * h + 1]
        out = pa[0:dv] / pa[dv:dv + 1] - lam * (pb[0:dv] / pb[dv:dv + 1])
        ms = jnp.mean(out * out, axis=0, keepdims=True)
        out = out * lax.rsqrt(ms + EPS) * (sg_ref[...] * (1.0 - lambda_init))
        o_ref[:, h * LANES:(h + 1) * LANES] = out.T.astype(BF16)

    _attend(i, t, streams, (s0_ref, s1_ref), (mx0_ref, mx1_ref), last_step_addend, finish_head)


def _da_call(z, slopes, lam_p, subln_g, heads, lambda_init):
    b, s, _ = z.shape
    t = min(ATTN_TILE, s)
    assert s % ATTN_KEYS == 0 and t % CHUNK == 0
    assert s <= CHUNK * 256, "key position // CHUNK must stay exact in bf16"
    hb = min(ATTN_BLOCKS, heads)
    groups = heads // hb
    assert heads % hb == 0
    kern = functools.partial(_da_kernel, t=t, hb=hb, lambda_init=lambda_init)
    w = hb * LANES
    return pl.pallas_call(
        kern,
        grid_spec=pltpu.PrefetchScalarGridSpec(
            num_scalar_prefetch=1,
            grid=(b, groups, s // t),
            in_specs=[
                pl.BlockSpec((None, t, w), lambda bi, g, i, sl: (bi, i, g)),
                pl.BlockSpec((None, s, w), lambda bi, g, i, sl: (bi, 0, groups + g)),
                pl.BlockSpec((None, s, w), lambda bi, g, i, sl: (bi, 0, 2 * groups + g)),
                pl.BlockSpec((4, HEAD_DIM), lambda bi, g, i, sl: (0, 0)),
                pl.BlockSpec((LANES, 1), lambda bi, g, i, sl: (0, 0)),
            ],
            out_specs=pl.BlockSpec((None, t, w), lambda bi, g, i, sl: (bi, i, g)),
            scratch_shapes=[
                pltpu.VMEM((hb, s, LANES), BF16), pltpu.VMEM((hb, s, LANES), BF16),
                pltpu.VMEM((hb, LANES + ONES_ROWS, s), BF16),
                pltpu.VMEM((2 * hb, 1, t), F32),
                pltpu.VMEM((2 * hb, LANES + ONES_ROWS, t), F32),
                pltpu.VMEM((2 * hb, ATTN_KEYS, t), F32), pltpu.VMEM((2 * hb, ATTN_KEYS, t), F32),
                pltpu.VMEM((2 * hb, 1, t), F32), pltpu.VMEM((2 * hb, 1, t), F32),
            ],
        ),
        out_shape=jax.ShapeDtypeStruct((b, s, heads * LANES), BF16),
        compiler_params=_params("arbitrary", "arbitrary", "arbitrary"),
        name="diff_attention",
    )(slopes, z, z, z, lam_p, subln_g.reshape(LANES, 1))


def _fox_gate_kernel(g_ref, b_ref, o_ref, *, chunk):
    s = g_ref.shape[0]
    r = lax.broadcasted_iota(jnp.int32, (chunk, chunk), 0)
    c = lax.broadcasted_iota(jnp.int32, (chunk, chunk), 1)
    tri = (c <= r).astype(BF16)
    carry = jnp.zeros((1, LANES), F32)
    for c0 in range(0, s, chunk):
        ls = _log_sigmoid(g_ref[c0:c0 + chunk, :] + b_ref[...])
        cs = _exact_lhs_dot(tri, ls) + carry
        carry = cs[chunk - 1:chunk, :]
        o_ref[c0:c0 + chunk, :] = cs * (-LOG2E)


def _fox_gate_call(gates, bias_row):
    b, s, _ = gates.shape
    chunk = min(GATE_CHUNK, s)
    assert s % chunk == 0
    kern = functools.partial(_fox_gate_kernel, chunk=chunk)
    return pl.pallas_call(
        kern,
        grid=(b,),
        in_specs=[
            pl.BlockSpec((None, s, LANES), lambda i: (i, 0, 0)),
            pl.BlockSpec((1, LANES), lambda i: (0, 0)),
        ],
        out_specs=pl.BlockSpec((None, s, LANES), lambda i: (i, 0, 0)),
        out_shape=jax.ShapeDtypeStruct((b, s, LANES), F32),
        compiler_params=_params("arbitrary"),
        name="fox_gate_cumsum",
    )(gates, bias_row)


def _fox_kernel(q_ref, k_ref, v_ref, f_ref, o_ref,
                ka_ref, kb_ref, vt_ref, m_ref, acc_ref, s0_ref, s1_ref, mx0_ref, mx1_ref, *, t, hb):
    dv = HEAD_DIM
    g = pl.program_id(1)
    i = pl.program_id(2)

    @pl.when(i == 0)
    def _():
        n = BIAS_PIECES * LANES
        src = lax.broadcasted_iota(jnp.int32, (n, LANES), 0)
        dst = lax.broadcasted_iota(jnp.int32, (n, LANES), 1)
        piece = jnp.right_shift(src, LANES.bit_length() - 1)
        head = src - piece * LANES
        for h in range(hb):
            pair = g * hb + h
            place_a = ((head == 2 * pair) & (dst == HEAD_DIM + piece)).astype(BF16)
            place_b = ((head == 2 * pair + 1) & (dst == piece)).astype(BF16)
            place = jnp.concatenate([place_a, place_b], axis=-1)

            def extras(r0, rows, place=place):
                pieces = jnp.concatenate(_split3(f_ref[pl.ds(r0, rows), :]), axis=-1)
                both = _dot(pieces, place)
                return both[:, 0:LANES], both[:, LANES:2 * LANES]
            _pack_keys(_lane_block(k_ref, h), _lane_block(v_ref, h), extras,
                       ka_ref.at[h], kb_ref.at[h],
                       [(vt_ref.at[2 * h + m], m * dv, dv) for m in range(2)])

    def last_step_addend(_):
        key = lax.broadcasted_iota(jnp.int32, (ATTN_KEYS, t), 0)
        qry = lax.broadcasted_iota(jnp.int32, (ATTN_KEYS, t), 1) + _last_step_offset(i, t)
        return jnp.where(key <= qry, 0.0, NEG)

    streams = []
    ones = [jnp.ones((1, LANES), F32)] * BIAS_PIECES
    for h in range(hb):
        qa, qb = _pack_queries(q_ref[:, h * LANES:(h + 1) * LANES], ones)
        for m, (q, k_sc) in enumerate(((qa, ka_ref), (qb, kb_ref))):
            vt = lambda k0, size, n=2 * h + m: vt_ref[n, :, pl.ds(k0, size)]
            stats = (m_ref.at[2 * h + m], acc_ref.at[2 * h + m])
            streams.append((q, k_sc.at[h], vt, 0, stats))
    def finish_pair(h):
        pa = acc_ref[2 * h]
        pb = acc_ref[2 * h + 1]
        out = jnp.concatenate([pa[0:dv] / pa[dv:dv + 1], pb[0:dv] / pb[dv:dv + 1]], axis=0)
        o_ref[:, h * LANES:(h + 1) * LANES] = out.T.astype(BF16)

    _attend(i, t, streams, (s0_ref, s1_ref), (mx0_ref, mx1_ref), last_step_addend, finish_pair)


def _fox_call(z, fneg, heads):
    b, s, _ = z.shape
    pairs = heads // 2
    t = min(ATTN_TILE, s)
    assert s % ATTN_KEYS == 0
    hb = min(ATTN_BLOCKS, pairs)
    groups = pairs // hb
    assert pairs % hb == 0
    kern = functools.partial(_fox_kernel, t=t, hb=hb)
    w = hb * LANES
    return pl.pallas_call(
        kern,
        grid=(b, groups, s // t),
        in_specs=[
            pl.BlockSpec((None, t, w), lambda bi, g, i: (bi, i, g)),
            pl.BlockSpec((None, s, w), lambda bi, g, i: (bi, 0, groups + g)),
            pl.BlockSpec((None, s, w), lambda bi, g, i: (bi, 0, 2 * groups + g)),
            pl.BlockSpec((None, s, LANES), lambda bi, g, i: (bi, 0, 0)),
        ],
        out_specs=pl.BlockSpec((None, t, w), lambda bi, g, i: (bi, i, g)),
        out_shape=jax.ShapeDtypeStruct((b, s, pairs * LANES), BF16),
        scratch_shapes=[
            pltpu.VMEM((hb, s, LANES), BF16), pltpu.VMEM((hb, s, LANES), BF16),
            pltpu.VMEM((2 * hb, HEAD_DIM + ONES_ROWS, s), BF16),
            pltpu.VMEM((2 * hb, 1, t), F32),
            pltpu.VMEM((2 * hb, HEAD_DIM + ONES_ROWS, t), F32),
            pltpu.VMEM((2 * hb, ATTN_KEYS, t), F32), pltpu.VMEM((2 * hb, ATTN_KEYS, t), F32),
            pltpu.VMEM((2 * hb, 1, t), F32), pltpu.VMEM((2 * hb, 1, t), F32),
        ],
        compiler_params=_params("arbitrary", "arbitrary", "arbitrary"),
        name="forgetting_attention",
    )(z, z, z, fneg)


def _mlstm_kernel(q_ref, k_ref, v_ref, o_ref, g_ref, gb_ref, cw_ref, cb_ref, y_ref,
                  halo_ref, ct_ref, m_ref, *, heads, taps, lc):
    @pl.when(pl.program_id(1) == 0)
    def _():
        halo_ref[...] = jnp.zeros(halo_ref.shape, F32)
        ct_ref[...] = jnp.zeros(ct_ref.shape, F32)
        m_ref[...] = jnp.zeros(m_ref.shape, F32)

    rows = lambda ref, sub: ref.at[pl.ds(sub * lc, lc), :]
    edge = halo_ref[...]
    for sub in range(q_ref.shape[0] // lc):
        prep, edge = _mlstm_prep(rows(q_ref, sub), rows(k_ref, sub), rows(g_ref, sub), gb_ref,
                                 cw_ref, cb_ref, edge, heads, taps)
        for hd in range(heads):
            _mlstm_head(prep, rows(v_ref, sub), rows(o_ref, sub), rows(y_ref, sub),
                        ct_ref, m_ref, hd, heads)
    halo_ref[...] = edge


def _mlstm_prep(q_ref, k_ref, g_ref, gb_ref, cw_ref, cb_ref, halo, heads, taps):
    lc, width = q_ref.shape
    pad = halo.shape[0]

    r = lax.broadcasted_iota(jnp.int32, (lc, lc), 0)
    c = lax.broadcasted_iota(jnp.int32, (lc, lc), 1)
    causal = c <= r

    xb = jnp.concatenate([q_ref[...], k_ref[...]], axis=-1)
    xf = xb.astype(F32)
    conv = cb_ref[...] + xf * cw_ref[taps - 1:taps, :]
    for sft in range(1, taps):
        shifted = _dot((r - c == sft).astype(BF16), xb)
        conv = conv + shifted * cw_ref[taps - 1 - sft:taps - sft, :]
    edge = jnp.concatenate([halo, xf[0:pad, :]], axis=0)
    first = cb_ref[...]
    for j in range(taps):
        off = pad - (taps - 1) + j
        first = first + edge[off:off + pad, :] * cw_ref[j:j + 1, :]
    conv = jnp.concatenate([first, conv[pad:, :]], axis=0)
    qk = conv * _sigmoid(conv)

    gcol = g_ref[...] + gb_ref[...]
    grow = gcol.T[:2 * heads, :]
    a_cols = _exact_lhs_dot(causal.astype(BF16), _log_sigmoid(gcol))
    a_rows = _exact_rhs_dot(_log_sigmoid(grow), (r <= c).astype(BF16))

    b_half = grow[0:heads, :] - a_rows[heads:2 * heads, :]
    b_rows = jnp.concatenate([b_half, b_half], axis=0)
    prep = dict(qk=qk, gcol=gcol, a_cols=a_cols, b_rows=b_rows, causal=causal,
                prefix_cols=_prefix_max_lanes(b_rows).T)
    return prep, xf[lc - pad:, :]


def _mlstm_head(prep, v_ref, o_ref, y_ref, ct_ref, m_ref, hd, heads):
    lc, width = v_ref.shape
    d = width // heads
    qk, gcol, a_cols, causal = prep["qk"], prep["gcol"], prep["a_cols"], prep["causal"]
    q = qk[:, hd * d:(hd + 1) * d]
    k = qk[:, width + hd * d:width + (hd + 1) * d] * (d ** -0.5)
    v = v_ref[:, hd * d:(hd + 1) * d]
    qb = q.astype(BF16)
    kb = k.astype(BF16)
    ig_c = gcol[:, hd:hd + 1]
    a_c = a_cols[:, heads + hd:heads + hd + 1]
    a_last = a_c[lc - 1:lc, :]
    ct = ct_ref[hd]
    m_prev = m_ref[hd:hd + 1, 0:1]
    x_c = jnp.maximum(prep["prefix_cols"][:, hd:hd + 1], m_prev)

    decay_mat = jnp.exp2(jnp.where(causal, prep["b_rows"][hd:hd + 1, :] * LOG2E - x_c * LOG2E, NEG))
    sqk = (_dot_nt(qb, kb) * decay_mat).astype(BF16)
    intra = _dot(sqk, jnp.concatenate([v, jnp.ones((lc, d), BF16)], axis=-1))
    inter = jnp.exp(m_prev - x_c) * _dot(qb, ct.astype(BF16))
    num = inter[:, 0:d] + intra[:, 0:d]
    den = inter[:, d:d + 1] + intra[:, d:d + 1]
    hval = num / jnp.maximum(jnp.abs(den), jnp.exp(-(a_c + x_c)))
    og = o_ref[:, hd * d:(hd + 1) * d].astype(F32)
    y_ref[:, hd * d:(hd + 1) * d] = (_sigmoid(og) * hval).astype(BF16)

    g_c = a_last - a_c + ig_c
    m_loc = jnp.max(g_c, axis=0, keepdims=True)
    w_c = jnp.exp(g_c - m_loc)
    m_new = jnp.maximum(a_last + m_prev, m_loc)
    decay = jnp.exp(a_last + m_prev - m_new)
    s_loc = jnp.exp(m_loc - m_new)
    wv = jnp.concatenate([w_c * v.astype(F32), jnp.broadcast_to(w_c, (lc, d))], axis=-1)
    ct_ref[hd] = decay * ct + s_loc * _dot_tn(kb, wv.astype(BF16))
    m_ref[hd:hd + 1, :] = jnp.broadcast_to(m_new, (1, LANES))


def _mlstm_call(z, gates, gate_bias, conv_w, conv_b, heads, col0):
    b, s, _ = z.shape
    width = conv_w.shape[1] // 2
    lc = min(ML_CHUNK, s)
    rows = min(ML_CHUNKS_PER_STEP * lc, s)
    assert s % rows == 0 and rows % lc == 0 and col0 % width == 0 and width // heads == LANES
    assert 2 * heads == 8, "gate rows are handled as one 8-sublane tile"
    cb = col0 // width
    taps = conv_w.shape[0]
    kern = functools.partial(_mlstm_kernel, heads=heads, taps=taps, lc=lc)
    zspec = lambda off: pl.BlockSpec((None, rows, width), lambda bi, ci: (bi, ci, cb + off))
    return pl.pallas_call(
        kern,
        grid=(b, s // rows),
        in_specs=[
            zspec(0), zspec(1), zspec(2), zspec(3),
            pl.BlockSpec((None, rows, LANES), lambda bi, ci: (bi, ci, 0)),
            pl.BlockSpec((1, LANES), lambda bi, ci: (0, 0)),
            pl.BlockSpec((taps, 2 * width), lambda bi, ci: (0, 0)),
            pl.BlockSpec((1, 2 * width), lambda bi, ci: (0, 0)),
        ],
        out_specs=pl.BlockSpec((None, rows, width), lambda bi, ci: (bi, ci, 0)),
        out_shape=jax.ShapeDtypeStruct((b, s, width), BF16),
        scratch_shapes=[
            pltpu.VMEM((8, 2 * width), F32),
            pltpu.VMEM((heads, LANES, 2 * LANES), F32),
            pltpu.VMEM((2 * heads, LANES), F32),
        ],
        compiler_params=_params("arbitrary", "arbitrary"),
        name="mlstm",
    )(z, z, z, z, gates, gate_bias, conv_w, conv_b)


def _mix_ffn_kernel(*refs, n_in, chunk):
    y_refs = refs[:n_in]
    wo_refs = refs[n_in:2 * n_in]
    (x_ref, gm_ref, sh_ref, sc_ref, g_ref, gate_ref, w1_ref, w3_ref, w2_ref, o_ref) = refs[2 * n_in:]
    mix = _dot(y_refs[0][...], wo_refs[0][...])
    for y_ref, w_ref in zip(y_refs[1:], wo_refs[1:]):
        mix = mix + _dot(y_ref[...], w_ref[...])
    x = x_ref[...] + gm_ref[...] * mix
    h = _norm_modulate(x, g_ref[...], sh_ref[...], sc_ref[...]).astype(BF16)
    hidden = w1_ref.shape[1]
    acc = None
    for c0 in range(0, hidden, chunk):
        a = _dot(h, w1_ref[:, c0:c0 + chunk])
        bb = _dot(h, w3_ref[:, c0:c0 + chunk])
        u = (a * _sigmoid(a) * bb).astype(BF16)
        part = _dot(u, w2_ref[c0:c0 + chunk, :])
        acc = part if acc is None else acc + part
    o_ref[...] = x + gate_ref[...] * acc


def _mix_ffn_call(ys, wo_all, wo_layer, x, gate_mix, shift, scale, g, gate, w1, w3, w2, layer, name):
    b, s, d = x.shape
    hidden = w1.shape[2]
    tm = min(ROW_TILE, s)
    assert s % tm == 0 and hidden % FFN_CHUNK == 0
    assert all(y.shape[2] == ys[0].shape[2] for y in ys)
    kern = functools.partial(_mix_ffn_kernel, n_in=len(ys), chunk=FFN_CHUNK)
    vec = pl.BlockSpec((None, 1, d), lambda i, j: (i, 0, 0))
    rows = lambda width: pl.BlockSpec((None, tm, width), lambda i, j: (i, j, 0))
    stacked = lambda w: pl.BlockSpec((None,) + w.shape[1:], lambda i, j: (layer, 0, 0))
    wo_specs = [pl.BlockSpec((None, y.shape[2], d), lambda i, j, n=n: (wo_layer, n, 0))
                for n, y in enumerate(ys)]
    return pl.pallas_call(
        kern,
        grid=(b, s // tm),
        in_specs=[rows(y.shape[2]) for y in ys] + wo_specs + [
            rows(d), vec, vec, vec,
            pl.BlockSpec((1, d), lambda i, j: (0, 0)),
            vec, stacked(w1), stacked(w3), stacked(w2),
        ],
        out_specs=rows(d),
        out_shape=jax.ShapeDtypeStruct((b, s, d), F32),
        compiler_params=_params("arbitrary", "arbitrary"),
        name=name,
    )(*ys, *([wo_all] * len(ys)), x, gate_mix, shift, scale, g, gate, w1, w3, w2)


def _pad_cols(w, n):
    return jnp.pad(w, ((0, 0), (0, n - w.shape[1])))


def _gain_row(q_g, k_g, width, total):
    reps = width // HEAD_DIM
    row = jnp.concatenate([jnp.tile(q_g.astype(F32) * (HEAD_DIM ** -0.5 * LOG2E), reps),
                           jnp.tile(k_g.astype(F32), reps),
                           jnp.ones((total - 2 * width,), F32)])
    return row.reshape(1, total)


def kernel(x, c, ada_w, ada_b, norm_mix_g, norm_ffn_g, ab_w_in, ml_b_i, ml_b_f, ml_conv_w, ml_conv_b,
           da_q_g, da_k_g, da_lambda, da_subln_g, ab_w_out, fx_w_in, fx_b_f, fx_q_g, fx_k_g, fx_w_out,
           ffn_w1, ffn_w3, ffn_w2):
    depth = ada_w.shape[0]
    b, s, d = x.shape
    ml_width = ml_conv_w.shape[2] // 2
    ml_heads = ml_b_i.shape[1]
    da_width = (ab_w_in.shape[2] - 4 * ml_width - 2 * ml_heads) // 3
    da_heads = da_width // (2 * HEAD_DIM)
    fx_heads = fx_b_f.shape[1]
    fx_width = fx_heads * HEAD_DIM
    ab_main = 3 * da_width + 4 * ml_width
    fx_main = 3 * fx_width

    mods = _ada_call(c, ada_w, ada_b).reshape(depth, b, 6, 1, d)
    slopes = 2.0 ** (-8.0 * jnp.arange(1, da_heads + 1, dtype=F32) / da_heads)

    ab_w_in_b, fx_w_in_b = ab_w_in.astype(BF16), fx_w_in.astype(BF16)
    ab_w_out_b, fx_w_out_b = ab_w_out.astype(BF16), fx_w_out.astype(BF16)
    ffn_w1_b, ffn_w3_b, ffn_w2_b = ffn_w1.astype(BF16), ffn_w3.astype(BF16), ffn_w2.astype(BF16)

    for l in range(depth):
        sh_m, sc_m, g_m, sh_f, sc_f, g_f = (mods[l, :, t] for t in range(6))
        j = l // 2
        g_mix = norm_mix_g[l].reshape(1, d)
        if l % 2 == 0:
            z, gates = _inproj_call(
                x, sh_m, sc_m, g_mix, ab_w_in_b, j, ab_main,
                _pad_cols(ab_w_in[j][:, ab_main:], LANES).astype(BF16),
                _gain_row(da_q_g[j], da_k_g[j], da_width, ab_main),
                2 * da_width, "inproj_even")
            lambda_init = 0.8 - 0.6 * math.exp(-0.3 * l)
            y_da = _da_call(z, slopes, da_lambda[j], da_subln_g[j], da_heads, lambda_init)
            gate_bias = _pad_cols(jnp.concatenate([ml_b_i[j], ml_b_f[j]]).reshape(1, -1), LANES)
            y_ml = _mlstm_call(z, gates, gate_bias, ml_conv_w[j], ml_conv_b[j].reshape(1, -1),
                               ml_heads, 3 * da_width)
            ys, wo_all, name = [y_da, y_ml], ab_w_out_b, "mix_ffn_even"
        else:
            z, gates = _inproj_call(
                x, sh_m, sc_m, g_mix, fx_w_in_b, j, fx_main,
                _pad_cols(fx_w_in[j][:, fx_main:], LANES).astype(BF16),
                _gain_row(fx_q_g[j], fx_k_g[j], fx_width, fx_main),
                2 * fx_width, "inproj_odd")
            fneg = _fox_gate_call(gates, _pad_cols(fx_b_f[j].reshape(1, -1), LANES))
            ys, wo_all, name = [_fox_call(z, fneg, fx_heads)], fx_w_out_b, "mix_ffn_odd"
        x = _mix_ffn_call(ys, wo_all, j, x, g_m, sh_f, sc_f, norm_ffn_g[l].reshape(1, d), g_f,
                          ffn_w1_b, ffn_w3_b, ffn_w2_b, l, name)
    return x
```

```python
import functools
import math

import jax
import jax.numpy as jnp
from jax import lax
from jax.experimental import pallas as pl
from jax.experimental.pallas import tpu as pltpu

F32 = jnp.float32
BF16 = jnp.bfloat16

EPS = 1e-6
LANES = 128
HEAD_DIM = 64
CHUNK = 64
CHUNK_SHIFT = CHUNK.bit_length() - 1
assert CHUNK == 1 << CHUNK_SHIFT
NEG = -1e30
LOG2E = math.log2(math.e)
BIAS_PIECES = 3
ONES_ROWS = 16

VMEM_LIMIT = 56 * 1024 * 1024

ROW_TILE = 512
ATTN_TILE = 256
ATTN_KEYS = 512
ATTN_BLOCKS = 4
PACK_ROWS = 512
ML_CHUNK = 256
ML_CHUNKS_PER_STEP = 2
GATE_CHUNK = 256
FFN_CHUNK = 256


def _params(*sem):
    return pltpu.CompilerParams(dimension_semantics=sem, vmem_limit_bytes=VMEM_LIMIT)


def _log_sigmoid(x):
    return jnp.minimum(x, 0.0) - jnp.log1p(jnp.exp(-jnp.abs(x)))


def _sigmoid(x):
    return 0.5 * jnp.tanh(0.5 * x) + 0.5


def _prefix_max_lanes(x):
    axis = x.ndim - 1
    lane = lax.broadcasted_iota(jnp.int32, x.shape, axis)
    shift = 1
    while shift < x.shape[axis]:
        x = jnp.maximum(x, jnp.where(lane >= shift, pltpu.roll(x, shift, axis=axis), NEG))
        shift *= 2
    return x


def _split3(x):
    x1 = x.astype(BF16)
    r1 = x - x1.astype(F32)
    x2 = r1.astype(BF16)
    x3 = (r1 - x2.astype(F32)).astype(BF16)
    return x1, x2, x3


def _dot(a, b):
    return jnp.dot(a, b, preferred_element_type=F32)


def _dot_nt(a, b):
    return lax.dot_general(a, b, (((1,), (1,)), ((), ())), preferred_element_type=F32)


def _dot_tn(a, b):
    return lax.dot_general(a, b, (((0,), (0,)), ((), ())), preferred_element_type=F32)


def _exact_lhs_dot(m_bf16, x):
    x1, x2, x3 = _split3(x)
    return _dot(m_bf16, x1) + _dot(m_bf16, x2) + _dot(m_bf16, x3)


def _exact_rhs_dot(x, m_bf16):
    x1, x2, x3 = _split3(x)
    return _dot(x1, m_bf16) + _dot(x2, m_bf16) + _dot(x3, m_bf16)


def _norm_modulate(x, g, sh, sc):
    ms = jnp.mean(x * x, axis=-1, keepdims=True)
    y = x * lax.rsqrt(ms + EPS) * g
    return y * (1.0 + sc) + sh


def _ada_kernel(c_ref, w_ref, b_ref, o_ref):
    c = c_ref[...]
    ca = c * _sigmoid(c)
    c1, c2, c3 = _split3(ca)
    w = w_ref[...]
    w1 = w.astype(BF16)
    w2 = (w - w1.astype(F32)).astype(BF16)
    acc = _dot(c1, w1) + _dot(c1, w2) + _dot(c2, w1) + _dot(c2, w2) + _dot(c3, w1)
    o_ref[...] = acc + b_ref[...]


def _ada_call(c, ada_w, ada_b):
    depth, d, n = ada_w.shape
    b = c.shape[0]
    tn = 1536
    assert n % tn == 0
    return pl.pallas_call(
        _ada_kernel,
        grid=(depth, n // tn),
        in_specs=[
            pl.BlockSpec((b, d), lambda l, j: (0, 0)),
            pl.BlockSpec((None, d, tn), lambda l, j: (l, 0, j)),
            pl.BlockSpec((None, 1, tn), lambda l, j: (l, 0, j)),
        ],
        out_specs=pl.BlockSpec((None, b, tn), lambda l, j: (l, 0, j)),
        out_shape=jax.ShapeDtypeStruct((depth, b, n), F32),
        compiler_params=_params("arbitrary", "arbitrary"),
        name="ada_mod",
    )(c, ada_w, ada_b.reshape(depth, 1, n))


def _inproj_kernel(x_ref, sh_ref, sc_ref, g_ref, w_ref, wg_ref, gain_ref, z_ref, gate_ref,
                   *, n_norm, chunk):
    half = x_ref.shape[0] // 2
    h_rows, first_rows = [], []
    for r0 in (0, half):
        h_part = _norm_modulate(x_ref[r0:r0 + half, :], g_ref[...], sh_ref[...], sc_ref[...]).astype(BF16)
        first_rows.append(_dot(h_part, w_ref[:, 0:chunk]))
        h_rows.append(h_part)
    h = jnp.concatenate(h_rows, axis=0)
    n = w_ref.shape[1]
    lane = lax.broadcasted_iota(jnp.int32, (1, LANES), 1)
    lo = lane < HEAD_DIM
    for c0 in range(0, n, chunk):
        zc = jnp.concatenate(first_rows, axis=0) if c0 == 0 else _dot(h, w_ref[:, c0:c0 + chunk])
        if c0 < n_norm:
            parts = []
            for s0 in range(0, chunk, LANES):
                zs = zc[:, s0:s0 + LANES]
                sq = zs * zs
                s_lo = jnp.sum(jnp.where(lo, sq, 0.0), axis=-1, keepdims=True)
                s_hi = jnp.sum(jnp.where(lo, 0.0, sq), axis=-1, keepdims=True)
                r_lo = lax.rsqrt(s_lo * (1.0 / HEAD_DIM) + EPS)
                r_hi = lax.rsqrt(s_hi * (1.0 / HEAD_DIM) + EPS)
                parts.append(zs * jnp.where(lo, r_lo, r_hi))
            zc = jnp.concatenate(parts, axis=-1) * gain_ref[:, c0:c0 + chunk]
        z_ref[:, c0:c0 + chunk] = zc.astype(BF16)
    gate_ref[...] = _dot(h, wg_ref[...])


def _inproj_call(x, shift, scale, g, w_all, layer, n, wg, gain, n_norm, name):
    b, s, d = x.shape
    tm = min(ROW_TILE, s)
    chunk = 512
    assert s % tm == 0 and n % chunk == 0 and n_norm % chunk == 0
    kern = functools.partial(_inproj_kernel, n_norm=n_norm, chunk=chunk)
    return pl.pallas_call(
        kern,
        grid=(b, s // tm),
        in_specs=[
            pl.BlockSpec((None, tm, d), lambda i, j: (i, j, 0)),
            pl.BlockSpec((None, 1, d), lambda i, j: (i, 0, 0)),
            pl.BlockSpec((None, 1, d), lambda i, j: (i, 0, 0)),
            pl.BlockSpec((1, d), lambda i, j: (0, 0)),
            pl.BlockSpec((None, d, n), lambda i, j: (layer, 0, 0)),
            pl.BlockSpec((d, LANES), lambda i, j: (0, 0)),
            pl.BlockSpec((1, n), lambda i, j: (0, 0)),
        ],
        out_specs=[
            pl.BlockSpec((None, tm, n), lambda i, j: (i, j, 0)),
            pl.BlockSpec((None, tm, LANES), lambda i, j: (i, j, 0)),
        ],
        out_shape=[
            jax.ShapeDtypeStruct((b, s, n), BF16),
            jax.ShapeDtypeStruct((b, s, LANES), F32),
        ],
        compiler_params=_params("arbitrary", "arbitrary"),
        name=name,
    )(x, shift, scale, g, w_all, wg, gain)


def _lane_ids():
    return lax.broadcasted_iota(jnp.int32, (1, LANES), 1)


def _place_pieces(pieces, first_lane):
    lane = _lane_ids()
    out = jnp.zeros(pieces[0].shape, F32)
    for j, p in enumerate(pieces):
        out = jnp.where(lane == first_lane + j, p.astype(F32), out)
    return out


def _pack_keys(k_ref, v_ref, extras_fn, ka_ref, kb_ref, vt_refs):
    s = k_ref.shape[0]
    rows = min(PACK_ROWS, s)
    keep_lo = jnp.where(_lane_ids() < HEAD_DIM, 1.0, 0.0).astype(BF16)
    keep_hi = jnp.where(_lane_ids() < HEAD_DIM, 0.0, 1.0).astype(BF16)

    def body(c, carry):
        r0 = pl.multiple_of(c * rows, rows)
        k2 = k_ref[pl.ds(r0, rows), :]
        ea, eb = extras_fn(r0, rows)
        ka_ref[pl.ds(r0, rows), :] = k2 * keep_lo + ea.astype(BF16)
        kb_ref[pl.ds(r0, rows), :] = k2 * keep_hi + eb.astype(BF16)
        vt = v_ref[pl.ds(r0, rows), :].T
        for ref, c0, width in vt_refs:
            ref[0:width, pl.ds(r0, rows)] = vt[c0:c0 + width, :]
            ref[width:width + ONES_ROWS, pl.ds(r0, rows)] = jnp.ones((ONES_ROWS, rows), BF16)
        return carry

    steps = s // rows
    lax.fori_loop(0, steps, body, 0, unroll=2 if steps % 2 == 0 else 1)


def _pack_queries(q2, factors):
    lane = _lane_ids()
    q2 = q2.astype(F32)
    lo = lane < HEAD_DIM
    fac_a = _place_pieces(factors, HEAD_DIM)
    fac_b = _place_pieces(factors, 0)
    return jnp.where(lo, q2, fac_a).astype(BF16), jnp.where(lo, fac_b, q2).astype(BF16)


def _init_stats(m_ref, acc_ref):
    m_ref[...] = jnp.full(m_ref.shape, NEG, F32)
    acc_ref[...] = jnp.zeros(acc_ref.shape, F32)


def _softmax_step(s, tile_max, vt, m_ref, acc_ref):
    m_old = m_ref[...]
    m_new = jnp.maximum(m_old, tile_max)
    alpha = jnp.exp2(m_old - m_new)
    p = jnp.exp2(s - m_new)
    acc_ref[...] = alpha * acc_ref[...] + _dot(vt, p.astype(BF16))
    m_ref[...] = m_new


def _attend(i, t, streams, s_buf, mx_buf, make_addend, finish_pair):
    tk = ATTN_KEYS
    n = (i * t + t + tk - 1) // tk

    def qk_one(step, slot, idx):
        k0 = pl.multiple_of(step * tk, tk)
        q, k_ref = streams[idx][0], streams[idx][1]
        s = _dot_nt(k_ref[pl.ds(k0, tk), :], q)
        s_buf[slot][idx] = s
        mx_buf[slot][idx] = jnp.max(s, axis=0, keepdims=True)

    def softmax_one(step, slot, idx, last, addends):
        k0 = pl.multiple_of(step * tk, tk)
        _, _, vt, add_key, stats = streams[idx]
        s = s_buf[slot][idx]
        if last:
            if add_key not in addends:
                addends[add_key] = make_addend(add_key, addends.setdefault("shared", {}))
            s = s + addends[add_key]
            tile_max = jnp.max(s, axis=0, keepdims=True)
        else:
            tile_max = mx_buf[slot][idx]
        _softmax_step(s, tile_max, vt(k0, tk), *stats)

    def qk_stage(step, slot):
        for idx in range(len(streams)):
            qk_one(step, slot, idx)

    def softmax_stage(step, slot, last):
        addends = {}
        for idx in range(len(streams)):
            softmax_one(step, slot, idx, last, addends)
            if last and idx % 2 == 1:
                finish_pair(idx // 2)

    def overlapped(qk_step, qk_slot, sm_step, sm_slot):
        for idx in range(len(streams)):
            qk_one(qk_step, qk_slot, idx)
            softmax_one(sm_step, sm_slot, idx, False, None)

    for idx in range(len(streams)):
        qk_one(0, 0, idx)
        _init_stats(*streams[idx][4])
    pairs = (n - 1) // 2

    def body(p, carry):
        overlapped(2 * p + 1, 1, 2 * p, 0)
        overlapped(2 * p + 2, 0, 2 * p + 1, 1)
        return carry

    lax.fori_loop(0, pairs, body, 0)

    @pl.when(n - 1 > 2 * pairs)
    def _():
        overlapped(n - 1, 1, n - 2, 0)
        softmax_stage(n - 1, 1, True)

    @pl.when(n - 1 == 2 * pairs)
    def _():
        softmax_stage(n - 1, 0, True)


def _last_step_offset(i, t):
    tk = ATTN_KEYS
    n = (i * t + t + tk - 1) // tk
    return i * t - (n - 1) * tk


def _lane_block(ref, h):
    return ref.at[:, pl.ds(h * LANES, LANES)]


def _da_kernel(slope_ref, q_ref, k_ref, v_ref, lam_ref, sg_ref, o_ref,
               ka_ref, kb_ref, vt_ref, m_ref, acc_ref, s0_ref, s1_ref, mx0_ref, mx1_ref,
               *, t, hb, lambda_init):
    dv = LANES
    g = pl.program_id(1)
    i = pl.program_id(2)
    slopes2 = [slope_ref[g * hb + h] * LOG2E for h in range(hb)]

    @pl.when(i == 0)
    def _():
        def extras(r0, rows):
            pos = lax.broadcasted_iota(jnp.int32, (rows, LANES), 0) + r0
            hi = jnp.right_shift(pos, CHUNK_SHIFT).astype(F32)
            lo = jnp.bitwise_and(pos, CHUNK - 1).astype(F32)
            lane = _lane_ids()

            def place(first):
                in_hi = (lane >= first) & (lane < first + BIAS_PIECES)
                in_lo = (lane >= first + BIAS_PIECES) & (lane < first + 2 * BIAS_PIECES)
                return jnp.where(in_hi, hi, jnp.where(in_lo, lo, 0.0))
            return place(HEAD_DIM), place(0)
        for h in range(hb):
            _pack_keys(_lane_block(k_ref, h), _lane_block(v_ref, h), extras,
                       ka_ref.at[h], kb_ref.at[h], [(vt_ref.at[h], 0, dv)])

    def last_step_addend(h, shared):
        if "base" not in shared:
            key = lax.broadcasted_iota(jnp.int32, (ATTN_KEYS, t), 0)
            qry = lax.broadcasted_iota(jnp.int32, (ATTN_KEYS, t), 1) + _last_step_offset(i, t)
            allowed = jnp.right_shift(key, CHUNK_SHIFT) <= jnp.right_shift(qry, CHUNK_SHIFT)
            shared["base"] = jnp.where(allowed, jnp.maximum(key - qry, 0).astype(F32), -NEG)
        return (-2.0 * slopes2[h]) * shared["base"]

    streams = []
    for h in range(hb):
        slope_pieces = [p.astype(F32) for p in _split3(jnp.full((1, LANES), slopes2[h], F32))]
        factors = [p * float(CHUNK) for p in slope_pieces] + slope_pieces
        qa, qb = _pack_queries(q_ref[:, h * LANES:(h + 1) * LANES], factors)
        vt = lambda k0, size, h=h: vt_ref[h, :, pl.ds(k0, size)]
        for m, (q, k_sc) in enumerate(((qa, ka_ref), (qb, kb_ref))):
            stats = (m_ref.at[2 * h + m], acc_ref.at[2 * h + m])
            streams.append((q, k_sc.at[h], vt, h, stats))
    def finish_head(h):
        lp = lam_ref[...]
        lam = (jnp.exp(jnp.sum(lp[0:1] * lp[1:2], axis=-1, keepdims=True))
               - jnp.exp(jnp.sum(lp[2:3] * lp[3:4], axis=-1, keepdims=True)) + lambda_init)
        pa = acc_ref[2 * h]
        pb = acc_ref[2 * h + 1]
        out = pa[0:dv] / pa[dv:dv + 1] - lam * (pb[0:dv] / pb[dv:dv + 1])
        ms = jnp.mean(out * out, axis=0, keepdims=True)
        out = out * lax.rsqrt(ms + EPS) * (sg_ref[...] * (1.0 - lambda_init))
        o_ref[:, h * LANES:(h + 1) * LANES] = out.T.astype(BF16)

    _attend(i, t, streams, (s0_ref, s1_ref), (mx0_ref, mx1_ref), last_step_addend, finish_head)


def _da_call(z, slopes, lam_p, subln_g, heads, lambda_init):
    b, s, _ = z.shape
    t = min(ATTN_TILE, s)
    assert s % ATTN_KEYS == 0 and t % CHUNK == 0
    assert s <= CHUNK * 256, "key position // CHUNK must stay exact in bf16"
    hb = min(ATTN_BLOCKS, heads)
    groups = heads // hb
    assert heads % hb == 0
    kern = functools.partial(_da_kernel, t=t, hb=hb, lambda_init=lambda_init)
    w = hb * LANES
    return pl.pallas_call(
        kern,
        grid_spec=pltpu.PrefetchScalarGridSpec(
            num_scalar_prefetch=1,
            grid=(b, groups, s // t),
            in_specs=[
                pl.BlockSpec((None, t, w), lambda bi, g, i, sl: (bi, i, g)),
                pl.BlockSpec((None, s, w), lambda bi, g, i, sl: (bi, 0, groups + g)),
                pl.BlockSpec((None, s, w), lambda bi, g, i, sl: (bi, 0, 2 * groups + g)),
                pl.BlockSpec((4, HEAD_DIM), lambda bi, g, i, sl: (0, 0)),
                pl.BlockSpec((LANES, 1), lambda bi, g, i, sl: (0, 0)),
            ],
            out_specs=pl.BlockSpec((None, t, w), lambda bi, g, i, sl: (bi, i, g)),
            scratch_shapes=[
                pltpu.VMEM((hb, s, LANES), BF16), pltpu.VMEM((hb, s, LANES), BF16),
                pltpu.VMEM((hb, LANES + ONES_ROWS, s), BF16),
                pltpu.VMEM((2 * hb, 1, t), F32),
                pltpu.VMEM((2 * hb, LANES + ONES_ROWS, t), F32),
                pltpu.VMEM((2 * hb, ATTN_KEYS, t), F32), pltpu.VMEM((2 * hb, ATTN_KEYS, t), F32),
                pltpu.VMEM((2 * hb, 1, t), F32), pltpu.VMEM((2 * hb, 1, t), F32),
            ],
        ),
        out_shape=jax.ShapeDtypeStruct((b, s, heads * LANES), BF16),
        compiler_params=_params("arbitrary", "arbitrary", "arbitrary"),
        name="diff_attention",
    )(slopes, z, z, z, lam_p, subln_g.reshape(LANES, 1))


def _fox_gate_kernel(g_ref, b_ref, o_ref, *, chunk):
    s = g_ref.shape[0]
    r = lax.broadcasted_iota(jnp.int32, (chunk, chunk), 0)
    c = lax.broadcasted_iota(jnp.int32, (chunk, chunk), 1)
    tri = (c <= r).astype(BF16)
    carry = jnp.zeros((1, LANES), F32)
    for c0 in range(0, s, chunk):
        ls = _log_sigmoid(g_ref[c0:c0 + chunk, :] + b_ref[...])
        cs = _exact_lhs_dot(tri, ls) + carry
        carry = cs[chunk - 1:chunk, :]
        o_ref[c0:c0 + chunk, :] = cs * (-LOG2E)


def _fox_gate_call(gates, bias_row):
    b, s, _ = gates.shape
    chunk = min(GATE_CHUNK, s)
    assert s % chunk == 0
    kern = functools.partial(_fox_gate_kernel, chunk=chunk)
    return pl.pallas_call(
        kern,
        grid=(b,),
        in_specs=[
            pl.BlockSpec((None, s, LANES), lambda i: (i, 0, 0)),
            pl.BlockSpec((1, LANES), lambda i: (0, 0)),
        ],
        out_specs=pl.BlockSpec((None, s, LANES), lambda i: (i, 0, 0)),
        out_shape=jax.ShapeDtypeStruct((b, s, LANES), F32),
        compiler_params=_params("arbitrary"),
        name="fox_gate_cumsum",
    )(gates, bias_row)


def _fox_kernel(q_ref, k_ref, v_ref, f_ref, o_ref,
                ka_ref, kb_ref, vt_ref, m_ref, acc_ref, s0_ref, s1_ref, mx0_ref, mx1_ref, *, t, hb):
    dv = HEAD_DIM
    g = pl.program_id(1)
    i = pl.program_id(2)

    @pl.when(i == 0)
    def _():
        n = BIAS_PIECES * LANES
        src = lax.broadcasted_iota(jnp.int32, (n, LANES), 0)
        dst = lax.broadcasted_iota(jnp.int32, (n, LANES), 1)
        piece = jnp.right_shift(src, LANES.bit_length() - 1)
        head = src - piece * LANES
        for h in range(hb):
            pair = g * hb + h
            place_a = ((head == 2 * pair) & (dst == HEAD_DIM + piece)).astype(BF16)
            place_b = ((head == 2 * pair + 1) & (dst == piece)).astype(BF16)
            place = jnp.concatenate([place_a, place_b], axis=-1)

            def extras(r0, rows, place=place):
                pieces = jnp.concatenate(_split3(f_ref[pl.ds(r0, rows), :]), axis=-1)
                both = _dot(pieces, place)
                return both[:, 0:LANES], both[:, LANES:2 * LANES]
            _pack_keys(_lane_block(k_ref, h), _lane_block(v_ref, h), extras,
                       ka_ref.at[h], kb_ref.at[h],
                       [(vt_ref.at[2 * h + m], m * dv, dv) for m in range(2)])

    def last_step_addend(_, shared):
        key = lax.broadcasted_iota(jnp.int32, (ATTN_KEYS, t), 0)
        qry = lax.broadcasted_iota(jnp.int32, (ATTN_KEYS, t), 1) + _last_step_offset(i, t)
        return jnp.where(key <= qry, 0.0, NEG)

    streams = []
    ones = [jnp.ones((1, LANES), F32)] * BIAS_PIECES
    for h in range(hb):
        qa, qb = _pack_queries(q_ref[:, h * LANES:(h + 1) * LANES], ones)
        for m, (q, k_sc) in enumerate(((qa, ka_ref), (qb, kb_ref))):
            vt = lambda k0, size, n=2 * h + m: vt_ref[n, :, pl.ds(k0, size)]
            stats = (m_ref.at[2 * h + m], acc_ref.at[2 * h + m])
            streams.append((q, k_sc.at[h], vt, 0, stats))
    def finish_pair(h):
        pa = acc_ref[2 * h]
        pb = acc_ref[2 * h + 1]
        out = jnp.concatenate([pa[0:dv] / pa[dv:dv + 1], pb[0:dv] / pb[dv:dv + 1]], axis=0)
        o_ref[:, h * LANES:(h + 1) * LANES] = out.T.astype(BF16)

    _attend(i, t, streams, (s0_ref, s1_ref), (mx0_ref, mx1_ref), last_step_addend, finish_pair)


def _fox_call(z, fneg, heads):
    b, s, _ = z.shape
    pairs = heads // 2
    t = min(ATTN_TILE, s)
    assert s % ATTN_KEYS == 0
    hb = min(ATTN_BLOCKS, pairs)
    groups = pairs // hb
    assert pairs % hb == 0
    kern = functools.partial(_fox_kernel, t=t, hb=hb)
    w = hb * LANES
    return pl.pallas_call(
        kern,
        grid=(b, groups, s // t),
        in_specs=[
            pl.BlockSpec((None, t, w), lambda bi, g, i: (bi, i, g)),
            pl.BlockSpec((None, s, w), lambda bi, g, i: (bi, 0, groups + g)),
            pl.BlockSpec((None, s, w), lambda bi, g, i: (bi, 0, 2 * groups + g)),
            pl.BlockSpec((None, s, LANES), lambda bi, g, i: (bi, 0, 0)),
        ],
        out_specs=pl.BlockSpec((None, t, w), lambda bi, g, i: (bi, i, g)),
        out_shape=jax.ShapeDtypeStruct((b, s, pairs * LANES), BF16),
        scratch_shapes=[
            pltpu.VMEM((hb, s, LANES), BF16), pltpu.VMEM((hb, s, LANES), BF16),
            pltpu.VMEM((2 * hb, HEAD_DIM + ONES_ROWS, s), BF16),
            pltpu.VMEM((2 * hb, 1, t), F32),
            pltpu.VMEM((2 * hb, HEAD_DIM + ONES_ROWS, t), F32),
            pltpu.VMEM((2 * hb, ATTN_KEYS, t), F32), pltpu.VMEM((2 * hb, ATTN_KEYS, t), F32),
            pltpu.VMEM((2 * hb, 1, t), F32), pltpu.VMEM((2 * hb, 1, t), F32),
        ],
        compiler_params=_params("arbitrary", "arbitrary", "arbitrary"),
        name="forgetting_attention",
    )(z, z, z, fneg)


def _mlstm_kernel(q_ref, k_ref, v_ref, o_ref, g_ref, gb_ref, cw_ref, cb_ref, y_ref,
                  halo_ref, ct_ref, m_ref, *, heads, taps, lc):
    @pl.when(pl.program_id(1) == 0)
    def _():
        halo_ref[...] = jnp.zeros(halo_ref.shape, F32)
        ct_ref[...] = jnp.zeros(ct_ref.shape, F32)
        m_ref[...] = jnp.zeros(m_ref.shape, F32)

    rows = lambda ref, sub: ref.at[pl.ds(sub * lc, lc), :]
    edge = halo_ref[...]
    for sub in range(q_ref.shape[0] // lc):
        prep, edge = _mlstm_prep(rows(q_ref, sub), rows(k_ref, sub), rows(g_ref, sub), gb_ref,
                                 cw_ref, cb_ref, edge, heads, taps)
        for hd in range(heads):
            _mlstm_head(prep, rows(v_ref, sub), rows(o_ref, sub), rows(y_ref, sub),
                        ct_ref, m_ref, hd, heads)
    halo_ref[...] = edge


def _mlstm_prep(q_ref, k_ref, g_ref, gb_ref, cw_ref, cb_ref, halo, heads, taps):
    lc, width = q_ref.shape
    pad = halo.shape[0]

    r = lax.broadcasted_iota(jnp.int32, (lc, lc), 0)
    c = lax.broadcasted_iota(jnp.int32, (lc, lc), 1)
    causal = c <= r

    gcol = g_ref[...] + gb_ref[...]
    grow = gcol.T[:2 * heads, :]
    a_cols = _exact_lhs_dot(causal.astype(BF16), _log_sigmoid(gcol))
    a_rows = _exact_rhs_dot(_log_sigmoid(grow), (r <= c).astype(BF16))

    b_half = grow[0:heads, :] - a_rows[heads:2 * heads, :]
    b_rows = jnp.concatenate([b_half, b_half], axis=0)
    prefix_cols = _prefix_max_lanes(b_rows).T

    xb = jnp.concatenate([q_ref[...], k_ref[...]], axis=-1)
    xf = xb.astype(F32)
    conv = cb_ref[...] + xf * cw_ref[taps - 1:taps, :]
    for sft in range(1, taps):
        shifted = _dot((r - c == sft).astype(BF16), xb)
        conv = conv + shifted * cw_ref[taps - 1 - sft:taps - sft, :]
    edge = jnp.concatenate([halo, xf[0:pad, :]], axis=0)
    first = cb_ref[...]
    for j in range(taps):
        off = pad - (taps - 1) + j
        first = first + edge[off:off + pad, :] * cw_ref[j:j + 1, :]
    conv = jnp.concatenate([first, conv[pad:, :]], axis=0)
    qk = conv * _sigmoid(conv)
    prep = dict(qk=qk, gcol=gcol, a_cols=a_cols, b_rows=b_rows, causal=causal,
                prefix_cols=prefix_cols)
    return prep, xf[lc - pad:, :]


def _mlstm_head(prep, v_ref, o_ref, y_ref, ct_ref, m_ref, hd, heads):
    lc, width = v_ref.shape
    d = width // heads
    qk, gcol, a_cols, causal = prep["qk"], prep["gcol"], prep["a_cols"], prep["causal"]
    q = qk[:, hd * d:(hd + 1) * d]
    k = qk[:, width + hd * d:width + (hd + 1) * d] * (d ** -0.5)
    v = v_ref[:, hd * d:(hd + 1) * d]
    qb = q.astype(BF16)
    kb = k.astype(BF16)
    ig_c = gcol[:, hd:hd + 1]
    a_c = a_cols[:, heads + hd:heads + hd + 1]
    a_last = a_c[lc - 1:lc, :]
    ct = ct_ref[hd]
    m_prev = m_ref[hd:hd + 1, 0:1]
    x_c = jnp.maximum(prep["prefix_cols"][:, hd:hd + 1], m_prev)

    g_c = a_last - a_c + ig_c
    m_loc = jnp.max(g_c, axis=0, keepdims=True)
    w_c = jnp.exp(g_c - m_loc)
    m_new = jnp.maximum(a_last + m_prev, m_loc)
    decay = jnp.exp(a_last + m_prev - m_new)
    s_loc = jnp.exp(m_loc - m_new)
    wv = jnp.concatenate([w_c * v.astype(F32), jnp.broadcast_to(w_c, (lc, d))], axis=-1)
    increment = _dot_tn(kb, wv.astype(BF16))
    raw = _dot_nt(qb, kb)
    from_state = _dot(qb, ct.astype(BF16))

    decay_mat = jnp.exp2(jnp.where(causal, prep["b_rows"][hd:hd + 1, :] * LOG2E - x_c * LOG2E, NEG))
    sqk = (raw * decay_mat).astype(BF16)
    intra = _dot(sqk, jnp.concatenate([v, jnp.ones((lc, d), BF16)], axis=-1))
    inter = jnp.exp(m_prev - x_c) * from_state
    num = inter[:, 0:d] + intra[:, 0:d]
    den = inter[:, d:d + 1] + intra[:, d:d + 1]
    hval = num / jnp.maximum(jnp.abs(den), jnp.exp(-(a_c + x_c)))
    og = o_ref[:, hd * d:(hd + 1) * d].astype(F32)
    y_ref[:, hd * d:(hd + 1) * d] = (_sigmoid(og) * hval).astype(BF16)

    ct_ref[hd] = decay * ct + s_loc * increment
    m_ref[hd:hd + 1, :] = jnp.broadcast_to(m_new, (1, LANES))


def _mlstm_call(z, gates, gate_bias, conv_w, conv_b, heads, col0):
    b, s, _ = z.shape
    width = conv_w.shape[1] // 2
    lc = min(ML_CHUNK, s)
    rows = min(ML_CHUNKS_PER_STEP * lc, s)
    assert s % rows == 0 and rows % lc == 0 and col0 % width == 0 and width // heads == LANES
    assert 2 * heads == 8, "gate rows are handled as one 8-sublane tile"
    cb = col0 // width
    taps = conv_w.shape[0]
    kern = functools.partial(_mlstm_kernel, heads=heads, taps=taps, lc=lc)
    zspec = lambda off: pl.BlockSpec((None, rows, width), lambda bi, ci: (bi, ci, cb + off))
    return pl.pallas_call(
        kern,
        grid=(b, s // rows),
        in_specs=[
            zspec(0), zspec(1), zspec(2), zspec(3),
            pl.BlockSpec((None, rows, LANES), lambda bi, ci: (bi, ci, 0)),
            pl.BlockSpec((1, LANES), lambda bi, ci: (0, 0)),
            pl.BlockSpec((taps, 2 * width), lambda bi, ci: (0, 0)),
            pl.BlockSpec((1, 2 * width), lambda bi, ci: (0, 0)),
        ],
        out_specs=pl.BlockSpec((None, rows, width), lambda bi, ci: (bi, ci, 0)),
        out_shape=jax.ShapeDtypeStruct((b, s, width), BF16),
        scratch_shapes=[
            pltpu.VMEM((8, 2 * width), F32),
            pltpu.VMEM((heads, LANES, 2 * LANES), F32),
            pltpu.VMEM((2 * heads, LANES), F32),
        ],
        compiler_params=_params("arbitrary", "arbitrary"),
        name="mlstm",
    )(z, z, z, z, gates, gate_bias, conv_w, conv_b)


def _mix_ffn_kernel(*refs, n_in, chunk):
    y_refs = refs[:n_in]
    wo_refs = refs[n_in:2 * n_in]
    (x_ref, gm_ref, sh_ref, sc_ref, g_ref, gate_ref, w1_ref, w3_ref, w2_ref, o_ref) = refs[2 * n_in:]
    mix = _dot(y_refs[0][...], wo_refs[0][...])
    for y_ref, w_ref in zip(y_refs[1:], wo_refs[1:]):
        mix = mix + _dot(y_ref[...], w_ref[...])
    x = x_ref[...] + gm_ref[...] * mix
    h = _norm_modulate(x, g_ref[...], sh_ref[...], sc_ref[...]).astype(BF16)
    hidden = w1_ref.shape[1]
    acc = None
    for c0 in range(0, hidden, chunk):
        a = _dot(h, w1_ref[:, c0:c0 + chunk])
        bb = _dot(h, w3_ref[:, c0:c0 + chunk])
        u = (a * _sigmoid(a) * bb).astype(BF16)
        part = _dot(u, w2_ref[c0:c0 + chunk, :])
        acc = part if acc is None else acc + part
    o_ref[...] = x + gate_ref[...] * acc


def _mix_ffn_call(ys, wo_all, wo_layer, x, gate_mix, shift, scale, g, gate, w1, w3, w2, layer, name):
    b, s, d = x.shape
    hidden = w1.shape[2]
    tm = min(ROW_TILE, s)
    assert s % tm == 0 and hidden % FFN_CHUNK == 0
    assert all(y.shape[2] == ys[0].shape[2] for y in ys)
    kern = functools.partial(_mix_ffn_kernel, n_in=len(ys), chunk=FFN_CHUNK)
    vec = pl.BlockSpec((None, 1, d), lambda i, j: (i, 0, 0))
    rows = lambda width: pl.BlockSpec((None, tm, width), lambda i, j: (i, j, 0))
    stacked = lambda w: pl.BlockSpec((None,) + w.shape[1:], lambda i, j: (layer, 0, 0))
    wo_specs = [pl.BlockSpec((None, y.shape[2], d), lambda i, j, n=n: (wo_layer, n, 0))
                for n, y in enumerate(ys)]
    return pl.pallas_call(
        kern,
        grid=(b, s // tm),
        in_specs=[rows(y.shape[2]) for y in ys] + wo_specs + [
            rows(d), vec, vec, vec,
            pl.BlockSpec((1, d), lambda i, j: (0, 0)),
            vec, stacked(w1), stacked(w3), stacked(w2),
        ],
        out_specs=rows(d),
        out_shape=jax.ShapeDtypeStruct((b, s, d), F32),
        compiler_params=_params("arbitrary", "arbitrary"),
        name=name,
    )(*ys, *([wo_all] * len(ys)), x, gate_mix, shift, scale, g, gate, w1, w3, w2)


def _pad_cols(w, n):
    return jnp.pad(w, ((0, 0), (0, n - w.shape[1])))


def _gain_row(q_g, k_g, width, total):
    reps = width // HEAD_DIM
    row = jnp.concatenate([jnp.tile(q_g.astype(F32) * (HEAD_DIM ** -0.5 * LOG2E), reps),
                           jnp.tile(k_g.astype(F32), reps),
                           jnp.ones((total - 2 * width,), F32)])
    return row.reshape(1, total)


def kernel(x, c, ada_w, ada_b, norm_mix_g, norm_ffn_g, ab_w_in, ml_b_i, ml_b_f, ml_conv_w, ml_conv_b,
           da_q_g, da_k_g, da_lambda, da_subln_g, ab_w_out, fx_w_in, fx_b_f, fx_q_g, fx_k_g, fx_w_out,
           ffn_w1, ffn_w3, ffn_w2):
    depth = ada_w.shape[0]
    b, s, d = x.shape
    ml_width = ml_conv_w.shape[2] // 2
    ml_heads = ml_b_i.shape[1]
    da_width = (ab_w_in.shape[2] - 4 * ml_width - 2 * ml_heads) // 3
    da_heads = da_width // (2 * HEAD_DIM)
    fx_heads = fx_b_f.shape[1]
    fx_width = fx_heads * HEAD_DIM
    ab_main = 3 * da_width + 4 * ml_width
    fx_main = 3 * fx_width

    mods = _ada_call(c, ada_w, ada_b).reshape(depth, b, 6, 1, d)
    slopes = 2.0 ** (-8.0 * jnp.arange(1, da_heads + 1, dtype=F32) / da_heads)

    ab_w_in_b, fx_w_in_b = ab_w_in.astype(BF16), fx_w_in.astype(BF16)
    ab_w_out_b, fx_w_out_b = ab_w_out.astype(BF16), fx_w_out.astype(BF16)
    ffn_w1_b, ffn_w3_b, ffn_w2_b = ffn_w1.astype(BF16), ffn_w3.astype(BF16), ffn_w2.astype(BF16)

    for l in range(depth):
        sh_m, sc_m, g_m, sh_f, sc_f, g_f = (mods[l, :, t] for t in range(6))
        j = l // 2
        g_mix = norm_mix_g[l].reshape(1, d)
        if l % 2 == 0:
            z, gates = _inproj_call(
                x, sh_m, sc_m, g_mix, ab_w_in_b, j, ab_main,
                _pad_cols(ab_w_in[j][:, ab_main:], LANES).astype(BF16),
                _gain_row(da_q_g[j], da_k_g[j], da_width, ab_main),
                2 * da_width, "inproj_even")
            lambda_init = 0.8 - 0.6 * math.exp(-0.3 * l)
            y_da = _da_call(z, slopes, da_lambda[j], da_subln_g[j], da_heads, lambda_init)
            gate_bias = _pad_cols(jnp.concatenate([ml_b_i[j], ml_b_f[j]]).reshape(1, -1), LANES)
            y_ml = _mlstm_call(z, gates, gate_bias, ml_conv_w[j], ml_conv_b[j].reshape(1, -1),
                               ml_heads, 3 * da_width)
            ys, wo_all, name = [y_da, y_ml], ab_w_out_b, "mix_ffn_even"
        else:
            z, gates = _inproj_call(
                x, sh_m, sc_m, g_mix, fx_w_in_b, j, fx_main,
                _pad_cols(fx_w_in[j][:, fx_main:], LANES).astype(BF16),
                _gain_row(fx_q_g[j], fx_k_g[j], fx_width, fx_main),
                2 * fx_width, "inproj_odd")
            fneg = _fox_gate_call(gates, _pad_cols(fx_b_f[j].reshape(1, -1), LANES))
            ys, wo_all, name = [_fox_call(z, fneg, fx_heads)], fx_w_out_b, "mix_ffn_odd"
        x = _mix_ffn_call(ys, wo_all, j, x, g_m, sh_f, sc_f, norm_ffn_g[l].reshape(1, d), g_f,
                          ffn_w1_b, ffn_w3_b, ffn_w2_b, l, name)
    return x
```

```python
import functools
import math

import jax
import jax.numpy as jnp
from jax import lax
from jax.experimental import pallas as pl
from jax.experimental.pallas import tpu as pltpu

F32 = jnp.float32
BF16 = jnp.bfloat16

EPS = 1e-6
LANES = 128
HEAD_DIM = 64
CHUNK = 64
CHUNK_SHIFT = CHUNK.bit_length() - 1
assert CHUNK == 1 << CHUNK_SHIFT
NEG = -1e30
LOG2E = math.log2(math.e)
BIAS_PIECES = 3
ONES_ROWS = 16

VMEM_LIMIT = 56 * 1024 * 1024

ADA_COLS = 1536
ROW_TILE = 512
INPROJ_CHUNK = 512
ATTN_TILE = 256
ATTN_KEYS = 512
ATTN_BLOCKS = 4
PACK_ROWS = 512
ML_CHUNK = 256
ML_CHUNKS_PER_STEP = 2
GATE_CHUNK = 256
FFN_CHUNK = 256


def _params(*sem):
    return pltpu.CompilerParams(dimension_semantics=sem, vmem_limit_bytes=VMEM_LIMIT)


def _log_sigmoid(x):
    return jnp.minimum(x, 0.0) - jnp.log1p(jnp.exp(-jnp.abs(x)))


def _sigmoid(x):
    return 0.5 * jnp.tanh(0.5 * x) + 0.5


def _silu(x):
    h = 0.5 * x
    return h * (jnp.tanh(h) + 1.0)


def _prefix_max_lanes(x):
    axis = x.ndim - 1
    lane = lax.broadcasted_iota(jnp.int32, x.shape, axis)
    shift = 1
    while shift < x.shape[axis]:
        x = jnp.maximum(x, jnp.where(lane >= shift, pltpu.roll(x, shift, axis=axis), NEG))
        shift *= 2
    return x


def _split3(x):
    x1 = x.astype(BF16)
    r1 = x - x1.astype(F32)
    x2 = r1.astype(BF16)
    x3 = (r1 - x2.astype(F32)).astype(BF16)
    return x1, x2, x3


def _dot(a, b):
    return jnp.dot(a, b, preferred_element_type=F32)


def _dot_nt(a, b):
    return lax.dot_general(a, b, (((1,), (1,)), ((), ())), preferred_element_type=F32)


def _dot_tn(a, b):
    return lax.dot_general(a, b, (((0,), (0,)), ((), ())), preferred_element_type=F32)


def _exact_lhs_dot(m_bf16, x):
    x1, x2, x3 = _split3(x)
    return _dot(m_bf16, x1) + _dot(m_bf16, x2) + _dot(m_bf16, x3)


def _exact_rhs_dot(x, m_bf16):
    x1, x2, x3 = _split3(x)
    return _dot(x1, m_bf16) + _dot(x2, m_bf16) + _dot(x3, m_bf16)


def _norm_modulate(x, g, sh, sc):
    ms = jnp.mean(x * x, axis=-1, keepdims=True)
    y = x * lax.rsqrt(ms + EPS) * g
    return y * (1.0 + sc) + sh


def _ada_kernel(c_ref, w_ref, b_ref, o_ref):
    c = c_ref[...]
    ca = _silu(c)
    c1, c2, c3 = _split3(ca)
    w = w_ref[...]
    w1 = w.astype(BF16)
    w2 = (w - w1.astype(F32)).astype(BF16)
    acc = _dot(c1, w1) + _dot(c1, w2) + _dot(c2, w1) + _dot(c2, w2) + _dot(c3, w1)
    o_ref[...] = acc + b_ref[...]


def _ada_call(c, ada_w, ada_b):
    depth, d, n = ada_w.shape
    b = c.shape[0]
    tn = min(ADA_COLS, n)
    assert n % tn == 0
    return pl.pallas_call(
        _ada_kernel,
        grid=(depth, n // tn),
        in_specs=[
            pl.BlockSpec((b, d), lambda l, j: (0, 0)),
            pl.BlockSpec((None, d, tn), lambda l, j: (l, 0, j)),
            pl.BlockSpec((None, 1, tn), lambda l, j: (l, 0, j)),
        ],
        out_specs=pl.BlockSpec((None, b, tn), lambda l, j: (l, 0, j)),
        out_shape=jax.ShapeDtypeStruct((depth, b, n), F32),
        compiler_params=_params("arbitrary", "arbitrary"),
        name="ada_mod",
    )(c, ada_w, ada_b.reshape(depth, 1, n))


def _inproj_kernel(x_ref, sh_ref, sc_ref, g_ref, w_ref, wg_ref, gain_ref, z_ref, gate_ref,
                   *, n_norm, chunk):
    half = x_ref.shape[0] // 2
    h_rows, first_rows = [], []
    for r0 in (0, half):
        h_part = _norm_modulate(x_ref[r0:r0 + half, :], g_ref[...], sh_ref[...], sc_ref[...]).astype(BF16)
        first_rows.append(_dot(h_part, w_ref[:, 0:chunk]))
        h_rows.append(h_part)
    h = jnp.concatenate(h_rows, axis=0)
    n = w_ref.shape[1]
    lane = lax.broadcasted_iota(jnp.int32, (1, LANES), 1)
    lo = lane < HEAD_DIM
    for c0 in range(0, n, chunk):
        zc = jnp.concatenate(first_rows, axis=0) if c0 == 0 else _dot(h, w_ref[:, c0:c0 + chunk])
        if c0 < n_norm:
            parts = []
            for s0 in range(0, chunk, LANES):
                zs = zc[:, s0:s0 + LANES]
                sq = zs * zs
                s_lo = jnp.sum(jnp.where(lo, sq, 0.0), axis=-1, keepdims=True)
                s_hi = jnp.sum(jnp.where(lo, 0.0, sq), axis=-1, keepdims=True)
                r_lo = lax.rsqrt(s_lo * (1.0 / HEAD_DIM) + EPS)
                r_hi = lax.rsqrt(s_hi * (1.0 / HEAD_DIM) + EPS)
                parts.append(zs * jnp.where(lo, r_lo, r_hi))
            zc = jnp.concatenate(parts, axis=-1) * gain_ref[:, c0:c0 + chunk]
        z_ref[:, c0:c0 + chunk] = zc.astype(BF16)
    gate_ref[...] = _dot(h, wg_ref[...])


def _inproj_call(x, shift, scale, g, w_all, layer, n, wg, gain, n_norm, name):
    b, s, d = x.shape
    tm = min(ROW_TILE, s)
    chunk = INPROJ_CHUNK
    assert s % tm == 0 and tm % 2 == 0 and n % chunk == 0 and n_norm % chunk == 0
    kern = functools.partial(_inproj_kernel, n_norm=n_norm, chunk=chunk)
    return pl.pallas_call(
        kern,
        grid=(b, s // tm),
        in_specs=[
            pl.BlockSpec((None, tm, d), lambda i, j: (i, j, 0)),
            pl.BlockSpec((None, 1, d), lambda i, j: (i, 0, 0)),
            pl.BlockSpec((None, 1, d), lambda i, j: (i, 0, 0)),
            pl.BlockSpec((1, d), lambda i, j: (0, 0)),
            pl.BlockSpec((None, d, n), lambda i, j: (layer, 0, 0)),
            pl.BlockSpec((d, LANES), lambda i, j: (0, 0)),
            pl.BlockSpec((1, n), lambda i, j: (0, 0)),
        ],
        out_specs=[
            pl.BlockSpec((None, tm, n), lambda i, j: (i, j, 0)),
            pl.BlockSpec((None, tm, LANES), lambda i, j: (i, j, 0)),
        ],
        out_shape=[
            jax.ShapeDtypeStruct((b, s, n), BF16),
            jax.ShapeDtypeStruct((b, s, LANES), F32),
        ],
        compiler_params=_params("arbitrary", "arbitrary"),
        name=name,
    )(x, shift, scale, g, w_all, wg, gain)


def _lane_ids():
    return lax.broadcasted_iota(jnp.int32, (1, LANES), 1)


def _place_pieces(pieces, first_lane):
    lane = _lane_ids()
    out = jnp.zeros(pieces[0].shape, F32)
    for j, p in enumerate(pieces):
        out = jnp.where(lane == first_lane + j, p.astype(F32), out)
    return out


def _pack_keys(k_ref, v_ref, extras_fn, ka_ref, kb_ref, vt_refs):
    s = k_ref.shape[0]
    rows = min(PACK_ROWS, s)
    keep_lo = jnp.where(_lane_ids() < HEAD_DIM, 1.0, 0.0).astype(BF16)
    keep_hi = jnp.where(_lane_ids() < HEAD_DIM, 0.0, 1.0).astype(BF16)

    def body(c, carry):
        r0 = pl.multiple_of(c * rows, rows)
        k2 = k_ref[pl.ds(r0, rows), :]
        ea, eb = extras_fn(r0, rows)
        ka_ref[pl.ds(r0, rows), :] = k2 * keep_lo + ea.astype(BF16)
        kb_ref[pl.ds(r0, rows), :] = k2 * keep_hi + eb.astype(BF16)
        vt = v_ref[pl.ds(r0, rows), :].T
        for ref, c0, width in vt_refs:
            ref[0:width, pl.ds(r0, rows)] = vt[c0:c0 + width, :]
            ref[width:width + ONES_ROWS, pl.ds(r0, rows)] = jnp.ones((ONES_ROWS, rows), BF16)
        return carry

    steps = s // rows
    lax.fori_loop(0, steps, body, 0, unroll=2 if steps % 2 == 0 else 1)


def _pack_queries(q2, factors):
    lane = _lane_ids()
    q2 = q2.astype(F32)
    lo = lane < HEAD_DIM
    fac_a = _place_pieces(factors, HEAD_DIM)
    fac_b = _place_pieces(factors, 0)
    return jnp.where(lo, q2, fac_a).astype(BF16), jnp.where(lo, fac_b, q2).astype(BF16)


def _init_stats(m_ref, acc_ref):
    m_ref[...] = jnp.full(m_ref.shape, NEG, F32)
    acc_ref[...] = jnp.zeros(acc_ref.shape, F32)


def _softmax_step(s, tile_max, vt, m_ref, acc_ref):
    m_old = m_ref[...]
    m_new = jnp.maximum(m_old, tile_max)
    alpha = jnp.exp2(m_old - m_new)
    p = jnp.exp2(s - m_new)
    acc_ref[...] = alpha * acc_ref[...] + _dot(vt, p.astype(BF16))
    m_ref[...] = m_new


def _attend(i, t, streams, s_buf, mx_buf, make_addend, finish_pair):
    tk = ATTN_KEYS
    n = (i * t + t + tk - 1) // tk

    def qk_one(step, slot, idx, rows=tk):
        k0 = pl.multiple_of(step * tk, tk)
        q, k_ref = streams[idx][0], streams[idx][1]
        s = _dot_nt(k_ref[pl.ds(k0, rows), :], q)
        s_buf[slot][idx, 0:rows, :] = s
        if rows == tk:
            mx_buf[slot][idx] = jnp.max(s, axis=0, keepdims=True)

    def softmax_one(step, slot, idx):
        k0 = pl.multiple_of(step * tk, tk)
        _, _, vt, _, stats = streams[idx]
        _softmax_step(s_buf[slot][idx], mx_buf[slot][idx], vt(k0, tk), *stats)

    def last_stage(step, slot, rows, offset):
        k0 = pl.multiple_of(step * tk, tk)
        addends = {}
        for idx, (_, _, vt, add_key, stats) in enumerate(streams):
            if add_key not in addends:
                addends[add_key] = make_addend(add_key, addends.setdefault("shared", {}), rows, offset)
            s = s_buf[slot][idx, 0:rows, :] + addends[add_key]
            _softmax_step(s, jnp.max(s, axis=0, keepdims=True), vt(k0, rows), *stats)
            if idx % 2 == 1:
                finish_pair(idx // 2)

    def overlapped(qk_step, qk_slot, sm_step, sm_slot, qk_rows=tk):
        for idx in range(len(streams)):
            qk_one(qk_step, qk_slot, idx, qk_rows)
            softmax_one(sm_step, sm_slot, idx)

    for idx in range(len(streams)):
        qk_one(0, 0, idx)
        _init_stats(*streams[idx][4])
    pairs = (n - 1) // 2

    def body(p, carry):
        overlapped(2 * p + 1, 1, 2 * p, 0)
        overlapped(2 * p + 2, 0, 2 * p + 1, 1)
        return carry

    lax.fori_loop(0, pairs, body, 0)

    for offset in range(0, tk, t):
        rows = offset + t
        here = i * t - (n - 1) * tk == offset

        @pl.when(here & (n - 1 > 2 * pairs))
        def _():
            overlapped(n - 1, 1, n - 2, 0, rows)
            last_stage(n - 1, 1, rows, offset)

        @pl.when(here & (n - 1 == 2 * pairs))
        def _():
            last_stage(n - 1, 0, rows, offset)


def _lane_block(ref, h):
    return ref.at[:, pl.ds(h * LANES, LANES)]


def _da_kernel(slope_ref, q_ref, k_ref, v_ref, lam_ref, sg_ref, o_ref,
               ka_ref, kb_ref, vt_ref, m_ref, acc_ref, s0_ref, s1_ref, mx0_ref, mx1_ref,
               *, t, hb, lambda_init):
    dv = LANES
    g = pl.program_id(1)
    i = pl.program_id(2)
    slopes2 = [slope_ref[g * hb + h] * LOG2E for h in range(hb)]

    @pl.when(i == 0)
    def _():
        def extras(r0, rows):
            pos = lax.broadcasted_iota(jnp.int32, (rows, LANES), 0) + r0
            hi = jnp.right_shift(pos, CHUNK_SHIFT).astype(F32)
            lo = jnp.bitwise_and(pos, CHUNK - 1).astype(F32)
            lane = _lane_ids()

            def place(first):
                in_hi = (lane >= first) & (lane < first + BIAS_PIECES)
                in_lo = (lane >= first + BIAS_PIECES) & (lane < first + 2 * BIAS_PIECES)
                return jnp.where(in_hi, hi, jnp.where(in_lo, lo, 0.0))
            return place(HEAD_DIM), place(0)
        for h in range(hb):
            _pack_keys(_lane_block(k_ref, h), _lane_block(v_ref, h), extras,
                       ka_ref.at[h], kb_ref.at[h], [(vt_ref.at[h], 0, dv)])

    def last_step_addend(h, shared, rows, offset):
        if "base" not in shared:
            key = lax.broadcasted_iota(jnp.int32, (rows, t), 0)
            qry = lax.broadcasted_iota(jnp.int32, (rows, t), 1) + offset
            allowed = jnp.right_shift(key, CHUNK_SHIFT) <= jnp.right_shift(qry, CHUNK_SHIFT)
            shared["base"] = jnp.where(allowed, jnp.maximum(key - qry, 0).astype(F32), -NEG)
        return (-2.0 * slopes2[h]) * shared["base"]

    streams = []
    for h in range(hb):
        slope_pieces = [p.astype(F32) for p in _split3(jnp.full((1, LANES), slopes2[h], F32))]
        factors = [p * float(CHUNK) for p in slope_pieces] + slope_pieces
        qa, qb = _pack_queries(q_ref[:, h * LANES:(h + 1) * LANES], factors)
        vt = lambda k0, size, h=h: vt_ref[h, :, pl.ds(k0, size)]
        for m, (q, k_sc) in enumerate(((qa, ka_ref), (qb, kb_ref))):
            stats = (m_ref.at[2 * h + m], acc_ref.at[2 * h + m])
            streams.append((q, k_sc.at[h], vt, h, stats))
    def finish_head(h):
        lp = lam_ref[...]
        lam = (jnp.exp(jnp.sum(lp[0:1] * lp[1:2], axis=-1, keepdims=True))
               - jnp.exp(jnp.sum(lp[2:3] * lp[3:4], axis=-1, keepdims=True)) + lambda_init)
        pa = acc_ref[2 * h]
        pb = acc_ref[2 * h + 1]
        out = pa[0:dv] / pa[dv:dv + 1] - lam * (pb[0:dv] / pb[dv:dv + 1])
        ms = jnp.mean(out * out, axis=0, keepdims=True)
        out = out * lax.rsqrt(ms + EPS) * (sg_ref[...] * (1.0 - lambda_init))
        o_ref[:, h * LANES:(h + 1) * LANES] = out.T.astype(BF16)

    _attend(i, t, streams, (s0_ref, s1_ref), (mx0_ref, mx1_ref), last_step_addend, finish_head)


def _da_call(z, slopes, lam_p, subln_g, heads, lambda_init):
    b, s, _ = z.shape
    t = min(ATTN_TILE, s)
    assert s % ATTN_KEYS == 0 and t % CHUNK == 0
    assert s <= CHUNK * 256, "key position // CHUNK must stay exact in bf16"
    hb = min(ATTN_BLOCKS, heads)
    groups = heads // hb
    assert heads % hb == 0
    kern = functools.partial(_da_kernel, t=t, hb=hb, lambda_init=lambda_init)
    w = hb * LANES
    return pl.pallas_call(
        kern,
        grid_spec=pltpu.PrefetchScalarGridSpec(
            num_scalar_prefetch=1,
            grid=(b, groups, s // t),
            in_specs=[
                pl.BlockSpec((None, t, w), lambda bi, g, i, sl: (bi, i, g)),
                pl.BlockSpec((None, s, w), lambda bi, g, i, sl: (bi, 0, groups + g)),
                pl.BlockSpec((None, s, w), lambda bi, g, i, sl: (bi, 0, 2 * groups + g)),
                pl.BlockSpec((4, HEAD_DIM), lambda bi, g, i, sl: (0, 0)),
                pl.BlockSpec((LANES, 1), lambda bi, g, i, sl: (0, 0)),
            ],
            out_specs=pl.BlockSpec((None, t, w), lambda bi, g, i, sl: (bi, i, g)),
            scratch_shapes=[
                pltpu.VMEM((hb, s, LANES), BF16), pltpu.VMEM((hb, s, LANES), BF16),
                pltpu.VMEM((hb, LANES + ONES_ROWS, s), BF16),
                pltpu.VMEM((2 * hb, 1, t), F32),
                pltpu.VMEM((2 * hb, LANES + ONES_ROWS, t), F32),
                pltpu.VMEM((2 * hb, ATTN_KEYS, t), F32), pltpu.VMEM((2 * hb, ATTN_KEYS, t), F32),
                pltpu.VMEM((2 * hb, 1, t), F32), pltpu.VMEM((2 * hb, 1, t), F32),
            ],
        ),
        out_shape=jax.ShapeDtypeStruct((b, s, heads * LANES), BF16),
        compiler_params=_params("arbitrary", "arbitrary", "arbitrary"),
        name="diff_attention",
    )(slopes, z, z, z, lam_p, subln_g.reshape(LANES, 1))


def _fox_gate_kernel(g_ref, b_ref, o_ref, *, chunk):
    s = g_ref.shape[0]
    r = lax.broadcasted_iota(jnp.int32, (chunk, chunk), 0)
    c = lax.broadcasted_iota(jnp.int32, (chunk, chunk), 1)
    tri = (c <= r).astype(BF16)
    carry = jnp.zeros((1, LANES), F32)
    for c0 in range(0, s, chunk):
        ls = _log_sigmoid(g_ref[c0:c0 + chunk, :] + b_ref[...])
        cs = _exact_lhs_dot(tri, ls) + carry
        carry = cs[chunk - 1:chunk, :]
        o_ref[c0:c0 + chunk, :] = cs * (-LOG2E)


def _fox_gate_call(gates, bias_row):
    b, s, _ = gates.shape
    chunk = min(GATE_CHUNK, s)
    assert s % chunk == 0
    kern = functools.partial(_fox_gate_kernel, chunk=chunk)
    return pl.pallas_call(
        kern,
        grid=(b,),
        in_specs=[
            pl.BlockSpec((None, s, LANES), lambda i: (i, 0, 0)),
            pl.BlockSpec((1, LANES), lambda i: (0, 0)),
        ],
        out_specs=pl.BlockSpec((None, s, LANES), lambda i: (i, 0, 0)),
        out_shape=jax.ShapeDtypeStruct((b, s, LANES), F32),
        compiler_params=_params("arbitrary"),
        name="fox_gate_cumsum",
    )(gates, bias_row)


def _fox_kernel(q_ref, k_ref, v_ref, f_ref, o_ref,
                ka_ref, kb_ref, vt_ref, m_ref, acc_ref, s0_ref, s1_ref, mx0_ref, mx1_ref, *, t, hb):
    dv = HEAD_DIM
    g = pl.program_id(1)
    i = pl.program_id(2)

    @pl.when(i == 0)
    def _():
        n = BIAS_PIECES * LANES
        src = lax.broadcasted_iota(jnp.int32, (n, LANES), 0)
        dst = lax.broadcasted_iota(jnp.int32, (n, LANES), 1)
        piece = jnp.right_shift(src, LANES.bit_length() - 1)
        head = src - piece * LANES
        for h in range(hb):
            pair = g * hb + h
            place_a = ((head == 2 * pair) & (dst == HEAD_DIM + piece)).astype(BF16)
            place_b = ((head == 2 * pair + 1) & (dst == piece)).astype(BF16)
            place = jnp.concatenate([place_a, place_b], axis=-1)

            def extras(r0, rows, place=place):
                pieces = jnp.concatenate(_split3(f_ref[pl.ds(r0, rows), :]), axis=-1)
                both = _dot(pieces, place)
                return both[:, 0:LANES], both[:, LANES:2 * LANES]
            _pack_keys(_lane_block(k_ref, h), _lane_block(v_ref, h), extras,
                       ka_ref.at[h], kb_ref.at[h],
                       [(vt_ref.at[2 * h + m], m * dv, dv) for m in range(2)])

    def last_step_addend(_, shared, rows, offset):
        key = lax.broadcasted_iota(jnp.int32, (rows, t), 0)
        qry = lax.broadcasted_iota(jnp.int32, (rows, t), 1) + offset
        return jnp.where(key <= qry, 0.0, NEG)

    streams = []
    ones = [jnp.ones((1, LANES), F32)] * BIAS_PIECES
    for h in range(hb):
        qa, qb = _pack_queries(q_ref[:, h * LANES:(h + 1) * LANES], ones)
        for m, (q, k_sc) in enumerate(((qa, ka_ref), (qb, kb_ref))):
            vt = lambda k0, size, n=2 * h + m: vt_ref[n, :, pl.ds(k0, size)]
            stats = (m_ref.at[2 * h + m], acc_ref.at[2 * h + m])
            streams.append((q, k_sc.at[h], vt, 0, stats))
    def finish_pair(h):
        pa = acc_ref[2 * h]
        pb = acc_ref[2 * h + 1]
        out = jnp.concatenate([pa[0:dv] / pa[dv:dv + 1], pb[0:dv] / pb[dv:dv + 1]], axis=0)
        o_ref[:, h * LANES:(h + 1) * LANES] = out.T.astype(BF16)

    _attend(i, t, streams, (s0_ref, s1_ref), (mx0_ref, mx1_ref), last_step_addend, finish_pair)


def _fox_call(z, fneg, heads):
    b, s, _ = z.shape
    pairs = heads // 2
    t = min(ATTN_TILE, s)
    assert s % ATTN_KEYS == 0
    hb = min(ATTN_BLOCKS, pairs)
    groups = pairs // hb
    assert pairs % hb == 0
    kern = functools.partial(_fox_kernel, t=t, hb=hb)
    w = hb * LANES
    return pl.pallas_call(
        kern,
        grid=(b, groups, s // t),
        in_specs=[
            pl.BlockSpec((None, t, w), lambda bi, g, i: (bi, i, g)),
            pl.BlockSpec((None, s, w), lambda bi, g, i: (bi, 0, groups + g)),
            pl.BlockSpec((None, s, w), lambda bi, g, i: (bi, 0, 2 * groups + g)),
            pl.BlockSpec((None, s, LANES), lambda bi, g, i: (bi, 0, 0)),
        ],
        out_specs=pl.BlockSpec((None, t, w), lambda bi, g, i: (bi, i, g)),
        out_shape=jax.ShapeDtypeStruct((b, s, pairs * LANES), BF16),
        scratch_shapes=[
            pltpu.VMEM((hb, s, LANES), BF16), pltpu.VMEM((hb, s, LANES), BF16),
            pltpu.VMEM((2 * hb, HEAD_DIM + ONES_ROWS, s), BF16),
            pltpu.VMEM((2 * hb, 1, t), F32),
            pltpu.VMEM((2 * hb, HEAD_DIM + ONES_ROWS, t), F32),
            pltpu.VMEM((2 * hb, ATTN_KEYS, t), F32), pltpu.VMEM((2 * hb, ATTN_KEYS, t), F32),
            pltpu.VMEM((2 * hb, 1, t), F32), pltpu.VMEM((2 * hb, 1, t), F32),
        ],
        compiler_params=_params("arbitrary", "arbitrary", "arbitrary"),
        name="forgetting_attention",
    )(z, z, z, fneg)


def _mlstm_kernel(q_ref, k_ref, v_ref, o_ref, g_ref, gb_ref, cw_ref, cb_ref, y_ref,
                  halo_ref, ct_ref, m_ref, *, heads, taps, lc):
    @pl.when(pl.program_id(1) == 0)
    def _():
        halo_ref[...] = jnp.zeros(halo_ref.shape, F32)
        ct_ref[...] = jnp.zeros(ct_ref.shape, F32)
        m_ref[...] = jnp.zeros(m_ref.shape, F32)

    rows = lambda ref, sub: ref.at[pl.ds(sub * lc, lc), :]
    edge = halo_ref[...]
    for sub in range(q_ref.shape[0] // lc):
        prep, edge = _mlstm_prep(rows(q_ref, sub), rows(k_ref, sub), rows(g_ref, sub), gb_ref,
                                 cw_ref, cb_ref, edge, heads, taps)
        for hd in range(heads):
            _mlstm_head(prep, rows(v_ref, sub), rows(o_ref, sub), rows(y_ref, sub),
                        ct_ref, m_ref, hd, heads)
    halo_ref[...] = edge


def _mlstm_prep(q_ref, k_ref, g_ref, gb_ref, cw_ref, cb_ref, halo, heads, taps):
    lc, width = q_ref.shape
    pad = halo.shape[0]

    r = lax.broadcasted_iota(jnp.int32, (lc, lc), 0)
    c = lax.broadcasted_iota(jnp.int32, (lc, lc), 1)
    causal = c <= r

    gcol = g_ref[...] + gb_ref[...]
    grow = gcol.T[:2 * heads, :]
    a_cols = _exact_lhs_dot(causal.astype(BF16), _log_sigmoid(gcol))
    a_rows = _exact_rhs_dot(_log_sigmoid(grow), (r <= c).astype(BF16))

    b_half = grow[0:heads, :] - a_rows[heads:2 * heads, :]
    b_rows = jnp.concatenate([b_half, b_half], axis=0)
    prefix_cols = _prefix_max_lanes(b_rows).T

    xb = jnp.concatenate([q_ref[...], k_ref[...]], axis=-1)
    xf = xb.astype(F32)
    conv = cb_ref[...] + xf * cw_ref[taps - 1:taps, :]
    for sft in range(1, taps):
        shifted = _dot((r - c == sft).astype(BF16), xb)
        conv = conv + shifted * cw_ref[taps - 1 - sft:taps - sft, :]
    edge = jnp.concatenate([halo, xf[0:pad, :]], axis=0)
    first = cb_ref[...]
    for j in range(taps):
        off = pad - (taps - 1) + j
        first = first + edge[off:off + pad, :] * cw_ref[j:j + 1, :]
    conv = jnp.concatenate([first, conv[pad:, :]], axis=0)
    qk = _silu(conv)
    prep = dict(qk=qk, gcol=gcol, a_cols=a_cols, b_rows=b_rows, causal=causal,
                prefix_cols=prefix_cols)
    return prep, xf[lc - pad:, :]


def _mlstm_head(prep, v_ref, o_ref, y_ref, ct_ref, m_ref, hd, heads):
    lc, width = v_ref.shape
    d = width // heads
    qk, gcol, a_cols, causal = prep["qk"], prep["gcol"], prep["a_cols"], prep["causal"]
    q = qk[:, hd * d:(hd + 1) * d]
    k = qk[:, width + hd * d:width + (hd + 1) * d] * (d ** -0.5)
    v = v_ref[:, hd * d:(hd + 1) * d]
    qb = q.astype(BF16)
    kb = k.astype(BF16)
    ig_c = gcol[:, hd:hd + 1]
    a_c = a_cols[:, heads + hd:heads + hd + 1]
    a_last = a_c[lc - 1:lc, :]
    ct = ct_ref[hd]
    m_prev = m_ref[hd:hd + 1, 0:1]
    x_c = jnp.maximum(prep["prefix_cols"][:, hd:hd + 1], m_prev)

    g_c = a_last - a_c + ig_c
    m_loc = jnp.max(g_c, axis=0, keepdims=True)
    w_c = jnp.exp(g_c - m_loc)
    m_new = jnp.maximum(a_last + m_prev, m_loc)
    decay = jnp.exp(a_last + m_prev - m_new)
    s_loc = jnp.exp(m_loc - m_new)
    wv = jnp.concatenate([w_c * v.astype(F32), jnp.broadcast_to(w_c, (lc, d))], axis=-1)
    increment = _dot_tn(kb, wv.astype(BF16))
    raw = _dot_nt(qb, kb)
    from_state = _dot(qb, ct.astype(BF16))

    decay_mat = jnp.exp2(jnp.where(causal, prep["b_rows"][hd:hd + 1, :] * LOG2E - x_c * LOG2E, NEG))
    sqk = (raw * decay_mat).astype(BF16)
    intra = _dot(sqk, jnp.concatenate([v, jnp.ones((lc, d), BF16)], axis=-1))
    inter = jnp.exp(m_prev - x_c) * from_state
    num = inter[:, 0:d] + intra[:, 0:d]
    den = inter[:, d:d + 1] + intra[:, d:d + 1]
    hval = num / jnp.maximum(jnp.abs(den), jnp.exp(-(a_c + x_c)))
    og = o_ref[:, hd * d:(hd + 1) * d].astype(F32)
    y_ref[:, hd * d:(hd + 1) * d] = (_sigmoid(og) * hval).astype(BF16)

    ct_ref[hd] = decay * ct + s_loc * increment
    m_ref[hd:hd + 1, :] = jnp.broadcast_to(m_new, (1, LANES))


def _mlstm_call(z, gates, gate_bias, conv_w, conv_b, heads, col0):
    b, s, _ = z.shape
    width = conv_w.shape[1] // 2
    lc = min(ML_CHUNK, s)
    rows = min(ML_CHUNKS_PER_STEP * lc, s)
    assert s % rows == 0 and rows % lc == 0 and col0 % width == 0 and width // heads == LANES
    assert 2 * heads == 8, "gate rows are handled as one 8-sublane tile"
    cb = col0 // width
    taps = conv_w.shape[0]
    kern = functools.partial(_mlstm_kernel, heads=heads, taps=taps, lc=lc)
    zspec = lambda off: pl.BlockSpec((None, rows, width), lambda bi, ci: (bi, ci, cb + off))
    return pl.pallas_call(
        kern,
        grid=(b, s // rows),
        in_specs=[
            zspec(0), zspec(1), zspec(2), zspec(3),
            pl.BlockSpec((None, rows, LANES), lambda bi, ci: (bi, ci, 0)),
            pl.BlockSpec((1, LANES), lambda bi, ci: (0, 0)),
            pl.BlockSpec((taps, 2 * width), lambda bi, ci: (0, 0)),
            pl.BlockSpec((1, 2 * width), lambda bi, ci: (0, 0)),
        ],
        out_specs=pl.BlockSpec((None, rows, width), lambda bi, ci: (bi, ci, 0)),
        out_shape=jax.ShapeDtypeStruct((b, s, width), BF16),
        scratch_shapes=[
            pltpu.VMEM((8, 2 * width), F32),
            pltpu.VMEM((heads, LANES, 2 * LANES), F32),
            pltpu.VMEM((2 * heads, LANES), F32),
        ],
        compiler_params=_params("arbitrary", "arbitrary"),
        name="mlstm",
    )(z, z, z, z, gates, gate_bias, conv_w, conv_b)


def _mix_ffn_kernel(*refs, n_in, chunk):
    y_refs = refs[:n_in]
    wo_refs = refs[n_in:2 * n_in]
    (x_ref, gm_ref, sh_ref, sc_ref, g_ref, gate_ref, w1_ref, w3_ref, w2_ref, o_ref) = refs[2 * n_in:]
    mix = _dot(y_refs[0][...], wo_refs[0][...])
    for y_ref, w_ref in zip(y_refs[1:], wo_refs[1:]):
        mix = mix + _dot(y_ref[...], w_ref[...])
    x = x_ref[...] + gm_ref[...] * mix
    h = _norm_modulate(x, g_ref[...], sh_ref[...], sc_ref[...]).astype(BF16)
    hidden = w1_ref.shape[1]
    acc = None
    for c0 in range(0, hidden, chunk):
        a = _dot(h, w1_ref[:, c0:c0 + chunk])
        bb = _dot(h, w3_ref[:, c0:c0 + chunk])
        u = (_silu(a) * bb).astype(BF16)
        part = _dot(u, w2_ref[c0:c0 + chunk, :])
        acc = part if acc is None else acc + part
    o_ref[...] = x + gate_ref[...] * acc


def _mix_ffn_call(ys, wo_all, wo_layer, x, gate_mix, shift, scale, g, gate, w1, w3, w2, layer, name):
    b, s, d = x.shape
    hidden = w1.shape[2]
    tm = min(ROW_TILE, s)
    assert s % tm == 0 and hidden % FFN_CHUNK == 0
    assert all(y.shape[2] == ys[0].shape[2] for y in ys)
    kern = functools.partial(_mix_ffn_kernel, n_in=len(ys), chunk=FFN_CHUNK)
    vec = pl.BlockSpec((None, 1, d), lambda i, j: (i, 0, 0))
    rows = lambda width: pl.BlockSpec((None, tm, width), lambda i, j: (i, j, 0))
    stacked = lambda w: pl.BlockSpec((None,) + w.shape[1:], lambda i, j: (layer, 0, 0))
    wo_specs = [pl.BlockSpec((None, y.shape[2], d), lambda i, j, n=n: (wo_layer, n, 0))
                for n, y in enumerate(ys)]
    return pl.pallas_call(
        kern,
        grid=(b, s // tm),
        in_specs=[rows(y.shape[2]) for y in ys] + wo_specs + [
            rows(d), vec, vec, vec,
            pl.BlockSpec((1, d), lambda i, j: (0, 0)),
            vec, stacked(w1), stacked(w3), stacked(w2),
        ],
        out_specs=rows(d),
        out_shape=jax.ShapeDtypeStruct((b, s, d), F32),
        compiler_params=_params("arbitrary", "arbitrary"),
        name=name,
    )(*ys, *([wo_all] * len(ys)), x, gate_mix, shift, scale, g, gate, w1, w3, w2)


def _pad_cols(w, n):
    return jnp.pad(w, ((0, 0), (0, n - w.shape[1])))


def _gain_row(q_g, k_g, width, total):
    reps = width // HEAD_DIM
    row = jnp.concatenate([jnp.tile(q_g.astype(F32) * (HEAD_DIM ** -0.5 * LOG2E), reps),
                           jnp.tile(k_g.astype(F32), reps),
                           jnp.ones((total - 2 * width,), F32)])
    return row.reshape(1, total)


def kernel(x, c, ada_w, ada_b, norm_mix_g, norm_ffn_g, ab_w_in, ml_b_i, ml_b_f, ml_conv_w, ml_conv_b,
           da_q_g, da_k_g, da_lambda, da_subln_g, ab_w_out, fx_w_in, fx_b_f, fx_q_g, fx_k_g, fx_w_out,
           ffn_w1, ffn_w3, ffn_w2):
    depth = ada_w.shape[0]
    b, s, d = x.shape
    ml_width = ml_conv_w.shape[2] // 2
    ml_heads = ml_b_i.shape[1]
    da_width = (ab_w_in.shape[2] - 4 * ml_width - 2 * ml_heads) // 3
    da_heads = da_width // (2 * HEAD_DIM)
    fx_heads = fx_b_f.shape[1]
    fx_width = fx_heads * HEAD_DIM
    ab_main = 3 * da_width + 4 * ml_width
    fx_main = 3 * fx_width

    mods = _ada_call(c, ada_w, ada_b).reshape(depth, b, 6, 1, d)
    slopes = 2.0 ** (-8.0 * jnp.arange(1, da_heads + 1, dtype=F32) / da_heads)

    ab_w_in_b, fx_w_in_b = ab_w_in.astype(BF16), fx_w_in.astype(BF16)
    ab_w_out_b, fx_w_out_b = ab_w_out.astype(BF16), fx_w_out.astype(BF16)
    ffn_w1_b, ffn_w3_b, ffn_w2_b = ffn_w1.astype(BF16), ffn_w3.astype(BF16), ffn_w2.astype(BF16)

    for l in range(depth):
        sh_m, sc_m, g_m, sh_f, sc_f, g_f = (mods[l, :, t] for t in range(6))
        j = l // 2
        g_mix = norm_mix_g[l].reshape(1, d)
        if l % 2 == 0:
            z, gates = _inproj_call(
                x, sh_m, sc_m, g_mix, ab_w_in_b, j, ab_main,
                _pad_cols(ab_w_in[j][:, ab_main:], LANES).astype(BF16),
                _gain_row(da_q_g[j], da_k_g[j], da_width, ab_main),
                2 * da_width, "inproj_even")
            lambda_init = 0.8 - 0.6 * math.exp(-0.3 * l)
            y_da = _da_call(z, slopes, da_lambda[j], da_subln_g[j], da_heads, lambda_init)
            gate_bias = _pad_cols(jnp.concatenate([ml_b_i[j], ml_b_f[j]]).reshape(1, -1), LANES)
            y_ml = _mlstm_call(z, gates, gate_bias, ml_conv_w[j], ml_conv_b[j].reshape(1, -1),
                               ml_heads, 3 * da_width)
            ys, wo_all, name = [y_da, y_ml], ab_w_out_b, "mix_ffn_even"
        else:
            z, gates = _inproj_call(
                x, sh_m, sc_m, g_mix, fx_w_in_b, j, fx_main,
                _pad_cols(fx_w_in[j][:, fx_main:], LANES).astype(BF16),
                _gain_row(fx_q_g[j], fx_k_g[j], fx_width, fx_main),
                2 * fx_width, "inproj_odd")
            fneg = _fox_gate_call(gates, _pad_cols(fx_b_f[j].reshape(1, -1), LANES))
            ys, wo_all, name = [_fox_call(z, fneg, fx_heads)], fx_w_out_b, "mix_ffn_odd"
        x = _mix_ffn_call(ys, wo_all, j, x, g_m, sh_f, sc_f, norm_ffn_g[l].reshape(1, d), g_f,
                          ffn_w1_b, ffn_w3_b, ffn_w2_b, l, name)
    return x
```

```python
import functools
import math

import jax
import jax.numpy as jnp
from jax import lax
from jax.experimental import pallas as pl
from jax.experimental.pallas import tpu as pltpu

F32 = jnp.float32
BF16 = jnp.bfloat16

EPS = 1e-6
LANES = 128
HEAD_DIM = 64
CHUNK = 64
CHUNK_SHIFT = CHUNK.bit_length() - 1
assert CHUNK == 1 << CHUNK_SHIFT
NEG = -1e30
LOG2E = math.log2(math.e)
BIAS_PIECES = 3
ONES_ROWS = 16

VMEM_LIMIT = 56 * 1024 * 1024

ADA_COLS = 1536
ROW_TILE = 512
INPROJ_CHUNK = 512
ATTN_TILE = 256
ATTN_KEYS = 512
ATTN_BLOCKS = 4
PACK_ROWS = 512
ML_CHUNK = 256
ML_CHUNKS_PER_STEP = 2
GATE_CHUNK = 256
FFN_CHUNK = 256


def _params(*sem):
    return pltpu.CompilerParams(dimension_semantics=sem, vmem_limit_bytes=VMEM_LIMIT)


def _log_sigmoid(x):
    return jnp.minimum(x, 0.0) - jnp.log1p(jnp.exp(-jnp.abs(x)))


def _sigmoid(x):
    return 0.5 * jnp.tanh(0.5 * x) + 0.5


def _silu(x):
    h = 0.5 * x
    return h * (jnp.tanh(h) + 1.0)


def _prefix_max_lanes(x):
    axis = x.ndim - 1
    lane = lax.broadcasted_iota(jnp.int32, x.shape, axis)
    shift = 1
    while shift < x.shape[axis]:
        x = jnp.maximum(x, jnp.where(lane >= shift, pltpu.roll(x, shift, axis=axis), NEG))
        shift *= 2
    return x


def _split3(x):
    x1 = x.astype(BF16)
    r1 = x - x1.astype(F32)
    x2 = r1.astype(BF16)
    x3 = (r1 - x2.astype(F32)).astype(BF16)
    return x1, x2, x3


def _dot(a, b):
    return jnp.dot(a, b, preferred_element_type=F32)


def _dot_nt(a, b):
    return lax.dot_general(a, b, (((1,), (1,)), ((), ())), preferred_element_type=F32)


def _dot_tn(a, b):
    return lax.dot_general(a, b, (((0,), (0,)), ((), ())), preferred_element_type=F32)


def _exact_lhs_dot(m_bf16, x):
    x1, x2, x3 = _split3(x)
    return _dot(m_bf16, x1) + _dot(m_bf16, x2) + _dot(m_bf16, x3)


def _exact_rhs_dot(x, m_bf16):
    x1, x2, x3 = _split3(x)
    return _dot(x1, m_bf16) + _dot(x2, m_bf16) + _dot(x3, m_bf16)


def _norm_modulate(x, g, sh, sc):
    ms = jnp.mean(x * x, axis=-1, keepdims=True)
    y = x * lax.rsqrt(ms + EPS) * g
    return y * (1.0 + sc) + sh


def _ada_kernel(c_ref, w_ref, b_ref, o_ref):
    c = c_ref[...]
    ca = _silu(c)
    c1, c2, c3 = _split3(ca)
    w = w_ref[...]
    w1 = w.astype(BF16)
    w2 = (w - w1.astype(F32)).astype(BF16)
    acc = _dot(c1, w1) + _dot(c1, w2) + _dot(c2, w1) + _dot(c2, w2) + _dot(c3, w1)
    o_ref[...] = acc + b_ref[...]


def _ada_call(c, ada_w, ada_b):
    depth, d, n = ada_w.shape
    b = c.shape[0]
    tn = min(ADA_COLS, n)
    assert n % tn == 0
    return pl.pallas_call(
        _ada_kernel,
        grid=(depth, n // tn),
        in_specs=[
            pl.BlockSpec((b, d), lambda l, j: (0, 0)),
            pl.BlockSpec((None, d, tn), lambda l, j: (l, 0, j)),
            pl.BlockSpec((None, 1, tn), lambda l, j: (l, 0, j)),
        ],
        out_specs=pl.BlockSpec((None, b, tn), lambda l, j: (l, 0, j)),
        out_shape=jax.ShapeDtypeStruct((depth, b, n), F32),
        compiler_params=_params("arbitrary", "arbitrary"),
        name="ada_mod",
    )(c, ada_w, ada_b.reshape(depth, 1, n))


def _inproj_kernel(x_ref, sh_ref, sc_ref, g_ref, w_ref, wg_ref, gain_ref, z_ref, gate_ref,
                   *, n_norm, chunk):
    half = x_ref.shape[0] // 2
    h_rows, first_rows = [], []
    for r0 in (0, half):
        h_part = _norm_modulate(x_ref[r0:r0 + half, :], g_ref[...], sh_ref[...], sc_ref[...]).astype(BF16)
        first_rows.append(_dot(h_part, w_ref[:, 0:chunk]))
        h_rows.append(h_part)
    h = jnp.concatenate(h_rows, axis=0)
    n = w_ref.shape[1]
    lane = lax.broadcasted_iota(jnp.int32, (1, LANES), 1)
    lo = lane < HEAD_DIM
    for c0 in range(0, n, chunk):
        zc = jnp.concatenate(first_rows, axis=0) if c0 == 0 else _dot(h, w_ref[:, c0:c0 + chunk])
        if c0 < n_norm:
            parts = []
            for s0 in range(0, chunk, LANES):
                zs = zc[:, s0:s0 + LANES]
                sq = zs * zs
                s_lo = jnp.sum(jnp.where(lo, sq, 0.0), axis=-1, keepdims=True)
                s_hi = jnp.sum(jnp.where(lo, 0.0, sq), axis=-1, keepdims=True)
                r_lo = lax.rsqrt(s_lo * (1.0 / HEAD_DIM) + EPS)
                r_hi = lax.rsqrt(s_hi * (1.0 / HEAD_DIM) + EPS)
                parts.append(zs * jnp.where(lo, r_lo, r_hi))
            zc = jnp.concatenate(parts, axis=-1) * gain_ref[:, c0:c0 + chunk]
        z_ref[:, c0:c0 + chunk] = zc.astype(BF16)
    gate_ref[...] = _dot(h, wg_ref[...])


def _inproj_call(x, shift, scale, g, w_all, layer, n, wg, gain, n_norm, name):
    b, s, d = x.shape
    tm = min(ROW_TILE, s)
    chunk = INPROJ_CHUNK
    assert s % tm == 0 and tm % 2 == 0 and n % chunk == 0 and n_norm % chunk == 0
    kern = functools.partial(_inproj_kernel, n_norm=n_norm, chunk=chunk)
    return pl.pallas_call(
        kern,
        grid=(b, s // tm),
        in_specs=[
            pl.BlockSpec((None, tm, d), lambda i, j: (i, j, 0)),
            pl.BlockSpec((None, 1, d), lambda i, j: (i, 0, 0)),
            pl.BlockSpec((None, 1, d), lambda i, j: (i, 0, 0)),
            pl.BlockSpec((1, d), lambda i, j: (0, 0)),
            pl.BlockSpec((None, d, n), lambda i, j: (layer, 0, 0)),
            pl.BlockSpec((d, LANES), lambda i, j: (0, 0)),
            pl.BlockSpec((1, n), lambda i, j: (0, 0)),
        ],
        out_specs=[
            pl.BlockSpec((None, tm, n), lambda i, j: (i, j, 0)),
            pl.BlockSpec((None, tm, LANES), lambda i, j: (i, j, 0)),
        ],
        out_shape=[
            jax.ShapeDtypeStruct((b, s, n), BF16),
            jax.ShapeDtypeStruct((b, s, LANES), F32),
        ],
        compiler_params=_params("arbitrary", "arbitrary"),
        name=name,
    )(x, shift, scale, g, w_all, wg, gain)


def _lane_ids():
    return lax.broadcasted_iota(jnp.int32, (1, LANES), 1)


def _place_pieces(pieces, first_lane):
    lane = _lane_ids()
    out = jnp.zeros(pieces[0].shape, F32)
    for j, p in enumerate(pieces):
        out = jnp.where(lane == first_lane + j, p.astype(F32), out)
    return out


def _pack_keys(blocks, shared_fn):
    s = blocks[0][0].shape[0]
    rows = min(PACK_ROWS, s)
    keep_lo = jnp.where(_lane_ids() < HEAD_DIM, 1.0, 0.0).astype(BF16)
    keep_hi = jnp.where(_lane_ids() < HEAD_DIM, 0.0, 1.0).astype(BF16)

    def body(c, carry):
        r0 = pl.multiple_of(c * rows, rows)
        shared = shared_fn(r0, rows)
        for k_ref, v_ref, ka_ref, kb_ref, vt_refs, extras_fn in blocks:
            k2 = k_ref[pl.ds(r0, rows), :]
            ea, eb = extras_fn(shared)
            ka_ref[pl.ds(r0, rows), :] = k2 * keep_lo + ea.astype(BF16)
            kb_ref[pl.ds(r0, rows), :] = k2 * keep_hi + eb.astype(BF16)
            vt = v_ref[pl.ds(r0, rows), :].T
            for ref, c0, width in vt_refs:
                ref[0:width, pl.ds(r0, rows)] = vt[c0:c0 + width, :]
                ref[width:width + ONES_ROWS, pl.ds(r0, rows)] = jnp.ones((ONES_ROWS, rows), BF16)
        return carry

    lax.fori_loop(0, s // rows, body, 0)


def _pack_queries(q2, factors):
    lane = _lane_ids()
    q2 = q2.astype(F32)
    lo = lane < HEAD_DIM
    fac_a = _place_pieces(factors, HEAD_DIM)
    fac_b = _place_pieces(factors, 0)
    return jnp.where(lo, q2, fac_a).astype(BF16), jnp.where(lo, fac_b, q2).astype(BF16)


def _init_stats(m_ref, acc_ref):
    m_ref[...] = jnp.full(m_ref.shape, NEG, F32)
    acc_ref[...] = jnp.zeros(acc_ref.shape, F32)


def _softmax_step(s, tile_max, vt, m_ref, acc_ref):
    m_old = m_ref[...]
    m_new = jnp.maximum(m_old, tile_max)
    alpha = jnp.exp2(m_old - m_new)
    p = jnp.exp2(s - m_new)
    acc_ref[...] = alpha * acc_ref[...] + _dot(vt, p.astype(BF16))
    m_ref[...] = m_new


def _attend(i, t, streams, s_buf, mx_buf, make_addend, finish_pair):
    tk = ATTN_KEYS
    n = (i * t + t + tk - 1) // tk

    def qk_one(step, slot, idx, rows=tk):
        k0 = pl.multiple_of(step * tk, tk)
        q, k_ref = streams[idx][0], streams[idx][1]
        s = _dot_nt(k_ref[pl.ds(k0, rows), :], q)
        s_buf[slot][idx, 0:rows, :] = s
        if rows == tk:
            mx_buf[slot][idx] = jnp.max(s, axis=0, keepdims=True)

    def softmax_one(step, slot, idx):
        k0 = pl.multiple_of(step * tk, tk)
        _, _, vt, _, stats = streams[idx]
        _softmax_step(s_buf[slot][idx], mx_buf[slot][idx], vt(k0, tk), *stats)

    def last_stage(step, slot, rows, offset):
        k0 = pl.multiple_of(step * tk, tk)
        addends = {}
        for idx, (_, _, vt, add_key, stats) in enumerate(streams):
            if add_key not in addends:
                addends[add_key] = make_addend(add_key, addends.setdefault("shared", {}), rows, offset)
            s = s_buf[slot][idx, 0:rows, :] + addends[add_key]
            _softmax_step(s, jnp.max(s, axis=0, keepdims=True), vt(k0, rows), *stats)
            if idx % 2 == 1:
                finish_pair(idx // 2)

    def overlapped(qk_step, qk_slot, sm_step, sm_slot, qk_rows=tk):
        for idx in range(len(streams)):
            qk_one(qk_step, qk_slot, idx, qk_rows)
            softmax_one(sm_step, sm_slot, idx)

    for idx in range(len(streams)):
        qk_one(0, 0, idx)
        _init_stats(*streams[idx][4])
    pairs = (n - 1) // 2

    def body(p, carry):
        overlapped(2 * p + 1, 1, 2 * p, 0)
        overlapped(2 * p + 2, 0, 2 * p + 1, 1)
        return carry

    lax.fori_loop(0, pairs, body, 0)

    for offset in range(0, tk, t):
        rows = offset + t
        here = i * t - (n - 1) * tk == offset

        @pl.when(here & (n - 1 > 2 * pairs))
        def _():
            overlapped(n - 1, 1, n - 2, 0, rows)
            last_stage(n - 1, 1, rows, offset)

        @pl.when(here & (n - 1 == 2 * pairs))
        def _():
            last_stage(n - 1, 0, rows, offset)


def _lane_block(ref, h):
    return ref.at[:, pl.ds(h * LANES, LANES)]


def _da_kernel(slope_ref, q_ref, k_ref, v_ref, lam_ref, sg_ref, o_ref,
               ka_ref, kb_ref, vt_ref, m_ref, acc_ref, s0_ref, s1_ref, mx0_ref, mx1_ref,
               *, t, hb, lambda_init):
    dv = LANES
    g = pl.program_id(1)
    i = pl.program_id(2)
    slopes2 = [slope_ref[g * hb + h] * LOG2E for h in range(hb)]

    @pl.when(i == 0)
    def _():
        def position_pieces(r0, rows):
            pos = lax.broadcasted_iota(jnp.int32, (rows, LANES), 0) + r0
            hi = jnp.right_shift(pos, CHUNK_SHIFT).astype(F32)
            lo = jnp.bitwise_and(pos, CHUNK - 1).astype(F32)
            lane = _lane_ids()

            def place(first):
                in_hi = (lane >= first) & (lane < first + BIAS_PIECES)
                in_lo = (lane >= first + BIAS_PIECES) & (lane < first + 2 * BIAS_PIECES)
                return jnp.where(in_hi, hi, jnp.where(in_lo, lo, 0.0)).astype(BF16)
            return place(HEAD_DIM), place(0)
        _pack_keys([(_lane_block(k_ref, h), _lane_block(v_ref, h), ka_ref.at[h], kb_ref.at[h],
                     [(vt_ref.at[h], 0, dv)], lambda shared: shared) for h in range(hb)],
                   position_pieces)

    def last_step_addend(h, shared, rows, offset):
        if "base" not in shared:
            key = lax.broadcasted_iota(jnp.int32, (rows, t), 0)
            qry = lax.broadcasted_iota(jnp.int32, (rows, t), 1) + offset
            allowed = jnp.right_shift(key, CHUNK_SHIFT) <= jnp.right_shift(qry, CHUNK_SHIFT)
            shared["base"] = jnp.where(allowed, jnp.maximum(key - qry, 0).astype(F32), -NEG)
        return (-2.0 * slopes2[h]) * shared["base"]

    streams = []
    for h in range(hb):
        slope_pieces = [p.astype(F32) for p in _split3(jnp.full((1, LANES), slopes2[h], F32))]
        factors = [p * float(CHUNK) for p in slope_pieces] + slope_pieces
        qa, qb = _pack_queries(q_ref[:, h * LANES:(h + 1) * LANES], factors)
        vt = lambda k0, size, h=h: vt_ref[h, :, pl.ds(k0, size)]
        for m, (q, k_sc) in enumerate(((qa, ka_ref), (qb, kb_ref))):
            stats = (m_ref.at[2 * h + m], acc_ref.at[2 * h + m])
            streams.append((q, k_sc.at[h], vt, h, stats))
    def finish_head(h):
        lp = lam_ref[...]
        lam = (jnp.exp(jnp.sum(lp[0:1] * lp[1:2], axis=-1, keepdims=True))
               - jnp.exp(jnp.sum(lp[2:3] * lp[3:4], axis=-1, keepdims=True)) + lambda_init)
        pa = acc_ref[2 * h]
        pb = acc_ref[2 * h + 1]
        out = pa[0:dv] / pa[dv:dv + 1] - lam * (pb[0:dv] / pb[dv:dv + 1])
        ms = jnp.mean(out * out, axis=0, keepdims=True)
        out = out * lax.rsqrt(ms + EPS) * (sg_ref[...] * (1.0 - lambda_init))
        o_ref[:, h * LANES:(h + 1) * LANES] = out.T.astype(BF16)

    _attend(i, t, streams, (s0_ref, s1_ref), (mx0_ref, mx1_ref), last_step_addend, finish_head)


def _da_call(z, slopes, lam_p, subln_g, heads, lambda_init):
    b, s, _ = z.shape
    t = min(ATTN_TILE, s)
    assert s % ATTN_KEYS == 0 and t % CHUNK == 0
    assert s <= CHUNK * 256, "key position // CHUNK must stay exact in bf16"
    hb = min(ATTN_BLOCKS, heads)
    groups = heads // hb
    assert heads % hb == 0
    kern = functools.partial(_da_kernel, t=t, hb=hb, lambda_init=lambda_init)
    w = hb * LANES
    return pl.pallas_call(
        kern,
        grid_spec=pltpu.PrefetchScalarGridSpec(
            num_scalar_prefetch=1,
            grid=(b, groups, s // t),
            in_specs=[
                pl.BlockSpec((None, t, w), lambda bi, g, i, sl: (bi, i, g)),
                pl.BlockSpec((None, s, w), lambda bi, g, i, sl: (bi, 0, groups + g)),
                pl.BlockSpec((None, s, w), lambda bi, g, i, sl: (bi, 0, 2 * groups + g)),
                pl.BlockSpec((4, HEAD_DIM), lambda bi, g, i, sl: (0, 0)),
                pl.BlockSpec((LANES, 1), lambda bi, g, i, sl: (0, 0)),
            ],
            out_specs=pl.BlockSpec((None, t, w), lambda bi, g, i, sl: (bi, i, g)),
            scratch_shapes=[
                pltpu.VMEM((hb, s, LANES), BF16), pltpu.VMEM((hb, s, LANES), BF16),
                pltpu.VMEM((hb, LANES + ONES_ROWS, s), BF16),
                pltpu.VMEM((2 * hb, 1, t), F32),
                pltpu.VMEM((2 * hb, LANES + ONES_ROWS, t), F32),
                pltpu.VMEM((2 * hb, ATTN_KEYS, t), F32), pltpu.VMEM((2 * hb, ATTN_KEYS, t), F32),
                pltpu.VMEM((2 * hb, 1, t), F32), pltpu.VMEM((2 * hb, 1, t), F32),
            ],
        ),
        out_shape=jax.ShapeDtypeStruct((b, s, heads * LANES), BF16),
        compiler_params=_params("arbitrary", "arbitrary", "arbitrary"),
        name="diff_attention",
    )(slopes, z, z, z, lam_p, subln_g.reshape(LANES, 1))


def _fox_gate_kernel(g_ref, b_ref, o_ref, *, chunk):
    s = g_ref.shape[0]
    r = lax.broadcasted_iota(jnp.int32, (chunk, chunk), 0)
    c = lax.broadcasted_iota(jnp.int32, (chunk, chunk), 1)
    tri = (c <= r).astype(BF16)
    carry = jnp.zeros((1, LANES), F32)
    for c0 in range(0, s, chunk):
        ls = _log_sigmoid(g_ref[c0:c0 + chunk, :] + b_ref[...])
        cs = _exact_lhs_dot(tri, ls) + carry
        carry = cs[chunk - 1:chunk, :]
        o_ref[c0:c0 + chunk, :] = cs * (-LOG2E)


def _fox_gate_call(gates, bias_row):
    b, s, _ = gates.shape
    chunk = min(GATE_CHUNK, s)
    assert s % chunk == 0
    kern = functools.partial(_fox_gate_kernel, chunk=chunk)
    return pl.pallas_call(
        kern,
        grid=(b,),
        in_specs=[
            pl.BlockSpec((None, s, LANES), lambda i: (i, 0, 0)),
            pl.BlockSpec((1, LANES), lambda i: (0, 0)),
        ],
        out_specs=pl.BlockSpec((None, s, LANES), lambda i: (i, 0, 0)),
        out_shape=jax.ShapeDtypeStruct((b, s, LANES), F32),
        compiler_params=_params("arbitrary"),
        name="fox_gate_cumsum",
    )(gates, bias_row)


def _fox_kernel(q_ref, k_ref, v_ref, f_ref, o_ref,
                ka_ref, kb_ref, vt_ref, m_ref, acc_ref, s0_ref, s1_ref, mx0_ref, mx1_ref, *, t, hb):
    dv = HEAD_DIM
    g = pl.program_id(1)
    i = pl.program_id(2)

    @pl.when(i == 0)
    def _():
        n = BIAS_PIECES * LANES
        src = lax.broadcasted_iota(jnp.int32, (n, LANES), 0)
        dst = lax.broadcasted_iota(jnp.int32, (n, LANES), 1)
        piece = jnp.right_shift(src, LANES.bit_length() - 1)
        head = src - piece * LANES
        blocks = []
        for h in range(hb):
            pair = g * hb + h
            place_a = ((head == 2 * pair) & (dst == HEAD_DIM + piece)).astype(BF16)
            place_b = ((head == 2 * pair + 1) & (dst == piece)).astype(BF16)
            place = jnp.concatenate([place_a, place_b], axis=-1)

            def extras(pieces, place=place):
                both = _dot(pieces, place)
                return both[:, 0:LANES], both[:, LANES:2 * LANES]
            blocks.append((_lane_block(k_ref, h), _lane_block(v_ref, h), ka_ref.at[h], kb_ref.at[h],
                           [(vt_ref.at[2 * h + m], m * dv, dv) for m in range(2)], extras))
        _pack_keys(blocks,
                   lambda r0, rows: jnp.concatenate(_split3(f_ref[pl.ds(r0, rows), :]), axis=-1))

    def last_step_addend(_, shared, rows, offset):
        key = lax.broadcasted_iota(jnp.int32, (rows, t), 0)
        qry = lax.broadcasted_iota(jnp.int32, (rows, t), 1) + offset
        return jnp.where(key <= qry, 0.0, NEG)

    streams = []
    ones = [jnp.ones((1, LANES), F32)] * BIAS_PIECES
    for h in range(hb):
        qa, qb = _pack_queries(q_ref[:, h * LANES:(h + 1) * LANES], ones)
        for m, (q, k_sc) in enumerate(((qa, ka_ref), (qb, kb_ref))):
            vt = lambda k0, size, n=2 * h + m: vt_ref[n, :, pl.ds(k0, size)]
            stats = (m_ref.at[2 * h + m], acc_ref.at[2 * h + m])
            streams.append((q, k_sc.at[h], vt, 0, stats))
    def finish_pair(h):
        pa = acc_ref[2 * h]
        pb = acc_ref[2 * h + 1]
        out = jnp.concatenate([pa[0:dv] / pa[dv:dv + 1], pb[0:dv] / pb[dv:dv + 1]], axis=0)
        o_ref[:, h * LANES:(h + 1) * LANES] = out.T.astype(BF16)

    _attend(i, t, streams, (s0_ref, s1_ref), (mx0_ref, mx1_ref), last_step_addend, finish_pair)


def _fox_call(z, fneg, heads):
    b, s, _ = z.shape
    pairs = heads // 2
    t = min(ATTN_TILE, s)
    assert s % ATTN_KEYS == 0
    hb = min(ATTN_BLOCKS, pairs)
    groups = pairs // hb
    assert pairs % hb == 0
    kern = functools.partial(_fox_kernel, t=t, hb=hb)
    w = hb * LANES
    return pl.pallas_call(
        kern,
        grid=(b, groups, s // t),
        in_specs=[
            pl.BlockSpec((None, t, w), lambda bi, g, i: (bi, i, g)),
            pl.BlockSpec((None, s, w), lambda bi, g, i: (bi, 0, groups + g)),
            pl.BlockSpec((None, s, w), lambda bi, g, i: (bi, 0, 2 * groups + g)),
            pl.BlockSpec((None, s, LANES), lambda bi, g, i: (bi, 0, 0)),
        ],
        out_specs=pl.BlockSpec((None, t, w), lambda bi, g, i: (bi, i, g)),
        out_shape=jax.ShapeDtypeStruct((b, s, pairs * LANES), BF16),
        scratch_shapes=[
            pltpu.VMEM((hb, s, LANES), BF16), pltpu.VMEM((hb, s, LANES), BF16),
            pltpu.VMEM((2 * hb, HEAD_DIM + ONES_ROWS, s), BF16),
            pltpu.VMEM((2 * hb, 1, t), F32),
            pltpu.VMEM((2 * hb, HEAD_DIM + ONES_ROWS, t), F32),
            pltpu.VMEM((2 * hb, ATTN_KEYS, t), F32), pltpu.VMEM((2 * hb, ATTN_KEYS, t), F32),
            pltpu.VMEM((2 * hb, 1, t), F32), pltpu.VMEM((2 * hb, 1, t), F32),
        ],
        compiler_params=_params("arbitrary", "arbitrary", "arbitrary"),
        name="forgetting_attention",
    )(z, z, z, fneg)


def _mlstm_kernel(q_ref, k_ref, v_ref, o_ref, g_ref, gb_ref, cw_ref, cb_ref, y_ref,
                  halo_ref, ct_ref, m_ref, *, heads, taps, lc):
    @pl.when(pl.program_id(1) == 0)
    def _():
        halo_ref[...] = jnp.zeros(halo_ref.shape, F32)
        ct_ref[...] = jnp.zeros(ct_ref.shape, F32)
        m_ref[...] = jnp.zeros(m_ref.shape, F32)

    rows = lambda ref, sub: ref.at[pl.ds(sub * lc, lc), :]
    edge = halo_ref[...]
    for sub in range(q_ref.shape[0] // lc):
        prep, edge = _mlstm_prep(rows(q_ref, sub), rows(k_ref, sub), rows(g_ref, sub), gb_ref,
                                 cw_ref, cb_ref, edge, heads, taps)
        for hd in range(heads):
            _mlstm_head(prep, rows(v_ref, sub), rows(o_ref, sub), rows(y_ref, sub),
                        ct_ref, m_ref, hd, heads)
    halo_ref[...] = edge


def _mlstm_prep(q_ref, k_ref, g_ref, gb_ref, cw_ref, cb_ref, halo, heads, taps):
    lc, width = q_ref.shape
    pad = halo.shape[0]

    r = lax.broadcasted_iota(jnp.int32, (lc, lc), 0)
    c = lax.broadcasted_iota(jnp.int32, (lc, lc), 1)
    causal = c <= r

    gcol = g_ref[...] + gb_ref[...]
    grow = gcol.T[:2 * heads, :]
    a_cols = _exact_lhs_dot(causal.astype(BF16), _log_sigmoid(gcol))
    a_rows = _exact_rhs_dot(_log_sigmoid(grow), (r <= c).astype(BF16))

    b_half = grow[0:heads, :] - a_rows[heads:2 * heads, :]
    b_rows = jnp.concatenate([b_half, b_half], axis=0)
    prefix_cols = _prefix_max_lanes(b_rows).T

    xb = jnp.concatenate([q_ref[...], k_ref[...]], axis=-1)
    xf = xb.astype(F32)
    conv = cb_ref[...] + xf * cw_ref[taps - 1:taps, :]
    for sft in range(1, taps):
        shifted = _dot((r - c == sft).astype(BF16), xb)
        conv = conv + shifted * cw_ref[taps - 1 - sft:taps - sft, :]
    edge = jnp.concatenate([halo, xf[0:pad, :]], axis=0)
    first = cb_ref[...]
    for j in range(taps):
        off = pad - (taps - 1) + j
        first = first + edge[off:off + pad, :] * cw_ref[j:j + 1, :]
    conv = jnp.concatenate([first, conv[pad:, :]], axis=0)
    qk = _silu(conv)
    prep = dict(qk=qk, gcol=gcol, a_cols=a_cols, b_rows=b_rows, causal=causal,
                prefix_cols=prefix_cols)
    return prep, xf[lc - pad:, :]


def _mlstm_head(prep, v_ref, o_ref, y_ref, ct_ref, m_ref, hd, heads):
    lc, width = v_ref.shape
    d = width // heads
    qk, gcol, a_cols, causal = prep["qk"], prep["gcol"], prep["a_cols"], prep["causal"]
    q = qk[:, hd * d:(hd + 1) * d]
    k = qk[:, width + hd * d:width + (hd + 1) * d] * (d ** -0.5)
    v = v_ref[:, hd * d:(hd + 1) * d]
    qb = q.astype(BF16)
    kb = k.astype(BF16)
    ig_c = gcol[:, hd:hd + 1]
    a_c = a_cols[:, heads + hd:heads + hd + 1]
    a_last = a_c[lc - 1:lc, :]
    ct = ct_ref[hd]
    m_prev = m_ref[hd:hd + 1, 0:1]
    x_c = jnp.maximum(prep["prefix_cols"][:, hd:hd + 1], m_prev)

    g_c = a_last - a_c + ig_c
    m_loc = jnp.max(g_c, axis=0, keepdims=True)
    w_c = jnp.exp(g_c - m_loc)
    m_new = jnp.maximum(a_last + m_prev, m_loc)
    decay = jnp.exp(a_last + m_prev - m_new)
    s_loc = jnp.exp(m_loc - m_new)
    wv = jnp.concatenate([w_c * v.astype(F32), jnp.broadcast_to(w_c, (lc, d))], axis=-1)
    increment = _dot_tn(kb, wv.astype(BF16))
    raw = _dot_nt(qb, kb)
    from_state = _dot(qb, ct.astype(BF16))

    decay_mat = jnp.exp2(jnp.where(causal, prep["b_rows"][hd:hd + 1, :] * LOG2E - x_c * LOG2E, NEG))
    sqk = (raw * decay_mat).astype(BF16)
    intra = _dot(sqk, jnp.concatenate([v, jnp.ones((lc, d), BF16)], axis=-1))
    inter = jnp.exp(m_prev - x_c) * from_state
    num = inter[:, 0:d] + intra[:, 0:d]
    den = inter[:, d:d + 1] + intra[:, d:d + 1]
    hval = num / jnp.maximum(jnp.abs(den), jnp.exp(-(a_c + x_c)))
    og = o_ref[:, hd * d:(hd + 1) * d].astype(F32)
    y_ref[:, hd * d:(hd + 1) * d] = (_sigmoid(og) * hval).astype(BF16)

    ct_ref[hd] = decay * ct + s_loc * increment
    m_ref[hd:hd + 1, :] = jnp.broadcast_to(m_new, (1, LANES))


def _mlstm_call(z, gates, gate_bias, conv_w, conv_b, heads, col0):
    b, s, _ = z.shape
    width = conv_w.shape[1] // 2
    lc = min(ML_CHUNK, s)
    rows = min(ML_CHUNKS_PER_STEP * lc, s)
    assert s % rows == 0 and rows % lc == 0 and col0 % width == 0 and width // heads == LANES
    assert 2 * heads == 8, "gate rows are handled as one 8-sublane tile"
    cb = col0 // width
    taps = conv_w.shape[0]
    kern = functools.partial(_mlstm_kernel, heads=heads, taps=taps, lc=lc)
    zspec = lambda off: pl.BlockSpec((None, rows, width), lambda bi, ci: (bi, ci, cb + off))
    return pl.pallas_call(
        kern,
        grid=(b, s // rows),
        in_specs=[
            zspec(0), zspec(1), zspec(2), zspec(3),
            pl.BlockSpec((None, rows, LANES), lambda bi, ci: (bi, ci, 0)),
            pl.BlockSpec((1, LANES), lambda bi, ci: (0, 0)),
            pl.BlockSpec((taps, 2 * width), lambda bi, ci: (0, 0)),
            pl.BlockSpec((1, 2 * width), lambda bi, ci: (0, 0)),
        ],
        out_specs=pl.BlockSpec((None, rows, width), lambda bi, ci: (bi, ci, 0)),
        out_shape=jax.ShapeDtypeStruct((b, s, width), BF16),
        scratch_shapes=[
            pltpu.VMEM((8, 2 * width), F32),
            pltpu.VMEM((heads, LANES, 2 * LANES), F32),
            pltpu.VMEM((2 * heads, LANES), F32),
        ],
        compiler_params=_params("arbitrary", "arbitrary"),
        name="mlstm",
    )(z, z, z, z, gates, gate_bias, conv_w, conv_b)


def _mix_ffn_kernel(*refs, n_in, chunk):
    y_refs = refs[:n_in]
    wo_refs = refs[n_in:2 * n_in]
    (x_ref, gm_ref, sh_ref, sc_ref, g_ref, gate_ref, w1_ref, w3_ref, w2_ref, o_ref) = refs[2 * n_in:]
    mix = _dot(y_refs[0][...], wo_refs[0][...])
    for y_ref, w_ref in zip(y_refs[1:], wo_refs[1:]):
        mix = mix + _dot(y_ref[...], w_ref[...])
    x = x_ref[...] + gm_ref[...] * mix
    h = _norm_modulate(x, g_ref[...], sh_ref[...], sc_ref[...]).astype(BF16)
    hidden = w1_ref.shape[1]
    acc = None
    for c0 in range(0, hidden, chunk):
        a = _dot(h, w1_ref[:, c0:c0 + chunk])
        bb = _dot(h, w3_ref[:, c0:c0 + chunk])
        u = (_silu(a) * bb).astype(BF16)
        part = _dot(u, w2_ref[c0:c0 + chunk, :])
        acc = part if acc is None else acc + part
    o_ref[...] = x + gate_ref[...] * acc


def _mix_ffn_call(ys, wo_all, wo_layer, x, gate_mix, shift, scale, g, gate, w1, w3, w2, layer, name):
    b, s, d = x.shape
    hidden = w1.shape[2]
    tm = min(ROW_TILE, s)
    assert s % tm == 0 and hidden % FFN_CHUNK == 0
    assert all(y.shape[2] == ys[0].shape[2] for y in ys)
    kern = functools.partial(_mix_ffn_kernel, n_in=len(ys), chunk=FFN_CHUNK)
    vec = pl.BlockSpec((None, 1, d), lambda i, j: (i, 0, 0))
    rows = lambda width: pl.BlockSpec((None, tm, width), lambda i, j: (i, j, 0))
    stacked = lambda w: pl.BlockSpec((None,) + w.shape[1:], lambda i, j: (layer, 0, 0))
    wo_specs = [pl.BlockSpec((None, y.shape[2], d), lambda i, j, n=n: (wo_layer, n, 0))
                for n, y in enumerate(ys)]
    return pl.pallas_call(
        kern,
        grid=(b, s // tm),
        in_specs=[rows(y.shape[2]) for y in ys] + wo_specs + [
            rows(d), vec, vec, vec,
            pl.BlockSpec((1, d), lambda i, j: (0, 0)),
            vec, stacked(w1), stacked(w3), stacked(w2),
        ],
        out_specs=rows(d),
        out_shape=jax.ShapeDtypeStruct((b, s, d), F32),
        compiler_params=_params("arbitrary", "arbitrary"),
        name=name,
    )(*ys, *([wo_all] * len(ys)), x, gate_mix, shift, scale, g, gate, w1, w3, w2)


def _pad_cols(w, n):
    return jnp.pad(w, ((0, 0), (0, n - w.shape[1])))


def _gain_row(q_g, k_g, width, total):
    reps = width // HEAD_DIM
    row = jnp.concatenate([jnp.tile(q_g.astype(F32) * (HEAD_DIM ** -0.5 * LOG2E), reps),
                           jnp.tile(k_g.astype(F32), reps),
                           jnp.ones((total - 2 * width,), F32)])
    return row.reshape(1, total)


def kernel(x, c, ada_w, ada_b, norm_mix_g, norm_ffn_g, ab_w_in, ml_b_i, ml_b_f, ml_conv_w, ml_conv_b,
           da_q_g, da_k_g, da_lambda, da_subln_g, ab_w_out, fx_w_in, fx_b_f, fx_q_g, fx_k_g, fx_w_out,
           ffn_w1, ffn_w3, ffn_w2):
    depth = ada_w.shape[0]
    b, s, d = x.shape
    ml_width = ml_conv_w.shape[2] // 2
    ml_heads = ml_b_i.shape[1]
    da_width = (ab_w_in.shape[2] - 4 * ml_width - 2 * ml_heads) // 3
    da_heads = da_width // (2 * HEAD_DIM)
    fx_heads = fx_b_f.shape[1]
    fx_width = fx_heads * HEAD_DIM
    ab_main = 3 * da_width + 4 * ml_width
    fx_main = 3 * fx_width

    mods = _ada_call(c, ada_w, ada_b).reshape(depth, b, 6, 1, d)
    slopes = 2.0 ** (-8.0 * jnp.arange(1, da_heads + 1, dtype=F32) / da_heads)

    ab_w_in_b, fx_w_in_b = ab_w_in.astype(BF16), fx_w_in.astype(BF16)
    ab_w_out_b, fx_w_out_b = ab_w_out.astype(BF16), fx_w_out.astype(BF16)
    ffn_w1_b, ffn_w3_b, ffn_w2_b = ffn_w1.astype(BF16), ffn_w3.astype(BF16), ffn_w2.astype(BF16)

    for l in range(depth):
        sh_m, sc_m, g_m, sh_f, sc_f, g_f = (mods[l, :, t] for t in range(6))
        j = l // 2
        g_mix = norm_mix_g[l].reshape(1, d)
        if l % 2 == 0:
            z, gates = _inproj_call(
                x, sh_m, sc_m, g_mix, ab_w_in_b, j, ab_main,
                _pad_cols(ab_w_in[j][:, ab_main:], LANES).astype(BF16),
                _gain_row(da_q_g[j], da_k_g[j], da_width, ab_main),
                2 * da_width, "inproj_even")
            lambda_init = 0.8 - 0.6 * math.exp(-0.3 * l)
            y_da = _da_call(z, slopes, da_lambda[j], da_subln_g[j], da_heads, lambda_init)
            gate_bias = _pad_cols(jnp.concatenate([ml_b_i[j], ml_b_f[j]]).reshape(1, -1), LANES)
            y_ml = _mlstm_call(z, gates, gate_bias, ml_conv_w[j], ml_conv_b[j].reshape(1, -1),
                               ml_heads, 3 * da_width)
            ys, wo_all, name = [y_da, y_ml], ab_w_out_b, "mix_ffn_even"
        else:
            z, gates = _inproj_call(
                x, sh_m, sc_m, g_mix, fx_w_in_b, j, fx_main,
                _pad_cols(fx_w_in[j][:, fx_main:], LANES).astype(BF16),
                _gain_row(fx_q_g[j], fx_k_g[j], fx_width, fx_main),
                2 * fx_width, "inproj_odd")
            fneg = _fox_gate_call(gates, _pad_cols(fx_b_f[j].reshape(1, -1), LANES))
            ys, wo_all, name = [_fox_call(z, fneg, fx_heads)], fx_w_out_b, "mix_ffn_odd"
        x = _mix_ffn_call(ys, wo_all, j, x, g_m, sh_f, sc_f, norm_ffn_g[l].reshape(1, d), g_f,
                          ffn_w1_b, ffn_w3_b, ffn_w2_b, l, name)
    return x
```

```python
import functools
import math

import jax
import jax.numpy as jnp
from jax import lax
from jax.experimental import pallas as pl
from jax.experimental.pallas import tpu as pltpu

F32 = jnp.float32
BF16 = jnp.bfloat16

EPS = 1e-6
LANES = 128
HEAD_DIM = 64
CHUNK = 64
CHUNK_SHIFT = CHUNK.bit_length() - 1
assert CHUNK == 1 << CHUNK_SHIFT
NEG = -1e30
LOG2E = math.log2(math.e)
BIAS_PIECES = 3
ONES_ROWS = 16

VMEM_LIMIT = 56 * 1024 * 1024

ADA_COLS = 1536
ROW_TILE = 512
INPROJ_CHUNK = 512
ATTN_TILE = 256
ATTN_RING = 2
ATTN_KEYS = ATTN_RING * ATTN_TILE
ATTN_BLOCKS = 4
PACK_ROWS = 512
ML_CHUNK = 256
ML_CHUNKS_PER_STEP = 2
GATE_CHUNK = 256
FFN_CHUNK = 256


def _params(*sem):
    return pltpu.CompilerParams(dimension_semantics=sem, vmem_limit_bytes=VMEM_LIMIT)


def _log_sigmoid(x):
    return jnp.minimum(x, 0.0) - jnp.log1p(jnp.exp(-jnp.abs(x)))


def _sigmoid(x):
    return 0.5 * jnp.tanh(0.5 * x) + 0.5


def _silu(x):
    h = 0.5 * x
    return h * (jnp.tanh(h) + 1.0)


def _prefix_max_lanes(x):
    axis = x.ndim - 1
    lane = lax.broadcasted_iota(jnp.int32, x.shape, axis)
    shift = 1
    while shift < x.shape[axis]:
        x = jnp.maximum(x, jnp.where(lane >= shift, pltpu.roll(x, shift, axis=axis), NEG))
        shift *= 2
    return x


def _split3(x):
    x1 = x.astype(BF16)
    r1 = x - x1.astype(F32)
    x2 = r1.astype(BF16)
    x3 = (r1 - x2.astype(F32)).astype(BF16)
    return x1, x2, x3


def _dot(a, b):
    return jnp.dot(a, b, preferred_element_type=F32)


def _dot_nt(a, b):
    return lax.dot_general(a, b, (((1,), (1,)), ((), ())), preferred_element_type=F32)


def _dot_tn(a, b):
    return lax.dot_general(a, b, (((0,), (0,)), ((), ())), preferred_element_type=F32)


def _exact_lhs_dot(m_bf16, x):
    x1, x2, x3 = _split3(x)
    return _dot(m_bf16, x1) + _dot(m_bf16, x2) + _dot(m_bf16, x3)


def _exact_rhs_dot(x, m_bf16):
    x1, x2, x3 = _split3(x)
    return _dot(x1, m_bf16) + _dot(x2, m_bf16) + _dot(x3, m_bf16)


def _norm_modulate(x, g, sh, sc):
    ms = jnp.mean(x * x, axis=-1, keepdims=True)
    return x * lax.rsqrt(ms + EPS) * (g * (1.0 + sc)) + sh


def _ada_kernel(c_ref, w_ref, b_ref, o_ref):
    c = c_ref[...]
    ca = _silu(c)
    c1, c2, c3 = _split3(ca)
    w = w_ref[...]
    w1 = w.astype(BF16)
    w2 = (w - w1.astype(F32)).astype(BF16)
    acc = _dot(c1, w1) + _dot(c1, w2) + _dot(c2, w1) + _dot(c2, w2) + _dot(c3, w1)
    o_ref[...] = acc + b_ref[...]


def _ada_call(c, ada_w, ada_b):
    depth, d, n = ada_w.shape
    b = c.shape[0]
    tn = min(ADA_COLS, n)
    assert n % tn == 0
    return pl.pallas_call(
        _ada_kernel,
        grid=(depth, n // tn),
        in_specs=[
            pl.BlockSpec((b, d), lambda l, j: (0, 0)),
            pl.BlockSpec((None, d, tn), lambda l, j: (l, 0, j)),
            pl.BlockSpec((None, 1, tn), lambda l, j: (l, 0, j)),
        ],
        out_specs=pl.BlockSpec((None, b, tn), lambda l, j: (l, 0, j)),
        out_shape=jax.ShapeDtypeStruct((depth, b, n), F32),
        compiler_params=_params("arbitrary", "arbitrary"),
        name="ada_mod",
    )(c, ada_w, ada_b.reshape(depth, 1, n))


def _inproj_kernel(x_ref, sh_ref, sc_ref, g_ref, w_ref, wg_ref, gain_ref, z_ref, gate_ref,
                   *, n_norm, chunk):
    half = x_ref.shape[0] // 2
    h_rows, first_rows = [], []
    for r0 in (0, half):
        h_part = _norm_modulate(x_ref[r0:r0 + half, :], g_ref[...], sh_ref[...], sc_ref[...]).astype(BF16)
        first_rows.append(_dot(h_part, w_ref[:, 0:chunk]))
        h_rows.append(h_part)
    h = jnp.concatenate(h_rows, axis=0)
    n = w_ref.shape[1]
    lane = lax.broadcasted_iota(jnp.int32, (1, LANES), 1)
    lo = lane < HEAD_DIM
    for c0 in range(0, n, chunk):
        zc = jnp.concatenate(first_rows, axis=0) if c0 == 0 else _dot(h, w_ref[:, c0:c0 + chunk])
        if c0 < n_norm:
            parts = []
            for s0 in range(0, chunk, LANES):
                zs = zc[:, s0:s0 + LANES]
                sq = zs * zs
                s_lo = jnp.sum(jnp.where(lo, sq, 0.0), axis=-1, keepdims=True)
                s_hi = jnp.sum(jnp.where(lo, 0.0, sq), axis=-1, keepdims=True)
                r_lo = lax.rsqrt(s_lo * (1.0 / HEAD_DIM) + EPS)
                r_hi = lax.rsqrt(s_hi * (1.0 / HEAD_DIM) + EPS)
                parts.append(zs * jnp.where(lo, r_lo, r_hi))
            zc = jnp.concatenate(parts, axis=-1) * gain_ref[:, c0:c0 + chunk]
        z_ref[:, c0:c0 + chunk] = zc.astype(BF16)
    gate_ref[...] = _dot(h, wg_ref[...])


def _inproj_call(x, shift, scale, g, w_all, layer, n, wg, gain, n_norm, name):
    b, s, d = x.shape
    tm = min(ROW_TILE, s)
    chunk = INPROJ_CHUNK
    assert s % tm == 0 and tm % 2 == 0 and n % chunk == 0 and n_norm % chunk == 0
    kern = functools.partial(_inproj_kernel, n_norm=n_norm, chunk=chunk)
    return pl.pallas_call(
        kern,
        grid=(b, s // tm),
        in_specs=[
            pl.BlockSpec((None, tm, d), lambda i, j: (i, j, 0)),
            pl.BlockSpec((None, 1, d), lambda i, j: (i, 0, 0)),
            pl.BlockSpec((None, 1, d), lambda i, j: (i, 0, 0)),
            pl.BlockSpec((1, d), lambda i, j: (0, 0)),
            pl.BlockSpec((None, d, n), lambda i, j: (layer, 0, 0)),
            pl.BlockSpec((d, LANES), lambda i, j: (0, 0)),
            pl.BlockSpec((1, n), lambda i, j: (0, 0)),
        ],
        out_specs=[
            pl.BlockSpec((None, tm, n), lambda i, j: (i, j, 0)),
            pl.BlockSpec((None, tm, LANES), lambda i, j: (i, j, 0)),
        ],
        out_shape=[
            jax.ShapeDtypeStruct((b, s, n), BF16),
            jax.ShapeDtypeStruct((b, s, LANES), F32),
        ],
        compiler_params=_params("arbitrary", "arbitrary"),
        name=name,
    )(x, shift, scale, g, w_all, wg, gain)


def _lane_ids():
    return lax.broadcasted_iota(jnp.int32, (1, LANES), 1)


def _place_pieces(pieces, first_lane):
    lane = _lane_ids()
    out = jnp.zeros(pieces[0].shape, F32)
    for j, p in enumerate(pieces):
        out = jnp.where(lane == first_lane + j, p.astype(F32), out)
    return out


def _pack_keys(blocks, shared_fn):
    s = blocks[0][0].shape[0]
    rows = min(PACK_ROWS, s)
    keep_lo = jnp.where(_lane_ids() < HEAD_DIM, 1.0, 0.0).astype(BF16)
    keep_hi = jnp.where(_lane_ids() < HEAD_DIM, 0.0, 1.0).astype(BF16)

    def body(c, carry):
        r0 = pl.multiple_of(c * rows, rows)
        shared = shared_fn(r0, rows)
        for k_ref, v_ref, ka_ref, kb_ref, vt_refs, extras_fn in blocks:
            k2 = k_ref[pl.ds(r0, rows), :]
            ea, eb = extras_fn(shared)
            ka_ref[pl.ds(r0, rows), :] = k2 * keep_lo + ea.astype(BF16)
            kb_ref[pl.ds(r0, rows), :] = k2 * keep_hi + eb.astype(BF16)
            vt = v_ref[pl.ds(r0, rows), :].T
            for ref, c0, width in vt_refs:
                ref[0:width, pl.ds(r0, rows)] = vt[c0:c0 + width, :]
                ref[width:width + ONES_ROWS, pl.ds(r0, rows)] = jnp.ones((ONES_ROWS, rows), BF16)
        return carry

    lax.fori_loop(0, s // rows, body, 0)


def _pack_queries(q2, factors):
    lane = _lane_ids()
    q2 = q2.astype(F32)
    lo = lane < HEAD_DIM
    fac_a = _place_pieces(factors, HEAD_DIM)
    fac_b = _place_pieces(factors, 0)
    return jnp.where(lo, q2, fac_a).astype(BF16), jnp.where(lo, fac_b, q2).astype(BF16)


def _init_stats(m_ref, acc_ref):
    m_ref[...] = jnp.full(m_ref.shape, NEG, F32)
    acc_ref[...] = jnp.zeros(acc_ref.shape, F32)


def _softmax_step(s, tile_max, vt, m_ref, acc_ref):
    m_old = m_ref[...]
    m_new = jnp.maximum(m_old, tile_max)
    alpha = jnp.exp2(m_old - m_new)
    p = jnp.exp2(s - m_new)
    acc_ref[...] = alpha * acc_ref[...] + _dot(vt, p.astype(BF16))
    m_ref[...] = m_new


def _attend(n, t, tiles, s_buf, mx_buf, make_addend, finish_pair):
    tk = ATTN_KEYS
    ring = len(tiles)
    count = len(tiles[0])

    def qk_one(u, step, idx, rows=tk):
        k0 = pl.multiple_of(step * tk, tk)
        q, k_ref = tiles[u][idx][0], tiles[u][idx][1]
        s = _dot_nt(k_ref[pl.ds(k0, rows), :], q)
        s_buf[u][idx, 0:rows, :] = s
        if rows == tk:
            mx_buf[u][idx] = jnp.max(s, axis=0, keepdims=True)

    def stage(u, step, qk_tile, qk_step, qk_rows=tk, last=False):
        k0 = pl.multiple_of(step * tk, tk)
        rows, offset = (u + 1) * t, u * t
        addends = {}
        for idx, (_, _, vt, add_key, stats) in enumerate(tiles[u]):
            if qk_tile is not None:
                qk_one(qk_tile, qk_step, idx, qk_rows)
            if not last:
                _softmax_step(s_buf[u][idx], mx_buf[u][idx], vt(k0, tk), *stats)
                continue
            if add_key not in addends:
                addends[add_key] = make_addend(add_key, addends.setdefault("shared", {}), rows, offset)
            s = s_buf[u][idx, 0:rows, :] + addends[add_key]
            _softmax_step(s, jnp.max(s, axis=0, keepdims=True), vt(k0, rows), *stats)
            if idx % 2 == 1:
                finish_pair(u, idx // 2)

    for idx in range(count):
        qk_one(0, 0, idx)
        for streams in tiles:
            _init_stats(*streams[idx][4])

    def body(j, carry):
        for u in range(ring):
            nxt = (u + 1) % ring
            stage(u, j, nxt, j + 1 if nxt == 0 else j)
        return carry

    lax.fori_loop(0, n - 1, body, 0)
    for u in range(ring):
        if u + 1 < ring:
            stage(u, n - 1, u + 1, n - 1, qk_rows=(u + 2) * t, last=True)
        else:
            stage(u, n - 1, None, None, last=True)


def _lane_block(ref, h):
    return ref.at[:, pl.ds(h * LANES, LANES)]


def _da_kernel(slope_ref, q_ref, k_ref, v_ref, lam_ref, sg_ref, o_ref,
               ka_ref, kb_ref, vt_ref, m_ref, acc_ref, s0_ref, s1_ref, mx0_ref, mx1_ref,
               *, t, hb, lambda_init):
    dv = LANES
    g = pl.program_id(1)
    i = pl.program_id(2)
    slopes2 = [slope_ref[g * hb + h] * LOG2E for h in range(hb)]

    @pl.when(i == 0)
    def _():
        def position_pieces(r0, rows):
            pos = lax.broadcasted_iota(jnp.int32, (rows, LANES), 0) + r0
            hi = jnp.right_shift(pos, CHUNK_SHIFT).astype(F32)
            lo = jnp.bitwise_and(pos, CHUNK - 1).astype(F32)
            lane = _lane_ids()

            def place(first):
                in_hi = (lane >= first) & (lane < first + BIAS_PIECES)
                in_lo = (lane >= first + BIAS_PIECES) & (lane < first + 2 * BIAS_PIECES)
                return jnp.where(in_hi, hi, jnp.where(in_lo, lo, 0.0)).astype(BF16)
            return place(HEAD_DIM), place(0)
        _pack_keys([(_lane_block(k_ref, h), _lane_block(v_ref, h), ka_ref.at[h], kb_ref.at[h],
                     [(vt_ref.at[h], 0, dv)], lambda shared: shared) for h in range(hb)],
                   position_pieces)

    def last_step_addend(h, shared, rows, offset):
        if "base" not in shared:
            key = lax.broadcasted_iota(jnp.int32, (rows, t), 0)
            qry = lax.broadcasted_iota(jnp.int32, (rows, t), 1) + offset
            allowed = jnp.right_shift(key, CHUNK_SHIFT) <= jnp.right_shift(qry, CHUNK_SHIFT)
            shared["base"] = jnp.where(allowed, jnp.maximum(key - qry, 0).astype(F32), -NEG)
        return (-2.0 * slopes2[h]) * shared["base"]

    ring = q_ref.shape[0] // t
    tiles = []
    for u in range(ring):
        streams = []
        for h in range(hb):
            slope_pieces = [p.astype(F32) for p in _split3(jnp.full((1, LANES), slopes2[h], F32))]
            factors = [p * float(CHUNK) for p in slope_pieces] + slope_pieces
            qa, qb = _pack_queries(q_ref[u * t:(u + 1) * t, h * LANES:(h + 1) * LANES], factors)
            vt = lambda k0, size, h=h: vt_ref[h, :, pl.ds(k0, size)]
            for m, (q, k_sc) in enumerate(((qa, ka_ref), (qb, kb_ref))):
                n_stream = (u * hb + h) * 2 + m
                streams.append((q, k_sc.at[h], vt, h, (m_ref.at[n_stream], acc_ref.at[n_stream])))
        tiles.append(streams)

    def finish_head(u, h):
        lp = lam_ref[...]
        lam = (jnp.exp(jnp.sum(lp[0:1] * lp[1:2], axis=-1, keepdims=True))
               - jnp.exp(jnp.sum(lp[2:3] * lp[3:4], axis=-1, keepdims=True)) + lambda_init)
        pa = acc_ref[(u * hb + h) * 2]
        pb = acc_ref[(u * hb + h) * 2 + 1]
        out = pa[0:dv] / pa[dv:dv + 1] - lam * (pb[0:dv] / pb[dv:dv + 1])
        ms = jnp.mean(out * out, axis=0, keepdims=True)
        out = out * lax.rsqrt(ms + EPS) * (sg_ref[...] * (1.0 - lambda_init))
        o_ref[u * t:(u + 1) * t, h * LANES:(h + 1) * LANES] = out.T.astype(BF16)

    _attend(i + 1, t, tiles, (s0_ref, s1_ref), (mx0_ref, mx1_ref), last_step_addend, finish_head)


def _da_call(z, slopes, lam_p, subln_g, heads, lambda_init):
    b, s, _ = z.shape
    t, tk = ATTN_TILE, ATTN_KEYS
    assert s % tk == 0 and t % CHUNK == 0 and tk == ATTN_RING * t
    assert s <= CHUNK * 256, "key position // CHUNK must stay exact in bf16"
    hb = min(ATTN_BLOCKS, heads)
    groups = heads // hb
    assert heads % hb == 0
    kern = functools.partial(_da_kernel, t=t, hb=hb, lambda_init=lambda_init)
    w = hb * LANES
    n_streams = 2 * hb * ATTN_RING
    return pl.pallas_call(
        kern,
        grid_spec=pltpu.PrefetchScalarGridSpec(
            num_scalar_prefetch=1,
            grid=(b, groups, s // tk),
            in_specs=[
                pl.BlockSpec((None, tk, w), lambda bi, g, i, sl: (bi, i, g)),
                pl.BlockSpec((None, s, w), lambda bi, g, i, sl: (bi, 0, groups + g)),
                pl.BlockSpec((None, s, w), lambda bi, g, i, sl: (bi, 0, 2 * groups + g)),
                pl.BlockSpec((4, HEAD_DIM), lambda bi, g, i, sl: (0, 0)),
                pl.BlockSpec((LANES, 1), lambda bi, g, i, sl: (0, 0)),
            ],
            out_specs=pl.BlockSpec((None, tk, w), lambda bi, g, i, sl: (bi, i, g)),
            scratch_shapes=[
                pltpu.VMEM((hb, s, LANES), BF16), pltpu.VMEM((hb, s, LANES), BF16),
                pltpu.VMEM((hb, LANES + ONES_ROWS, s), BF16),
                pltpu.VMEM((n_streams, 1, t), F32),
                pltpu.VMEM((n_streams, LANES + ONES_ROWS, t), F32),
                pltpu.VMEM((2 * hb, ATTN_KEYS, t), F32), pltpu.VMEM((2 * hb, ATTN_KEYS, t), F32),
                pltpu.VMEM((2 * hb, 1, t), F32), pltpu.VMEM((2 * hb, 1, t), F32),
            ],
        ),
        out_shape=jax.ShapeDtypeStruct((b, s, heads * LANES), BF16),
        compiler_params=_params("arbitrary", "arbitrary", "arbitrary"),
        name="diff_attention",
    )(slopes, z, z, z, lam_p, subln_g.reshape(LANES, 1))


def _fox_gate_kernel(g_ref, b_ref, o_ref, *, chunk):
    s = g_ref.shape[0]
    r = lax.broadcasted_iota(jnp.int32, (chunk, chunk), 0)
    c = lax.broadcasted_iota(jnp.int32, (chunk, chunk), 1)
    tri = (c <= r).astype(BF16)
    carry = jnp.zeros((1, LANES), F32)
    for c0 in range(0, s, chunk):
        ls = _log_sigmoid(g_ref[c0:c0 + chunk, :] + b_ref[...])
        cs = _exact_lhs_dot(tri, ls) + carry
        carry = cs[chunk - 1:chunk, :]
        o_ref[c0:c0 + chunk, :] = cs * (-LOG2E)


def _fox_gate_call(gates, bias_row):
    b, s, _ = gates.shape
    chunk = min(GATE_CHUNK, s)
    assert s % chunk == 0
    kern = functools.partial(_fox_gate_kernel, chunk=chunk)
    return pl.pallas_call(
        kern,
        grid=(b,),
        in_specs=[
            pl.BlockSpec((None, s, LANES), lambda i: (i, 0, 0)),
            pl.BlockSpec((1, LANES), lambda i: (0, 0)),
        ],
        out_specs=pl.BlockSpec((None, s, LANES), lambda i: (i, 0, 0)),
        out_shape=jax.ShapeDtypeStruct((b, s, LANES), F32),
        compiler_params=_params("arbitrary"),
        name="fox_gate_cumsum",
    )(gates, bias_row)


def _fox_kernel(q_ref, k_ref, v_ref, f_ref, o_ref,
                ka_ref, kb_ref, vt_ref, m_ref, acc_ref, s0_ref, s1_ref, mx0_ref, mx1_ref, *, t, hb):
    dv = HEAD_DIM
    g = pl.program_id(1)
    i = pl.program_id(2)

    @pl.when(i == 0)
    def _():
        n = BIAS_PIECES * LANES
        src = lax.broadcasted_iota(jnp.int32, (n, LANES), 0)
        dst = lax.broadcasted_iota(jnp.int32, (n, LANES), 1)
        piece = jnp.right_shift(src, LANES.bit_length() - 1)
        head = src - piece * LANES
        blocks = []
        for h in range(hb):
            pair = g * hb + h
            place_a = ((head == 2 * pair) & (dst == HEAD_DIM + piece)).astype(BF16)
            place_b = ((head == 2 * pair + 1) & (dst == piece)).astype(BF16)
            place = jnp.concatenate([place_a, place_b], axis=-1)

            def extras(pieces, place=place):
                both = _dot(pieces, place)
                return both[:, 0:LANES], both[:, LANES:2 * LANES]
            blocks.append((_lane_block(k_ref, h), _lane_block(v_ref, h), ka_ref.at[h], kb_ref.at[h],
                           [(vt_ref.at[2 * h + m], m * dv, dv) for m in range(2)], extras))
        _pack_keys(blocks,
                   lambda r0, rows: jnp.concatenate(_split3(f_ref[pl.ds(r0, rows), :]), axis=-1))

    def last_step_addend(_, shared, rows, offset):
        key = lax.broadcasted_iota(jnp.int32, (rows, t), 0)
        qry = lax.broadcasted_iota(jnp.int32, (rows, t), 1) + offset
        return jnp.where(key <= qry, 0.0, NEG)

    ring = q_ref.shape[0] // t
    ones = [jnp.ones((1, LANES), F32)] * BIAS_PIECES
    tiles = []
    for u in range(ring):
        streams = []
        for h in range(hb):
            qa, qb = _pack_queries(q_ref[u * t:(u + 1) * t, h * LANES:(h + 1) * LANES], ones)
            for m, (q, k_sc) in enumerate(((qa, ka_ref), (qb, kb_ref))):
                vt = lambda k0, size, n=2 * h + m: vt_ref[n, :, pl.ds(k0, size)]
                n_stream = (u * hb + h) * 2 + m
                streams.append((q, k_sc.at[h], vt, 0, (m_ref.at[n_stream], acc_ref.at[n_stream])))
        tiles.append(streams)

    def finish_pair(u, h):
        pa = acc_ref[(u * hb + h) * 2]
        pb = acc_ref[(u * hb + h) * 2 + 1]
        out = jnp.concatenate([pa[0:dv] / pa[dv:dv + 1], pb[0:dv] / pb[dv:dv + 1]], axis=0)
        o_ref[u * t:(u + 1) * t, h * LANES:(h + 1) * LANES] = out.T.astype(BF16)

    _attend(i + 1, t, tiles, (s0_ref, s1_ref), (mx0_ref, mx1_ref), last_step_addend, finish_pair)


def _fox_call(z, fneg, heads):
    b, s, _ = z.shape
    pairs = heads // 2
    t, tk = ATTN_TILE, ATTN_KEYS
    assert s % tk == 0 and tk == ATTN_RING * t
    hb = min(ATTN_BLOCKS, pairs)
    groups = pairs // hb
    assert pairs % hb == 0
    kern = functools.partial(_fox_kernel, t=t, hb=hb)
    w = hb * LANES
    n_streams = 2 * hb * ATTN_RING
    return pl.pallas_call(
        kern,
        grid=(b, groups, s // tk),
        in_specs=[
            pl.BlockSpec((None, tk, w), lambda bi, g, i: (bi, i, g)),
            pl.BlockSpec((None, s, w), lambda bi, g, i: (bi, 0, groups + g)),
            pl.BlockSpec((None, s, w), lambda bi, g, i: (bi, 0, 2 * groups + g)),
            pl.BlockSpec((None, s, LANES), lambda bi, g, i: (bi, 0, 0)),
        ],
        out_specs=pl.BlockSpec((None, tk, w), lambda bi, g, i: (bi, i, g)),
        out_shape=jax.ShapeDtypeStruct((b, s, pairs * LANES), BF16),
        scratch_shapes=[
            pltpu.VMEM((hb, s, LANES), BF16), pltpu.VMEM((hb, s, LANES), BF16),
            pltpu.VMEM((2 * hb, HEAD_DIM + ONES_ROWS, s), BF16),
            pltpu.VMEM((n_streams, 1, t), F32),
            pltpu.VMEM((n_streams, HEAD_DIM + ONES_ROWS, t), F32),
            pltpu.VMEM((2 * hb, ATTN_KEYS, t), F32), pltpu.VMEM((2 * hb, ATTN_KEYS, t), F32),
            pltpu.VMEM((2 * hb, 1, t), F32), pltpu.VMEM((2 * hb, 1, t), F32),
        ],
        compiler_params=_params("arbitrary", "arbitrary", "arbitrary"),
        name="forgetting_attention",
    )(z, z, z, fneg)


def _mlstm_kernel(q_ref, k_ref, v_ref, o_ref, g_ref, gb_ref, cw_ref, cb_ref, y_ref,
                  halo_ref, ct_ref, m_ref, *, heads, taps, lc):
    @pl.when(pl.program_id(1) == 0)
    def _():
        halo_ref[...] = jnp.zeros(halo_ref.shape, F32)
        ct_ref[...] = jnp.zeros(ct_ref.shape, F32)
        m_ref[...] = jnp.zeros(m_ref.shape, F32)

    rows = lambda ref, sub: ref.at[pl.ds(sub * lc, lc), :]
    edge = halo_ref[...]
    for sub in range(q_ref.shape[0] // lc):
        prep, edge = _mlstm_prep(rows(q_ref, sub), rows(k_ref, sub), rows(g_ref, sub), gb_ref,
                                 cw_ref, cb_ref, edge, heads, taps)
        for hd in range(heads):
            _mlstm_head(prep, rows(v_ref, sub), rows(o_ref, sub), rows(y_ref, sub),
                        ct_ref, m_ref, hd, heads)
    halo_ref[...] = edge


def _mlstm_prep(q_ref, k_ref, g_ref, gb_ref, cw_ref, cb_ref, halo, heads, taps):
    lc, width = q_ref.shape
    pad = halo.shape[0]

    r = lax.broadcasted_iota(jnp.int32, (lc, lc), 0)
    c = lax.broadcasted_iota(jnp.int32, (lc, lc), 1)
    causal = c <= r

    gcol = g_ref[...] + gb_ref[...]
    grow = gcol.T[:2 * heads, :]
    a_cols = _exact_lhs_dot(causal.astype(BF16), _log_sigmoid(gcol))
    a_rows = _exact_rhs_dot(_log_sigmoid(grow), (r <= c).astype(BF16))

    b_half = grow[0:heads, :] - a_rows[heads:2 * heads, :]
    b_rows = jnp.concatenate([b_half, b_half], axis=0)
    prefix_cols = _prefix_max_lanes(b_rows).T

    xb = jnp.concatenate([q_ref[...], k_ref[...]], axis=-1)
    xf = xb.astype(F32)
    conv = cb_ref[...] + xf * cw_ref[taps - 1:taps, :]
    for sft in range(1, taps):
        shifted = _dot((r - c == sft).astype(BF16), xb)
        conv = conv + shifted * cw_ref[taps - 1 - sft:taps - sft, :]
    edge = jnp.concatenate([halo, xf[0:pad, :]], axis=0)
    first = cb_ref[...]
    for j in range(taps):
        off = pad - (taps - 1) + j
        first = first + edge[off:off + pad, :] * cw_ref[j:j + 1, :]
    conv = jnp.concatenate([first, conv[pad:, :]], axis=0)
    qk = _silu(conv)
    prep = dict(qk=qk, gcol=gcol, a_cols=a_cols, b_rows=b_rows, causal=causal,
                prefix_cols=prefix_cols)
    return prep, xf[lc - pad:, :]


def _mlstm_head(prep, v_ref, o_ref, y_ref, ct_ref, m_ref, hd, heads):
    lc, width = v_ref.shape
    d = width // heads
    qk, gcol, a_cols, causal = prep["qk"], prep["gcol"], prep["a_cols"], prep["causal"]
    q = qk[:, hd * d:(hd + 1) * d]
    k = qk[:, width + hd * d:width + (hd + 1) * d] * (d ** -0.5)
    v = v_ref[:, hd * d:(hd + 1) * d]
    qb = q.astype(BF16)
    kb = k.astype(BF16)
    ig_c = gcol[:, hd:hd + 1]
    a_c = a_cols[:, heads + hd:heads + hd + 1]
    a_last = a_c[lc - 1:lc, :]
    ct = ct_ref[hd]
    m_prev = m_ref[hd:hd + 1, 0:1]
    x_c = jnp.maximum(prep["prefix_cols"][:, hd:hd + 1], m_prev)

    g_c = a_last - a_c + ig_c
    m_loc = jnp.max(g_c, axis=0, keepdims=True)
    w_c = jnp.exp(g_c - m_loc)
    m_new = jnp.maximum(a_last + m_prev, m_loc)
    decay = jnp.exp(a_last + m_prev - m_new)
    s_loc = jnp.exp(m_loc - m_new)
    wv = jnp.concatenate([w_c * v.astype(F32), jnp.broadcast_to(w_c, (lc, d))], axis=-1)
    increment = _dot_tn(kb, wv.astype(BF16))
    raw = _dot_nt(qb, kb)
    from_state = _dot(qb, ct.astype(BF16))

    decay_mat = jnp.exp2(jnp.where(causal, prep["b_rows"][hd:hd + 1, :] * LOG2E - x_c * LOG2E, NEG))
    sqk = (raw * decay_mat).astype(BF16)
    intra = _dot(sqk, jnp.concatenate([v, jnp.ones((lc, d), BF16)], axis=-1))
    inter = jnp.exp(m_prev - x_c) * from_state
    num = inter[:, 0:d] + intra[:, 0:d]
    den = inter[:, d:d + 1] + intra[:, d:d + 1]
    hval = num / jnp.maximum(jnp.abs(den), jnp.exp(-(a_c + x_c)))
    og = o_ref[:, hd * d:(hd + 1) * d].astype(F32)
    y_ref[:, hd * d:(hd + 1) * d] = (_sigmoid(og) * hval).astype(BF16)

    ct_ref[hd] = decay * ct + s_loc * increment
    m_ref[hd:hd + 1, :] = jnp.broadcast_to(m_new, (1, LANES))


def _mlstm_call(z, gates, gate_bias, conv_w, conv_b, heads, col0):
    b, s, _ = z.shape
    width = conv_w.shape[1] // 2
    lc = min(ML_CHUNK, s)
    rows = min(ML_CHUNKS_PER_STEP * lc, s)
    assert s % rows == 0 and rows % lc == 0 and col0 % width == 0 and width // heads == LANES
    assert 2 * heads == 8, "gate rows are handled as one 8-sublane tile"
    cb = col0 // width
    taps = conv_w.shape[0]
    kern = functools.partial(_mlstm_kernel, heads=heads, taps=taps, lc=lc)
    zspec = lambda off: pl.BlockSpec((None, rows, width), lambda bi, ci: (bi, ci, cb + off))
    return pl.pallas_call(
        kern,
        grid=(b, s // rows),
        in_specs=[
            zspec(0), zspec(1), zspec(2), zspec(3),
            pl.BlockSpec((None, rows, LANES), lambda bi, ci: (bi, ci, 0)),
            pl.BlockSpec((1, LANES), lambda bi, ci: (0, 0)),
            pl.BlockSpec((taps, 2 * width), lambda bi, ci: (0, 0)),
            pl.BlockSpec((1, 2 * width), lambda bi, ci: (0, 0)),
        ],
        out_specs=pl.BlockSpec((None, rows, width), lambda bi, ci: (bi, ci, 0)),
        out_shape=jax.ShapeDtypeStruct((b, s, width), BF16),
        scratch_shapes=[
            pltpu.VMEM((8, 2 * width), F32),
            pltpu.VMEM((heads, LANES, 2 * LANES), F32),
            pltpu.VMEM((2 * heads, LANES), F32),
        ],
        compiler_params=_params("arbitrary", "arbitrary"),
        name="mlstm",
    )(z, z, z, z, gates, gate_bias, conv_w, conv_b)


def _mix_ffn_kernel(*refs, n_in, chunk):
    y_refs = refs[:n_in]
    wo_refs = refs[n_in:2 * n_in]
    (x_ref, gm_ref, sh_ref, sc_ref, g_ref, gate_ref, w1_ref, w3_ref, w2_ref, o_ref) = refs[2 * n_in:]
    mix = _dot(y_refs[0][...], wo_refs[0][...])
    for y_ref, w_ref in zip(y_refs[1:], wo_refs[1:]):
        mix = mix + _dot(y_ref[...], w_ref[...])
    x = x_ref[...] + gm_ref[...] * mix
    h = _norm_modulate(x, g_ref[...], sh_ref[...], sc_ref[...]).astype(BF16)
    hidden = w1_ref.shape[1]

    acc = None
    for c0 in range(0, hidden, chunk):
        a = _dot(h, w1_ref[:, c0:c0 + chunk])
        bb = _dot(h, w3_ref[:, c0:c0 + chunk])
        u = (_silu(a) * bb).astype(BF16)
        part = _dot(u, w2_ref[c0:c0 + chunk, :])
        acc = part if acc is None else acc + part
    o_ref[...] = x + gate_ref[...] * acc


def _mix_ffn_call(ys, wo_all, wo_layer, x, gate_mix, shift, scale, g, gate, w1, w3, w2, layer, name):
    b, s, d = x.shape
    hidden = w1.shape[2]
    tm = min(ROW_TILE, s)
    assert s % tm == 0 and hidden % FFN_CHUNK == 0
    assert all(y.shape[2] == ys[0].shape[2] for y in ys)
    kern = functools.partial(_mix_ffn_kernel, n_in=len(ys), chunk=FFN_CHUNK)
    vec = pl.BlockSpec((None, 1, d), lambda i, j: (i, 0, 0))
    rows = lambda width: pl.BlockSpec((None, tm, width), lambda i, j: (i, j, 0))
    stacked = lambda w: pl.BlockSpec((None,) + w.shape[1:], lambda i, j: (layer, 0, 0))
    wo_specs = [pl.BlockSpec((None, y.shape[2], d), lambda i, j, n=n: (wo_layer, n, 0))
                for n, y in enumerate(ys)]
    return pl.pallas_call(
        kern,
        grid=(b, s // tm),
        in_specs=[rows(y.shape[2]) for y in ys] + wo_specs + [
            rows(d), vec, vec, vec,
            pl.BlockSpec((1, d), lambda i, j: (0, 0)),
            vec, stacked(w1), stacked(w3), stacked(w2),
        ],
        out_specs=rows(d),
        out_shape=jax.ShapeDtypeStruct((b, s, d), F32),
        compiler_params=_params("arbitrary", "arbitrary"),
        name=name,
    )(*ys, *([wo_all] * len(ys)), x, gate_mix, shift, scale, g, gate, w1, w3, w2)


def _pad_cols(w, n):
    return jnp.pad(w, ((0, 0), (0, n - w.shape[1])))


def _gain_row(q_g, k_g, width, total):
    reps = width // HEAD_DIM
    row = jnp.concatenate([jnp.tile(q_g.astype(F32) * (HEAD_DIM ** -0.5 * LOG2E), reps),
                           jnp.tile(k_g.astype(F32), reps),
                           jnp.ones((total - 2 * width,), F32)])
    return row.reshape(1, total)


def kernel(x, c, ada_w, ada_b, norm_mix_g, norm_ffn_g, ab_w_in, ml_b_i, ml_b_f, ml_conv_w, ml_conv_b,
           da_q_g, da_k_g, da_lambda, da_subln_g, ab_w_out, fx_w_in, fx_b_f, fx_q_g, fx_k_g, fx_w_out,
           ffn_w1, ffn_w3, ffn_w2):
    depth = ada_w.shape[0]
    b, s, d = x.shape
    ml_width = ml_conv_w.shape[2] // 2
    ml_heads = ml_b_i.shape[1]
    da_width = (ab_w_in.shape[2] - 4 * ml_width - 2 * ml_heads) // 3
    da_heads = da_width // (2 * HEAD_DIM)
    fx_heads = fx_b_f.shape[1]
    fx_width = fx_heads * HEAD_DIM
    ab_main = 3 * da_width + 4 * ml_width
    fx_main = 3 * fx_width

    mods = _ada_call(c, ada_w, ada_b).reshape(depth, b, 6, 1, d)
    slopes = 2.0 ** (-8.0 * jnp.arange(1, da_heads + 1, dtype=F32) / da_heads)

    ab_w_in_b = ab_w_in[:, :, :ab_main].astype(BF16)
    fx_w_in_b = fx_w_in[:, :, :fx_main].astype(BF16)
    ab_w_out_b, fx_w_out_b = ab_w_out.astype(BF16), fx_w_out.astype(BF16)
    ffn_w1_b, ffn_w3_b, ffn_w2_b = ffn_w1.astype(BF16), ffn_w3.astype(BF16), ffn_w2.astype(BF16)

    for l in range(depth):
        sh_m, sc_m, g_m, sh_f, sc_f, g_f = (mods[l, :, t] for t in range(6))
        j = l // 2
        g_mix = norm_mix_g[l].reshape(1, d)
        if l % 2 == 0:
            z, gates = _inproj_call(
                x, sh_m, sc_m, g_mix, ab_w_in_b, j, ab_main,
                _pad_cols(ab_w_in[j][:, ab_main:], LANES).astype(BF16),
                _gain_row(da_q_g[j], da_k_g[j], da_width, ab_main),
                2 * da_width, "inproj_even")
            lambda_init = 0.8 - 0.6 * math.exp(-0.3 * l)
            y_da = _da_call(z, slopes, da_lambda[j], da_subln_g[j], da_heads, lambda_init)
            gate_bias = _pad_cols(jnp.concatenate([ml_b_i[j], ml_b_f[j]]).reshape(1, -1), LANES)
            y_ml = _mlstm_call(z, gates, gate_bias, ml_conv_w[j], ml_conv_b[j].reshape(1, -1),
                               ml_heads, 3 * da_width)
            ys, wo_all, name = [y_da, y_ml], ab_w_out_b, "mix_ffn_even"
        else:
            z, gates = _inproj_call(
                x, sh_m, sc_m, g_mix, fx_w_in_b, j, fx_main,
                _pad_cols(fx_w_in[j][:, fx_main:], LANES).astype(BF16),
                _gain_row(fx_q_g[j], fx_k_g[j], fx_width, fx_main),
                2 * fx_width, "inproj_odd")
            fneg = _fox_gate_call(gates, _pad_cols(fx_b_f[j].reshape(1, -1), LANES))
            ys, wo_all, name = [_fox_call(z, fneg, fx_heads)], fx_w_out_b, "mix_ffn_odd"
        x = _mix_ffn_call(ys, wo_all, j, x, g_m, sh_f, sc_f, norm_ffn_g[l].reshape(1, d), g_f,
                          ffn_w1_b, ffn_w3_b, ffn_w2_b, l, name)
    return x
```

```python
import functools
import math

import jax
import jax.numpy as jnp
from jax import lax
from jax.experimental import pallas as pl
from jax.experimental.pallas import tpu as pltpu

F32 = jnp.float32
BF16 = jnp.bfloat16

EPS = 1e-6
LANES = 128
HEAD_DIM = 64
CHUNK = 64
CHUNK_SHIFT = CHUNK.bit_length() - 1
assert CHUNK == 1 << CHUNK_SHIFT
NEG = -1e30
LOG2E = math.log2(math.e)
BIAS_PIECES = 3
ONES_ROWS = 16

VMEM_LIMIT = 56 * 1024 * 1024

ADA_COLS = 1536
ROW_TILE = 512
INPROJ_CHUNK = 512
ATTN_TILE = 256
ATTN_RING = 2
ATTN_KEYS = ATTN_RING * ATTN_TILE
ATTN_BLOCKS = 4
PACK_ROWS = 512
ML_CHUNK = 256
ML_CHUNKS_PER_STEP = 2
GATE_CHUNK = 256
FFN_CHUNK = 256


def _params(*sem):
    return pltpu.CompilerParams(dimension_semantics=sem, vmem_limit_bytes=VMEM_LIMIT)


def _log_sigmoid(x):
    return jnp.minimum(x, 0.0) - jnp.log1p(jnp.exp(-jnp.abs(x)))


def _sigmoid(x):
    return 0.5 * jnp.tanh(0.5 * x) + 0.5


def _silu(x):
    h = 0.5 * x
    return h * (jnp.tanh(h) + 1.0)


def _prefix_max_lanes(x):
    axis = x.ndim - 1
    lane = lax.broadcasted_iota(jnp.int32, x.shape, axis)
    shift = 1
    while shift < x.shape[axis]:
        x = jnp.maximum(x, jnp.where(lane >= shift, pltpu.roll(x, shift, axis=axis), NEG))
        shift *= 2
    return x


def _split3(x):
    x1 = x.astype(BF16)
    r1 = x - x1.astype(F32)
    x2 = r1.astype(BF16)
    x3 = (r1 - x2.astype(F32)).astype(BF16)
    return x1, x2, x3


def _dot(a, b):
    return jnp.dot(a, b, preferred_element_type=F32)


def _dot_nt(a, b):
    return lax.dot_general(a, b, (((1,), (1,)), ((), ())), preferred_element_type=F32)


def _dot_tn(a, b):
    return lax.dot_general(a, b, (((0,), (0,)), ((), ())), preferred_element_type=F32)


def _exact_lhs_dot(m_bf16, x):
    x1, x2, x3 = _split3(x)
    return _dot(m_bf16, x1) + _dot(m_bf16, x2) + _dot(m_bf16, x3)


def _exact_rhs_dot(x, m_bf16):
    x1, x2, x3 = _split3(x)
    return _dot(x1, m_bf16) + _dot(x2, m_bf16) + _dot(x3, m_bf16)


def _norm_modulate(x, g, sh, sc):
    ms = jnp.mean(x * x, axis=-1, keepdims=True)
    return x * lax.rsqrt(ms + EPS) * (g * (1.0 + sc)) + sh


def _ada_kernel(c_ref, w_ref, b_ref, o_ref):
    c = c_ref[...]
    ca = _silu(c)
    c1, c2, c3 = _split3(ca)
    w = w_ref[...]
    w1 = w.astype(BF16)
    w2 = (w - w1.astype(F32)).astype(BF16)
    acc = _dot(c1, w1) + _dot(c1, w2) + _dot(c2, w1) + _dot(c2, w2) + _dot(c3, w1)
    o_ref[...] = acc + b_ref[...]


def _ada_call(c, ada_w, ada_b):
    depth, d, n = ada_w.shape
    b = c.shape[0]
    tn = min(ADA_COLS, n)
    assert n % tn == 0
    return pl.pallas_call(
        _ada_kernel,
        grid=(depth, n // tn),
        in_specs=[
            pl.BlockSpec((b, d), lambda l, j: (0, 0)),
            pl.BlockSpec((None, d, tn), lambda l, j: (l, 0, j)),
            pl.BlockSpec((None, 1, tn), lambda l, j: (l, 0, j)),
        ],
        out_specs=pl.BlockSpec((None, b, tn), lambda l, j: (l, 0, j)),
        out_shape=jax.ShapeDtypeStruct((depth, b, n), F32),
        compiler_params=_params("arbitrary", "arbitrary"),
        name="ada_mod",
    )(c, ada_w, ada_b.reshape(depth, 1, n))


def _inproj_kernel(x_ref, sh_ref, sc_ref, g_ref, w_ref, wg_ref, gain_ref, z_ref, gate_ref,
                   *, n_norm, chunk):
    half = x_ref.shape[0] // 2
    h_rows, first_rows = [], []
    for r0 in (0, half):
        h_part = _norm_modulate(x_ref[r0:r0 + half, :], g_ref[...], sh_ref[...], sc_ref[...]).astype(BF16)
        first_rows.append(_dot(h_part, w_ref[:, 0:chunk]))
        h_rows.append(h_part)
    h = jnp.concatenate(h_rows, axis=0)
    n = w_ref.shape[1]
    lane = lax.broadcasted_iota(jnp.int32, (1, LANES), 1)
    lo = lane < HEAD_DIM
    for c0 in range(0, n, chunk):
        zc = jnp.concatenate(first_rows, axis=0) if c0 == 0 else _dot(h, w_ref[:, c0:c0 + chunk])
        if c0 < n_norm:
            parts = []
            for s0 in range(0, chunk, LANES):
                zs = zc[:, s0:s0 + LANES]
                sq = zs * zs
                s_lo = jnp.sum(jnp.where(lo, sq, 0.0), axis=-1, keepdims=True)
                s_hi = jnp.sum(jnp.where(lo, 0.0, sq), axis=-1, keepdims=True)
                r_lo = lax.rsqrt(s_lo * (1.0 / HEAD_DIM) + EPS)
                r_hi = lax.rsqrt(s_hi * (1.0 / HEAD_DIM) + EPS)
                parts.append(zs * jnp.where(lo, r_lo, r_hi))
            zc = jnp.concatenate(parts, axis=-1) * gain_ref[:, c0:c0 + chunk]
        z_ref[:, c0:c0 + chunk] = zc.astype(BF16)
    gate_ref[...] = _dot(h, wg_ref[...])


def _inproj_call(x, shift, scale, g, w_all, layer, n, wg, gain, n_norm, name):
    b, s, d = x.shape
    tm = min(ROW_TILE, s)
    chunk = INPROJ_CHUNK
    assert s % tm == 0 and tm % 2 == 0 and n % chunk == 0 and n_norm % chunk == 0
    kern = functools.partial(_inproj_kernel, n_norm=n_norm, chunk=chunk)
    return pl.pallas_call(
        kern,
        grid=(b, s // tm),
        in_specs=[
            pl.BlockSpec((None, tm, d), lambda i, j: (i, j, 0)),
            pl.BlockSpec((None, 1, d), lambda i, j: (i, 0, 0)),
            pl.BlockSpec((None, 1, d), lambda i, j: (i, 0, 0)),
            pl.BlockSpec((1, d), lambda i, j: (0, 0)),
            pl.BlockSpec((None, d, n), lambda i, j: (layer, 0, 0)),
            pl.BlockSpec((d, LANES), lambda i, j: (0, 0)),
            pl.BlockSpec((1, n), lambda i, j: (0, 0)),
        ],
        out_specs=[
            pl.BlockSpec((None, tm, n), lambda i, j: (i, j, 0)),
            pl.BlockSpec((None, tm, LANES), lambda i, j: (i, j, 0)),
        ],
        out_shape=[
            jax.ShapeDtypeStruct((b, s, n), BF16),
            jax.ShapeDtypeStruct((b, s, LANES), F32),
        ],
        compiler_params=_params("arbitrary", "arbitrary"),
        name=name,
    )(x, shift, scale, g, w_all, wg, gain)


def _lane_ids():
    return lax.broadcasted_iota(jnp.int32, (1, LANES), 1)


def _place_pieces(pieces, first_lane):
    lane = _lane_ids()
    out = jnp.zeros(pieces[0].shape, F32)
    for j, p in enumerate(pieces):
        out = jnp.where(lane == first_lane + j, p.astype(F32), out)
    return out


def _pack_keys(blocks, shared_fn):
    s = blocks[0][0].shape[0]
    rows = min(PACK_ROWS, s)
    keep_lo = jnp.where(_lane_ids() < HEAD_DIM, 1.0, 0.0).astype(BF16)
    keep_hi = jnp.where(_lane_ids() < HEAD_DIM, 0.0, 1.0).astype(BF16)

    def body(c, carry):
        r0 = pl.multiple_of(c * rows, rows)
        shared = shared_fn(r0, rows)
        for k_ref, v_ref, ka_ref, kb_ref, vt_refs, extras_fn in blocks:
            k2 = k_ref[pl.ds(r0, rows), :]
            ea, eb = extras_fn(shared)
            ka_ref[pl.ds(r0, rows), :] = k2 * keep_lo + ea.astype(BF16)
            kb_ref[pl.ds(r0, rows), :] = k2 * keep_hi + eb.astype(BF16)
            vt = v_ref[pl.ds(r0, rows), :].T
            for ref, c0, width in vt_refs:
                ref[0:width, pl.ds(r0, rows)] = vt[c0:c0 + width, :]
                ref[width:width + ONES_ROWS, pl.ds(r0, rows)] = jnp.ones((ONES_ROWS, rows), BF16)
        return carry

    lax.fori_loop(0, s // rows, body, 0)


def _pack_queries(q2, factors):
    lane = _lane_ids()
    q2 = q2.astype(F32)
    lo = lane < HEAD_DIM
    fac_a = _place_pieces(factors, HEAD_DIM)
    fac_b = _place_pieces(factors, 0)
    return jnp.where(lo, q2, fac_a).astype(BF16), jnp.where(lo, fac_b, q2).astype(BF16)


def _init_stats(m_ref, acc_ref):
    m_ref[...] = jnp.full(m_ref.shape, NEG, F32)
    acc_ref[...] = jnp.zeros(acc_ref.shape, F32)


def _softmax_step(s, tile_max, vt, m_ref, acc_ref):
    m_old = m_ref[...]
    m_new = jnp.maximum(m_old, tile_max)
    alpha = jnp.exp2(m_old - m_new)
    p = jnp.exp2(s - m_new)
    acc_ref[...] = alpha * acc_ref[...] + _dot(vt, p.astype(BF16))
    m_ref[...] = m_new


def _attend(n, t, tiles, s_buf, mx_buf, make_addend, finish_pair):
    tk = ATTN_KEYS
    ring = len(tiles)
    count = len(tiles[0])

    def qk_one(u, step, idx, rows=tk):
        k0 = pl.multiple_of(step * tk, tk)
        q, k_ref = tiles[u][idx][0], tiles[u][idx][1]
        s = _dot_nt(k_ref[pl.ds(k0, rows), :], q)
        s_buf[u][idx, 0:rows, :] = s
        if rows == tk:
            mx_buf[u][idx] = jnp.max(s, axis=0, keepdims=True)

    def stage(u, step, qk_tile, qk_step, qk_rows=tk, last=False):
        k0 = pl.multiple_of(step * tk, tk)
        rows, offset = (u + 1) * t, u * t
        addends = {}
        for idx, (_, _, vt, add_key, stats) in enumerate(tiles[u]):
            if qk_tile is not None:
                qk_one(qk_tile, qk_step, idx, qk_rows)
            if not last:
                _softmax_step(s_buf[u][idx], mx_buf[u][idx], vt(k0, tk), *stats)
                continue
            if add_key not in addends:
                addends[add_key] = make_addend(add_key, addends.setdefault("shared", {}), rows, offset)
            s = s_buf[u][idx, 0:rows, :] + addends[add_key]
            _softmax_step(s, jnp.max(s, axis=0, keepdims=True), vt(k0, rows), *stats)
            if idx % 2 == 1:
                finish_pair(u, idx // 2)

    first = ring - 1
    for idx in range(count):
        qk_one(first, 0, idx)
        for streams in tiles:
            _init_stats(*streams[idx][4])

    def body(j, carry):
        for u in range(first, -1, -1):
            if u > 0:
                stage(u, j, u - 1, j)
            else:
                stage(u, j, first, j + 1)
        return carry

    lax.fori_loop(0, n - 1, body, 0)
    for u in range(first, -1, -1):
        if u > 0:
            stage(u, n - 1, u - 1, n - 1, qk_rows=u * t, last=True)
        else:
            stage(u, n - 1, None, None, last=True)


def _lane_block(ref, h):
    return ref.at[:, pl.ds(h * LANES, LANES)]


def _da_kernel(slope_ref, q_ref, k_ref, v_ref, lam_ref, sg_ref, o_ref,
               ka_ref, kb_ref, vt_ref, m_ref, acc_ref, s0_ref, s1_ref, mx0_ref, mx1_ref,
               *, t, hb, lambda_init):
    dv = LANES
    g = pl.program_id(1)
    i = pl.program_id(2)
    slopes2 = [slope_ref[g * hb + h] * LOG2E for h in range(hb)]

    @pl.when(i == 0)
    def _():
        def position_pieces(r0, rows):
            pos = lax.broadcasted_iota(jnp.int32, (rows, LANES), 0) + r0
            hi = jnp.right_shift(pos, CHUNK_SHIFT).astype(F32)
            lo = jnp.bitwise_and(pos, CHUNK - 1).astype(F32)
            lane = _lane_ids()

            def place(first):
                in_hi = (lane >= first) & (lane < first + BIAS_PIECES)
                in_lo = (lane >= first + BIAS_PIECES) & (lane < first + 2 * BIAS_PIECES)
                return jnp.where(in_hi, hi, jnp.where(in_lo, lo, 0.0)).astype(BF16)
            return place(HEAD_DIM), place(0)
        _pack_keys([(_lane_block(k_ref, h), _lane_block(v_ref, h), ka_ref.at[h], kb_ref.at[h],
                     [(vt_ref.at[h], 0, dv)], lambda shared: shared) for h in range(hb)],
                   position_pieces)

    def last_step_addend(h, shared, rows, offset):
        if "base" not in shared:
            key = lax.broadcasted_iota(jnp.int32, (rows, t), 0)
            qry = lax.broadcasted_iota(jnp.int32, (rows, t), 1) + offset
            allowed = jnp.right_shift(key, CHUNK_SHIFT) <= jnp.right_shift(qry, CHUNK_SHIFT)
            shared["base"] = jnp.where(allowed, jnp.maximum(key - qry, 0).astype(F32), -NEG)
        return (-2.0 * slopes2[h]) * shared["base"]

    ring = q_ref.shape[0] // t
    tiles = []
    for u in range(ring):
        streams = []
        for h in range(hb):
            slope_pieces = [p.astype(F32) for p in _split3(jnp.full((1, LANES), slopes2[h], F32))]
            factors = [p * float(CHUNK) for p in slope_pieces] + slope_pieces
            qa, qb = _pack_queries(q_ref[u * t:(u + 1) * t, h * LANES:(h + 1) * LANES], factors)
            vt = lambda k0, size, h=h: vt_ref[h, :, pl.ds(k0, size)]
            for m, (q, k_sc) in enumerate(((qa, ka_ref), (qb, kb_ref))):
                n_stream = (u * hb + h) * 2 + m
                streams.append((q, k_sc.at[h], vt, h, (m_ref.at[n_stream], acc_ref.at[n_stream])))
        tiles.append(streams)

    def finish_head(u, h):
        lp = lam_ref[...]
        lam = (jnp.exp(jnp.sum(lp[0:1] * lp[1:2], axis=-1, keepdims=True))
               - jnp.exp(jnp.sum(lp[2:3] * lp[3:4], axis=-1, keepdims=True)) + lambda_init)
        pa = acc_ref[(u * hb + h) * 2]
        pb = acc_ref[(u * hb + h) * 2 + 1]
        out = pa[0:dv] / pa[dv:dv + 1] - lam * (pb[0:dv] / pb[dv:dv + 1])
        ms = jnp.mean(out * out, axis=0, keepdims=True)
        out = out * lax.rsqrt(ms + EPS) * (sg_ref[...] * (1.0 - lambda_init))
        o_ref[u * t:(u + 1) * t, h * LANES:(h + 1) * LANES] = out.T.astype(BF16)

    _attend(i + 1, t, tiles, (s0_ref, s1_ref), (mx0_ref, mx1_ref), last_step_addend, finish_head)


def _da_call(z, slopes, lam_p, subln_g, heads, lambda_init):
    b, s, _ = z.shape
    t, tk = ATTN_TILE, ATTN_KEYS
    assert s % tk == 0 and t % CHUNK == 0 and tk == ATTN_RING * t
    assert s <= CHUNK * 256, "key position // CHUNK must stay exact in bf16"
    hb = min(ATTN_BLOCKS, heads)
    groups = heads // hb
    assert heads % hb == 0
    kern = functools.partial(_da_kernel, t=t, hb=hb, lambda_init=lambda_init)
    w = hb * LANES
    n_streams = 2 * hb * ATTN_RING
    return pl.pallas_call(
        kern,
        grid_spec=pltpu.PrefetchScalarGridSpec(
            num_scalar_prefetch=1,
            grid=(b, groups, s // tk),
            in_specs=[
                pl.BlockSpec((None, tk, w), lambda bi, g, i, sl: (bi, i, g)),
                pl.BlockSpec((None, s, w), lambda bi, g, i, sl: (bi, 0, groups + g)),
                pl.BlockSpec((None, s, w), lambda bi, g, i, sl: (bi, 0, 2 * groups + g)),
                pl.BlockSpec((4, HEAD_DIM), lambda bi, g, i, sl: (0, 0)),
                pl.BlockSpec((LANES, 1), lambda bi, g, i, sl: (0, 0)),
            ],
            out_specs=pl.BlockSpec((None, tk, w), lambda bi, g, i, sl: (bi, i, g)),
            scratch_shapes=[
                pltpu.VMEM((hb, s, LANES), BF16), pltpu.VMEM((hb, s, LANES), BF16),
                pltpu.VMEM((hb, LANES + ONES_ROWS, s), BF16),
                pltpu.VMEM((n_streams, 1, t), F32),
                pltpu.VMEM((n_streams, LANES + ONES_ROWS, t), F32),
                pltpu.VMEM((2 * hb, ATTN_KEYS, t), F32), pltpu.VMEM((2 * hb, ATTN_KEYS, t), F32),
                pltpu.VMEM((2 * hb, 1, t), F32), pltpu.VMEM((2 * hb, 1, t), F32),
            ],
        ),
        out_shape=jax.ShapeDtypeStruct((b, s, heads * LANES), BF16),
        compiler_params=_params("arbitrary", "arbitrary", "arbitrary"),
        name="diff_attention",
    )(slopes, z, z, z, lam_p, subln_g.reshape(LANES, 1))


def _fox_gate_kernel(g_ref, b_ref, o_ref, *, chunk):
    s = g_ref.shape[0]
    r = lax.broadcasted_iota(jnp.int32, (chunk, chunk), 0)
    c = lax.broadcasted_iota(jnp.int32, (chunk, chunk), 1)
    tri = (c <= r).astype(BF16)
    carry = jnp.zeros((1, LANES), F32)
    for c0 in range(0, s, chunk):
        ls = _log_sigmoid(g_ref[c0:c0 + chunk, :] + b_ref[...])
        cs = _exact_lhs_dot(tri, ls) + carry
        carry = cs[chunk - 1:chunk, :]
        o_ref[c0:c0 + chunk, :] = cs * (-LOG2E)


def _fox_gate_call(gates, bias_row):
    b, s, _ = gates.shape
    chunk = min(GATE_CHUNK, s)
    assert s % chunk == 0
    kern = functools.partial(_fox_gate_kernel, chunk=chunk)
    return pl.pallas_call(
        kern,
        grid=(b,),
        in_specs=[
            pl.BlockSpec((None, s, LANES), lambda i: (i, 0, 0)),
            pl.BlockSpec((1, LANES), lambda i: (0, 0)),
        ],
        out_specs=pl.BlockSpec((None, s, LANES), lambda i: (i, 0, 0)),
        out_shape=jax.ShapeDtypeStruct((b, s, LANES), F32),
        compiler_params=_params("arbitrary"),
        name="fox_gate_cumsum",
    )(gates, bias_row)


def _fox_kernel(q_ref, k_ref, v_ref, f_ref, o_ref,
                ka_ref, kb_ref, vt_ref, m_ref, acc_ref, s0_ref, s1_ref, mx0_ref, mx1_ref, *, t, hb):
    dv = HEAD_DIM
    g = pl.program_id(1)
    i = pl.program_id(2)

    @pl.when(i == 0)
    def _():
        n = BIAS_PIECES * LANES
        src = lax.broadcasted_iota(jnp.int32, (n, LANES), 0)
        dst = lax.broadcasted_iota(jnp.int32, (n, LANES), 1)
        piece = jnp.right_shift(src, LANES.bit_length() - 1)
        head = src - piece * LANES
        blocks = []
        for h in range(hb):
            pair = g * hb + h
            place_a = ((head == 2 * pair) & (dst == HEAD_DIM + piece)).astype(BF16)
            place_b = ((head == 2 * pair + 1) & (dst == piece)).astype(BF16)
            place = jnp.concatenate([place_a, place_b], axis=-1)

            def extras(pieces, place=place):
                both = _dot(pieces, place)
                return both[:, 0:LANES], both[:, LANES:2 * LANES]
            blocks.append((_lane_block(k_ref, h), _lane_block(v_ref, h), ka_ref.at[h], kb_ref.at[h],
                           [(vt_ref.at[2 * h + m], m * dv, dv) for m in range(2)], extras))
        _pack_keys(blocks,
                   lambda r0, rows: jnp.concatenate(_split3(f_ref[pl.ds(r0, rows), :]), axis=-1))

    def last_step_addend(_, shared, rows, offset):
        key = lax.broadcasted_iota(jnp.int32, (rows, t), 0)
        qry = lax.broadcasted_iota(jnp.int32, (rows, t), 1) + offset
        return jnp.where(key <= qry, 0.0, NEG)

    ring = q_ref.shape[0] // t
    ones = [jnp.ones((1, LANES), F32)] * BIAS_PIECES
    tiles = []
    for u in range(ring):
        streams = []
        for h in range(hb):
            qa, qb = _pack_queries(q_ref[u * t:(u + 1) * t, h * LANES:(h + 1) * LANES], ones)
            for m, (q, k_sc) in enumerate(((qa, ka_ref), (qb, kb_ref))):
                vt = lambda k0, size, n=2 * h + m: vt_ref[n, :, pl.ds(k0, size)]
                n_stream = (u * hb + h) * 2 + m
                streams.append((q, k_sc.at[h], vt, 0, (m_ref.at[n_stream], acc_ref.at[n_stream])))
        tiles.append(streams)

    def finish_pair(u, h):
        pa = acc_ref[(u * hb + h) * 2]
        pb = acc_ref[(u * hb + h) * 2 + 1]
        out = jnp.concatenate([pa[0:dv] / pa[dv:dv + 1], pb[0:dv] / pb[dv:dv + 1]], axis=0)
        o_ref[u * t:(u + 1) * t, h * LANES:(h + 1) * LANES] = out.T.astype(BF16)

    _attend(i + 1, t, tiles, (s0_ref, s1_ref), (mx0_ref, mx1_ref), last_step_addend, finish_pair)


def _fox_call(z, fneg, heads):
    b, s, _ = z.shape
    pairs = heads // 2
    t, tk = ATTN_TILE, ATTN_KEYS
    assert s % tk == 0 and tk == ATTN_RING * t
    hb = min(ATTN_BLOCKS, pairs)
    groups = pairs // hb
    assert pairs % hb == 0
    kern = functools.partial(_fox_kernel, t=t, hb=hb)
    w = hb * LANES
    n_streams = 2 * hb * ATTN_RING
    return pl.pallas_call(
        kern,
        grid=(b, groups, s // tk),
        in_specs=[
            pl.BlockSpec((None, tk, w), lambda bi, g, i: (bi, i, g)),
            pl.BlockSpec((None, s, w), lambda bi, g, i: (bi, 0, groups + g)),
            pl.BlockSpec((None, s, w), lambda bi, g, i: (bi, 0, 2 * groups + g)),
            pl.BlockSpec((None, s, LANES), lambda bi, g, i: (bi, 0, 0)),
        ],
        out_specs=pl.BlockSpec((None, tk, w), lambda bi, g, i: (bi, i, g)),
        out_shape=jax.ShapeDtypeStruct((b, s, pairs * LANES), BF16),
        scratch_shapes=[
            pltpu.VMEM((hb, s, LANES), BF16), pltpu.VMEM((hb, s, LANES), BF16),
            pltpu.VMEM((2 * hb, HEAD_DIM + ONES_ROWS, s), BF16),
            pltpu.VMEM((n_streams, 1, t), F32),
            pltpu.VMEM((n_streams, HEAD_DIM + ONES_ROWS, t), F32),
            pltpu.VMEM((2 * hb, ATTN_KEYS, t), F32), pltpu.VMEM((2 * hb, ATTN_KEYS, t), F32),
            pltpu.VMEM((2 * hb, 1, t), F32), pltpu.VMEM((2 * hb, 1, t), F32),
        ],
        compiler_params=_params("arbitrary", "arbitrary", "arbitrary"),
        name="forgetting_attention",
    )(z, z, z, fneg)


def _mlstm_kernel(q_ref, k_ref, v_ref, o_ref, g_ref, gb_ref, cw_ref, cb_ref, y_ref,
                  halo_ref, ct_ref, m_ref, *, heads, taps, lc):
    @pl.when(pl.program_id(1) == 0)
    def _():
        halo_ref[...] = jnp.zeros(halo_ref.shape, F32)
        ct_ref[...] = jnp.zeros(ct_ref.shape, F32)
        m_ref[...] = jnp.zeros(m_ref.shape, F32)

    rows = lambda ref, sub: ref.at[pl.ds(sub * lc, lc), :]
    edge = halo_ref[...]
    for sub in range(q_ref.shape[0] // lc):
        prep, edge = _mlstm_prep(rows(q_ref, sub), rows(k_ref, sub), rows(g_ref, sub), gb_ref,
                                 cw_ref, cb_ref, edge, heads, taps)
        for hd in range(heads):
            _mlstm_head(prep, rows(v_ref, sub), rows(o_ref, sub), rows(y_ref, sub),
                        ct_ref, m_ref, hd, heads)
    halo_ref[...] = edge


def _mlstm_prep(q_ref, k_ref, g_ref, gb_ref, cw_ref, cb_ref, halo, heads, taps):
    lc, width = q_ref.shape
    pad = halo.shape[0]

    r = lax.broadcasted_iota(jnp.int32, (lc, lc), 0)
    c = lax.broadcasted_iota(jnp.int32, (lc, lc), 1)
    causal = c <= r

    gcol = g_ref[...] + gb_ref[...]
    grow = gcol.T[:2 * heads, :]
    a_cols = _exact_lhs_dot(causal.astype(BF16), _log_sigmoid(gcol))
    a_rows = _exact_rhs_dot(_log_sigmoid(grow), (r <= c).astype(BF16))

    b_half = grow[0:heads, :] - a_rows[heads:2 * heads, :]
    b_rows = jnp.concatenate([b_half, b_half], axis=0)
    prefix_cols = _prefix_max_lanes(b_rows).T

    xb = jnp.concatenate([q_ref[...], k_ref[...]], axis=-1)
    xf = xb.astype(F32)
    conv = cb_ref[...] + xf * cw_ref[taps - 1:taps, :]
    for sft in range(1, taps):
        shifted = _dot((r - c == sft).astype(BF16), xb)
        conv = conv + shifted * cw_ref[taps - 1 - sft:taps - sft, :]
    edge = jnp.concatenate([halo, xf[0:pad, :]], axis=0)
    first = cb_ref[...]
    for j in range(taps):
        off = pad - (taps - 1) + j
        first = first + edge[off:off + pad, :] * cw_ref[j:j + 1, :]
    conv = jnp.concatenate([first, conv[pad:, :]], axis=0)
    qk = _silu(conv)
    prep = dict(qk=qk, gcol=gcol, a_cols=a_cols, b_rows=b_rows, causal=causal,
                prefix_cols=prefix_cols)
    return prep, xf[lc - pad:, :]


def _mlstm_head(prep, v_ref, o_ref, y_ref, ct_ref, m_ref, hd, heads):
    lc, width = v_ref.shape
    d = width // heads
    qk, gcol, a_cols, causal = prep["qk"], prep["gcol"], prep["a_cols"], prep["causal"]
    q = qk[:, hd * d:(hd + 1) * d]
    k = qk[:, width + hd * d:width + (hd + 1) * d] * (d ** -0.5)
    v = v_ref[:, hd * d:(hd + 1) * d]
    qb = q.astype(BF16)
    kb = k.astype(BF16)
    ig_c = gcol[:, hd:hd + 1]
    a_c = a_cols[:, heads + hd:heads + hd + 1]
    a_last = a_c[lc - 1:lc, :]
    ct = ct_ref[hd]
    m_prev = m_ref[hd:hd + 1, 0:1]
    x_c = jnp.maximum(prep["prefix_cols"][:, hd:hd + 1], m_prev)

    g_c = a_last - a_c + ig_c
    m_loc = jnp.max(g_c, axis=0, keepdims=True)
    w_c = jnp.exp(g_c - m_loc)
    m_new = jnp.maximum(a_last + m_prev, m_loc)
    decay = jnp.exp(a_last + m_prev - m_new)
    s_loc = jnp.exp(m_loc - m_new)
    wv = jnp.concatenate([w_c * v.astype(F32), jnp.broadcast_to(w_c, (lc, d))], axis=-1)
    increment = _dot_tn(kb, wv.astype(BF16))
    raw = _dot_nt(qb, kb)
    from_state = _dot(qb, ct.astype(BF16))

    decay_mat = jnp.exp2(jnp.where(causal, prep["b_rows"][hd:hd + 1, :] * LOG2E - x_c * LOG2E, NEG))
    sqk = (raw * decay_mat).astype(BF16)
    intra = _dot(sqk, jnp.concatenate([v, jnp.ones((lc, d), BF16)], axis=-1))
    inter = jnp.exp(m_prev - x_c) * from_state
    num = inter[:, 0:d] + intra[:, 0:d]
    den = inter[:, d:d + 1] + intra[:, d:d + 1]
    hval = num / jnp.maximum(jnp.abs(den), jnp.exp(-(a_c + x_c)))
    og = o_ref[:, hd * d:(hd + 1) * d].astype(F32)
    y_ref[:, hd * d:(hd + 1) * d] = (_sigmoid(og) * hval).astype(BF16)

    ct_ref[hd] = decay * ct + s_loc * increment
    m_ref[hd:hd + 1, :] = jnp.broadcast_to(m_new, (1, LANES))


def _mlstm_call(z, gates, gate_bias, conv_w, conv_b, heads, col0):
    b, s, _ = z.shape
    width = conv_w.shape[1] // 2
    lc = min(ML_CHUNK, s)
    rows = min(ML_CHUNKS_PER_STEP * lc, s)
    assert s % rows == 0 and rows % lc == 0 and col0 % width == 0 and width // heads == LANES
    assert 2 * heads == 8, "gate rows are handled as one 8-sublane tile"
    cb = col0 // width
    taps = conv_w.shape[0]
    kern = functools.partial(_mlstm_kernel, heads=heads, taps=taps, lc=lc)
    zspec = lambda off: pl.BlockSpec((None, rows, width), lambda bi, ci: (bi, ci, cb + off))
    return pl.pallas_call(
        kern,
        grid=(b, s // rows),
        in_specs=[
            zspec(0), zspec(1), zspec(2), zspec(3),
            pl.BlockSpec((None, rows, LANES), lambda bi, ci: (bi, ci, 0)),
            pl.BlockSpec((1, LANES), lambda bi, ci: (0, 0)),
            pl.BlockSpec((taps, 2 * width), lambda bi, ci: (0, 0)),
            pl.BlockSpec((1, 2 * width), lambda bi, ci: (0, 0)),
        ],
        out_specs=pl.BlockSpec((None, rows, width), lambda bi, ci: (bi, ci, 0)),
        out_shape=jax.ShapeDtypeStruct((b, s, width), BF16),
        scratch_shapes=[
            pltpu.VMEM((8, 2 * width), F32),
            pltpu.VMEM((heads, LANES, 2 * LANES), F32),
            pltpu.VMEM((2 * heads, LANES), F32),
        ],
        compiler_params=_params("arbitrary", "arbitrary"),
        name="mlstm",
    )(z, z, z, z, gates, gate_bias, conv_w, conv_b)


def _mix_ffn_kernel(*refs, n_in, chunk):
    y_refs = refs[:n_in]
    wo_refs = refs[n_in:2 * n_in]
    (x_ref, gm_ref, sh_ref, sc_ref, g_ref, gate_ref, w1_ref, w3_ref, w2_ref, o_ref) = refs[2 * n_in:]
    mix = _dot(y_refs[0][...], wo_refs[0][...])
    for y_ref, w_ref in zip(y_refs[1:], wo_refs[1:]):
        mix = mix + _dot(y_ref[...], w_ref[...])
    x = x_ref[...] + gm_ref[...] * mix
    h = _norm_modulate(x, g_ref[...], sh_ref[...], sc_ref[...]).astype(BF16)
    hidden = w1_ref.shape[1]

    acc = None
    for c0 in range(0, hidden, chunk):
        a = _dot(h, w1_ref[:, c0:c0 + chunk])
        bb = _dot(h, w3_ref[:, c0:c0 + chunk])
        u = (_silu(a) * bb).astype(BF16)
        part = _dot(u, w2_ref[c0:c0 + chunk, :])
        acc = part if acc is None else acc + part
    o_ref[...] = x + gate_ref[...] * acc


def _mix_ffn_call(ys, wo_all, wo_layer, x, gate_mix, shift, scale, g, gate, w1, w3, w2, layer, name):
    b, s, d = x.shape
    hidden = w1.shape[2]
    tm = min(ROW_TILE, s)
    assert s % tm == 0 and hidden % FFN_CHUNK == 0
    assert all(y.shape[2] == ys[0].shape[2] for y in ys)
    kern = functools.partial(_mix_ffn_kernel, n_in=len(ys), chunk=FFN_CHUNK)
    vec = pl.BlockSpec((None, 1, d), lambda i, j: (i, 0, 0))
    rows = lambda width: pl.BlockSpec((None, tm, width), lambda i, j: (i, j, 0))
    stacked = lambda w: pl.BlockSpec((None,) + w.shape[1:], lambda i, j: (layer, 0, 0))
    wo_specs = [pl.BlockSpec((None, y.shape[2], d), lambda i, j, n=n: (wo_layer, n, 0))
                for n, y in enumerate(ys)]
    return pl.pallas_call(
        kern,
        grid=(b, s // tm),
        in_specs=[rows(y.shape[2]) for y in ys] + wo_specs + [
            rows(d), vec, vec, vec,
            pl.BlockSpec((1, d), lambda i, j: (0, 0)),
            vec, stacked(w1), stacked(w3), stacked(w2),
        ],
        out_specs=rows(d),
        out_shape=jax.ShapeDtypeStruct((b, s, d), F32),
        compiler_params=_params("arbitrary", "arbitrary"),
        name=name,
    )(*ys, *([wo_all] * len(ys)), x, gate_mix, shift, scale, g, gate, w1, w3, w2)


def _pad_cols(w, n):
    return jnp.pad(w, ((0, 0), (0, n - w.shape[1])))


def _gain_row(q_g, k_g, width, total):
    reps = width // HEAD_DIM
    row = jnp.concatenate([jnp.tile(q_g.astype(F32) * (HEAD_DIM ** -0.5 * LOG2E), reps),
                           jnp.tile(k_g.astype(F32), reps),
                           jnp.ones((total - 2 * width,), F32)])
    return row.reshape(1, total)


def kernel(x, c, ada_w, ada_b, norm_mix_g, norm_ffn_g, ab_w_in, ml_b_i, ml_b_f, ml_conv_w, ml_conv_b,
           da_q_g, da_k_g, da_lambda, da_subln_g, ab_w_out, fx_w_in, fx_b_f, fx_q_g, fx_k_g, fx_w_out,
           ffn_w1, ffn_w3, ffn_w2):
    depth = ada_w.shape[0]
    b, s, d = x.shape
    ml_width = ml_conv_w.shape[2] // 2
    ml_heads = ml_b_i.shape[1]
    da_width = (ab_w_in.shape[2] - 4 * ml_width - 2 * ml_heads) // 3
    da_heads = da_width // (2 * HEAD_DIM)
    fx_heads = fx_b_f.shape[1]
    fx_width = fx_heads * HEAD_DIM
    ab_main = 3 * da_width + 4 * ml_width
    fx_main = 3 * fx_width

    mods = _ada_call(c, ada_w, ada_b).reshape(depth, b, 6, 1, d)
    slopes = 2.0 ** (-8.0 * jnp.arange(1, da_heads + 1, dtype=F32) / da_heads)

    ab_w_in_b, fx_w_in_b = ab_w_in.astype(BF16), fx_w_in.astype(BF16)
    ab_w_out_b, fx_w_out_b = ab_w_out.astype(BF16), fx_w_out.astype(BF16)
    ffn_w1_b, ffn_w3_b, ffn_w2_b = ffn_w1.astype(BF16), ffn_w3.astype(BF16), ffn_w2.astype(BF16)

    for l in range(depth):
        sh_m, sc_m, g_m, sh_f, sc_f, g_f = (mods[l, :, t] for t in range(6))
        j = l // 2
        g_mix = norm_mix_g[l].reshape(1, d)
        if l % 2 == 0:
            z, gates = _inproj_call(
                x, sh_m, sc_m, g_mix, ab_w_in_b, j, ab_main,
                _pad_cols(ab_w_in[j][:, ab_main:], LANES).astype(BF16),
                _gain_row(da_q_g[j], da_k_g[j], da_width, ab_main),
                2 * da_width, "inproj_even")
            lambda_init = 0.8 - 0.6 * math.exp(-0.3 * l)
            y_da = _da_call(z, slopes, da_lambda[j], da_subln_g[j], da_heads, lambda_init)
            gate_bias = _pad_cols(jnp.concatenate([ml_b_i[j], ml_b_f[j]]).reshape(1, -1), LANES)
            y_ml = _mlstm_call(z, gates, gate_bias, ml_conv_w[j], ml_conv_b[j].reshape(1, -1),
                               ml_heads, 3 * da_width)
            ys, wo_all, name = [y_da, y_ml], ab_w_out_b, "mix_ffn_even"
        else:
            z, gates = _inproj_call(
                x, sh_m, sc_m, g_mix, fx_w_in_b, j, fx_main,
                _pad_cols(fx_w_in[j][:, fx_main:], LANES).astype(BF16),
                _gain_row(fx_q_g[j], fx_k_g[j], fx_width, fx_main),
                2 * fx_width, "inproj_odd")
            fneg = _fox_gate_call(gates, _pad_cols(fx_b_f[j].reshape(1, -1), LANES))
            ys, wo_all, name = [_fox_call(z, fneg, fx_heads)], fx_w_out_b, "mix_ffn_odd"
        x = _mix_ffn_call(ys, wo_all, j, x, g_m, sh_f, sc_f, norm_ffn_g[l].reshape(1, d), g_f,
                          ffn_w1_b, ffn_w3_b, ffn_w2_b, l, name)
    return x
```

```python
import functools
import math

import jax
import jax.numpy as jnp
from jax import lax
from jax.experimental import pallas as pl
from jax.experimental.pallas import tpu as pltpu

F32 = jnp.float32
BF16 = jnp.bfloat16

EPS = 1e-6
LANES = 128
HEAD_DIM = 64
CHUNK = 64
CHUNK_SHIFT = CHUNK.bit_length() - 1
assert CHUNK == 1 << CHUNK_SHIFT
NEG = -1e30
LOG2E = math.log2(math.e)
BIAS_PIECES = 3
ONES_ROWS = 16

VMEM_LIMIT = 56 * 1024 * 1024

ADA_COLS = 1536
ROW_TILE = 512
INPROJ_CHUNK = 512
ATTN_TILE = 256
ATTN_RING = 2
ATTN_KEYS = ATTN_RING * ATTN_TILE
ATTN_BLOCKS = 4
PACK_ROWS = 512
ML_CHUNK = 256
ML_CHUNKS_PER_STEP = 2
GATE_CHUNK = 256
FFN_CHUNK = 256


def _params(*sem):
    return pltpu.CompilerParams(dimension_semantics=sem, vmem_limit_bytes=VMEM_LIMIT)


def _log_sigmoid(x):
    return jnp.minimum(x, 0.0) - jnp.log1p(jnp.exp(-jnp.abs(x)))


def _sigmoid(x):
    return 0.5 * jnp.tanh(0.5 * x) + 0.5


def _silu(x):
    h = 0.5 * x
    return h * (jnp.tanh(h) + 1.0)


def _prefix_max_lanes(x):
    axis = x.ndim - 1
    lane = lax.broadcasted_iota(jnp.int32, x.shape, axis)
    shift = 1
    while shift < x.shape[axis]:
        x = jnp.maximum(x, jnp.where(lane >= shift, pltpu.roll(x, shift, axis=axis), NEG))
        shift *= 2
    return x


def _split3(x):
    x1 = x.astype(BF16)
    r1 = x - x1.astype(F32)
    x2 = r1.astype(BF16)
    x3 = (r1 - x2.astype(F32)).astype(BF16)
    return x1, x2, x3


def _dot(a, b):
    return jnp.dot(a, b, preferred_element_type=F32)


def _dot_nt(a, b):
    return lax.dot_general(a, b, (((1,), (1,)), ((), ())), preferred_element_type=F32)


def _dot_tn(a, b):
    return lax.dot_general(a, b, (((0,), (0,)), ((), ())), preferred_element_type=F32)


def _exact_lhs_dot(m_bf16, x):
    x1, x2, x3 = _split3(x)
    return _dot(m_bf16, x1) + _dot(m_bf16, x2) + _dot(m_bf16, x3)


def _exact_rhs_dot(x, m_bf16):
    x1, x2, x3 = _split3(x)
    return _dot(x1, m_bf16) + _dot(x2, m_bf16) + _dot(x3, m_bf16)


def _norm_modulate(x, g, sh, sc):
    ms = jnp.mean(x * x, axis=-1, keepdims=True)
    return x * lax.rsqrt(ms + EPS) * (g * (1.0 + sc)) + sh


def _ada_kernel(c_ref, w_ref, b_ref, o_ref):
    c = c_ref[...]
    ca = _silu(c)
    c1, c2, c3 = _split3(ca)
    w = w_ref[...]
    w1 = w.astype(BF16)
    w2 = (w - w1.astype(F32)).astype(BF16)
    acc = _dot(c1, w1) + _dot(c1, w2) + _dot(c2, w1) + _dot(c2, w2) + _dot(c3, w1)
    o_ref[...] = acc + b_ref[...]


def _ada_call(c, ada_w, ada_b):
    depth, d, n = ada_w.shape
    b = c.shape[0]
    tn = min(ADA_COLS, n)
    assert n % tn == 0
    return pl.pallas_call(
        _ada_kernel,
        grid=(depth, n // tn),
        in_specs=[
            pl.BlockSpec((b, d), lambda l, j: (0, 0)),
            pl.BlockSpec((None, d, tn), lambda l, j: (l, 0, j)),
            pl.BlockSpec((None, 1, tn), lambda l, j: (l, 0, j)),
        ],
        out_specs=pl.BlockSpec((None, b, tn), lambda l, j: (l, 0, j)),
        out_shape=jax.ShapeDtypeStruct((depth, b, n), F32),
        compiler_params=_params("arbitrary", "arbitrary"),
        name="ada_mod",
    )(c, ada_w, ada_b.reshape(depth, 1, n))


def _inproj_kernel(x_ref, sh_ref, sc_ref, g_ref, w_ref, wg_ref, gain_ref, z_ref, gate_ref,
                   *, n_norm, chunk):
    half = x_ref.shape[0] // 2
    h_rows, first_rows = [], []
    for r0 in (0, half):
        h_part = _norm_modulate(x_ref[r0:r0 + half, :], g_ref[...], sh_ref[...], sc_ref[...]).astype(BF16)
        first_rows.append(_dot(h_part, w_ref[:, 0:chunk]))
        h_rows.append(h_part)
    h = jnp.concatenate(h_rows, axis=0)
    n = w_ref.shape[1]
    lane = lax.broadcasted_iota(jnp.int32, (1, LANES), 1)
    lo = lane < HEAD_DIM
    for c0 in range(0, n, chunk):
        zc = jnp.concatenate(first_rows, axis=0) if c0 == 0 else _dot(h, w_ref[:, c0:c0 + chunk])
        if c0 < n_norm:
            parts = []
            for s0 in range(0, chunk, LANES):
                zs = zc[:, s0:s0 + LANES]
                sq = zs * zs
                s_lo = jnp.sum(jnp.where(lo, sq, 0.0), axis=-1, keepdims=True)
                s_hi = jnp.sum(jnp.where(lo, 0.0, sq), axis=-1, keepdims=True)
                r_lo = lax.rsqrt(s_lo * (1.0 / HEAD_DIM) + EPS)
                r_hi = lax.rsqrt(s_hi * (1.0 / HEAD_DIM) + EPS)
                parts.append(zs * jnp.where(lo, r_lo, r_hi))
            zc = jnp.concatenate(parts, axis=-1) * gain_ref[:, c0:c0 + chunk]
        z_ref[:, c0:c0 + chunk] = zc.astype(BF16)
    gate_ref[...] = _dot(h, wg_ref[...])


def _inproj_call(x, shift, scale, g, w_all, layer, n, wg, gain, n_norm, name):
    b, s, d = x.shape
    tm = min(ROW_TILE, s)
    chunk = INPROJ_CHUNK
    assert s % tm == 0 and tm % 2 == 0 and n % chunk == 0 and n_norm % chunk == 0
    kern = functools.partial(_inproj_kernel, n_norm=n_norm, chunk=chunk)
    return pl.pallas_call(
        kern,
        grid=(b, s // tm),
        in_specs=[
            pl.BlockSpec((None, tm, d), lambda i, j: (i, j, 0)),
            pl.BlockSpec((None, 1, d), lambda i, j: (i, 0, 0)),
            pl.BlockSpec((None, 1, d), lambda i, j: (i, 0, 0)),
            pl.BlockSpec((1, d), lambda i, j: (0, 0)),
            pl.BlockSpec((None, d, n), lambda i, j: (layer, 0, 0)),
            pl.BlockSpec((d, LANES), lambda i, j: (0, 0)),
            pl.BlockSpec((1, n), lambda i, j: (0, 0)),
        ],
        out_specs=[
            pl.BlockSpec((None, tm, n), lambda i, j: (i, j, 0)),
            pl.BlockSpec((None, tm, LANES), lambda i, j: (i, j, 0)),
        ],
        out_shape=[
            jax.ShapeDtypeStruct((b, s, n), BF16),
            jax.ShapeDtypeStruct((b, s, LANES), F32),
        ],
        compiler_params=_params("arbitrary", "arbitrary"),
        name=name,
    )(x, shift, scale, g, w_all, wg, gain)


def _lane_ids():
    return lax.broadcasted_iota(jnp.int32, (1, LANES), 1)


def _place_pieces(pieces, first_lane):
    lane = _lane_ids()
    out = jnp.zeros(pieces[0].shape, F32)
    for j, p in enumerate(pieces):
        out = jnp.where(lane == first_lane + j, p.astype(F32), out)
    return out


def _pack_keys(blocks, shared_fn):
    s = blocks[0][0].shape[0]
    rows = min(PACK_ROWS, s)
    keep_lo = jnp.where(_lane_ids() < HEAD_DIM, 1.0, 0.0).astype(BF16)
    keep_hi = jnp.where(_lane_ids() < HEAD_DIM, 0.0, 1.0).astype(BF16)

    def body(c, carry):
        r0 = pl.multiple_of(c * rows, rows)
        shared = shared_fn(r0, rows)
        for k_ref, v_ref, ka_ref, kb_ref, vt_refs, extras_fn in blocks:
            k2 = k_ref[pl.ds(r0, rows), :]
            ea, eb = extras_fn(shared)
            ka_ref[pl.ds(r0, rows), :] = k2 * keep_lo + ea.astype(BF16)
            kb_ref[pl.ds(r0, rows), :] = k2 * keep_hi + eb.astype(BF16)
            vt = v_ref[pl.ds(r0, rows), :].T
            for ref, c0, width in vt_refs:
                ref[0:width, pl.ds(r0, rows)] = vt[c0:c0 + width, :]
                ref[width:width + ONES_ROWS, pl.ds(r0, rows)] = jnp.ones((ONES_ROWS, rows), BF16)
        return carry

    lax.fori_loop(0, s // rows, body, 0)


def _pack_queries(q2, factors):
    lane = _lane_ids()
    q2 = q2.astype(F32)
    lo = lane < HEAD_DIM
    fac_a = _place_pieces(factors, HEAD_DIM)
    fac_b = _place_pieces(factors, 0)
    return jnp.where(lo, q2, fac_a).astype(BF16), jnp.where(lo, fac_b, q2).astype(BF16)


def _init_stats(m_ref, acc_ref):
    m_ref[...] = jnp.full(m_ref.shape, NEG, F32)
    acc_ref[...] = jnp.zeros(acc_ref.shape, F32)


def _softmax_step(s, tile_max, vt, m_ref, acc_ref):
    m_old = m_ref[...]
    m_new = jnp.maximum(m_old, tile_max)
    alpha = jnp.exp2(m_old - m_new)
    p = jnp.exp2(s - m_new)
    acc_ref[...] = alpha * acc_ref[...] + _dot(vt, p.astype(BF16))
    m_ref[...] = m_new


def _attend(n, t, tiles, s_buf, mx_buf, make_addend, finish_pair):
    tk = ATTN_KEYS
    ring = len(tiles)
    count = len(tiles[0])

    def qk_one(u, step, idx, rows=tk):
        k0 = pl.multiple_of(step * tk, tk)
        q, k_ref = tiles[u][idx][0], tiles[u][idx][1]
        s = _dot_nt(k_ref[pl.ds(k0, rows), :], q)
        s_buf[u][idx, 0:rows, :] = s
        if rows == tk:
            mx_buf[u][idx] = jnp.max(s, axis=0, keepdims=True)

    def stage(u, step, qk_tile, qk_step, qk_rows=tk, last=False):
        k0 = pl.multiple_of(step * tk, tk)
        rows, offset = (u + 1) * t, u * t
        addends = {}
        for idx, (_, _, vt, add_key, stats) in enumerate(tiles[u]):
            if qk_tile is not None:
                qk_one(qk_tile, qk_step, idx, qk_rows)
            if not last:
                _softmax_step(s_buf[u][idx], mx_buf[u][idx], vt(k0, tk), *stats)
                continue
            if add_key not in addends:
                addends[add_key] = make_addend(add_key, addends.setdefault("shared", {}), rows, offset)
            s = s_buf[u][idx, 0:rows, :] + addends[add_key]
            _softmax_step(s, jnp.max(s, axis=0, keepdims=True), vt(k0, rows), *stats)
            if idx % 2 == 1:
                finish_pair(u, idx // 2)

    first = ring - 1
    for idx in range(count):
        qk_one(first, 0, idx)
        for streams in tiles:
            _init_stats(*streams[idx][4])

    def ring_pass(j):
        for u in range(first, -1, -1):
            if u > 0:
                stage(u, j, u - 1, j)
            else:
                stage(u, j, first, j + 1)

    def body(p, carry):
        ring_pass(2 * p)
        ring_pass(2 * p + 1)
        return carry

    doubles = (n - 1) // 2
    lax.fori_loop(0, doubles, body, 0)

    @pl.when(n - 1 > 2 * doubles)
    def _():
        ring_pass(n - 2)
    for u in range(first, -1, -1):
        if u > 0:
            stage(u, n - 1, u - 1, n - 1, qk_rows=u * t, last=True)
        else:
            stage(u, n - 1, None, None, last=True)


def _lane_block(ref, h):
    return ref.at[:, pl.ds(h * LANES, LANES)]


def _da_kernel(slope_ref, q_ref, k_ref, v_ref, lam_ref, sg_ref, o_ref,
               ka_ref, kb_ref, vt_ref, m_ref, acc_ref, s0_ref, s1_ref, mx0_ref, mx1_ref,
               *, t, hb, lambda_init):
    dv = LANES
    g = pl.program_id(1)
    i = pl.program_id(2)
    slopes2 = [slope_ref[g * hb + h] * LOG2E for h in range(hb)]

    @pl.when(i == 0)
    def _():
        def position_pieces(r0, rows):
            pos = lax.broadcasted_iota(jnp.int32, (rows, LANES), 0) + r0
            hi = jnp.right_shift(pos, CHUNK_SHIFT).astype(F32)
            lo = jnp.bitwise_and(pos, CHUNK - 1).astype(F32)
            lane = _lane_ids()

            def place(first):
                in_hi = (lane >= first) & (lane < first + BIAS_PIECES)
                in_lo = (lane >= first + BIAS_PIECES) & (lane < first + 2 * BIAS_PIECES)
                return jnp.where(in_hi, hi, jnp.where(in_lo, lo, 0.0)).astype(BF16)
            return place(HEAD_DIM), place(0)
        _pack_keys([(_lane_block(k_ref, h), _lane_block(v_ref, h), ka_ref.at[h], kb_ref.at[h],
                     [(vt_ref.at[h], 0, dv)], lambda shared: shared) for h in range(hb)],
                   position_pieces)

    def last_step_addend(h, shared, rows, offset):
        if "base" not in shared:
            key = lax.broadcasted_iota(jnp.int32, (rows, t), 0)
            qry = lax.broadcasted_iota(jnp.int32, (rows, t), 1) + offset
            allowed = jnp.right_shift(key, CHUNK_SHIFT) <= jnp.right_shift(qry, CHUNK_SHIFT)
            shared["base"] = jnp.where(allowed, jnp.maximum(key - qry, 0).astype(F32), -NEG)
        return (-2.0 * slopes2[h]) * shared["base"]

    ring = q_ref.shape[0] // t
    tiles = []
    for u in range(ring):
        streams = []
        for h in range(hb):
            slope_pieces = [p.astype(F32) for p in _split3(jnp.full((1, LANES), slopes2[h], F32))]
            factors = [p * float(CHUNK) for p in slope_pieces] + slope_pieces
            qa, qb = _pack_queries(q_ref[u * t:(u + 1) * t, h * LANES:(h + 1) * LANES], factors)
            vt = lambda k0, size, h=h: vt_ref[h, :, pl.ds(k0, size)]
            for m, (q, k_sc) in enumerate(((qa, ka_ref), (qb, kb_ref))):
                n_stream = (u * hb + h) * 2 + m
                streams.append((q, k_sc.at[h], vt, h, (m_ref.at[n_stream], acc_ref.at[n_stream])))
        tiles.append(streams)

    def finish_head(u, h):
        lp = lam_ref[...]
        lam = (jnp.exp(jnp.sum(lp[0:1] * lp[1:2], axis=-1, keepdims=True))
               - jnp.exp(jnp.sum(lp[2:3] * lp[3:4], axis=-1, keepdims=True)) + lambda_init)
        pa = acc_ref[(u * hb + h) * 2]
        pb = acc_ref[(u * hb + h) * 2 + 1]
        out = pa[0:dv] / pa[dv:dv + 1] - lam * (pb[0:dv] / pb[dv:dv + 1])
        ms = jnp.mean(out * out, axis=0, keepdims=True)
        out = out * lax.rsqrt(ms + EPS) * (sg_ref[...] * (1.0 - lambda_init))
        o_ref[u * t:(u + 1) * t, h * LANES:(h + 1) * LANES] = out.T.astype(BF16)

    _attend(i + 1, t, tiles, (s0_ref, s1_ref), (mx0_ref, mx1_ref), last_step_addend, finish_head)


def _da_call(z, slopes, lam_p, subln_g, heads, lambda_init):
    b, s, _ = z.shape
    t, tk = ATTN_TILE, ATTN_KEYS
    assert s % tk == 0 and t % CHUNK == 0 and tk == ATTN_RING * t
    assert s <= CHUNK * 256, "key position // CHUNK must stay exact in bf16"
    hb = min(ATTN_BLOCKS, heads)
    groups = heads // hb
    assert heads % hb == 0
    kern = functools.partial(_da_kernel, t=t, hb=hb, lambda_init=lambda_init)
    w = hb * LANES
    n_streams = 2 * hb * ATTN_RING
    return pl.pallas_call(
        kern,
        grid_spec=pltpu.PrefetchScalarGridSpec(
            num_scalar_prefetch=1,
            grid=(b, groups, s // tk),
            in_specs=[
                pl.BlockSpec((None, tk, w), lambda bi, g, i, sl: (bi, i, g)),
                pl.BlockSpec((None, s, w), lambda bi, g, i, sl: (bi, 0, groups + g)),
                pl.BlockSpec((None, s, w), lambda bi, g, i, sl: (bi, 0, 2 * groups + g)),
                pl.BlockSpec((4, HEAD_DIM), lambda bi, g, i, sl: (0, 0)),
                pl.BlockSpec((LANES, 1), lambda bi, g, i, sl: (0, 0)),
            ],
            out_specs=pl.BlockSpec((None, tk, w), lambda bi, g, i, sl: (bi, i, g)),
            scratch_shapes=[
                pltpu.VMEM((hb, s, LANES), BF16), pltpu.VMEM((hb, s, LANES), BF16),
                pltpu.VMEM((hb, LANES + ONES_ROWS, s), BF16),
                pltpu.VMEM((n_streams, 1, t), F32),
                pltpu.VMEM((n_streams, LANES + ONES_ROWS, t), F32),
                pltpu.VMEM((2 * hb, ATTN_KEYS, t), F32), pltpu.VMEM((2 * hb, ATTN_KEYS, t), F32),
                pltpu.VMEM((2 * hb, 1, t), F32), pltpu.VMEM((2 * hb, 1, t), F32),
            ],
        ),
        out_shape=jax.ShapeDtypeStruct((b, s, heads * LANES), BF16),
        compiler_params=_params("arbitrary", "arbitrary", "arbitrary"),
        name="diff_attention",
    )(slopes, z, z, z, lam_p, subln_g.reshape(LANES, 1))


def _fox_gate_kernel(g_ref, b_ref, o_ref, *, chunk):
    s = g_ref.shape[0]
    r = lax.broadcasted_iota(jnp.int32, (chunk, chunk), 0)
    c = lax.broadcasted_iota(jnp.int32, (chunk, chunk), 1)
    tri = (c <= r).astype(BF16)
    carry = jnp.zeros((1, LANES), F32)
    for c0 in range(0, s, chunk):
        ls = _log_sigmoid(g_ref[c0:c0 + chunk, :] + b_ref[...])
        cs = _exact_lhs_dot(tri, ls) + carry
        carry = cs[chunk - 1:chunk, :]
        o_ref[c0:c0 + chunk, :] = cs * (-LOG2E)


def _fox_gate_call(gates, bias_row):
    b, s, _ = gates.shape
    chunk = min(GATE_CHUNK, s)
    assert s % chunk == 0
    kern = functools.partial(_fox_gate_kernel, chunk=chunk)
    return pl.pallas_call(
        kern,
        grid=(b,),
        in_specs=[
            pl.BlockSpec((None, s, LANES), lambda i: (i, 0, 0)),
            pl.BlockSpec((1, LANES), lambda i: (0, 0)),
        ],
        out_specs=pl.BlockSpec((None, s, LANES), lambda i: (i, 0, 0)),
        out_shape=jax.ShapeDtypeStruct((b, s, LANES), F32),
        compiler_params=_params("arbitrary"),
        name="fox_gate_cumsum",
    )(gates, bias_row)


def _fox_kernel(q_ref, k_ref, v_ref, f_ref, o_ref,
                ka_ref, kb_ref, vt_ref, m_ref, acc_ref, s0_ref, s1_ref, mx0_ref, mx1_ref, *, t, hb):
    dv = HEAD_DIM
    g = pl.program_id(1)
    i = pl.program_id(2)

    @pl.when(i == 0)
    def _():
        n = BIAS_PIECES * LANES
        src = lax.broadcasted_iota(jnp.int32, (n, LANES), 0)
        dst = lax.broadcasted_iota(jnp.int32, (n, LANES), 1)
        piece = jnp.right_shift(src, LANES.bit_length() - 1)
        head = src - piece * LANES
        blocks = []
        for h in range(hb):
            pair = g * hb + h
            place_a = ((head == 2 * pair) & (dst == HEAD_DIM + piece)).astype(BF16)
            place_b = ((head == 2 * pair + 1) & (dst == piece)).astype(BF16)
            place = jnp.concatenate([place_a, place_b], axis=-1)

            def extras(pieces, place=place):
                both = _dot(pieces, place)
                return both[:, 0:LANES], both[:, LANES:2 * LANES]
            blocks.append((_lane_block(k_ref, h), _lane_block(v_ref, h), ka_ref.at[h], kb_ref.at[h],
                           [(vt_ref.at[2 * h + m], m * dv, dv) for m in range(2)], extras))
        _pack_keys(blocks,
                   lambda r0, rows: jnp.concatenate(_split3(f_ref[pl.ds(r0, rows), :]), axis=-1))

    def last_step_addend(_, shared, rows, offset):
        key = lax.broadcasted_iota(jnp.int32, (rows, t), 0)
        qry = lax.broadcasted_iota(jnp.int32, (rows, t), 1) + offset
        return jnp.where(key <= qry, 0.0, NEG)

    ring = q_ref.shape[0] // t
    ones = [jnp.ones((1, LANES), F32)] * BIAS_PIECES
    tiles = []
    for u in range(ring):
        streams = []
        for h in range(hb):
            qa, qb = _pack_queries(q_ref[u * t:(u + 1) * t, h * LANES:(h + 1) * LANES], ones)
            for m, (q, k_sc) in enumerate(((qa, ka_ref), (qb, kb_ref))):
                vt = lambda k0, size, n=2 * h + m: vt_ref[n, :, pl.ds(k0, size)]
                n_stream = (u * hb + h) * 2 + m
                streams.append((q, k_sc.at[h], vt, 0, (m_ref.at[n_stream], acc_ref.at[n_stream])))
        tiles.append(streams)

    def finish_pair(u, h):
        pa = acc_ref[(u * hb + h) * 2]
        pb = acc_ref[(u * hb + h) * 2 + 1]
        out = jnp.concatenate([pa[0:dv] / pa[dv:dv + 1], pb[0:dv] / pb[dv:dv + 1]], axis=0)
        o_ref[u * t:(u + 1) * t, h * LANES:(h + 1) * LANES] = out.T.astype(BF16)

    _attend(i + 1, t, tiles, (s0_ref, s1_ref), (mx0_ref, mx1_ref), last_step_addend, finish_pair)


def _fox_call(z, fneg, heads):
    b, s, _ = z.shape
    pairs = heads // 2
    t, tk = ATTN_TILE, ATTN_KEYS
    assert s % tk == 0 and tk == ATTN_RING * t
    hb = min(ATTN_BLOCKS, pairs)
    groups = pairs // hb
    assert pairs % hb == 0
    kern = functools.partial(_fox_kernel, t=t, hb=hb)
    w = hb * LANES
    n_streams = 2 * hb * ATTN_RING
    return pl.pallas_call(
        kern,
        grid=(b, groups, s // tk),
        in_specs=[
            pl.BlockSpec((None, tk, w), lambda bi, g, i: (bi, i, g)),
            pl.BlockSpec((None, s, w), lambda bi, g, i: (bi, 0, groups + g)),
            pl.BlockSpec((None, s, w), lambda bi, g, i: (bi, 0, 2 * groups + g)),
            pl.BlockSpec((None, s, LANES), lambda bi, g, i: (bi, 0, 0)),
        ],
        out_specs=pl.BlockSpec((None, tk, w), lambda bi, g, i: (bi, i, g)),
        out_shape=jax.ShapeDtypeStruct((b, s, pairs * LANES), BF16),
        scratch_shapes=[
            pltpu.VMEM((hb, s, LANES), BF16), pltpu.VMEM((hb, s, LANES), BF16),
            pltpu.VMEM((2 * hb, HEAD_DIM + ONES_ROWS, s), BF16),
            pltpu.VMEM((n_streams, 1, t), F32),
            pltpu.VMEM((n_streams, HEAD_DIM + ONES_ROWS, t), F32),
            pltpu.VMEM((2 * hb, ATTN_KEYS, t), F32), pltpu.VMEM((2 * hb, ATTN_KEYS, t), F32),
            pltpu.VMEM((2 * hb, 1, t), F32), pltpu.VMEM((2 * hb, 1, t), F32),
        ],
        compiler_params=_params("arbitrary", "arbitrary", "arbitrary"),
        name="forgetting_attention",
    )(z, z, z, fneg)


def _mlstm_kernel(q_ref, k_ref, v_ref, o_ref, g_ref, gb_ref, cw_ref, cb_ref, y_ref,
                  halo_ref, ct_ref, m_ref, *, heads, taps, lc):
    @pl.when(pl.program_id(1) == 0)
    def _():
        halo_ref[...] = jnp.zeros(halo_ref.shape, F32)
        ct_ref[...] = jnp.zeros(ct_ref.shape, F32)
        m_ref[...] = jnp.zeros(m_ref.shape, F32)

    rows = lambda ref, sub: ref.at[pl.ds(sub * lc, lc), :]
    edge = halo_ref[...]
    for sub in range(q_ref.shape[0] // lc):
        prep, edge = _mlstm_prep(rows(q_ref, sub), rows(k_ref, sub), rows(g_ref, sub), gb_ref,
                                 cw_ref, cb_ref, edge, heads, taps)
        for hd in range(heads):
            _mlstm_head(prep, rows(v_ref, sub), rows(o_ref, sub), rows(y_ref, sub),
                        ct_ref, m_ref, hd, heads)
    halo_ref[...] = edge


def _mlstm_prep(q_ref, k_ref, g_ref, gb_ref, cw_ref, cb_ref, halo, heads, taps):
    lc, width = q_ref.shape
    pad = halo.shape[0]

    r = lax.broadcasted_iota(jnp.int32, (lc, lc), 0)
    c = lax.broadcasted_iota(jnp.int32, (lc, lc), 1)
    causal = c <= r

    gcol = g_ref[...] + gb_ref[...]
    grow = gcol.T[:2 * heads, :]
    a_cols = _exact_lhs_dot(causal.astype(BF16), _log_sigmoid(gcol))
    a_rows = _exact_rhs_dot(_log_sigmoid(grow), (r <= c).astype(BF16))

    b_half = grow[0:heads, :] - a_rows[heads:2 * heads, :]
    b_rows = jnp.concatenate([b_half, b_half], axis=0)
    prefix_cols = _prefix_max_lanes(b_rows).T

    xb = jnp.concatenate([q_ref[...], k_ref[...]], axis=-1)
    xf = xb.astype(F32)
    conv = cb_ref[...] + xf * cw_ref[taps - 1:taps, :]
    for sft in range(1, taps):
        shifted = _dot((r - c == sft).astype(BF16), xb)
        conv = conv + shifted * cw_ref[taps - 1 - sft:taps - sft, :]
    edge = jnp.concatenate([halo, xf[0:pad, :]], axis=0)
    first = cb_ref[...]
    for j in range(taps):
        off = pad - (taps - 1) + j
        first = first + edge[off:off + pad, :] * cw_ref[j:j + 1, :]
    conv = jnp.concatenate([first, conv[pad:, :]], axis=0)
    qk = _silu(conv)
    prep = dict(qk=qk, gcol=gcol, a_cols=a_cols, b_rows=b_rows, causal=causal,
                prefix_cols=prefix_cols)
    return prep, xf[lc - pad:, :]


def _mlstm_head(prep, v_ref, o_ref, y_ref, ct_ref, m_ref, hd, heads):
    lc, width = v_ref.shape
    d = width // heads
    qk, gcol, a_cols, causal = prep["qk"], prep["gcol"], prep["a_cols"], prep["causal"]
    q = qk[:, hd * d:(hd + 1) * d]
    k = qk[:, width + hd * d:width + (hd + 1) * d] * (d ** -0.5)
    v = v_ref[:, hd * d:(hd + 1) * d]
    qb = q.astype(BF16)
    kb = k.astype(BF16)
    ig_c = gcol[:, hd:hd + 1]
    a_c = a_cols[:, heads + hd:heads + hd + 1]
    a_last = a_c[lc - 1:lc, :]
    ct = ct_ref[hd]
    m_prev = m_ref[hd:hd + 1, 0:1]
    x_c = jnp.maximum(prep["prefix_cols"][:, hd:hd + 1], m_prev)

    g_c = a_last - a_c + ig_c
    m_loc = jnp.max(g_c, axis=0, keepdims=True)
    w_c = jnp.exp(g_c - m_loc)
    m_new = jnp.maximum(a_last + m_prev, m_loc)
    decay = jnp.exp(a_last + m_prev - m_new)
    s_loc = jnp.exp(m_loc - m_new)
    wv = jnp.concatenate([w_c * v.astype(F32), jnp.broadcast_to(w_c, (lc, d))], axis=-1)
    increment = _dot_tn(kb, wv.astype(BF16))
    raw = _dot_nt(qb, kb)
    from_state = _dot(qb, ct.astype(BF16))

    decay_mat = jnp.exp2(jnp.where(causal, prep["b_rows"][hd:hd + 1, :] * LOG2E - x_c * LOG2E, NEG))
    sqk = (raw * decay_mat).astype(BF16)
    intra = _dot(sqk, jnp.concatenate([v, jnp.ones((lc, d), BF16)], axis=-1))
    inter = jnp.exp(m_prev - x_c) * from_state
    num = inter[:, 0:d] + intra[:, 0:d]
    den = inter[:, d:d + 1] + intra[:, d:d + 1]
    hval = num / jnp.maximum(jnp.abs(den), jnp.exp(-(a_c + x_c)))
    og = o_ref[:, hd * d:(hd + 1) * d].astype(F32)
    y_ref[:, hd * d:(hd + 1) * d] = (_sigmoid(og) * hval).astype(BF16)

    ct_ref[hd] = decay * ct + s_loc * increment
    m_ref[hd:hd + 1, :] = jnp.broadcast_to(m_new, (1, LANES))


def _mlstm_call(z, gates, gate_bias, conv_w, conv_b, heads, col0):
    b, s, _ = z.shape
    width = conv_w.shape[1] // 2
    lc = min(ML_CHUNK, s)
    rows = min(ML_CHUNKS_PER_STEP * lc, s)
    assert s % rows == 0 and rows % lc == 0 and col0 % width == 0 and width // heads == LANES
    assert 2 * heads == 8, "gate rows are handled as one 8-sublane tile"
    cb = col0 // width
    taps = conv_w.shape[0]
    kern = functools.partial(_mlstm_kernel, heads=heads, taps=taps, lc=lc)
    zspec = lambda off: pl.BlockSpec((None, rows, width), lambda bi, ci: (bi, ci, cb + off))
    return pl.pallas_call(
        kern,
        grid=(b, s // rows),
        in_specs=[
            zspec(0), zspec(1), zspec(2), zspec(3),
            pl.BlockSpec((None, rows, LANES), lambda bi, ci: (bi, ci, 0)),
            pl.BlockSpec((1, LANES), lambda bi, ci: (0, 0)),
            pl.BlockSpec((taps, 2 * width), lambda bi, ci: (0, 0)),
            pl.BlockSpec((1, 2 * width), lambda bi, ci: (0, 0)),
        ],
        out_specs=pl.BlockSpec((None, rows, width), lambda bi, ci: (bi, ci, 0)),
        out_shape=jax.ShapeDtypeStruct((b, s, width), BF16),
        scratch_shapes=[
            pltpu.VMEM((8, 2 * width), F32),
            pltpu.VMEM((heads, LANES, 2 * LANES), F32),
            pltpu.VMEM((2 * heads, LANES), F32),
        ],
        compiler_params=_params("arbitrary", "arbitrary"),
        name="mlstm",
    )(z, z, z, z, gates, gate_bias, conv_w, conv_b)


def _mix_ffn_kernel(*refs, n_in, chunk):
    y_refs = refs[:n_in]
    wo_refs = refs[n_in:2 * n_in]
    (x_ref, gm_ref, sh_ref, sc_ref, g_ref, gate_ref, w1_ref, w3_ref, w2_ref, o_ref) = refs[2 * n_in:]
    mix = _dot(y_refs[0][...], wo_refs[0][...])
    for y_ref, w_ref in zip(y_refs[1:], wo_refs[1:]):
        mix = mix + _dot(y_ref[...], w_ref[...])
    x = x_ref[...] + gm_ref[...] * mix
    h = _norm_modulate(x, g_ref[...], sh_ref[...], sc_ref[...]).astype(BF16)
    hidden = w1_ref.shape[1]

    acc = None
    for c0 in range(0, hidden, chunk):
        a = _dot(h, w1_ref[:, c0:c0 + chunk])
        bb = _dot(h, w3_ref[:, c0:c0 + chunk])
        u = (_silu(a) * bb).astype(BF16)
        part = _dot(u, w2_ref[c0:c0 + chunk, :])
        acc = part if acc is None else acc + part
    o_ref[...] = x + gate_ref[...] * acc


def _mix_ffn_call(ys, wo_all, wo_layer, x, gate_mix, shift, scale, g, gate, w1, w3, w2, layer, name):
    b, s, d = x.shape
    hidden = w1.shape[2]
    tm = min(ROW_TILE, s)
    assert s % tm == 0 and hidden % FFN_CHUNK == 0
    assert all(y.shape[2] == ys[0].shape[2] for y in ys)
    kern = functools.partial(_mix_ffn_kernel, n_in=len(ys), chunk=FFN_CHUNK)
    vec = pl.BlockSpec((None, 1, d), lambda i, j: (i, 0, 0))
    rows = lambda width: pl.BlockSpec((None, tm, width), lambda i, j: (i, j, 0))
    stacked = lambda w: pl.BlockSpec((None,) + w.shape[1:], lambda i, j: (layer, 0, 0))
    wo_specs = [pl.BlockSpec((None, y.shape[2], d), lambda i, j, n=n: (wo_layer, n, 0))
                for n, y in enumerate(ys)]
    return pl.pallas_call(
        kern,
        grid=(b, s // tm),
        in_specs=[rows(y.shape[2]) for y in ys] + wo_specs + [
            rows(d), vec, vec, vec,
            pl.BlockSpec((1, d), lambda i, j: (0, 0)),
            vec, stacked(w1), stacked(w3), stacked(w2),
        ],
        out_specs=rows(d),
        out_shape=jax.ShapeDtypeStruct((b, s, d), F32),
        compiler_params=_params("arbitrary", "arbitrary"),
        name=name,
    )(*ys, *([wo_all] * len(ys)), x, gate_mix, shift, scale, g, gate, w1, w3, w2)


def _pad_cols(w, n):
    return jnp.pad(w, ((0, 0), (0, n - w.shape[1])))


def _gain_row(q_g, k_g, width, total):
    reps = width // HEAD_DIM
    row = jnp.concatenate([jnp.tile(q_g.astype(F32) * (HEAD_DIM ** -0.5 * LOG2E), reps),
                           jnp.tile(k_g.astype(F32), reps),
                           jnp.ones((total - 2 * width,), F32)])
    return row.reshape(1, total)


def kernel(x, c, ada_w, ada_b, norm_mix_g, norm_ffn_g, ab_w_in, ml_b_i, ml_b_f, ml_conv_w, ml_conv_b,
           da_q_g, da_k_g, da_lambda, da_subln_g, ab_w_out, fx_w_in, fx_b_f, fx_q_g, fx_k_g, fx_w_out,
           ffn_w1, ffn_w3, ffn_w2):
    depth = ada_w.shape[0]
    b, s, d = x.shape
    ml_width = ml_conv_w.shape[2] // 2
    ml_heads = ml_b_i.shape[1]
    da_width = (ab_w_in.shape[2] - 4 * ml_width - 2 * ml_heads) // 3
    da_heads = da_width // (2 * HEAD_DIM)
    fx_heads = fx_b_f.shape[1]
    fx_width = fx_heads * HEAD_DIM
    ab_main = 3 * da_width + 4 * ml_width
    fx_main = 3 * fx_width

    mods = _ada_call(c, ada_w, ada_b).reshape(depth, b, 6, 1, d)
    slopes = 2.0 ** (-8.0 * jnp.arange(1, da_heads + 1, dtype=F32) / da_heads)

    ab_w_in_b, fx_w_in_b = ab_w_in.astype(BF16), fx_w_in.astype(BF16)
    ab_w_out_b, fx_w_out_b = ab_w_out.astype(BF16), fx_w_out.astype(BF16)
    ffn_w1_b, ffn_w3_b, ffn_w2_b = ffn_w1.astype(BF16), ffn_w3.astype(BF16), ffn_w2.astype(BF16)

    for l in range(depth):
        sh_m, sc_m, g_m, sh_f, sc_f, g_f = (mods[l, :, t] for t in range(6))
        j = l // 2
        g_mix = norm_mix_g[l].reshape(1, d)
        if l % 2 == 0:
            z, gates = _inproj_call(
                x, sh_m, sc_m, g_mix, ab_w_in_b, j, ab_main,
                _pad_cols(ab_w_in[j][:, ab_main:], LANES).astype(BF16),
                _gain_row(da_q_g[j], da_k_g[j], da_width, ab_main),
                2 * da_width, "inproj_even")
            lambda_init = 0.8 - 0.6 * math.exp(-0.3 * l)
            y_da = _da_call(z, slopes, da_lambda[j], da_subln_g[j], da_heads, lambda_init)
            gate_bias = _pad_cols(jnp.concatenate([ml_b_i[j], ml_b_f[j]]).reshape(1, -1), LANES)
            y_ml = _mlstm_call(z, gates, gate_bias, ml_conv_w[j], ml_conv_b[j].reshape(1, -1),
                               ml_heads, 3 * da_width)
            ys, wo_all, name = [y_da, y_ml], ab_w_out_b, "mix_ffn_even"
        else:
            z, gates = _inproj_call(
                x, sh_m, sc_m, g_mix, fx_w_in_b, j, fx_main,
                _pad_cols(fx_w_in[j][:, fx_main:], LANES).astype(BF16),
                _gain_row(fx_q_g[j], fx_k_g[j], fx_width, fx_main),
                2 * fx_width, "inproj_odd")
            fneg = _fox_gate_call(gates, _pad_cols(fx_b_f[j].reshape(1, -1), LANES))
            ys, wo_all, name = [_fox_call(z, fneg, fx_heads)], fx_w_out_b, "mix_ffn_odd"
        x = _mix_ffn_call(ys, wo_all, j, x, g_m, sh_f, sc_f, norm_ffn_g[l].reshape(1, d), g_f,
                          ffn_w1_b, ffn_w3_b, ffn_w2_b, l, name)
    return x
```

```python
import functools
import math

import jax
import jax.numpy as jnp
from jax import lax
from jax.experimental import pallas as pl
from jax.experimental.pallas import tpu as pltpu

F32 = jnp.float32
BF16 = jnp.bfloat16

EPS = 1e-6
LANES = 128
HEAD_DIM = 64
CHUNK = 64
CHUNK_SHIFT = CHUNK.bit_length() - 1
assert CHUNK == 1 << CHUNK_SHIFT
NEG = -1e30
LOG2E = math.log2(math.e)
BIAS_PIECES = 3
ONES_ROWS = 16

VMEM_LIMIT = 56 * 1024 * 1024

ADA_COLS = 1536
ROW_TILE = 1024
FFN_ROWS = 512
INPROJ_ROWS = 1024
INPROJ_CHUNK = 512
ATTN_TILE = 256
ATTN_RING = 2
ATTN_KEYS = ATTN_RING * ATTN_TILE
ATTN_BLOCKS = 4
PACK_ROWS = 512
ML_CHUNK = 256
ML_CHUNKS_PER_STEP = 2
GATE_CHUNK = 256
FFN_CHUNK = 256


def _params(*sem):
    return pltpu.CompilerParams(dimension_semantics=sem, vmem_limit_bytes=VMEM_LIMIT)


def _log_sigmoid(x):
    return jnp.minimum(x, 0.0) - jnp.log1p(jnp.exp(-jnp.abs(x)))


def _sigmoid(x):
    return 0.5 * jnp.tanh(0.5 * x) + 0.5


def _silu(x):
    h = 0.5 * x
    return h * (jnp.tanh(h) + 1.0)


def _prefix_max_lanes(x):
    axis = x.ndim - 1
    lane = lax.broadcasted_iota(jnp.int32, x.shape, axis)
    shift = 1
    while shift < x.shape[axis]:
        x = jnp.maximum(x, jnp.where(lane >= shift, pltpu.roll(x, shift, axis=axis), NEG))
        shift *= 2
    return x


def _split3(x):
    x1 = x.astype(BF16)
    r1 = x - x1.astype(F32)
    x2 = r1.astype(BF16)
    x3 = (r1 - x2.astype(F32)).astype(BF16)
    return x1, x2, x3


def _dot(a, b):
    return jnp.dot(a, b, preferred_element_type=F32)


def _dot_nt(a, b):
    return lax.dot_general(a, b, (((1,), (1,)), ((), ())), preferred_element_type=F32)


def _dot_tn(a, b):
    return lax.dot_general(a, b, (((0,), (0,)), ((), ())), preferred_element_type=F32)


def _exact_lhs_dot(m_bf16, x):
    x1, x2, x3 = _split3(x)
    return _dot(m_bf16, x1) + _dot(m_bf16, x2) + _dot(m_bf16, x3)


def _exact_rhs_dot(x, m_bf16):
    x1, x2, x3 = _split3(x)
    return _dot(x1, m_bf16) + _dot(x2, m_bf16) + _dot(x3, m_bf16)


def _norm_modulate(x, g, sh, sc):
    ms = jnp.mean(x * x, axis=-1, keepdims=True)
    return x * lax.rsqrt(ms + EPS) * (g * (1.0 + sc)) + sh


def _ada_kernel(c_ref, w_ref, b_ref, o_ref):
    c = c_ref[...]
    ca = _silu(c)
    c1, c2, c3 = _split3(ca)
    w = w_ref[...]
    w1 = w.astype(BF16)
    w2 = (w - w1.astype(F32)).astype(BF16)
    acc = _dot(c1, w1) + _dot(c1, w2) + _dot(c2, w1) + _dot(c2, w2) + _dot(c3, w1)
    o_ref[...] = acc + b_ref[...]


def _ada_call(c, ada_w, ada_b):
    depth, d, n = ada_w.shape
    b = c.shape[0]
    tn = min(ADA_COLS, n)
    assert n % tn == 0
    return pl.pallas_call(
        _ada_kernel,
        grid=(depth, n // tn),
        in_specs=[
            pl.BlockSpec((b, d), lambda l, j: (0, 0)),
            pl.BlockSpec((None, d, tn), lambda l, j: (l, 0, j)),
            pl.BlockSpec((None, 1, tn), lambda l, j: (l, 0, j)),
        ],
        out_specs=pl.BlockSpec((None, b, tn), lambda l, j: (l, 0, j)),
        out_shape=jax.ShapeDtypeStruct((depth, b, n), F32),
        compiler_params=_params("arbitrary", "arbitrary"),
        name="ada_mod",
    )(c, ada_w, ada_b.reshape(depth, 1, n))


def _inproj_kernel(x_ref, sh_ref, sc_ref, g_ref, w_ref, wg_ref, gain_ref, z_ref, gate_ref,
                   *, n_norm, chunk):
    half = x_ref.shape[0] // 2
    h_rows, first_rows = [], []
    for r0 in (0, half):
        h_part = _norm_modulate(x_ref[r0:r0 + half, :], g_ref[...], sh_ref[...], sc_ref[...]).astype(BF16)
        first_rows.append(_dot(h_part, w_ref[:, 0:chunk]))
        h_rows.append(h_part)
    h = jnp.concatenate(h_rows, axis=0)
    n = w_ref.shape[1]
    lane = lax.broadcasted_iota(jnp.int32, (1, LANES), 1)
    lo = lane < HEAD_DIM
    for c0 in range(0, n, chunk):
        zc = jnp.concatenate(first_rows, axis=0) if c0 == 0 else _dot(h, w_ref[:, c0:c0 + chunk])
        if c0 < n_norm:
            parts = []
            for s0 in range(0, chunk, LANES):
                zs = zc[:, s0:s0 + LANES]
                sq = zs * zs
                s_lo = jnp.sum(jnp.where(lo, sq, 0.0), axis=-1, keepdims=True)
                s_hi = jnp.sum(jnp.where(lo, 0.0, sq), axis=-1, keepdims=True)
                r_lo = lax.rsqrt(s_lo * (1.0 / HEAD_DIM) + EPS)
                r_hi = lax.rsqrt(s_hi * (1.0 / HEAD_DIM) + EPS)
                parts.append(zs * jnp.where(lo, r_lo, r_hi))
            zc = jnp.concatenate(parts, axis=-1) * gain_ref[:, c0:c0 + chunk]
        z_ref[:, c0:c0 + chunk] = zc.astype(BF16)
    gate_ref[...] = _dot(h, wg_ref[...])


def _inproj_call(x, shift, scale, g, w_all, layer, n, wg, gain, n_norm, name):
    b, s, d = x.shape
    tm = min(INPROJ_ROWS, s)
    chunk = INPROJ_CHUNK
    assert s % tm == 0 and tm % 2 == 0 and n % chunk == 0 and n_norm % chunk == 0
    kern = functools.partial(_inproj_kernel, n_norm=n_norm, chunk=chunk)
    return pl.pallas_call(
        kern,
        grid=(b, s // tm),
        in_specs=[
            pl.BlockSpec((None, tm, d), lambda i, j: (i, j, 0)),
            pl.BlockSpec((None, 1, d), lambda i, j: (i, 0, 0)),
            pl.BlockSpec((None, 1, d), lambda i, j: (i, 0, 0)),
            pl.BlockSpec((1, d), lambda i, j: (0, 0)),
            pl.BlockSpec((None, d, n), lambda i, j: (layer, 0, 0)),
            pl.BlockSpec((d, LANES), lambda i, j: (0, 0)),
            pl.BlockSpec((1, n), lambda i, j: (0, 0)),
        ],
        out_specs=[
            pl.BlockSpec((None, tm, n), lambda i, j: (i, j, 0)),
            pl.BlockSpec((None, tm, LANES), lambda i, j: (i, j, 0)),
        ],
        out_shape=[
            jax.ShapeDtypeStruct((b, s, n), BF16),
            jax.ShapeDtypeStruct((b, s, LANES), F32),
        ],
        compiler_params=_params("arbitrary", "arbitrary"),
        name=name,
    )(x, shift, scale, g, w_all, wg, gain)


def _lane_ids():
    return lax.broadcasted_iota(jnp.int32, (1, LANES), 1)


def _place_pieces(pieces, first_lane):
    lane = _lane_ids()
    out = jnp.zeros(pieces[0].shape, F32)
    for j, p in enumerate(pieces):
        out = jnp.where(lane == first_lane + j, p.astype(F32), out)
    return out


def _pack_keys(blocks, shared_fn):
    s = blocks[0][0].shape[0]
    rows = min(PACK_ROWS, s)
    keep_lo = jnp.where(_lane_ids() < HEAD_DIM, 1.0, 0.0).astype(BF16)
    keep_hi = jnp.where(_lane_ids() < HEAD_DIM, 0.0, 1.0).astype(BF16)

    def body(c, carry):
        r0 = pl.multiple_of(c * rows, rows)
        shared = shared_fn(r0, rows)
        for k_ref, v_ref, ka_ref, kb_ref, vt_refs, extras_fn in blocks:
            k2 = k_ref[pl.ds(r0, rows), :]
            ea, eb = extras_fn(shared)
            ka_ref[pl.ds(r0, rows), :] = k2 * keep_lo + ea.astype(BF16)
            kb_ref[pl.ds(r0, rows), :] = k2 * keep_hi + eb.astype(BF16)
            vt = v_ref[pl.ds(r0, rows), :].T
            for ref, c0, width in vt_refs:
                ref[0:width, pl.ds(r0, rows)] = vt[c0:c0 + width, :]
                ref[width:width + ONES_ROWS, pl.ds(r0, rows)] = jnp.ones((ONES_ROWS, rows), BF16)
        return carry

    lax.fori_loop(0, s // rows, body, 0)


def _pack_queries(q2, factors):
    lane = _lane_ids()
    q2 = q2.astype(F32)
    lo = lane < HEAD_DIM
    fac_a = _place_pieces(factors, HEAD_DIM)
    fac_b = _place_pieces(factors, 0)
    return jnp.where(lo, q2, fac_a).astype(BF16), jnp.where(lo, fac_b, q2).astype(BF16)


def _init_stats(m_ref, acc_ref):
    m_ref[...] = jnp.full(m_ref.shape, NEG, F32)
    acc_ref[...] = jnp.zeros(acc_ref.shape, F32)


def _softmax_step(s, tile_max, vt, m_ref, acc_ref):
    m_old = m_ref[...]
    m_new = jnp.maximum(m_old, tile_max)
    alpha = jnp.exp2(m_old - m_new)
    p = jnp.exp2(s - m_new)
    acc_ref[...] = alpha * acc_ref[...] + _dot(vt, p.astype(BF16))
    m_ref[...] = m_new


def _attend(n, t, tiles, s_buf, mx_buf, make_addend, finish_pair):
    tk = ATTN_KEYS
    ring = len(tiles)
    count = len(tiles[0])

    def qk_one(u, step, idx, rows=tk):
        k0 = pl.multiple_of(step * tk, tk)
        q, k_ref = tiles[u][idx][0], tiles[u][idx][1]
        s = _dot_nt(k_ref[pl.ds(k0, rows), :], q)
        s_buf[u][idx, 0:rows, :] = s
        if rows == tk:
            mx_buf[u][idx] = jnp.max(s, axis=0, keepdims=True)

    def stage(u, step, qk_tile, qk_step, qk_rows=tk, last=False):
        k0 = pl.multiple_of(step * tk, tk)
        rows, offset = (u + 1) * t, u * t
        addends = {}
        for idx, (_, _, vt, add_key, stats) in enumerate(tiles[u]):
            if qk_tile is not None:
                qk_one(qk_tile, qk_step, idx, qk_rows)
            if not last:
                _softmax_step(s_buf[u][idx], mx_buf[u][idx], vt(k0, tk), *stats)
                continue
            if add_key not in addends:
                addends[add_key] = make_addend(add_key, addends.setdefault("shared", {}), rows, offset)
            s = s_buf[u][idx, 0:rows, :] + addends[add_key]
            _softmax_step(s, jnp.max(s, axis=0, keepdims=True), vt(k0, rows), *stats)
            if idx % 2 == 1:
                finish_pair(u, idx // 2)

    first = ring - 1
    for idx in range(count):
        qk_one(first, 0, idx)
        for streams in tiles:
            _init_stats(*streams[idx][4])

    def ring_pass(j):
        for u in range(first, -1, -1):
            if u > 0:
                stage(u, j, u - 1, j)
            else:
                stage(u, j, first, j + 1)

    def body(p, carry):
        ring_pass(2 * p)
        ring_pass(2 * p + 1)
        return carry

    doubles = (n - 1) // 2
    lax.fori_loop(0, doubles, body, 0)

    @pl.when(n - 1 > 2 * doubles)
    def _():
        ring_pass(n - 2)
    for u in range(first, -1, -1):
        if u > 0:
            stage(u, n - 1, u - 1, n - 1, qk_rows=u * t, last=True)
        else:
            stage(u, n - 1, None, None, last=True)


def _lane_block(ref, h):
    return ref.at[:, pl.ds(h * LANES, LANES)]


def _da_kernel(slope_ref, q_ref, k_ref, v_ref, lam_ref, sg_ref, o_ref,
               ka_ref, kb_ref, vt_ref, m_ref, acc_ref, s0_ref, s1_ref, mx0_ref, mx1_ref,
               *, t, hb, lambda_init):
    dv = LANES
    g = pl.program_id(1)
    i = pl.program_id(2)
    slopes2 = [slope_ref[g * hb + h] * LOG2E for h in range(hb)]

    @pl.when(i == 0)
    def _():
        def position_pieces(r0, rows):
            pos = lax.broadcasted_iota(jnp.int32, (rows, LANES), 0) + r0
            hi = jnp.right_shift(pos, CHUNK_SHIFT).astype(F32)
            lo = jnp.bitwise_and(pos, CHUNK - 1).astype(F32)
            lane = _lane_ids()

            def place(first):
                in_hi = (lane >= first) & (lane < first + BIAS_PIECES)
                in_lo = (lane >= first + BIAS_PIECES) & (lane < first + 2 * BIAS_PIECES)
                return jnp.where(in_hi, hi, jnp.where(in_lo, lo, 0.0)).astype(BF16)
            return place(HEAD_DIM), place(0)
        _pack_keys([(_lane_block(k_ref, h), _lane_block(v_ref, h), ka_ref.at[h], kb_ref.at[h],
                     [(vt_ref.at[h], 0, dv)], lambda shared: shared) for h in range(hb)],
                   position_pieces)

    def last_step_addend(h, shared, rows, offset):
        if "base" not in shared:
            key = lax.broadcasted_iota(jnp.int32, (rows, t), 0)
            qry = lax.broadcasted_iota(jnp.int32, (rows, t), 1) + offset
            allowed = jnp.right_shift(key, CHUNK_SHIFT) <= jnp.right_shift(qry, CHUNK_SHIFT)
            shared["base"] = jnp.where(allowed, jnp.maximum(key - qry, 0).astype(F32), -NEG)
        return (-2.0 * slopes2[h]) * shared["base"]

    ring = q_ref.shape[0] // t
    tiles = []
    for u in range(ring):
        streams = []
        for h in range(hb):
            slope_pieces = [p.astype(F32) for p in _split3(jnp.full((1, LANES), slopes2[h], F32))]
            factors = [p * float(CHUNK) for p in slope_pieces] + slope_pieces
            qa, qb = _pack_queries(q_ref[u * t:(u + 1) * t, h * LANES:(h + 1) * LANES], factors)
            vt = lambda k0, size, h=h: vt_ref[h, :, pl.ds(k0, size)]
            for m, (q, k_sc) in enumerate(((qa, ka_ref), (qb, kb_ref))):
                n_stream = (u * hb + h) * 2 + m
                streams.append((q, k_sc.at[h], vt, h, (m_ref.at[n_stream], acc_ref.at[n_stream])))
        tiles.append(streams)

    def finish_head(u, h):
        lp = lam_ref[...]
        lam = (jnp.exp(jnp.sum(lp[0:1] * lp[1:2], axis=-1, keepdims=True))
               - jnp.exp(jnp.sum(lp[2:3] * lp[3:4], axis=-1, keepdims=True)) + lambda_init)
        pa = acc_ref[(u * hb + h) * 2]
        pb = acc_ref[(u * hb + h) * 2 + 1]
        out = pa[0:dv] / pa[dv:dv + 1] - lam * (pb[0:dv] / pb[dv:dv + 1])
        ms = jnp.mean(out * out, axis=0, keepdims=True)
        out = out * lax.rsqrt(ms + EPS) * (sg_ref[...] * (1.0 - lambda_init))
        o_ref[u * t:(u + 1) * t, h * LANES:(h + 1) * LANES] = out.T.astype(BF16)

    _attend(i + 1, t, tiles, (s0_ref, s1_ref), (mx0_ref, mx1_ref), last_step_addend, finish_head)


def _da_call(z, slopes, lam_p, subln_g, heads, lambda_init):
    b, s, _ = z.shape
    t, tk = ATTN_TILE, ATTN_KEYS
    assert s % tk == 0 and t % CHUNK == 0 and tk == ATTN_RING * t
    assert s <= CHUNK * 256, "key position // CHUNK must stay exact in bf16"
    hb = min(ATTN_BLOCKS, heads)
    groups = heads // hb
    assert heads % hb == 0
    kern = functools.partial(_da_kernel, t=t, hb=hb, lambda_init=lambda_init)
    w = hb * LANES
    n_streams = 2 * hb * ATTN_RING
    return pl.pallas_call(
        kern,
        grid_spec=pltpu.PrefetchScalarGridSpec(
            num_scalar_prefetch=1,
            grid=(b, groups, s // tk),
            in_specs=[
                pl.BlockSpec((None, tk, w), lambda bi, g, i, sl: (bi, i, g)),
                pl.BlockSpec((None, s, w), lambda bi, g, i, sl: (bi, 0, groups + g)),
                pl.BlockSpec((None, s, w), lambda bi, g, i, sl: (bi, 0, 2 * groups + g)),
                pl.BlockSpec((4, HEAD_DIM), lambda bi, g, i, sl: (0, 0)),
                pl.BlockSpec((LANES, 1), lambda bi, g, i, sl: (0, 0)),
            ],
            out_specs=pl.BlockSpec((None, tk, w), lambda bi, g, i, sl: (bi, i, g)),
            scratch_shapes=[
                pltpu.VMEM((hb, s, LANES), BF16), pltpu.VMEM((hb, s, LANES), BF16),
                pltpu.VMEM((hb, LANES + ONES_ROWS, s), BF16),
                pltpu.VMEM((n_streams, 1, t), F32),
                pltpu.VMEM((n_streams, LANES + ONES_ROWS, t), F32),
                pltpu.VMEM((2 * hb, ATTN_KEYS, t), F32), pltpu.VMEM((2 * hb, ATTN_KEYS, t), F32),
                pltpu.VMEM((2 * hb, 1, t), F32), pltpu.VMEM((2 * hb, 1, t), F32),
            ],
        ),
        out_shape=jax.ShapeDtypeStruct((b, s, heads * LANES), BF16),
        compiler_params=_params("arbitrary", "arbitrary", "arbitrary"),
        name="diff_attention",
    )(slopes, z, z, z, lam_p, subln_g.reshape(LANES, 1))


def _fox_gate_kernel(g_ref, b_ref, o_ref, *, chunk):
    s = g_ref.shape[0]
    r = lax.broadcasted_iota(jnp.int32, (chunk, chunk), 0)
    c = lax.broadcasted_iota(jnp.int32, (chunk, chunk), 1)
    tri = (c <= r).astype(BF16)
    carry = jnp.zeros((1, LANES), F32)
    for c0 in range(0, s, chunk):
        ls = _log_sigmoid(g_ref[c0:c0 + chunk, :] + b_ref[...])
        cs = _exact_lhs_dot(tri, ls) + carry
        carry = cs[chunk - 1:chunk, :]
        o_ref[c0:c0 + chunk, :] = cs * (-LOG2E)


def _fox_gate_call(gates, bias_row):
    b, s, _ = gates.shape
    chunk = min(GATE_CHUNK, s)
    assert s % chunk == 0
    kern = functools.partial(_fox_gate_kernel, chunk=chunk)
    return pl.pallas_call(
        kern,
        grid=(b,),
        in_specs=[
            pl.BlockSpec((None, s, LANES), lambda i: (i, 0, 0)),
            pl.BlockSpec((1, LANES), lambda i: (0, 0)),
        ],
        out_specs=pl.BlockSpec((None, s, LANES), lambda i: (i, 0, 0)),
        out_shape=jax.ShapeDtypeStruct((b, s, LANES), F32),
        compiler_params=_params("arbitrary"),
        name="fox_gate_cumsum",
    )(gates, bias_row)


def _fox_kernel(q_ref, k_ref, v_ref, f_ref, o_ref,
                ka_ref, kb_ref, vt_ref, m_ref, acc_ref, s0_ref, s1_ref, mx0_ref, mx1_ref, *, t, hb):
    dv = HEAD_DIM
    g = pl.program_id(1)
    i = pl.program_id(2)

    @pl.when(i == 0)
    def _():
        n = BIAS_PIECES * LANES
        src = lax.broadcasted_iota(jnp.int32, (n, LANES), 0)
        dst = lax.broadcasted_iota(jnp.int32, (n, LANES), 1)
        piece = jnp.right_shift(src, LANES.bit_length() - 1)
        head = src - piece * LANES
        blocks = []
        for h in range(hb):
            pair = g * hb + h
            place_a = ((head == 2 * pair) & (dst == HEAD_DIM + piece)).astype(BF16)
            place_b = ((head == 2 * pair + 1) & (dst == piece)).astype(BF16)
            place = jnp.concatenate([place_a, place_b], axis=-1)

            def extras(pieces, place=place):
                both = _dot(pieces, place)
                return both[:, 0:LANES], both[:, LANES:2 * LANES]
            blocks.append((_lane_block(k_ref, h), _lane_block(v_ref, h), ka_ref.at[h], kb_ref.at[h],
                           [(vt_ref.at[2 * h + m], m * dv, dv) for m in range(2)], extras))
        _pack_keys(blocks,
                   lambda r0, rows: jnp.concatenate(_split3(f_ref[pl.ds(r0, rows), :]), axis=-1))

    def last_step_addend(_, shared, rows, offset):
        key = lax.broadcasted_iota(jnp.int32, (rows, t), 0)
        qry = lax.broadcasted_iota(jnp.int32, (rows, t), 1) + offset
        return jnp.where(key <= qry, 0.0, NEG)

    ring = q_ref.shape[0] // t
    ones = [jnp.ones((1, LANES), F32)] * BIAS_PIECES
    tiles = []
    for u in range(ring):
        streams = []
        for h in range(hb):
            qa, qb = _pack_queries(q_ref[u * t:(u + 1) * t, h * LANES:(h + 1) * LANES], ones)
            for m, (q, k_sc) in enumerate(((qa, ka_ref), (qb, kb_ref))):
                vt = lambda k0, size, n=2 * h + m: vt_ref[n, :, pl.ds(k0, size)]
                n_stream = (u * hb + h) * 2 + m
                streams.append((q, k_sc.at[h], vt, 0, (m_ref.at[n_stream], acc_ref.at[n_stream])))
        tiles.append(streams)

    def finish_pair(u, h):
        pa = acc_ref[(u * hb + h) * 2]
        pb = acc_ref[(u * hb + h) * 2 + 1]
        out = jnp.concatenate([pa[0:dv] / pa[dv:dv + 1], pb[0:dv] / pb[dv:dv + 1]], axis=0)
        o_ref[u * t:(u + 1) * t, h * LANES:(h + 1) * LANES] = out.T.astype(BF16)

    _attend(i + 1, t, tiles, (s0_ref, s1_ref), (mx0_ref, mx1_ref), last_step_addend, finish_pair)


def _fox_call(z, fneg, heads):
    b, s, _ = z.shape
    pairs = heads // 2
    t, tk = ATTN_TILE, ATTN_KEYS
    assert s % tk == 0 and tk == ATTN_RING * t
    hb = min(ATTN_BLOCKS, pairs)
    groups = pairs // hb
    assert pairs % hb == 0
    kern = functools.partial(_fox_kernel, t=t, hb=hb)
    w = hb * LANES
    n_streams = 2 * hb * ATTN_RING
    return pl.pallas_call(
        kern,
        grid=(b, groups, s // tk),
        in_specs=[
            pl.BlockSpec((None, tk, w), lambda bi, g, i: (bi, i, g)),
            pl.BlockSpec((None, s, w), lambda bi, g, i: (bi, 0, groups + g)),
            pl.BlockSpec((None, s, w), lambda bi, g, i: (bi, 0, 2 * groups + g)),
            pl.BlockSpec((None, s, LANES), lambda bi, g, i: (bi, 0, 0)),
        ],
        out_specs=pl.BlockSpec((None, tk, w), lambda bi, g, i: (bi, i, g)),
        out_shape=jax.ShapeDtypeStruct((b, s, pairs * LANES), BF16),
        scratch_shapes=[
            pltpu.VMEM((hb, s, LANES), BF16), pltpu.VMEM((hb, s, LANES), BF16),
            pltpu.VMEM((2 * hb, HEAD_DIM + ONES_ROWS, s), BF16),
            pltpu.VMEM((n_streams, 1, t), F32),
            pltpu.VMEM((n_streams, HEAD_DIM + ONES_ROWS, t), F32),
            pltpu.VMEM((2 * hb, ATTN_KEYS, t), F32), pltpu.VMEM((2 * hb, ATTN_KEYS, t), F32),
            pltpu.VMEM((2 * hb, 1, t), F32), pltpu.VMEM((2 * hb, 1, t), F32),
        ],
        compiler_params=_params("arbitrary", "arbitrary", "arbitrary"),
        name="forgetting_attention",
    )(z, z, z, fneg)


def _mlstm_kernel(q_ref, k_ref, v_ref, o_ref, g_ref, gb_ref, cw_ref, cb_ref, y_ref,
                  halo_ref, ct_ref, m_ref, *, heads, taps, lc):
    @pl.when(pl.program_id(1) == 0)
    def _():
        halo_ref[...] = jnp.zeros(halo_ref.shape, F32)
        ct_ref[...] = jnp.zeros(ct_ref.shape, F32)
        m_ref[...] = jnp.zeros(m_ref.shape, F32)

    rows = lambda ref, sub: ref.at[pl.ds(sub * lc, lc), :]
    edge = halo_ref[...]
    for sub in range(q_ref.shape[0] // lc):
        prep, edge = _mlstm_prep(rows(q_ref, sub), rows(k_ref, sub), rows(g_ref, sub), gb_ref,
                                 cw_ref, cb_ref, edge, heads, taps)
        for hd in range(heads):
            _mlstm_head(prep, rows(v_ref, sub), rows(o_ref, sub), rows(y_ref, sub),
                        ct_ref, m_ref, hd, heads)
    halo_ref[...] = edge


def _mlstm_prep(q_ref, k_ref, g_ref, gb_ref, cw_ref, cb_ref, halo, heads, taps):
    lc, width = q_ref.shape
    pad = halo.shape[0]

    r = lax.broadcasted_iota(jnp.int32, (lc, lc), 0)
    c = lax.broadcasted_iota(jnp.int32, (lc, lc), 1)
    causal = c <= r

    gcol = g_ref[...] + gb_ref[...]
    grow = gcol.T[:2 * heads, :]
    a_cols = _exact_lhs_dot(causal.astype(BF16), _log_sigmoid(gcol))
    a_rows = _exact_rhs_dot(_log_sigmoid(grow), (r <= c).astype(BF16))

    b_half = grow[0:heads, :] - a_rows[heads:2 * heads, :]
    b_rows = jnp.concatenate([b_half, b_half], axis=0)
    prefix_cols = _prefix_max_lanes(b_rows).T

    xb = jnp.concatenate([q_ref[...], k_ref[...]], axis=-1)
    xf = xb.astype(F32)
    conv = cb_ref[...] + xf * cw_ref[taps - 1:taps, :]
    for sft in range(1, taps):
        shifted = _dot((r - c == sft).astype(BF16), xb)
        conv = conv + shifted * cw_ref[taps - 1 - sft:taps - sft, :]
    edge = jnp.concatenate([halo, xf[0:pad, :]], axis=0)
    first = cb_ref[...]
    for j in range(taps):
        off = pad - (taps - 1) + j
        first = first + edge[off:off + pad, :] * cw_ref[j:j + 1, :]
    conv = jnp.concatenate([first, conv[pad:, :]], axis=0)
    qk = _silu(conv)
    prep = dict(qk=qk, gcol=gcol, a_cols=a_cols, b_rows=b_rows, causal=causal,
                prefix_cols=prefix_cols)
    return prep, xf[lc - pad:, :]


def _mlstm_head(prep, v_ref, o_ref, y_ref, ct_ref, m_ref, hd, heads):
    lc, width = v_ref.shape
    d = width // heads
    qk, gcol, a_cols, causal = prep["qk"], prep["gcol"], prep["a_cols"], prep["causal"]
    q = qk[:, hd * d:(hd + 1) * d]
    k = qk[:, width + hd * d:width + (hd + 1) * d] * (d ** -0.5)
    v = v_ref[:, hd * d:(hd + 1) * d]
    qb = q.astype(BF16)
    kb = k.astype(BF16)
    ig_c = gcol[:, hd:hd + 1]
    a_c = a_cols[:, heads + hd:heads + hd + 1]
    a_last = a_c[lc - 1:lc, :]
    ct = ct_ref[hd]
    m_prev = m_ref[hd:hd + 1, 0:1]
    x_c = jnp.maximum(prep["prefix_cols"][:, hd:hd + 1], m_prev)

    g_c = a_last - a_c + ig_c
    m_loc = jnp.max(g_c, axis=0, keepdims=True)
    w_c = jnp.exp(g_c - m_loc)
    m_new = jnp.maximum(a_last + m_prev, m_loc)
    decay = jnp.exp(a_last + m_prev - m_new)
    s_loc = jnp.exp(m_loc - m_new)
    wv = jnp.concatenate([w_c * v.astype(F32), jnp.broadcast_to(w_c, (lc, d))], axis=-1)
    increment = _dot_tn(kb, wv.astype(BF16))
    raw = _dot_nt(qb, kb)
    from_state = _dot(qb, ct.astype(BF16))

    decay_mat = jnp.exp2(jnp.where(causal, prep["b_rows"][hd:hd + 1, :] * LOG2E - x_c * LOG2E, NEG))
    sqk = (raw * decay_mat).astype(BF16)
    intra = _dot(sqk, jnp.concatenate([v, jnp.ones((lc, d), BF16)], axis=-1))
    inter = jnp.exp(m_prev - x_c) * from_state
    num = inter[:, 0:d] + intra[:, 0:d]
    den = inter[:, d:d + 1] + intra[:, d:d + 1]
    hval = num / jnp.maximum(jnp.abs(den), jnp.exp(-(a_c + x_c)))
    og = o_ref[:, hd * d:(hd + 1) * d].astype(F32)
    y_ref[:, hd * d:(hd + 1) * d] = (_sigmoid(og) * hval).astype(BF16)

    ct_ref[hd] = decay * ct + s_loc * increment
    m_ref[hd:hd + 1, :] = jnp.broadcast_to(m_new, (1, LANES))


def _mlstm_call(z, gates, gate_bias, conv_w, conv_b, heads, col0):
    b, s, _ = z.shape
    width = conv_w.shape[1] // 2
    lc = min(ML_CHUNK, s)
    rows = min(ML_CHUNKS_PER_STEP * lc, s)
    assert s % rows == 0 and rows % lc == 0 and col0 % width == 0 and width // heads == LANES
    assert 2 * heads == 8, "gate rows are handled as one 8-sublane tile"
    cb = col0 // width
    taps = conv_w.shape[0]
    kern = functools.partial(_mlstm_kernel, heads=heads, taps=taps, lc=lc)
    zspec = lambda off: pl.BlockSpec((None, rows, width), lambda bi, ci: (bi, ci, cb + off))
    return pl.pallas_call(
        kern,
        grid=(b, s // rows),
        in_specs=[
            zspec(0), zspec(1), zspec(2), zspec(3),
            pl.BlockSpec((None, rows, LANES), lambda bi, ci: (bi, ci, 0)),
            pl.BlockSpec((1, LANES), lambda bi, ci: (0, 0)),
            pl.BlockSpec((taps, 2 * width), lambda bi, ci: (0, 0)),
            pl.BlockSpec((1, 2 * width), lambda bi, ci: (0, 0)),
        ],
        out_specs=pl.BlockSpec((None, rows, width), lambda bi, ci: (bi, ci, 0)),
        out_shape=jax.ShapeDtypeStruct((b, s, width), BF16),
        scratch_shapes=[
            pltpu.VMEM((8, 2 * width), F32),
            pltpu.VMEM((heads, LANES, 2 * LANES), F32),
            pltpu.VMEM((2 * heads, LANES), F32),
        ],
        compiler_params=_params("arbitrary", "arbitrary"),
        name="mlstm",
    )(z, z, z, z, gates, gate_bias, conv_w, conv_b)


def _mix_ffn_kernel(*refs, n_in, chunk):
    y_refs = refs[:n_in]
    wo_refs = refs[n_in:2 * n_in]
    (x_ref, gm_ref, sh_ref, sc_ref, g_ref, gate_ref, w1_ref, w3_ref, w2_ref, o_ref) = refs[2 * n_in:]
    hidden = w1_ref.shape[1]
    sub = min(FFN_ROWS, x_ref.shape[0])

    def mixed(r0):
        mix = _dot(y_refs[0][r0:r0 + sub, :], wo_refs[0][...])
        for y_ref, w_ref in zip(y_refs[1:], wo_refs[1:]):
            mix = mix + _dot(y_ref[r0:r0 + sub, :], w_ref[...])
        x = x_ref[r0:r0 + sub, :] + gm_ref[...] * mix
        return x, _norm_modulate(x, g_ref[...], sh_ref[...], sc_ref[...]).astype(BF16)

    def ffn(r0, x, h):
        acc = None
        for c0 in range(0, hidden, chunk):
            a = _dot(h, w1_ref[:, c0:c0 + chunk])
            bb = _dot(h, w3_ref[:, c0:c0 + chunk])
            u = (_silu(a) * bb).astype(BF16)
            part = _dot(u, w2_ref[c0:c0 + chunk, :])
            acc = part if acc is None else acc + part
        o_ref[r0:r0 + sub, :] = x + gate_ref[...] * acc

    starts = list(range(0, x_ref.shape[0], sub))
    ready = mixed(starts[0])
    for k, r0 in enumerate(starts):
        current = ready
        if k + 1 < len(starts):
            ready = mixed(starts[k + 1])
        ffn(r0, *current)


def _mix_ffn_call(ys, wo_all, wo_layer, x, gate_mix, shift, scale, g, gate, w1, w3, w2, layer, name):
    b, s, d = x.shape
    hidden = w1.shape[2]
    tm = min(ROW_TILE, s)
    assert s % tm == 0 and hidden % FFN_CHUNK == 0
    assert all(y.shape[2] == ys[0].shape[2] for y in ys)
    kern = functools.partial(_mix_ffn_kernel, n_in=len(ys), chunk=FFN_CHUNK)
    vec = pl.BlockSpec((None, 1, d), lambda i, j: (i, 0, 0))
    rows = lambda width: pl.BlockSpec((None, tm, width), lambda i, j: (i, j, 0))
    stacked = lambda w: pl.BlockSpec((None,) + w.shape[1:], lambda i, j: (layer, 0, 0))
    wo_specs = [pl.BlockSpec((None, y.shape[2], d), lambda i, j, n=n: (wo_layer, n, 0))
                for n, y in enumerate(ys)]
    return pl.pallas_call(
        kern,
        grid=(b, s // tm),
        in_specs=[rows(y.shape[2]) for y in ys] + wo_specs + [
            rows(d), vec, vec, vec,
            pl.BlockSpec((1, d), lambda i, j: (0, 0)),
            vec, stacked(w1), stacked(w3), stacked(w2),
        ],
        out_specs=rows(d),
        out_shape=jax.ShapeDtypeStruct((b, s, d), F32),
        compiler_params=_params("arbitrary", "arbitrary"),
        name=name,
    )(*ys, *([wo_all] * len(ys)), x, gate_mix, shift, scale, g, gate, w1, w3, w2)


def _pad_cols(w, n):
    return jnp.pad(w, ((0, 0), (0, n - w.shape[1])))


def _gain_row(q_g, k_g, width, total):
    reps = width // HEAD_DIM
    row = jnp.concatenate([jnp.tile(q_g.astype(F32) * (HEAD_DIM ** -0.5 * LOG2E), reps),
                           jnp.tile(k_g.astype(F32), reps),
                           jnp.ones((total - 2 * width,), F32)])
    return row.reshape(1, total)


def kernel(x, c, ada_w, ada_b, norm_mix_g, norm_ffn_g, ab_w_in, ml_b_i, ml_b_f, ml_conv_w, ml_conv_b,
           da_q_g, da_k_g, da_lambda, da_subln_g, ab_w_out, fx_w_in, fx_b_f, fx_q_g, fx_k_g, fx_w_out,
           ffn_w1, ffn_w3, ffn_w2):
    depth = ada_w.shape[0]
    b, s, d = x.shape
    ml_width = ml_conv_w.shape[2] // 2
    ml_heads = ml_b_i.shape[1]
    da_width = (ab_w_in.shape[2] - 4 * ml_width - 2 * ml_heads) // 3
    da_heads = da_width // (2 * HEAD_DIM)
    fx_heads = fx_b_f.shape[1]
    fx_width = fx_heads * HEAD_DIM
    ab_main = 3 * da_width + 4 * ml_width
    fx_main = 3 * fx_width

    mods = _ada_call(c, ada_w, ada_b).reshape(depth, b, 6, 1, d)
    slopes = 2.0 ** (-8.0 * jnp.arange(1, da_heads + 1, dtype=F32) / da_heads)

    ab_w_in_b, fx_w_in_b = ab_w_in.astype(BF16), fx_w_in.astype(BF16)
    ab_w_out_b, fx_w_out_b = ab_w_out.astype(BF16), fx_w_out.astype(BF16)
    ffn_w1_b, ffn_w3_b, ffn_w2_b = ffn_w1.astype(BF16), ffn_w3.astype(BF16), ffn_w2.astype(BF16)

    for l in range(depth):
        sh_m, sc_m, g_m, sh_f, sc_f, g_f = (mods[l, :, t] for t in range(6))
        j = l // 2
        g_mix = norm_mix_g[l].reshape(1, d)
        if l % 2 == 0:
            z, gates = _inproj_call(
                x, sh_m, sc_m, g_mix, ab_w_in_b, j, ab_main,
                _pad_cols(ab_w_in[j][:, ab_main:], LANES).astype(BF16),
                _gain_row(da_q_g[j], da_k_g[j], da_width, ab_main),
                2 * da_width, "inproj_even")
            lambda_init = 0.8 - 0.6 * math.exp(-0.3 * l)
            y_da = _da_call(z, slopes, da_lambda[j], da_subln_g[j], da_heads, lambda_init)
            gate_bias = _pad_cols(jnp.concatenate([ml_b_i[j], ml_b_f[j]]).reshape(1, -1), LANES)
            y_ml = _mlstm_call(z, gates, gate_bias, ml_conv_w[j], ml_conv_b[j].reshape(1, -1),
                               ml_heads, 3 * da_width)
            ys, wo_all, name = [y_da, y_ml], ab_w_out_b, "mix_ffn_even"
        else:
            z, gates = _inproj_call(
                x, sh_m, sc_m, g_mix, fx_w_in_b, j, fx_main,
                _pad_cols(fx_w_in[j][:, fx_main:], LANES).astype(BF16),
                _gain_row(fx_q_g[j], fx_k_g[j], fx_width, fx_main),
                2 * fx_width, "inproj_odd")
            fneg = _fox_gate_call(gates, _pad_cols(fx_b_f[j].reshape(1, -1), LANES))
            ys, wo_all, name = [_fox_call(z, fneg, fx_heads)], fx_w_out_b, "mix_ffn_odd"
        x = _mix_ffn_call(ys, wo_all, j, x, g_m, sh_f, sc_f, norm_ffn_g[l].reshape(1, d), g_f,
                          ffn_w1_b, ffn_w3_b, ffn_w2_b, l, name)
    return x
```

```python
import functools
import math

import jax
import jax.numpy as jnp
from jax import lax
from jax.experimental import pallas as pl
from jax.experimental.pallas import tpu as pltpu

F32 = jnp.float32
BF16 = jnp.bfloat16

EPS = 1e-6
LANES = 128
HEAD_DIM = 64
CHUNK = 64
CHUNK_SHIFT = CHUNK.bit_length() - 1
assert CHUNK == 1 << CHUNK_SHIFT
NEG = -1e30
LOG2E = math.log2(math.e)
BIAS_PIECES = 3
ONES_ROWS = 16

VMEM_LIMIT = 56 * 1024 * 1024

ADA_COLS = 1536
ROW_TILE = 1024
FFN_ROWS = 512
INPROJ_ROWS = 1024
INPROJ_CHUNK = 512
ATTN_TILE = 256
ATTN_RING = 2
ATTN_KEYS = ATTN_RING * ATTN_TILE
ATTN_BLOCKS = 4
PACK_ROWS = 512
ML_CHUNK = 256
ML_CHUNKS_PER_STEP = 2
GATE_CHUNK = 256
FFN_CHUNK = 256


def _params(*sem):
    return pltpu.CompilerParams(dimension_semantics=sem, vmem_limit_bytes=VMEM_LIMIT)


def _log_sigmoid(x):
    return jnp.minimum(x, 0.0) - jnp.log1p(jnp.exp(-jnp.abs(x)))


def _sigmoid(x):
    return 0.5 * jnp.tanh(0.5 * x) + 0.5


def _silu(x):
    h = 0.5 * x
    return h * (jnp.tanh(h) + 1.0)


def _prefix_max_lanes(x):
    axis = x.ndim - 1
    lane = lax.broadcasted_iota(jnp.int32, x.shape, axis)
    shift = 1
    while shift < x.shape[axis]:
        x = jnp.maximum(x, jnp.where(lane >= shift, pltpu.roll(x, shift, axis=axis), NEG))
        shift *= 2
    return x


def _split3(x):
    x1 = x.astype(BF16)
    r1 = x - x1.astype(F32)
    x2 = r1.astype(BF16)
    x3 = (r1 - x2.astype(F32)).astype(BF16)
    return x1, x2, x3


def _dot(a, b):
    return jnp.dot(a, b, preferred_element_type=F32)


def _dot_nt(a, b):
    return lax.dot_general(a, b, (((1,), (1,)), ((), ())), preferred_element_type=F32)


def _dot_tn(a, b):
    return lax.dot_general(a, b, (((0,), (0,)), ((), ())), preferred_element_type=F32)


def _exact_lhs_dot(m_bf16, x):
    x1, x2, x3 = _split3(x)
    return _dot(m_bf16, x1) + _dot(m_bf16, x2) + _dot(m_bf16, x3)


def _exact_rhs_dot(x, m_bf16):
    x1, x2, x3 = _split3(x)
    return _dot(x1, m_bf16) + _dot(x2, m_bf16) + _dot(x3, m_bf16)


def _norm_modulate(x, g, sh, sc):
    ms = jnp.mean(x * x, axis=-1, keepdims=True)
    return x * lax.rsqrt(ms + EPS) * (g * (1.0 + sc)) + sh


def _ada_kernel(c_ref, w_ref, b_ref, o_ref):
    c = c_ref[...]
    ca = _silu(c)
    c1, c2, c3 = _split3(ca)
    w = w_ref[...]
    w1 = w.astype(BF16)
    w2 = (w - w1.astype(F32)).astype(BF16)
    acc = _dot(c1, w1) + _dot(c1, w2) + _dot(c2, w1) + _dot(c2, w2) + _dot(c3, w1)
    o_ref[...] = acc + b_ref[...]


def _ada_call(c, ada_w, ada_b):
    depth, d, n = ada_w.shape
    b = c.shape[0]
    tn = min(ADA_COLS, n)
    assert n % tn == 0
    return pl.pallas_call(
        _ada_kernel,
        grid=(depth, n // tn),
        in_specs=[
            pl.BlockSpec((b, d), lambda l, j: (0, 0)),
            pl.BlockSpec((None, d, tn), lambda l, j: (l, 0, j)),
            pl.BlockSpec((None, 1, tn), lambda l, j: (l, 0, j)),
        ],
        out_specs=pl.BlockSpec((None, b, tn), lambda l, j: (l, 0, j)),
        out_shape=jax.ShapeDtypeStruct((depth, b, n), F32),
        compiler_params=_params("arbitrary", "arbitrary"),
        name="ada_mod",
    )(c, ada_w, ada_b.reshape(depth, 1, n))


def _inproj_kernel(x_ref, sh_ref, sc_ref, g_ref, w_ref, wg_ref, gain_ref, z_ref, gate_ref,
                   *, n_norm, chunk):
    half = x_ref.shape[0] // 2
    h_rows, first_rows = [], []
    for r0 in (0, half):
        h_part = _norm_modulate(x_ref[r0:r0 + half, :], g_ref[...], sh_ref[...], sc_ref[...]).astype(BF16)
        first_rows.append(_dot(h_part, w_ref[:, 0:chunk]))
        h_rows.append(h_part)
    h = jnp.concatenate(h_rows, axis=0)
    n = w_ref.shape[1]
    lane = lax.broadcasted_iota(jnp.int32, (1, LANES), 1)
    lo = lane < HEAD_DIM
    for c0 in range(0, n, chunk):
        zc = jnp.concatenate(first_rows, axis=0) if c0 == 0 else _dot(h, w_ref[:, c0:c0 + chunk])
        if c0 < n_norm:
            parts = []
            for s0 in range(0, chunk, LANES):
                zs = zc[:, s0:s0 + LANES]
                sq = zs * zs
                s_lo = jnp.sum(jnp.where(lo, sq, 0.0), axis=-1, keepdims=True)
                s_hi = jnp.sum(jnp.where(lo, 0.0, sq), axis=-1, keepdims=True)
                r_lo = lax.rsqrt(s_lo + HEAD_DIM * EPS)
                r_hi = lax.rsqrt(s_hi + HEAD_DIM * EPS)
                parts.append(zs * jnp.where(lo, r_lo, r_hi))
            zc = jnp.concatenate(parts, axis=-1) * gain_ref[:, c0:c0 + chunk]
        z_ref[:, c0:c0 + chunk] = zc.astype(BF16)
    gate_ref[...] = _dot(h, wg_ref[...])


def _inproj_call(x, shift, scale, g, w_all, layer, n, wg, gain, n_norm, name):
    b, s, d = x.shape
    tm = min(INPROJ_ROWS, s)
    chunk = INPROJ_CHUNK
    assert s % tm == 0 and tm % 2 == 0 and n % chunk == 0 and n_norm % chunk == 0
    kern = functools.partial(_inproj_kernel, n_norm=n_norm, chunk=chunk)
    return pl.pallas_call(
        kern,
        grid=(b, s // tm),
        in_specs=[
            pl.BlockSpec((None, tm, d), lambda i, j: (i, j, 0)),
            pl.BlockSpec((None, 1, d), lambda i, j: (i, 0, 0)),
            pl.BlockSpec((None, 1, d), lambda i, j: (i, 0, 0)),
            pl.BlockSpec((1, d), lambda i, j: (0, 0)),
            pl.BlockSpec((None, d, n), lambda i, j: (layer, 0, 0)),
            pl.BlockSpec((d, LANES), lambda i, j: (0, 0)),
            pl.BlockSpec((1, n), lambda i, j: (0, 0)),
        ],
        out_specs=[
            pl.BlockSpec((None, tm, n), lambda i, j: (i, j, 0)),
            pl.BlockSpec((None, tm, LANES), lambda i, j: (i, j, 0)),
        ],
        out_shape=[
            jax.ShapeDtypeStruct((b, s, n), BF16),
            jax.ShapeDtypeStruct((b, s, LANES), F32),
        ],
        compiler_params=_params("arbitrary", "arbitrary"),
        name=name,
    )(x, shift, scale, g, w_all, wg, gain)


def _lane_ids():
    return lax.broadcasted_iota(jnp.int32, (1, LANES), 1)


def _place_pieces(pieces, first_lane):
    lane = _lane_ids()
    out = jnp.zeros(pieces[0].shape, F32)
    for j, p in enumerate(pieces):
        out = jnp.where(lane == first_lane + j, p.astype(F32), out)
    return out


def _pack_keys(blocks, shared_fn):
    s = blocks[0][0].shape[0]
    rows = min(PACK_ROWS, s)
    keep_lo = jnp.where(_lane_ids() < HEAD_DIM, 1.0, 0.0).astype(BF16)
    keep_hi = jnp.where(_lane_ids() < HEAD_DIM, 0.0, 1.0).astype(BF16)

    def body(c, carry):
        r0 = pl.multiple_of(c * rows, rows)
        shared = shared_fn(r0, rows)
        for k_ref, v_ref, ka_ref, kb_ref, vt_refs, extras_fn in blocks:
            k2 = k_ref[pl.ds(r0, rows), :]
            ea, eb = extras_fn(shared)
            ka_ref[pl.ds(r0, rows), :] = k2 * keep_lo + ea.astype(BF16)
            kb_ref[pl.ds(r0, rows), :] = k2 * keep_hi + eb.astype(BF16)
            vt = v_ref[pl.ds(r0, rows), :].T
            for ref, c0, width in vt_refs:
                ref[0:width, pl.ds(r0, rows)] = vt[c0:c0 + width, :]
                ref[width:width + ONES_ROWS, pl.ds(r0, rows)] = jnp.ones((ONES_ROWS, rows), BF16)
        return carry

    lax.fori_loop(0, s // rows, body, 0)


def _pack_queries(q2, factors):
    lane = _lane_ids()
    q2 = q2.astype(F32)
    lo = lane < HEAD_DIM
    fac_a = _place_pieces(factors, HEAD_DIM)
    fac_b = _place_pieces(factors, 0)
    return jnp.where(lo, q2, fac_a).astype(BF16), jnp.where(lo, fac_b, q2).astype(BF16)


def _init_stats(m_ref, acc_ref):
    m_ref[...] = jnp.full(m_ref.shape, NEG, F32)
    acc_ref[...] = jnp.zeros(acc_ref.shape, F32)


def _softmax_step(s, tile_max, vt, m_ref, acc_ref):
    m_old = m_ref[...]
    m_new = jnp.maximum(m_old, tile_max)
    alpha = jnp.exp2(m_old - m_new)
    p = jnp.exp2(s - m_new)
    acc_ref[...] = alpha * acc_ref[...] + _dot(vt, p.astype(BF16))
    m_ref[...] = m_new


def _attend(n, t, tiles, s_buf, mx_buf, make_addend, finish_pair):
    tk = ATTN_KEYS
    ring = len(tiles)
    count = len(tiles[0])

    def qk_one(u, step, idx, rows=tk):
        k0 = pl.multiple_of(step * tk, tk)
        q, k_ref = tiles[u][idx][0], tiles[u][idx][1]
        s = _dot_nt(k_ref[pl.ds(k0, rows), :], q)
        s_buf[u][idx, 0:rows, :] = s
        if rows == tk:
            mx_buf[u][idx] = jnp.max(s, axis=0, keepdims=True)

    def stage(u, step, qk_tile, qk_step, qk_rows=tk, last=False):
        k0 = pl.multiple_of(step * tk, tk)
        rows, offset = (u + 1) * t, u * t
        addends = {}
        for idx, (_, _, vt, add_key, stats) in enumerate(tiles[u]):
            if qk_tile is not None:
                qk_one(qk_tile, qk_step, idx, qk_rows)
            if not last:
                _softmax_step(s_buf[u][idx], mx_buf[u][idx], vt(k0, tk), *stats)
                continue
            if add_key not in addends:
                addends[add_key] = make_addend(add_key, addends.setdefault("shared", {}), rows, offset)
            s = s_buf[u][idx, 0:rows, :] + addends[add_key]
            _softmax_step(s, jnp.max(s, axis=0, keepdims=True), vt(k0, rows), *stats)
            if idx % 2 == 1:
                finish_pair(u, idx // 2)

    first = ring - 1
    for idx in range(count):
        qk_one(first, 0, idx)
        for streams in tiles:
            _init_stats(*streams[idx][4])

    def ring_pass(j):
        for u in range(first, -1, -1):
            if u > 0:
                stage(u, j, u - 1, j)
            else:
                stage(u, j, first, j + 1)

    def body(p, carry):
        ring_pass(2 * p)
        ring_pass(2 * p + 1)
        return carry

    doubles = (n - 1) // 2
    lax.fori_loop(0, doubles, body, 0)

    @pl.when(n - 1 > 2 * doubles)
    def _():
        ring_pass(n - 2)
    for u in range(first, -1, -1):
        if u > 0:
            stage(u, n - 1, u - 1, n - 1, qk_rows=u * t, last=True)
        else:
            stage(u, n - 1, None, None, last=True)


def _lane_block(ref, h):
    return ref.at[:, pl.ds(h * LANES, LANES)]


def _da_kernel(slope_ref, q_ref, k_ref, v_ref, lam_ref, sg_ref, o_ref,
               ka_ref, kb_ref, vt_ref, m_ref, acc_ref, s0_ref, s1_ref, mx0_ref, mx1_ref,
               *, t, hb, lambda_init):
    dv = LANES
    g = pl.program_id(1)
    i = pl.program_id(2)
    slopes2 = [slope_ref[g * hb + h] * LOG2E for h in range(hb)]

    @pl.when(i == 0)
    def _():
        def position_pieces(r0, rows):
            pos = lax.broadcasted_iota(jnp.int32, (rows, LANES), 0) + r0
            hi = jnp.right_shift(pos, CHUNK_SHIFT).astype(F32)
            lo = jnp.bitwise_and(pos, CHUNK - 1).astype(F32)
            lane = _lane_ids()

            def place(first):
                in_hi = (lane >= first) & (lane < first + BIAS_PIECES)
                in_lo = (lane >= first + BIAS_PIECES) & (lane < first + 2 * BIAS_PIECES)
                return jnp.where(in_hi, hi, jnp.where(in_lo, lo, 0.0)).astype(BF16)
            return place(HEAD_DIM), place(0)
        _pack_keys([(_lane_block(k_ref, h), _lane_block(v_ref, h), ka_ref.at[h], kb_ref.at[h],
                     [(vt_ref.at[h], 0, dv)], lambda shared: shared) for h in range(hb)],
                   position_pieces)

    def last_step_addend(h, shared, rows, offset):
        if "base" not in shared:
            key = lax.broadcasted_iota(jnp.int32, (rows, t), 0)
            qry = lax.broadcasted_iota(jnp.int32, (rows, t), 1) + offset
            allowed = jnp.right_shift(key, CHUNK_SHIFT) <= jnp.right_shift(qry, CHUNK_SHIFT)
            shared["base"] = jnp.where(allowed, jnp.maximum(key - qry, 0).astype(F32), -NEG)
        return (-2.0 * slopes2[h]) * shared["base"]

    ring = q_ref.shape[0] // t
    tiles = []
    for u in range(ring):
        streams = []
        for h in range(hb):
            slope_pieces = [p.astype(F32) for p in _split3(jnp.full((1, LANES), slopes2[h], F32))]
            factors = [p * float(CHUNK) for p in slope_pieces] + slope_pieces
            qa, qb = _pack_queries(q_ref[u * t:(u + 1) * t, h * LANES:(h + 1) * LANES], factors)
            vt = lambda k0, size, h=h: vt_ref[h, :, pl.ds(k0, size)]
            for m, (q, k_sc) in enumerate(((qa, ka_ref), (qb, kb_ref))):
                n_stream = (u * hb + h) * 2 + m
                streams.append((q, k_sc.at[h], vt, h, (m_ref.at[n_stream], acc_ref.at[n_stream])))
        tiles.append(streams)

    def finish_head(u, h):
        lp = lam_ref[...]
        lam = (jnp.exp(jnp.sum(lp[0:1] * lp[1:2], axis=-1, keepdims=True))
               - jnp.exp(jnp.sum(lp[2:3] * lp[3:4], axis=-1, keepdims=True)) + lambda_init)
        pa = acc_ref[(u * hb + h) * 2]
        pb = acc_ref[(u * hb + h) * 2 + 1]
        out = pa[0:dv] / pa[dv:dv + 1] - lam * (pb[0:dv] / pb[dv:dv + 1])
        ms = jnp.mean(out * out, axis=0, keepdims=True)
        out = out * lax.rsqrt(ms + EPS) * (sg_ref[...] * (1.0 - lambda_init))
        o_ref[u * t:(u + 1) * t, h * LANES:(h + 1) * LANES] = out.T.astype(BF16)

    _attend(i + 1, t, tiles, (s0_ref, s1_ref), (mx0_ref, mx1_ref), last_step_addend, finish_head)


def _da_call(z, slopes, lam_p, subln_g, heads, lambda_init):
    b, s, _ = z.shape
    t, tk = ATTN_TILE, ATTN_KEYS
    assert s % tk == 0 and t % CHUNK == 0 and tk == ATTN_RING * t
    assert s <= CHUNK * 256, "key position // CHUNK must stay exact in bf16"
    hb = min(ATTN_BLOCKS, heads)
    groups = heads // hb
    assert heads % hb == 0
    kern = functools.partial(_da_kernel, t=t, hb=hb, lambda_init=lambda_init)
    w = hb * LANES
    n_streams = 2 * hb * ATTN_RING
    return pl.pallas_call(
        kern,
        grid_spec=pltpu.PrefetchScalarGridSpec(
            num_scalar_prefetch=1,
            grid=(b, groups, s // tk),
            in_specs=[
                pl.BlockSpec((None, tk, w), lambda bi, g, i, sl: (bi, i, g)),
                pl.BlockSpec((None, s, w), lambda bi, g, i, sl: (bi, 0, groups + g)),
                pl.BlockSpec((None, s, w), lambda bi, g, i, sl: (bi, 0, 2 * groups + g)),
                pl.BlockSpec((4, HEAD_DIM), lambda bi, g, i, sl: (0, 0)),
                pl.BlockSpec((LANES, 1), lambda bi, g, i, sl: (0, 0)),
            ],
            out_specs=pl.BlockSpec((None, tk, w), lambda bi, g, i, sl: (bi, i, g)),
            scratch_shapes=[
                pltpu.VMEM((hb, s, LANES), BF16), pltpu.VMEM((hb, s, LANES), BF16),
                pltpu.VMEM((hb, LANES + ONES_ROWS, s), BF16),
                pltpu.VMEM((n_streams, 1, t), F32),
                pltpu.VMEM((n_streams, LANES + ONES_ROWS, t), F32),
                pltpu.VMEM((2 * hb, ATTN_KEYS, t), F32), pltpu.VMEM((2 * hb, ATTN_KEYS, t), F32),
                pltpu.VMEM((2 * hb, 1, t), F32), pltpu.VMEM((2 * hb, 1, t), F32),
            ],
        ),
        out_shape=jax.ShapeDtypeStruct((b, s, heads * LANES), BF16),
        compiler_params=_params("arbitrary", "arbitrary", "arbitrary"),
        name="diff_attention",
    )(slopes, z, z, z, lam_p, subln_g.reshape(LANES, 1))


def _fox_gate_kernel(g_ref, b_ref, o_ref, *, chunk):
    s = g_ref.shape[0]
    r = lax.broadcasted_iota(jnp.int32, (chunk, chunk), 0)
    c = lax.broadcasted_iota(jnp.int32, (chunk, chunk), 1)
    tri = (c <= r).astype(BF16)
    carry = jnp.zeros((1, LANES), F32)
    for c0 in range(0, s, chunk):
        ls = _log_sigmoid(g_ref[c0:c0 + chunk, :] + b_ref[...])
        cs = _exact_lhs_dot(tri, ls) + carry
        carry = cs[chunk - 1:chunk, :]
        o_ref[c0:c0 + chunk, :] = cs * (-LOG2E)


def _fox_gate_call(gates, bias_row):
    b, s, _ = gates.shape
    chunk = min(GATE_CHUNK, s)
    assert s % chunk == 0
    kern = functools.partial(_fox_gate_kernel, chunk=chunk)
    return pl.pallas_call(
        kern,
        grid=(b,),
        in_specs=[
            pl.BlockSpec((None, s, LANES), lambda i: (i, 0, 0)),
            pl.BlockSpec((1, LANES), lambda i: (0, 0)),
        ],
        out_specs=pl.BlockSpec((None, s, LANES), lambda i: (i, 0, 0)),
        out_shape=jax.ShapeDtypeStruct((b, s, LANES), F32),
        compiler_params=_params("arbitrary"),
        name="fox_gate_cumsum",
    )(gates, bias_row)


def _fox_kernel(q_ref, k_ref, v_ref, f_ref, o_ref,
                ka_ref, kb_ref, vt_ref, m_ref, acc_ref, s0_ref, s1_ref, mx0_ref, mx1_ref, *, t, hb):
    dv = HEAD_DIM
    g = pl.program_id(1)
    i = pl.program_id(2)

    @pl.when(i == 0)
    def _():
        half = LANES // 2
        src = lax.broadcasted_iota(jnp.int32, (2 * LANES, LANES), 0)
        dst = lax.broadcasted_iota(jnp.int32, (2 * LANES, LANES), 1)
        piece = jnp.where(src < LANES, 0, jnp.where(src < LANES + half, 1, 2))
        head = src - jnp.where(src < LANES, 0, jnp.where(src < LANES + half, LANES, LANES + half))

        def pieces_of(r0, rows):
            x = f_ref[pl.ds(r0, rows), :]
            x1 = x.astype(BF16)
            r1 = x - x1.astype(F32)
            x2 = r1.astype(BF16).astype(F32)
            tail = jnp.where(_lane_ids() < half, x2, pltpu.roll(r1 - x2, half, axis=1))
            return jnp.concatenate([x1, tail.astype(BF16)], axis=-1)

        blocks = []
        for h in range(hb):
            pair = g * hb + h
            place_a = ((head == 2 * pair) & (dst == HEAD_DIM + piece)).astype(BF16)
            place_b = ((head == 2 * pair + 1) & (dst == piece)).astype(BF16)
            place = jnp.concatenate([place_a, place_b], axis=-1)

            def extras(pieces, place=place):
                both = _dot(pieces, place)
                return both[:, 0:LANES], both[:, LANES:2 * LANES]
            blocks.append((_lane_block(k_ref, h), _lane_block(v_ref, h), ka_ref.at[h], kb_ref.at[h],
                           [(vt_ref.at[2 * h + m], m * dv, dv) for m in range(2)], extras))
        _pack_keys(blocks, pieces_of)

    def last_step_addend(_, shared, rows, offset):
        key = lax.broadcasted_iota(jnp.int32, (rows, t), 0)
        qry = lax.broadcasted_iota(jnp.int32, (rows, t), 1) + offset
        return jnp.where(key <= qry, 0.0, NEG)

    ring = q_ref.shape[0] // t
    ones = [jnp.ones((1, LANES), F32)] * BIAS_PIECES
    tiles = []
    for u in range(ring):
        streams = []
        for h in range(hb):
            qa, qb = _pack_queries(q_ref[u * t:(u + 1) * t, h * LANES:(h + 1) * LANES], ones)
            for m, (q, k_sc) in enumerate(((qa, ka_ref), (qb, kb_ref))):
                vt = lambda k0, size, n=2 * h + m: vt_ref[n, :, pl.ds(k0, size)]
                n_stream = (u * hb + h) * 2 + m
                streams.append((q, k_sc.at[h], vt, 0, (m_ref.at[n_stream], acc_ref.at[n_stream])))
        tiles.append(streams)

    def finish_pair(u, h):
        pa = acc_ref[(u * hb + h) * 2]
        pb = acc_ref[(u * hb + h) * 2 + 1]
        out = jnp.concatenate([pa[0:dv] / pa[dv:dv + 1], pb[0:dv] / pb[dv:dv + 1]], axis=0)
        o_ref[u * t:(u + 1) * t, h * LANES:(h + 1) * LANES] = out.T.astype(BF16)

    _attend(i + 1, t, tiles, (s0_ref, s1_ref), (mx0_ref, mx1_ref), last_step_addend, finish_pair)


def _fox_call(z, fneg, heads):
    b, s, _ = z.shape
    pairs = heads // 2
    t, tk = ATTN_TILE, ATTN_KEYS
    assert s % tk == 0 and tk == ATTN_RING * t
    assert heads <= LANES // 2, "the forget-gate pieces of all heads share 128 + 64 + 64 contraction rows"
    hb = min(ATTN_BLOCKS, pairs)
    groups = pairs // hb
    assert pairs % hb == 0
    kern = functools.partial(_fox_kernel, t=t, hb=hb)
    w = hb * LANES
    n_streams = 2 * hb * ATTN_RING
    return pl.pallas_call(
        kern,
        grid=(b, groups, s // tk),
        in_specs=[
            pl.BlockSpec((None, tk, w), lambda bi, g, i: (bi, i, g)),
            pl.BlockSpec((None, s, w), lambda bi, g, i: (bi, 0, groups + g)),
            pl.BlockSpec((None, s, w), lambda bi, g, i: (bi, 0, 2 * groups + g)),
            pl.BlockSpec((None, s, LANES), lambda bi, g, i: (bi, 0, 0)),
        ],
        out_specs=pl.BlockSpec((None, tk, w), lambda bi, g, i: (bi, i, g)),
        out_shape=jax.ShapeDtypeStruct((b, s, pairs * LANES), BF16),
        scratch_shapes=[
            pltpu.VMEM((hb, s, LANES), BF16), pltpu.VMEM((hb, s, LANES), BF16),
            pltpu.VMEM((2 * hb, HEAD_DIM + ONES_ROWS, s), BF16),
            pltpu.VMEM((n_streams, 1, t), F32),
            pltpu.VMEM((n_streams, HEAD_DIM + ONES_ROWS, t), F32),
            pltpu.VMEM((2 * hb, ATTN_KEYS, t), F32), pltpu.VMEM((2 * hb, ATTN_KEYS, t), F32),
            pltpu.VMEM((2 * hb, 1, t), F32), pltpu.VMEM((2 * hb, 1, t), F32),
        ],
        compiler_params=_params("arbitrary", "arbitrary", "arbitrary"),
        name="forgetting_attention",
    )(z, z, z, fneg)


def _mlstm_kernel(q_ref, k_ref, v_ref, o_ref, g_ref, gb_ref, cw_ref, cb_ref, y_ref,
                  halo_ref, ct_ref, m_ref, *, heads, taps, lc):
    @pl.when(pl.program_id(1) == 0)
    def _():
        halo_ref[...] = jnp.zeros(halo_ref.shape, F32)
        ct_ref[...] = jnp.zeros(ct_ref.shape, F32)
        m_ref[...] = jnp.zeros(m_ref.shape, F32)

    rows = lambda ref, sub: ref.at[pl.ds(sub * lc, lc), :]
    edge = halo_ref[...]
    for sub in range(q_ref.shape[0] // lc):
        prep, edge = _mlstm_prep(rows(q_ref, sub), rows(k_ref, sub), rows(g_ref, sub), gb_ref,
                                 cw_ref, cb_ref, edge, heads, taps)
        for hd in range(heads):
            _mlstm_head(prep, rows(v_ref, sub), rows(o_ref, sub), rows(y_ref, sub),
                        ct_ref, m_ref, hd, heads)
    halo_ref[...] = edge


def _mlstm_prep(q_ref, k_ref, g_ref, gb_ref, cw_ref, cb_ref, halo, heads, taps):
    lc, width = q_ref.shape
    pad = halo.shape[0]

    r = lax.broadcasted_iota(jnp.int32, (lc, lc), 0)
    c = lax.broadcasted_iota(jnp.int32, (lc, lc), 1)
    causal = c <= r

    gcol = g_ref[...] + gb_ref[...]
    grow = gcol.T[:2 * heads, :]
    a_cols = _exact_lhs_dot(causal.astype(BF16), _log_sigmoid(gcol))
    a_rows = _exact_rhs_dot(_log_sigmoid(grow), (r <= c).astype(BF16))

    b_half = grow[0:heads, :] - a_rows[heads:2 * heads, :]
    b_rows = jnp.concatenate([b_half, b_half], axis=0)
    prefix_cols = _prefix_max_lanes(b_rows).T

    xb = jnp.concatenate([q_ref[...], k_ref[...]], axis=-1)
    xf = xb.astype(F32)
    conv = cb_ref[...] + xf * cw_ref[taps - 1:taps, :]
    for sft in range(1, taps):
        shifted = _dot((r - c == sft).astype(BF16), xb)
        conv = conv + shifted * cw_ref[taps - 1 - sft:taps - sft, :]
    edge = jnp.concatenate([halo, xf[0:pad, :]], axis=0)
    first = cb_ref[...]
    for j in range(taps):
        off = pad - (taps - 1) + j
        first = first + edge[off:off + pad, :] * cw_ref[j:j + 1, :]
    conv = jnp.concatenate([first, conv[pad:, :]], axis=0)
    qk = _silu(conv)
    prep = dict(qk=qk, gcol=gcol, a_cols=a_cols, b_rows=b_rows, causal=causal,
                prefix_cols=prefix_cols)
    return prep, xf[lc - pad:, :]


def _mlstm_head(prep, v_ref, o_ref, y_ref, ct_ref, m_ref, hd, heads):
    lc, width = v_ref.shape
    d = width // heads
    qk, gcol, a_cols, causal = prep["qk"], prep["gcol"], prep["a_cols"], prep["causal"]
    q = qk[:, hd * d:(hd + 1) * d]
    k = qk[:, width + hd * d:width + (hd + 1) * d] * (d ** -0.5)
    v = v_ref[:, hd * d:(hd + 1) * d]
    qb = q.astype(BF16)
    kb = k.astype(BF16)
    ig_c = gcol[:, hd:hd + 1]
    a_c = a_cols[:, heads + hd:heads + hd + 1]
    a_last = a_c[lc - 1:lc, :]
    ct = ct_ref[hd]
    m_prev = m_ref[hd:hd + 1, 0:1]
    x_c = jnp.maximum(prep["prefix_cols"][:, hd:hd + 1], m_prev)

    g_c = a_last - a_c + ig_c
    m_loc = jnp.max(g_c, axis=0, keepdims=True)
    w_c = jnp.exp(g_c - m_loc)
    m_new = jnp.maximum(a_last + m_prev, m_loc)
    decay = jnp.exp(a_last + m_prev - m_new)
    s_loc = jnp.exp(m_loc - m_new)
    wv = jnp.concatenate([w_c * v.astype(F32), jnp.broadcast_to(w_c, (lc, d))], axis=-1)
    increment = _dot_tn(kb, wv.astype(BF16))
    raw = _dot_nt(qb, kb)
    from_state = _dot(qb, ct.astype(BF16))

    decay_mat = jnp.exp2(jnp.where(causal, prep["b_rows"][hd:hd + 1, :] * LOG2E - x_c * LOG2E, NEG))
    sqk = (raw * decay_mat).astype(BF16)
    intra = _dot(sqk, jnp.concatenate([v, jnp.ones((lc, d), BF16)], axis=-1))
    inter = jnp.exp(m_prev - x_c) * from_state
    num = inter[:, 0:d] + intra[:, 0:d]
    den = inter[:, d:d + 1] + intra[:, d:d + 1]
    hval = num / jnp.maximum(jnp.abs(den), jnp.exp(-(a_c + x_c)))
    og = o_ref[:, hd * d:(hd + 1) * d].astype(F32)
    y_ref[:, hd * d:(hd + 1) * d] = (_sigmoid(og) * hval).astype(BF16)

    ct_ref[hd] = decay * ct + s_loc * increment
    m_ref[hd:hd + 1, :] = jnp.broadcast_to(m_new, (1, LANES))


def _mlstm_call(z, gates, gate_bias, conv_w, conv_b, heads, col0):
    b, s, _ = z.shape
    width = conv_w.shape[1] // 2
    lc = min(ML_CHUNK, s)
    rows = min(ML_CHUNKS_PER_STEP * lc, s)
    assert s % rows == 0 and rows % lc == 0 and col0 % width == 0 and width // heads == LANES
    assert 2 * heads == 8, "gate rows are handled as one 8-sublane tile"
    cb = col0 // width
    taps = conv_w.shape[0]
    kern = functools.partial(_mlstm_kernel, heads=heads, taps=taps, lc=lc)
    zspec = lambda off: pl.BlockSpec((None, rows, width), lambda bi, ci: (bi, ci, cb + off))
    return pl.pallas_call(
        kern,
        grid=(b, s // rows),
        in_specs=[
            zspec(0), zspec(1), zspec(2), zspec(3),
            pl.BlockSpec((None, rows, LANES), lambda bi, ci: (bi, ci, 0)),
            pl.BlockSpec((1, LANES), lambda bi, ci: (0, 0)),
            pl.BlockSpec((taps, 2 * width), lambda bi, ci: (0, 0)),
            pl.BlockSpec((1, 2 * width), lambda bi, ci: (0, 0)),
        ],
        out_specs=pl.BlockSpec((None, rows, width), lambda bi, ci: (bi, ci, 0)),
        out_shape=jax.ShapeDtypeStruct((b, s, width), BF16),
        scratch_shapes=[
            pltpu.VMEM((8, 2 * width), F32),
            pltpu.VMEM((heads, LANES, 2 * LANES), F32),
            pltpu.VMEM((2 * heads, LANES), F32),
        ],
        compiler_params=_params("arbitrary", "arbitrary"),
        name="mlstm",
    )(z, z, z, z, gates, gate_bias, conv_w, conv_b)


def _mix_ffn_kernel(*refs, n_in, chunk):
    y_refs = refs[:n_in]
    wo_refs = refs[n_in:2 * n_in]
    (x_ref, gm_ref, sh_ref, sc_ref, g_ref, gate_ref, w1_ref, w3_ref, w2_ref, o_ref) = refs[2 * n_in:]
    hidden = w1_ref.shape[1]
    sub = min(FFN_ROWS, x_ref.shape[0])

    def mixed(r0):
        mix = _dot(y_refs[0][r0:r0 + sub, :], wo_refs[0][...])
        for y_ref, w_ref in zip(y_refs[1:], wo_refs[1:]):
            mix = mix + _dot(y_ref[r0:r0 + sub, :], w_ref[...])
        x = x_ref[r0:r0 + sub, :] + gm_ref[...] * mix
        return x, _norm_modulate(x, g_ref[...], sh_ref[...], sc_ref[...]).astype(BF16)

    def ffn(r0, x, h):
        acc = None
        for c0 in range(0, hidden, chunk):
            a = _dot(h, w1_ref[:, c0:c0 + chunk])
            bb = _dot(h, w3_ref[:, c0:c0 + chunk])
            u = (_silu(a) * bb).astype(BF16)
            part = _dot(u, w2_ref[c0:c0 + chunk, :])
            acc = part if acc is None else acc + part
        o_ref[r0:r0 + sub, :] = x + gate_ref[...] * acc

    starts = list(range(0, x_ref.shape[0], sub))
    ready = mixed(starts[0])
    for k, r0 in enumerate(starts):
        current = ready
        if k + 1 < len(starts):
            ready = mixed(starts[k + 1])
        ffn(r0, *current)


def _mix_ffn_call(ys, wo_all, wo_layer, x, gate_mix, shift, scale, g, gate, w1, w3, w2, layer, name):
    b, s, d = x.shape
    hidden = w1.shape[2]
    tm = min(ROW_TILE, s)
    assert s % tm == 0 and hidden % FFN_CHUNK == 0
    assert all(y.shape[2] == ys[0].shape[2] for y in ys)
    kern = functools.partial(_mix_ffn_kernel, n_in=len(ys), chunk=FFN_CHUNK)
    vec = pl.BlockSpec((None, 1, d), lambda i, j: (i, 0, 0))
    rows = lambda width: pl.BlockSpec((None, tm, width), lambda i, j: (i, j, 0))
    stacked = lambda w: pl.BlockSpec((None,) + w.shape[1:], lambda i, j: (layer, 0, 0))
    wo_specs = [pl.BlockSpec((None, y.shape[2], d), lambda i, j, n=n: (wo_layer, n, 0))
                for n, y in enumerate(ys)]
    return pl.pallas_call(
        kern,
        grid=(b, s // tm),
        in_specs=[rows(y.shape[2]) for y in ys] + wo_specs + [
            rows(d), vec, vec, vec,
            pl.BlockSpec((1, d), lambda i, j: (0, 0)),
            vec, stacked(w1), stacked(w3), stacked(w2),
        ],
        out_specs=rows(d),
        out_shape=jax.ShapeDtypeStruct((b, s, d), F32),
        compiler_params=_params("arbitrary", "arbitrary"),
        name=name,
    )(*ys, *([wo_all] * len(ys)), x, gate_mix, shift, scale, g, gate, w1, w3, w2)


def _pad_cols(w, n):
    return jnp.pad(w, ((0, 0), (0, n - w.shape[1])))


def _gain_row(q_g, k_g, width, total):
    reps = width // HEAD_DIM
    row = jnp.concatenate([jnp.tile(q_g.astype(F32) * LOG2E, reps),
                           jnp.tile(k_g.astype(F32) * HEAD_DIM ** 0.5, reps),
                           jnp.ones((total - 2 * width,), F32)])
    return row.reshape(1, total)


def kernel(x, c, ada_w, ada_b, norm_mix_g, norm_ffn_g, ab_w_in, ml_b_i, ml_b_f, ml_conv_w, ml_conv_b,
           da_q_g, da_k_g, da_lambda, da_subln_g, ab_w_out, fx_w_in, fx_b_f, fx_q_g, fx_k_g, fx_w_out,
           ffn_w1, ffn_w3, ffn_w2):
    depth = ada_w.shape[0]
    b, s, d = x.shape
    ml_width = ml_conv_w.shape[2] // 2
    ml_heads = ml_b_i.shape[1]
    da_width = (ab_w_in.shape[2] - 4 * ml_width - 2 * ml_heads) // 3
    da_heads = da_width // (2 * HEAD_DIM)
    fx_heads = fx_b_f.shape[1]
    fx_width = fx_heads * HEAD_DIM
    ab_main = 3 * da_width + 4 * ml_width
    fx_main = 3 * fx_width

    mods = _ada_call(c, ada_w, ada_b).reshape(depth, b, 6, 1, d)
    slopes = 2.0 ** (-8.0 * jnp.arange(1, da_heads + 1, dtype=F32) / da_heads)

    ab_w_in_b, fx_w_in_b = ab_w_in.astype(BF16), fx_w_in.astype(BF16)
    ab_w_out_b, fx_w_out_b = ab_w_out.astype(BF16), fx_w_out.astype(BF16)
    ffn_w1_b, ffn_w3_b, ffn_w2_b = ffn_w1.astype(BF16), ffn_w3.astype(BF16), ffn_w2.astype(BF16)

    for l in range(depth):
        sh_m, sc_m, g_m, sh_f, sc_f, g_f = (mods[l, :, t] for t in range(6))
        j = l // 2
        g_mix = norm_mix_g[l].reshape(1, d)
        if l % 2 == 0:
            z, gates = _inproj_call(
                x, sh_m, sc_m, g_mix, ab_w_in_b, j, ab_main,
                _pad_cols(ab_w_in[j][:, ab_main:], LANES).astype(BF16),
                _gain_row(da_q_g[j], da_k_g[j], da_width, ab_main),
                2 * da_width, "inproj_even")
            lambda_init = 0.8 - 0.6 * math.exp(-0.3 * l)
            y_da = _da_call(z, slopes, da_lambda[j], da_subln_g[j], da_heads, lambda_init)
            gate_bias = _pad_cols(jnp.concatenate([ml_b_i[j], ml_b_f[j]]).reshape(1, -1), LANES)
            y_ml = _mlstm_call(z, gates, gate_bias, ml_conv_w[j], ml_conv_b[j].reshape(1, -1),
                               ml_heads, 3 * da_width)
            ys, wo_all, name = [y_da, y_ml], ab_w_out_b, "mix_ffn_even"
        else:
            z, gates = _inproj_call(
                x, sh_m, sc_m, g_mix, fx_w_in_b, j, fx_main,
                _pad_cols(fx_w_in[j][:, fx_main:], LANES).astype(BF16),
                _gain_row(fx_q_g[j], fx_k_g[j], fx_width, fx_main),
                2 * fx_width, "inproj_odd")
            fneg = _fox_gate_call(gates, _pad_cols(fx_b_f[j].reshape(1, -1), LANES))
            ys, wo_all, name = [_fox_call(z, fneg, fx_heads)], fx_w_out_b, "mix_ffn_odd"
        x = _mix_ffn_call(ys, wo_all, j, x, g_m, sh_f, sc_f, norm_ffn_g[l].reshape(1, d), g_f,
                          ffn_w1_b, ffn_w3_b, ffn_w2_b, l, name)
    return x
```

```python
import functools
import math

import jax
import jax.numpy as jnp
from jax import lax
from jax.experimental import pallas as pl
from jax.experimental.pallas import tpu as pltpu

F32 = jnp.float32
BF16 = jnp.bfloat16

EPS = 1e-6
LANES = 128
HEAD_DIM = 64
CHUNK = 64
CHUNK_SHIFT = CHUNK.bit_length() - 1
assert CHUNK == 1 << CHUNK_SHIFT
NEG = -1e30
LOG2E = math.log2(math.e)
BIAS_PIECES = 3
ONES_ROWS = 16

VMEM_LIMIT = 56 * 1024 * 1024

ADA_COLS = 1536
ROW_TILE = 1024
FFN_ROWS = 512
INPROJ_ROWS = 1024
INPROJ_CHUNK = 512
ATTN_TILE = 256
ATTN_RING = 2
ATTN_KEYS = ATTN_RING * ATTN_TILE
ATTN_CHAIN = 2
assert ATTN_CHAIN % 2 == 0
ATTN_BLOCKS = 4
PACK_ROWS = 512
ML_CHUNK = 256
ML_CHUNKS_PER_STEP = 2
GATE_CHUNK = 256
FFN_CHUNK = 256


def _params(*sem):
    return pltpu.CompilerParams(dimension_semantics=sem, vmem_limit_bytes=VMEM_LIMIT)


def _log_sigmoid(x):
    return jnp.minimum(x, 0.0) - jnp.log1p(jnp.exp(-jnp.abs(x)))


def _sigmoid(x):
    return 0.5 * jnp.tanh(0.5 * x) + 0.5


def _silu(x):
    h = 0.5 * x
    return h * (jnp.tanh(h) + 1.0)


def _prefix_max_lanes(x):
    axis = x.ndim - 1
    lane = lax.broadcasted_iota(jnp.int32, x.shape, axis)
    shift = 1
    while shift < x.shape[axis]:
        x = jnp.maximum(x, jnp.where(lane >= shift, pltpu.roll(x, shift, axis=axis), NEG))
        shift *= 2
    return x


def _split3(x):
    x1 = x.astype(BF16)
    r1 = x - x1.astype(F32)
    x2 = r1.astype(BF16)
    x3 = (r1 - x2.astype(F32)).astype(BF16)
    return x1, x2, x3


def _dot(a, b):
    return jnp.dot(a, b, preferred_element_type=F32)


def _dot_nt(a, b):
    return lax.dot_general(a, b, (((1,), (1,)), ((), ())), preferred_element_type=F32)


def _dot_tn(a, b):
    return lax.dot_general(a, b, (((0,), (0,)), ((), ())), preferred_element_type=F32)


def _exact_lhs_dot(m_bf16, x):
    x1, x2, x3 = _split3(x)
    return _dot(m_bf16, x1) + _dot(m_bf16, x2) + _dot(m_bf16, x3)


def _exact_rhs_dot(x, m_bf16):
    x1, x2, x3 = _split3(x)
    return _dot(x1, m_bf16) + _dot(x2, m_bf16) + _dot(x3, m_bf16)


def _norm_modulate(x, g, sh, sc):
    ms = jnp.mean(x * x, axis=-1, keepdims=True)
    return x * lax.rsqrt(ms + EPS) * (g * (1.0 + sc)) + sh


def _ada_kernel(c_ref, w_ref, b_ref, o_ref):
    c = c_ref[...]
    ca = _silu(c)
    c1, c2, c3 = _split3(ca)
    w = w_ref[...]
    w1 = w.astype(BF16)
    w2 = (w - w1.astype(F32)).astype(BF16)
    acc = _dot(c1, w1) + _dot(c1, w2) + _dot(c2, w1) + _dot(c2, w2) + _dot(c3, w1)
    o_ref[...] = acc + b_ref[...]


def _ada_call(c, ada_w, ada_b):
    depth, d, n = ada_w.shape
    b = c.shape[0]
    tn = min(ADA_COLS, n)
    assert n % tn == 0
    return pl.pallas_call(
        _ada_kernel,
        grid=(depth, n // tn),
        in_specs=[
            pl.BlockSpec((b, d), lambda l, j: (0, 0)),
            pl.BlockSpec((None, d, tn), lambda l, j: (l, 0, j)),
            pl.BlockSpec((None, 1, tn), lambda l, j: (l, 0, j)),
        ],
        out_specs=pl.BlockSpec((None, b, tn), lambda l, j: (l, 0, j)),
        out_shape=jax.ShapeDtypeStruct((depth, b, n), F32),
        compiler_params=_params("arbitrary", "arbitrary"),
        name="ada_mod",
    )(c, ada_w, ada_b.reshape(depth, 1, n))


def _inproj_kernel(x_ref, sh_ref, sc_ref, g_ref, w_ref, wg_ref, gain_ref, z_ref, gate_ref,
                   *, n_norm, chunk):
    half = x_ref.shape[0] // 2
    h_rows, first_rows = [], []
    for r0 in (0, half):
        h_part = _norm_modulate(x_ref[r0:r0 + half, :], g_ref[...], sh_ref[...], sc_ref[...]).astype(BF16)
        first_rows.append(_dot(h_part, w_ref[:, 0:chunk]))
        h_rows.append(h_part)
    h = jnp.concatenate(h_rows, axis=0)
    n = w_ref.shape[1]
    lane = lax.broadcasted_iota(jnp.int32, (1, LANES), 1)
    lo = lane < HEAD_DIM
    for c0 in range(0, n, chunk):
        zc = jnp.concatenate(first_rows, axis=0) if c0 == 0 else _dot(h, w_ref[:, c0:c0 + chunk])
        if c0 < n_norm:
            parts = []
            for s0 in range(0, chunk, LANES):
                zs = zc[:, s0:s0 + LANES]
                sq = zs * zs
                s_lo = jnp.sum(jnp.where(lo, sq, 0.0), axis=-1, keepdims=True)
                s_hi = jnp.sum(jnp.where(lo, 0.0, sq), axis=-1, keepdims=True)
                r_lo = lax.rsqrt(s_lo + HEAD_DIM * EPS)
                r_hi = lax.rsqrt(s_hi + HEAD_DIM * EPS)
                parts.append(zs * jnp.where(lo, r_lo, r_hi))
            zc = jnp.concatenate(parts, axis=-1) * gain_ref[:, c0:c0 + chunk]
        z_ref[:, c0:c0 + chunk] = zc.astype(BF16)
    gate_ref[...] = _dot(h, wg_ref[...])


def _inproj_call(x, shift, scale, g, w_all, layer, n, wg, gain, n_norm, name):
    b, s, d = x.shape
    tm = min(INPROJ_ROWS, s)
    chunk = INPROJ_CHUNK
    assert s % tm == 0 and tm % 2 == 0 and n % chunk == 0 and n_norm % chunk == 0
    kern = functools.partial(_inproj_kernel, n_norm=n_norm, chunk=chunk)
    return pl.pallas_call(
        kern,
        grid=(b, s // tm),
        in_specs=[
            pl.BlockSpec((None, tm, d), lambda i, j: (i, j, 0)),
            pl.BlockSpec((None, 1, d), lambda i, j: (i, 0, 0)),
            pl.BlockSpec((None, 1, d), lambda i, j: (i, 0, 0)),
            pl.BlockSpec((1, d), lambda i, j: (0, 0)),
            pl.BlockSpec((None, d, n), lambda i, j: (layer, 0, 0)),
            pl.BlockSpec((d, LANES), lambda i, j: (0, 0)),
            pl.BlockSpec((1, n), lambda i, j: (0, 0)),
        ],
        out_specs=[
            pl.BlockSpec((None, tm, n), lambda i, j: (i, j, 0)),
            pl.BlockSpec((None, tm, LANES), lambda i, j: (i, j, 0)),
        ],
        out_shape=[
            jax.ShapeDtypeStruct((b, s, n), BF16),
            jax.ShapeDtypeStruct((b, s, LANES), F32),
        ],
        compiler_params=_params("arbitrary", "arbitrary"),
        name=name,
    )(x, shift, scale, g, w_all, wg, gain)


def _lane_ids():
    return lax.broadcasted_iota(jnp.int32, (1, LANES), 1)


def _place_pieces(pieces, first_lane):
    lane = _lane_ids()
    out = jnp.zeros(pieces[0].shape, F32)
    for j, p in enumerate(pieces):
        out = jnp.where(lane == first_lane + j, p.astype(F32), out)
    return out


def _pack_keys(blocks, shared_fn):
    s = blocks[0][0].shape[0]
    rows = min(PACK_ROWS, s)
    keep_lo = jnp.where(_lane_ids() < HEAD_DIM, 1.0, 0.0).astype(BF16)
    keep_hi = jnp.where(_lane_ids() < HEAD_DIM, 0.0, 1.0).astype(BF16)

    def body(c, carry):
        r0 = pl.multiple_of(c * rows, rows)
        shared = shared_fn(r0, rows)
        for k_ref, v_ref, ka_ref, kb_ref, vt_refs, extras_fn in blocks:
            k2 = k_ref[pl.ds(r0, rows), :]
            ea, eb = extras_fn(shared)
            ka_ref[pl.ds(r0, rows), :] = k2 * keep_lo + ea.astype(BF16)
            kb_ref[pl.ds(r0, rows), :] = k2 * keep_hi + eb.astype(BF16)
            vt = v_ref[pl.ds(r0, rows), :].T
            for ref, c0, width in vt_refs:
                ref[0:width, pl.ds(r0, rows)] = vt[c0:c0 + width, :]
                ref[width:width + ONES_ROWS, pl.ds(r0, rows)] = jnp.ones((ONES_ROWS, rows), BF16)
        return carry

    lax.fori_loop(0, s // rows, body, 0)


def _pack_queries(q2, factors):
    lane = _lane_ids()
    q2 = q2.astype(F32)
    lo = lane < HEAD_DIM
    fac_a = _place_pieces(factors, HEAD_DIM)
    fac_b = _place_pieces(factors, 0)
    return jnp.where(lo, q2, fac_a).astype(BF16), jnp.where(lo, fac_b, q2).astype(BF16)


def _init_stats(m_ref, acc_ref):
    m_ref[...] = jnp.full(m_ref.shape, NEG, F32)
    acc_ref[...] = jnp.zeros(acc_ref.shape, F32)


def _softmax_step(s, tile_max, vt, m_ref, acc_ref):
    m_old = m_ref[...]
    m_new = jnp.maximum(m_old, tile_max)
    alpha = jnp.exp2(m_old - m_new)
    p = jnp.exp2(s - m_new)
    acc_ref[...] = alpha * acc_ref[...] + _dot(vt, p.astype(BF16))
    m_ref[...] = m_new


def _attend(n0, t, chain, s_buf, mx_buf, make_addend, finish_pair):
    tk = ATTN_KEYS
    first = ATTN_RING - 1
    built = {}

    def group(c):
        if c not in built:
            built[c] = chain[c]()
            for streams in built[c]:
                for stream in streams:
                    _init_stats(*stream[4])
        return built[c]

    def qk_one(tiles, u, step, idx, rows=tk):
        k0 = pl.multiple_of(step * tk, tk)
        q, k_ref = tiles[u][idx][0], tiles[u][idx][1]
        s = _dot_nt(k_ref[pl.ds(k0, rows), :], q)
        s_buf[u][idx, 0:rows, :] = s
        if rows == tk:
            mx_buf[u][idx] = jnp.max(s, axis=0, keepdims=True)

    def stage(c, u, step, partner, last=False):
        tiles = group(c)
        k0 = pl.multiple_of(step * tk, tk)
        rows, offset = (u + 1) * t, u * t
        addends = {}
        for idx, (_, _, vt, add_key, stats) in enumerate(tiles[u]):
            if partner is not None:
                p_tiles, p_u, p_step, p_rows = partner
                qk_one(p_tiles, p_u, p_step, idx, p_rows)
            if not last:
                _softmax_step(s_buf[u][idx], mx_buf[u][idx], vt(k0, tk), *stats)
                continue
            if add_key not in addends:
                addends[add_key] = make_addend(add_key, addends.setdefault("shared", {}), rows, offset)
            s = s_buf[u][idx, 0:rows, :] + addends[add_key]
            _softmax_step(s, jnp.max(s, axis=0, keepdims=True), vt(k0, rows), *stats)
            if idx % 2 == 1:
                finish_pair(c, u, idx // 2)

    for idx in range(len(group(0)[first])):
        qk_one(group(0), first, 0, idx)

    for c in range(len(chain)):
        n = n0 + c
        tiles = group(c)

        def ring_pass(j, c=c, tiles=tiles):
            for u in range(first, -1, -1):
                stage(c, u, j, (tiles, u - 1, j, tk) if u > 0 else (tiles, first, j + 1, tk))

        def body(p, carry, ring_pass=ring_pass):
            ring_pass(2 * p)
            ring_pass(2 * p + 1)
            return carry

        lax.fori_loop(0, (n - 1) // 2, body, 0)
        if c % 2 == 1:
            ring_pass(n - 2)
        for u in range(first, -1, -1):
            if u > 0:
                partner = (tiles, u - 1, n - 1, u * t)
            elif c + 1 < len(chain):
                partner = (group(c + 1), first, 0, tk)
            else:
                partner = None
            stage(c, u, n - 1, partner, last=True)


def _lane_block(ref, h):
    return ref.at[:, pl.ds(h * LANES, LANES)]


def _da_kernel(slope_ref, q_ref, k_ref, v_ref, lam_ref, sg_ref, o_ref,
               ka_ref, kb_ref, vt_ref, m_ref, acc_ref, s0_ref, s1_ref, mx0_ref, mx1_ref,
               *, t, hb, lambda_init):
    dv = LANES
    g = pl.program_id(1)
    i = pl.program_id(2)
    slopes2 = [slope_ref[g * hb + h] * LOG2E for h in range(hb)]

    @pl.when(i == 0)
    def _():
        def position_pieces(r0, rows):
            pos = lax.broadcasted_iota(jnp.int32, (rows, LANES), 0) + r0
            hi = jnp.right_shift(pos, CHUNK_SHIFT).astype(F32)
            lo = jnp.bitwise_and(pos, CHUNK - 1).astype(F32)
            lane = _lane_ids()

            def place(first):
                in_hi = (lane >= first) & (lane < first + BIAS_PIECES)
                in_lo = (lane >= first + BIAS_PIECES) & (lane < first + 2 * BIAS_PIECES)
                return jnp.where(in_hi, hi, jnp.where(in_lo, lo, 0.0)).astype(BF16)
            return place(HEAD_DIM), place(0)
        _pack_keys([(_lane_block(k_ref, h), _lane_block(v_ref, h), ka_ref.at[h], kb_ref.at[h],
                     [(vt_ref.at[h], 0, dv)], lambda shared: shared) for h in range(hb)],
                   position_pieces)

    def last_step_addend(h, shared, rows, offset):
        if "base" not in shared:
            key = lax.broadcasted_iota(jnp.int32, (rows, t), 0)
            qry = lax.broadcasted_iota(jnp.int32, (rows, t), 1) + offset
            allowed = jnp.right_shift(key, CHUNK_SHIFT) <= jnp.right_shift(qry, CHUNK_SHIFT)
            shared["base"] = jnp.where(allowed, jnp.maximum(key - qry, 0).astype(F32), -NEG)
        return (-2.0 * slopes2[h]) * shared["base"]

    def build_group(c):
        tiles = []
        for u in range(ATTN_RING):
            tile = c * ATTN_RING + u
            streams = []
            for h in range(hb):
                slope_pieces = [p.astype(F32) for p in _split3(jnp.full((1, LANES), slopes2[h], F32))]
                factors = [p * float(CHUNK) for p in slope_pieces] + slope_pieces
                qa, qb = _pack_queries(q_ref[tile * t:(tile + 1) * t, h * LANES:(h + 1) * LANES], factors)
                vt = lambda k0, size, h=h: vt_ref[h, :, pl.ds(k0, size)]
                for m, (q, k_sc) in enumerate(((qa, ka_ref), (qb, kb_ref))):
                    n_stream = (tile * hb + h) * 2 + m
                    streams.append((q, k_sc.at[h], vt, h, (m_ref.at[n_stream], acc_ref.at[n_stream])))
            tiles.append(streams)
        return tiles

    def finish_head(c, u, h):
        tile = c * ATTN_RING + u
        lp = lam_ref[...]
        lam = (jnp.exp(jnp.sum(lp[0:1] * lp[1:2], axis=-1, keepdims=True))
               - jnp.exp(jnp.sum(lp[2:3] * lp[3:4], axis=-1, keepdims=True)) + lambda_init)
        pa = acc_ref[(tile * hb + h) * 2]
        pb = acc_ref[(tile * hb + h) * 2 + 1]
        out = pa[0:dv] / pa[dv:dv + 1] - lam * (pb[0:dv] / pb[dv:dv + 1])
        ms = jnp.mean(out * out, axis=0, keepdims=True)
        out = out * lax.rsqrt(ms + EPS) * (sg_ref[...] * (1.0 - lambda_init))
        o_ref[tile * t:(tile + 1) * t, h * LANES:(h + 1) * LANES] = out.T.astype(BF16)

    chain = [functools.partial(build_group, c) for c in range(ATTN_CHAIN)]
    _attend(ATTN_CHAIN * i + 1, t, chain, (s0_ref, s1_ref), (mx0_ref, mx1_ref),
            last_step_addend, finish_head)


def _da_call(z, slopes, lam_p, subln_g, heads, lambda_init):
    b, s, _ = z.shape
    t, tk = ATTN_TILE, ATTN_KEYS
    assert s % tk == 0 and t % CHUNK == 0 and tk == ATTN_RING * t
    assert s <= CHUNK * 256, "key position // CHUNK must stay exact in bf16"
    hb = min(ATTN_BLOCKS, heads)
    groups = heads // hb
    assert heads % hb == 0
    kern = functools.partial(_da_kernel, t=t, hb=hb, lambda_init=lambda_init)
    w = hb * LANES
    n_streams = 2 * hb * ATTN_RING * ATTN_CHAIN
    rows = ATTN_CHAIN * tk
    assert s % rows == 0
    return pl.pallas_call(
        kern,
        grid_spec=pltpu.PrefetchScalarGridSpec(
            num_scalar_prefetch=1,
            grid=(b, groups, s // rows),
            in_specs=[
                pl.BlockSpec((None, rows, w), lambda bi, g, i, sl: (bi, i, g)),
                pl.BlockSpec((None, s, w), lambda bi, g, i, sl: (bi, 0, groups + g)),
                pl.BlockSpec((None, s, w), lambda bi, g, i, sl: (bi, 0, 2 * groups + g)),
                pl.BlockSpec((4, HEAD_DIM), lambda bi, g, i, sl: (0, 0)),
                pl.BlockSpec((LANES, 1), lambda bi, g, i, sl: (0, 0)),
            ],
            out_specs=pl.BlockSpec((None, rows, w), lambda bi, g, i, sl: (bi, i, g)),
            scratch_shapes=[
                pltpu.VMEM((hb, s, LANES), BF16), pltpu.VMEM((hb, s, LANES), BF16),
                pltpu.VMEM((hb, LANES + ONES_ROWS, s), BF16),
                pltpu.VMEM((n_streams, 1, t), F32),
                pltpu.VMEM((n_streams, LANES + ONES_ROWS, t), F32),
                pltpu.VMEM((2 * hb, ATTN_KEYS, t), F32), pltpu.VMEM((2 * hb, ATTN_KEYS, t), F32),
                pltpu.VMEM((2 * hb, 1, t), F32), pltpu.VMEM((2 * hb, 1, t), F32),
            ],
        ),
        out_shape=jax.ShapeDtypeStruct((b, s, heads * LANES), BF16),
        compiler_params=_params("arbitrary", "arbitrary", "arbitrary"),
        name="diff_attention",
    )(slopes, z, z, z, lam_p, subln_g.reshape(LANES, 1))


def _fox_gate_kernel(g_ref, b_ref, o_ref, *, chunk):
    s = g_ref.shape[0]
    r = lax.broadcasted_iota(jnp.int32, (chunk, chunk), 0)
    c = lax.broadcasted_iota(jnp.int32, (chunk, chunk), 1)
    tri = (c <= r).astype(BF16)
    carry = jnp.zeros((1, LANES), F32)
    for c0 in range(0, s, chunk):
        ls = _log_sigmoid(g_ref[c0:c0 + chunk, :] + b_ref[...])
        cs = _exact_lhs_dot(tri, ls) + carry
        carry = cs[chunk - 1:chunk, :]
        o_ref[c0:c0 + chunk, :] = cs * (-LOG2E)


def _fox_gate_call(gates, bias_row):
    b, s, _ = gates.shape
    chunk = min(GATE_CHUNK, s)
    assert s % chunk == 0
    kern = functools.partial(_fox_gate_kernel, chunk=chunk)
    return pl.pallas_call(
        kern,
        grid=(b,),
        in_specs=[
            pl.BlockSpec((None, s, LANES), lambda i: (i, 0, 0)),
            pl.BlockSpec((1, LANES), lambda i: (0, 0)),
        ],
        out_specs=pl.BlockSpec((None, s, LANES), lambda i: (i, 0, 0)),
        out_shape=jax.ShapeDtypeStruct((b, s, LANES), F32),
        compiler_params=_params("arbitrary"),
        name="fox_gate_cumsum",
    )(gates, bias_row)


def _fox_kernel(q_ref, k_ref, v_ref, f_ref, o_ref,
                ka_ref, kb_ref, vt_ref, m_ref, acc_ref, s0_ref, s1_ref, mx0_ref, mx1_ref, *, t, hb):
    dv = HEAD_DIM
    g = pl.program_id(1)
    i = pl.program_id(2)

    @pl.when(i == 0)
    def _():
        half = LANES // 2
        src = lax.broadcasted_iota(jnp.int32, (2 * LANES, LANES), 0)
        dst = lax.broadcasted_iota(jnp.int32, (2 * LANES, LANES), 1)
        piece = jnp.where(src < LANES, 0, jnp.where(src < LANES + half, 1, 2))
        head = src - jnp.where(src < LANES, 0, jnp.where(src < LANES + half, LANES, LANES + half))

        def pieces_of(r0, rows):
            x = f_ref[pl.ds(r0, rows), :]
            x1 = x.astype(BF16)
            r1 = x - x1.astype(F32)
            x2 = r1.astype(BF16).astype(F32)
            tail = jnp.where(_lane_ids() < half, x2, pltpu.roll(r1 - x2, half, axis=1))
            return jnp.concatenate([x1, tail.astype(BF16)], axis=-1)

        blocks = []
        for h in range(hb):
            pair = g * hb + h
            place_a = ((head == 2 * pair) & (dst == HEAD_DIM + piece)).astype(BF16)
            place_b = ((head == 2 * pair + 1) & (dst == piece)).astype(BF16)
            place = jnp.concatenate([place_a, place_b], axis=-1)

            def extras(pieces, place=place):
                both = _dot(pieces, place)
                return both[:, 0:LANES], both[:, LANES:2 * LANES]
            blocks.append((_lane_block(k_ref, h), _lane_block(v_ref, h), ka_ref.at[h], kb_ref.at[h],
                           [(vt_ref.at[2 * h + m], m * dv, dv) for m in range(2)], extras))
        _pack_keys(blocks, pieces_of)

    def last_step_addend(_, shared, rows, offset):
        key = lax.broadcasted_iota(jnp.int32, (rows, t), 0)
        qry = lax.broadcasted_iota(jnp.int32, (rows, t), 1) + offset
        return jnp.where(key <= qry, 0.0, NEG)

    def build_group(c):
        ones = [jnp.ones((1, LANES), F32)] * BIAS_PIECES
        tiles = []
        for u in range(ATTN_RING):
            tile = c * ATTN_RING + u
            streams = []
            for h in range(hb):
                qa, qb = _pack_queries(q_ref[tile * t:(tile + 1) * t, h * LANES:(h + 1) * LANES], ones)
                for m, (q, k_sc) in enumerate(((qa, ka_ref), (qb, kb_ref))):
                    vt = lambda k0, size, n=2 * h + m: vt_ref[n, :, pl.ds(k0, size)]
                    n_stream = (tile * hb + h) * 2 + m
                    streams.append((q, k_sc.at[h], vt, 0, (m_ref.at[n_stream], acc_ref.at[n_stream])))
            tiles.append(streams)
        return tiles

    def finish_pair(c, u, h):
        tile = c * ATTN_RING + u
        pa = acc_ref[(tile * hb + h) * 2]
        pb = acc_ref[(tile * hb + h) * 2 + 1]
        out = jnp.concatenate([pa[0:dv] / pa[dv:dv + 1], pb[0:dv] / pb[dv:dv + 1]], axis=0)
        o_ref[tile * t:(tile + 1) * t, h * LANES:(h + 1) * LANES] = out.T.astype(BF16)

    chain = [functools.partial(build_group, c) for c in range(ATTN_CHAIN)]
    _attend(ATTN_CHAIN * i + 1, t, chain, (s0_ref, s1_ref), (mx0_ref, mx1_ref),
            last_step_addend, finish_pair)


def _fox_call(z, fneg, heads):
    b, s, _ = z.shape
    pairs = heads // 2
    t, tk = ATTN_TILE, ATTN_KEYS
    assert s % tk == 0 and tk == ATTN_RING * t
    assert heads <= LANES // 2, "the forget-gate pieces of all heads share 128 + 64 + 64 contraction rows"
    hb = min(ATTN_BLOCKS, pairs)
    groups = pairs // hb
    assert pairs % hb == 0
    kern = functools.partial(_fox_kernel, t=t, hb=hb)
    w = hb * LANES
    n_streams = 2 * hb * ATTN_RING * ATTN_CHAIN
    rows = ATTN_CHAIN * tk
    assert s % rows == 0
    return pl.pallas_call(
        kern,
        grid=(b, groups, s // rows),
        in_specs=[
            pl.BlockSpec((None, rows, w), lambda bi, g, i: (bi, i, g)),
            pl.BlockSpec((None, s, w), lambda bi, g, i: (bi, 0, groups + g)),
            pl.BlockSpec((None, s, w), lambda bi, g, i: (bi, 0, 2 * groups + g)),
            pl.BlockSpec((None, s, LANES), lambda bi, g, i: (bi, 0, 0)),
        ],
        out_specs=pl.BlockSpec((None, rows, w), lambda bi, g, i: (bi, i, g)),
        out_shape=jax.ShapeDtypeStruct((b, s, pairs * LANES), BF16),
        scratch_shapes=[
            pltpu.VMEM((hb, s, LANES), BF16), pltpu.VMEM((hb, s, LANES), BF16),
            pltpu.VMEM((2 * hb, HEAD_DIM + ONES_ROWS, s), BF16),
            pltpu.VMEM((n_streams, 1, t), F32),
            pltpu.VMEM((n_streams, HEAD_DIM + ONES_ROWS, t), F32),
            pltpu.VMEM((2 * hb, ATTN_KEYS, t), F32), pltpu.VMEM((2 * hb, ATTN_KEYS, t), F32),
            pltpu.VMEM((2 * hb, 1, t), F32), pltpu.VMEM((2 * hb, 1, t), F32),
        ],
        compiler_params=_params("arbitrary", "arbitrary", "arbitrary"),
        name="forgetting_attention",
    )(z, z, z, fneg)


def _mlstm_kernel(q_ref, k_ref, v_ref, o_ref, g_ref, gb_ref, cw_ref, cb_ref, y_ref,
                  halo_ref, ct_ref, m_ref, *, heads, taps, lc):
    @pl.when(pl.program_id(1) == 0)
    def _():
        halo_ref[...] = jnp.zeros(halo_ref.shape, F32)
        ct_ref[...] = jnp.zeros(ct_ref.shape, F32)
        m_ref[...] = jnp.zeros(m_ref.shape, F32)

    rows = lambda ref, sub: ref.at[pl.ds(sub * lc, lc), :]
    edge = halo_ref[...]
    for sub in range(q_ref.shape[0] // lc):
        prep, edge = _mlstm_prep(rows(q_ref, sub), rows(k_ref, sub), rows(g_ref, sub), gb_ref,
                                 cw_ref, cb_ref, edge, heads, taps)
        for hd in range(heads):
            _mlstm_head(prep, rows(v_ref, sub), rows(o_ref, sub), rows(y_ref, sub),
                        ct_ref, m_ref, hd, heads)
    halo_ref[...] = edge


def _mlstm_prep(q_ref, k_ref, g_ref, gb_ref, cw_ref, cb_ref, halo, heads, taps):
    lc, width = q_ref.shape
    pad = halo.shape[0]

    r = lax.broadcasted_iota(jnp.int32, (lc, lc), 0)
    c = lax.broadcasted_iota(jnp.int32, (lc, lc), 1)
    causal = c <= r

    gcol = g_ref[...] + gb_ref[...]
    grow = gcol.T[:2 * heads, :]
    a_cols = _exact_lhs_dot(causal.astype(BF16), _log_sigmoid(gcol))
    a_rows = _exact_rhs_dot(_log_sigmoid(grow), (r <= c).astype(BF16))

    b_half = grow[0:heads, :] - a_rows[heads:2 * heads, :]
    b_rows = jnp.concatenate([b_half, b_half], axis=0)
    prefix_cols = _prefix_max_lanes(b_rows).T

    xb = jnp.concatenate([q_ref[...], k_ref[...]], axis=-1)
    xf = xb.astype(F32)
    conv = cb_ref[...] + xf * cw_ref[taps - 1:taps, :]
    for sft in range(1, taps):
        shifted = _dot((r - c == sft).astype(BF16), xb)
        conv = conv + shifted * cw_ref[taps - 1 - sft:taps - sft, :]
    edge = jnp.concatenate([halo, xf[0:pad, :]], axis=0)
    first = cb_ref[...]
    for j in range(taps):
        off = pad - (taps - 1) + j
        first = first + edge[off:off + pad, :] * cw_ref[j:j + 1, :]
    conv = jnp.concatenate([first, conv[pad:, :]], axis=0)
    qk = _silu(conv)
    prep = dict(qk=qk, gcol=gcol, a_cols=a_cols, b_rows=b_rows, causal=causal,
                prefix_cols=prefix_cols)
    return prep, xf[lc - pad:, :]


def _mlstm_head(prep, v_ref, o_ref, y_ref, ct_ref, m_ref, hd, heads):
    lc, width = v_ref.shape
    d = width // heads
    qk, gcol, a_cols, causal = prep["qk"], prep["gcol"], prep["a_cols"], prep["causal"]
    q = qk[:, hd * d:(hd + 1) * d]
    k = qk[:, width + hd * d:width + (hd + 1) * d] * (d ** -0.5)
    v = v_ref[:, hd * d:(hd + 1) * d]
    qb = q.astype(BF16)
    kb = k.astype(BF16)
    ig_c = gcol[:, hd:hd + 1]
    a_c = a_cols[:, heads + hd:heads + hd + 1]
    a_last = a_c[lc - 1:lc, :]
    ct = ct_ref[hd]
    m_prev = m_ref[hd:hd + 1, 0:1]
    x_c = jnp.maximum(prep["prefix_cols"][:, hd:hd + 1], m_prev)

    g_c = a_last - a_c + ig_c
    m_loc = jnp.max(g_c, axis=0, keepdims=True)
    w_c = jnp.exp(g_c - m_loc)
    m_new = jnp.maximum(a_last + m_prev, m_loc)
    decay = jnp.exp(a_last + m_prev - m_new)
    s_loc = jnp.exp(m_loc - m_new)
    wv = jnp.concatenate([w_c * v.astype(F32), jnp.broadcast_to(w_c, (lc, d))], axis=-1)
    increment = _dot_tn(kb, wv.astype(BF16))
    raw = _dot_nt(qb, kb)
    from_state = _dot(qb, ct.astype(BF16))

    decay_mat = jnp.exp2(jnp.where(causal, prep["b_rows"][hd:hd + 1, :] * LOG2E - x_c * LOG2E, NEG))
    sqk = (raw * decay_mat).astype(BF16)
    intra = _dot(sqk, jnp.concatenate([v, jnp.ones((lc, d), BF16)], axis=-1))
    inter = jnp.exp(m_prev - x_c) * from_state
    num = inter[:, 0:d] + intra[:, 0:d]
    den = inter[:, d:d + 1] + intra[:, d:d + 1]
    hval = num / jnp.maximum(jnp.abs(den), jnp.exp(-(a_c + x_c)))
    og = o_ref[:, hd * d:(hd + 1) * d].astype(F32)
    y_ref[:, hd * d:(hd + 1) * d] = (_sigmoid(og) * hval).astype(BF16)

    ct_ref[hd] = decay * ct + s_loc * increment
    m_ref[hd:hd + 1, :] = jnp.broadcast_to(m_new, (1, LANES))


def _mlstm_call(z, gates, gate_bias, conv_w, conv_b, heads, col0):
    b, s, _ = z.shape
    width = conv_w.shape[1] // 2
    lc = min(ML_CHUNK, s)
    rows = min(ML_CHUNKS_PER_STEP * lc, s)
    assert s % rows == 0 and rows % lc == 0 and col0 % width == 0 and width // heads == LANES
    assert 2 * heads == 8, "gate rows are handled as one 8-sublane tile"
    cb = col0 // width
    taps = conv_w.shape[0]
    kern = functools.partial(_mlstm_kernel, heads=heads, taps=taps, lc=lc)
    zspec = lambda off: pl.BlockSpec((None, rows, width), lambda bi, ci: (bi, ci, cb + off))
    return pl.pallas_call(
        kern,
        grid=(b, s // rows),
        in_specs=[
            zspec(0), zspec(1), zspec(2), zspec(3),
            pl.BlockSpec((None, rows, LANES), lambda bi, ci: (bi, ci, 0)),
            pl.BlockSpec((1, LANES), lambda bi, ci: (0, 0)),
            pl.BlockSpec((taps, 2 * width), lambda bi, ci: (0, 0)),
            pl.BlockSpec((1, 2 * width), lambda bi, ci: (0, 0)),
        ],
        out_specs=pl.BlockSpec((None, rows, width), lambda bi, ci: (bi, ci, 0)),
        out_shape=jax.ShapeDtypeStruct((b, s, width), BF16),
        scratch_shapes=[
            pltpu.VMEM((8, 2 * width), F32),
            pltpu.VMEM((heads, LANES, 2 * LANES), F32),
            pltpu.VMEM((2 * heads, LANES), F32),
        ],
        compiler_params=_params("arbitrary", "arbitrary"),
        name="mlstm",
    )(z, z, z, z, gates, gate_bias, conv_w, conv_b)


def _mix_ffn_kernel(*refs, n_in, chunk):
    y_refs = refs[:n_in]
    wo_refs = refs[n_in:2 * n_in]
    (x_ref, gm_ref, sh_ref, sc_ref, g_ref, gate_ref, w1_ref, w3_ref, w2_ref, o_ref) = refs[2 * n_in:]
    hidden = w1_ref.shape[1]
    sub = min(FFN_ROWS, x_ref.shape[0])

    def mixed(r0):
        mix = _dot(y_refs[0][r0:r0 + sub, :], wo_refs[0][...])
        for y_ref, w_ref in zip(y_refs[1:], wo_refs[1:]):
            mix = mix + _dot(y_ref[r0:r0 + sub, :], w_ref[...])
        x = x_ref[r0:r0 + sub, :] + gm_ref[...] * mix
        return x, _norm_modulate(x, g_ref[...], sh_ref[...], sc_ref[...]).astype(BF16)

    def ffn(r0, x, h):
        acc = None
        for c0 in range(0, hidden, chunk):
            a = _dot(h, w1_ref[:, c0:c0 + chunk])
            bb = _dot(h, w3_ref[:, c0:c0 + chunk])
            u = (_silu(a) * bb).astype(BF16)
            part = _dot(u, w2_ref[c0:c0 + chunk, :])
            acc = part if acc is None else acc + part
        o_ref[r0:r0 + sub, :] = x + gate_ref[...] * acc

    starts = list(range(0, x_ref.shape[0], sub))
    ready = mixed(starts[0])
    for k, r0 in enumerate(starts):
        current = ready
        if k + 1 < len(starts):
            ready = mixed(starts[k + 1])
        ffn(r0, *current)


def _mix_ffn_call(ys, wo_all, wo_layer, x, gate_mix, shift, scale, g, gate, w1, w3, w2, layer, name):
    b, s, d = x.shape
    hidden = w1.shape[2]
    tm = min(ROW_TILE, s)
    assert s % tm == 0 and hidden % FFN_CHUNK == 0
    assert all(y.shape[2] == ys[0].shape[2] for y in ys)
    kern = functools.partial(_mix_ffn_kernel, n_in=len(ys), chunk=FFN_CHUNK)
    vec = pl.BlockSpec((None, 1, d), lambda i, j: (i, 0, 0))
    rows = lambda width: pl.BlockSpec((None, tm, width), lambda i, j: (i, j, 0))
    stacked = lambda w: pl.BlockSpec((None,) + w.shape[1:], lambda i, j: (layer, 0, 0))
    wo_specs = [pl.BlockSpec((None, y.shape[2], d), lambda i, j, n=n: (wo_layer, n, 0))
                for n, y in enumerate(ys)]
    return pl.pallas_call(
        kern,
        grid=(b, s // tm),
        in_specs=[rows(y.shape[2]) for y in ys] + wo_specs + [
            rows(d), vec, vec, vec,
            pl.BlockSpec((1, d), lambda i, j: (0, 0)),
            vec, stacked(w1), stacked(w3), stacked(w2),
        ],
        out_specs=rows(d),
        out_shape=jax.ShapeDtypeStruct((b, s, d), F32),
        compiler_params=_params("arbitrary", "arbitrary"),
        name=name,
    )(*ys, *([wo_all] * len(ys)), x, gate_mix, shift, scale, g, gate, w1, w3, w2)


def _pad_cols(w, n):
    return jnp.pad(w, ((0, 0), (0, n - w.shape[1])))


def _gain_row(q_g, k_g, width, total):
    reps = width // HEAD_DIM
    row = jnp.concatenate([jnp.tile(q_g.astype(F32) * LOG2E, reps),
                           jnp.tile(k_g.astype(F32) * HEAD_DIM ** 0.5, reps),
                           jnp.ones((total - 2 * width,), F32)])
    return row.reshape(1, total)


def kernel(x, c, ada_w, ada_b, norm_mix_g, norm_ffn_g, ab_w_in, ml_b_i, ml_b_f, ml_conv_w, ml_conv_b,
           da_q_g, da_k_g, da_lambda, da_subln_g, ab_w_out, fx_w_in, fx_b_f, fx_q_g, fx_k_g, fx_w_out,
           ffn_w1, ffn_w3, ffn_w2):
    depth = ada_w.shape[0]
    b, s, d = x.shape
    ml_width = ml_conv_w.shape[2] // 2
    ml_heads = ml_b_i.shape[1]
    da_width = (ab_w_in.shape[2] - 4 * ml_width - 2 * ml_heads) // 3
    da_heads = da_width // (2 * HEAD_DIM)
    fx_heads = fx_b_f.shape[1]
    fx_width = fx_heads * HEAD_DIM
    ab_main = 3 * da_width + 4 * ml_width
    fx_main = 3 * fx_width

    mods = _ada_call(c, ada_w, ada_b).reshape(depth, b, 6, 1, d)
    slopes = 2.0 ** (-8.0 * jnp.arange(1, da_heads + 1, dtype=F32) / da_heads)

    ab_w_in_b, fx_w_in_b = ab_w_in.astype(BF16), fx_w_in.astype(BF16)
    ab_w_out_b, fx_w_out_b = ab_w_out.astype(BF16), fx_w_out.astype(BF16)
    ffn_w1_b, ffn_w3_b, ffn_w2_b = ffn_w1.astype(BF16), ffn_w3.astype(BF16), ffn_w2.astype(BF16)

    for l in range(depth):
        sh_m, sc_m, g_m, sh_f, sc_f, g_f = (mods[l, :, t] for t in range(6))
        j = l // 2
        g_mix = norm_mix_g[l].reshape(1, d)
        if l % 2 == 0:
            z, gates = _inproj_call(
                x, sh_m, sc_m, g_mix, ab_w_in_b, j, ab_main,
                _pad_cols(ab_w_in[j][:, ab_main:], LANES).astype(BF16),
                _gain_row(da_q_g[j], da_k_g[j], da_width, ab_main),
                2 * da_width, "inproj_even")
            lambda_init = 0.8 - 0.6 * math.exp(-0.3 * l)
            y_da = _da_call(z, slopes, da_lambda[j], da_subln_g[j], da_heads, lambda_init)
            gate_bias = _pad_cols(jnp.concatenate([ml_b_i[j], ml_b_f[j]]).reshape(1, -1), LANES)
            y_ml = _mlstm_call(z, gates, gate_bias, ml_conv_w[j], ml_conv_b[j].reshape(1, -1),
                               ml_heads, 3 * da_width)
            ys, wo_all, name = [y_da, y_ml], ab_w_out_b, "mix_ffn_even"
        else:
            z, gates = _inproj_call(
                x, sh_m, sc_m, g_mix, fx_w_in_b, j, fx_main,
                _pad_cols(fx_w_in[j][:, fx_main:], LANES).astype(BF16),
                _gain_row(fx_q_g[j], fx_k_g[j], fx_width, fx_main),
                2 * fx_width, "inproj_odd")
            fneg = _fox_gate_call(gates, _pad_cols(fx_b_f[j].reshape(1, -1), LANES))
            ys, wo_all, name = [_fox_call(z, fneg, fx_heads)], fx_w_out_b, "mix_ffn_odd"
        x = _mix_ffn_call(ys, wo_all, j, x, g_m, sh_f, sc_f, norm_ffn_g[l].reshape(1, d), g_f,
                          ffn_w1_b, ffn_w3_b, ffn_w2_b, l, name)
    return x
```

```python
import functools
import math

import jax
import jax.numpy as jnp
from jax import lax
from jax.experimental import pallas as pl
from jax.experimental.pallas import tpu as pltpu

F32 = jnp.float32
BF16 = jnp.bfloat16

EPS = 1e-6
LANES = 128
HEAD_DIM = 64
CHUNK = 64
CHUNK_SHIFT = CHUNK.bit_length() - 1
assert CHUNK == 1 << CHUNK_SHIFT
NEG = -1e30
LOG2E = math.log2(math.e)
BIAS_PIECES = 3
ONES_ROWS = 16

VMEM_LIMIT = 56 * 1024 * 1024

ADA_COLS = 1536
ROW_TILE = 1024
FFN_ROWS = 512
INPROJ_ROWS = 1024
INPROJ_CHUNK = 512
ATTN_TILE = 256
ATTN_RING = 2
ATTN_KEYS = ATTN_RING * ATTN_TILE
ATTN_CHAIN = 2
assert ATTN_CHAIN % 2 == 0
ATTN_BLOCKS = 4
PACK_ROWS = 512
ML_CHUNK = 256
ML_CHUNKS_PER_STEP = 2
GATE_CHUNK = 256
FFN_CHUNK = 256


def _params(*sem):
    return pltpu.CompilerParams(dimension_semantics=sem, vmem_limit_bytes=VMEM_LIMIT)


def _log_sigmoid(x):
    return jnp.minimum(x, 0.0) - jnp.log1p(jnp.exp(-jnp.abs(x)))


def _sigmoid(x):
    return 0.5 * jnp.tanh(0.5 * x) + 0.5


def _silu(x):
    h = 0.5 * x
    return h * (jnp.tanh(h) + 1.0)


def _prefix_max_lanes(x):
    axis = x.ndim - 1
    lane = lax.broadcasted_iota(jnp.int32, x.shape, axis)
    shift = 1
    while shift < x.shape[axis]:
        x = jnp.maximum(x, jnp.where(lane >= shift, pltpu.roll(x, shift, axis=axis), NEG))
        shift *= 2
    return x


def _split3(x):
    x1 = x.astype(BF16)
    r1 = x - x1.astype(F32)
    x2 = r1.astype(BF16)
    x3 = (r1 - x2.astype(F32)).astype(BF16)
    return x1, x2, x3


def _dot(a, b):
    return jnp.dot(a, b, preferred_element_type=F32)


def _dot_nt(a, b):
    return lax.dot_general(a, b, (((1,), (1,)), ((), ())), preferred_element_type=F32)


def _dot_tn(a, b):
    return lax.dot_general(a, b, (((0,), (0,)), ((), ())), preferred_element_type=F32)


def _exact_lhs_dot(m_bf16, x):
    x1, x2, x3 = _split3(x)
    return _dot(m_bf16, x1) + _dot(m_bf16, x2) + _dot(m_bf16, x3)


def _exact_rhs_dot(x, m_bf16):
    x1, x2, x3 = _split3(x)
    return _dot(x1, m_bf16) + _dot(x2, m_bf16) + _dot(x3, m_bf16)


def _norm_modulate(x, g, sh, sc):
    ms = jnp.mean(x * x, axis=-1, keepdims=True)
    return x * lax.rsqrt(ms + EPS) * (g * (1.0 + sc)) + sh


def _ada_kernel(c_ref, w_ref, b_ref, o_ref):
    c = c_ref[...]
    ca = _silu(c)
    c1, c2, c3 = _split3(ca)
    w = w_ref[...]
    w1 = w.astype(BF16)
    w2 = (w - w1.astype(F32)).astype(BF16)
    acc = _dot(c1, w1) + _dot(c1, w2) + _dot(c2, w1) + _dot(c2, w2) + _dot(c3, w1)
    o_ref[...] = acc + b_ref[...]


def _ada_call(c, ada_w, ada_b):
    depth, d, n = ada_w.shape
    b = c.shape[0]
    tn = min(ADA_COLS, n)
    assert n % tn == 0
    return pl.pallas_call(
        _ada_kernel,
        grid=(depth, n // tn),
        in_specs=[
            pl.BlockSpec((b, d), lambda l, j: (0, 0)),
            pl.BlockSpec((None, d, tn), lambda l, j: (l, 0, j)),
            pl.BlockSpec((None, 1, tn), lambda l, j: (l, 0, j)),
        ],
        out_specs=pl.BlockSpec((None, b, tn), lambda l, j: (l, 0, j)),
        out_shape=jax.ShapeDtypeStruct((depth, b, n), F32),
        compiler_params=_params("arbitrary", "arbitrary"),
        name="ada_mod",
    )(c, ada_w, ada_b.reshape(depth, 1, n))


def _inproj_kernel(x_ref, sh_ref, sc_ref, g_ref, w_ref, wg_ref, gain_ref, z_ref, gate_ref,
                   *, n_norm, chunk):
    half = x_ref.shape[0] // 2
    h_rows, first_rows = [], []
    for r0 in (0, half):
        h_part = _norm_modulate(x_ref[r0:r0 + half, :], g_ref[...], sh_ref[...], sc_ref[...]).astype(BF16)
        first_rows.append(_dot(h_part, w_ref[:, 0:chunk]))
        h_rows.append(h_part)
    h = jnp.concatenate(h_rows, axis=0)
    n = w_ref.shape[1]
    lane = lax.broadcasted_iota(jnp.int32, (1, LANES), 1)
    lo = lane < HEAD_DIM
    for c0 in range(0, n, chunk):
        zc = jnp.concatenate(first_rows, axis=0) if c0 == 0 else _dot(h, w_ref[:, c0:c0 + chunk])
        if c0 < n_norm:
            parts = []
            for s0 in range(0, chunk, LANES):
                zs = zc[:, s0:s0 + LANES]
                sq = zs * zs
                s_lo = jnp.sum(jnp.where(lo, sq, 0.0), axis=-1, keepdims=True)
                s_hi = jnp.sum(jnp.where(lo, 0.0, sq), axis=-1, keepdims=True)
                r_lo = lax.rsqrt(s_lo + HEAD_DIM * EPS)
                r_hi = lax.rsqrt(s_hi + HEAD_DIM * EPS)
                parts.append(zs * jnp.where(lo, r_lo, r_hi))
            zc = jnp.concatenate(parts, axis=-1) * gain_ref[:, c0:c0 + chunk]
        z_ref[:, c0:c0 + chunk] = zc.astype(BF16)
    gate_ref[...] = _dot(h, wg_ref[...])


def _inproj_call(x, shift, scale, g, w_all, layer, n, wg, gain, n_norm, name):
    b, s, d = x.shape
    tm = min(INPROJ_ROWS, s)
    chunk = INPROJ_CHUNK
    assert s % tm == 0 and tm % 2 == 0 and n % chunk == 0 and n_norm % chunk == 0
    kern = functools.partial(_inproj_kernel, n_norm=n_norm, chunk=chunk)
    return pl.pallas_call(
        kern,
        grid=(b, s // tm),
        in_specs=[
            pl.BlockSpec((None, tm, d), lambda i, j: (i, j, 0)),
            pl.BlockSpec((None, 1, d), lambda i, j: (i, 0, 0)),
            pl.BlockSpec((None, 1, d), lambda i, j: (i, 0, 0)),
            pl.BlockSpec((1, d), lambda i, j: (0, 0)),
            pl.BlockSpec((None, d, n), lambda i, j: (layer, 0, 0)),
            pl.BlockSpec((d, LANES), lambda i, j: (0, 0)),
            pl.BlockSpec((1, n), lambda i, j: (0, 0)),
        ],
        out_specs=[
            pl.BlockSpec((None, tm, n), lambda i, j: (i, j, 0)),
            pl.BlockSpec((None, tm, LANES), lambda i, j: (i, j, 0)),
        ],
        out_shape=[
            jax.ShapeDtypeStruct((b, s, n), BF16),
            jax.ShapeDtypeStruct((b, s, LANES), F32),
        ],
        compiler_params=_params("arbitrary", "arbitrary"),
        name=name,
    )(x, shift, scale, g, w_all, wg, gain)


def _lane_ids():
    return lax.broadcasted_iota(jnp.int32, (1, LANES), 1)


def _place_pieces(pieces, first_lane):
    lane = _lane_ids()
    out = jnp.zeros(pieces[0].shape, F32)
    for j, p in enumerate(pieces):
        out = jnp.where(lane == first_lane + j, p.astype(F32), out)
    return out


def _pack_keys(blocks, shared_fn):
    s = blocks[0][0].shape[0]
    rows = min(PACK_ROWS, s)
    keep_lo = jnp.where(_lane_ids() < HEAD_DIM, 1.0, 0.0).astype(BF16)
    keep_hi = jnp.where(_lane_ids() < HEAD_DIM, 0.0, 1.0).astype(BF16)

    def body(c, carry):
        r0 = pl.multiple_of(c * rows, rows)
        shared = shared_fn(r0, rows)
        for k_ref, v_ref, ka_ref, kb_ref, vt_refs, extras_fn in blocks:
            k2 = k_ref[pl.ds(r0, rows), :]
            ea, eb = extras_fn(shared)
            ka_ref[pl.ds(r0, rows), :] = k2 * keep_lo + ea.astype(BF16)
            kb_ref[pl.ds(r0, rows), :] = k2 * keep_hi + eb.astype(BF16)
            vt = v_ref[pl.ds(r0, rows), :].T
            for ref, c0, width in vt_refs:
                ref[0:width, pl.ds(r0, rows)] = vt[c0:c0 + width, :]
                ref[width:width + ONES_ROWS, pl.ds(r0, rows)] = jnp.ones((ONES_ROWS, rows), BF16)
        return carry

    lax.fori_loop(0, s // rows, body, 0)


def _pack_queries(q2, factors):
    lane = _lane_ids()
    q2 = q2.astype(F32)
    lo = lane < HEAD_DIM
    fac_a = _place_pieces(factors, HEAD_DIM)
    fac_b = _place_pieces(factors, 0)
    return jnp.where(lo, q2, fac_a).astype(BF16), jnp.where(lo, fac_b, q2).astype(BF16)


def _init_stats(m_ref, acc_ref):
    m_ref[...] = jnp.full(m_ref.shape, NEG, F32)
    acc_ref[...] = jnp.zeros(acc_ref.shape, F32)


def _softmax_step(s, tile_max, vt, m_ref, acc_ref):
    m_old = m_ref[...]
    m_new = jnp.maximum(m_old, tile_max)
    alpha = jnp.exp2(m_old - m_new)
    p = jnp.exp2(s - m_new)
    acc_ref[...] = alpha * acc_ref[...] + _dot(vt, p.astype(BF16))
    m_ref[...] = m_new


def _attend(n0, t, chain, s_buf, mx_buf, make_addend, finish_pair):
    tk = ATTN_KEYS
    first = ATTN_RING - 1
    built = {}

    def group(c):
        if c not in built:
            built[c] = chain[c]()
            for streams in built[c]:
                for stream in streams:
                    _init_stats(*stream[4])
        return built[c]

    def qk_one(tiles, u, step, idx, rows=tk):
        k0 = pl.multiple_of(step * tk, tk)
        q, k_ref = tiles[u][idx][0], tiles[u][idx][1]
        s = _dot_nt(k_ref[pl.ds(k0, rows), :], q)
        s_buf[u][idx, 0:rows, :] = s
        if rows == tk:
            mx_buf[u][idx] = jnp.max(s, axis=0, keepdims=True)

    def stage(c, u, step, partner, last=False):
        tiles = group(c)
        k0 = pl.multiple_of(step * tk, tk)
        rows, offset = (u + 1) * t, u * t
        addends = {}
        for idx, (_, _, vt, add_key, stats) in enumerate(tiles[u]):
            if partner is not None:
                p_tiles, p_u, p_step, p_rows = partner
                qk_one(p_tiles, p_u, p_step, idx, p_rows)
            if not last:
                _softmax_step(s_buf[u][idx], mx_buf[u][idx], vt(k0, tk), *stats)
                continue
            if add_key not in addends:
                addends[add_key] = make_addend(add_key, addends.setdefault("shared", {}), rows, offset)
            s = s_buf[u][idx, 0:rows, :] + addends[add_key]
            _softmax_step(s, jnp.max(s, axis=0, keepdims=True), vt(k0, rows), *stats)
            if idx % 2 == 1:
                finish_pair(c, u, idx // 2)

    for idx in range(len(group(0)[first])):
        qk_one(group(0), first, 0, idx)

    for c in range(len(chain)):
        n = n0 + c
        tiles = group(c)

        def ring_pass(j, c=c, tiles=tiles):
            for u in range(first, -1, -1):
                stage(c, u, j, (tiles, u - 1, j, tk) if u > 0 else (tiles, first, j + 1, tk))

        def body(p, carry, ring_pass=ring_pass):
            ring_pass(2 * p)
            ring_pass(2 * p + 1)
            return carry

        lax.fori_loop(0, (n - 1) // 2, body, 0)
        if c % 2 == 1:
            ring_pass(n - 2)
        for u in range(first, -1, -1):
            if u > 0:
                partner = (tiles, u - 1, n - 1, u * t)
            elif c + 1 < len(chain):
                partner = (group(c + 1), first, 0, tk)
            else:
                partner = None
            stage(c, u, n - 1, partner, last=True)


def _lane_block(ref, h):
    return ref.at[:, pl.ds(h * LANES, LANES)]


def _da_kernel(slope_ref, q_ref, k_ref, v_ref, lam_ref, sg_ref, o_ref,
               ka_ref, kb_ref, vt_ref, m_ref, acc_ref, s0_ref, s1_ref, mx0_ref, mx1_ref,
               *, t, hb, lambda_init):
    dv = LANES
    g = pl.program_id(1)
    i = pl.program_id(2)
    slopes2 = [slope_ref[g * hb + h] * LOG2E for h in range(hb)]

    @pl.when(i == 0)
    def _():
        def position_pieces(r0, rows):
            pos = lax.broadcasted_iota(jnp.int32, (rows, LANES), 0) + r0
            hi = jnp.right_shift(pos, CHUNK_SHIFT).astype(F32)
            lo = jnp.bitwise_and(pos, CHUNK - 1).astype(F32)
            lane = _lane_ids()

            def place(first):
                in_hi = (lane >= first) & (lane < first + BIAS_PIECES)
                in_lo = (lane >= first + BIAS_PIECES) & (lane < first + 2 * BIAS_PIECES)
                return jnp.where(in_hi, hi, jnp.where(in_lo, lo, 0.0)).astype(BF16)
            return place(HEAD_DIM), place(0)
        _pack_keys([(_lane_block(k_ref, h), _lane_block(v_ref, h), ka_ref.at[h], kb_ref.at[h],
                     [(vt_ref.at[h], 0, dv)], lambda shared: shared) for h in range(hb)],
                   position_pieces)

    def last_step_addend(h, shared, rows, offset):
        if "base" not in shared:
            key = lax.broadcasted_iota(jnp.int32, (rows, t), 0)
            qry = lax.broadcasted_iota(jnp.int32, (rows, t), 1) + offset
            allowed = jnp.right_shift(key, CHUNK_SHIFT) <= jnp.right_shift(qry, CHUNK_SHIFT)
            shared["base"] = jnp.where(allowed, jnp.maximum(key - qry, 0).astype(F32), -NEG)
        return (-2.0 * slopes2[h]) * shared["base"]

    def build_group(c):
        tiles = []
        for u in range(ATTN_RING):
            tile = c * ATTN_RING + u
            streams = []
            for h in range(hb):
                slope_pieces = [p.astype(F32) for p in _split3(jnp.full((1, LANES), slopes2[h], F32))]
                factors = [p * float(CHUNK) for p in slope_pieces] + slope_pieces
                qa, qb = _pack_queries(q_ref[tile * t:(tile + 1) * t, h * LANES:(h + 1) * LANES], factors)
                vt = lambda k0, size, h=h: vt_ref[h, :, pl.ds(k0, size)]
                for m, (q, k_sc) in enumerate(((qa, ka_ref), (qb, kb_ref))):
                    n_stream = (tile * hb + h) * 2 + m
                    streams.append((q, k_sc.at[h], vt, h, (m_ref.at[n_stream], acc_ref.at[n_stream])))
            tiles.append(streams)
        return tiles

    def finish_head(c, u, h):
        tile = c * ATTN_RING + u
        lp = lam_ref[...]
        lam = (jnp.exp(jnp.sum(lp[0:1] * lp[1:2], axis=-1, keepdims=True))
               - jnp.exp(jnp.sum(lp[2:3] * lp[3:4], axis=-1, keepdims=True)) + lambda_init)
        pa = acc_ref[(tile * hb + h) * 2]
        pb = acc_ref[(tile * hb + h) * 2 + 1]
        out = pa[0:dv] / pa[dv:dv + 1] - lam * (pb[0:dv] / pb[dv:dv + 1])
        ms = jnp.mean(out * out, axis=0, keepdims=True)
        out = out * lax.rsqrt(ms + EPS) * (sg_ref[...] * (1.0 - lambda_init))
        o_ref[tile * t:(tile + 1) * t, h * LANES:(h + 1) * LANES] = out.T.astype(BF16)

    chain = [functools.partial(build_group, c) for c in range(ATTN_CHAIN)]
    _attend(ATTN_CHAIN * i + 1, t, chain, (s0_ref, s1_ref), (mx0_ref, mx1_ref),
            last_step_addend, finish_head)


def _da_call(z, slopes, lam_p, subln_g, heads, lambda_init):
    b, s, _ = z.shape
    t, tk = ATTN_TILE, ATTN_KEYS
    assert s % tk == 0 and t % CHUNK == 0 and tk == ATTN_RING * t
    assert s <= CHUNK * 256, "key position // CHUNK must stay exact in bf16"
    hb = min(ATTN_BLOCKS, heads)
    groups = heads // hb
    assert heads % hb == 0
    kern = functools.partial(_da_kernel, t=t, hb=hb, lambda_init=lambda_init)
    w = hb * LANES
    n_streams = 2 * hb * ATTN_RING * ATTN_CHAIN
    rows = ATTN_CHAIN * tk
    assert s % rows == 0
    return pl.pallas_call(
        kern,
        grid_spec=pltpu.PrefetchScalarGridSpec(
            num_scalar_prefetch=1,
            grid=(b, groups, s // rows),
            in_specs=[
                pl.BlockSpec((None, rows, w), lambda bi, g, i, sl: (bi, i, g)),
                pl.BlockSpec((None, s, w), lambda bi, g, i, sl: (bi, 0, groups + g)),
                pl.BlockSpec((None, s, w), lambda bi, g, i, sl: (bi, 0, 2 * groups + g)),
                pl.BlockSpec((4, HEAD_DIM), lambda bi, g, i, sl: (0, 0)),
                pl.BlockSpec((LANES, 1), lambda bi, g, i, sl: (0, 0)),
            ],
            out_specs=pl.BlockSpec((None, rows, w), lambda bi, g, i, sl: (bi, i, g)),
            scratch_shapes=[
                pltpu.VMEM((hb, s, LANES), BF16), pltpu.VMEM((hb, s, LANES), BF16),
                pltpu.VMEM((hb, LANES + ONES_ROWS, s), BF16),
                pltpu.VMEM((n_streams, 1, t), F32),
                pltpu.VMEM((n_streams, LANES + ONES_ROWS, t), F32),
                pltpu.VMEM((2 * hb, ATTN_KEYS, t), F32), pltpu.VMEM((2 * hb, ATTN_KEYS, t), F32),
                pltpu.VMEM((2 * hb, 1, t), F32), pltpu.VMEM((2 * hb, 1, t), F32),
            ],
        ),
        out_shape=jax.ShapeDtypeStruct((b, s, heads * LANES), BF16),
        compiler_params=_params("arbitrary", "arbitrary", "arbitrary"),
        name="diff_attention",
    )(slopes, z, z, z, lam_p, subln_g.reshape(LANES, 1))


def _fox_gate_kernel(g_ref, b_ref, o_ref, *, chunk):
    s = g_ref.shape[0]
    r = lax.broadcasted_iota(jnp.int32, (chunk, chunk), 0)
    c = lax.broadcasted_iota(jnp.int32, (chunk, chunk), 1)
    tri = (c <= r).astype(BF16)
    carry = jnp.zeros((1, LANES), F32)
    for c0 in range(0, s, chunk):
        ls = _log_sigmoid(g_ref[c0:c0 + chunk, :] + b_ref[...])
        cs = _exact_lhs_dot(tri, ls) + carry
        carry = cs[chunk - 1:chunk, :]
        o_ref[c0:c0 + chunk, :] = cs * (-LOG2E)


def _fox_gate_call(gates, bias_row):
    b, s, _ = gates.shape
    chunk = min(GATE_CHUNK, s)
    assert s % chunk == 0
    kern = functools.partial(_fox_gate_kernel, chunk=chunk)
    return pl.pallas_call(
        kern,
        grid=(b,),
        in_specs=[
            pl.BlockSpec((None, s, LANES), lambda i: (i, 0, 0)),
            pl.BlockSpec((1, LANES), lambda i: (0, 0)),
        ],
        out_specs=pl.BlockSpec((None, s, LANES), lambda i: (i, 0, 0)),
        out_shape=jax.ShapeDtypeStruct((b, s, LANES), F32),
        compiler_params=_params("arbitrary"),
        name="fox_gate_cumsum",
    )(gates, bias_row)


def _fox_kernel(q_ref, k_ref, v_ref, f_ref, o_ref,
                ka_ref, kb_ref, vt_ref, m_ref, acc_ref, s0_ref, s1_ref, mx0_ref, mx1_ref, *, t, hb):
    dv = HEAD_DIM
    g = pl.program_id(1)
    i = pl.program_id(2)

    @pl.when(i == 0)
    def _():
        half = LANES // 2
        src = lax.broadcasted_iota(jnp.int32, (2 * LANES, LANES), 0)
        dst = lax.broadcasted_iota(jnp.int32, (2 * LANES, LANES), 1)
        piece = jnp.where(src < LANES, 0, jnp.where(src < LANES + half, 1, 2))
        head = src - jnp.where(src < LANES, 0, jnp.where(src < LANES + half, LANES, LANES + half))

        def pieces_of(r0, rows):
            x = f_ref[pl.ds(r0, rows), :]
            x1 = x.astype(BF16)
            r1 = x - x1.astype(F32)
            x2 = r1.astype(BF16).astype(F32)
            tail = jnp.where(_lane_ids() < half, x2, pltpu.roll(r1 - x2, half, axis=1))
            return jnp.concatenate([x1, tail.astype(BF16)], axis=-1)

        blocks = []
        for h in range(hb):
            pair = g * hb + h
            place_a = ((head == 2 * pair) & (dst == HEAD_DIM + piece)).astype(BF16)
            place_b = ((head == 2 * pair + 1) & (dst == piece)).astype(BF16)
            place = jnp.concatenate([place_a, place_b], axis=-1)

            def extras(pieces, place=place):
                both = _dot(pieces, place)
                return both[:, 0:LANES], both[:, LANES:2 * LANES]
            blocks.append((_lane_block(k_ref, h), _lane_block(v_ref, h), ka_ref.at[h], kb_ref.at[h],
                           [(vt_ref.at[2 * h + m], m * dv, dv) for m in range(2)], extras))
        _pack_keys(blocks, pieces_of)

    def last_step_addend(_, shared, rows, offset):
        key = lax.broadcasted_iota(jnp.int32, (rows, t), 0)
        qry = lax.broadcasted_iota(jnp.int32, (rows, t), 1) + offset
        return jnp.where(key <= qry, 0.0, NEG)

    def build_group(c):
        ones = [jnp.ones((1, LANES), F32)] * BIAS_PIECES
        tiles = []
        for u in range(ATTN_RING):
            tile = c * ATTN_RING + u
            streams = []
            for h in range(hb):
                qa, qb = _pack_queries(q_ref[tile * t:(tile + 1) * t, h * LANES:(h + 1) * LANES], ones)
                for m, (q, k_sc) in enumerate(((qa, ka_ref), (qb, kb_ref))):
                    vt = lambda k0, size, n=2 * h + m: vt_ref[n, :, pl.ds(k0, size)]
                    n_stream = (tile * hb + h) * 2 + m
                    streams.append((q, k_sc.at[h], vt, 0, (m_ref.at[n_stream], acc_ref.at[n_stream])))
            tiles.append(streams)
        return tiles

    def finish_pair(c, u, h):
        tile = c * ATTN_RING + u
        pa = acc_ref[(tile * hb + h) * 2]
        pb = acc_ref[(tile * hb + h) * 2 + 1]
        out = jnp.concatenate([pa[0:dv] / pa[dv:dv + 1], pb[0:dv] / pb[dv:dv + 1]], axis=0)
        o_ref[tile * t:(tile + 1) * t, h * LANES:(h + 1) * LANES] = out.T.astype(BF16)

    chain = [functools.partial(build_group, c) for c in range(ATTN_CHAIN)]
    _attend(ATTN_CHAIN * i + 1, t, chain, (s0_ref, s1_ref), (mx0_ref, mx1_ref),
            last_step_addend, finish_pair)


def _fox_call(z, fneg, heads):
    b, s, _ = z.shape
    pairs = heads // 2
    t, tk = ATTN_TILE, ATTN_KEYS
    assert s % tk == 0 and tk == ATTN_RING * t
    assert heads <= LANES // 2, "the forget-gate pieces of all heads share 128 + 64 + 64 contraction rows"
    hb = min(ATTN_BLOCKS, pairs)
    groups = pairs // hb
    assert pairs % hb == 0
    kern = functools.partial(_fox_kernel, t=t, hb=hb)
    w = hb * LANES
    n_streams = 2 * hb * ATTN_RING * ATTN_CHAIN
    rows = ATTN_CHAIN * tk
    assert s % rows == 0
    return pl.pallas_call(
        kern,
        grid=(b, groups, s // rows),
        in_specs=[
            pl.BlockSpec((None, rows, w), lambda bi, g, i: (bi, i, g)),
            pl.BlockSpec((None, s, w), lambda bi, g, i: (bi, 0, groups + g)),
            pl.BlockSpec((None, s, w), lambda bi, g, i: (bi, 0, 2 * groups + g)),
            pl.BlockSpec((None, s, LANES), lambda bi, g, i: (bi, 0, 0)),
        ],
        out_specs=pl.BlockSpec((None, rows, w), lambda bi, g, i: (bi, i, g)),
        out_shape=jax.ShapeDtypeStruct((b, s, pairs * LANES), BF16),
        scratch_shapes=[
            pltpu.VMEM((hb, s, LANES), BF16), pltpu.VMEM((hb, s, LANES), BF16),
            pltpu.VMEM((2 * hb, HEAD_DIM + ONES_ROWS, s), BF16),
            pltpu.VMEM((n_streams, 1, t), F32),
            pltpu.VMEM((n_streams, HEAD_DIM + ONES_ROWS, t), F32),
            pltpu.VMEM((2 * hb, ATTN_KEYS, t), F32), pltpu.VMEM((2 * hb, ATTN_KEYS, t), F32),
            pltpu.VMEM((2 * hb, 1, t), F32), pltpu.VMEM((2 * hb, 1, t), F32),
        ],
        compiler_params=_params("arbitrary", "arbitrary", "arbitrary"),
        name="forgetting_attention",
    )(z, z, z, fneg)


def _mlstm_kernel(q_ref, k_ref, v_ref, o_ref, g_ref, gb_ref, cw_ref, cb_ref, y_ref,
                  halo_ref, ct_ref, m_ref, *, heads, taps, lc):
    @pl.when(pl.program_id(1) == 0)
    def _():
        halo_ref[...] = jnp.zeros(halo_ref.shape, F32)
        ct_ref[...] = jnp.zeros(ct_ref.shape, F32)
        m_ref[...] = jnp.zeros(m_ref.shape, F32)

    rows = lambda ref, sub: ref.at[pl.ds(sub * lc, lc), :]
    edge = halo_ref[...]
    for sub in range(q_ref.shape[0] // lc):
        gates = _mlstm_gates(rows(g_ref, sub), gb_ref, heads)
        prep, edge = _mlstm_prep(rows(q_ref, sub), rows(k_ref, sub), gates, cw_ref, cb_ref,
                                 edge, taps)
        for hd in range(heads):
            _mlstm_head(prep, rows(v_ref, sub), rows(o_ref, sub), rows(y_ref, sub),
                        ct_ref, m_ref, hd, heads)
    halo_ref[...] = edge


def _mlstm_gates(g_ref, gb_ref, heads):
    lc = g_ref.shape[0]
    r = lax.broadcasted_iota(jnp.int32, (lc, lc), 0)
    c = lax.broadcasted_iota(jnp.int32, (lc, lc), 1)
    causal = c <= r

    gcol = g_ref[...] + gb_ref[...]
    grow = gcol.T[:2 * heads, :]
    a_cols = _exact_lhs_dot(causal.astype(BF16), _log_sigmoid(gcol))
    a_rows = _exact_rhs_dot(_log_sigmoid(grow), (r <= c).astype(BF16))

    b_half = grow[0:heads, :] - a_rows[heads:2 * heads, :]
    b_rows = jnp.concatenate([b_half, b_half], axis=0)
    return dict(gcol=gcol, a_cols=a_cols, b_rows=b_rows, causal=causal,
                prefix_cols=_prefix_max_lanes(b_rows).T)


def _mlstm_prep(q_ref, k_ref, gates, cw_ref, cb_ref, halo, taps):
    lc = q_ref.shape[0]
    pad = halo.shape[0]
    r = lax.broadcasted_iota(jnp.int32, (lc, lc), 0)
    c = lax.broadcasted_iota(jnp.int32, (lc, lc), 1)

    xb = jnp.concatenate([q_ref[...], k_ref[...]], axis=-1)
    xf = xb.astype(F32)
    conv = cb_ref[...] + xf * cw_ref[taps - 1:taps, :]
    for sft in range(1, taps):
        shifted = _dot((r - c == sft).astype(BF16), xb)
        conv = conv + shifted * cw_ref[taps - 1 - sft:taps - sft, :]
    edge = jnp.concatenate([halo, xf[0:pad, :]], axis=0)
    first = cb_ref[...]
    for j in range(taps):
        off = pad - (taps - 1) + j
        first = first + edge[off:off + pad, :] * cw_ref[j:j + 1, :]
    conv = jnp.concatenate([first, conv[pad:, :]], axis=0)
    return dict(gates, qk=_silu(conv)), xf[lc - pad:, :]


def _mlstm_head(prep, v_ref, o_ref, y_ref, ct_ref, m_ref, hd, heads):
    lc, width = v_ref.shape
    d = width // heads
    qk, gcol, a_cols, causal = prep["qk"], prep["gcol"], prep["a_cols"], prep["causal"]
    q = qk[:, hd * d:(hd + 1) * d]
    k = qk[:, width + hd * d:width + (hd + 1) * d] * (d ** -0.5)
    v = v_ref[:, hd * d:(hd + 1) * d]
    qb = q.astype(BF16)
    kb = k.astype(BF16)
    ig_c = gcol[:, hd:hd + 1]
    a_c = a_cols[:, heads + hd:heads + hd + 1]
    a_last = a_c[lc - 1:lc, :]
    ct = ct_ref[hd]
    m_prev = m_ref[hd:hd + 1, 0:1]
    x_c = jnp.maximum(prep["prefix_cols"][:, hd:hd + 1], m_prev)

    g_c = a_last - a_c + ig_c
    m_loc = jnp.max(g_c, axis=0, keepdims=True)
    w_c = jnp.exp(g_c - m_loc)
    m_new = jnp.maximum(a_last + m_prev, m_loc)
    decay = jnp.exp(a_last + m_prev - m_new)
    s_loc = jnp.exp(m_loc - m_new)
    wv = jnp.concatenate([w_c * v.astype(F32), jnp.broadcast_to(w_c, (lc, d))], axis=-1)
    increment = _dot_tn(kb, wv.astype(BF16))
    raw = _dot_nt(qb, kb)
    from_state = _dot(qb, ct.astype(BF16))

    decay_mat = jnp.exp2(jnp.where(causal, prep["b_rows"][hd:hd + 1, :] * LOG2E - x_c * LOG2E, NEG))
    sqk = (raw * decay_mat).astype(BF16)
    intra = _dot(sqk, jnp.concatenate([v, jnp.ones((lc, d), BF16)], axis=-1))
    inter = jnp.exp(m_prev - x_c) * from_state
    num = inter[:, 0:d] + intra[:, 0:d]
    den = inter[:, d:d + 1] + intra[:, d:d + 1]
    hval = num / jnp.maximum(jnp.abs(den), jnp.exp(-(a_c + x_c)))
    og = o_ref[:, hd * d:(hd + 1) * d].astype(F32)
    y_ref[:, hd * d:(hd + 1) * d] = (_sigmoid(og) * hval).astype(BF16)

    ct_ref[hd] = decay * ct + s_loc * increment
    m_ref[hd:hd + 1, :] = jnp.broadcast_to(m_new, (1, LANES))


def _mlstm_call(z, gates, gate_bias, conv_w, conv_b, heads, col0):
    b, s, _ = z.shape
    width = conv_w.shape[1] // 2
    lc = min(ML_CHUNK, s)
    rows = min(ML_CHUNKS_PER_STEP * lc, s)
    assert s % rows == 0 and rows % lc == 0 and col0 % width == 0 and width // heads == LANES
    assert 2 * heads == 8, "gate rows are handled as one 8-sublane tile"
    cb = col0 // width
    taps = conv_w.shape[0]
    kern = functools.partial(_mlstm_kernel, heads=heads, taps=taps, lc=lc)
    zspec = lambda off: pl.BlockSpec((None, rows, width), lambda bi, ci: (bi, ci, cb + off))
    return pl.pallas_call(
        kern,
        grid=(b, s // rows),
        in_specs=[
            zspec(0), zspec(1), zspec(2), zspec(3),
            pl.BlockSpec((None, rows, LANES), lambda bi, ci: (bi, ci, 0)),
            pl.BlockSpec((1, LANES), lambda bi, ci: (0, 0)),
            pl.BlockSpec((taps, 2 * width), lambda bi, ci: (0, 0)),
            pl.BlockSpec((1, 2 * width), lambda bi, ci: (0, 0)),
        ],
        out_specs=pl.BlockSpec((None, rows, width), lambda bi, ci: (bi, ci, 0)),
        out_shape=jax.ShapeDtypeStruct((b, s, width), BF16),
        scratch_shapes=[
            pltpu.VMEM((8, 2 * width), F32),
            pltpu.VMEM((heads, LANES, 2 * LANES), F32),
            pltpu.VMEM((2 * heads, LANES), F32),
        ],
        compiler_params=_params("arbitrary", "arbitrary"),
        name="mlstm",
    )(z, z, z, z, gates, gate_bias, conv_w, conv_b)


def _mix_ffn_kernel(*refs, n_in, chunk):
    y_refs = refs[:n_in]
    wo_refs = refs[n_in:2 * n_in]
    (x_ref, gm_ref, sh_ref, sc_ref, g_ref, gate_ref, w1_ref, w3_ref, w2_ref, o_ref) = refs[2 * n_in:]
    hidden = w1_ref.shape[1]
    sub = min(FFN_ROWS, x_ref.shape[0])

    def mixed(r0):
        mix = _dot(y_refs[0][r0:r0 + sub, :], wo_refs[0][...])
        for y_ref, w_ref in zip(y_refs[1:], wo_refs[1:]):
            mix = mix + _dot(y_ref[r0:r0 + sub, :], w_ref[...])
        x = x_ref[r0:r0 + sub, :] + gm_ref[...] * mix
        return x, _norm_modulate(x, g_ref[...], sh_ref[...], sc_ref[...]).astype(BF16)

    def ffn(r0, x, h):
        acc = None
        for c0 in range(0, hidden, chunk):
            a = _dot(h, w1_ref[:, c0:c0 + chunk])
            bb = _dot(h, w3_ref[:, c0:c0 + chunk])
            u = (_silu(a) * bb).astype(BF16)
            part = _dot(u, w2_ref[c0:c0 + chunk, :])
            acc = part if acc is None else acc + part
        o_ref[r0:r0 + sub, :] = x + gate_ref[...] * acc

    starts = list(range(0, x_ref.shape[0], sub))
    ready = mixed(starts[0])
    for k, r0 in enumerate(starts):
        current = ready
        if k + 1 < len(starts):
            ready = mixed(starts[k + 1])
        ffn(r0, *current)


def _mix_ffn_call(ys, wo_all, wo_layer, x, gate_mix, shift, scale, g, gate, w1, w3, w2, layer, name):
    b, s, d = x.shape
    hidden = w1.shape[2]
    tm = min(ROW_TILE, s)
    assert s % tm == 0 and hidden % FFN_CHUNK == 0
    assert all(y.shape[2] == ys[0].shape[2] for y in ys)
    kern = functools.partial(_mix_ffn_kernel, n_in=len(ys), chunk=FFN_CHUNK)
    vec = pl.BlockSpec((None, 1, d), lambda i, j: (i, 0, 0))
    rows = lambda width: pl.BlockSpec((None, tm, width), lambda i, j: (i, j, 0))
    stacked = lambda w: pl.BlockSpec((None,) + w.shape[1:], lambda i, j: (layer, 0, 0))
    wo_specs = [pl.BlockSpec((None, y.shape[2], d), lambda i, j, n=n: (wo_layer, n, 0))
                for n, y in enumerate(ys)]
    return pl.pallas_call(
        kern,
        grid=(b, s // tm),
        in_specs=[rows(y.shape[2]) for y in ys] + wo_specs + [
            rows(d), vec, vec, vec,
            pl.BlockSpec((1, d), lambda i, j: (0, 0)),
            vec, stacked(w1), stacked(w3), stacked(w2),
        ],
        out_specs=rows(d),
        out_shape=jax.ShapeDtypeStruct((b, s, d), F32),
        compiler_params=_params("arbitrary", "arbitrary"),
        name=name,
    )(*ys, *([wo_all] * len(ys)), x, gate_mix, shift, scale, g, gate, w1, w3, w2)


def _pad_cols(w, n):
    return jnp.pad(w, ((0, 0), (0, n - w.shape[1])))


def _gain_row(q_g, k_g, width, total):
    reps = width // HEAD_DIM
    row = jnp.concatenate([jnp.tile(q_g.astype(F32) * LOG2E, reps),
                           jnp.tile(k_g.astype(F32) * HEAD_DIM ** 0.5, reps),
                           jnp.ones((total - 2 * width,), F32)])
    return row.reshape(1, total)


def kernel(x, c, ada_w, ada_b, norm_mix_g, norm_ffn_g, ab_w_in, ml_b_i, ml_b_f, ml_conv_w, ml_conv_b,
           da_q_g, da_k_g, da_lambda, da_subln_g, ab_w_out, fx_w_in, fx_b_f, fx_q_g, fx_k_g, fx_w_out,
           ffn_w1, ffn_w3, ffn_w2):
    depth = ada_w.shape[0]
    b, s, d = x.shape
    ml_width = ml_conv_w.shape[2] // 2
    ml_heads = ml_b_i.shape[1]
    da_width = (ab_w_in.shape[2] - 4 * ml_width - 2 * ml_heads) // 3
    da_heads = da_width // (2 * HEAD_DIM)
    fx_heads = fx_b_f.shape[1]
    fx_width = fx_heads * HEAD_DIM
    ab_main = 3 * da_width + 4 * ml_width
    fx_main = 3 * fx_width

    mods = _ada_call(c, ada_w, ada_b).reshape(depth, b, 6, 1, d)
    slopes = 2.0 ** (-8.0 * jnp.arange(1, da_heads + 1, dtype=F32) / da_heads)

    ab_w_in_b, fx_w_in_b = ab_w_in.astype(BF16), fx_w_in.astype(BF16)
    ab_w_out_b, fx_w_out_b = ab_w_out.astype(BF16), fx_w_out.astype(BF16)
    ffn_w1_b, ffn_w3_b, ffn_w2_b = ffn_w1.astype(BF16), ffn_w3.astype(BF16), ffn_w2.astype(BF16)

    for l in range(depth):
        sh_m, sc_m, g_m, sh_f, sc_f, g_f = (mods[l, :, t] for t in range(6))
        j = l // 2
        g_mix = norm_mix_g[l].reshape(1, d)
        if l % 2 == 0:
            z, gates = _inproj_call(
                x, sh_m, sc_m, g_mix, ab_w_in_b, j, ab_main,
                _pad_cols(ab_w_in[j][:, ab_main:], LANES).astype(BF16),
                _gain_row(da_q_g[j], da_k_g[j], da_width, ab_main),
                2 * da_width, "inproj_even")
            lambda_init = 0.8 - 0.6 * math.exp(-0.3 * l)
            y_da = _da_call(z, slopes, da_lambda[j], da_subln_g[j], da_heads, lambda_init)
            gate_bias = _pad_cols(jnp.concatenate([ml_b_i[j], ml_b_f[j]]).reshape(1, -1), LANES)
            y_ml = _mlstm_call(z, gates, gate_bias, ml_conv_w[j], ml_conv_b[j].reshape(1, -1),
                               ml_heads, 3 * da_width)
            ys, wo_all, name = [y_da, y_ml], ab_w_out_b, "mix_ffn_even"
        else:
            z, gates = _inproj_call(
                x, sh_m, sc_m, g_mix, fx_w_in_b, j, fx_main,
                _pad_cols(fx_w_in[j][:, fx_main:], LANES).astype(BF16),
                _gain_row(fx_q_g[j], fx_k_g[j], fx_width, fx_main),
                2 * fx_width, "inproj_odd")
            fneg = _fox_gate_call(gates, _pad_cols(fx_b_f[j].reshape(1, -1), LANES))
            ys, wo_all, name = [_fox_call(z, fneg, fx_heads)], fx_w_out_b, "mix_ffn_odd"
        x = _mix_ffn_call(ys, wo_all, j, x, g_m, sh_f, sc_f, norm_ffn_g[l].reshape(1, d), g_f,
                          ffn_w1_b, ffn_w3_b, ffn_w2_b, l, name)
    return x
```

```python
import functools
import math

import jax
import jax.numpy as jnp
from jax import lax
from jax.experimental import pallas as pl
from jax.experimental.pallas import tpu as pltpu

F32 = jnp.float32
BF16 = jnp.bfloat16

EPS = 1e-6
LANES = 128
HEAD_DIM = 64
CHUNK = 64
CHUNK_SHIFT = CHUNK.bit_length() - 1
assert CHUNK == 1 << CHUNK_SHIFT
NEG = -1e30
LOG2E = math.log2(math.e)
BIAS_PIECES = 3
ONES_ROWS = 16

VMEM_LIMIT = 56 * 1024 * 1024

ADA_COLS = 1536
ROW_TILE = 1024
FFN_ROWS = 512
INPROJ_ROWS = 1024
INPROJ_CHUNK = 512
ATTN_TILE = 256
ATTN_RING = 2
ATTN_KEYS = ATTN_RING * ATTN_TILE
ATTN_CHAIN = 2
assert ATTN_CHAIN % 2 == 0
ATTN_BLOCKS = 4
PACK_ROWS = 512
ML_CHUNK = 256
ML_CHUNKS_PER_STEP = 2
GATE_CHUNK = 256
FFN_CHUNK = 256


def _params(*sem):
    return pltpu.CompilerParams(dimension_semantics=sem, vmem_limit_bytes=VMEM_LIMIT)


def _log_sigmoid(x):
    return jnp.minimum(x, 0.0) - jnp.log1p(jnp.exp(-jnp.abs(x)))


def _sigmoid(x):
    return 0.5 * jnp.tanh(0.5 * x) + 0.5


def _silu(x):
    h = 0.5 * x
    return h * (jnp.tanh(h) + 1.0)


def _prefix_max_lanes(x):
    axis = x.ndim - 1
    lane = lax.broadcasted_iota(jnp.int32, x.shape, axis)
    shift = 1
    while shift < x.shape[axis]:
        x = jnp.maximum(x, jnp.where(lane >= shift, pltpu.roll(x, shift, axis=axis), NEG))
        shift *= 2
    return x


def _split3(x):
    x1 = x.astype(BF16)
    r1 = x - x1.astype(F32)
    x2 = r1.astype(BF16)
    x3 = (r1 - x2.astype(F32)).astype(BF16)
    return x1, x2, x3


def _dot(a, b):
    return jnp.dot(a, b, preferred_element_type=F32)


def _dot_nt(a, b):
    return lax.dot_general(a, b, (((1,), (1,)), ((), ())), preferred_element_type=F32)


def _dot_tn(a, b):
    return lax.dot_general(a, b, (((0,), (0,)), ((), ())), preferred_element_type=F32)


def _exact_lhs_dot(m_bf16, x):
    x1, x2, x3 = _split3(x)
    return _dot(m_bf16, x1) + _dot(m_bf16, x2) + _dot(m_bf16, x3)


def _exact_rhs_dot(x, m_bf16):
    x1, x2, x3 = _split3(x)
    return _dot(x1, m_bf16) + _dot(x2, m_bf16) + _dot(x3, m_bf16)


def _norm_modulate(x, g, sh, sc):
    ms = jnp.mean(x * x, axis=-1, keepdims=True)
    return x * lax.rsqrt(ms + EPS) * (g * (1.0 + sc)) + sh


def _ada_kernel(c_ref, w_ref, b_ref, o_ref):
    c = c_ref[...]
    ca = _silu(c)
    c1, c2, c3 = _split3(ca)
    w = w_ref[...]
    w1 = w.astype(BF16)
    w2 = (w - w1.astype(F32)).astype(BF16)
    acc = _dot(c1, w1) + _dot(c1, w2) + _dot(c2, w1) + _dot(c2, w2) + _dot(c3, w1)
    o_ref[...] = acc + b_ref[...]


def _ada_call(c, ada_w, ada_b):
    depth, d, n = ada_w.shape
    b = c.shape[0]
    tn = min(ADA_COLS, n)
    assert n % tn == 0
    return pl.pallas_call(
        _ada_kernel,
        grid=(depth, n // tn),
        in_specs=[
            pl.BlockSpec((b, d), lambda l, j: (0, 0)),
            pl.BlockSpec((None, d, tn), lambda l, j: (l, 0, j)),
            pl.BlockSpec((None, 1, tn), lambda l, j: (l, 0, j)),
        ],
        out_specs=pl.BlockSpec((None, b, tn), lambda l, j: (l, 0, j)),
        out_shape=jax.ShapeDtypeStruct((depth, b, n), F32),
        compiler_params=_params("arbitrary", "arbitrary"),
        name="ada_mod",
    )(c, ada_w, ada_b.reshape(depth, 1, n))


def _inproj_kernel(x_ref, sh_ref, sc_ref, g_ref, w_ref, wg_ref, gain_ref, z_ref, gate_ref,
                   *, n_norm, chunk):
    half = x_ref.shape[0] // 2
    h_rows, first_rows = [], []
    for r0 in (0, half):
        h_part = _norm_modulate(x_ref[r0:r0 + half, :], g_ref[...], sh_ref[...], sc_ref[...]).astype(BF16)
        first_rows.append(_dot(h_part, w_ref[:, 0:chunk]))
        h_rows.append(h_part)
    h = jnp.concatenate(h_rows, axis=0)
    n = w_ref.shape[1]
    lane = lax.broadcasted_iota(jnp.int32, (1, LANES), 1)
    lo = lane < HEAD_DIM
    for c0 in range(0, n, chunk):
        zc = jnp.concatenate(first_rows, axis=0) if c0 == 0 else _dot(h, w_ref[:, c0:c0 + chunk])
        if c0 < n_norm:
            parts = []
            for s0 in range(0, chunk, LANES):
                zs = zc[:, s0:s0 + LANES]
                sq = zs * zs
                s_lo = jnp.sum(jnp.where(lo, sq, 0.0), axis=-1, keepdims=True)
                s_hi = jnp.sum(jnp.where(lo, 0.0, sq), axis=-1, keepdims=True)
                r_lo = lax.rsqrt(s_lo + HEAD_DIM * EPS)
                r_hi = lax.rsqrt(s_hi + HEAD_DIM * EPS)
                parts.append(zs * jnp.where(lo, r_lo, r_hi))
            zc = jnp.concatenate(parts, axis=-1) * gain_ref[:, c0:c0 + chunk]
        z_ref[:, c0:c0 + chunk] = zc.astype(BF16)
    gate_ref[...] = _dot(h, wg_ref[...])


def _inproj_call(x, shift, scale, g, w_all, layer, n, wg, gain, n_norm, name):
    b, s, d = x.shape
    tm = min(INPROJ_ROWS, s)
    chunk = INPROJ_CHUNK
    assert s % tm == 0 and tm % 2 == 0 and n % chunk == 0 and n_norm % chunk == 0
    kern = functools.partial(_inproj_kernel, n_norm=n_norm, chunk=chunk)
    return pl.pallas_call(
        kern,
        grid=(b, s // tm),
        in_specs=[
            pl.BlockSpec((None, tm, d), lambda i, j: (i, j, 0)),
            pl.BlockSpec((None, 1, d), lambda i, j: (i, 0, 0)),
            pl.BlockSpec((None, 1, d), lambda i, j: (i, 0, 0)),
            pl.BlockSpec((1, d), lambda i, j: (0, 0)),
            pl.BlockSpec((None, d, n), lambda i, j: (layer, 0, 0)),
            pl.BlockSpec((d, LANES), lambda i, j: (0, 0)),
            pl.BlockSpec((1, n), lambda i, j: (0, 0)),
        ],
        out_specs=[
            pl.BlockSpec((None, tm, n), lambda i, j: (i, j, 0)),
            pl.BlockSpec((None, tm, LANES), lambda i, j: (i, j, 0)),
        ],
        out_shape=[
            jax.ShapeDtypeStruct((b, s, n), BF16),
            jax.ShapeDtypeStruct((b, s, LANES), F32),
        ],
        compiler_params=_params("arbitrary", "arbitrary"),
        name=name,
    )(x, shift, scale, g, w_all, wg, gain)


def _lane_ids():
    return lax.broadcasted_iota(jnp.int32, (1, LANES), 1)


def _place_pieces(pieces, first_lane):
    lane = _lane_ids()
    out = jnp.zeros(pieces[0].shape, F32)
    for j, p in enumerate(pieces):
        out = jnp.where(lane == first_lane + j, p.astype(F32), out)
    return out


def _pack_keys(blocks, shared_fn):
    s = blocks[0][0].shape[0]
    rows = min(PACK_ROWS, s)
    keep_lo = jnp.where(_lane_ids() < HEAD_DIM, 1.0, 0.0).astype(BF16)
    keep_hi = jnp.where(_lane_ids() < HEAD_DIM, 0.0, 1.0).astype(BF16)

    def body(c, carry):
        r0 = pl.multiple_of(c * rows, rows)
        shared = shared_fn(r0, rows)
        for k_ref, v_ref, ka_ref, kb_ref, vt_refs, extras_fn in blocks:
            k2 = k_ref[pl.ds(r0, rows), :]
            ea, eb = extras_fn(shared)
            ka_ref[pl.ds(r0, rows), :] = k2 * keep_lo + ea.astype(BF16)
            kb_ref[pl.ds(r0, rows), :] = k2 * keep_hi + eb.astype(BF16)
            vt = v_ref[pl.ds(r0, rows), :].T
            for ref, c0, width in vt_refs:
                ref[0:width, pl.ds(r0, rows)] = vt[c0:c0 + width, :]
                ref[width:width + ONES_ROWS, pl.ds(r0, rows)] = jnp.ones((ONES_ROWS, rows), BF16)
        return carry

    lax.fori_loop(0, s // rows, body, 0)


def _pack_queries(q2, factors):
    lane = _lane_ids()
    q2 = q2.astype(F32)
    lo = lane < HEAD_DIM
    fac_a = _place_pieces(factors, HEAD_DIM)
    fac_b = _place_pieces(factors, 0)
    return jnp.where(lo, q2, fac_a).astype(BF16), jnp.where(lo, fac_b, q2).astype(BF16)


def _init_stats(m_ref, acc_ref):
    m_ref[...] = jnp.full(m_ref.shape, NEG, F32)
    acc_ref[...] = jnp.zeros(acc_ref.shape, F32)


def _softmax_step(s, tile_max, vt, m_ref, acc_ref):
    m_old = m_ref[...]
    m_new = jnp.maximum(m_old, tile_max)
    alpha = jnp.exp2(m_old - m_new)
    p = jnp.exp2(s - m_new)
    acc_ref[...] = alpha * acc_ref[...] + _dot(vt, p.astype(BF16))
    m_ref[...] = m_new


def _attend(n0, t, chain, s_buf, mx_buf, make_addend, finish_pair):
    tk = ATTN_KEYS
    first = ATTN_RING - 1
    built = {}

    def group(c):
        if c not in built:
            built[c] = chain[c]()
            for streams in built[c]:
                for stream in streams:
                    _init_stats(*stream[4])
        return built[c]

    def qk_one(tiles, u, step, idx, rows=tk):
        k0 = pl.multiple_of(step * tk, tk)
        q, k_ref = tiles[u][idx][0], tiles[u][idx][1]
        s = _dot_nt(k_ref[pl.ds(k0, rows), :], q)
        s_buf[u][idx, 0:rows, :] = s
        if rows == tk:
            mx_buf[u][idx] = jnp.max(s, axis=0, keepdims=True)

    def stage(c, u, step, partner, last=False):
        tiles = group(c)
        k0 = pl.multiple_of(step * tk, tk)
        rows, offset = (u + 1) * t, u * t
        addends = {}
        for idx, (_, _, vt, add_key, stats) in enumerate(tiles[u]):
            if partner is not None:
                p_tiles, p_u, p_step, p_rows = partner
                qk_one(p_tiles, p_u, p_step, idx, p_rows)
            if not last:
                _softmax_step(s_buf[u][idx], mx_buf[u][idx], vt(k0, tk), *stats)
                continue
            if add_key not in addends:
                addends[add_key] = make_addend(add_key, addends.setdefault("shared", {}), rows, offset)
            s = s_buf[u][idx, 0:rows, :] + addends[add_key]
            _softmax_step(s, jnp.max(s, axis=0, keepdims=True), vt(k0, rows), *stats)
            if idx % 2 == 1:
                finish_pair(c, u, idx // 2)

    for idx in range(len(group(0)[first])):
        qk_one(group(0), first, 0, idx)

    for c in range(len(chain)):
        n = n0 + c
        tiles = group(c)

        def ring_pass(j, c=c, tiles=tiles):
            for u in range(first, -1, -1):
                stage(c, u, j, (tiles, u - 1, j, tk) if u > 0 else (tiles, first, j + 1, tk))

        odd = c % 2

        def body(p, carry, ring_pass=ring_pass, odd=odd):
            ring_pass(2 * p + odd)
            ring_pass(2 * p + 1 + odd)
            return carry

        if odd:
            ring_pass(0)
        lax.fori_loop(0, (n - 1) // 2, body, 0)
        for u in range(first, -1, -1):
            if u > 0:
                partner = (tiles, u - 1, n - 1, u * t)
            elif c + 1 < len(chain):
                partner = (group(c + 1), first, 0, tk)
            else:
                partner = None
            stage(c, u, n - 1, partner, last=True)


def _lane_block(ref, h):
    return ref.at[:, pl.ds(h * LANES, LANES)]


def _da_kernel(slope_ref, q_ref, k_ref, v_ref, lam_ref, sg_ref, o_ref,
               ka_ref, kb_ref, vt_ref, m_ref, acc_ref, s0_ref, s1_ref, mx0_ref, mx1_ref,
               *, t, hb, lambda_init):
    dv = LANES
    g = pl.program_id(1)
    i = pl.program_id(2)
    slopes2 = [slope_ref[g * hb + h] * LOG2E for h in range(hb)]

    @pl.when(i == 0)
    def _():
        def position_pieces(r0, rows):
            pos = lax.broadcasted_iota(jnp.int32, (rows, LANES), 0) + r0
            hi = jnp.right_shift(pos, CHUNK_SHIFT).astype(F32)
            lo = jnp.bitwise_and(pos, CHUNK - 1).astype(F32)
            lane = _lane_ids()

            def place(first):
                in_hi = (lane >= first) & (lane < first + BIAS_PIECES)
                in_lo = (lane >= first + BIAS_PIECES) & (lane < first + 2 * BIAS_PIECES)
                return jnp.where(in_hi, hi, jnp.where(in_lo, lo, 0.0)).astype(BF16)
            return place(HEAD_DIM), place(0)
        _pack_keys([(_lane_block(k_ref, h), _lane_block(v_ref, h), ka_ref.at[h], kb_ref.at[h],
                     [(vt_ref.at[h], 0, dv)], lambda shared: shared) for h in range(hb)],
                   position_pieces)

    def last_step_addend(h, shared, rows, offset):
        if "base" not in shared:
            key = lax.broadcasted_iota(jnp.int32, (rows, t), 0)
            qry = lax.broadcasted_iota(jnp.int32, (rows, t), 1) + offset
            allowed = jnp.right_shift(key, CHUNK_SHIFT) <= jnp.right_shift(qry, CHUNK_SHIFT)
            shared["base"] = jnp.where(allowed, jnp.maximum(key - qry, 0).astype(F32), -NEG)
        return (-2.0 * slopes2[h]) * shared["base"]

    def build_group(c):
        tiles = []
        for u in range(ATTN_RING):
            tile = c * ATTN_RING + u
            streams = []
            for h in range(hb):
                slope_pieces = [p.astype(F32) for p in _split3(jnp.full((1, LANES), slopes2[h], F32))]
                factors = [p * float(CHUNK) for p in slope_pieces] + slope_pieces
                qa, qb = _pack_queries(q_ref[tile * t:(tile + 1) * t, h * LANES:(h + 1) * LANES], factors)
                vt = lambda k0, size, h=h: vt_ref[h, :, pl.ds(k0, size)]
                for m, (q, k_sc) in enumerate(((qa, ka_ref), (qb, kb_ref))):
                    n_stream = (tile * hb + h) * 2 + m
                    streams.append((q, k_sc.at[h], vt, h, (m_ref.at[n_stream], acc_ref.at[n_stream])))
            tiles.append(streams)
        return tiles

    def finish_head(c, u, h):
        tile = c * ATTN_RING + u
        lp = lam_ref[...]
        lam = (jnp.exp(jnp.sum(lp[0:1] * lp[1:2], axis=-1, keepdims=True))
               - jnp.exp(jnp.sum(lp[2:3] * lp[3:4], axis=-1, keepdims=True)) + lambda_init)
        pa = acc_ref[(tile * hb + h) * 2]
        pb = acc_ref[(tile * hb + h) * 2 + 1]
        out = pa[0:dv] / pa[dv:dv + 1] - lam * (pb[0:dv] / pb[dv:dv + 1])
        ms = jnp.mean(out * out, axis=0, keepdims=True)
        out = out * lax.rsqrt(ms + EPS) * (sg_ref[...] * (1.0 - lambda_init))
        o_ref[tile * t:(tile + 1) * t, h * LANES:(h + 1) * LANES] = out.T.astype(BF16)

    chain = [functools.partial(build_group, c) for c in range(ATTN_CHAIN)]
    _attend(ATTN_CHAIN * i + 1, t, chain, (s0_ref, s1_ref), (mx0_ref, mx1_ref),
            last_step_addend, finish_head)


def _da_call(z, slopes, lam_p, subln_g, heads, lambda_init):
    b, s, _ = z.shape
    t, tk = ATTN_TILE, ATTN_KEYS
    assert s % tk == 0 and t % CHUNK == 0 and tk == ATTN_RING * t
    assert s <= CHUNK * 256, "key position // CHUNK must stay exact in bf16"
    hb = min(ATTN_BLOCKS, heads)
    groups = heads // hb
    assert heads % hb == 0
    kern = functools.partial(_da_kernel, t=t, hb=hb, lambda_init=lambda_init)
    w = hb * LANES
    n_streams = 2 * hb * ATTN_RING * ATTN_CHAIN
    rows = ATTN_CHAIN * tk
    assert s % rows == 0
    return pl.pallas_call(
        kern,
        grid_spec=pltpu.PrefetchScalarGridSpec(
            num_scalar_prefetch=1,
            grid=(b, groups, s // rows),
            in_specs=[
                pl.BlockSpec((None, rows, w), lambda bi, g, i, sl: (bi, i, g)),
                pl.BlockSpec((None, s, w), lambda bi, g, i, sl: (bi, 0, groups + g)),
                pl.BlockSpec((None, s, w), lambda bi, g, i, sl: (bi, 0, 2 * groups + g)),
                pl.BlockSpec((4, HEAD_DIM), lambda bi, g, i, sl: (0, 0)),
                pl.BlockSpec((LANES, 1), lambda bi, g, i, sl: (0, 0)),
            ],
            out_specs=pl.BlockSpec((None, rows, w), lambda bi, g, i, sl: (bi, i, g)),
            scratch_shapes=[
                pltpu.VMEM((hb, s, LANES), BF16), pltpu.VMEM((hb, s, LANES), BF16),
                pltpu.VMEM((hb, LANES + ONES_ROWS, s), BF16),
                pltpu.VMEM((n_streams, 1, t), F32),
                pltpu.VMEM((n_streams, LANES + ONES_ROWS, t), F32),
                pltpu.VMEM((2 * hb, ATTN_KEYS, t), F32), pltpu.VMEM((2 * hb, ATTN_KEYS, t), F32),
                pltpu.VMEM((2 * hb, 1, t), F32), pltpu.VMEM((2 * hb, 1, t), F32),
            ],
        ),
        out_shape=jax.ShapeDtypeStruct((b, s, heads * LANES), BF16),
        compiler_params=_params("arbitrary", "arbitrary", "arbitrary"),
        name="diff_attention",
    )(slopes, z, z, z, lam_p, subln_g.reshape(LANES, 1))


def _fox_gate_kernel(g_ref, b_ref, o_ref, *, chunk):
    s = g_ref.shape[0]
    r = lax.broadcasted_iota(jnp.int32, (chunk, chunk), 0)
    c = lax.broadcasted_iota(jnp.int32, (chunk, chunk), 1)
    tri = (c <= r).astype(BF16)
    carry = jnp.zeros((1, LANES), F32)
    for c0 in range(0, s, chunk):
        ls = _log_sigmoid(g_ref[c0:c0 + chunk, :] + b_ref[...])
        cs = _exact_lhs_dot(tri, ls) + carry
        carry = cs[chunk - 1:chunk, :]
        o_ref[c0:c0 + chunk, :] = cs * (-LOG2E)


def _fox_gate_call(gates, bias_row):
    b, s, _ = gates.shape
    chunk = min(GATE_CHUNK, s)
    assert s % chunk == 0
    kern = functools.partial(_fox_gate_kernel, chunk=chunk)
    return pl.pallas_call(
        kern,
        grid=(b,),
        in_specs=[
            pl.BlockSpec((None, s, LANES), lambda i: (i, 0, 0)),
            pl.BlockSpec((1, LANES), lambda i: (0, 0)),
        ],
        out_specs=pl.BlockSpec((None, s, LANES), lambda i: (i, 0, 0)),
        out_shape=jax.ShapeDtypeStruct((b, s, LANES), F32),
        compiler_params=_params("arbitrary"),
        name="fox_gate_cumsum",
    )(gates, bias_row)


def _fox_kernel(q_ref, k_ref, v_ref, f_ref, o_ref,
                ka_ref, kb_ref, vt_ref, m_ref, acc_ref, s0_ref, s1_ref, mx0_ref, mx1_ref, *, t, hb):
    dv = HEAD_DIM
    g = pl.program_id(1)
    i = pl.program_id(2)

    @pl.when(i == 0)
    def _():
        half = LANES // 2
        src = lax.broadcasted_iota(jnp.int32, (2 * LANES, LANES), 0)
        dst = lax.broadcasted_iota(jnp.int32, (2 * LANES, LANES), 1)
        piece = jnp.where(src < LANES, 0, jnp.where(src < LANES + half, 1, 2))
        head = src - jnp.where(src < LANES, 0, jnp.where(src < LANES + half, LANES, LANES + half))

        def pieces_of(r0, rows):
            x = f_ref[pl.ds(r0, rows), :]
            x1 = x.astype(BF16)
            r1 = x - x1.astype(F32)
            x2 = r1.astype(BF16).astype(F32)
            tail = jnp.where(_lane_ids() < half, x2, pltpu.roll(r1 - x2, half, axis=1))
            return jnp.concatenate([x1, tail.astype(BF16)], axis=-1)

        blocks = []
        for h in range(hb):
            pair = g * hb + h
            place_a = ((head == 2 * pair) & (dst == HEAD_DIM + piece)).astype(BF16)
            place_b = ((head == 2 * pair + 1) & (dst == piece)).astype(BF16)
            place = jnp.concatenate([place_a, place_b], axis=-1)

            def extras(pieces, place=place):
                both = _dot(pieces, place)
                return both[:, 0:LANES], both[:, LANES:2 * LANES]
            blocks.append((_lane_block(k_ref, h), _lane_block(v_ref, h), ka_ref.at[h], kb_ref.at[h],
                           [(vt_ref.at[2 * h + m], m * dv, dv) for m in range(2)], extras))
        _pack_keys(blocks, pieces_of)

    def last_step_addend(_, shared, rows, offset):
        key = lax.broadcasted_iota(jnp.int32, (rows, t), 0)
        qry = lax.broadcasted_iota(jnp.int32, (rows, t), 1) + offset
        return jnp.where(key <= qry, 0.0, NEG)

    def build_group(c):
        ones = [jnp.ones((1, LANES), F32)] * BIAS_PIECES
        tiles = []
        for u in range(ATTN_RING):
            tile = c * ATTN_RING + u
            streams = []
            for h in range(hb):
                qa, qb = _pack_queries(q_ref[tile * t:(tile + 1) * t, h * LANES:(h + 1) * LANES], ones)
                for m, (q, k_sc) in enumerate(((qa, ka_ref), (qb, kb_ref))):
                    vt = lambda k0, size, n=2 * h + m: vt_ref[n, :, pl.ds(k0, size)]
                    n_stream = (tile * hb + h) * 2 + m
                    streams.append((q, k_sc.at[h], vt, 0, (m_ref.at[n_stream], acc_ref.at[n_stream])))
            tiles.append(streams)
        return tiles

    def finish_pair(c, u, h):
        tile = c * ATTN_RING + u
        pa = acc_ref[(tile * hb + h) * 2]
        pb = acc_ref[(tile * hb + h) * 2 + 1]
        out = jnp.concatenate([pa[0:dv] / pa[dv:dv + 1], pb[0:dv] / pb[dv:dv + 1]], axis=0)
        o_ref[tile * t:(tile + 1) * t, h * LANES:(h + 1) * LANES] = out.T.astype(BF16)

    chain = [functools.partial(build_group, c) for c in range(ATTN_CHAIN)]
    _attend(ATTN_CHAIN * i + 1, t, chain, (s0_ref, s1_ref), (mx0_ref, mx1_ref),
            last_step_addend, finish_pair)


def _fox_call(z, fneg, heads):
    b, s, _ = z.shape
    pairs = heads // 2
    t, tk = ATTN_TILE, ATTN_KEYS
    assert s % tk == 0 and tk == ATTN_RING * t
    assert heads <= LANES // 2, "the forget-gate pieces of all heads share 128 + 64 + 64 contraction rows"
    hb = min(ATTN_BLOCKS, pairs)
    groups = pairs // hb
    assert pairs % hb == 0
    kern = functools.partial(_fox_kernel, t=t, hb=hb)
    w = hb * LANES
    n_streams = 2 * hb * ATTN_RING * ATTN_CHAIN
    rows = ATTN_CHAIN * tk
    assert s % rows == 0
    return pl.pallas_call(
        kern,
        grid=(b, groups, s // rows),
        in_specs=[
            pl.BlockSpec((None, rows, w), lambda bi, g, i: (bi, i, g)),
            pl.BlockSpec((None, s, w), lambda bi, g, i: (bi, 0, groups + g)),
            pl.BlockSpec((None, s, w), lambda bi, g, i: (bi, 0, 2 * groups + g)),
            pl.BlockSpec((None, s, LANES), lambda bi, g, i: (bi, 0, 0)),
        ],
        out_specs=pl.BlockSpec((None, rows, w), lambda bi, g, i: (bi, i, g)),
        out_shape=jax.ShapeDtypeStruct((b, s, pairs * LANES), BF16),
        scratch_shapes=[
            pltpu.VMEM((hb, s, LANES), BF16), pltpu.VMEM((hb, s, LANES), BF16),
            pltpu.VMEM((2 * hb, HEAD_DIM + ONES_ROWS, s), BF16),
            pltpu.VMEM((n_streams, 1, t), F32),
            pltpu.VMEM((n_streams, HEAD_DIM + ONES_ROWS, t), F32),
            pltpu.VMEM((2 * hb, ATTN_KEYS, t), F32), pltpu.VMEM((2 * hb, ATTN_KEYS, t), F32),
            pltpu.VMEM((2 * hb, 1, t), F32), pltpu.VMEM((2 * hb, 1, t), F32),
        ],
        compiler_params=_params("arbitrary", "arbitrary", "arbitrary"),
        name="forgetting_attention",
    )(z, z, z, fneg)


def _mlstm_kernel(q_ref, k_ref, v_ref, o_ref, g_ref, gb_ref, cw_ref, cb_ref, y_ref,
                  halo_ref, ct_ref, m_ref, *, heads, taps, lc):
    @pl.when(pl.program_id(1) == 0)
    def _():
        halo_ref[...] = jnp.zeros(halo_ref.shape, F32)
        ct_ref[...] = jnp.zeros(ct_ref.shape, F32)
        m_ref[...] = jnp.zeros(m_ref.shape, F32)

    rows = lambda ref, sub: ref.at[pl.ds(sub * lc, lc), :]
    edge = halo_ref[...]
    for sub in range(q_ref.shape[0] // lc):
        gates = _mlstm_gates(rows(g_ref, sub), gb_ref, heads)
        prep, edge = _mlstm_prep(rows(q_ref, sub), rows(k_ref, sub), gates, cw_ref, cb_ref,
                                 edge, taps)
        for hd in range(heads):
            _mlstm_head(prep, rows(v_ref, sub), rows(o_ref, sub), rows(y_ref, sub),
                        ct_ref, m_ref, hd, heads)
    halo_ref[...] = edge


def _mlstm_gates(g_ref, gb_ref, heads):
    lc = g_ref.shape[0]
    r = lax.broadcasted_iota(jnp.int32, (lc, lc), 0)
    c = lax.broadcasted_iota(jnp.int32, (lc, lc), 1)
    causal = c <= r

    gcol = g_ref[...] + gb_ref[...]
    grow = gcol.T[:2 * heads, :]
    a_cols = _exact_lhs_dot(causal.astype(BF16), _log_sigmoid(gcol))
    a_rows = _exact_rhs_dot(_log_sigmoid(grow), (r <= c).astype(BF16))

    b_half = grow[0:heads, :] - a_rows[heads:2 * heads, :]
    b_rows = jnp.concatenate([b_half, b_half], axis=0)
    return dict(gcol=gcol, a_cols=a_cols, b_rows=b_rows, causal=causal,
                prefix_cols=_prefix_max_lanes(b_rows).T)


def _mlstm_prep(q_ref, k_ref, gates, cw_ref, cb_ref, halo, taps):
    lc = q_ref.shape[0]
    pad = halo.shape[0]
    r = lax.broadcasted_iota(jnp.int32, (lc, lc), 0)
    c = lax.broadcasted_iota(jnp.int32, (lc, lc), 1)

    xb = jnp.concatenate([q_ref[...], k_ref[...]], axis=-1)
    xf = xb.astype(F32)
    conv = cb_ref[...] + xf * cw_ref[taps - 1:taps, :]
    for sft in range(1, taps):
        shifted = _dot((r - c == sft).astype(BF16), xb)
        conv = conv + shifted * cw_ref[taps - 1 - sft:taps - sft, :]
    edge = jnp.concatenate([halo, xf[0:pad, :]], axis=0)
    first = cb_ref[...]
    for j in range(taps):
        off = pad - (taps - 1) + j
        first = first + edge[off:off + pad, :] * cw_ref[j:j + 1, :]
    conv = jnp.concatenate([first, conv[pad:, :]], axis=0)
    return dict(gates, qk=_silu(conv)), xf[lc - pad:, :]


def _mlstm_head(prep, v_ref, o_ref, y_ref, ct_ref, m_ref, hd, heads):
    lc, width = v_ref.shape
    d = width // heads
    qk, gcol, a_cols, causal = prep["qk"], prep["gcol"], prep["a_cols"], prep["causal"]
    q = qk[:, hd * d:(hd + 1) * d]
    k = qk[:, width + hd * d:width + (hd + 1) * d] * (d ** -0.5)
    v = v_ref[:, hd * d:(hd + 1) * d]
    qb = q.astype(BF16)
    kb = k.astype(BF16)
    ig_c = gcol[:, hd:hd + 1]
    a_c = a_cols[:, heads + hd:heads + hd + 1]
    a_last = a_c[lc - 1:lc, :]
    ct = ct_ref[hd]
    m_prev = m_ref[hd:hd + 1, 0:1]
    x_c = jnp.maximum(prep["prefix_cols"][:, hd:hd + 1], m_prev)

    g_c = a_last - a_c + ig_c
    m_loc = jnp.max(g_c, axis=0, keepdims=True)
    w_c = jnp.exp(g_c - m_loc)
    m_new = jnp.maximum(a_last + m_prev, m_loc)
    decay = jnp.exp(a_last + m_prev - m_new)
    s_loc = jnp.exp(m_loc - m_new)
    wv = jnp.concatenate([w_c * v.astype(F32), jnp.broadcast_to(w_c, (lc, d))], axis=-1)
    increment = _dot_tn(kb, wv.astype(BF16))
    raw = _dot_nt(qb, kb)
    from_state = _dot(qb, ct.astype(BF16))

    decay_mat = jnp.exp2(jnp.where(causal, prep["b_rows"][hd:hd + 1, :] * LOG2E - x_c * LOG2E, NEG))
    sqk = (raw * decay_mat).astype(BF16)
    intra = _dot(sqk, jnp.concatenate([v, jnp.ones((lc, d), BF16)], axis=-1))
    inter = jnp.exp(m_prev - x_c) * from_state
    num = inter[:, 0:d] + intra[:, 0:d]
    den = inter[:, d:d + 1] + intra[:, d:d + 1]
    hval = num / jnp.maximum(jnp.abs(den), jnp.exp(-(a_c + x_c)))
    og = o_ref[:, hd * d:(hd + 1) * d].astype(F32)
    y_ref[:, hd * d:(hd + 1) * d] = (_sigmoid(og) * hval).astype(BF16)

    ct_ref[hd] = decay * ct + s_loc * increment
    m_ref[hd:hd + 1, :] = jnp.broadcast_to(m_new, (1, LANES))


def _mlstm_call(z, gates, gate_bias, conv_w, conv_b, heads, col0):
    b, s, _ = z.shape
    width = conv_w.shape[1] // 2
    lc = min(ML_CHUNK, s)
    rows = min(ML_CHUNKS_PER_STEP * lc, s)
    assert s % rows == 0 and rows % lc == 0 and col0 % width == 0 and width // heads == LANES
    assert 2 * heads == 8, "gate rows are handled as one 8-sublane tile"
    cb = col0 // width
    taps = conv_w.shape[0]
    kern = functools.partial(_mlstm_kernel, heads=heads, taps=taps, lc=lc)
    zspec = lambda off: pl.BlockSpec((None, rows, width), lambda bi, ci: (bi, ci, cb + off))
    return pl.pallas_call(
        kern,
        grid=(b, s // rows),
        in_specs=[
            zspec(0), zspec(1), zspec(2), zspec(3),
            pl.BlockSpec((None, rows, LANES), lambda bi, ci: (bi, ci, 0)),
            pl.BlockSpec((1, LANES), lambda bi, ci: (0, 0)),
            pl.BlockSpec((taps, 2 * width), lambda bi, ci: (0, 0)),
            pl.BlockSpec((1, 2 * width), lambda bi, ci: (0, 0)),
        ],
        out_specs=pl.BlockSpec((None, rows, width), lambda bi, ci: (bi, ci, 0)),
        out_shape=jax.ShapeDtypeStruct((b, s, width), BF16),
        scratch_shapes=[
            pltpu.VMEM((8, 2 * width), F32),
            pltpu.VMEM((heads, LANES, 2 * LANES), F32),
            pltpu.VMEM((2 * heads, LANES), F32),
        ],
        compiler_params=_params("arbitrary", "arbitrary"),
        name="mlstm",
    )(z, z, z, z, gates, gate_bias, conv_w, conv_b)


def _mix_ffn_kernel(*refs, n_in, chunk):
    y_refs = refs[:n_in]
    wo_refs = refs[n_in:2 * n_in]
    (x_ref, gm_ref, sh_ref, sc_ref, g_ref, gate_ref, w1_ref, w3_ref, w2_ref, o_ref) = refs[2 * n_in:]
    hidden = w1_ref.shape[1]
    sub = min(FFN_ROWS, x_ref.shape[0])

    def mixed(r0):
        mix = _dot(y_refs[0][r0:r0 + sub, :], wo_refs[0][...])
        for y_ref, w_ref in zip(y_refs[1:], wo_refs[1:]):
            mix = mix + _dot(y_ref[r0:r0 + sub, :], w_ref[...])
        x = x_ref[r0:r0 + sub, :] + gm_ref[...] * mix
        return x, _norm_modulate(x, g_ref[...], sh_ref[...], sc_ref[...]).astype(BF16)

    def ffn(r0, x, h):
        acc = None
        for c0 in range(0, hidden, chunk):
            a = _dot(h, w1_ref[:, c0:c0 + chunk])
            bb = _dot(h, w3_ref[:, c0:c0 + chunk])
            u = (_silu(a) * bb).astype(BF16)
            part = _dot(u, w2_ref[c0:c0 + chunk, :])
            acc = part if acc is None else acc + part
        o_ref[r0:r0 + sub, :] = x + gate_ref[...] * acc

    starts = list(range(0, x_ref.shape[0], sub))
    ready = mixed(starts[0])
    for k, r0 in enumerate(starts):
        current = ready
        if k + 1 < len(starts):
            ready = mixed(starts[k + 1])
        ffn(r0, *current)


def _mix_ffn_call(ys, wo_all, wo_layer, x, gate_mix, shift, scale, g, gate, w1, w3, w2, layer, name):
    b, s, d = x.shape
    hidden = w1.shape[2]
    tm = min(ROW_TILE, s)
    assert s % tm == 0 and hidden % FFN_CHUNK == 0
    assert all(y.shape[2] == ys[0].shape[2] for y in ys)
    kern = functools.partial(_mix_ffn_kernel, n_in=len(ys), chunk=FFN_CHUNK)
    vec = pl.BlockSpec((None, 1, d), lambda i, j: (i, 0, 0))
    rows = lambda width: pl.BlockSpec((None, tm, width), lambda i, j: (i, j, 0))
    stacked = lambda w: pl.BlockSpec((None,) + w.shape[1:], lambda i, j: (layer, 0, 0))
    wo_specs = [pl.BlockSpec((None, y.shape[2], d), lambda i, j, n=n: (wo_layer, n, 0))
                for n, y in enumerate(ys)]
    return pl.pallas_call(
        kern,
        grid=(b, s // tm),
        in_specs=[rows(y.shape[2]) for y in ys] + wo_specs + [
            rows(d), vec, vec, vec,
            pl.BlockSpec((1, d), lambda i, j: (0, 0)),
            vec, stacked(w1), stacked(w3), stacked(w2),
        ],
        out_specs=rows(d),
        out_shape=jax.ShapeDtypeStruct((b, s, d), F32),
        compiler_params=_params("arbitrary", "arbitrary"),
        name=name,
    )(*ys, *([wo_all] * len(ys)), x, gate_mix, shift, scale, g, gate, w1, w3, w2)


def _pad_cols(w, n):
    return jnp.pad(w, ((0, 0), (0, n - w.shape[1])))


def _gain_row(q_g, k_g, width, total):
    reps = width // HEAD_DIM
    row = jnp.concatenate([jnp.tile(q_g.astype(F32) * LOG2E, reps),
                           jnp.tile(k_g.astype(F32) * HEAD_DIM ** 0.5, reps),
                           jnp.ones((total - 2 * width,), F32)])
    return row.reshape(1, total)


def kernel(x, c, ada_w, ada_b, norm_mix_g, norm_ffn_g, ab_w_in, ml_b_i, ml_b_f, ml_conv_w, ml_conv_b,
           da_q_g, da_k_g, da_lambda, da_subln_g, ab_w_out, fx_w_in, fx_b_f, fx_q_g, fx_k_g, fx_w_out,
           ffn_w1, ffn_w3, ffn_w2):
    depth = ada_w.shape[0]
    b, s, d = x.shape
    ml_width = ml_conv_w.shape[2] // 2
    ml_heads = ml_b_i.shape[1]
    da_width = (ab_w_in.shape[2] - 4 * ml_width - 2 * ml_heads) // 3
    da_heads = da_width // (2 * HEAD_DIM)
    fx_heads = fx_b_f.shape[1]
    fx_width = fx_heads * HEAD_DIM
    ab_main = 3 * da_width + 4 * ml_width
    fx_main = 3 * fx_width

    mods = _ada_call(c, ada_w, ada_b).reshape(depth, b, 6, 1, d)
    slopes = 2.0 ** (-8.0 * jnp.arange(1, da_heads + 1, dtype=F32) / da_heads)

    ab_w_in_b, fx_w_in_b = ab_w_in.astype(BF16), fx_w_in.astype(BF16)
    ab_w_out_b, fx_w_out_b = ab_w_out.astype(BF16), fx_w_out.astype(BF16)
    ffn_w1_b, ffn_w3_b, ffn_w2_b = ffn_w1.astype(BF16), ffn_w3.astype(BF16), ffn_w2.astype(BF16)

    for l in range(depth):
        sh_m, sc_m, g_m, sh_f, sc_f, g_f = (mods[l, :, t] for t in range(6))
        j = l // 2
        g_mix = norm_mix_g[l].reshape(1, d)
        if l % 2 == 0:
            z, gates = _inproj_call(
                x, sh_m, sc_m, g_mix, ab_w_in_b, j, ab_main,
                _pad_cols(ab_w_in[j][:, ab_main:], LANES).astype(BF16),
                _gain_row(da_q_g[j], da_k_g[j], da_width, ab_main),
                2 * da_width, "inproj_even")
            lambda_init = 0.8 - 0.6 * math.exp(-0.3 * l)
            y_da = _da_call(z, slopes, da_lambda[j], da_subln_g[j], da_heads, lambda_init)
            gate_bias = _pad_cols(jnp.concatenate([ml_b_i[j], ml_b_f[j]]).reshape(1, -1), LANES)
            y_ml = _mlstm_call(z, gates, gate_bias, ml_conv_w[j], ml_conv_b[j].reshape(1, -1),
                               ml_heads, 3 * da_width)
            ys, wo_all, name = [y_da, y_ml], ab_w_out_b, "mix_ffn_even"
        else:
            z, gates = _inproj_call(
                x, sh_m, sc_m, g_mix, fx_w_in_b, j, fx_main,
                _pad_cols(fx_w_in[j][:, fx_main:], LANES).astype(BF16),
                _gain_row(fx_q_g[j], fx_k_g[j], fx_width, fx_main),
                2 * fx_width, "inproj_odd")
            fneg = _fox_gate_call(gates, _pad_cols(fx_b_f[j].reshape(1, -1), LANES))
            ys, wo_all, name = [_fox_call(z, fneg, fx_heads)], fx_w_out_b, "mix_ffn_odd"
        x = _mix_ffn_call(ys, wo_all, j, x, g_m, sh_f, sc_f, norm_ffn_g[l].reshape(1, d), g_f,
                          ffn_w1_b, ffn_w3_b, ffn_w2_b, l, name)
    return x
```

```python
import functools
import math

import jax
import jax.numpy as jnp
from jax import lax
from jax.experimental import pallas as pl
from jax.experimental.pallas import tpu as pltpu

F32 = jnp.float32
BF16 = jnp.bfloat16

EPS = 1e-6
LANES = 128
HEAD_DIM = 64
CHUNK = 64
CHUNK_SHIFT = CHUNK.bit_length() - 1
assert CHUNK == 1 << CHUNK_SHIFT
NEG = -1e30
LOG2E = math.log2(math.e)
BIAS_PIECES = 3

VMEM_LIMIT = 56 * 1024 * 1024

ADA_COLS = 1536
ROW_TILE = 1024
FFN_ROWS = 512
INPROJ_ROWS = 1024
INPROJ_CHUNK = 512
ATTN_TILE = 256
ATTN_RING = 2
ATTN_KEYS = ATTN_RING * ATTN_TILE
ATTN_CHAIN = 2
assert ATTN_CHAIN % 2 == 0
ATTN_BLOCKS = 4
PACK_ROWS = 512
ML_CHUNK = 256
ML_CHUNKS_PER_STEP = 2
GATE_CHUNK = 256
FFN_CHUNK = 256


def _params(*sem):
    return pltpu.CompilerParams(dimension_semantics=sem, vmem_limit_bytes=VMEM_LIMIT)


def _log_sigmoid(x):
    return jnp.minimum(x, 0.0) - jnp.log1p(jnp.exp(-jnp.abs(x)))


def _sigmoid(x):
    return 0.5 * jnp.tanh(0.5 * x) + 0.5


def _silu(x):
    h = 0.5 * x
    return h * (jnp.tanh(h) + 1.0)


def _prefix_max_lanes(x):
    axis = x.ndim - 1
    lane = lax.broadcasted_iota(jnp.int32, x.shape, axis)
    shift = 1
    while shift < x.shape[axis]:
        x = jnp.maximum(x, jnp.where(lane >= shift, pltpu.roll(x, shift, axis=axis), NEG))
        shift *= 2
    return x


def _split3(x):
    x1 = x.astype(BF16)
    r1 = x - x1.astype(F32)
    x2 = r1.astype(BF16)
    x3 = (r1 - x2.astype(F32)).astype(BF16)
    return x1, x2, x3


def _dot(a, b):
    return jnp.dot(a, b, preferred_element_type=F32)


def _dot_nt(a, b):
    return lax.dot_general(a, b, (((1,), (1,)), ((), ())), preferred_element_type=F32)


def _dot_tn(a, b):
    return lax.dot_general(a, b, (((0,), (0,)), ((), ())), preferred_element_type=F32)


def _exact_lhs_dot(m_bf16, x):
    x1, x2, x3 = _split3(x)
    return _dot(m_bf16, x1) + _dot(m_bf16, x2) + _dot(m_bf16, x3)


def _exact_rhs_dot(x, m_bf16):
    x1, x2, x3 = _split3(x)
    return _dot(x1, m_bf16) + _dot(x2, m_bf16) + _dot(x3, m_bf16)


def _norm_modulate(x, g, sh, sc):
    ms = jnp.mean(x * x, axis=-1, keepdims=True)
    return x * lax.rsqrt(ms + EPS) * (g * (1.0 + sc)) + sh


def _ada_kernel(c_ref, w_ref, b_ref, o_ref):
    c = c_ref[...]
    ca = _silu(c)
    c1, c2, c3 = _split3(ca)
    w = w_ref[...]
    w1 = w.astype(BF16)
    w2 = (w - w1.astype(F32)).astype(BF16)
    acc = _dot(c1, w1) + _dot(c1, w2) + _dot(c2, w1) + _dot(c2, w2) + _dot(c3, w1)
    o_ref[...] = acc + b_ref[...]


def _ada_call(c, ada_w, ada_b):
    depth, d, n = ada_w.shape
    b = c.shape[0]
    tn = min(ADA_COLS, n)
    assert n % tn == 0
    return pl.pallas_call(
        _ada_kernel,
        grid=(depth, n // tn),
        in_specs=[
            pl.BlockSpec((b, d), lambda l, j: (0, 0)),
            pl.BlockSpec((None, d, tn), lambda l, j: (l, 0, j)),
            pl.BlockSpec((None, 1, tn), lambda l, j: (l, 0, j)),
        ],
        out_specs=pl.BlockSpec((None, b, tn), lambda l, j: (l, 0, j)),
        out_shape=jax.ShapeDtypeStruct((depth, b, n), F32),
        compiler_params=_params("arbitrary", "arbitrary"),
        name="ada_mod",
    )(c, ada_w, ada_b.reshape(depth, 1, n))


def _inproj_kernel(x_ref, sh_ref, sc_ref, g_ref, w_ref, wg_ref, gain_ref, z_ref, gate_ref,
                   *, n_norm, chunk):
    half = x_ref.shape[0] // 2
    h_rows, first_rows = [], []
    for r0 in (0, half):
        h_part = _norm_modulate(x_ref[r0:r0 + half, :], g_ref[...], sh_ref[...], sc_ref[...]).astype(BF16)
        first_rows.append(_dot(h_part, w_ref[:, 0:chunk]))
        h_rows.append(h_part)
    h = jnp.concatenate(h_rows, axis=0)
    n = w_ref.shape[1]
    lane = lax.broadcasted_iota(jnp.int32, (1, LANES), 1)
    lo = lane < HEAD_DIM
    for c0 in range(0, n, chunk):
        zc = jnp.concatenate(first_rows, axis=0) if c0 == 0 else _dot(h, w_ref[:, c0:c0 + chunk])
        if c0 < n_norm:
            parts = []
            for s0 in range(0, chunk, LANES):
                zs = zc[:, s0:s0 + LANES]
                sq = zs * zs
                s_lo = jnp.sum(jnp.where(lo, sq, 0.0), axis=-1, keepdims=True)
                s_hi = jnp.sum(jnp.where(lo, 0.0, sq), axis=-1, keepdims=True)
                r_lo = lax.rsqrt(s_lo + HEAD_DIM * EPS)
                r_hi = lax.rsqrt(s_hi + HEAD_DIM * EPS)
                parts.append(zs * jnp.where(lo, r_lo, r_hi))
            zc = jnp.concatenate(parts, axis=-1) * gain_ref[:, c0:c0 + chunk]
        z_ref[:, c0:c0 + chunk] = zc.astype(BF16)
    gate_ref[...] = _dot(h, wg_ref[...])


def _inproj_call(x, shift, scale, g, w_all, layer, n, wg, gain, n_norm, name):
    b, s, d = x.shape
    tm = min(INPROJ_ROWS, s)
    chunk = INPROJ_CHUNK
    assert s % tm == 0 and tm % 2 == 0 and n % chunk == 0 and n_norm % chunk == 0
    kern = functools.partial(_inproj_kernel, n_norm=n_norm, chunk=chunk)
    return pl.pallas_call(
        kern,
        grid=(b, s // tm),
        in_specs=[
            pl.BlockSpec((None, tm, d), lambda i, j: (i, j, 0)),
            pl.BlockSpec((None, 1, d), lambda i, j: (i, 0, 0)),
            pl.BlockSpec((None, 1, d), lambda i, j: (i, 0, 0)),
            pl.BlockSpec((1, d), lambda i, j: (0, 0)),
            pl.BlockSpec((None, d, n), lambda i, j: (layer, 0, 0)),
            pl.BlockSpec((d, LANES), lambda i, j: (0, 0)),
            pl.BlockSpec((1, n), lambda i, j: (0, 0)),
        ],
        out_specs=[
            pl.BlockSpec((None, tm, n), lambda i, j: (i, j, 0)),
            pl.BlockSpec((None, tm, LANES), lambda i, j: (i, j, 0)),
        ],
        out_shape=[
            jax.ShapeDtypeStruct((b, s, n), BF16),
            jax.ShapeDtypeStruct((b, s, LANES), F32),
        ],
        compiler_params=_params("arbitrary", "arbitrary"),
        name=name,
    )(x, shift, scale, g, w_all, wg, gain)


def _lane_ids():
    return lax.broadcasted_iota(jnp.int32, (1, LANES), 1)


def _place_pieces(pieces, first_lane):
    lane = _lane_ids()
    out = jnp.zeros(pieces[0].shape, F32)
    for j, p in enumerate(pieces):
        out = jnp.where(lane == first_lane + j, p.astype(F32), out)
    return out


def _pack_keys(blocks, shared_fn):
    s = blocks[0][0].shape[0]
    rows = min(PACK_ROWS, s)
    keep_lo = jnp.where(_lane_ids() < HEAD_DIM, 1.0, 0.0).astype(BF16)
    keep_hi = jnp.where(_lane_ids() < HEAD_DIM, 0.0, 1.0).astype(BF16)

    def body(c, carry):
        r0 = pl.multiple_of(c * rows, rows)
        shared = shared_fn(r0, rows)
        for k_ref, v_ref, ka_ref, kb_ref, vt_refs, extras_fn in blocks:
            k2 = k_ref[pl.ds(r0, rows), :]
            ea, eb = extras_fn(shared)
            ka_ref[pl.ds(r0, rows), :] = k2 * keep_lo + ea.astype(BF16)
            kb_ref[pl.ds(r0, rows), :] = k2 * keep_hi + eb.astype(BF16)
            vt = v_ref[pl.ds(r0, rows), :].T
            for ref, c0, width in vt_refs:
                ref[:, pl.ds(r0, rows)] = vt[c0:c0 + width, :]
        return carry

    lax.fori_loop(0, s // rows, body, 0)


def _pack_queries(q2, factors):
    lane = _lane_ids()
    q2 = q2.astype(F32)
    lo = lane < HEAD_DIM
    fac_a = _place_pieces(factors, HEAD_DIM)
    fac_b = _place_pieces(factors, 0)
    return jnp.where(lo, q2, fac_a).astype(BF16), jnp.where(lo, fac_b, q2).astype(BF16)


def _init_stats(ml_ref, acc_ref):
    ml_ref[0:1, :] = jnp.full((1, ml_ref.shape[1]), NEG, F32)
    ml_ref[1:2, :] = jnp.zeros((1, ml_ref.shape[1]), F32)
    acc_ref[...] = jnp.zeros(acc_ref.shape, F32)


def _softmax_step(s, tile_max, vt, ml_ref, acc_ref):
    m_old = ml_ref[0:1, :]
    m_new = jnp.maximum(m_old, tile_max)
    alpha = jnp.exp2(m_old - m_new)
    p = jnp.exp2(s - m_new)
    ml_ref[1:2, :] = alpha * ml_ref[1:2, :] + jnp.sum(p, axis=0, keepdims=True)
    acc_ref[...] = alpha * acc_ref[...] + _dot(vt, p.astype(BF16))
    ml_ref[0:1, :] = m_new


def _attend(n0, t, chain, s_buf, mx_buf, make_addend, finish_pair):
    tk = ATTN_KEYS
    first = ATTN_RING - 1
    built = {}

    def group(c):
        if c not in built:
            built[c] = chain[c]()
            for streams in built[c]:
                for stream in streams:
                    _init_stats(*stream[4])
        return built[c]

    def qk_one(tiles, u, step, idx, rows=tk):
        k0 = pl.multiple_of(step * tk, tk)
        q, k_ref = tiles[u][idx][0], tiles[u][idx][1]
        s = _dot_nt(k_ref[pl.ds(k0, rows), :], q)
        s_buf[u][idx, 0:rows, :] = s
        if rows == tk:
            mx_buf[u][idx] = jnp.max(s, axis=0, keepdims=True)

    def stage(c, u, step, partner, last=False):
        tiles = group(c)
        k0 = pl.multiple_of(step * tk, tk)
        rows, offset = (u + 1) * t, u * t
        addends = {}
        for idx, (_, _, vt, add_key, stats) in enumerate(tiles[u]):
            if partner is not None:
                p_tiles, p_u, p_step, p_rows = partner
                qk_one(p_tiles, p_u, p_step, idx, p_rows)
            if not last:
                _softmax_step(s_buf[u][idx], mx_buf[u][idx], vt(k0, tk), *stats)
                continue
            if add_key not in addends:
                addends[add_key] = make_addend(add_key, addends.setdefault("shared", {}), rows, offset)
            s = s_buf[u][idx, 0:rows, :] + addends[add_key]
            _softmax_step(s, jnp.max(s, axis=0, keepdims=True), vt(k0, rows), *stats)
            if idx % 2 == 1:
                finish_pair(c, u, idx // 2)

    for idx in range(len(group(0)[first])):
        qk_one(group(0), first, 0, idx)

    for c in range(len(chain)):
        n = n0 + c
        tiles = group(c)

        def ring_pass(j, c=c, tiles=tiles):
            for u in range(first, -1, -1):
                stage(c, u, j, (tiles, u - 1, j, tk) if u > 0 else (tiles, first, j + 1, tk))

        odd = c % 2

        def body(p, carry, ring_pass=ring_pass, odd=odd):
            ring_pass(2 * p + odd)
            ring_pass(2 * p + 1 + odd)
            return carry

        if odd:
            ring_pass(0)
        lax.fori_loop(0, (n - 1) // 2, body, 0)
        for u in range(first, -1, -1):
            if u > 0:
                partner = (tiles, u - 1, n - 1, u * t)
            elif c + 1 < len(chain):
                partner = (group(c + 1), first, 0, tk)
            else:
                partner = None
            stage(c, u, n - 1, partner, last=True)


def _lane_block(ref, h):
    return ref.at[:, pl.ds(h * LANES, LANES)]


def _da_kernel(slope_ref, q_ref, k_ref, v_ref, lam_ref, sg_ref, o_ref,
               ka_ref, kb_ref, vt_ref, m_ref, acc_ref, s0_ref, s1_ref, mx0_ref, mx1_ref,
               *, t, hb, lambda_init):
    dv = LANES
    g = pl.program_id(1)
    i = pl.program_id(2)
    slopes2 = [slope_ref[g * hb + h] * LOG2E for h in range(hb)]

    @pl.when(i == 0)
    def _():
        def position_pieces(r0, rows):
            pos = lax.broadcasted_iota(jnp.int32, (rows, LANES), 0) + r0
            hi = jnp.right_shift(pos, CHUNK_SHIFT).astype(F32)
            lo = jnp.bitwise_and(pos, CHUNK - 1).astype(F32)
            lane = _lane_ids()

            def place(first):
                in_hi = (lane >= first) & (lane < first + BIAS_PIECES)
                in_lo = (lane >= first + BIAS_PIECES) & (lane < first + 2 * BIAS_PIECES)
                return jnp.where(in_hi, hi, jnp.where(in_lo, lo, 0.0)).astype(BF16)
            return place(HEAD_DIM), place(0)
        _pack_keys([(_lane_block(k_ref, h), _lane_block(v_ref, h), ka_ref.at[h], kb_ref.at[h],
                     [(vt_ref.at[h], 0, dv)], lambda shared: shared) for h in range(hb)],
                   position_pieces)

    def last_step_addend(h, shared, rows, offset):
        if "base" not in shared:
            key = lax.broadcasted_iota(jnp.int32, (rows, t), 0)
            qry = lax.broadcasted_iota(jnp.int32, (rows, t), 1) + offset
            allowed = jnp.right_shift(key, CHUNK_SHIFT) <= jnp.right_shift(qry, CHUNK_SHIFT)
            shared["base"] = jnp.where(allowed, jnp.maximum(key - qry, 0).astype(F32), -NEG)
        return (-2.0 * slopes2[h]) * shared["base"]

    def build_group(c):
        tiles = []
        for u in range(ATTN_RING):
            tile = c * ATTN_RING + u
            streams = []
            for h in range(hb):
                slope_pieces = [p.astype(F32) for p in _split3(jnp.full((1, LANES), slopes2[h], F32))]
                factors = [p * float(CHUNK) for p in slope_pieces] + slope_pieces
                qa, qb = _pack_queries(q_ref[tile * t:(tile + 1) * t, h * LANES:(h + 1) * LANES], factors)
                vt = lambda k0, size, h=h: vt_ref[h, :, pl.ds(k0, size)]
                for m, (q, k_sc) in enumerate(((qa, ka_ref), (qb, kb_ref))):
                    n_stream = (tile * hb + h) * 2 + m
                    streams.append((q, k_sc.at[h], vt, h, (m_ref.at[n_stream], acc_ref.at[n_stream])))
            tiles.append(streams)
        return tiles

    def finish_head(c, u, h):
        tile = c * ATTN_RING + u
        lp = lam_ref[...]
        lam = (jnp.exp(jnp.sum(lp[0:1] * lp[1:2], axis=-1, keepdims=True))
               - jnp.exp(jnp.sum(lp[2:3] * lp[3:4], axis=-1, keepdims=True)) + lambda_init)
        na, nb = (tile * hb + h) * 2, (tile * hb + h) * 2 + 1
        out = acc_ref[na] / m_ref[na][1:2, :] - lam * (acc_ref[nb] / m_ref[nb][1:2, :])
        ms = jnp.mean(out * out, axis=0, keepdims=True)
        out = out * lax.rsqrt(ms + EPS) * (sg_ref[...] * (1.0 - lambda_init))
        o_ref[tile * t:(tile + 1) * t, h * LANES:(h + 1) * LANES] = out.T.astype(BF16)

    chain = [functools.partial(build_group, c) for c in range(ATTN_CHAIN)]
    _attend(ATTN_CHAIN * i + 1, t, chain, (s0_ref, s1_ref), (mx0_ref, mx1_ref),
            last_step_addend, finish_head)


def _da_call(z, slopes, lam_p, subln_g, heads, lambda_init):
    b, s, _ = z.shape
    t, tk = ATTN_TILE, ATTN_KEYS
    assert s % tk == 0 and t % CHUNK == 0 and tk == ATTN_RING * t
    assert s <= CHUNK * 256, "key position // CHUNK must stay exact in bf16"
    hb = min(ATTN_BLOCKS, heads)
    groups = heads // hb
    assert heads % hb == 0
    kern = functools.partial(_da_kernel, t=t, hb=hb, lambda_init=lambda_init)
    w = hb * LANES
    n_streams = 2 * hb * ATTN_RING * ATTN_CHAIN
    rows = ATTN_CHAIN * tk
    assert s % rows == 0
    return pl.pallas_call(
        kern,
        grid_spec=pltpu.PrefetchScalarGridSpec(
            num_scalar_prefetch=1,
            grid=(b, groups, s // rows),
            in_specs=[
                pl.BlockSpec((None, rows, w), lambda bi, g, i, sl: (bi, i, g)),
                pl.BlockSpec((None, s, w), lambda bi, g, i, sl: (bi, 0, groups + g)),
                pl.BlockSpec((None, s, w), lambda bi, g, i, sl: (bi, 0, 2 * groups + g)),
                pl.BlockSpec((4, HEAD_DIM), lambda bi, g, i, sl: (0, 0)),
                pl.BlockSpec((LANES, 1), lambda bi, g, i, sl: (0, 0)),
            ],
            out_specs=pl.BlockSpec((None, rows, w), lambda bi, g, i, sl: (bi, i, g)),
            scratch_shapes=[
                pltpu.VMEM((hb, s, LANES), BF16), pltpu.VMEM((hb, s, LANES), BF16),
                pltpu.VMEM((hb, LANES, s), BF16),
                pltpu.VMEM((n_streams, 2, t), F32),
                pltpu.VMEM((n_streams, LANES, t), F32),
                pltpu.VMEM((2 * hb, ATTN_KEYS, t), F32), pltpu.VMEM((2 * hb, ATTN_KEYS, t), F32),
                pltpu.VMEM((2 * hb, 1, t), F32), pltpu.VMEM((2 * hb, 1, t), F32),
            ],
        ),
        out_shape=jax.ShapeDtypeStruct((b, s, heads * LANES), BF16),
        compiler_params=_params("arbitrary", "arbitrary", "arbitrary"),
        name="diff_attention",
    )(slopes, z, z, z, lam_p, subln_g.reshape(LANES, 1))


def _fox_gate_kernel(g_ref, b_ref, o_ref, *, chunk):
    s = g_ref.shape[0]
    r = lax.broadcasted_iota(jnp.int32, (chunk, chunk), 0)
    c = lax.broadcasted_iota(jnp.int32, (chunk, chunk), 1)
    tri = (c <= r).astype(BF16)
    carry = jnp.zeros((1, LANES), F32)
    for c0 in range(0, s, chunk):
        ls = _log_sigmoid(g_ref[c0:c0 + chunk, :] + b_ref[...])
        cs = _exact_lhs_dot(tri, ls) + carry
        carry = cs[chunk - 1:chunk, :]
        o_ref[c0:c0 + chunk, :] = cs * (-LOG2E)


def _fox_gate_call(gates, bias_row):
    b, s, _ = gates.shape
    chunk = min(GATE_CHUNK, s)
    assert s % chunk == 0
    kern = functools.partial(_fox_gate_kernel, chunk=chunk)
    return pl.pallas_call(
        kern,
        grid=(b,),
        in_specs=[
            pl.BlockSpec((None, s, LANES), lambda i: (i, 0, 0)),
            pl.BlockSpec((1, LANES), lambda i: (0, 0)),
        ],
        out_specs=pl.BlockSpec((None, s, LANES), lambda i: (i, 0, 0)),
        out_shape=jax.ShapeDtypeStruct((b, s, LANES), F32),
        compiler_params=_params("arbitrary"),
        name="fox_gate_cumsum",
    )(gates, bias_row)


def _fox_kernel(q_ref, k_ref, v_ref, f_ref, o_ref,
                ka_ref, kb_ref, vt_ref, m_ref, acc_ref, s0_ref, s1_ref, mx0_ref, mx1_ref, *, t, hb):
    dv = HEAD_DIM
    g = pl.program_id(1)
    i = pl.program_id(2)

    @pl.when(i == 0)
    def _():
        half = LANES // 2
        src = lax.broadcasted_iota(jnp.int32, (2 * LANES, LANES), 0)
        dst = lax.broadcasted_iota(jnp.int32, (2 * LANES, LANES), 1)
        piece = jnp.where(src < LANES, 0, jnp.where(src < LANES + half, 1, 2))
        head = src - jnp.where(src < LANES, 0, jnp.where(src < LANES + half, LANES, LANES + half))

        def pieces_of(r0, rows):
            x = f_ref[pl.ds(r0, rows), :]
            x1 = x.astype(BF16)
            r1 = x - x1.astype(F32)
            x2 = r1.astype(BF16).astype(F32)
            tail = jnp.where(_lane_ids() < half, x2, pltpu.roll(r1 - x2, half, axis=1))
            return jnp.concatenate([x1, tail.astype(BF16)], axis=-1)

        blocks = []
        for h in range(hb):
            pair = g * hb + h
            place_a = ((head == 2 * pair) & (dst == HEAD_DIM + piece)).astype(BF16)
            place_b = ((head == 2 * pair + 1) & (dst == piece)).astype(BF16)
            place = jnp.concatenate([place_a, place_b], axis=-1)

            def extras(pieces, place=place):
                both = _dot(pieces, place)
                return both[:, 0:LANES], both[:, LANES:2 * LANES]
            blocks.append((_lane_block(k_ref, h), _lane_block(v_ref, h), ka_ref.at[h], kb_ref.at[h],
                           [(vt_ref.at[2 * h + m], m * dv, dv) for m in range(2)], extras))
        _pack_keys(blocks, pieces_of)

    def last_step_addend(_, shared, rows, offset):
        key = lax.broadcasted_iota(jnp.int32, (rows, t), 0)
        qry = lax.broadcasted_iota(jnp.int32, (rows, t), 1) + offset
        return jnp.where(key <= qry, 0.0, NEG)

    def build_group(c):
        ones = [jnp.ones((1, LANES), F32)] * BIAS_PIECES
        tiles = []
        for u in range(ATTN_RING):
            tile = c * ATTN_RING + u
            streams = []
            for h in range(hb):
                qa, qb = _pack_queries(q_ref[tile * t:(tile + 1) * t, h * LANES:(h + 1) * LANES], ones)
                for m, (q, k_sc) in enumerate(((qa, ka_ref), (qb, kb_ref))):
                    vt = lambda k0, size, n=2 * h + m: vt_ref[n, :, pl.ds(k0, size)]
                    n_stream = (tile * hb + h) * 2 + m
                    streams.append((q, k_sc.at[h], vt, 0, (m_ref.at[n_stream], acc_ref.at[n_stream])))
            tiles.append(streams)
        return tiles

    def finish_pair(c, u, h):
        tile = c * ATTN_RING + u
        na, nb = (tile * hb + h) * 2, (tile * hb + h) * 2 + 1
        out = jnp.concatenate([acc_ref[na] / m_ref[na][1:2, :], acc_ref[nb] / m_ref[nb][1:2, :]], axis=0)
        o_ref[tile * t:(tile + 1) * t, h * LANES:(h + 1) * LANES] = out.T.astype(BF16)

    chain = [functools.partial(build_group, c) for c in range(ATTN_CHAIN)]
    _attend(ATTN_CHAIN * i + 1, t, chain, (s0_ref, s1_ref), (mx0_ref, mx1_ref),
            last_step_addend, finish_pair)


def _fox_call(z, fneg, heads):
    b, s, _ = z.shape
    pairs = heads // 2
    t, tk = ATTN_TILE, ATTN_KEYS
    assert s % tk == 0 and tk == ATTN_RING * t
    assert heads <= LANES // 2, "the forget-gate pieces of all heads share 128 + 64 + 64 contraction rows"
    hb = min(ATTN_BLOCKS, pairs)
    groups = pairs // hb
    assert pairs % hb == 0
    kern = functools.partial(_fox_kernel, t=t, hb=hb)
    w = hb * LANES
    n_streams = 2 * hb * ATTN_RING * ATTN_CHAIN
    rows = ATTN_CHAIN * tk
    assert s % rows == 0
    return pl.pallas_call(
        kern,
        grid=(b, groups, s // rows),
        in_specs=[
            pl.BlockSpec((None, rows, w), lambda bi, g, i: (bi, i, g)),
            pl.BlockSpec((None, s, w), lambda bi, g, i: (bi, 0, groups + g)),
            pl.BlockSpec((None, s, w), lambda bi, g, i: (bi, 0, 2 * groups + g)),
            pl.BlockSpec((None, s, LANES), lambda bi, g, i: (bi, 0, 0)),
        ],
        out_specs=pl.BlockSpec((None, rows, w), lambda bi, g, i: (bi, i, g)),
        out_shape=jax.ShapeDtypeStruct((b, s, pairs * LANES), BF16),
        scratch_shapes=[
            pltpu.VMEM((hb, s, LANES), BF16), pltpu.VMEM((hb, s, LANES), BF16),
            pltpu.VMEM((2 * hb, HEAD_DIM, s), BF16),
            pltpu.VMEM((n_streams, 2, t), F32),
            pltpu.VMEM((n_streams, HEAD_DIM, t), F32),
            pltpu.VMEM((2 * hb, ATTN_KEYS, t), F32), pltpu.VMEM((2 * hb, ATTN_KEYS, t), F32),
            pltpu.VMEM((2 * hb, 1, t), F32), pltpu.VMEM((2 * hb, 1, t), F32),
        ],
        compiler_params=_params("arbitrary", "arbitrary", "arbitrary"),
        name="forgetting_attention",
    )(z, z, z, fneg)


def _mlstm_kernel(q_ref, k_ref, v_ref, o_ref, g_ref, gb_ref, cw_ref, cb_ref, y_ref,
                  halo_ref, ct_ref, m_ref, *, heads, taps, lc):
    @pl.when(pl.program_id(1) == 0)
    def _():
        halo_ref[...] = jnp.zeros(halo_ref.shape, F32)
        ct_ref[...] = jnp.zeros(ct_ref.shape, F32)
        m_ref[...] = jnp.zeros(m_ref.shape, F32)

    rows = lambda ref, sub: ref.at[pl.ds(sub * lc, lc), :]
    edge = halo_ref[...]
    for sub in range(q_ref.shape[0] // lc):
        gates = _mlstm_gates(rows(g_ref, sub), gb_ref, heads)
        prep, edge = _mlstm_prep(rows(q_ref, sub), rows(k_ref, sub), gates, cw_ref, cb_ref,
                                 edge, taps)
        for hd in range(heads):
            _mlstm_head(prep, rows(v_ref, sub), rows(o_ref, sub), rows(y_ref, sub),
                        ct_ref, m_ref, hd, heads)
    halo_ref[...] = edge


def _mlstm_gates(g_ref, gb_ref, heads):
    lc = g_ref.shape[0]
    r = lax.broadcasted_iota(jnp.int32, (lc, lc), 0)
    c = lax.broadcasted_iota(jnp.int32, (lc, lc), 1)
    causal = c <= r

    gcol = g_ref[...] + gb_ref[...]
    grow = gcol.T[:2 * heads, :]
    a_cols = _exact_lhs_dot(causal.astype(BF16), _log_sigmoid(gcol))
    a_rows = _exact_rhs_dot(_log_sigmoid(grow), (r <= c).astype(BF16))

    b_half = grow[0:heads, :] - a_rows[heads:2 * heads, :]
    b_rows = jnp.concatenate([b_half, b_half], axis=0)
    return dict(gcol=gcol, a_cols=a_cols, b_rows=b_rows, causal=causal,
                prefix_cols=_prefix_max_lanes(b_rows).T)


def _mlstm_prep(q_ref, k_ref, gates, cw_ref, cb_ref, halo, taps):
    lc = q_ref.shape[0]
    pad = halo.shape[0]
    r = lax.broadcasted_iota(jnp.int32, (lc, lc), 0)
    c = lax.broadcasted_iota(jnp.int32, (lc, lc), 1)

    xb = jnp.concatenate([q_ref[...], k_ref[...]], axis=-1)
    xf = xb.astype(F32)
    conv = cb_ref[...] + xf * cw_ref[taps - 1:taps, :]
    for sft in range(1, taps):
        shifted = _dot((r - c == sft).astype(BF16), xb)
        conv = conv + shifted * cw_ref[taps - 1 - sft:taps - sft, :]
    edge = jnp.concatenate([halo, xf[0:pad, :]], axis=0)
    first = cb_ref[...]
    for j in range(taps):
        off = pad - (taps - 1) + j
        first = first + edge[off:off + pad, :] * cw_ref[j:j + 1, :]
    conv = jnp.concatenate([first, conv[pad:, :]], axis=0)
    return dict(gates, qk=_silu(conv)), xf[lc - pad:, :]


def _mlstm_head(prep, v_ref, o_ref, y_ref, ct_ref, m_ref, hd, heads):
    lc, width = v_ref.shape
    d = width // heads
    qk, gcol, a_cols, causal = prep["qk"], prep["gcol"], prep["a_cols"], prep["causal"]
    q = qk[:, hd * d:(hd + 1) * d]
    k = qk[:, width + hd * d:width + (hd + 1) * d] * (d ** -0.5)
    v = v_ref[:, hd * d:(hd + 1) * d]
    qb = q.astype(BF16)
    kb = k.astype(BF16)
    ig_c = gcol[:, hd:hd + 1]
    a_c = a_cols[:, heads + hd:heads + hd + 1]
    a_last = a_c[lc - 1:lc, :]
    ct = ct_ref[hd]
    m_prev = m_ref[hd:hd + 1, 0:1]
    x_c = jnp.maximum(prep["prefix_cols"][:, hd:hd + 1], m_prev)

    g_c = a_last - a_c + ig_c
    m_loc = jnp.max(g_c, axis=0, keepdims=True)
    w_c = jnp.exp(g_c - m_loc)
    m_new = jnp.maximum(a_last + m_prev, m_loc)
    decay = jnp.exp(a_last + m_prev - m_new)
    s_loc = jnp.exp(m_loc - m_new)
    wv = jnp.concatenate([w_c * v.astype(F32), jnp.broadcast_to(w_c, (lc, d))], axis=-1)
    increment = _dot_tn(kb, wv.astype(BF16))
    raw = _dot_nt(qb, kb)
    from_state = _dot(qb, ct.astype(BF16))

    decay_mat = jnp.exp2(jnp.where(causal, prep["b_rows"][hd:hd + 1, :] * LOG2E - x_c * LOG2E, NEG))
    sqk = (raw * decay_mat).astype(BF16)
    intra = _dot(sqk, jnp.concatenate([v, jnp.ones((lc, d), BF16)], axis=-1))
    inter = jnp.exp(m_prev - x_c) * from_state
    num = inter[:, 0:d] + intra[:, 0:d]
    den = inter[:, d:d + 1] + intra[:, d:d + 1]
    hval = num / jnp.maximum(jnp.abs(den), jnp.exp(-(a_c + x_c)))
    og = o_ref[:, hd * d:(hd + 1) * d].astype(F32)
    y_ref[:, hd * d:(hd + 1) * d] = (_sigmoid(og) * hval).astype(BF16)

    ct_ref[hd] = decay * ct + s_loc * increment
    m_ref[hd:hd + 1, :] = jnp.broadcast_to(m_new, (1, LANES))


def _mlstm_call(z, gates, gate_bias, conv_w, conv_b, heads, col0):
    b, s, _ = z.shape
    width = conv_w.shape[1] // 2
    lc = min(ML_CHUNK, s)
    rows = min(ML_CHUNKS_PER_STEP * lc, s)
    assert s % rows == 0 and rows % lc == 0 and col0 % width == 0 and width // heads == LANES
    assert 2 * heads == 8, "gate rows are handled as one 8-sublane tile"
    cb = col0 // width
    taps = conv_w.shape[0]
    kern = functools.partial(_mlstm_kernel, heads=heads, taps=taps, lc=lc)
    zspec = lambda off: pl.BlockSpec((None, rows, width), lambda bi, ci: (bi, ci, cb + off))
    return pl.pallas_call(
        kern,
        grid=(b, s // rows),
        in_specs=[
            zspec(0), zspec(1), zspec(2), zspec(3),
            pl.BlockSpec((None, rows, LANES), lambda bi, ci: (bi, ci, 0)),
            pl.BlockSpec((1, LANES), lambda bi, ci: (0, 0)),
            pl.BlockSpec((taps, 2 * width), lambda bi, ci: (0, 0)),
            pl.BlockSpec((1, 2 * width), lambda bi, ci: (0, 0)),
        ],
        out_specs=pl.BlockSpec((None, rows, width), lambda bi, ci: (bi, ci, 0)),
        out_shape=jax.ShapeDtypeStruct((b, s, width), BF16),
        scratch_shapes=[
            pltpu.VMEM((8, 2 * width), F32),
            pltpu.VMEM((heads, LANES, 2 * LANES), F32),
            pltpu.VMEM((2 * heads, LANES), F32),
        ],
        compiler_params=_params("arbitrary", "arbitrary"),
        name="mlstm",
    )(z, z, z, z, gates, gate_bias, conv_w, conv_b)


def _mix_ffn_kernel(*refs, n_in, chunk):
    y_refs = refs[:n_in]
    wo_refs = refs[n_in:2 * n_in]
    (x_ref, gm_ref, sh_ref, sc_ref, g_ref, gate_ref, w1_ref, w3_ref, w2_ref, o_ref) = refs[2 * n_in:]
    hidden = w1_ref.shape[1]
    sub = min(FFN_ROWS, x_ref.shape[0])

    def mixed(r0):
        mix = _dot(y_refs[0][r0:r0 + sub, :], wo_refs[0][...])
        for y_ref, w_ref in zip(y_refs[1:], wo_refs[1:]):
            mix = mix + _dot(y_ref[r0:r0 + sub, :], w_ref[...])
        x = x_ref[r0:r0 + sub, :] + gm_ref[...] * mix
        return x, _norm_modulate(x, g_ref[...], sh_ref[...], sc_ref[...]).astype(BF16)

    def ffn(r0, x, h):
        acc = None
        for c0 in range(0, hidden, chunk):
            a = _dot(h, w1_ref[:, c0:c0 + chunk])
            bb = _dot(h, w3_ref[:, c0:c0 + chunk])
            u = (_silu(a) * bb).astype(BF16)
            part = _dot(u, w2_ref[c0:c0 + chunk, :])
            acc = part if acc is None else acc + part
        o_ref[r0:r0 + sub, :] = x + gate_ref[...] * acc

    starts = list(range(0, x_ref.shape[0], sub))
    ready = mixed(starts[0])
    for k, r0 in enumerate(starts):
        current = ready
        if k + 1 < len(starts):
            ready = mixed(starts[k + 1])
        ffn(r0, *current)


def _mix_ffn_call(ys, wo_all, wo_layer, x, gate_mix, shift, scale, g, gate, w1, w3, w2, layer, name):
    b, s, d = x.shape
    hidden = w1.shape[2]
    tm = min(ROW_TILE, s)
    assert s % tm == 0 and hidden % FFN_CHUNK == 0
    assert all(y.shape[2] == ys[0].shape[2] for y in ys)
    kern = functools.partial(_mix_ffn_kernel, n_in=len(ys), chunk=FFN_CHUNK)
    vec = pl.BlockSpec((None, 1, d), lambda i, j: (i, 0, 0))
    rows = lambda width: pl.BlockSpec((None, tm, width), lambda i, j: (i, j, 0))
    stacked = lambda w: pl.BlockSpec((None,) + w.shape[1:], lambda i, j: (layer, 0, 0))
    wo_specs = [pl.BlockSpec((None, y.shape[2], d), lambda i, j, n=n: (wo_layer, n, 0))
                for n, y in enumerate(ys)]
    return pl.pallas_call(
        kern,
        grid=(b, s // tm),
        in_specs=[rows(y.shape[2]) for y in ys] + wo_specs + [
            rows(d), vec, vec, vec,
            pl.BlockSpec((1, d), lambda i, j: (0, 0)),
            vec, stacked(w1), stacked(w3), stacked(w2),
        ],
        out_specs=rows(d),
        out_shape=jax.ShapeDtypeStruct((b, s, d), F32),
        compiler_params=_params("arbitrary", "arbitrary"),
        name=name,
    )(*ys, *([wo_all] * len(ys)), x, gate_mix, shift, scale, g, gate, w1, w3, w2)


def _pad_cols(w, n):
    return jnp.pad(w, ((0, 0), (0, n - w.shape[1])))


def _gain_row(q_g, k_g, width, total):
    reps = width // HEAD_DIM
    row = jnp.concatenate([jnp.tile(q_g.astype(F32) * LOG2E, reps),
                           jnp.tile(k_g.astype(F32) * HEAD_DIM ** 0.5, reps),
                           jnp.ones((total - 2 * width,), F32)])
    return row.reshape(1, total)


def kernel(x, c, ada_w, ada_b, norm_mix_g, norm_ffn_g, ab_w_in, ml_b_i, ml_b_f, ml_conv_w, ml_conv_b,
           da_q_g, da_k_g, da_lambda, da_subln_g, ab_w_out, fx_w_in, fx_b_f, fx_q_g, fx_k_g, fx_w_out,
           ffn_w1, ffn_w3, ffn_w2):
    depth = ada_w.shape[0]
    b, s, d = x.shape
    ml_width = ml_conv_w.shape[2] // 2
    ml_heads = ml_b_i.shape[1]
    da_width = (ab_w_in.shape[2] - 4 * ml_width - 2 * ml_heads) // 3
    da_heads = da_width // (2 * HEAD_DIM)
    fx_heads = fx_b_f.shape[1]
    fx_width = fx_heads * HEAD_DIM
    ab_main = 3 * da_width + 4 * ml_width
    fx_main = 3 * fx_width

    mods = _ada_call(c, ada_w, ada_b).reshape(depth, b, 6, 1, d)
    slopes = 2.0 ** (-8.0 * jnp.arange(1, da_heads + 1, dtype=F32) / da_heads)

    ab_w_in_b, fx_w_in_b = ab_w_in.astype(BF16), fx_w_in.astype(BF16)
    ab_w_out_b, fx_w_out_b = ab_w_out.astype(BF16), fx_w_out.astype(BF16)
    ffn_w1_b, ffn_w3_b, ffn_w2_b = ffn_w1.astype(BF16), ffn_w3.astype(BF16), ffn_w2.astype(BF16)

    for l in range(depth):
        sh_m, sc_m, g_m, sh_f, sc_f, g_f = (mods[l, :, t] for t in range(6))
        j = l // 2
        g_mix = norm_mix_g[l].reshape(1, d)
        if l % 2 == 0:
            z, gates = _inproj_call(
                x, sh_m, sc_m, g_mix, ab_w_in_b, j, ab_main,
                _pad_cols(ab_w_in[j][:, ab_main:], LANES).astype(BF16),
                _gain_row(da_q_g[j], da_k_g[j], da_width, ab_main),
                2 * da_width, "inproj_even")
            lambda_init = 0.8 - 0.6 * math.exp(-0.3 * l)
            y_da = _da_call(z, slopes, da_lambda[j], da_subln_g[j], da_heads, lambda_init)
            gate_bias = _pad_cols(jnp.concatenate([ml_b_i[j], ml_b_f[j]]).reshape(1, -1), LANES)
            y_ml = _mlstm_call(z, gates, gate_bias, ml_conv_w[j], ml_conv_b[j].reshape(1, -1),
                               ml_heads, 3 * da_width)
            ys, wo_all, name = [y_da, y_ml], ab_w_out_b, "mix_ffn_even"
        else:
            z, gates = _inproj_call(
                x, sh_m, sc_m, g_mix, fx_w_in_b, j, fx_main,
                _pad_cols(fx_w_in[j][:, fx_main:], LANES).astype(BF16),
                _gain_row(fx_q_g[j], fx_k_g[j], fx_width, fx_main),
                2 * fx_width, "inproj_odd")
            fneg = _fox_gate_call(gates, _pad_cols(fx_b_f[j].reshape(1, -1), LANES))
            ys, wo_all, name = [_fox_call(z, fneg, fx_heads)], fx_w_out_b, "mix_ffn_odd"
        x = _mix_ffn_call(ys, wo_all, j, x, g_m, sh_f, sc_f, norm_ffn_g[l].reshape(1, d), g_f,
                          ffn_w1_b, ffn_w3_b, ffn_w2_b, l, name)
    return x
```

```python
import functools
import math

import jax
import jax.numpy as jnp
from jax import lax
from jax.experimental import pallas as pl
from jax.experimental.pallas import tpu as pltpu

F32 = jnp.float32
BF16 = jnp.bfloat16

EPS = 1e-6
LANES = 128
HEAD_DIM = 64
CHUNK = 64
CHUNK_SHIFT = CHUNK.bit_length() - 1
assert CHUNK == 1 << CHUNK_SHIFT
NEG = -1e30
LOG2E = math.log2(math.e)
BIAS_PIECES = 3
ONES_ROWS = 16

VMEM_LIMIT = 56 * 1024 * 1024

ADA_COLS = 1536
ROW_TILE = 1024
FFN_ROWS = 512
INPROJ_ROWS = 1024
INPROJ_CHUNK = 512
ATTN_TILE = 256
ATTN_RING = 2
ATTN_KEYS = ATTN_RING * ATTN_TILE
ATTN_CHAIN = 2
assert ATTN_CHAIN % 2 == 0
ATTN_BLOCKS = 4
PACK_ROWS = 512
ML_CHUNK = 256
ML_CHUNKS_PER_STEP = 4
GATE_CHUNK = 256
FFN_CHUNK = 256


def _params(*sem):
    return pltpu.CompilerParams(dimension_semantics=sem, vmem_limit_bytes=VMEM_LIMIT)


def _log_sigmoid(x):
    return jnp.minimum(x, 0.0) - jnp.log1p(jnp.exp(-jnp.abs(x)))


def _sigmoid(x):
    return 0.5 * jnp.tanh(0.5 * x) + 0.5


def _silu(x):
    h = 0.5 * x
    return h * (jnp.tanh(h) + 1.0)


def _prefix_max_lanes(x):
    axis = x.ndim - 1
    lane = lax.broadcasted_iota(jnp.int32, x.shape, axis)
    shift = 1
    while shift < x.shape[axis]:
        x = jnp.maximum(x, jnp.where(lane >= shift, pltpu.roll(x, shift, axis=axis), NEG))
        shift *= 2
    return x


def _split3(x):
    x1 = x.astype(BF16)
    r1 = x - x1.astype(F32)
    x2 = r1.astype(BF16)
    x3 = (r1 - x2.astype(F32)).astype(BF16)
    return x1, x2, x3


def _dot(a, b):
    return jnp.dot(a, b, preferred_element_type=F32)


def _dot_nt(a, b):
    return lax.dot_general(a, b, (((1,), (1,)), ((), ())), preferred_element_type=F32)


def _dot_tn(a, b):
    return lax.dot_general(a, b, (((0,), (0,)), ((), ())), preferred_element_type=F32)


def _exact_lhs_dot(m_bf16, x):
    x1, x2, x3 = _split3(x)
    return _dot(m_bf16, x1) + _dot(m_bf16, x2) + _dot(m_bf16, x3)


def _exact_rhs_dot(x, m_bf16):
    x1, x2, x3 = _split3(x)
    return _dot(x1, m_bf16) + _dot(x2, m_bf16) + _dot(x3, m_bf16)


def _norm_modulate(x, g, sh, sc):
    ms = jnp.mean(x * x, axis=-1, keepdims=True)
    return x * lax.rsqrt(ms + EPS) * (g * (1.0 + sc)) + sh


def _ada_kernel(c_ref, w_ref, b_ref, o_ref):
    c = c_ref[...]
    ca = _silu(c)
    c1, c2, c3 = _split3(ca)
    w = w_ref[...]
    w1 = w.astype(BF16)
    w2 = (w - w1.astype(F32)).astype(BF16)
    acc = _dot(c1, w1) + _dot(c1, w2) + _dot(c2, w1) + _dot(c2, w2) + _dot(c3, w1)
    o_ref[...] = acc + b_ref[...]


def _ada_call(c, ada_w, ada_b):
    depth, d, n = ada_w.shape
    b = c.shape[0]
    tn = min(ADA_COLS, n)
    assert n % tn == 0
    return pl.pallas_call(
        _ada_kernel,
        grid=(depth, n // tn),
        in_specs=[
            pl.BlockSpec((b, d), lambda l, j: (0, 0)),
            pl.BlockSpec((None, d, tn), lambda l, j: (l, 0, j)),
            pl.BlockSpec((None, 1, tn), lambda l, j: (l, 0, j)),
        ],
        out_specs=pl.BlockSpec((None, b, tn), lambda l, j: (l, 0, j)),
        out_shape=jax.ShapeDtypeStruct((depth, b, n), F32),
        compiler_params=_params("arbitrary", "arbitrary"),
        name="ada_mod",
    )(c, ada_w, ada_b.reshape(depth, 1, n))


def _inproj_kernel(x_ref, sh_ref, sc_ref, g_ref, w_ref, wg_ref, gain_ref, z_ref, gate_ref,
                   *, n_norm, chunk):
    half = x_ref.shape[0] // 2
    h_rows, first_rows = [], []
    for r0 in (0, half):
        h_part = _norm_modulate(x_ref[r0:r0 + half, :], g_ref[...], sh_ref[...], sc_ref[...]).astype(BF16)
        first_rows.append(_dot(h_part, w_ref[:, 0:chunk]))
        h_rows.append(h_part)
    h = jnp.concatenate(h_rows, axis=0)
    n = w_ref.shape[1]
    lane = lax.broadcasted_iota(jnp.int32, (1, LANES), 1)
    lo = lane < HEAD_DIM
    for c0 in range(0, n, chunk):
        zc = jnp.concatenate(first_rows, axis=0) if c0 == 0 else _dot(h, w_ref[:, c0:c0 + chunk])
        if c0 < n_norm:
            parts = []
            for s0 in range(0, chunk, LANES):
                zs = zc[:, s0:s0 + LANES]
                sq = zs * zs
                s_lo = jnp.sum(jnp.where(lo, sq, 0.0), axis=-1, keepdims=True)
                s_hi = jnp.sum(jnp.where(lo, 0.0, sq), axis=-1, keepdims=True)
                r_lo = lax.rsqrt(s_lo + HEAD_DIM * EPS)
                r_hi = lax.rsqrt(s_hi + HEAD_DIM * EPS)
                parts.append(zs * jnp.where(lo, r_lo, r_hi))
            zc = jnp.concatenate(parts, axis=-1) * gain_ref[:, c0:c0 + chunk]
        z_ref[:, c0:c0 + chunk] = zc.astype(BF16)
    gate_ref[...] = _dot(h, wg_ref[...])


def _inproj_call(x, shift, scale, g, w_all, layer, n, wg, gain, n_norm, name):
    b, s, d = x.shape
    tm = min(INPROJ_ROWS, s)
    chunk = INPROJ_CHUNK
    assert s % tm == 0 and tm % 2 == 0 and n % chunk == 0 and n_norm % chunk == 0
    kern = functools.partial(_inproj_kernel, n_norm=n_norm, chunk=chunk)
    return pl.pallas_call(
        kern,
        grid=(b, s // tm),
        in_specs=[
            pl.BlockSpec((None, tm, d), lambda i, j: (i, j, 0)),
            pl.BlockSpec((None, 1, d), lambda i, j: (i, 0, 0)),
            pl.BlockSpec((None, 1, d), lambda i, j: (i, 0, 0)),
            pl.BlockSpec((1, d), lambda i, j: (0, 0)),
            pl.BlockSpec((None, d, n), lambda i, j: (layer, 0, 0)),
            pl.BlockSpec((d, LANES), lambda i, j: (0, 0)),
            pl.BlockSpec((1, n), lambda i, j: (0, 0)),
        ],
        out_specs=[
            pl.BlockSpec((None, tm, n), lambda i, j: (i, j, 0)),
            pl.BlockSpec((None, tm, LANES), lambda i, j: (i, j, 0)),
        ],
        out_shape=[
            jax.ShapeDtypeStruct((b, s, n), BF16),
            jax.ShapeDtypeStruct((b, s, LANES), F32),
        ],
        compiler_params=_params("arbitrary", "arbitrary"),
        name=name,
    )(x, shift, scale, g, w_all, wg, gain)


def _lane_ids():
    return lax.broadcasted_iota(jnp.int32, (1, LANES), 1)


def _place_pieces(pieces, first_lane):
    lane = _lane_ids()
    out = jnp.zeros(pieces[0].shape, F32)
    for j, p in enumerate(pieces):
        out = jnp.where(lane == first_lane + j, p.astype(F32), out)
    return out


def _pack_keys(blocks, shared_fn):
    s = blocks[0][0].shape[0]
    rows = min(PACK_ROWS, s)
    keep_lo = jnp.where(_lane_ids() < HEAD_DIM, 1.0, 0.0).astype(BF16)
    keep_hi = jnp.where(_lane_ids() < HEAD_DIM, 0.0, 1.0).astype(BF16)

    def body(c, carry):
        r0 = pl.multiple_of(c * rows, rows)
        shared = shared_fn(r0, rows)
        for k_ref, v_ref, ka_ref, kb_ref, vt_refs, extras_fn in blocks:
            k2 = k_ref[pl.ds(r0, rows), :]
            ea, eb = extras_fn(shared)
            ka_ref[pl.ds(r0, rows), :] = k2 * keep_lo + ea.astype(BF16)
            kb_ref[pl.ds(r0, rows), :] = k2 * keep_hi + eb.astype(BF16)
            vt = v_ref[pl.ds(r0, rows), :].T
            for ref, c0, width in vt_refs:
                ref[0:width, pl.ds(r0, rows)] = vt[c0:c0 + width, :]
                ref[width:width + ONES_ROWS, pl.ds(r0, rows)] = jnp.ones((ONES_ROWS, rows), BF16)
        return carry

    lax.fori_loop(0, s // rows, body, 0)


def _pack_queries(q2, factors):
    lane = _lane_ids()
    q2 = q2.astype(F32)
    lo = lane < HEAD_DIM
    fac_a = _place_pieces(factors, HEAD_DIM)
    fac_b = _place_pieces(factors, 0)
    return jnp.where(lo, q2, fac_a).astype(BF16), jnp.where(lo, fac_b, q2).astype(BF16)


def _init_stats(m_ref, acc_ref):
    m_ref[...] = jnp.full(m_ref.shape, NEG, F32)
    acc_ref[...] = jnp.zeros(acc_ref.shape, F32)


def _softmax_step(s, tile_max, vt, m_ref, acc_ref):
    m_old = m_ref[...]
    m_new = jnp.maximum(m_old, tile_max)
    alpha = jnp.exp2(m_old - m_new)
    p = jnp.exp2(s - m_new)
    acc_ref[...] = alpha * acc_ref[...] + _dot(vt, p.astype(BF16))
    m_ref[...] = m_new


def _attend(n0, t, chain, s_buf, mx_buf, make_addend, finish_pair):
    tk = ATTN_KEYS
    first = ATTN_RING - 1
    built = {}

    def group(c):
        if c not in built:
            built[c] = chain[c]()
            for streams in built[c]:
                for stream in streams:
                    _init_stats(*stream[4])
        return built[c]

    def qk_one(tiles, u, step, idx, rows=tk):
        k0 = pl.multiple_of(step * tk, tk)
        q, k_ref = tiles[u][idx][0], tiles[u][idx][1]
        s = _dot_nt(k_ref[pl.ds(k0, rows), :], q)
        s_buf[u][idx, 0:rows, :] = s
        if rows == tk:
            mx_buf[u][idx] = jnp.max(s, axis=0, keepdims=True)

    def stage(c, u, step, partner, last=False):
        tiles = group(c)
        k0 = pl.multiple_of(step * tk, tk)
        rows, offset = (u + 1) * t, u * t
        addends = {}
        for idx, (_, _, vt, add_key, stats) in enumerate(tiles[u]):
            if partner is not None:
                p_tiles, p_u, p_step, p_rows = partner
                qk_one(p_tiles, p_u, p_step, idx, p_rows)
            if not last:
                _softmax_step(s_buf[u][idx], mx_buf[u][idx], vt(k0, tk), *stats)
                continue
            if add_key not in addends:
                addends[add_key] = make_addend(add_key, addends.setdefault("shared", {}), rows, offset)
            s = s_buf[u][idx, 0:rows, :] + addends[add_key]
            _softmax_step(s, jnp.max(s, axis=0, keepdims=True), vt(k0, rows), *stats)
            if idx % 2 == 1:
                finish_pair(c, u, idx // 2)

    for idx in range(len(group(0)[first])):
        qk_one(group(0), first, 0, idx)

    for c in range(len(chain)):
        n = n0 + c
        tiles = group(c)

        def ring_pass(j, c=c, tiles=tiles):
            for u in range(first, -1, -1):
                stage(c, u, j, (tiles, u - 1, j, tk) if u > 0 else (tiles, first, j + 1, tk))

        odd = c % 2

        def body(p, carry, ring_pass=ring_pass, odd=odd):
            ring_pass(2 * p + odd)
            ring_pass(2 * p + 1 + odd)
            return carry

        if odd:
            ring_pass(0)
        lax.fori_loop(0, (n - 1) // 2, body, 0)
        for u in range(first, -1, -1):
            if u > 0:
                partner = (tiles, u - 1, n - 1, u * t)
            elif c + 1 < len(chain):
                partner = (group(c + 1), first, 0, tk)
            else:
                partner = None
            stage(c, u, n - 1, partner, last=True)


def _lane_block(ref, h):
    return ref.at[:, pl.ds(h * LANES, LANES)]


def _da_kernel(slope_ref, q_ref, k_ref, v_ref, lam_ref, sg_ref, o_ref,
               ka_ref, kb_ref, vt_ref, m_ref, acc_ref, s0_ref, s1_ref, mx0_ref, mx1_ref,
               *, t, hb, lambda_init):
    dv = LANES
    g = pl.program_id(1)
    i = pl.program_id(2)
    slopes2 = [slope_ref[g * hb + h] * LOG2E for h in range(hb)]

    @pl.when(i == 0)
    def _():
        def position_pieces(r0, rows):
            pos = lax.broadcasted_iota(jnp.int32, (rows, LANES), 0) + r0
            hi = jnp.right_shift(pos, CHUNK_SHIFT).astype(F32)
            lo = jnp.bitwise_and(pos, CHUNK - 1).astype(F32)
            lane = _lane_ids()

            def place(first):
                in_hi = (lane >= first) & (lane < first + BIAS_PIECES)
                in_lo = (lane >= first + BIAS_PIECES) & (lane < first + 2 * BIAS_PIECES)
                return jnp.where(in_hi, hi, jnp.where(in_lo, lo, 0.0)).astype(BF16)
            return place(HEAD_DIM), place(0)
        _pack_keys([(_lane_block(k_ref, h), _lane_block(v_ref, h), ka_ref.at[h], kb_ref.at[h],
                     [(vt_ref.at[h], 0, dv)], lambda shared: shared) for h in range(hb)],
                   position_pieces)

    def last_step_addend(h, shared, rows, offset):
        if "base" not in shared:
            key = lax.broadcasted_iota(jnp.int32, (rows, t), 0)
            qry = lax.broadcasted_iota(jnp.int32, (rows, t), 1) + offset
            allowed = jnp.right_shift(key, CHUNK_SHIFT) <= jnp.right_shift(qry, CHUNK_SHIFT)
            shared["base"] = jnp.where(allowed, jnp.maximum(key - qry, 0).astype(F32), -NEG)
        return (-2.0 * slopes2[h]) * shared["base"]

    def build_group(c):
        tiles = []
        for u in range(ATTN_RING):
            tile = c * ATTN_RING + u
            streams = []
            for h in range(hb):
                slope_pieces = [p.astype(F32) for p in _split3(jnp.full((1, LANES), slopes2[h], F32))]
                factors = [p * float(CHUNK) for p in slope_pieces] + slope_pieces
                qa, qb = _pack_queries(q_ref[tile * t:(tile + 1) * t, h * LANES:(h + 1) * LANES], factors)
                vt = lambda k0, size, h=h: vt_ref[h, :, pl.ds(k0, size)]
                for m, (q, k_sc) in enumerate(((qa, ka_ref), (qb, kb_ref))):
                    n_stream = (tile * hb + h) * 2 + m
                    streams.append((q, k_sc.at[h], vt, h, (m_ref.at[n_stream], acc_ref.at[n_stream])))
            tiles.append(streams)
        return tiles

    def finish_head(c, u, h):
        tile = c * ATTN_RING + u
        lp = lam_ref[...]
        lam = (jnp.exp(jnp.sum(lp[0:1] * lp[1:2], axis=-1, keepdims=True))
               - jnp.exp(jnp.sum(lp[2:3] * lp[3:4], axis=-1, keepdims=True)) + lambda_init)
        pa = acc_ref[(tile * hb + h) * 2]
        pb = acc_ref[(tile * hb + h) * 2 + 1]
        out = pa[0:dv] / pa[dv:dv + 1] - lam * (pb[0:dv] / pb[dv:dv + 1])
        ms = jnp.mean(out * out, axis=0, keepdims=True)
        out = out * lax.rsqrt(ms + EPS) * (sg_ref[...] * (1.0 - lambda_init))
        o_ref[tile * t:(tile + 1) * t, h * LANES:(h + 1) * LANES] = out.T.astype(BF16)

    chain = [functools.partial(build_group, c) for c in range(ATTN_CHAIN)]
    _attend(ATTN_CHAIN * i + 1, t, chain, (s0_ref, s1_ref), (mx0_ref, mx1_ref),
            last_step_addend, finish_head)


def _da_call(z, slopes, lam_p, subln_g, heads, lambda_init):
    b, s, _ = z.shape
    t, tk = ATTN_TILE, ATTN_KEYS
    assert s % tk == 0 and t % CHUNK == 0 and tk == ATTN_RING * t
    assert s <= CHUNK * 256, "key position // CHUNK must stay exact in bf16"
    hb = min(ATTN_BLOCKS, heads)
    groups = heads // hb
    assert heads % hb == 0
    kern = functools.partial(_da_kernel, t=t, hb=hb, lambda_init=lambda_init)
    w = hb * LANES
    n_streams = 2 * hb * ATTN_RING * ATTN_CHAIN
    rows = ATTN_CHAIN * tk
    assert s % rows == 0
    return pl.pallas_call(
        kern,
        grid_spec=pltpu.PrefetchScalarGridSpec(
            num_scalar_prefetch=1,
            grid=(b, groups, s // rows),
            in_specs=[
                pl.BlockSpec((None, rows, w), lambda bi, g, i, sl: (bi, i, g)),
                pl.BlockSpec((None, s, w), lambda bi, g, i, sl: (bi, 0, groups + g)),
                pl.BlockSpec((None, s, w), lambda bi, g, i, sl: (bi, 0, 2 * groups + g)),
                pl.BlockSpec((4, HEAD_DIM), lambda bi, g, i, sl: (0, 0)),
                pl.BlockSpec((LANES, 1), lambda bi, g, i, sl: (0, 0)),
            ],
            out_specs=pl.BlockSpec((None, rows, w), lambda bi, g, i, sl: (bi, i, g)),
            scratch_shapes=[
                pltpu.VMEM((hb, s, LANES), BF16), pltpu.VMEM((hb, s, LANES), BF16),
                pltpu.VMEM((hb, LANES + ONES_ROWS, s), BF16),
                pltpu.VMEM((n_streams, 1, t), F32),
                pltpu.VMEM((n_streams, LANES + ONES_ROWS, t), F32),
                pltpu.VMEM((2 * hb, ATTN_KEYS, t), F32), pltpu.VMEM((2 * hb, ATTN_KEYS, t), F32),
                pltpu.VMEM((2 * hb, 1, t), F32), pltpu.VMEM((2 * hb, 1, t), F32),
            ],
        ),
        out_shape=jax.ShapeDtypeStruct((b, s, heads * LANES), BF16),
        compiler_params=_params("arbitrary", "arbitrary", "arbitrary"),
        name="diff_attention",
    )(slopes, z, z, z, lam_p, subln_g.reshape(LANES, 1))


def _fox_gate_kernel(g_ref, b_ref, o_ref, *, chunk):
    s = g_ref.shape[0]
    r = lax.broadcasted_iota(jnp.int32, (chunk, chunk), 0)
    c = lax.broadcasted_iota(jnp.int32, (chunk, chunk), 1)
    tri = (c <= r).astype(BF16)
    carry = jnp.zeros((1, LANES), F32)
    for c0 in range(0, s, chunk):
        ls = _log_sigmoid(g_ref[c0:c0 + chunk, :] + b_ref[...])
        cs = _exact_lhs_dot(tri, ls) + carry
        carry = cs[chunk - 1:chunk, :]
        o_ref[c0:c0 + chunk, :] = cs * (-LOG2E)


def _fox_gate_call(gates, bias_row):
    b, s, _ = gates.shape
    chunk = min(GATE_CHUNK, s)
    assert s % chunk == 0
    kern = functools.partial(_fox_gate_kernel, chunk=chunk)
    return pl.pallas_call(
        kern,
        grid=(b,),
        in_specs=[
            pl.BlockSpec((None, s, LANES), lambda i: (i, 0, 0)),
            pl.BlockSpec((1, LANES), lambda i: (0, 0)),
        ],
        out_specs=pl.BlockSpec((None, s, LANES), lambda i: (i, 0, 0)),
        out_shape=jax.ShapeDtypeStruct((b, s, LANES), F32),
        compiler_params=_params("arbitrary"),
        name="fox_gate_cumsum",
    )(gates, bias_row)


def _fox_kernel(q_ref, k_ref, v_ref, f_ref, o_ref,
                ka_ref, kb_ref, vt_ref, m_ref, acc_ref, s0_ref, s1_ref, mx0_ref, mx1_ref, *, t, hb):
    dv = HEAD_DIM
    g = pl.program_id(1)
    i = pl.program_id(2)

    @pl.when(i == 0)
    def _():
        half = LANES // 2
        src = lax.broadcasted_iota(jnp.int32, (2 * LANES, LANES), 0)
        dst = lax.broadcasted_iota(jnp.int32, (2 * LANES, LANES), 1)
        piece = jnp.where(src < LANES, 0, jnp.where(src < LANES + half, 1, 2))
        head = src - jnp.where(src < LANES, 0, jnp.where(src < LANES + half, LANES, LANES + half))

        def pieces_of(r0, rows):
            x = f_ref[pl.ds(r0, rows), :]
            x1 = x.astype(BF16)
            r1 = x - x1.astype(F32)
            x2 = r1.astype(BF16).astype(F32)
            tail = jnp.where(_lane_ids() < half, x2, pltpu.roll(r1 - x2, half, axis=1))
            return jnp.concatenate([x1, tail.astype(BF16)], axis=-1)

        blocks = []
        for h in range(hb):
            pair = g * hb + h
            place_a = ((head == 2 * pair) & (dst == HEAD_DIM + piece)).astype(BF16)
            place_b = ((head == 2 * pair + 1) & (dst == piece)).astype(BF16)
            place = jnp.concatenate([place_a, place_b], axis=-1)

            def extras(pieces, place=place):
                both = _dot(pieces, place)
                return both[:, 0:LANES], both[:, LANES:2 * LANES]
            blocks.append((_lane_block(k_ref, h), _lane_block(v_ref, h), ka_ref.at[h], kb_ref.at[h],
                           [(vt_ref.at[2 * h + m], m * dv, dv) for m in range(2)], extras))
        _pack_keys(blocks, pieces_of)

    def last_step_addend(_, shared, rows, offset):
        key = lax.broadcasted_iota(jnp.int32, (rows, t), 0)
        qry = lax.broadcasted_iota(jnp.int32, (rows, t), 1) + offset
        return jnp.where(key <= qry, 0.0, NEG)

    def build_group(c):
        ones = [jnp.ones((1, LANES), F32)] * BIAS_PIECES
        tiles = []
        for u in range(ATTN_RING):
            tile = c * ATTN_RING + u
            streams = []
            for h in range(hb):
                qa, qb = _pack_queries(q_ref[tile * t:(tile + 1) * t, h * LANES:(h + 1) * LANES], ones)
                for m, (q, k_sc) in enumerate(((qa, ka_ref), (qb, kb_ref))):
                    vt = lambda k0, size, n=2 * h + m: vt_ref[n, :, pl.ds(k0, size)]
                    n_stream = (tile * hb + h) * 2 + m
                    streams.append((q, k_sc.at[h], vt, 0, (m_ref.at[n_stream], acc_ref.at[n_stream])))
            tiles.append(streams)
        return tiles

    def finish_pair(c, u, h):
        tile = c * ATTN_RING + u
        pa = acc_ref[(tile * hb + h) * 2]
        pb = acc_ref[(tile * hb + h) * 2 + 1]
        out = jnp.concatenate([pa[0:dv] / pa[dv:dv + 1], pb[0:dv] / pb[dv:dv + 1]], axis=0)
        o_ref[tile * t:(tile + 1) * t, h * LANES:(h + 1) * LANES] = out.T.astype(BF16)

    chain = [functools.partial(build_group, c) for c in range(ATTN_CHAIN)]
    _attend(ATTN_CHAIN * i + 1, t, chain, (s0_ref, s1_ref), (mx0_ref, mx1_ref),
            last_step_addend, finish_pair)


def _fox_call(z, fneg, heads):
    b, s, _ = z.shape
    pairs = heads // 2
    t, tk = ATTN_TILE, ATTN_KEYS
    assert s % tk == 0 and tk == ATTN_RING * t
    assert heads <= LANES // 2, "the forget-gate pieces of all heads share 128 + 64 + 64 contraction rows"
    hb = min(ATTN_BLOCKS, pairs)
    groups = pairs // hb
    assert pairs % hb == 0
    kern = functools.partial(_fox_kernel, t=t, hb=hb)
    w = hb * LANES
    n_streams = 2 * hb * ATTN_RING * ATTN_CHAIN
    rows = ATTN_CHAIN * tk
    assert s % rows == 0
    return pl.pallas_call(
        kern,
        grid=(b, groups, s // rows),
        in_specs=[
            pl.BlockSpec((None, rows, w), lambda bi, g, i: (bi, i, g)),
            pl.BlockSpec((None, s, w), lambda bi, g, i: (bi, 0, groups + g)),
            pl.BlockSpec((None, s, w), lambda bi, g, i: (bi, 0, 2 * groups + g)),
            pl.BlockSpec((None, s, LANES), lambda bi, g, i: (bi, 0, 0)),
        ],
        out_specs=pl.BlockSpec((None, rows, w), lambda bi, g, i: (bi, i, g)),
        out_shape=jax.ShapeDtypeStruct((b, s, pairs * LANES), BF16),
        scratch_shapes=[
            pltpu.VMEM((hb, s, LANES), BF16), pltpu.VMEM((hb, s, LANES), BF16),
            pltpu.VMEM((2 * hb, HEAD_DIM + ONES_ROWS, s), BF16),
            pltpu.VMEM((n_streams, 1, t), F32),
            pltpu.VMEM((n_streams, HEAD_DIM + ONES_ROWS, t), F32),
            pltpu.VMEM((2 * hb, ATTN_KEYS, t), F32), pltpu.VMEM((2 * hb, ATTN_KEYS, t), F32),
            pltpu.VMEM((2 * hb, 1, t), F32), pltpu.VMEM((2 * hb, 1, t), F32),
        ],
        compiler_params=_params("arbitrary", "arbitrary", "arbitrary"),
        name="forgetting_attention",
    )(z, z, z, fneg)


def _mlstm_kernel(q_ref, k_ref, v_ref, o_ref, g_ref, gb_ref, cw_ref, cb_ref, y_ref,
                  halo_ref, ct_ref, m_ref, *, heads, taps, lc):
    @pl.when(pl.program_id(1) == 0)
    def _():
        halo_ref[...] = jnp.zeros(halo_ref.shape, F32)
        ct_ref[...] = jnp.zeros(ct_ref.shape, F32)
        m_ref[...] = jnp.zeros(m_ref.shape, F32)

    rows = lambda ref, sub: ref.at[pl.ds(sub * lc, lc), :]
    edge = halo_ref[...]
    for sub in range(q_ref.shape[0] // lc):
        gates = _mlstm_gates(rows(g_ref, sub), gb_ref, heads)
        prep, edge = _mlstm_prep(rows(q_ref, sub), rows(k_ref, sub), gates, cw_ref, cb_ref,
                                 edge, taps)
        for hd in range(heads):
            _mlstm_head(prep, rows(v_ref, sub), rows(o_ref, sub), rows(y_ref, sub),
                        ct_ref, m_ref, hd, heads)
    halo_ref[...] = edge


def _mlstm_gates(g_ref, gb_ref, heads):
    lc = g_ref.shape[0]
    r = lax.broadcasted_iota(jnp.int32, (lc, lc), 0)
    c = lax.broadcasted_iota(jnp.int32, (lc, lc), 1)
    causal = c <= r

    gcol = g_ref[...] + gb_ref[...]
    grow = gcol.T[:2 * heads, :]
    a_cols = _exact_lhs_dot(causal.astype(BF16), _log_sigmoid(gcol))
    a_rows = _exact_rhs_dot(_log_sigmoid(grow), (r <= c).astype(BF16))

    b_half = grow[0:heads, :] - a_rows[heads:2 * heads, :]
    b_rows = jnp.concatenate([b_half, b_half], axis=0)
    return dict(gcol=gcol, a_cols=a_cols, b_rows=b_rows, causal=causal,
                prefix_cols=_prefix_max_lanes(b_rows).T)


def _mlstm_prep(q_ref, k_ref, gates, cw_ref, cb_ref, halo, taps):
    lc = q_ref.shape[0]
    pad = halo.shape[0]
    r = lax.broadcasted_iota(jnp.int32, (lc, lc), 0)
    c = lax.broadcasted_iota(jnp.int32, (lc, lc), 1)

    xb = jnp.concatenate([q_ref[...], k_ref[...]], axis=-1)
    xf = xb.astype(F32)
    conv = cb_ref[...] + xf * cw_ref[taps - 1:taps, :]
    for sft in range(1, taps):
        shifted = _dot((r - c == sft).astype(BF16), xb)
        conv = conv + shifted * cw_ref[taps - 1 - sft:taps - sft, :]
    edge = jnp.concatenate([halo, xf[0:pad, :]], axis=0)
    first = cb_ref[...]
    for j in range(taps):
        off = pad - (taps - 1) + j
        first = first + edge[off:off + pad, :] * cw_ref[j:j + 1, :]
    conv = jnp.concatenate([first, conv[pad:, :]], axis=0)
    return dict(gates, qk=_silu(conv)), xf[lc - pad:, :]


def _mlstm_head(prep, v_ref, o_ref, y_ref, ct_ref, m_ref, hd, heads):
    lc, width = v_ref.shape
    d = width // heads
    qk, gcol, a_cols, causal = prep["qk"], prep["gcol"], prep["a_cols"], prep["causal"]
    q = qk[:, hd * d:(hd + 1) * d]
    k = qk[:, width + hd * d:width + (hd + 1) * d] * (d ** -0.5)
    v = v_ref[:, hd * d:(hd + 1) * d]
    qb = q.astype(BF16)
    kb = k.astype(BF16)
    ig_c = gcol[:, hd:hd + 1]
    a_c = a_cols[:, heads + hd:heads + hd + 1]
    a_last = a_c[lc - 1:lc, :]
    ct = ct_ref[hd]
    m_prev = m_ref[hd:hd + 1, 0:1]
    x_c = jnp.maximum(prep["prefix_cols"][:, hd:hd + 1], m_prev)

    g_c = a_last - a_c + ig_c
    m_loc = jnp.max(g_c, axis=0, keepdims=True)
    w_c = jnp.exp(g_c - m_loc)
    m_new = jnp.maximum(a_last + m_prev, m_loc)
    decay = jnp.exp(a_last + m_prev - m_new)
    s_loc = jnp.exp(m_loc - m_new)
    wv = jnp.concatenate([w_c * v.astype(F32), jnp.broadcast_to(w_c, (lc, d))], axis=-1)
    increment = _dot_tn(kb, wv.astype(BF16))
    raw = _dot_nt(qb, kb)
    from_state = _dot(qb, ct.astype(BF16))

    decay_mat = jnp.exp2(jnp.where(causal, prep["b_rows"][hd:hd + 1, :] * LOG2E - x_c * LOG2E, NEG))
    sqk = (raw * decay_mat).astype(BF16)
    intra = _dot(sqk, jnp.concatenate([v, jnp.ones((lc, d), BF16)], axis=-1))
    inter = jnp.exp(m_prev - x_c) * from_state
    num = inter[:, 0:d] + intra[:, 0:d]
    den = inter[:, d:d + 1] + intra[:, d:d + 1]
    hval = num / jnp.maximum(jnp.abs(den), jnp.exp(-(a_c + x_c)))
    og = o_ref[:, hd * d:(hd + 1) * d].astype(F32)
    y_ref[:, hd * d:(hd + 1) * d] = (_sigmoid(og) * hval).astype(BF16)

    ct_ref[hd] = decay * ct + s_loc * increment
    m_ref[hd:hd + 1, :] = jnp.broadcast_to(m_new, (1, LANES))


def _mlstm_call(z, gates, gate_bias, conv_w, conv_b, heads, col0):
    b, s, _ = z.shape
    width = conv_w.shape[1] // 2
    lc = min(ML_CHUNK, s)
    rows = min(ML_CHUNKS_PER_STEP * lc, s)
    assert s % rows == 0 and rows % lc == 0 and col0 % width == 0 and width // heads == LANES
    assert 2 * heads == 8, "gate rows are handled as one 8-sublane tile"
    cb = col0 // width
    taps = conv_w.shape[0]
    kern = functools.partial(_mlstm_kernel, heads=heads, taps=taps, lc=lc)
    zspec = lambda off: pl.BlockSpec((None, rows, width), lambda bi, ci: (bi, ci, cb + off))
    return pl.pallas_call(
        kern,
        grid=(b, s // rows),
        in_specs=[
            zspec(0), zspec(1), zspec(2), zspec(3),
            pl.BlockSpec((None, rows, LANES), lambda bi, ci: (bi, ci, 0)),
            pl.BlockSpec((1, LANES), lambda bi, ci: (0, 0)),
            pl.BlockSpec((taps, 2 * width), lambda bi, ci: (0, 0)),
            pl.BlockSpec((1, 2 * width), lambda bi, ci: (0, 0)),
        ],
        out_specs=pl.BlockSpec((None, rows, width), lambda bi, ci: (bi, ci, 0)),
        out_shape=jax.ShapeDtypeStruct((b, s, width), BF16),
        scratch_shapes=[
            pltpu.VMEM((8, 2 * width), F32),
            pltpu.VMEM((heads, LANES, 2 * LANES), F32),
            pltpu.VMEM((2 * heads, LANES), F32),
        ],
        compiler_params=_params("arbitrary", "arbitrary"),
        name="mlstm",
    )(z, z, z, z, gates, gate_bias, conv_w, conv_b)


def _mix_ffn_kernel(*refs, n_in, chunk):
    y_refs = refs[:n_in]
    wo_refs = refs[n_in:2 * n_in]
    (x_ref, gm_ref, sh_ref, sc_ref, g_ref, gate_ref, w1_ref, w3_ref, w2_ref, o_ref) = refs[2 * n_in:]
    hidden = w1_ref.shape[1]
    sub = min(FFN_ROWS, x_ref.shape[0])

    def mixed(r0):
        mix = _dot(y_refs[0][r0:r0 + sub, :], wo_refs[0][...])
        for y_ref, w_ref in zip(y_refs[1:], wo_refs[1:]):
            mix = mix + _dot(y_ref[r0:r0 + sub, :], w_ref[...])
        x = x_ref[r0:r0 + sub, :] + gm_ref[...] * mix
        return x, _norm_modulate(x, g_ref[...], sh_ref[...], sc_ref[...]).astype(BF16)

    def ffn(r0, x, h):
        acc = None
        for c0 in range(0, hidden, chunk):
            a = _dot(h, w1_ref[:, c0:c0 + chunk])
            bb = _dot(h, w3_ref[:, c0:c0 + chunk])
            u = (_silu(a) * bb).astype(BF16)
            part = _dot(u, w2_ref[c0:c0 + chunk, :])
            acc = part if acc is None else acc + part
        o_ref[r0:r0 + sub, :] = x + gate_ref[...] * acc

    starts = list(range(0, x_ref.shape[0], sub))
    ready = mixed(starts[0])
    for k, r0 in enumerate(starts):
        current = ready
        if k + 1 < len(starts):
            ready = mixed(starts[k + 1])
        ffn(r0, *current)


def _mix_ffn_call(ys, wo_all, wo_layer, x, gate_mix, shift, scale, g, gate, w1, w3, w2, layer, name):
    b, s, d = x.shape
    hidden = w1.shape[2]
    tm = min(ROW_TILE, s)
    assert s % tm == 0 and hidden % FFN_CHUNK == 0
    assert all(y.shape[2] == ys[0].shape[2] for y in ys)
    kern = functools.partial(_mix_ffn_kernel, n_in=len(ys), chunk=FFN_CHUNK)
    vec = pl.BlockSpec((None, 1, d), lambda i, j: (i, 0, 0))
    rows = lambda width: pl.BlockSpec((None, tm, width), lambda i, j: (i, j, 0))
    stacked = lambda w: pl.BlockSpec((None,) + w.shape[1:], lambda i, j: (layer, 0, 0))
    wo_specs = [pl.BlockSpec((None, y.shape[2], d), lambda i, j, n=n: (wo_layer, n, 0))
                for n, y in enumerate(ys)]
    return pl.pallas_call(
        kern,
        grid=(b, s // tm),
        in_specs=[rows(y.shape[2]) for y in ys] + wo_specs + [
            rows(d), vec, vec, vec,
            pl.BlockSpec((1, d), lambda i, j: (0, 0)),
            vec, stacked(w1), stacked(w3), stacked(w2),
        ],
        out_specs=rows(d),
        out_shape=jax.ShapeDtypeStruct((b, s, d), F32),
        compiler_params=_params("arbitrary", "arbitrary"),
        name=name,
    )(*ys, *([wo_all] * len(ys)), x, gate_mix, shift, scale, g, gate, w1, w3, w2)


def _pad_cols(w, n):
    return jnp.pad(w, ((0, 0), (0, n - w.shape[1])))


def _gain_row(q_g, k_g, width, total):
    reps = width // HEAD_DIM
    row = jnp.concatenate([jnp.tile(q_g.astype(F32) * LOG2E, reps),
                           jnp.tile(k_g.astype(F32) * HEAD_DIM ** 0.5, reps),
                           jnp.ones((total - 2 * width,), F32)])
    return row.reshape(1, total)


def kernel(x, c, ada_w, ada_b, norm_mix_g, norm_ffn_g, ab_w_in, ml_b_i, ml_b_f, ml_conv_w, ml_conv_b,
           da_q_g, da_k_g, da_lambda, da_subln_g, ab_w_out, fx_w_in, fx_b_f, fx_q_g, fx_k_g, fx_w_out,
           ffn_w1, ffn_w3, ffn_w2):
    depth = ada_w.shape[0]
    b, s, d = x.shape
    ml_width = ml_conv_w.shape[2] // 2
    ml_heads = ml_b_i.shape[1]
    da_width = (ab_w_in.shape[2] - 4 * ml_width - 2 * ml_heads) // 3
    da_heads = da_width // (2 * HEAD_DIM)
    fx_heads = fx_b_f.shape[1]
    fx_width = fx_heads * HEAD_DIM
    ab_main = 3 * da_width + 4 * ml_width
    fx_main = 3 * fx_width

    mods = _ada_call(c, ada_w, ada_b).reshape(depth, b, 6, 1, d)
    slopes = 2.0 ** (-8.0 * jnp.arange(1, da_heads + 1, dtype=F32) / da_heads)

    ab_w_in_b, fx_w_in_b = ab_w_in.astype(BF16), fx_w_in.astype(BF16)
    ab_w_out_b, fx_w_out_b = ab_w_out.astype(BF16), fx_w_out.astype(BF16)
    ffn_w1_b, ffn_w3_b, ffn_w2_b = ffn_w1.astype(BF16), ffn_w3.astype(BF16), ffn_w2.astype(BF16)

    for l in range(depth):
        sh_m, sc_m, g_m, sh_f, sc_f, g_f = (mods[l, :, t] for t in range(6))
        j = l // 2
        g_mix = norm_mix_g[l].reshape(1, d)
        if l % 2 == 0:
            z, gates = _inproj_call(
                x, sh_m, sc_m, g_mix, ab_w_in_b, j, ab_main,
                _pad_cols(ab_w_in[j][:, ab_main:], LANES).astype(BF16),
                _gain_row(da_q_g[j], da_k_g[j], da_width, ab_main),
                2 * da_width, "inproj_even")
            lambda_init = 0.8 - 0.6 * math.exp(-0.3 * l)
            y_da = _da_call(z, slopes, da_lambda[j], da_subln_g[j], da_heads, lambda_init)
            gate_bias = _pad_cols(jnp.concatenate([ml_b_i[j], ml_b_f[j]]).reshape(1, -1), LANES)
            y_ml = _mlstm_call(z, gates, gate_bias, ml_conv_w[j], ml_conv_b[j].reshape(1, -1),
                               ml_heads, 3 * da_width)
            ys, wo_all, name = [y_da, y_ml], ab_w_out_b, "mix_ffn_even"
        else:
            z, gates = _inproj_call(
                x, sh_m, sc_m, g_mix, fx_w_in_b, j, fx_main,
                _pad_cols(fx_w_in[j][:, fx_main:], LANES).astype(BF16),
                _gain_row(fx_q_g[j], fx_k_g[j], fx_width, fx_main),
                2 * fx_width, "inproj_odd")
            fneg = _fox_gate_call(gates, _pad_cols(fx_b_f[j].reshape(1, -1), LANES))
            ys, wo_all, name = [_fox_call(z, fneg, fx_heads)], fx_w_out_b, "mix_ffn_odd"
        x = _mix_ffn_call(ys, wo_all, j, x, g_m, sh_f, sc_f, norm_ffn_g[l].reshape(1, d), g_f,
                          ffn_w1_b, ffn_w3_b, ffn_w2_b, l, name)
    return x
```

```python
import functools
import math

import jax
import jax.numpy as jnp
from jax import lax
from jax.experimental import pallas as pl
from jax.experimental.pallas import tpu as pltpu

F32 = jnp.float32
BF16 = jnp.bfloat16

EPS = 1e-6
LANES = 128
HEAD_DIM = 64
CHUNK = 64
CHUNK_SHIFT = CHUNK.bit_length() - 1
assert CHUNK == 1 << CHUNK_SHIFT
NEG = -1e30
LOG2E = math.log2(math.e)
BIAS_PIECES = 3
ONES_ROWS = 16

VMEM_LIMIT = 56 * 1024 * 1024

ADA_COLS = 3072
ROW_TILE = 1024
FFN_ROWS = 512
INPROJ_ROWS = 1024
INPROJ_CHUNK = 512
ATTN_TILE = 256
ATTN_RING = 2
ATTN_KEYS = ATTN_RING * ATTN_TILE
ATTN_CHAIN = 2
assert ATTN_CHAIN % 2 == 0
ATTN_BLOCKS = 4
PACK_ROWS = 512
ML_CHUNK = 256
ML_CHUNKS_PER_STEP = 2
GATE_CHUNK = 256
FFN_CHUNK = 256


def _params(*sem):
    return pltpu.CompilerParams(dimension_semantics=sem, vmem_limit_bytes=VMEM_LIMIT)


def _log_sigmoid(x):
    return jnp.minimum(x, 0.0) - jnp.log1p(jnp.exp(-jnp.abs(x)))


def _sigmoid(x):
    return 0.5 * jnp.tanh(0.5 * x) + 0.5


def _silu(x):
    h = 0.5 * x
    return h * (jnp.tanh(h) + 1.0)


def _prefix_max_lanes(x):
    axis = x.ndim - 1
    lane = lax.broadcasted_iota(jnp.int32, x.shape, axis)
    shift = 1
    while shift < x.shape[axis]:
        x = jnp.maximum(x, jnp.where(lane >= shift, pltpu.roll(x, shift, axis=axis), NEG))
        shift *= 2
    return x


def _split3(x):
    x1 = x.astype(BF16)
    r1 = x - x1.astype(F32)
    x2 = r1.astype(BF16)
    x3 = (r1 - x2.astype(F32)).astype(BF16)
    return x1, x2, x3


def _dot(a, b):
    return jnp.dot(a, b, preferred_element_type=F32)


def _dot_nt(a, b):
    return lax.dot_general(a, b, (((1,), (1,)), ((), ())), preferred_element_type=F32)


def _dot_tn(a, b):
    return lax.dot_general(a, b, (((0,), (0,)), ((), ())), preferred_element_type=F32)


def _exact_lhs_dot(m_bf16, x):
    x1, x2, x3 = _split3(x)
    return _dot(m_bf16, x1) + _dot(m_bf16, x2) + _dot(m_bf16, x3)


def _exact_rhs_dot(x, m_bf16):
    x1, x2, x3 = _split3(x)
    return _dot(x1, m_bf16) + _dot(x2, m_bf16) + _dot(x3, m_bf16)


def _norm_modulate(x, g, sh, sc):
    ms = jnp.mean(x * x, axis=-1, keepdims=True)
    return x * lax.rsqrt(ms + EPS) * (g * (1.0 + sc)) + sh


def _ada_kernel(c_ref, w_ref, b_ref, o_ref):
    c = c_ref[...]
    ca = _silu(c)
    c1, c2, c3 = _split3(ca)
    w = w_ref[...]
    w1 = w.astype(BF16)
    w2 = (w - w1.astype(F32)).astype(BF16)
    acc = _dot(c1, w1) + _dot(c1, w2) + _dot(c2, w1) + _dot(c2, w2) + _dot(c3, w1)
    o_ref[...] = acc + b_ref[...]


def _ada_call(c, ada_w, ada_b):
    depth, d, n = ada_w.shape
    b = c.shape[0]
    tn = min(ADA_COLS, n)
    assert n % tn == 0
    return pl.pallas_call(
        _ada_kernel,
        grid=(depth, n // tn),
        in_specs=[
            pl.BlockSpec((b, d), lambda l, j: (0, 0)),
            pl.BlockSpec((None, d, tn), lambda l, j: (l, 0, j)),
            pl.BlockSpec((None, 1, tn), lambda l, j: (l, 0, j)),
        ],
        out_specs=pl.BlockSpec((None, b, tn), lambda l, j: (l, 0, j)),
        out_shape=jax.ShapeDtypeStruct((depth, b, n), F32),
        compiler_params=_params("arbitrary", "arbitrary"),
        name="ada_mod",
    )(c, ada_w, ada_b.reshape(depth, 1, n))


def _inproj_kernel(x_ref, sh_ref, sc_ref, g_ref, w_ref, wg_ref, gain_ref, z_ref, gate_ref,
                   *, n_norm, chunk):
    half = x_ref.shape[0] // 2
    h_rows, first_rows = [], []
    for r0 in (0, half):
        h_part = _norm_modulate(x_ref[r0:r0 + half, :], g_ref[...], sh_ref[...], sc_ref[...]).astype(BF16)
        first_rows.append(_dot(h_part, w_ref[:, 0:chunk]))
        h_rows.append(h_part)
    h = jnp.concatenate(h_rows, axis=0)
    n = w_ref.shape[1]
    lane = lax.broadcasted_iota(jnp.int32, (1, LANES), 1)
    lo = lane < HEAD_DIM
    for c0 in range(0, n, chunk):
        zc = jnp.concatenate(first_rows, axis=0) if c0 == 0 else _dot(h, w_ref[:, c0:c0 + chunk])
        if c0 < n_norm:
            parts = []
            for s0 in range(0, chunk, LANES):
                zs = zc[:, s0:s0 + LANES]
                sq = zs * zs
                s_lo = jnp.sum(jnp.where(lo, sq, 0.0), axis=-1, keepdims=True)
                s_hi = jnp.sum(jnp.where(lo, 0.0, sq), axis=-1, keepdims=True)
                r_lo = lax.rsqrt(s_lo + HEAD_DIM * EPS)
                r_hi = lax.rsqrt(s_hi + HEAD_DIM * EPS)
                parts.append(zs * jnp.where(lo, r_lo, r_hi))
            zc = jnp.concatenate(parts, axis=-1) * gain_ref[:, c0:c0 + chunk]
        z_ref[:, c0:c0 + chunk] = zc.astype(BF16)
    gate_ref[...] = _dot(h, wg_ref[...])


def _inproj_call(x, shift, scale, g, w_all, layer, n, wg, gain, n_norm, name):
    b, s, d = x.shape
    tm = min(INPROJ_ROWS, s)
    chunk = INPROJ_CHUNK
    assert s % tm == 0 and tm % 2 == 0 and n % chunk == 0 and n_norm % chunk == 0
    kern = functools.partial(_inproj_kernel, n_norm=n_norm, chunk=chunk)
    return pl.pallas_call(
        kern,
        grid=(b, s // tm),
        in_specs=[
            pl.BlockSpec((None, tm, d), lambda i, j: (i, j, 0)),
            pl.BlockSpec((None, 1, d), lambda i, j: (i, 0, 0)),
            pl.BlockSpec((None, 1, d), lambda i, j: (i, 0, 0)),
            pl.BlockSpec((1, d), lambda i, j: (0, 0)),
            pl.BlockSpec((None, d, n), lambda i, j: (layer, 0, 0)),
            pl.BlockSpec((d, LANES), lambda i, j: (0, 0)),
            pl.BlockSpec((1, n), lambda i, j: (0, 0)),
        ],
        out_specs=[
            pl.BlockSpec((None, tm, n), lambda i, j: (i, j, 0)),
            pl.BlockSpec((None, tm, LANES), lambda i, j: (i, j, 0)),
        ],
        out_shape=[
            jax.ShapeDtypeStruct((b, s, n), BF16),
            jax.ShapeDtypeStruct((b, s, LANES), F32),
        ],
        compiler_params=_params("arbitrary", "arbitrary"),
        name=name,
    )(x, shift, scale, g, w_all, wg, gain)


def _lane_ids():
    return lax.broadcasted_iota(jnp.int32, (1, LANES), 1)


def _place_pieces(pieces, first_lane):
    lane = _lane_ids()
    out = jnp.zeros(pieces[0].shape, F32)
    for j, p in enumerate(pieces):
        out = jnp.where(lane == first_lane + j, p.astype(F32), out)
    return out


def _pack_keys(blocks, shared_fn):
    s = blocks[0][0].shape[0]
    rows = min(PACK_ROWS, s)
    keep_lo = jnp.where(_lane_ids() < HEAD_DIM, 1.0, 0.0).astype(BF16)
    keep_hi = jnp.where(_lane_ids() < HEAD_DIM, 0.0, 1.0).astype(BF16)

    def body(c, carry):
        r0 = pl.multiple_of(c * rows, rows)
        shared = shared_fn(r0, rows)
        for k_ref, v_ref, ka_ref, kb_ref, vt_refs, extras_fn in blocks:
            k2 = k_ref[pl.ds(r0, rows), :]
            ea, eb = extras_fn(shared)
            ka_ref[pl.ds(r0, rows), :] = k2 * keep_lo + ea.astype(BF16)
            kb_ref[pl.ds(r0, rows), :] = k2 * keep_hi + eb.astype(BF16)
            vt = v_ref[pl.ds(r0, rows), :].T
            for ref, c0, width in vt_refs:
                ref[0:width, pl.ds(r0, rows)] = vt[c0:c0 + width, :]
                ref[width:width + ONES_ROWS, pl.ds(r0, rows)] = jnp.ones((ONES_ROWS, rows), BF16)
        return carry

    lax.fori_loop(0, s // rows, body, 0)


def _pack_queries(q2, factors):
    lane = _lane_ids()
    q2 = q2.astype(F32)
    lo = lane < HEAD_DIM
    fac_a = _place_pieces(factors, HEAD_DIM)
    fac_b = _place_pieces(factors, 0)
    return jnp.where(lo, q2, fac_a).astype(BF16), jnp.where(lo, fac_b, q2).astype(BF16)


def _init_stats(m_ref, acc_ref):
    m_ref[...] = jnp.full(m_ref.shape, NEG, F32)
    acc_ref[...] = jnp.zeros(acc_ref.shape, F32)


def _softmax_step(s, tile_max, vt, m_ref, acc_ref):
    m_old = m_ref[...]
    m_new = jnp.maximum(m_old, tile_max)
    alpha = jnp.exp2(m_old - m_new)
    p = jnp.exp2(s - m_new)
    acc_ref[...] = alpha * acc_ref[...] + _dot(vt, p.astype(BF16))
    m_ref[...] = m_new


def _attend(n0, t, chain, s_buf, mx_buf, make_addend, finish_pair):
    tk = ATTN_KEYS
    first = ATTN_RING - 1
    built = {}

    def group(c):
        if c not in built:
            built[c] = chain[c]()
            for streams in built[c]:
                for stream in streams:
                    _init_stats(*stream[4])
        return built[c]

    def qk_one(tiles, u, step, idx, rows=tk):
        k0 = pl.multiple_of(step * tk, tk)
        q, k_ref = tiles[u][idx][0], tiles[u][idx][1]
        s = _dot_nt(k_ref[pl.ds(k0, rows), :], q)
        s_buf[u][idx, 0:rows, :] = s
        if rows == tk:
            mx_buf[u][idx] = jnp.max(s, axis=0, keepdims=True)

    def stage(c, u, step, partner, last=False):
        tiles = group(c)
        k0 = pl.multiple_of(step * tk, tk)
        rows, offset = (u + 1) * t, u * t
        addends = {}
        for idx, (_, _, vt, add_key, stats) in enumerate(tiles[u]):
            if partner is not None:
                p_tiles, p_u, p_step, p_rows = partner
                qk_one(p_tiles, p_u, p_step, idx, p_rows)
            if not last:
                _softmax_step(s_buf[u][idx], mx_buf[u][idx], vt(k0, tk), *stats)
                continue
            if add_key not in addends:
                addends[add_key] = make_addend(add_key, addends.setdefault("shared", {}), rows, offset)
            s = s_buf[u][idx, 0:rows, :] + addends[add_key]
            _softmax_step(s, jnp.max(s, axis=0, keepdims=True), vt(k0, rows), *stats)
            if idx % 2 == 1:
                finish_pair(c, u, idx // 2)

    for idx in range(len(group(0)[first])):
        qk_one(group(0), first, 0, idx)

    for c in range(len(chain)):
        n = n0 + c
        tiles = group(c)

        def ring_pass(j, c=c, tiles=tiles):
            for u in range(first, -1, -1):
                stage(c, u, j, (tiles, u - 1, j, tk) if u > 0 else (tiles, first, j + 1, tk))

        odd = c % 2

        def body(p, carry, ring_pass=ring_pass, odd=odd):
            ring_pass(2 * p + odd)
            ring_pass(2 * p + 1 + odd)
            return carry

        if odd:
            ring_pass(0)
        lax.fori_loop(0, (n - 1) // 2, body, 0)
        for u in range(first, -1, -1):
            if u > 0:
                partner = (tiles, u - 1, n - 1, u * t)
            elif c + 1 < len(chain):
                partner = (group(c + 1), first, 0, tk)
            else:
                partner = None
            stage(c, u, n - 1, partner, last=True)


def _lane_block(ref, h):
    return ref.at[:, pl.ds(h * LANES, LANES)]


def _da_kernel(slope_ref, q_ref, k_ref, v_ref, lam_ref, sg_ref, o_ref,
               ka_ref, kb_ref, vt_ref, m_ref, acc_ref, s0_ref, s1_ref, mx0_ref, mx1_ref,
               *, t, hb, lambda_init):
    dv = LANES
    g = pl.program_id(1)
    i = pl.program_id(2)
    slopes2 = [slope_ref[g * hb + h] * LOG2E for h in range(hb)]

    @pl.when(i == 0)
    def _():
        def position_pieces(r0, rows):
            pos = lax.broadcasted_iota(jnp.int32, (rows, LANES), 0) + r0
            hi = jnp.right_shift(pos, CHUNK_SHIFT).astype(F32)
            lo = jnp.bitwise_and(pos, CHUNK - 1).astype(F32)
            lane = _lane_ids()

            def place(first):
                in_hi = (lane >= first) & (lane < first + BIAS_PIECES)
                in_lo = (lane >= first + BIAS_PIECES) & (lane < first + 2 * BIAS_PIECES)
                return jnp.where(in_hi, hi, jnp.where(in_lo, lo, 0.0)).astype(BF16)
            return place(HEAD_DIM), place(0)
        _pack_keys([(_lane_block(k_ref, h), _lane_block(v_ref, h), ka_ref.at[h], kb_ref.at[h],
                     [(vt_ref.at[h], 0, dv)], lambda shared: shared) for h in range(hb)],
                   position_pieces)

    def last_step_addend(h, shared, rows, offset):
        if "base" not in shared:
            key = lax.broadcasted_iota(jnp.int32, (rows, t), 0)
            qry = lax.broadcasted_iota(jnp.int32, (rows, t), 1) + offset
            allowed = jnp.right_shift(key, CHUNK_SHIFT) <= jnp.right_shift(qry, CHUNK_SHIFT)
            shared["base"] = jnp.where(allowed, jnp.maximum(key - qry, 0).astype(F32), -NEG)
        return (-2.0 * slopes2[h]) * shared["base"]

    def build_group(c):
        tiles = []
        for u in range(ATTN_RING):
            tile = c * ATTN_RING + u
            streams = []
            for h in range(hb):
                slope_pieces = [p.astype(F32) for p in _split3(jnp.full((1, LANES), slopes2[h], F32))]
                factors = [p * float(CHUNK) for p in slope_pieces] + slope_pieces
                qa, qb = _pack_queries(q_ref[tile * t:(tile + 1) * t, h * LANES:(h + 1) * LANES], factors)
                vt = lambda k0, size, h=h: vt_ref[h, :, pl.ds(k0, size)]
                for m, (q, k_sc) in enumerate(((qa, ka_ref), (qb, kb_ref))):
                    n_stream = (tile * hb + h) * 2 + m
                    streams.append((q, k_sc.at[h], vt, h, (m_ref.at[n_stream], acc_ref.at[n_stream])))
            tiles.append(streams)
        return tiles

    def finish_head(c, u, h):
        tile = c * ATTN_RING + u
        lp = lam_ref[...]
        lam = (jnp.exp(jnp.sum(lp[0:1] * lp[1:2], axis=-1, keepdims=True))
               - jnp.exp(jnp.sum(lp[2:3] * lp[3:4], axis=-1, keepdims=True)) + lambda_init)
        pa = acc_ref[(tile * hb + h) * 2]
        pb = acc_ref[(tile * hb + h) * 2 + 1]
        out = pa[0:dv] / pa[dv:dv + 1] - lam * (pb[0:dv] / pb[dv:dv + 1])
        ms = jnp.mean(out * out, axis=0, keepdims=True)
        out = out * lax.rsqrt(ms + EPS) * (sg_ref[...] * (1.0 - lambda_init))
        o_ref[tile * t:(tile + 1) * t, h * LANES:(h + 1) * LANES] = out.T.astype(BF16)

    chain = [functools.partial(build_group, c) for c in range(ATTN_CHAIN)]
    _attend(ATTN_CHAIN * i + 1, t, chain, (s0_ref, s1_ref), (mx0_ref, mx1_ref),
            last_step_addend, finish_head)


def _da_call(z, slopes, lam_p, subln_g, heads, lambda_init):
    b, s, _ = z.shape
    t, tk = ATTN_TILE, ATTN_KEYS
    assert s % tk == 0 and t % CHUNK == 0 and tk == ATTN_RING * t
    assert s <= CHUNK * 256, "key position // CHUNK must stay exact in bf16"
    hb = min(ATTN_BLOCKS, heads)
    groups = heads // hb
    assert heads % hb == 0
    kern = functools.partial(_da_kernel, t=t, hb=hb, lambda_init=lambda_init)
    w = hb * LANES
    n_streams = 2 * hb * ATTN_RING * ATTN_CHAIN
    rows = ATTN_CHAIN * tk
    assert s % rows == 0
    return pl.pallas_call(
        kern,
        grid_spec=pltpu.PrefetchScalarGridSpec(
            num_scalar_prefetch=1,
            grid=(b, groups, s // rows),
            in_specs=[
                pl.BlockSpec((None, rows, w), lambda bi, g, i, sl: (bi, i, g)),
                pl.BlockSpec((None, s, w), lambda bi, g, i, sl: (bi, 0, groups + g)),
                pl.BlockSpec((None, s, w), lambda bi, g, i, sl: (bi, 0, 2 * groups + g)),
                pl.BlockSpec((4, HEAD_DIM), lambda bi, g, i, sl: (0, 0)),
                pl.BlockSpec((LANES, 1), lambda bi, g, i, sl: (0, 0)),
            ],
            out_specs=pl.BlockSpec((None, rows, w), lambda bi, g, i, sl: (bi, i, g)),
            scratch_shapes=[
                pltpu.VMEM((hb, s, LANES), BF16), pltpu.VMEM((hb, s, LANES), BF16),
                pltpu.VMEM((hb, LANES + ONES_ROWS, s), BF16),
                pltpu.VMEM((n_streams, 1, t), F32),
                pltpu.VMEM((n_streams, LANES + ONES_ROWS, t), F32),
                pltpu.VMEM((2 * hb, ATTN_KEYS, t), F32), pltpu.VMEM((2 * hb, ATTN_KEYS, t), F32),
                pltpu.VMEM((2 * hb, 1, t), F32), pltpu.VMEM((2 * hb, 1, t), F32),
            ],
        ),
        out_shape=jax.ShapeDtypeStruct((b, s, heads * LANES), BF16),
        compiler_params=_params("arbitrary", "arbitrary", "arbitrary"),
        name="diff_attention",
    )(slopes, z, z, z, lam_p, subln_g.reshape(LANES, 1))


def _fox_gate_kernel(g_ref, b_ref, o_ref, *, chunk):
    s = g_ref.shape[0]
    r = lax.broadcasted_iota(jnp.int32, (chunk, chunk), 0)
    c = lax.broadcasted_iota(jnp.int32, (chunk, chunk), 1)
    tri = (c <= r).astype(BF16)
    carry = jnp.zeros((1, LANES), F32)
    for c0 in range(0, s, chunk):
        ls = _log_sigmoid(g_ref[c0:c0 + chunk, :] + b_ref[...])
        cs = _exact_lhs_dot(tri, ls) + carry
        carry = cs[chunk - 1:chunk, :]
        o_ref[c0:c0 + chunk, :] = cs * (-LOG2E)


def _fox_gate_call(gates, bias_row):
    b, s, _ = gates.shape
    chunk = min(GATE_CHUNK, s)
    assert s % chunk == 0
    kern = functools.partial(_fox_gate_kernel, chunk=chunk)
    return pl.pallas_call(
        kern,
        grid=(b,),
        in_specs=[
            pl.BlockSpec((None, s, LANES), lambda i: (i, 0, 0)),
            pl.BlockSpec((1, LANES), lambda i: (0, 0)),
        ],
        out_specs=pl.BlockSpec((None, s, LANES), lambda i: (i, 0, 0)),
        out_shape=jax.ShapeDtypeStruct((b, s, LANES), F32),
        compiler_params=_params("arbitrary"),
        name="fox_gate_cumsum",
    )(gates, bias_row)


def _fox_kernel(q_ref, k_ref, v_ref, f_ref, o_ref,
                ka_ref, kb_ref, vt_ref, m_ref, acc_ref, s0_ref, s1_ref, mx0_ref, mx1_ref, *, t, hb):
    dv = HEAD_DIM
    g = pl.program_id(1)
    i = pl.program_id(2)

    @pl.when(i == 0)
    def _():
        half = LANES // 2
        src = lax.broadcasted_iota(jnp.int32, (2 * LANES, LANES), 0)
        dst = lax.broadcasted_iota(jnp.int32, (2 * LANES, LANES), 1)
        piece = jnp.where(src < LANES, 0, jnp.where(src < LANES + half, 1, 2))
        head = src - jnp.where(src < LANES, 0, jnp.where(src < LANES + half, LANES, LANES + half))

        def pieces_of(r0, rows):
            x = f_ref[pl.ds(r0, rows), :]
            x1 = x.astype(BF16)
            r1 = x - x1.astype(F32)
            x2 = r1.astype(BF16).astype(F32)
            tail = jnp.where(_lane_ids() < half, x2, pltpu.roll(r1 - x2, half, axis=1))
            return jnp.concatenate([x1, tail.astype(BF16)], axis=-1)

        blocks = []
        for h in range(hb):
            pair = g * hb + h
            place_a = ((head == 2 * pair) & (dst == HEAD_DIM + piece)).astype(BF16)
            place_b = ((head == 2 * pair + 1) & (dst == piece)).astype(BF16)
            place = jnp.concatenate([place_a, place_b], axis=-1)

            def extras(pieces, place=place):
                both = _dot(pieces, place)
                return both[:, 0:LANES], both[:, LANES:2 * LANES]
            blocks.append((_lane_block(k_ref, h), _lane_block(v_ref, h), ka_ref.at[h], kb_ref.at[h],
                           [(vt_ref.at[2 * h + m], m * dv, dv) for m in range(2)], extras))
        _pack_keys(blocks, pieces_of)

    def last_step_addend(_, shared, rows, offset):
        key = lax.broadcasted_iota(jnp.int32, (rows, t), 0)
        qry = lax.broadcasted_iota(jnp.int32, (rows, t), 1) + offset
        return jnp.where(key <= qry, 0.0, NEG)

    def build_group(c):
        ones = [jnp.ones((1, LANES), F32)] * BIAS_PIECES
        tiles = []
        for u in range(ATTN_RING):
            tile = c * ATTN_RING + u
            streams = []
            for h in range(hb):
                qa, qb = _pack_queries(q_ref[tile * t:(tile + 1) * t, h * LANES:(h + 1) * LANES], ones)
                for m, (q, k_sc) in enumerate(((qa, ka_ref), (qb, kb_ref))):
                    vt = lambda k0, size, n=2 * h + m: vt_ref[n, :, pl.ds(k0, size)]
                    n_stream = (tile * hb + h) * 2 + m
                    streams.append((q, k_sc.at[h], vt, 0, (m_ref.at[n_stream], acc_ref.at[n_stream])))
            tiles.append(streams)
        return tiles

    def finish_pair(c, u, h):
        tile = c * ATTN_RING + u
        pa = acc_ref[(tile * hb + h) * 2]
        pb = acc_ref[(tile * hb + h) * 2 + 1]
        out = jnp.concatenate([pa[0:dv] / pa[dv:dv + 1], pb[0:dv] / pb[dv:dv + 1]], axis=0)
        o_ref[tile * t:(tile + 1) * t, h * LANES:(h + 1) * LANES] = out.T.astype(BF16)

    chain = [functools.partial(build_group, c) for c in range(ATTN_CHAIN)]
    _attend(ATTN_CHAIN * i + 1, t, chain, (s0_ref, s1_ref), (mx0_ref, mx1_ref),
            last_step_addend, finish_pair)


def _fox_call(z, fneg, heads):
    b, s, _ = z.shape
    pairs = heads // 2
    t, tk = ATTN_TILE, ATTN_KEYS
    assert s % tk == 0 and tk == ATTN_RING * t
    assert heads <= LANES // 2, "the forget-gate pieces of all heads share 128 + 64 + 64 contraction rows"
    hb = min(ATTN_BLOCKS, pairs)
    groups = pairs // hb
    assert pairs % hb == 0
    kern = functools.partial(_fox_kernel, t=t, hb=hb)
    w = hb * LANES
    n_streams = 2 * hb * ATTN_RING * ATTN_CHAIN
    rows = ATTN_CHAIN * tk
    assert s % rows == 0
    return pl.pallas_call(
        kern,
        grid=(b, groups, s // rows),
        in_specs=[
            pl.BlockSpec((None, rows, w), lambda bi, g, i: (bi, i, g)),
            pl.BlockSpec((None, s, w), lambda bi, g, i: (bi, 0, groups + g)),
            pl.BlockSpec((None, s, w), lambda bi, g, i: (bi, 0, 2 * groups + g)),
            pl.BlockSpec((None, s, LANES), lambda bi, g, i: (bi, 0, 0)),
        ],
        out_specs=pl.BlockSpec((None, rows, w), lambda bi, g, i: (bi, i, g)),
        out_shape=jax.ShapeDtypeStruct((b, s, pairs * LANES), BF16),
        scratch_shapes=[
            pltpu.VMEM((hb, s, LANES), BF16), pltpu.VMEM((hb, s, LANES), BF16),
            pltpu.VMEM((2 * hb, HEAD_DIM + ONES_ROWS, s), BF16),
            pltpu.VMEM((n_streams, 1, t), F32),
            pltpu.VMEM((n_streams, HEAD_DIM + ONES_ROWS, t), F32),
            pltpu.VMEM((2 * hb, ATTN_KEYS, t), F32), pltpu.VMEM((2 * hb, ATTN_KEYS, t), F32),
            pltpu.VMEM((2 * hb, 1, t), F32), pltpu.VMEM((2 * hb, 1, t), F32),
        ],
        compiler_params=_params("arbitrary", "arbitrary", "arbitrary"),
        name="forgetting_attention",
    )(z, z, z, fneg)


def _mlstm_kernel(q_ref, k_ref, v_ref, o_ref, g_ref, gb_ref, cw_ref, cb_ref, y_ref,
                  halo_ref, ct_ref, m_ref, *, heads, taps, lc):
    @pl.when(pl.program_id(1) == 0)
    def _():
        halo_ref[...] = jnp.zeros(halo_ref.shape, F32)
        ct_ref[...] = jnp.zeros(ct_ref.shape, F32)
        m_ref[...] = jnp.zeros(m_ref.shape, F32)

    rows = lambda ref, sub: ref.at[pl.ds(sub * lc, lc), :]
    edge = halo_ref[...]
    for sub in range(q_ref.shape[0] // lc):
        gates = _mlstm_gates(rows(g_ref, sub), gb_ref, heads)
        prep, edge = _mlstm_prep(rows(q_ref, sub), rows(k_ref, sub), gates, cw_ref, cb_ref,
                                 edge, taps)
        for hd in range(heads):
            _mlstm_head(prep, rows(v_ref, sub), rows(o_ref, sub), rows(y_ref, sub),
                        ct_ref, m_ref, hd, heads)
    halo_ref[...] = edge


def _mlstm_gates(g_ref, gb_ref, heads):
    lc = g_ref.shape[0]
    r = lax.broadcasted_iota(jnp.int32, (lc, lc), 0)
    c = lax.broadcasted_iota(jnp.int32, (lc, lc), 1)
    causal = c <= r

    gcol = g_ref[...] + gb_ref[...]
    grow = gcol.T[:2 * heads, :]
    a_cols = _exact_lhs_dot(causal.astype(BF16), _log_sigmoid(gcol))
    a_rows = _exact_rhs_dot(_log_sigmoid(grow), (r <= c).astype(BF16))

    b_half = grow[0:heads, :] - a_rows[heads:2 * heads, :]
    b_rows = jnp.concatenate([b_half, b_half], axis=0)
    return dict(gcol=gcol, a_cols=a_cols, b_rows=b_rows, causal=causal,
                prefix_cols=_prefix_max_lanes(b_rows).T)


def _mlstm_prep(q_ref, k_ref, gates, cw_ref, cb_ref, halo, taps):
    lc = q_ref.shape[0]
    pad = halo.shape[0]
    r = lax.broadcasted_iota(jnp.int32, (lc, lc), 0)
    c = lax.broadcasted_iota(jnp.int32, (lc, lc), 1)

    xb = jnp.concatenate([q_ref[...], k_ref[...]], axis=-1)
    xf = xb.astype(F32)
    conv = cb_ref[...] + xf * cw_ref[taps - 1:taps, :]
    for sft in range(1, taps):
        shifted = _dot((r - c == sft).astype(BF16), xb)
        conv = conv + shifted * cw_ref[taps - 1 - sft:taps - sft, :]
    edge = jnp.concatenate([halo, xf[0:pad, :]], axis=0)
    first = cb_ref[...]
    for j in range(taps):
        off = pad - (taps - 1) + j
        first = first + edge[off:off + pad, :] * cw_ref[j:j + 1, :]
    conv = jnp.concatenate([first, conv[pad:, :]], axis=0)
    return dict(gates, qk=_silu(conv)), xf[lc - pad:, :]


def _mlstm_head(prep, v_ref, o_ref, y_ref, ct_ref, m_ref, hd, heads):
    lc, width = v_ref.shape
    d = width // heads
    qk, gcol, a_cols, causal = prep["qk"], prep["gcol"], prep["a_cols"], prep["causal"]
    q = qk[:, hd * d:(hd + 1) * d]
    k = qk[:, width + hd * d:width + (hd + 1) * d] * (d ** -0.5)
    v = v_ref[:, hd * d:(hd + 1) * d]
    qb = q.astype(BF16)
    kb = k.astype(BF16)
    ig_c = gcol[:, hd:hd + 1]
    a_c = a_cols[:, heads + hd:heads + hd + 1]
    a_last = a_c[lc - 1:lc, :]
    ct = ct_ref[hd]
    m_prev = m_ref[hd:hd + 1, 0:1]
    x_c = jnp.maximum(prep["prefix_cols"][:, hd:hd + 1], m_prev)

    g_c = a_last - a_c + ig_c
    m_loc = jnp.max(g_c, axis=0, keepdims=True)
    w_c = jnp.exp(g_c - m_loc)
    m_new = jnp.maximum(a_last + m_prev, m_loc)
    decay = jnp.exp(a_last + m_prev - m_new)
    s_loc = jnp.exp(m_loc - m_new)
    wv = jnp.concatenate([w_c * v.astype(F32), jnp.broadcast_to(w_c, (lc, d))], axis=-1)
    increment = _dot_tn(kb, wv.astype(BF16))
    raw = _dot_nt(qb, kb)
    from_state = _dot(qb, ct.astype(BF16))

    decay_mat = jnp.exp2(jnp.where(causal, prep["b_rows"][hd:hd + 1, :] * LOG2E - x_c * LOG2E, NEG))
    sqk = (raw * decay_mat).astype(BF16)
    intra = _dot(sqk, jnp.concatenate([v, jnp.ones((lc, d), BF16)], axis=-1))
    inter = jnp.exp(m_prev - x_c) * from_state
    num = inter[:, 0:d] + intra[:, 0:d]
    den = inter[:, d:d + 1] + intra[:, d:d + 1]
    hval = num / jnp.maximum(jnp.abs(den), jnp.exp(-(a_c + x_c)))
    og = o_ref[:, hd * d:(hd + 1) * d].astype(F32)
    y_ref[:, hd * d:(hd + 1) * d] = (_sigmoid(og) * hval).astype(BF16)

    ct_ref[hd] = decay * ct + s_loc * increment
    m_ref[hd:hd + 1, :] = jnp.broadcast_to(m_new, (1, LANES))


def _mlstm_call(z, gates, gate_bias, conv_w, conv_b, heads, col0):
    b, s, _ = z.shape
    width = conv_w.shape[1] // 2
    lc = min(ML_CHUNK, s)
    rows = min(ML_CHUNKS_PER_STEP * lc, s)
    assert s % rows == 0 and rows % lc == 0 and col0 % width == 0 and width // heads == LANES
    assert 2 * heads == 8, "gate rows are handled as one 8-sublane tile"
    cb = col0 // width
    taps = conv_w.shape[0]
    kern = functools.partial(_mlstm_kernel, heads=heads, taps=taps, lc=lc)
    zspec = lambda off: pl.BlockSpec((None, rows, width), lambda bi, ci: (bi, ci, cb + off))
    return pl.pallas_call(
        kern,
        grid=(b, s // rows),
        in_specs=[
            zspec(0), zspec(1), zspec(2), zspec(3),
            pl.BlockSpec((None, rows, LANES), lambda bi, ci: (bi, ci, 0)),
            pl.BlockSpec((1, LANES), lambda bi, ci: (0, 0)),
            pl.BlockSpec((taps, 2 * width), lambda bi, ci: (0, 0)),
            pl.BlockSpec((1, 2 * width), lambda bi, ci: (0, 0)),
        ],
        out_specs=pl.BlockSpec((None, rows, width), lambda bi, ci: (bi, ci, 0)),
        out_shape=jax.ShapeDtypeStruct((b, s, width), BF16),
        scratch_shapes=[
            pltpu.VMEM((8, 2 * width), F32),
            pltpu.VMEM((heads, LANES, 2 * LANES), F32),
            pltpu.VMEM((2 * heads, LANES), F32),
        ],
        compiler_params=_params("arbitrary", "arbitrary"),
        name="mlstm",
    )(z, z, z, z, gates, gate_bias, conv_w, conv_b)


def _mix_ffn_kernel(*refs, n_in, chunk):
    y_refs = refs[:n_in]
    wo_refs = refs[n_in:2 * n_in]
    (x_ref, gm_ref, sh_ref, sc_ref, g_ref, gate_ref, w1_ref, w3_ref, w2_ref, o_ref) = refs[2 * n_in:]
    hidden = w1_ref.shape[1]
    sub = min(FFN_ROWS, x_ref.shape[0])

    def mixed(r0):
        mix = _dot(y_refs[0][r0:r0 + sub, :], wo_refs[0][...])
        for y_ref, w_ref in zip(y_refs[1:], wo_refs[1:]):
            mix = mix + _dot(y_ref[r0:r0 + sub, :], w_ref[...])
        x = x_ref[r0:r0 + sub, :] + gm_ref[...] * mix
        return x, _norm_modulate(x, g_ref[...], sh_ref[...], sc_ref[...]).astype(BF16)

    def ffn(r0, x, h):
        acc = None
        for c0 in range(0, hidden, chunk):
            a = _dot(h, w1_ref[:, c0:c0 + chunk])
            bb = _dot(h, w3_ref[:, c0:c0 + chunk])
            u = (_silu(a) * bb).astype(BF16)
            part = _dot(u, w2_ref[c0:c0 + chunk, :])
            acc = part if acc is None else acc + part
        o_ref[r0:r0 + sub, :] = x + gate_ref[...] * acc

    starts = list(range(0, x_ref.shape[0], sub))
    ready = mixed(starts[0])
    for k, r0 in enumerate(starts):
        current = ready
        if k + 1 < len(starts):
            ready = mixed(starts[k + 1])
        ffn(r0, *current)


def _mix_ffn_call(ys, wo_all, wo_layer, x, gate_mix, shift, scale, g, gate, w1, w3, w2, layer, name):
    b, s, d = x.shape
    hidden = w1.shape[2]
    tm = min(ROW_TILE, s)
    assert s % tm == 0 and hidden % FFN_CHUNK == 0
    assert all(y.shape[2] == ys[0].shape[2] for y in ys)
    kern = functools.partial(_mix_ffn_kernel, n_in=len(ys), chunk=FFN_CHUNK)
    vec = pl.BlockSpec((None, 1, d), lambda i, j: (i, 0, 0))
    rows = lambda width: pl.BlockSpec((None, tm, width), lambda i, j: (i, j, 0))
    stacked = lambda w: pl.BlockSpec((None,) + w.shape[1:], lambda i, j: (layer, 0, 0))
    wo_specs = [pl.BlockSpec((None, y.shape[2], d), lambda i, j, n=n: (wo_layer, n, 0))
                for n, y in enumerate(ys)]
    return pl.pallas_call(
        kern,
        grid=(b, s // tm),
        in_specs=[rows(y.shape[2]) for y in ys] + wo_specs + [
            rows(d), vec, vec, vec,
            pl.BlockSpec((1, d), lambda i, j: (0, 0)),
            vec, stacked(w1), stacked(w3), stacked(w2),
        ],
        out_specs=rows(d),
        out_shape=jax.ShapeDtypeStruct((b, s, d), F32),
        compiler_params=_params("arbitrary", "arbitrary"),
        name=name,
    )(*ys, *([wo_all] * len(ys)), x, gate_mix, shift, scale, g, gate, w1, w3, w2)


def _pad_cols(w, n):
    return jnp.pad(w, ((0, 0), (0, n - w.shape[1])))


def _gain_row(q_g, k_g, width, total):
    reps = width // HEAD_DIM
    row = jnp.concatenate([jnp.tile(q_g.astype(F32) * LOG2E, reps),
                           jnp.tile(k_g.astype(F32) * HEAD_DIM ** 0.5, reps),
                           jnp.ones((total - 2 * width,), F32)])
    return row.reshape(1, total)


def kernel(x, c, ada_w, ada_b, norm_mix_g, norm_ffn_g, ab_w_in, ml_b_i, ml_b_f, ml_conv_w, ml_conv_b,
           da_q_g, da_k_g, da_lambda, da_subln_g, ab_w_out, fx_w_in, fx_b_f, fx_q_g, fx_k_g, fx_w_out,
           ffn_w1, ffn_w3, ffn_w2):
    depth = ada_w.shape[0]
    b, s, d = x.shape
    ml_width = ml_conv_w.shape[2] // 2
    ml_heads = ml_b_i.shape[1]
    da_width = (ab_w_in.shape[2] - 4 * ml_width - 2 * ml_heads) // 3
    da_heads = da_width // (2 * HEAD_DIM)
    fx_heads = fx_b_f.shape[1]
    fx_width = fx_heads * HEAD_DIM
    ab_main = 3 * da_width + 4 * ml_width
    fx_main = 3 * fx_width

    mods = _ada_call(c, ada_w, ada_b).reshape(depth, b, 6, 1, d)
    slopes = 2.0 ** (-8.0 * jnp.arange(1, da_heads + 1, dtype=F32) / da_heads)

    ab_w_in_b, fx_w_in_b = ab_w_in.astype(BF16), fx_w_in.astype(BF16)
    ab_w_out_b, fx_w_out_b = ab_w_out.astype(BF16), fx_w_out.astype(BF16)
    ffn_w1_b, ffn_w3_b, ffn_w2_b = ffn_w1.astype(BF16), ffn_w3.astype(BF16), ffn_w2.astype(BF16)

    for l in range(depth):
        sh_m, sc_m, g_m, sh_f, sc_f, g_f = (mods[l, :, t] for t in range(6))
        j = l // 2
        g_mix = norm_mix_g[l].reshape(1, d)
        if l % 2 == 0:
            z, gates = _inproj_call(
                x, sh_m, sc_m, g_mix, ab_w_in_b, j, ab_main,
                _pad_cols(ab_w_in[j][:, ab_main:], LANES).astype(BF16),
                _gain_row(da_q_g[j], da_k_g[j], da_width, ab_main),
                2 * da_width, "inproj_even")
            lambda_init = 0.8 - 0.6 * math.exp(-0.3 * l)
            y_da = _da_call(z, slopes, da_lambda[j], da_subln_g[j], da_heads, lambda_init)
            gate_bias = _pad_cols(jnp.concatenate([ml_b_i[j], ml_b_f[j]]).reshape(1, -1), LANES)
            y_ml = _mlstm_call(z, gates, gate_bias, ml_conv_w[j], ml_conv_b[j].reshape(1, -1),
                               ml_heads, 3 * da_width)
            ys, wo_all, name = [y_da, y_ml], ab_w_out_b, "mix_ffn_even"
        else:
            z, gates = _inproj_call(
                x, sh_m, sc_m, g_mix, fx_w_in_b, j, fx_main,
                _pad_cols(fx_w_in[j][:, fx_main:], LANES).astype(BF16),
                _gain_row(fx_q_g[j], fx_k_g[j], fx_width, fx_main),
                2 * fx_width, "inproj_odd")
            fneg = _fox_gate_call(gates, _pad_cols(fx_b_f[j].reshape(1, -1), LANES))
            ys, wo_all, name = [_fox_call(z, fneg, fx_heads)], fx_w_out_b, "mix_ffn_odd"
        x = _mix_ffn_call(ys, wo_all, j, x, g_m, sh_f, sc_f, norm_ffn_g[l].reshape(1, d), g_f,
                          ffn_w1_b, ffn_w3_b, ffn_w2_b, l, name)
    return x
```
